```python
import math
import jax, jax.numpy as jnp
from jax import lax
import numpy as np

D_MODEL = 1024
BATCH = 4
SEQ = 4096
DEPTH = 1
DEC_BATCH = 128
DEC_SEQ = 1
PAST_LEN = 8192
PAGE_SIZE = 128

LRU_WIDTH = D_MODEL // 2
LRU_BLOCKS = 8
LRU_BLOCK = LRU_WIDTH // LRU_BLOCKS
CONV_WIDTH = 4
LRU_C = 8.0
N_HEADS = 8
N_KV_HEADS = 2
HEAD_DIM = 64
Q_PER_KV = N_HEADS // N_KV_HEADS
ATTN_WIDTH = N_HEADS * HEAD_DIM
KV_WIDTH = N_KV_HEADS * HEAD_DIM
WINDOW = 128
BLOCK = 128
MIX_WIDTH = LRU_WIDTH + ATTN_WIDTH
IN_WIDTH = 2 * LRU_WIDTH + ATTN_WIDTH + 2 * KV_WIDTH
SPLITS = (LRU_WIDTH, 2 * LRU_WIDTH, 2 * LRU_WIDTH + ATTN_WIDTH, 2 * LRU_WIDTH + ATTN_WIDTH + KV_WIDTH)
N_GROUPS = 4
EXPERTS_PER_GROUP = 8
N_EXPERTS = N_GROUPS * EXPERTS_PER_GROUP
TOP_K = 2
D_EXPERT = D_MODEL // 4
DEEPNORM_ALPHA = (2.0 * DEPTH) ** 0.25
DEEPNORM_BETA = (8.0 * DEPTH) ** -0.25
LN_EPS = 1e-5
RMS_EPS = 1e-6

kernel_name = 'hymba_rglru_swa_sink_hier_moe_deepnorm_step'


def layer_norm(x, g, b):
    xf = x.astype(jnp.float32)
    mu = jnp.mean(xf, -1, keepdims=True)
    var = jnp.mean(jnp.square(xf - mu), -1, keepdims=True)
    return ((xf - mu) * lax.rsqrt(var + LN_EPS) * g + b).astype(x.dtype)


def rms_norm(x, g):
    xf = x.astype(jnp.float32)
    return (xf * lax.rsqrt(jnp.mean(xf * xf, -1, keepdims=True) + RMS_EPS) * g).astype(x.dtype)


def ada_modulation(c, w_ada, b_ada):
    m = jax.nn.silu(c) @ w_ada + b_ada
    return [t[:, None, :] for t in jnp.split(m, 6, axis=-1)]


def causal_conv(u, ctx, w, b):
    T = u.shape[1]
    up = jnp.concatenate([ctx.astype(u.dtype), u], axis=1)
    out = b + sum(w[k] * up[:, k:k + T] for k in range(CONV_WIDTH))
    return out, up[:, -(CONV_WIDTH - 1):]


def block_diag(x, w, b):
    xb = x.reshape(x.shape[:-1] + (LRU_BLOCKS, LRU_BLOCK))
    return (jnp.einsum('btnd,nde->btne', xb, w) + b).reshape(x.shape)


def rg_lru(x, h0, w_a, b_a, w_x, b_x, lam):
    xf = x.astype(jnp.float32)
    r = jax.nn.sigmoid(block_diag(xf, w_a, b_a))
    i = jax.nn.sigmoid(block_diag(xf, w_x, b_x))
    log_a = -LRU_C * r * jax.nn.softplus(-lam.astype(jnp.float32))
    a = jnp.exp(log_a)
    bterm = jnp.sqrt(-jnp.expm1(2.0 * log_a)) * (i * xf)

    def combine(e1, e2):
        a1, b1 = e1
        a2, b2 = e2
        return a1 * a2, a2 * b1 + b2

    A, Bc = lax.associative_scan(combine, (a, bterm), axis=1)
    h = A * h0.astype(jnp.float32)[:, None, :] + Bc
    return h, h[:, -1]


def sink_softmax(s, mask, sinks):
    s = jnp.where(mask, s, -jnp.inf)
    sk = sinks.astype(jnp.float32).reshape(N_KV_HEADS, Q_PER_KV)[:, :, None, None]
    sk = jnp.broadcast_to(sk, s.shape[:-1] + (1,))
    p = jax.nn.softmax(jnp.concatenate([s, sk], axis=-1), axis=-1)
    return p[..., :-1]


def window_attention_prompt(q, k, v, sinks):
    Bn, T = q.shape[:2]
    nb = T // BLOCK
    qb = q.reshape(Bn, nb, BLOCK, N_KV_HEADS, Q_PER_KV, HEAD_DIM)

    def band(z):
        zb = z.reshape(Bn, nb, BLOCK, N_KV_HEADS, HEAD_DIM)
        prev = jnp.pad(zb, ((0, 0), (1, 0), (0, 0), (0, 0), (0, 0)))[:, :-1]
        return jnp.concatenate([prev, zb], axis=2)

    kb, vb = band(k), band(v)
    s = jnp.einsum('bnqkgd,bnskd->bnkgqs', qb, kb).astype(jnp.float32) * (HEAD_DIM ** -0.5)
    qi = jnp.arange(BLOCK)[:, None]
    sj = jnp.arange(2 * BLOCK)[None, :]
    rel = BLOCK + qi - sj
    local = (rel >= 0) & (rel <= WINDOW)
    no_prev = (jnp.arange(nb)[:, None, None] == 0) & (sj[None] < BLOCK)
    mask = (local[None] & ~no_prev)[None, :, None, None]
    p = sink_softmax(s, mask, sinks)
    o = jnp.einsum('bnkgqs,bnskd->bnqkgd', p.astype(vb.dtype), vb)
    return o.reshape(Bn, T, ATTN_WIDTH)


def window_attention_sample(q, k, v, k_buf, v_buf, sinks):
    Bn, S = q.shape[:2]
    kk = jnp.concatenate([k_buf.astype(k.dtype), k], axis=1)
    vv = jnp.concatenate([v_buf.astype(v.dtype), v], axis=1)
    qg = q.reshape(Bn, S, N_KV_HEADS, Q_PER_KV, HEAD_DIM)
    s = jnp.einsum('bqkgd,bskd->bkgqs', qg, kk).astype(jnp.float32) * (HEAD_DIM ** -0.5)
    j = jnp.arange(S)[:, None]
    m = jnp.arange(WINDOW + S)[None, :]
    mask = (m >= j) & (m <= WINDOW + j)
    p = sink_softmax(s, mask, sinks)
    o = jnp.einsum('bkgqs,bskd->bqkgd', p.astype(vv.dtype), vv)
    return o.reshape(Bn, S, ATTN_WIDTH), kk[:, -WINDOW:], vv[:, -WINDOW:]


def hier_moe(h, w_group, b_group, w_router, b_router, w_gate, w_up, w_down):
    hf = h.astype(jnp.float32)
    pg = jax.nn.softmax(hf @ w_group + b_group, axis=-1)
    g_val, g_idx = lax.top_k(pg, 1)
    logits_all = jnp.einsum('btd,gde->btge', hf, w_router) + b_router
    onehot_g = jax.nn.one_hot(g_idx[..., 0], N_GROUPS, dtype=jnp.float32)
    pe = jax.nn.softmax(jnp.einsum('btge,btg->bte', logits_all, onehot_g), axis=-1)
    e_val, e_idx = lax.top_k(pe, TOP_K)
    weight = g_val * e_val / jnp.sum(e_val, -1, keepdims=True)
    expert = g_idx * EXPERTS_PER_GROUP + e_idx
    combine = jnp.sum(jax.nn.one_hot(expert, N_EXPERTS, dtype=jnp.float32) * weight[..., None], axis=-2)
    a = jnp.einsum('btd,edf->btef', h, w_gate)
    u = jnp.einsum('btd,edf->btef', h, w_up)
    z = jax.nn.silu(a) * u * combine[..., None].astype(a.dtype)
    return jnp.einsum('btef,efd->btd', z, w_down).astype(h.dtype)


def decoder_layer(x, c, conv_ctx, h0, k_buf, v_buf, lw):
    (w_ada, b_ada, w_in, conv_w, conv_b, w_rg_a, b_rg_a, w_rg_x, b_rg_x, lru_lambda, sinks,
     g_lru, g_attn, w_out, ln1_g, ln1_b, w_group, b_group, w_router, b_router,
     w_gate, w_up, w_down, ln2_g, ln2_b) = lw
    Bn, T = x.shape[:2]
    sh1, sc1, gt1, sh2, sc2, gt2 = ada_modulation(c, w_ada, b_ada)
    h = x * (1.0 + sc1) + sh1
    zin = h @ w_in
    xb, gate, q, k, v = jnp.split(zin, SPLITS, axis=-1)
    xc, conv_state = causal_conv(xb, conv_ctx, conv_w, conv_b)
    hs, h_last = rg_lru(xc, h0, w_rg_a, b_rg_a, w_rg_x, b_rg_x, lru_lambda)
    y_lru = (hs * jax.nn.gelu(gate.astype(jnp.float32))).astype(x.dtype)
    q = q.reshape(Bn, T, N_HEADS, HEAD_DIM)
    k = k.reshape(Bn, T, N_KV_HEADS, HEAD_DIM)
    v = v.reshape(Bn, T, N_KV_HEADS, HEAD_DIM)
    if k_buf is None:
        y_att = window_attention_prompt(q, k, v, sinks)
        k_new, v_new = k[:, -WINDOW:], v[:, -WINDOW:]
    else:
        y_att, k_new, v_new = window_attention_sample(q, k, v, k_buf, v_buf, sinks)
    mix = jnp.concatenate([rms_norm(y_lru, g_lru), rms_norm(y_att, g_attn)], axis=-1) @ w_out
    x = layer_norm(DEEPNORM_ALPHA * x + (1.0 + gt1) * mix, ln1_g, ln1_b)
    h = x * (1.0 + sc2) + sh2
    f = hier_moe(h, w_group, b_group, w_router, b_router, w_gate, w_up, w_down)
    x = layer_norm(DEEPNORM_ALPHA * x + (1.0 + gt2) * f, ln2_g, ln2_b)
    return x, conv_state, h_last, k_new, v_new


def setup_inputs(seed: int = 0) -> dict:
    key = jax.random.key(seed)
    ks = jax.random.split(key, 40)
    f32 = jnp.float32
    L = DEPTH

    def nrm(k, shape, scale):
        return jax.random.normal(k, shape, f32) * scale

    w_in = nrm(ks[8], (L, D_MODEL, IN_WIDTH), D_MODEL ** -0.5)
    w_in = w_in.at[..., IN_WIDTH - KV_WIDTH:].multiply(DEEPNORM_BETA)
    a_c = jax.random.uniform(ks[15], (L, LRU_WIDTH), f32, 0.9, 0.999)
    s = a_c ** (1.0 / LRU_C)
    lru_lambda = jnp.log(s) - jnp.log1p(-s)
    return {
        'x_prompt': nrm(ks[0], (BATCH, SEQ, D_MODEL), 1.0),
        'x_sample': nrm(ks[1], (DEC_BATCH, DEC_SEQ, D_MODEL), 1.0),
        'c_prompt': nrm(ks[2], (BATCH, D_MODEL), 1.0),
        'c_sample': nrm(ks[3], (DEC_BATCH, D_MODEL), 1.0),
        'state_conv': nrm(ks[4], (L, DEC_BATCH, CONV_WIDTH - 1, LRU_WIDTH), 1.0),
        'state_h': nrm(ks[5], (L, DEC_BATCH, LRU_WIDTH), 0.5),
        'cache_k': nrm(ks[6], (L, DEC_BATCH, WINDOW, N_KV_HEADS, HEAD_DIM), 1.0),
        'cache_v': nrm(ks[7], (L, DEC_BATCH, WINDOW, N_KV_HEADS, HEAD_DIM), DEEPNORM_BETA),
        'w_ada': nrm(ks[9], (L, D_MODEL, 6 * D_MODEL), 0.1 * D_MODEL ** -0.5),
        'b_ada': nrm(ks[10], (L, 6 * D_MODEL), 0.01),
        'w_in': w_in,
        'conv_w': nrm(ks[11], (L, CONV_WIDTH, LRU_WIDTH), CONV_WIDTH ** -0.5),
        'conv_b': nrm(ks[12], (L, LRU_WIDTH), 0.01),
        'w_rg_a': nrm(ks[13], (L, LRU_BLOCKS, LRU_BLOCK, LRU_BLOCK), LRU_BLOCK ** -0.5),
        'b_rg_a': nrm(ks[14], (L, LRU_BLOCKS, LRU_BLOCK), 0.01),
        'w_rg_x': nrm(ks[16], (L, LRU_BLOCKS, LRU_BLOCK, LRU_BLOCK), LRU_BLOCK ** -0.5),
        'b_rg_x': nrm(ks[17], (L, LRU_BLOCKS, LRU_BLOCK), 0.01),
        'lru_lambda': lru_lambda,
        'sinks': nrm(ks[18], (L, N_HEADS), 1.0),
        'g_lru': 1.0 + nrm(ks[19], (L, LRU_WIDTH), 0.01),
        'g_attn': 1.0 + nrm(ks[20], (L, ATTN_WIDTH), 0.01),
        'w_out': nrm(ks[21], (L, MIX_WIDTH, D_MODEL), DEEPNORM_BETA * MIX_WIDTH ** -0.5),
        'ln1_g': 1.0 + nrm(ks[22], (L, D_MODEL), 0.01),
        'ln1_b': nrm(ks[23], (L, D_MODEL), 0.01),
        'w_group': nrm(ks[24], (L, D_MODEL, N_GROUPS), D_MODEL ** -0.5),
        'b_group': nrm(ks[25], (L, N_GROUPS), 0.01),
        'w_router': nrm(ks[26], (L, N_GROUPS, D_MODEL, EXPERTS_PER_GROUP), D_MODEL ** -0.5),
        'b_router': nrm(ks[27], (L, N_GROUPS, EXPERTS_PER_GROUP), 0.01),
        'w_gate': nrm(ks[28], (L, N_EXPERTS, D_MODEL, D_EXPERT), D_MODEL ** -0.5),
        'w_up': nrm(ks[29], (L, N_EXPERTS, D_MODEL, D_EXPERT), D_MODEL ** -0.5),
        'w_down': nrm(ks[30], (L, N_EXPERTS, D_EXPERT, D_MODEL), DEEPNORM_BETA * D_EXPERT ** -0.5),
        'ln2_g': 1.0 + nrm(ks[31], (L, D_MODEL), 0.01),
        'ln2_b': nrm(ks[32], (L, D_MODEL), 0.01),
    }


def reference(x_prompt, x_sample, c_prompt, c_sample, state_conv, state_h, cache_k, cache_v,
              w_ada, b_ada, w_in, conv_w, conv_b, w_rg_a, b_rg_a, w_rg_x, b_rg_x, lru_lambda, sinks,
              g_lru, g_attn, w_out, ln1_g, ln1_b, w_group, b_group, w_router, b_router,
              w_gate, w_up, w_down, ln2_g, ln2_b):
    xp, xs = x_prompt, x_sample
    conv_p, h_p, k_p, v_p = [], [], [], []
    conv_s, h_s, k_s, v_s = [], [], [], []
    for l in range(DEPTH):
        lw = (w_ada[l], b_ada[l], w_in[l], conv_w[l], conv_b[l], w_rg_a[l], b_rg_a[l], w_rg_x[l],
              b_rg_x[l], lru_lambda[l], sinks[l], g_lru[l], g_attn[l], w_out[l], ln1_g[l], ln1_b[l],
              w_group[l], b_group[l], w_router[l], b_router[l], w_gate[l], w_up[l], w_down[l],
              ln2_g[l], ln2_b[l])
        zero_ctx = jnp.zeros((xp.shape[0], CONV_WIDTH - 1, LRU_WIDTH), xp.dtype)
        zero_h = jnp.zeros((xp.shape[0], LRU_WIDTH), jnp.float32)
        xp, cp, hp, kp, vp = decoder_layer(xp, c_prompt, zero_ctx, zero_h, None, None, lw)
        xs, cs, hs, kss, vss = decoder_layer(xs, c_sample, state_conv[l], state_h[l], cache_k[l], cache_v[l], lw)
        conv_p.append(cp); h_p.append(hp); k_p.append(kp); v_p.append(vp)
        conv_s.append(cs); h_s.append(hs); k_s.append(kss); v_s.append(vss)
    return (xp, xs,
            jnp.stack(conv_p), jnp.stack(h_p), jnp.stack(k_p), jnp.stack(v_p),
            jnp.stack(conv_s), jnp.stack(h_s), jnp.stack(k_s), jnp.stack(v_s))
```

```python
import functools

import jax
import jax.numpy as jnp
import numpy as np
from jax import lax
from jax.experimental import pallas as pl
from jax.experimental.pallas import tpu as pltpu

F32 = jnp.float32
BF16 = jnp.bfloat16
HIGHEST = lax.Precision.HIGHEST

D_MODEL = 1024
BATCH = 4
SEQ = 4096
DEC_BATCH = 128
LRU_WIDTH = 512
LRU_BLOCKS = 8
LRU_BLOCK = 64
CONV_WIDTH = 4
LRU_C = 8.0
N_HEADS = 8
N_KV_HEADS = 2
HEAD_DIM = 64
ATTN_WIDTH = 512
KV_WIDTH = 128
WINDOW = 128
IN_WIDTH = 2 * LRU_WIDTH + ATTN_WIDTH + 2 * KV_WIDTH
N_GROUPS = 4
EXPERTS_PER_GROUP = 8
N_EXPERTS = 32
D_EXPERT = 256
DEEPNORM_ALPHA = 2.0 ** 0.25
LN_EPS = 1e-5
RMS_EPS = 1e-6
ATTN_SCALE = HEAD_DIM ** -0.5

LANES = 128
ROUTE_LANES = 128
VMEM_LIMIT = 56 * 1024 * 1024

HEAD_PERM = np.concatenate(
    [np.concatenate([np.arange(64 * c, 64 * c + 64), np.arange(64 * (c + 4), 64 * (c + 4) + 64)])
     for c in range(4)])


def _cparams(sem):
    return pltpu.CompilerParams(dimension_semantics=sem, vmem_limit_bytes=VMEM_LIMIT)


def _dot(a, b, exact):
    if exact:
        return jnp.dot(a, b, precision=HIGHEST, preferred_element_type=F32)
    return jnp.dot(a.astype(BF16), b.astype(BF16), preferred_element_type=F32)


def _dot_nt(a, b, exact):
    dn = (((1,), (1,)), ((), ()))
    if exact:
        return lax.dot_general(a, b, dn, precision=HIGHEST, preferred_element_type=F32)
    return lax.dot_general(a.astype(BF16), b.astype(BF16), dn, preferred_element_type=F32)


def _sigmoid(x):
    return 1.0 / (1.0 + jnp.exp(-x))


def _silu(x):
    return x * _sigmoid(x)


def _gelu_tanh(x):
    return 0.5 * x * (1.0 + jnp.tanh(np.sqrt(2.0 / np.pi).astype(np.float32) * (x + 0.044715 * (x * x * x))))


def _softplus(x):
    return jnp.maximum(x, 0.0) + jnp.log1p(jnp.exp(-jnp.abs(x)))


def _layer_norm(x, g, b):
    mu = jnp.mean(x, axis=-1, keepdims=True)
    xc = x - mu
    var = jnp.mean(xc * xc, axis=-1, keepdims=True)
    return xc * lax.rsqrt(var + LN_EPS) * g + b


def _rms_norm(x, g):
    return x * lax.rsqrt(jnp.mean(x * x, axis=-1, keepdims=True) + RMS_EPS) * g


def _ada_kernel(c_ref, w_ref, b_ref, o_ref):
    o_ref[...] = _dot(_silu(c_ref[...]), w_ref[...], True) + b_ref[...]


def _ada(c_all, w_ada, b_ada):
    rows = c_all.shape[0]
    bn = 512
    return pl.pallas_call(
        _ada_kernel,
        grid=(6 * D_MODEL // bn,),
        in_specs=[pl.BlockSpec((rows, D_MODEL), lambda j: (0, 0)),
                  pl.BlockSpec((D_MODEL, bn), lambda j: (0, j)),
                  pl.BlockSpec((1, bn), lambda j: (0, j))],
        out_specs=pl.BlockSpec((rows, bn), lambda j: (0, j)),
        out_shape=jax.ShapeDtypeStruct((rows, 6 * D_MODEL), F32),
        compiler_params=_cparams(("arbitrary",)),
        name="ada_modulation",
    )(c_all, w_ada, b_ada.reshape(1, -1))


def _inproj_kernel(x_ref, mod_ref, w_ref, o_ref):
    sh1 = mod_ref[0, 0:1, :]
    sc1 = mod_ref[0, 1:2, :]
    h = x_ref[0] * (1.0 + sc1) + sh1
    o_ref[0] = _dot(h, w_ref[...], False)


def _inproj(x, modp, w_in_bf16, tm=512):
    b, t, d = x.shape
    return pl.pallas_call(
        _inproj_kernel,
        grid=(b, t // tm),
        in_specs=[pl.BlockSpec((1, tm, d), lambda i, j: (i, j, 0)),
                  pl.BlockSpec((1, 6, d), lambda i, j: (i, 0, 0)),
                  pl.BlockSpec((d, IN_WIDTH), lambda i, j: (0, 0))],
        out_specs=pl.BlockSpec((1, tm, IN_WIDTH), lambda i, j: (i, j, 0)),
        out_shape=jax.ShapeDtypeStruct((b, t, IN_WIDTH), F32),
        compiler_params=_cparams(("arbitrary", "arbitrary")),
        name="prompt_inproj",
    )(x, modp, w_in_bf16)


def _lru_gates(xc, wlo, whi, bgate, sp_neg_lam, exact):
    g_lo = _dot(xc[:, :256], wlo, exact)
    g_hi = _dot(xc[:, 256:], whi, exact)
    ga = jnp.concatenate([g_lo[:, :256], g_hi[:, :256]], axis=-1) + bgate[:, :LRU_WIDTH]
    gx = jnp.concatenate([g_lo[:, 256:], g_hi[:, 256:]], axis=-1) + bgate[:, LRU_WIDTH:]
    r = _sigmoid(ga)
    i = _sigmoid(gx)
    log_a = -LRU_C * r * sp_neg_lam
    a = jnp.exp(log_a)
    one_minus_a2 = -jnp.tanh(log_a) * (a * a + 1.0)
    bterm = jnp.sqrt(one_minus_a2) * (i * xc)
    return a, bterm


def _lru_kernel(z_ref, convw_ref, convb_ref, wlo_ref, whi_ref, bgate_ref, lam_ref,
                y_ref, cstate_ref, hlast_ref, tail_ref, carry_ref, *, tl):
    j = pl.program_id(1)

    @pl.when(j == 0)
    def _():
        tail_ref[...] = jnp.zeros_like(tail_ref)
        carry_ref[...] = jnp.zeros_like(carry_ref)

    xb = z_ref[0, :, :LRU_WIDTH]
    gate = z_ref[0, :, LRU_WIDTH:]
    rows = lax.broadcasted_iota(jnp.int32, (tl, LRU_WIDTH), 0)

    xc = convb_ref[...] + convw_ref[3:4, :] * xb
    rows8 = lax.broadcasted_iota(jnp.int32, (8, LRU_WIDTH), 0)
    tail = tail_ref[...]
    for back in (1, 2, 3):
        rolled = pltpu.roll(xb, back, axis=0)
        top = jnp.where(rows8 >= back, rolled[:8], pltpu.roll(tail, back, axis=0))
        shifted = jnp.concatenate([top, rolled[8:]], axis=0)
        xc = xc + convw_ref[3 - back:4 - back, :] * shifted
    tail_ref[...] = xb[tl - 8:, :]
    cstate_ref[0] = xb[tl - 8:, :]

    sp = _softplus(-lam_ref[...])
    a, bterm = _lru_gates(xc, wlo_ref[...], whi_ref[...], bgate_ref[...], sp, False)

    s = 1
    while s < tl:
        a_sh = jnp.where(rows >= s, pltpu.roll(a, s, axis=0), 1.0)
        b_sh = jnp.where(rows >= s, pltpu.roll(bterm, s, axis=0), 0.0)
        bterm = a * b_sh + bterm
        a = a * a_sh
        s *= 2
    h = a * carry_ref[7:8, :] + bterm
    carry_ref[...] = h[tl - 8:, :]
    hlast_ref[0] = h[tl - 8:, :]
    y_ref[0] = h * _gelu_tanh(gate)


def _lru(zin, conv_w, conv_b, wlo, whi, bgate, lam, tl=512):
    b, t, _ = zin.shape
    kern = functools.partial(_lru_kernel, tl=tl)
    full = lambda shp: pl.BlockSpec(shp, lambda i, j: tuple(0 for _ in shp))
    return pl.pallas_call(
        kern,
        grid=(b, t // tl),
        in_specs=[pl.BlockSpec((1, tl, 2 * LRU_WIDTH), lambda i, j: (i, j, 0)),
                  full((CONV_WIDTH, LRU_WIDTH)), full((1, LRU_WIDTH)),
                  full((256, 512)), full((256, 512)), full((1, 2 * LRU_WIDTH)), full((1, LRU_WIDTH))],
        out_specs=[pl.BlockSpec((1, tl, LRU_WIDTH), lambda i, j: (i, j, 0)),
                   pl.BlockSpec((1, 8, LRU_WIDTH), lambda i, j: (i, 0, 0)),
                   pl.BlockSpec((1, 8, LRU_WIDTH), lambda i, j: (i, 0, 0))],
        out_shape=[jax.ShapeDtypeStruct((b, t, LRU_WIDTH), F32),
                   jax.ShapeDtypeStruct((b, 8, LRU_WIDTH), F32),
                   jax.ShapeDtypeStruct((b, 8, LRU_WIDTH), F32)],
        scratch_shapes=[pltpu.VMEM((8, LRU_WIDTH), F32), pltpu.VMEM((8, LRU_WIDTH), F32)],
        compiler_params=_cparams(("arbitrary", "arbitrary")),
        name="prompt_rglru",
    )(zin, conv_w, conv_b, wlo, whi, bgate, lam)


def _attn_kernel(q_ref, k_ref, v_ref, sink_ref, o_ref, kprev_ref, vprev_ref):
    j = pl.program_id(1)

    @pl.when(j == 0)
    def _():
        kprev_ref[...] = jnp.zeros_like(kprev_ref)
        vprev_ref[...] = jnp.zeros_like(vprev_ref)

    blk = WINDOW
    lane = lax.broadcasted_iota(jnp.int32, (blk, LANES), 1)
    low = lane < HEAD_DIM
    q = q_ref[0]
    pieces = []
    for half in (0, 1):
        for c in range(4):
            qc = q[:, LANES * c:LANES * (c + 1)]
            pieces.append(jnp.where(low if half == 0 else ~low, qc, 0.0).astype(BF16))
    q8 = jnp.concatenate(pieces, axis=0)
    k_cur = k_ref[0]
    v_cur = v_ref[0]
    k_band = jnp.concatenate([kprev_ref[...], k_cur], axis=0)
    v_band = jnp.concatenate([vprev_ref[...], v_cur], axis=0)
    s = _dot_nt(q8, k_band, False) * ATTN_SCALE
    s = s.reshape(N_HEADS, blk, 2 * blk)
    qi = lax.broadcasted_iota(jnp.int32, (blk, 2 * blk), 0)
    sj = lax.broadcasted_iota(jnp.int32, (blk, 2 * blk), 1)
    rel = blk + qi - sj
    valid = (rel >= 0) & (rel <= WINDOW) & ((sj >= blk) | (j > 0))
    s = jnp.where(valid[None], s, -jnp.inf)
    sink = sink_ref[...].reshape(N_HEADS, blk, 1)
    m = jnp.maximum(jnp.max(s, axis=-1, keepdims=True), sink)
    e = jnp.exp(s - m)
    den = jnp.sum(e, axis=-1, keepdims=True) + jnp.exp(sink - m)
    p = (e * (1.0 / den)).reshape(N_HEADS * blk, 2 * blk)
    o8 = _dot(p, v_band, False)
    cols = []
    for c in range(4):
        cols.append(jnp.where(low, o8[blk * c:blk * (c + 1)], o8[blk * (c + 4):blk * (c + 5)]))
    o_ref[0] = jnp.concatenate(cols, axis=-1)
    kprev_ref[...] = k_cur
    vprev_ref[...] = v_cur


def _attn(zin, sinks):
    b, t, _ = zin.shape
    blk = WINDOW
    sink_col = jnp.repeat(sinks.astype(F32), blk).reshape(N_HEADS * blk, 1)
    return pl.pallas_call(
        _attn_kernel,
        grid=(b, t // blk),
        in_specs=[pl.BlockSpec((1, blk, ATTN_WIDTH), lambda i, j: (i, j, 2)),
                  pl.BlockSpec((1, blk, KV_WIDTH), lambda i, j: (i, j, 12)),
                  pl.BlockSpec((1, blk, KV_WIDTH), lambda i, j: (i, j, 13)),
                  pl.BlockSpec((N_HEADS * blk, 1), lambda i, j: (0, 0))],
        out_specs=pl.BlockSpec((1, blk, ATTN_WIDTH), lambda i, j: (i, j, 0)),
        out_shape=jax.ShapeDtypeStruct((b, t, ATTN_WIDTH), F32),
        scratch_shapes=[pltpu.VMEM((blk, KV_WIDTH), F32), pltpu.VMEM((blk, KV_WIDTH), F32)],
        compiler_params=_cparams(("arbitrary", "arbitrary")),
        name="prompt_window_attention",
    )(zin, zin, zin, sink_col)


def _route(h2, wr, br, exact_unused=True):
    t = h2.shape[0]
    logits = _dot(h2, wr, True) + br
    lane = lax.broadcasted_iota(jnp.int32, (t, ROUTE_LANES), 1).astype(F32)
    neg = -jnp.inf
    big = float(ROUTE_LANES)
    is_g = (lane >= N_EXPERTS) & (lane < N_EXPERTS + N_GROUPS)
    lg = jnp.where(is_g, logits, neg)
    mg = jnp.max(lg, axis=-1, keepdims=True)
    eg = jnp.where(is_g, jnp.exp(lg - mg), 0.0)
    pg = eg / jnp.sum(eg, axis=-1, keepdims=True)
    g_val = jnp.max(pg, axis=-1, keepdims=True)
    g_lane = jnp.min(jnp.where((pg == g_val) & is_g, lane, big), axis=-1, keepdims=True)
    g_idx = g_lane - N_EXPERTS
    in_grp = (lane >= g_idx * EXPERTS_PER_GROUP) & (lane < (g_idx + 1.0) * EXPERTS_PER_GROUP)
    le = jnp.where(in_grp, logits, neg)
    me = jnp.max(le, axis=-1, keepdims=True)
    ee = jnp.where(in_grp, jnp.exp(le - me), 0.0)
    pe = ee / jnp.sum(ee, axis=-1, keepdims=True)
    v1 = jnp.max(pe, axis=-1, keepdims=True)
    l1 = jnp.min(jnp.where((pe == v1) & in_grp, lane, big), axis=-1, keepdims=True)
    rest = in_grp & (lane != l1)
    pe2 = jnp.where(rest, pe, -1.0)
    v2 = jnp.max(pe2, axis=-1, keepdims=True)
    l2 = jnp.min(jnp.where((pe2 == v2) & rest, lane, big), axis=-1, keepdims=True)
    tot = v1 + v2
    w1 = g_val * v1 / tot
    w2 = g_val * v2 / tot
    return jnp.where(lane == l1, w1, 0.0) + jnp.where(lane == l2, w2, 0.0)


def _outproj_body(x, ylru, yatt, sh2, sc2, gt1, glru, gattn, wout, ln1g, ln1b, wr, br, exact):
    mixin = jnp.concatenate([_rms_norm(ylru, glru), _rms_norm(yatt, gattn)], axis=-1)
    mix = _dot(mixin, wout, exact)
    x1 = _layer_norm(DEEPNORM_ALPHA * x + (1.0 + gt1) * mix, ln1g, ln1b)
    h2 = x1 * (1.0 + sc2) + sh2
    return x1, _route(h2, wr, br)


def _outproj_prompt_kernel(x_ref, ylru_ref, yatt_ref, mod_ref, glru_ref, gattn_ref, wout_ref,
                           ln1g_ref, ln1b_ref, wr_ref, br_ref, x1_ref, comb_ref):
    gt1 = mod_ref[0, 2:3, :]
    sh2 = mod_ref[0, 3:4, :]
    sc2 = mod_ref[0, 4:5, :]
    x1, comb = _outproj_body(x_ref[...], ylru_ref[...], yatt_ref[...], sh2, sc2, gt1,
                             glru_ref[...], gattn_ref[...], wout_ref[...], ln1g_ref[...], ln1b_ref[...],
                             wr_ref[...], br_ref[...], False)
    x1_ref[...] = x1
    comb_ref[...] = comb


def _outproj_prompt(x2d, ylru2d, yatt2d, modp, glru, gattn, wout_bf16, ln1g, ln1b, wr, br, tm=512):
    n, d = x2d.shape
    per_batch = SEQ // tm
    full = lambda shp: pl.BlockSpec(shp, lambda i: tuple(0 for _ in shp))
    return pl.pallas_call(
        _outproj_prompt_kernel,
        grid=(n // tm,),
        in_specs=[pl.BlockSpec((tm, d), lambda i: (i, 0)),
                  pl.BlockSpec((tm, LRU_WIDTH), lambda i: (i, 0)),
                  pl.BlockSpec((tm, ATTN_WIDTH), lambda i: (i, 0)),
                  pl.BlockSpec((1, 6, d), lambda i: (i // per_batch, 0, 0)),
                  full((1, LRU_WIDTH)), full((1, ATTN_WIDTH)), full((d, d)),
                  full((1, d)), full((1, d)), full((d, ROUTE_LANES)), full((1, ROUTE_LANES))],
        out_specs=[pl.BlockSpec((tm, d), lambda i: (i, 0)),
                   pl.BlockSpec((tm, ROUTE_LANES), lambda i: (i, 0))],
        out_shape=[jax.ShapeDtypeStruct((n, d), F32), jax.ShapeDtypeStruct((n, ROUTE_LANES), F32)],
        compiler_params=_cparams(("arbitrary",)),
        name="prompt_outproj_ln_route",
    )(x2d, ylru2d, yatt2d, modp, glru, gattn, wout_bf16, ln1g, ln1b, wr, br)


def _moe_kernel(x1_ref, comb_ref, sh2_ref, sc2_ref, gt2_ref, wg_ref, wu_ref, wd_ref, ln2g_ref, ln2b_ref,
                o_ref, h2_ref, acc_ref):
    e = pl.program_id(1)

    @pl.when(e == 0)
    def _():
        h2_ref[...] = (x1_ref[...] * (1.0 + sc2_ref[...]) + sh2_ref[...]).astype(BF16)
        acc_ref[...] = jnp.zeros_like(acc_ref)

    h2 = h2_ref[...]
    a = jnp.dot(h2, wg_ref[0].astype(BF16), preferred_element_type=F32)
    u = jnp.dot(h2, wu_ref[0].astype(BF16), preferred_element_type=F32)
    comb = comb_ref[...]
    lane = lax.broadcasted_iota(jnp.int32, comb.shape, 1)
    c_e = jnp.sum(jnp.where(lane == e, comb, 0.0), axis=-1, keepdims=True)
    z = _silu(a) * u * c_e
    acc_ref[...] += jnp.dot(z.astype(BF16), wd_ref[0].astype(BF16), preferred_element_type=F32)

    @pl.when(e == N_EXPERTS - 1)
    def _():
        o_ref[...] = _layer_norm(DEEPNORM_ALPHA * x1_ref[...] + (1.0 + gt2_ref[...]) * acc_ref[...],
                                 ln2g_ref[...], ln2b_ref[...])


def _moe_dense(x1, comb, sh2, sc2, gt2, mod_rows_per_tile, w_gate, w_up, w_down, ln2g, ln2b, tm):
    n, d = x1.shape
    if mod_rows_per_tile:
        mspec = pl.BlockSpec((1, 1, d), lambda i, e: (i // mod_rows_per_tile, 0, 0))
        sh2, sc2, gt2 = (m.reshape(-1, 1, d) for m in (sh2, sc2, gt2))
        kern = lambda x1r, cr, s1, s2, s3, *rest: _moe_kernel(x1r, cr, s1.at[0], s2.at[0], s3.at[0], *rest)
    else:
        mspec = pl.BlockSpec((tm, d), lambda i, e: (i, 0))
        kern = _moe_kernel
    full = lambda shp: pl.BlockSpec(shp, lambda i, e: tuple(0 for _ in shp))
    return pl.pallas_call(
        kern,
        grid=(n // tm, N_EXPERTS),
        in_specs=[pl.BlockSpec((tm, d), lambda i, e: (i, 0)),
                  pl.BlockSpec((tm, ROUTE_LANES), lambda i, e: (i, 0)),
                  mspec, mspec, mspec,
                  pl.BlockSpec((1, d, D_EXPERT), lambda i, e: (e, 0, 0)),
                  pl.BlockSpec((1, d, D_EXPERT), lambda i, e: (e, 0, 0)),
                  pl.BlockSpec((1, D_EXPERT, d), lambda i, e: (e, 0, 0)),
                  full((1, d)), full((1, d))],
        out_specs=pl.BlockSpec((tm, d), lambda i, e: (i, 0)),
        out_shape=jax.ShapeDtypeStruct((n, d), F32),
        scratch_shapes=[pltpu.VMEM((tm, d), BF16), pltpu.VMEM((tm, d), F32)],
        compiler_params=_cparams(("arbitrary", "arbitrary")),
        name="moe_dense_ln",
    )(x1, comb, sh2, sc2, gt2, w_gate, w_up, w_down, ln2g, ln2b)


def _sample_in_kernel(x_ref, sh1_ref, sc1_ref, win_ref, ctx_ref, h0_ref, convw_ref, convb_ref,
                      wlo_ref, whi_ref, bgate_ref, lam_ref,
                      ylru_ref, q_ref, k_ref, v_ref, cstate_ref, hnew_ref):
    h = x_ref[...] * (1.0 + sc1_ref[...]) + sh1_ref[...]
    z = _dot(h, win_ref[...], True)
    xb = z[:, :LRU_WIDTH]
    gate = z[:, LRU_WIDTH:2 * LRU_WIDTH]
    c0 = ctx_ref[:, 0, :]
    c1 = ctx_ref[:, 1, :]
    c2 = ctx_ref[:, 2, :]
    xc = (convb_ref[...] + convw_ref[0:1, :] * c0 + convw_ref[1:2, :] * c1
          + convw_ref[2:3, :] * c2 + convw_ref[3:4, :] * xb)
    cstate_ref[:, 0, :] = c1
    cstate_ref[:, 1, :] = c2
    cstate_ref[:, 2, :] = xb
    sp = _softplus(-lam_ref[...])
    a, bterm = _lru_gates(xc, wlo_ref[...], whi_ref[...], bgate_ref[...], sp, True)
    hn = a * h0_ref[...] + bterm
    hnew_ref[...] = hn
    ylru_ref[...] = hn * _gelu_tanh(gate)
    low = lax.broadcasted_iota(jnp.int32, (DEC_BATCH, LANES), 1) < HEAD_DIM
    for c in range(4):
        qc = z[:, 2 * LRU_WIDTH + LANES * c:2 * LRU_WIDTH + LANES * (c + 1)]
        q_ref[pl.ds(c, DEC_BATCH, stride=N_HEADS), :] = jnp.where(low, qc, 0.0)
        q_ref[pl.ds(c + 4, DEC_BATCH, stride=N_HEADS), :] = jnp.where(low, 0.0, qc)
    k_ref[...] = z[:, 2 * LRU_WIDTH + ATTN_WIDTH:2 * LRU_WIDTH + ATTN_WIDTH + KV_WIDTH]
    v_ref[...] = z[:, 2 * LRU_WIDTH + ATTN_WIDTH + KV_WIDTH:]


def _sample_in(x, sh1, sc1, w_in_p, ctx, h0, conv_w, conv_b, wlo, whi, bgate, lam):
    n = DEC_BATCH
    outs = [jax.ShapeDtypeStruct((n, LRU_WIDTH), F32),
            jax.ShapeDtypeStruct((n * N_HEADS, LANES), F32),
            jax.ShapeDtypeStruct((n, KV_WIDTH), F32),
            jax.ShapeDtypeStruct((n, KV_WIDTH), F32),
            jax.ShapeDtypeStruct((n, CONV_WIDTH - 1, LRU_WIDTH), F32),
            jax.ShapeDtypeStruct((n, LRU_WIDTH), F32)]
    return pl.pallas_call(
        _sample_in_kernel,
        out_shape=outs,
        compiler_params=pltpu.CompilerParams(vmem_limit_bytes=VMEM_LIMIT),
        name="sample_inproj_rglru",
    )(x, sh1, sc1, w_in_p, ctx, h0, conv_w, conv_b, wlo, whi, bgate, lam)


def _sample_attn_kernel(q_ref, kn_ref, vn_ref, ck_ref, cv_ref, sink_ref, y_ref, nk_ref, nv_ref, *, bb):
    rows = lax.broadcasted_iota(jnp.int32, (WINDOW, KV_WIDTH), 0)
    sink = sink_ref[...]
    for b in range(bb):
        q8 = q_ref[b]
        kb = ck_ref[b]
        vb = cv_ref[b]
        kn = kn_ref[b:b + 1, :]
        vn = vn_ref[b:b + 1, :]
        s = _dot_nt(q8, kb, True) * ATTN_SCALE
        s_self = jnp.sum(q8 * kn, axis=-1, keepdims=True) * ATTN_SCALE
        m = jnp.maximum(jnp.maximum(jnp.max(s, axis=-1, keepdims=True), s_self), sink)
        e = jnp.exp(s - m)
        e_self = jnp.exp(s_self - m)
        den = jnp.sum(e, axis=-1, keepdims=True) + e_self + jnp.exp(sink - m)
        inv = 1.0 / den
        y_ref[b] = _dot(e * inv, vb, True) + (e_self * inv) * vn
        nk_ref[b] = jnp.where(rows == WINDOW - 1, kn, pltpu.roll(kb, WINDOW - 1, axis=0))
        nv_ref[b] = jnp.where(rows == WINDOW - 1, vn, pltpu.roll(vb, WINDOW - 1, axis=0))


def _sample_attn(q3, kn, vn, cache_k, cache_v, sinks, bb=8):
    n = DEC_BATCH
    kern = functools.partial(_sample_attn_kernel, bb=bb)
    return pl.pallas_call(
        kern,
        grid=(n // bb,),
        in_specs=[pl.BlockSpec((bb, N_HEADS, LANES), lambda i: (i, 0, 0)),
                  pl.BlockSpec((bb, KV_WIDTH), lambda i: (i, 0)),
                  pl.BlockSpec((bb, KV_WIDTH), lambda i: (i, 0)),
                  pl.BlockSpec((bb, WINDOW, KV_WIDTH), lambda i: (i, 0, 0)),
                  pl.BlockSpec((bb, WINDOW, KV_WIDTH), lambda i: (i, 0, 0)),
                  pl.BlockSpec((N_HEADS, 1), lambda i: (0, 0))],
        out_specs=[pl.BlockSpec((bb, N_HEADS, LANES), lambda i: (i, 0, 0)),
                   pl.BlockSpec((bb, WINDOW, KV_WIDTH), lambda i: (i, 0, 0)),
                   pl.BlockSpec((bb, WINDOW, KV_WIDTH), lambda i: (i, 0, 0))],
        out_shape=[jax.ShapeDtypeStruct((n, N_HEADS, LANES), F32),
                   jax.ShapeDtypeStruct((n, WINDOW, KV_WIDTH), F32),
                   jax.ShapeDtypeStruct((n, WINDOW, KV_WIDTH), F32)],
        compiler_params=_cparams(("arbitrary",)),
        name="sample_cache_attention",
    )(q3, kn, vn, cache_k, cache_v, sinks.reshape(N_HEADS, 1))


def _sample_out_kernel(x_ref, ylru_ref, yatt_ref, sh2_ref, sc2_ref, gt1_ref, glru_ref, gattn_ref, wout_ref,
                       ln1g_ref, ln1b_ref, wr_ref, br_ref, x1_ref, comb_ref):
    low = lax.broadcasted_iota(jnp.int32, (DEC_BATCH, LANES), 1) < HEAD_DIM
    yatt = jnp.concatenate(
        [jnp.where(low, yatt_ref[pl.ds(c, DEC_BATCH, stride=N_HEADS), :],
                   yatt_ref[pl.ds(c + 4, DEC_BATCH, stride=N_HEADS), :]) for c in range(4)], axis=-1)
    x1, comb = _outproj_body(x_ref[...], ylru_ref[...], yatt, sh2_ref[...], sc2_ref[...], gt1_ref[...],
                             glru_ref[...], gattn_ref[...], wout_ref[...], ln1g_ref[...], ln1b_ref[...],
                             wr_ref[...], br_ref[...], True)
    x1_ref[...] = x1
    comb_ref[...] = comb


def _sample_out(x, ylru, yatt2d, sh2, sc2, gt1, glru, gattn, wout_p, ln1g, ln1b, wr, br):
    n = DEC_BATCH
    return pl.pallas_call(
        _sample_out_kernel,
        out_shape=[jax.ShapeDtypeStruct((n, D_MODEL), F32), jax.ShapeDtypeStruct((n, ROUTE_LANES), F32)],
        compiler_params=pltpu.CompilerParams(vmem_limit_bytes=VMEM_LIMIT),
        name="sample_outproj_ln_route",
    )(x, ylru, yatt2d, sh2, sc2, gt1, glru, gattn, wout_p, ln1g, ln1b, wr, br)


def _block_diag_halves(w_a, w_x):
    def bd(w4):
        eye = jnp.eye(4, dtype=w4.dtype)
        return (w4[:, :, None, :] * eye[:, None, :, None]).reshape(256, 256)
    lo = jnp.concatenate([bd(w_a[:4]), bd(w_x[:4])], axis=1)
    hi = jnp.concatenate([bd(w_a[4:]), bd(w_x[4:])], axis=1)
    return lo, hi


def kernel(x_prompt, x_sample, c_prompt, c_sample, state_conv, state_h, cache_k, cache_v, w_ada, b_ada, w_in,
           conv_w, conv_b, w_rg_a, b_rg_a, w_rg_x, b_rg_x, lru_lambda, sinks, g_lru, g_attn, w_out, ln1_g, ln1_b,
           w_group, b_group, w_router, b_router, w_gate, w_up, w_down, ln2_g, ln2_b):
    d = D_MODEL
    perm = jnp.asarray(HEAD_PERM)
    w_in0 = w_in[0]
    q0 = 2 * LRU_WIDTH
    w_in_p = jnp.concatenate([w_in0[:, :q0], w_in0[:, q0:q0 + ATTN_WIDTH][:, perm], w_in0[:, q0 + ATTN_WIDTH:]],
                             axis=1)
    w_out0 = w_out[0]
    w_out_p = jnp.concatenate([w_out0[:LRU_WIDTH], w_out0[LRU_WIDTH:][perm]], axis=0)
    g_attn_p = g_attn[0][perm].reshape(1, -1)
    glru = g_lru[0].reshape(1, -1)
    wlo, whi = _block_diag_halves(w_rg_a[0], w_rg_x[0])
    bgate = jnp.concatenate([b_rg_a[0].reshape(-1), b_rg_x[0].reshape(-1)]).reshape(1, -1)
    lam = lru_lambda[0].reshape(1, -1)
    convw = conv_w[0]
    convb = conv_b[0].reshape(1, -1)
    ln1g, ln1b = ln1_g[0].reshape(1, -1), ln1_b[0].reshape(1, -1)
    ln2g, ln2b = ln2_g[0].reshape(1, -1), ln2_b[0].reshape(1, -1)
    wr = jnp.concatenate([jnp.transpose(w_router[0], (1, 0, 2)).reshape(d, N_EXPERTS), w_group[0],
                          jnp.zeros((d, ROUTE_LANES - N_EXPERTS - N_GROUPS), F32)], axis=1)
    br = jnp.concatenate([b_router[0].reshape(-1), b_group[0],
                          jnp.zeros((ROUTE_LANES - N_EXPERTS - N_GROUPS,), F32)]).reshape(1, -1)
    sink_p = sinks[0]

    c_all = jnp.concatenate([c_prompt, jnp.zeros((8 - BATCH, d), F32), c_sample], axis=0)
    mod = _ada(c_all, w_ada[0], b_ada[0])
    modp = mod[:BATCH].reshape(BATCH, 6, d)
    mods = mod[8:]
    sh1_s, sc1_s, gt1_s, sh2_s, sc2_s, gt2_s = (mods[:, k * d:(k + 1) * d] for k in range(6))

    zin = _inproj(x_prompt, modp, w_in_p.astype(BF16))
    ylru, cstate8, hlast8 = _lru(zin, convw, convb, wlo.astype(BF16), whi.astype(BF16), bgate, lam)
    yatt = _attn(zin, sink_p)
    n_p = BATCH * SEQ
    x1_p, comb_p = _outproj_prompt(x_prompt.reshape(n_p, d), ylru.reshape(n_p, LRU_WIDTH),
                                   yatt.reshape(n_p, ATTN_WIDTH), modp, glru, g_attn_p, w_out_p.astype(BF16),
                                   ln1g, ln1b, wr, br)
    tm = 1024
    y_p = _moe_dense(x1_p, comb_p, modp[:, 3], modp[:, 4], modp[:, 5], SEQ // tm,
                     w_gate[0], w_up[0], w_down[0], ln2g, ln2b, tm)

    ylru_s, q2d, kn, vn, cstate_s, hnew_s = _sample_in(
        x_sample.reshape(DEC_BATCH, d), sh1_s, sc1_s, w_in_p, state_conv[0], state_h[0],
        convw, convb, wlo, whi, bgate, lam)
    yatt3, newk, newv = _sample_attn(q2d.reshape(DEC_BATCH, N_HEADS, LANES), kn, vn,
                                     cache_k[0].reshape(DEC_BATCH, WINDOW, KV_WIDTH),
                                     cache_v[0].reshape(DEC_BATCH, WINDOW, KV_WIDTH), sink_p)
    x1_s, comb_s = _sample_out(x_sample.reshape(DEC_BATCH, d), ylru_s, yatt3.reshape(DEC_BATCH * N_HEADS, LANES),
                               sh2_s, sc2_s, gt1_s, glru, g_attn_p, w_out_p, ln1g, ln1b, wr, br)
    y_s = _moe_dense(x1_s, comb_s, sh2_s, sc2_s, gt2_s, 0, w_gate[0], w_up[0], w_down[0], ln2g, ln2b, DEC_BATCH)

    kq = 2 * LRU_WIDTH + ATTN_WIDTH
    return (y_p.reshape(BATCH, SEQ, d),
            y_s.reshape(DEC_BATCH, 1, d),
            cstate8[:, 5:8][None],
            hlast8[:, 7][None],
            zin[:, SEQ - WINDOW:, kq:kq + KV_WIDTH].reshape(1, BATCH, WINDOW, N_KV_HEADS, HEAD_DIM),
            zin[:, SEQ - WINDOW:, kq + KV_WIDTH:].reshape(1, BATCH, WINDOW, N_KV_HEADS, HEAD_DIM),
            cstate_s[None],
            hnew_s[None],
            newk.reshape(1, DEC_BATCH, WINDOW, N_KV_HEADS, HEAD_DIM),
            newv.reshape(1, DEC_BATCH, WINDOW, N_KV_HEADS, HEAD_DIM))
```

```python
import functools

import jax
import jax.numpy as jnp
import numpy as np
from jax import lax
from jax.experimental import pallas as pl
from jax.experimental.pallas import tpu as pltpu

F32 = jnp.float32
BF16 = jnp.bfloat16
HIGHEST = lax.Precision.HIGHEST

D_MODEL = 1024
BATCH = 4
SEQ = 4096
DEC_BATCH = 128
LRU_WIDTH = 512
LRU_BLOCKS = 8
LRU_BLOCK = 64
CONV_WIDTH = 4
LRU_C = 8.0
N_HEADS = 8
N_KV_HEADS = 2
HEAD_DIM = 64
ATTN_WIDTH = 512
KV_WIDTH = 128
WINDOW = 128
IN_WIDTH = 2 * LRU_WIDTH + ATTN_WIDTH + 2 * KV_WIDTH
N_GROUPS = 4
EXPERTS_PER_GROUP = 8
N_EXPERTS = 32
D_EXPERT = 256
DEEPNORM_ALPHA = 2.0 ** 0.25
LN_EPS = 1e-5
RMS_EPS = 1e-6
ATTN_SCALE = HEAD_DIM ** -0.5

LANES = 128
ROUTE_LANES = 128
ROUTE_INFO = 40
VMEM_LIMIT = 56 * 1024 * 1024

HEAD_PERM = np.concatenate(
    [np.concatenate([np.arange(64 * c, 64 * c + 64), np.arange(64 * (c + 4), 64 * (c + 4) + 64)])
     for c in range(4)])


def _cparams(sem):
    return pltpu.CompilerParams(dimension_semantics=sem, vmem_limit_bytes=VMEM_LIMIT)


def _dot(a, b, exact):
    if exact:
        return jnp.dot(a, b, precision=HIGHEST, preferred_element_type=F32)
    return jnp.dot(a.astype(BF16), b.astype(BF16), preferred_element_type=F32)


def _dot_nt(a, b, exact):
    dn = (((1,), (1,)), ((), ()))
    if exact:
        return lax.dot_general(a, b, dn, precision=HIGHEST, preferred_element_type=F32)
    return lax.dot_general(a.astype(BF16), b.astype(BF16), dn, preferred_element_type=F32)


def _sigmoid(x):
    return 1.0 / (1.0 + jnp.exp(-x))


def _silu(x):
    return x * _sigmoid(x)


def _gelu_tanh(x):
    return 0.5 * x * (1.0 + jnp.tanh(np.sqrt(2.0 / np.pi).astype(np.float32) * (x + 0.044715 * (x * x * x))))


def _softplus(x):
    return jnp.maximum(x, 0.0) + jnp.log1p(jnp.exp(-jnp.abs(x)))


def _layer_norm(x, g, b):
    mu = jnp.mean(x, axis=-1, keepdims=True)
    xc = x - mu
    var = jnp.mean(xc * xc, axis=-1, keepdims=True)
    return xc * lax.rsqrt(var + LN_EPS) * g + b


def _rms_norm(x, g):
    return x * lax.rsqrt(jnp.mean(x * x, axis=-1, keepdims=True) + RMS_EPS) * g


def _ada_kernel(c_ref, w_ref, b_ref, o_ref):
    o_ref[...] = _dot(_silu(c_ref[...]), w_ref[...], True) + b_ref[...]


def _ada(c_all, w_ada, b_ada):
    rows = c_all.shape[0]
    bn = 512
    return pl.pallas_call(
        _ada_kernel,
        grid=(6 * D_MODEL // bn,),
        in_specs=[pl.BlockSpec((rows, D_MODEL), lambda j: (0, 0)),
                  pl.BlockSpec((D_MODEL, bn), lambda j: (0, j)),
                  pl.BlockSpec((1, bn), lambda j: (0, j))],
        out_specs=pl.BlockSpec((rows, bn), lambda j: (0, j)),
        out_shape=jax.ShapeDtypeStruct((rows, 6 * D_MODEL), F32),
        compiler_params=_cparams(("arbitrary",)),
        name="ada_modulation",
    )(c_all, w_ada, b_ada.reshape(1, -1))


def _inproj_kernel(x_ref, mod_ref, w_ref, o_ref):
    sh1 = mod_ref[0, 0:1, :]
    sc1 = mod_ref[0, 1:2, :]
    h = x_ref[0] * (1.0 + sc1) + sh1
    o_ref[0] = _dot(h, w_ref[...], False)


def _inproj(x, modp, w_in_bf16, tm=512):
    b, t, d = x.shape
    return pl.pallas_call(
        _inproj_kernel,
        grid=(b, t // tm),
        in_specs=[pl.BlockSpec((1, tm, d), lambda i, j: (i, j, 0)),
                  pl.BlockSpec((1, 6, d), lambda i, j: (i, 0, 0)),
                  pl.BlockSpec((d, IN_WIDTH), lambda i, j: (0, 0))],
        out_specs=pl.BlockSpec((1, tm, IN_WIDTH), lambda i, j: (i, j, 0)),
        out_shape=jax.ShapeDtypeStruct((b, t, IN_WIDTH), F32),
        compiler_params=_cparams(("arbitrary", "arbitrary")),
        name="prompt_inproj",
    )(x, modp, w_in_bf16)


def _lru_gates(xc, wlo, whi, bgate, sp_neg_lam, exact):
    g_lo = _dot(xc[:, :256], wlo, exact)
    g_hi = _dot(xc[:, 256:], whi, exact)
    ga = jnp.concatenate([g_lo[:, :256], g_hi[:, :256]], axis=-1) + bgate[:, :LRU_WIDTH]
    gx = jnp.concatenate([g_lo[:, 256:], g_hi[:, 256:]], axis=-1) + bgate[:, LRU_WIDTH:]
    r = _sigmoid(ga)
    i = _sigmoid(gx)
    log_a = -LRU_C * r * sp_neg_lam
    a = jnp.exp(log_a)
    one_minus_a2 = -jnp.tanh(log_a) * (a * a + 1.0)
    bterm = jnp.sqrt(one_minus_a2) * (i * xc)
    return a, bterm


def _lru_kernel(z_ref, convw_ref, convb_ref, wlo_ref, whi_ref, bgate_ref, lam_ref,
                y_ref, cstate_ref, hlast_ref, tail_ref, carry_ref, *, tl):
    j = pl.program_id(1)

    @pl.when(j == 0)
    def _():
        tail_ref[...] = jnp.zeros_like(tail_ref)
        carry_ref[...] = jnp.zeros_like(carry_ref)

    xb = z_ref[0, :, :LRU_WIDTH]
    gate = z_ref[0, :, LRU_WIDTH:]
    rows = lax.broadcasted_iota(jnp.int32, (tl, LRU_WIDTH), 0)

    xc = convb_ref[...] + convw_ref[3:4, :] * xb
    rows8 = lax.broadcasted_iota(jnp.int32, (8, LRU_WIDTH), 0)
    tail = tail_ref[...]
    for back in (1, 2, 3):
        rolled = pltpu.roll(xb, back, axis=0)
        top = jnp.where(rows8 >= back, rolled[:8], pltpu.roll(tail, back, axis=0))
        shifted = jnp.concatenate([top, rolled[8:]], axis=0)
        xc = xc + convw_ref[3 - back:4 - back, :] * shifted
    tail_ref[...] = xb[tl - 8:, :]
    cstate_ref[0] = xb[tl - 8:, :]

    sp = _softplus(-lam_ref[...])
    a, bterm = _lru_gates(xc, wlo_ref[...], whi_ref[...], bgate_ref[...], sp, False)

    s = 1
    while s < tl:
        a_sh = jnp.where(rows >= s, pltpu.roll(a, s, axis=0), 1.0)
        b_sh = jnp.where(rows >= s, pltpu.roll(bterm, s, axis=0), 0.0)
        bterm = a * b_sh + bterm
        a = a * a_sh
        s *= 2
    h = a * carry_ref[7:8, :] + bterm
    carry_ref[...] = h[tl - 8:, :]
    hlast_ref[0] = h[tl - 8:, :]
    y_ref[0] = h * _gelu_tanh(gate)


def _lru(zin, conv_w, conv_b, wlo, whi, bgate, lam, tl=512):
    b, t, _ = zin.shape
    kern = functools.partial(_lru_kernel, tl=tl)
    full = lambda shp: pl.BlockSpec(shp, lambda i, j: tuple(0 for _ in shp))
    return pl.pallas_call(
        kern,
        grid=(b, t // tl),
        in_specs=[pl.BlockSpec((1, tl, 2 * LRU_WIDTH), lambda i, j: (i, j, 0)),
                  full((CONV_WIDTH, LRU_WIDTH)), full((1, LRU_WIDTH)),
                  full((256, 512)), full((256, 512)), full((1, 2 * LRU_WIDTH)), full((1, LRU_WIDTH))],
        out_specs=[pl.BlockSpec((1, tl, LRU_WIDTH), lambda i, j: (i, j, 0)),
                   pl.BlockSpec((1, 8, LRU_WIDTH), lambda i, j: (i, 0, 0)),
                   pl.BlockSpec((1, 8, LRU_WIDTH), lambda i, j: (i, 0, 0))],
        out_shape=[jax.ShapeDtypeStruct((b, t, LRU_WIDTH), F32),
                   jax.ShapeDtypeStruct((b, 8, LRU_WIDTH), F32),
                   jax.ShapeDtypeStruct((b, 8, LRU_WIDTH), F32)],
        scratch_shapes=[pltpu.VMEM((8, LRU_WIDTH), F32), pltpu.VMEM((8, LRU_WIDTH), F32)],
        compiler_params=_cparams(("arbitrary", "arbitrary")),
        name="prompt_rglru",
    )(zin, conv_w, conv_b, wlo, whi, bgate, lam)


def _attn_kernel(q_ref, k_ref, v_ref, sink_ref, o_ref, kprev_ref, vprev_ref):
    j = pl.program_id(1)

    @pl.when(j == 0)
    def _():
        kprev_ref[...] = jnp.zeros_like(kprev_ref)
        vprev_ref[...] = jnp.zeros_like(vprev_ref)

    blk = WINDOW
    lane = lax.broadcasted_iota(jnp.int32, (blk, LANES), 1)
    low = lane < HEAD_DIM
    q = q_ref[0]
    pieces = []
    for half in (0, 1):
        for c in range(4):
            qc = q[:, LANES * c:LANES * (c + 1)]
            pieces.append(jnp.where(low if half == 0 else ~low, qc, 0.0).astype(BF16))
    q8 = jnp.concatenate(pieces, axis=0)
    k_cur = k_ref[0]
    v_cur = v_ref[0]
    k_band = jnp.concatenate([kprev_ref[...], k_cur], axis=0)
    v_band = jnp.concatenate([vprev_ref[...], v_cur], axis=0)
    s = _dot_nt(q8, k_band, False) * ATTN_SCALE
    s = s.reshape(N_HEADS, blk, 2 * blk)
    qi = lax.broadcasted_iota(jnp.int32, (blk, 2 * blk), 0)
    sj = lax.broadcasted_iota(jnp.int32, (blk, 2 * blk), 1)
    rel = blk + qi - sj
    valid = (rel >= 0) & (rel <= WINDOW) & ((sj >= blk) | (j > 0))
    s = jnp.where(valid[None], s, -jnp.inf)
    sink = sink_ref[...].reshape(N_HEADS, blk, 1)
    m = jnp.maximum(jnp.max(s, axis=-1, keepdims=True), sink)
    e = jnp.exp(s - m)
    den = jnp.sum(e, axis=-1, keepdims=True) + jnp.exp(sink - m)
    p = (e * (1.0 / den)).reshape(N_HEADS * blk, 2 * blk)
    o8 = _dot(p, v_band, False)
    cols = []
    for c in range(4):
        cols.append(jnp.where(low, o8[blk * c:blk * (c + 1)], o8[blk * (c + 4):blk * (c + 5)]))
    o_ref[0] = jnp.concatenate(cols, axis=-1)
    kprev_ref[...] = k_cur
    vprev_ref[...] = v_cur


def _attn(zin, sinks):
    b, t, _ = zin.shape
    blk = WINDOW
    sink_col = jnp.repeat(sinks.astype(F32), blk).reshape(N_HEADS * blk, 1)
    return pl.pallas_call(
        _attn_kernel,
        grid=(b, t // blk),
        in_specs=[pl.BlockSpec((1, blk, ATTN_WIDTH), lambda i, j: (i, j, 2)),
                  pl.BlockSpec((1, blk, KV_WIDTH), lambda i, j: (i, j, 12)),
                  pl.BlockSpec((1, blk, KV_WIDTH), lambda i, j: (i, j, 13)),
                  pl.BlockSpec((N_HEADS * blk, 1), lambda i, j: (0, 0))],
        out_specs=pl.BlockSpec((1, blk, ATTN_WIDTH), lambda i, j: (i, j, 0)),
        out_shape=jax.ShapeDtypeStruct((b, t, ATTN_WIDTH), F32),
        scratch_shapes=[pltpu.VMEM((blk, KV_WIDTH), F32), pltpu.VMEM((blk, KV_WIDTH), F32)],
        compiler_params=_cparams(("arbitrary", "arbitrary")),
        name="prompt_window_attention",
    )(zin, zin, zin, sink_col)


def _route(h2, wr, br, exact_unused=True):
    t = h2.shape[0]
    logits = _dot(h2, wr, True) + br
    lane = lax.broadcasted_iota(jnp.int32, (t, ROUTE_LANES), 1).astype(F32)
    neg = -jnp.inf
    big = float(ROUTE_LANES)
    is_g = (lane >= N_EXPERTS) & (lane < N_EXPERTS + N_GROUPS)
    lg = jnp.where(is_g, logits, neg)
    mg = jnp.max(lg, axis=-1, keepdims=True)
    eg = jnp.where(is_g, jnp.exp(lg - mg), 0.0)
    pg = eg / jnp.sum(eg, axis=-1, keepdims=True)
    g_val = jnp.max(pg, axis=-1, keepdims=True)
    g_lane = jnp.min(jnp.where((pg == g_val) & is_g, lane, big), axis=-1, keepdims=True)
    g_idx = g_lane - N_EXPERTS
    in_grp = (lane >= g_idx * EXPERTS_PER_GROUP) & (lane < (g_idx + 1.0) * EXPERTS_PER_GROUP)
    le = jnp.where(in_grp, logits, neg)
    me = jnp.max(le, axis=-1, keepdims=True)
    ee = jnp.where(in_grp, jnp.exp(le - me), 0.0)
    pe = ee / jnp.sum(ee, axis=-1, keepdims=True)
    v1 = jnp.max(pe, axis=-1, keepdims=True)
    l1 = jnp.min(jnp.where((pe == v1) & in_grp, lane, big), axis=-1, keepdims=True)
    rest = in_grp & (lane != l1)
    pe2 = jnp.where(rest, pe, -1.0)
    v2 = jnp.max(pe2, axis=-1, keepdims=True)
    l2 = jnp.min(jnp.where((pe2 == v2) & rest, lane, big), axis=-1, keepdims=True)
    tot = v1 + v2
    w1 = g_val * v1 / tot
    w2 = g_val * v2 / tot
    comb = jnp.where(lane == l1, w1, 0.0) + jnp.where(lane == l2, w2, 0.0)
    return (comb + jnp.where(lane == ROUTE_INFO, l1, 0.0) + jnp.where(lane == ROUTE_INFO + 1, l2, 0.0)
            + jnp.where(lane == ROUTE_INFO + 2, w1, 0.0) + jnp.where(lane == ROUTE_INFO + 3, w2, 0.0))


def _outproj_body(x, ylru, yatt, sh2, sc2, gt1, glru, gattn, wout, ln1g, ln1b, wr, br, exact):
    mixin = jnp.concatenate([_rms_norm(ylru, glru), _rms_norm(yatt, gattn)], axis=-1)
    mix = _dot(mixin, wout, exact)
    x1 = _layer_norm(DEEPNORM_ALPHA * x + (1.0 + gt1) * mix, ln1g, ln1b)
    h2 = x1 * (1.0 + sc2) + sh2
    return x1, _route(h2, wr, br)


def _outproj_prompt_kernel(x_ref, ylru_ref, yatt_ref, mod_ref, glru_ref, gattn_ref, wout_ref,
                           ln1g_ref, ln1b_ref, wr_ref, br_ref, x1_ref, comb_ref):
    gt1 = mod_ref[0, 2:3, :]
    sh2 = mod_ref[0, 3:4, :]
    sc2 = mod_ref[0, 4:5, :]
    x1, comb = _outproj_body(x_ref[...], ylru_ref[...], yatt_ref[...], sh2, sc2, gt1,
                             glru_ref[...], gattn_ref[...], wout_ref[...], ln1g_ref[...], ln1b_ref[...],
                             wr_ref[...], br_ref[...], False)
    x1_ref[...] = x1
    comb_ref[...] = comb


def _outproj_prompt(x2d, ylru2d, yatt2d, modp, glru, gattn, wout_bf16, ln1g, ln1b, wr, br, tm=512):
    n, d = x2d.shape
    per_batch = SEQ // tm
    full = lambda shp: pl.BlockSpec(shp, lambda i: tuple(0 for _ in shp))
    return pl.pallas_call(
        _outproj_prompt_kernel,
        grid=(n // tm,),
        in_specs=[pl.BlockSpec((tm, d), lambda i: (i, 0)),
                  pl.BlockSpec((tm, LRU_WIDTH), lambda i: (i, 0)),
                  pl.BlockSpec((tm, ATTN_WIDTH), lambda i: (i, 0)),
                  pl.BlockSpec((1, 6, d), lambda i: (i // per_batch, 0, 0)),
                  full((1, LRU_WIDTH)), full((1, ATTN_WIDTH)), full((d, d)),
                  full((1, d)), full((1, d)), full((d, ROUTE_LANES)), full((1, ROUTE_LANES))],
        out_specs=[pl.BlockSpec((tm, d), lambda i: (i, 0)),
                   pl.BlockSpec((tm, ROUTE_LANES), lambda i: (i, 0))],
        out_shape=[jax.ShapeDtypeStruct((n, d), F32), jax.ShapeDtypeStruct((n, ROUTE_LANES), F32)],
        compiler_params=_cparams(("arbitrary",)),
        name="prompt_outproj_ln_route",
    )(x2d, ylru2d, yatt2d, modp, glru, gattn, wout_bf16, ln1g, ln1b, wr, br)


def _moe_kernel(x1_ref, comb_ref, sh2_ref, sc2_ref, gt2_ref, wg_ref, wu_ref, wd_ref, ln2g_ref, ln2b_ref,
                o_ref, h2_ref, acc_ref):
    e = pl.program_id(1)

    @pl.when(e == 0)
    def _():
        h2_ref[...] = (x1_ref[...] * (1.0 + sc2_ref[...]) + sh2_ref[...]).astype(BF16)
        acc_ref[...] = jnp.zeros_like(acc_ref)

    h2 = h2_ref[...]
    a = jnp.dot(h2, wg_ref[0].astype(BF16), preferred_element_type=F32)
    u = jnp.dot(h2, wu_ref[0].astype(BF16), preferred_element_type=F32)
    comb = comb_ref[...]
    lane = lax.broadcasted_iota(jnp.int32, comb.shape, 1)
    c_e = jnp.sum(jnp.where(lane == e, comb, 0.0), axis=-1, keepdims=True)
    z = _silu(a) * u * c_e
    acc_ref[...] += jnp.dot(z.astype(BF16), wd_ref[0].astype(BF16), preferred_element_type=F32)

    @pl.when(e == N_EXPERTS - 1)
    def _():
        o_ref[...] = _layer_norm(DEEPNORM_ALPHA * x1_ref[...] + (1.0 + gt2_ref[...]) * acc_ref[...],
                                 ln2g_ref[...], ln2b_ref[...])


def _moe_dense(x1, comb, sh2, sc2, gt2, mod_rows_per_tile, w_gate, w_up, w_down, ln2g, ln2b, tm):
    n, d = x1.shape
    if mod_rows_per_tile:
        mspec = pl.BlockSpec((1, 1, d), lambda i, e: (i // mod_rows_per_tile, 0, 0))
        sh2, sc2, gt2 = (m.reshape(-1, 1, d) for m in (sh2, sc2, gt2))
        kern = lambda x1r, cr, s1, s2, s3, *rest: _moe_kernel(x1r, cr, s1.at[0], s2.at[0], s3.at[0], *rest)
    else:
        mspec = pl.BlockSpec((tm, d), lambda i, e: (i, 0))
        kern = _moe_kernel
    full = lambda shp: pl.BlockSpec(shp, lambda i, e: tuple(0 for _ in shp))
    return pl.pallas_call(
        kern,
        grid=(n // tm, N_EXPERTS),
        in_specs=[pl.BlockSpec((tm, d), lambda i, e: (i, 0)),
                  pl.BlockSpec((tm, ROUTE_LANES), lambda i, e: (i, 0)),
                  mspec, mspec, mspec,
                  pl.BlockSpec((1, d, D_EXPERT), lambda i, e: (e, 0, 0)),
                  pl.BlockSpec((1, d, D_EXPERT), lambda i, e: (e, 0, 0)),
                  pl.BlockSpec((1, D_EXPERT, d), lambda i, e: (e, 0, 0)),
                  full((1, d)), full((1, d))],
        out_specs=pl.BlockSpec((tm, d), lambda i, e: (i, 0)),
        out_shape=jax.ShapeDtypeStruct((n, d), F32),
        scratch_shapes=[pltpu.VMEM((tm, d), BF16), pltpu.VMEM((tm, d), F32)],
        compiler_params=_cparams(("arbitrary", "arbitrary")),
        name="moe_dense_ln",
    )(x1, comb, sh2, sc2, gt2, w_gate, w_up, w_down, ln2g, ln2b)


MOE_TS = SEQ
MOE_SUB = 512
MOE_NSUB = MOE_TS // MOE_SUB
MOE_CHUNK = 128
MOE_NCHUNK = 2 * MOE_TS // MOE_CHUNK + N_EXPERTS
MOE_PITCH = MOE_CHUNK + 8
MOE_GROUP = 8


def _moe_routed_kernel(ce_ref, nv_ref, x1_ref, mod_ref, idx_ref, wgt_ref, wg_ref, wu_ref, wd_ref,
                       ln2g_ref, ln2b_ref, o_ref, hbuf_ref, acc_ref, xg_ref, yt_ref):
    b = pl.program_id(0)
    s = pl.program_id(1)

    @pl.when(s < MOE_NSUB)
    def _fill():
        sh2 = mod_ref[0, 3:4, :]
        sc2 = mod_ref[0, 4:5, :]
        h2 = x1_ref[...] * (1.0 + sc2) + sh2
        for j in range(8):
            hbuf_ref[pl.ds(s * (8 * MOE_SUB) + j, MOE_SUB, stride=8), :] = h2[:, LANES * j:LANES * (j + 1)]
        acc_ref[pl.ds(pl.multiple_of(s * (8 * MOE_SUB), 8), 8 * MOE_SUB), :] = jnp.zeros((8 * MOE_SUB, LANES), F32)
        hbuf_ref[8 * MOE_TS:8 * MOE_TS + 8, :] = jnp.zeros((8, LANES), F32)
        acc_ref[8 * MOE_TS:8 * MOE_TS + 8, :] = jnp.zeros((8, LANES), F32)

    c = jnp.clip(s - MOE_NSUB, 0, MOE_NCHUNK - 1)

    @pl.when((s >= MOE_NSUB) & (s < MOE_NSUB + MOE_NCHUNK) & (nv_ref[b, c] > 0))
    def _chunk():
        for r in range(MOE_CHUNK):
            t = idx_ref[0, 0, r]
            xg_ref[pl.ds(r, 8, stride=MOE_PITCH), :] = hbuf_ref[pl.ds(pl.multiple_of(t * 8, 8), 8), :]
        x = jnp.concatenate([xg_ref[MOE_PITCH * j:MOE_PITCH * j + MOE_CHUNK, :] for j in range(8)],
                            axis=-1).astype(BF16)
        a = jnp.dot(x, wg_ref[0], preferred_element_type=F32)
        u = jnp.dot(x, wu_ref[0], preferred_element_type=F32)
        z = (_silu(a) * u).astype(BF16)
        y = jnp.dot(z, wd_ref[0], preferred_element_type=F32)
        for j in range(8):
            yt_ref[MOE_PITCH * j:MOE_PITCH * j + MOE_CHUNK, :] = y[:, LANES * j:LANES * (j + 1)]
        for g in range(MOE_CHUNK // MOE_GROUP):
            rows = range(g * MOE_GROUP, (g + 1) * MOE_GROUP)
            starts = [pl.multiple_of(idx_ref[0, 0, r] * 8, 8) for r in rows]
            new = [acc_ref[pl.ds(st, 8), :] + wgt_ref[0, 0, r] * yt_ref[pl.ds(r, 8, stride=MOE_PITCH), :]
                   for st, r in zip(starts, rows)]
            for st, v in zip(starts, new):
                acc_ref[pl.ds(st, 8), :] = v

    @pl.when(s >= MOE_NSUB + MOE_NCHUNK)
    def _finish():
        o = s - (MOE_NSUB + MOE_NCHUNK)
        gt2 = mod_ref[0, 5:6, :]
        f = jnp.concatenate([acc_ref[pl.ds(o * (8 * MOE_SUB) + j, MOE_SUB, stride=8), :] for j in range(8)], axis=-1)
        o_ref[...] = _layer_norm(DEEPNORM_ALPHA * x1_ref[...] + (1.0 + gt2) * f, ln2g_ref[...], ln2b_ref[...])


def _moe_plan(e1, e2, w1, w2):
    bsz, t = e1.shape
    ea = jnp.stack([e1, e2], axis=-1).reshape(bsz, 2 * t)
    wa = jnp.stack([w1, w2], axis=-1).reshape(bsz, 2 * t)
    ta = jnp.broadcast_to(jnp.repeat(jnp.arange(t, dtype=jnp.int32), 2)[None], (bsz, 2 * t))
    oh = (ea[..., None] == jnp.arange(N_EXPERTS, dtype=jnp.int32)).astype(jnp.int32)
    cnt = jnp.sum(oh, axis=1)
    rank = jnp.sum((jnp.cumsum(oh, axis=1) - oh) * oh, axis=-1)
    kch = (cnt + MOE_CHUNK - 1) // MOE_CHUNK
    cend = jnp.cumsum(kch, axis=-1)
    cstart = cend - kch
    pos = jnp.take_along_axis(cstart, ea, axis=-1) * MOE_CHUNK + rank
    nrows = MOE_NCHUNK * MOE_CHUNK
    bidx = jnp.arange(bsz, dtype=jnp.int32)[:, None]
    idx = jnp.full((bsz, nrows), t, jnp.int32).at[bidx, pos].set(ta)
    wgt = jnp.zeros((bsz, nrows), F32).at[bidx, pos].set(wa)
    cid = jnp.arange(MOE_NCHUNK, dtype=jnp.int32)
    ce = jnp.sum((cid[None, :, None] >= cend[:, None, :]).astype(jnp.int32), axis=-1)
    ntot = cend[:, -1:]
    last = jnp.take_along_axis(jnp.minimum(ce, N_EXPERTS - 1), jnp.maximum(ntot - 1, 0), axis=-1)
    valid = cid[None] < ntot
    ce = jnp.where(valid, jnp.minimum(ce, N_EXPERTS - 1), last)
    left = jnp.take_along_axis(cnt, ce, axis=-1) - (cid[None] - jnp.take_along_axis(cstart, ce, axis=-1)) * MOE_CHUNK
    nv = jnp.where(valid, jnp.clip(left, 0, MOE_CHUNK), 0).astype(jnp.int32)
    return (ce.astype(jnp.int32), nv, idx.reshape(bsz * MOE_NCHUNK, 1, MOE_CHUNK),
            wgt.reshape(bsz * MOE_NCHUNK, 1, MOE_CHUNK))


def _moe_routed(x1, modp, plan, wg_bf16, wu_bf16, wd_bf16, ln2g, ln2b):
    n, d = x1.shape
    bsz = n // MOE_TS
    ce, nv, idx, wgt = plan
    nsteps = 2 * MOE_NSUB + MOE_NCHUNK

    def sub_index(s):
        return jnp.where(s < MOE_NSUB, s,
                         jnp.where(s < MOE_NSUB + MOE_NCHUNK, MOE_NSUB - 1, s - (MOE_NSUB + MOE_NCHUNK)))

    def x1_map(b, s, ce_r, nv_r):
        return (b * MOE_NSUB + sub_index(s), 0)

    def out_map(b, s, ce_r, nv_r):
        return (b * MOE_NSUB + jnp.maximum(s - (MOE_NSUB + MOE_NCHUNK), 0), 0)

    def chunk_of(s):
        return jnp.clip(s - MOE_NSUB, 0, MOE_NCHUNK - 1)

    def row_map(b, s, ce_r, nv_r):
        return (b * MOE_NCHUNK + chunk_of(s), 0, 0)

    def w_map(b, s, ce_r, nv_r):
        return (ce_r[b, chunk_of(s)], 0, 0)

    const = lambda shp: pl.BlockSpec(shp, lambda b, s, ce_r, nv_r: tuple(0 for _ in shp))
    grid_spec = pltpu.PrefetchScalarGridSpec(
        num_scalar_prefetch=2,
        grid=(bsz, nsteps),
        in_specs=[pl.BlockSpec((MOE_SUB, d), x1_map),
                  pl.BlockSpec((1, 6, d), lambda b, s, ce_r, nv_r: (b, 0, 0)),
                  pl.BlockSpec((1, 1, MOE_CHUNK), row_map, memory_space=pltpu.SMEM),
                  pl.BlockSpec((1, 1, MOE_CHUNK), row_map, memory_space=pltpu.SMEM),
                  pl.BlockSpec((1, d, D_EXPERT), w_map),
                  pl.BlockSpec((1, d, D_EXPERT), w_map),
                  pl.BlockSpec((1, D_EXPERT, d), w_map),
                  const((1, d)), const((1, d))],
        out_specs=pl.BlockSpec((MOE_SUB, d), out_map),
        scratch_shapes=[pltpu.VMEM((8 * (MOE_TS + 1), LANES), F32),
                        pltpu.VMEM((8 * (MOE_TS + 1), LANES), F32),
                        pltpu.VMEM((8 * MOE_PITCH, LANES), F32),
                        pltpu.VMEM((8 * MOE_PITCH, LANES), F32)])
    return pl.pallas_call(
        _moe_routed_kernel,
        grid_spec=grid_spec,
        out_shape=jax.ShapeDtypeStruct((n, d), F32),
        compiler_params=_cparams(("arbitrary", "arbitrary")),
        name="moe_routed_ln",
    )(ce, nv, x1, modp, idx, wgt, wg_bf16, wu_bf16, wd_bf16, ln2g, ln2b)


def _sample_in_kernel(x_ref, sh1_ref, sc1_ref, win_ref, ctx_ref, h0_ref, convw_ref, convb_ref,
                      wlo_ref, whi_ref, bgate_ref, lam_ref,
                      ylru_ref, q_ref, k_ref, v_ref, cstate_ref, hnew_ref):
    h = x_ref[...] * (1.0 + sc1_ref[...]) + sh1_ref[...]
    z = _dot(h, win_ref[...], True)
    xb = z[:, :LRU_WIDTH]
    gate = z[:, LRU_WIDTH:2 * LRU_WIDTH]
    c0 = ctx_ref[:, 0, :]
    c1 = ctx_ref[:, 1, :]
    c2 = ctx_ref[:, 2, :]
    xc = (convb_ref[...] + convw_ref[0:1, :] * c0 + convw_ref[1:2, :] * c1
          + convw_ref[2:3, :] * c2 + convw_ref[3:4, :] * xb)
    cstate_ref[:, 0, :] = c1
    cstate_ref[:, 1, :] = c2
    cstate_ref[:, 2, :] = xb
    sp = _softplus(-lam_ref[...])
    a, bterm = _lru_gates(xc, wlo_ref[...], whi_ref[...], bgate_ref[...], sp, True)
    hn = a * h0_ref[...] + bterm
    hnew_ref[...] = hn
    ylru_ref[...] = hn * _gelu_tanh(gate)
    low = lax.broadcasted_iota(jnp.int32, (DEC_BATCH, LANES), 1) < HEAD_DIM
    for c in range(4):
        qc = z[:, 2 * LRU_WIDTH + LANES * c:2 * LRU_WIDTH + LANES * (c + 1)]
        q_ref[pl.ds(c, DEC_BATCH, stride=N_HEADS), :] = jnp.where(low, qc, 0.0)
        q_ref[pl.ds(c + 4, DEC_BATCH, stride=N_HEADS), :] = jnp.where(low, 0.0, qc)
    k_ref[...] = z[:, 2 * LRU_WIDTH + ATTN_WIDTH:2 * LRU_WIDTH + ATTN_WIDTH + KV_WIDTH]
    v_ref[...] = z[:, 2 * LRU_WIDTH + ATTN_WIDTH + KV_WIDTH:]


def _sample_in(x, sh1, sc1, w_in_p, ctx, h0, conv_w, conv_b, wlo, whi, bgate, lam):
    n = DEC_BATCH
    outs = [jax.ShapeDtypeStruct((n, LRU_WIDTH), F32),
            jax.ShapeDtypeStruct((n * N_HEADS, LANES), F32),
            jax.ShapeDtypeStruct((n, KV_WIDTH), F32),
            jax.ShapeDtypeStruct((n, KV_WIDTH), F32),
            jax.ShapeDtypeStruct((n, CONV_WIDTH - 1, LRU_WIDTH), F32),
            jax.ShapeDtypeStruct((n, LRU_WIDTH), F32)]
    return pl.pallas_call(
        _sample_in_kernel,
        out_shape=outs,
        compiler_params=pltpu.CompilerParams(vmem_limit_bytes=VMEM_LIMIT),
        name="sample_inproj_rglru",
    )(x, sh1, sc1, w_in_p, ctx, h0, conv_w, conv_b, wlo, whi, bgate, lam)


def _sample_attn_kernel(q_ref, kn_ref, vn_ref, ck_ref, cv_ref, sink_ref, y_ref, nk_ref, nv_ref, *, bb):
    rows = lax.broadcasted_iota(jnp.int32, (WINDOW, KV_WIDTH), 0)
    sink = sink_ref[...]
    for b in range(bb):
        q8 = q_ref[b]
        kb = ck_ref[b]
        vb = cv_ref[b]
        kn = kn_ref[b:b + 1, :]
        vn = vn_ref[b:b + 1, :]
        s = _dot_nt(q8, kb, True) * ATTN_SCALE
        s_self = jnp.sum(q8 * kn, axis=-1, keepdims=True) * ATTN_SCALE
        m = jnp.maximum(jnp.maximum(jnp.max(s, axis=-1, keepdims=True), s_self), sink)
        e = jnp.exp(s - m)
        e_self = jnp.exp(s_self - m)
        den = jnp.sum(e, axis=-1, keepdims=True) + e_self + jnp.exp(sink - m)
        inv = 1.0 / den
        y_ref[b] = _dot(e * inv, vb, True) + (e_self * inv) * vn
        nk_ref[b] = jnp.where(rows == WINDOW - 1, kn, pltpu.roll(kb, WINDOW - 1, axis=0))
        nv_ref[b] = jnp.where(rows == WINDOW - 1, vn, pltpu.roll(vb, WINDOW - 1, axis=0))


def _sample_attn(q3, kn, vn, cache_k, cache_v, sinks, bb=8):
    n = DEC_BATCH
    kern = functools.partial(_sample_attn_kernel, bb=bb)
    return pl.pallas_call(
        kern,
        grid=(n // bb,),
        in_specs=[pl.BlockSpec((bb, N_HEADS, LANES), lambda i: (i, 0, 0)),
                  pl.BlockSpec((bb, KV_WIDTH), lambda i: (i, 0)),
                  pl.BlockSpec((bb, KV_WIDTH), lambda i: (i, 0)),
                  pl.BlockSpec((bb, WINDOW, KV_WIDTH), lambda i: (i, 0, 0)),
                  pl.BlockSpec((bb, WINDOW, KV_WIDTH), lambda i: (i, 0, 0)),
                  pl.BlockSpec((N_HEADS, 1), lambda i: (0, 0))],
        out_specs=[pl.BlockSpec((bb, N_HEADS, LANES), lambda i: (i, 0, 0)),
                   pl.BlockSpec((bb, WINDOW, KV_WIDTH), lambda i: (i, 0, 0)),
                   pl.BlockSpec((bb, WINDOW, KV_WIDTH), lambda i: (i, 0, 0))],
        out_shape=[jax.ShapeDtypeStruct((n, N_HEADS, LANES), F32),
                   jax.ShapeDtypeStruct((n, WINDOW, KV_WIDTH), F32),
                   jax.ShapeDtypeStruct((n, WINDOW, KV_WIDTH), F32)],
        compiler_params=_cparams(("arbitrary",)),
        name="sample_cache_attention",
    )(q3, kn, vn, cache_k, cache_v, sinks.reshape(N_HEADS, 1))


def _sample_out_kernel(x_ref, ylru_ref, yatt_ref, sh2_ref, sc2_ref, gt1_ref, glru_ref, gattn_ref, wout_ref,
                       ln1g_ref, ln1b_ref, wr_ref, br_ref, x1_ref, comb_ref):
    low = lax.broadcasted_iota(jnp.int32, (DEC_BATCH, LANES), 1) < HEAD_DIM
    yatt = jnp.concatenate(
        [jnp.where(low, yatt_ref[pl.ds(c, DEC_BATCH, stride=N_HEADS), :],
                   yatt_ref[pl.ds(c + 4, DEC_BATCH, stride=N_HEADS), :]) for c in range(4)], axis=-1)
    x1, comb = _outproj_body(x_ref[...], ylru_ref[...], yatt, sh2_ref[...], sc2_ref[...], gt1_ref[...],
                             glru_ref[...], gattn_ref[...], wout_ref[...], ln1g_ref[...], ln1b_ref[...],
                             wr_ref[...], br_ref[...], True)
    x1_ref[...] = x1
    comb_ref[...] = comb


def _sample_out(x, ylru, yatt2d, sh2, sc2, gt1, glru, gattn, wout_p, ln1g, ln1b, wr, br):
    n = DEC_BATCH
    return pl.pallas_call(
        _sample_out_kernel,
        out_shape=[jax.ShapeDtypeStruct((n, D_MODEL), F32), jax.ShapeDtypeStruct((n, ROUTE_LANES), F32)],
        compiler_params=pltpu.CompilerParams(vmem_limit_bytes=VMEM_LIMIT),
        name="sample_outproj_ln_route",
    )(x, ylru, yatt2d, sh2, sc2, gt1, glru, gattn, wout_p, ln1g, ln1b, wr, br)


def _block_diag_halves(w_a, w_x):
    def bd(w4):
        eye = jnp.eye(4, dtype=w4.dtype)
        return (w4[:, :, None, :] * eye[:, None, :, None]).reshape(256, 256)
    lo = jnp.concatenate([bd(w_a[:4]), bd(w_x[:4])], axis=1)
    hi = jnp.concatenate([bd(w_a[4:]), bd(w_x[4:])], axis=1)
    return lo, hi


def kernel(x_prompt, x_sample, c_prompt, c_sample, state_conv, state_h, cache_k, cache_v, w_ada, b_ada, w_in,
           conv_w, conv_b, w_rg_a, b_rg_a, w_rg_x, b_rg_x, lru_lambda, sinks, g_lru, g_attn, w_out, ln1_g, ln1_b,
           w_group, b_group, w_router, b_router, w_gate, w_up, w_down, ln2_g, ln2_b):
    d = D_MODEL
    perm = jnp.asarray(HEAD_PERM)
    w_in0 = w_in[0]
    q0 = 2 * LRU_WIDTH
    w_in_p = jnp.concatenate([w_in0[:, :q0], w_in0[:, q0:q0 + ATTN_WIDTH][:, perm], w_in0[:, q0 + ATTN_WIDTH:]],
                             axis=1)
    w_out0 = w_out[0]
    w_out_p = jnp.concatenate([w_out0[:LRU_WIDTH], w_out0[LRU_WIDTH:][perm]], axis=0)
    g_attn_p = g_attn[0][perm].reshape(1, -1)
    glru = g_lru[0].reshape(1, -1)
    wlo, whi = _block_diag_halves(w_rg_a[0], w_rg_x[0])
    bgate = jnp.concatenate([b_rg_a[0].reshape(-1), b_rg_x[0].reshape(-1)]).reshape(1, -1)
    lam = lru_lambda[0].reshape(1, -1)
    convw = conv_w[0]
    convb = conv_b[0].reshape(1, -1)
    ln1g, ln1b = ln1_g[0].reshape(1, -1), ln1_b[0].reshape(1, -1)
    ln2g, ln2b = ln2_g[0].reshape(1, -1), ln2_b[0].reshape(1, -1)
    wr = jnp.concatenate([jnp.transpose(w_router[0], (1, 0, 2)).reshape(d, N_EXPERTS), w_group[0],
                          jnp.zeros((d, ROUTE_LANES - N_EXPERTS - N_GROUPS), F32)], axis=1)
    br = jnp.concatenate([b_router[0].reshape(-1), b_group[0],
                          jnp.zeros((ROUTE_LANES - N_EXPERTS - N_GROUPS,), F32)]).reshape(1, -1)
    sink_p = sinks[0]

    c_all = jnp.concatenate([c_prompt, jnp.zeros((8 - BATCH, d), F32), c_sample], axis=0)
    mod = _ada(c_all, w_ada[0], b_ada[0])
    modp = mod[:BATCH].reshape(BATCH, 6, d)
    mods = mod[8:]
    sh1_s, sc1_s, gt1_s, sh2_s, sc2_s, gt2_s = (mods[:, k * d:(k + 1) * d] for k in range(6))

    zin = _inproj(x_prompt, modp, w_in_p.astype(BF16))
    ylru, cstate8, hlast8 = _lru(zin, convw, convb, wlo.astype(BF16), whi.astype(BF16), bgate, lam)
    yatt = _attn(zin, sink_p)
    n_p = BATCH * SEQ
    x1_p, comb_p = _outproj_prompt(x_prompt.reshape(n_p, d), ylru.reshape(n_p, LRU_WIDTH),
                                   yatt.reshape(n_p, ATTN_WIDTH), modp, glru, g_attn_p, w_out_p.astype(BF16),
                                   ln1g, ln1b, wr, br)
    info = comb_p[:, ROUTE_INFO:ROUTE_INFO + 4].reshape(BATCH, SEQ, 4)
    plan = _moe_plan(info[..., 0].astype(jnp.int32), info[..., 1].astype(jnp.int32), info[..., 2], info[..., 3])
    y_p = _moe_routed(x1_p, modp, plan, w_gate[0].astype(BF16), w_up[0].astype(BF16), w_down[0].astype(BF16),
                      ln2g, ln2b)

    ylru_s, q2d, kn, vn, cstate_s, hnew_s = _sample_in(
        x_sample.reshape(DEC_BATCH, d), sh1_s, sc1_s, w_in_p, state_conv[0], state_h[0],
        convw, convb, wlo, whi, bgate, lam)
    yatt3, newk, newv = _sample_attn(q2d.reshape(DEC_BATCH, N_HEADS, LANES), kn, vn,
                                     cache_k[0].reshape(DEC_BATCH, WINDOW, KV_WIDTH),
                                     cache_v[0].reshape(DEC_BATCH, WINDOW, KV_WIDTH), sink_p)
    x1_s, comb_s = _sample_out(x_sample.reshape(DEC_BATCH, d), ylru_s, yatt3.reshape(DEC_BATCH * N_HEADS, LANES),
                               sh2_s, sc2_s, gt1_s, glru, g_attn_p, w_out_p, ln1g, ln1b, wr, br)
    y_s = _moe_dense(x1_s, comb_s, sh2_s, sc2_s, gt2_s, 0, w_gate[0], w_up[0], w_down[0], ln2g, ln2b, DEC_BATCH)

    kq = 2 * LRU_WIDTH + ATTN_WIDTH
    return (y_p.reshape(BATCH, SEQ, d),
            y_s.reshape(DEC_BATCH, 1, d),
            cstate8[:, 5:8][None],
            hlast8[:, 7][None],
            zin[:, SEQ - WINDOW:, kq:kq + KV_WIDTH].reshape(1, BATCH, WINDOW, N_KV_HEADS, HEAD_DIM),
            zin[:, SEQ - WINDOW:, kq + KV_WIDTH:].reshape(1, BATCH, WINDOW, N_KV_HEADS, HEAD_DIM),
            cstate_s[None],
            hnew_s[None],
            newk.reshape(1, DEC_BATCH, WINDOW, N_KV_HEADS, HEAD_DIM),
            newv.reshape(1, DEC_BATCH, WINDOW, N_KV_HEADS, HEAD_DIM))
```

```python
import functools

import jax
import jax.numpy as jnp
import numpy as np
from jax import lax
from jax.experimental import pallas as pl
from jax.experimental.pallas import tpu as pltpu

F32 = jnp.float32
BF16 = jnp.bfloat16
HIGHEST = lax.Precision.HIGHEST

D_MODEL = 1024
BATCH = 4
SEQ = 4096
DEC_BATCH = 128
LRU_WIDTH = 512
LRU_BLOCKS = 8
LRU_BLOCK = 64
CONV_WIDTH = 4
LRU_C = 8.0
N_HEADS = 8
N_KV_HEADS = 2
HEAD_DIM = 64
ATTN_WIDTH = 512
KV_WIDTH = 128
WINDOW = 128
IN_WIDTH = 2 * LRU_WIDTH + ATTN_WIDTH + 2 * KV_WIDTH
N_GROUPS = 4
EXPERTS_PER_GROUP = 8
N_EXPERTS = 32
D_EXPERT = 256
DEEPNORM_ALPHA = 2.0 ** 0.25
LN_EPS = 1e-5
RMS_EPS = 1e-6
ATTN_SCALE = HEAD_DIM ** -0.5

LANES = 128
ROUTE_LANES = 128
ROUTE_INFO = 40
VMEM_LIMIT = 56 * 1024 * 1024

HEAD_PERM = np.concatenate(
    [np.concatenate([np.arange(64 * c, 64 * c + 64), np.arange(64 * (c + 4), 64 * (c + 4) + 64)])
     for c in range(4)])


def _cparams(sem):
    return pltpu.CompilerParams(dimension_semantics=sem, vmem_limit_bytes=VMEM_LIMIT)


def _dot(a, b, exact):
    if exact:
        return jnp.dot(a, b, precision=HIGHEST, preferred_element_type=F32)
    return jnp.dot(a.astype(BF16), b.astype(BF16), preferred_element_type=F32)


def _dot_nt(a, b, exact):
    dn = (((1,), (1,)), ((), ()))
    if exact:
        return lax.dot_general(a, b, dn, precision=HIGHEST, preferred_element_type=F32)
    return lax.dot_general(a.astype(BF16), b.astype(BF16), dn, preferred_element_type=F32)


def _sigmoid(x):
    return 1.0 / (1.0 + jnp.exp(-x))


def _silu(x):
    return x * _sigmoid(x)


def _gelu_tanh(x):
    return 0.5 * x * (1.0 + jnp.tanh(np.sqrt(2.0 / np.pi).astype(np.float32) * (x + 0.044715 * (x * x * x))))


def _softplus(x):
    return jnp.maximum(x, 0.0) + jnp.log1p(jnp.exp(-jnp.abs(x)))


def _layer_norm(x, g, b):
    mu = jnp.mean(x, axis=-1, keepdims=True)
    xc = x - mu
    var = jnp.mean(xc * xc, axis=-1, keepdims=True)
    return xc * lax.rsqrt(var + LN_EPS) * g + b


def _rms_norm(x, g):
    return x * lax.rsqrt(jnp.mean(x * x, axis=-1, keepdims=True) + RMS_EPS) * g


def _ada_kernel(c_ref, w_ref, b_ref, o_ref):
    o_ref[...] = _dot(_silu(c_ref[...]), w_ref[...], True) + b_ref[...]


def _ada(c_all, w_ada, b_ada):
    rows = c_all.shape[0]
    bn = 512
    return pl.pallas_call(
        _ada_kernel,
        grid=(6 * D_MODEL // bn,),
        in_specs=[pl.BlockSpec((rows, D_MODEL), lambda j: (0, 0)),
                  pl.BlockSpec((D_MODEL, bn), lambda j: (0, j)),
                  pl.BlockSpec((1, bn), lambda j: (0, j))],
        out_specs=pl.BlockSpec((rows, bn), lambda j: (0, j)),
        out_shape=jax.ShapeDtypeStruct((rows, 6 * D_MODEL), F32),
        compiler_params=_cparams(("arbitrary",)),
        name="ada_modulation",
    )(c_all, w_ada, b_ada.reshape(1, -1))


def _inproj_kernel(x_ref, mod_ref, w_ref, o_ref):
    sh1 = mod_ref[0, 0:1, :]
    sc1 = mod_ref[0, 1:2, :]
    h = x_ref[0] * (1.0 + sc1) + sh1
    o_ref[0] = _dot(h, w_ref[...], False)


def _inproj(x, modp, w_in_bf16, tm=512):
    b, t, d = x.shape
    return pl.pallas_call(
        _inproj_kernel,
        grid=(b, t // tm),
        in_specs=[pl.BlockSpec((1, tm, d), lambda i, j: (i, j, 0)),
                  pl.BlockSpec((1, 6, d), lambda i, j: (i, 0, 0)),
                  pl.BlockSpec((d, IN_WIDTH), lambda i, j: (0, 0))],
        out_specs=pl.BlockSpec((1, tm, IN_WIDTH), lambda i, j: (i, j, 0)),
        out_shape=jax.ShapeDtypeStruct((b, t, IN_WIDTH), F32),
        compiler_params=_cparams(("arbitrary", "arbitrary")),
        name="prompt_inproj",
    )(x, modp, w_in_bf16)


def _lru_gates(xc, wlo, whi, bgate, sp_neg_lam, exact):
    g_lo = _dot(xc[:, :256], wlo, exact)
    g_hi = _dot(xc[:, 256:], whi, exact)
    ga = jnp.concatenate([g_lo[:, :256], g_hi[:, :256]], axis=-1) + bgate[:, :LRU_WIDTH]
    gx = jnp.concatenate([g_lo[:, 256:], g_hi[:, 256:]], axis=-1) + bgate[:, LRU_WIDTH:]
    r = _sigmoid(ga)
    i = _sigmoid(gx)
    log_a = -LRU_C * r * sp_neg_lam
    a = jnp.exp(log_a)
    one_minus_a2 = -jnp.tanh(log_a) * (a * a + 1.0)
    bterm = jnp.sqrt(one_minus_a2) * (i * xc)
    return a, bterm


def _lru_kernel(z_ref, convw_ref, convb_ref, wlo_ref, whi_ref, bgate_ref, lam_ref,
                y_ref, cstate_ref, hlast_ref, tail_ref, carry_ref, *, tl):
    j = pl.program_id(1)

    @pl.when(j == 0)
    def _():
        tail_ref[...] = jnp.zeros_like(tail_ref)
        carry_ref[...] = jnp.zeros_like(carry_ref)

    xb = z_ref[0, :, :LRU_WIDTH]
    gate = z_ref[0, :, LRU_WIDTH:]
    rows = lax.broadcasted_iota(jnp.int32, (tl, LRU_WIDTH), 0)

    xc = convb_ref[...] + convw_ref[3:4, :] * xb
    rows8 = lax.broadcasted_iota(jnp.int32, (8, LRU_WIDTH), 0)
    tail = tail_ref[...]
    for back in (1, 2, 3):
        rolled = pltpu.roll(xb, back, axis=0)
        top = jnp.where(rows8 >= back, rolled[:8], pltpu.roll(tail, back, axis=0))
        shifted = jnp.concatenate([top, rolled[8:]], axis=0)
        xc = xc + convw_ref[3 - back:4 - back, :] * shifted
    tail_ref[...] = xb[tl - 8:, :]
    cstate_ref[0] = xb[tl - 8:, :]

    sp = _softplus(-lam_ref[...])
    a, bterm = _lru_gates(xc, wlo_ref[...], whi_ref[...], bgate_ref[...], sp, False)

    s = 1
    while s < tl:
        a_sh = jnp.where(rows >= s, pltpu.roll(a, s, axis=0), 1.0)
        b_sh = jnp.where(rows >= s, pltpu.roll(bterm, s, axis=0), 0.0)
        bterm = a * b_sh + bterm
        a = a * a_sh
        s *= 2
    h = a * carry_ref[7:8, :] + bterm
    carry_ref[...] = h[tl - 8:, :]
    hlast_ref[0] = h[tl - 8:, :]
    y_ref[0] = h * _gelu_tanh(gate)


def _lru(zin, conv_w, conv_b, wlo, whi, bgate, lam, tl=512):
    b, t, _ = zin.shape
    kern = functools.partial(_lru_kernel, tl=tl)
    full = lambda shp: pl.BlockSpec(shp, lambda i, j: tuple(0 for _ in shp))
    return pl.pallas_call(
        kern,
        grid=(b, t // tl),
        in_specs=[pl.BlockSpec((1, tl, 2 * LRU_WIDTH), lambda i, j: (i, j, 0)),
                  full((CONV_WIDTH, LRU_WIDTH)), full((1, LRU_WIDTH)),
                  full((256, 512)), full((256, 512)), full((1, 2 * LRU_WIDTH)), full((1, LRU_WIDTH))],
        out_specs=[pl.BlockSpec((1, tl, LRU_WIDTH), lambda i, j: (i, j, 0)),
                   pl.BlockSpec((1, 8, LRU_WIDTH), lambda i, j: (i, 0, 0)),
                   pl.BlockSpec((1, 8, LRU_WIDTH), lambda i, j: (i, 0, 0))],
        out_shape=[jax.ShapeDtypeStruct((b, t, LRU_WIDTH), F32),
                   jax.ShapeDtypeStruct((b, 8, LRU_WIDTH), F32),
                   jax.ShapeDtypeStruct((b, 8, LRU_WIDTH), F32)],
        scratch_shapes=[pltpu.VMEM((8, LRU_WIDTH), F32), pltpu.VMEM((8, LRU_WIDTH), F32)],
        compiler_params=_cparams(("arbitrary", "arbitrary")),
        name="prompt_rglru",
    )(zin, conv_w, conv_b, wlo, whi, bgate, lam)


def _attn_kernel(q_ref, k_ref, v_ref, sink_ref, o_ref, kprev_ref, vprev_ref):
    j = pl.program_id(1)

    @pl.when(j == 0)
    def _():
        kprev_ref[...] = jnp.zeros_like(kprev_ref)
        vprev_ref[...] = jnp.zeros_like(vprev_ref)

    blk = WINDOW
    lane = lax.broadcasted_iota(jnp.int32, (blk, LANES), 1)
    low = lane < HEAD_DIM
    q = q_ref[0]
    pieces = []
    for half in (0, 1):
        for c in range(4):
            qc = q[:, LANES * c:LANES * (c + 1)]
            pieces.append(jnp.where(low if half == 0 else ~low, qc, 0.0).astype(BF16))
    q8 = jnp.concatenate(pieces, axis=0)
    k_cur = k_ref[0]
    v_cur = v_ref[0]
    k_band = jnp.concatenate([kprev_ref[...], k_cur], axis=0)
    v_band = jnp.concatenate([vprev_ref[...], v_cur], axis=0)
    s = _dot_nt(q8, k_band, False) * ATTN_SCALE
    s = s.reshape(N_HEADS, blk, 2 * blk)
    qi = lax.broadcasted_iota(jnp.int32, (blk, 2 * blk), 0)
    sj = lax.broadcasted_iota(jnp.int32, (blk, 2 * blk), 1)
    rel = blk + qi - sj
    valid = (rel >= 0) & (rel <= WINDOW) & ((sj >= blk) | (j > 0))
    s = jnp.where(valid[None], s, -jnp.inf)
    sink = sink_ref[...].reshape(N_HEADS, blk, 1)
    m = jnp.maximum(jnp.max(s, axis=-1, keepdims=True), sink)
    e = jnp.exp(s - m)
    den = jnp.sum(e, axis=-1, keepdims=True) + jnp.exp(sink - m)
    p = (e * (1.0 / den)).reshape(N_HEADS * blk, 2 * blk)
    o8 = _dot(p, v_band, False)
    cols = []
    for c in range(4):
        cols.append(jnp.where(low, o8[blk * c:blk * (c + 1)], o8[blk * (c + 4):blk * (c + 5)]))
    o_ref[0] = jnp.concatenate(cols, axis=-1)
    kprev_ref[...] = k_cur
    vprev_ref[...] = v_cur


def _attn(zin, sinks):
    b, t, _ = zin.shape
    blk = WINDOW
    sink_col = jnp.repeat(sinks.astype(F32), blk).reshape(N_HEADS * blk, 1)
    return pl.pallas_call(
        _attn_kernel,
        grid=(b, t // blk),
        in_specs=[pl.BlockSpec((1, blk, ATTN_WIDTH), lambda i, j: (i, j, 2)),
                  pl.BlockSpec((1, blk, KV_WIDTH), lambda i, j: (i, j, 12)),
                  pl.BlockSpec((1, blk, KV_WIDTH), lambda i, j: (i, j, 13)),
                  pl.BlockSpec((N_HEADS * blk, 1), lambda i, j: (0, 0))],
        out_specs=pl.BlockSpec((1, blk, ATTN_WIDTH), lambda i, j: (i, j, 0)),
        out_shape=jax.ShapeDtypeStruct((b, t, ATTN_WIDTH), F32),
        scratch_shapes=[pltpu.VMEM((blk, KV_WIDTH), F32), pltpu.VMEM((blk, KV_WIDTH), F32)],
        compiler_params=_cparams(("arbitrary", "arbitrary")),
        name="prompt_window_attention",
    )(zin, zin, zin, sink_col)


def _route(h2, wr, br, exact_unused=True):
    t = h2.shape[0]
    logits = _dot(h2, wr, True) + br
    lane = lax.broadcasted_iota(jnp.int32, (t, ROUTE_LANES), 1).astype(F32)
    neg = -jnp.inf
    big = float(ROUTE_LANES)
    is_g = (lane >= N_EXPERTS) & (lane < N_EXPERTS + N_GROUPS)
    lg = jnp.where(is_g, logits, neg)
    mg = jnp.max(lg, axis=-1, keepdims=True)
    eg = jnp.where(is_g, jnp.exp(lg - mg), 0.0)
    pg = eg / jnp.sum(eg, axis=-1, keepdims=True)
    g_val = jnp.max(pg, axis=-1, keepdims=True)
    g_lane = jnp.min(jnp.where((pg == g_val) & is_g, lane, big), axis=-1, keepdims=True)
    g_idx = g_lane - N_EXPERTS
    in_grp = (lane >= g_idx * EXPERTS_PER_GROUP) & (lane < (g_idx + 1.0) * EXPERTS_PER_GROUP)
    le = jnp.where(in_grp, logits, neg)
    me = jnp.max(le, axis=-1, keepdims=True)
    ee = jnp.where(in_grp, jnp.exp(le - me), 0.0)
    pe = ee / jnp.sum(ee, axis=-1, keepdims=True)
    v1 = jnp.max(pe, axis=-1, keepdims=True)
    l1 = jnp.min(jnp.where((pe == v1) & in_grp, lane, big), axis=-1, keepdims=True)
    rest = in_grp & (lane != l1)
    pe2 = jnp.where(rest, pe, -1.0)
    v2 = jnp.max(pe2, axis=-1, keepdims=True)
    l2 = jnp.min(jnp.where((pe2 == v2) & rest, lane, big), axis=-1, keepdims=True)
    tot = v1 + v2
    w1 = g_val * v1 / tot
    w2 = g_val * v2 / tot
    comb = jnp.where(lane == l1, w1, 0.0) + jnp.where(lane == l2, w2, 0.0)
    return (comb + jnp.where(lane == ROUTE_INFO, l1, 0.0) + jnp.where(lane == ROUTE_INFO + 1, l2, 0.0)
            + jnp.where(lane == ROUTE_INFO + 2, w1, 0.0) + jnp.where(lane == ROUTE_INFO + 3, w2, 0.0))


def _outproj_body(x, ylru, yatt, sh2, sc2, gt1, glru, gattn, wout, ln1g, ln1b, wr, br, exact):
    mixin = jnp.concatenate([_rms_norm(ylru, glru), _rms_norm(yatt, gattn)], axis=-1)
    mix = _dot(mixin, wout, exact)
    x1 = _layer_norm(DEEPNORM_ALPHA * x + (1.0 + gt1) * mix, ln1g, ln1b)
    h2 = x1 * (1.0 + sc2) + sh2
    return x1, _route(h2, wr, br)


def _outproj_prompt_kernel(x_ref, ylru_ref, yatt_ref, mod_ref, glru_ref, gattn_ref, wout_ref,
                           ln1g_ref, ln1b_ref, wr_ref, br_ref, x1_ref, info_ref, cnt_ref, tri_ref, carry_ref,
                           *, tm, per_seq):
    i = pl.program_id(0)

    @pl.when(i == 0)
    def _():
        r = lax.broadcasted_iota(jnp.int32, (tm, tm), 0)
        c = lax.broadcasted_iota(jnp.int32, (tm, tm), 1)
        tri_ref[...] = jnp.where(c < r, 1.0, 0.0).astype(BF16)

    @pl.when(i % per_seq == 0)
    def _():
        carry_ref[...] = jnp.zeros_like(carry_ref)

    gt1 = mod_ref[0, 2:3, :]
    sh2 = mod_ref[0, 3:4, :]
    sc2 = mod_ref[0, 4:5, :]
    x1, comb = _outproj_body(x_ref[...], ylru_ref[...], yatt_ref[...], sh2, sc2, gt1,
                             glru_ref[...], gattn_ref[...], wout_ref[...], ln1g_ref[...], ln1b_ref[...],
                             wr_ref[...], br_ref[...], False)
    x1_ref[...] = x1
    lane = lax.broadcasted_iota(jnp.int32, (tm, ROUTE_LANES), 1).astype(F32)
    l1 = jnp.sum(jnp.where(lane == ROUTE_INFO, comb, 0.0), axis=-1, keepdims=True)
    l2 = jnp.sum(jnp.where(lane == ROUTE_INFO + 1, comb, 0.0), axis=-1, keepdims=True)
    o1 = lane == l1
    o2 = lane == l2
    onehot = jnp.where(o1 | o2, 1.0, 0.0)
    before = jnp.dot(tri_ref[...], onehot.astype(BF16), preferred_element_type=F32) + carry_ref[0:1, :]
    rank1 = jnp.sum(jnp.where(o1, before, 0.0), axis=-1, keepdims=True)
    rank2 = jnp.sum(jnp.where(o2, before, 0.0), axis=-1, keepdims=True)
    total = carry_ref[0:1, :] + jnp.sum(onehot, axis=0, keepdims=True)
    carry_ref[...] = jnp.broadcast_to(total, carry_ref.shape)
    cnt_ref[0] = jnp.broadcast_to(total, (8, ROUTE_LANES))
    info = (comb + jnp.where(lane == ROUTE_INFO + 4, rank1, 0.0) + jnp.where(lane == ROUTE_INFO + 5, rank2, 0.0))
    info_ref[0] = jnp.transpose(info)[ROUTE_INFO:ROUTE_INFO + 8, :]


def _outproj_prompt(x2d, ylru2d, yatt2d, modp, glru, gattn, wout_bf16, ln1g, ln1b, wr, br, tm=512):
    n, d = x2d.shape
    per_seq = SEQ // tm
    full = lambda shp: pl.BlockSpec(shp, lambda i: tuple(0 for _ in shp))
    kern = functools.partial(_outproj_prompt_kernel, tm=tm, per_seq=per_seq)
    return pl.pallas_call(
        kern,
        grid=(n // tm,),
        in_specs=[pl.BlockSpec((tm, d), lambda i: (i, 0)),
                  pl.BlockSpec((tm, LRU_WIDTH), lambda i: (i, 0)),
                  pl.BlockSpec((tm, ATTN_WIDTH), lambda i: (i, 0)),
                  pl.BlockSpec((1, 6, d), lambda i: (i // per_seq, 0, 0)),
                  full((1, LRU_WIDTH)), full((1, ATTN_WIDTH)), full((d, d)),
                  full((1, d)), full((1, d)), full((d, ROUTE_LANES)), full((1, ROUTE_LANES))],
        out_specs=[pl.BlockSpec((tm, d), lambda i: (i, 0)),
                   pl.BlockSpec((1, 8, tm), lambda i: (i, 0, 0)),
                   pl.BlockSpec((1, 8, ROUTE_LANES), lambda i: (i // per_seq, 0, 0))],
        out_shape=[jax.ShapeDtypeStruct((n, d), F32),
                   jax.ShapeDtypeStruct((n // tm, 8, tm), F32),
                   jax.ShapeDtypeStruct((n // SEQ, 8, ROUTE_LANES), F32)],
        scratch_shapes=[pltpu.VMEM((tm, tm), BF16), pltpu.VMEM((8, ROUTE_LANES), F32)],
        compiler_params=_cparams(("arbitrary",)),
        name="prompt_outproj_ln_route",
    )(x2d, ylru2d, yatt2d, modp, glru, gattn, wout_bf16, ln1g, ln1b, wr, br)


def _moe_kernel(x1_ref, comb_ref, sh2_ref, sc2_ref, gt2_ref, wg_ref, wu_ref, wd_ref, ln2g_ref, ln2b_ref,
                o_ref, h2_ref, acc_ref):
    e = pl.program_id(1)

    @pl.when(e == 0)
    def _():
        h2_ref[...] = (x1_ref[...] * (1.0 + sc2_ref[...]) + sh2_ref[...]).astype(BF16)
        acc_ref[...] = jnp.zeros_like(acc_ref)

    h2 = h2_ref[...]
    a = jnp.dot(h2, wg_ref[0].astype(BF16), preferred_element_type=F32)
    u = jnp.dot(h2, wu_ref[0].astype(BF16), preferred_element_type=F32)
    comb = comb_ref[...]
    lane = lax.broadcasted_iota(jnp.int32, comb.shape, 1)
    c_e = jnp.sum(jnp.where(lane == e, comb, 0.0), axis=-1, keepdims=True)
    z = _silu(a) * u * c_e
    acc_ref[...] += jnp.dot(z.astype(BF16), wd_ref[0].astype(BF16), preferred_element_type=F32)

    @pl.when(e == N_EXPERTS - 1)
    def _():
        o_ref[...] = _layer_norm(DEEPNORM_ALPHA * x1_ref[...] + (1.0 + gt2_ref[...]) * acc_ref[...],
                                 ln2g_ref[...], ln2b_ref[...])


def _moe_dense(x1, comb, sh2, sc2, gt2, mod_rows_per_tile, w_gate, w_up, w_down, ln2g, ln2b, tm):
    n, d = x1.shape
    if mod_rows_per_tile:
        mspec = pl.BlockSpec((1, 1, d), lambda i, e: (i // mod_rows_per_tile, 0, 0))
        sh2, sc2, gt2 = (m.reshape(-1, 1, d) for m in (sh2, sc2, gt2))
        kern = lambda x1r, cr, s1, s2, s3, *rest: _moe_kernel(x1r, cr, s1.at[0], s2.at[0], s3.at[0], *rest)
    else:
        mspec = pl.BlockSpec((tm, d), lambda i, e: (i, 0))
        kern = _moe_kernel
    full = lambda shp: pl.BlockSpec(shp, lambda i, e: tuple(0 for _ in shp))
    return pl.pallas_call(
        kern,
        grid=(n // tm, N_EXPERTS),
        in_specs=[pl.BlockSpec((tm, d), lambda i, e: (i, 0)),
                  pl.BlockSpec((tm, ROUTE_LANES), lambda i, e: (i, 0)),
                  mspec, mspec, mspec,
                  pl.BlockSpec((1, d, D_EXPERT), lambda i, e: (e, 0, 0)),
                  pl.BlockSpec((1, d, D_EXPERT), lambda i, e: (e, 0, 0)),
                  pl.BlockSpec((1, D_EXPERT, d), lambda i, e: (e, 0, 0)),
                  full((1, d)), full((1, d))],
        out_specs=pl.BlockSpec((tm, d), lambda i, e: (i, 0)),
        out_shape=jax.ShapeDtypeStruct((n, d), F32),
        scratch_shapes=[pltpu.VMEM((tm, d), BF16), pltpu.VMEM((tm, d), F32)],
        compiler_params=_cparams(("arbitrary", "arbitrary")),
        name="moe_dense_ln",
    )(x1, comb, sh2, sc2, gt2, w_gate, w_up, w_down, ln2g, ln2b)


RB_SUB = 512
RB_NSUB = SEQ // RB_SUB
RB_CHUNK = 128
RB_CHUNK_BITS = 7
RB_NCHUNK = 2 * SEQ // RB_CHUNK
RB_PITCH = RB_CHUNK + 8
RB_SPITCH = RB_SUB + 8
RB_TOK_UNROLL = 8


def _rb_kernel(cnt_ref, x1_ref, mod_ref, ints_ref, wts_ref, wg_hbm, wu_hbm, wd_hbm, ln2g_ref, ln2b_ref,
               o_ref, buf_ref, stage_ref, ytmp_ref, wg_buf, wu_buf, wd_buf, start_ref, sem):
    b = pl.program_id(0)
    s = pl.program_id(1)

    def slab_start(p):
        return lax.shift_right_logical(p, RB_CHUNK_BITS) * (8 * RB_PITCH) + (p & (RB_CHUNK - 1))

    @pl.when(s == 0)
    def _starts():
        def body(e, run):
            start_ref[e] = run
            return run + cnt_ref[b, e]
        lax.fori_loop(0, N_EXPERTS, body, jnp.int32(0))

    @pl.when(s < RB_NSUB)
    def _dispatch():
        sh2 = mod_ref[0, 3:4, :]
        sc2 = mod_ref[0, 4:5, :]
        h2 = x1_ref[...] * (1.0 + sc2) + sh2
        for j in range(8):
            stage_ref[RB_SPITCH * j:RB_SPITCH * j + RB_SUB, :] = h2[:, LANES * j:LANES * (j + 1)]

        def body(i, carry):
            for k in range(RB_TOK_UNROLL):
                t = i * RB_TOK_UNROLL + k
                slab = stage_ref[pl.ds(t, 8, stride=RB_SPITCH), :]
                for a in range(2):
                    p = start_ref[ints_ref[0, a, t]] + ints_ref[0, 2 + a, t]
                    buf_ref[pl.ds(slab_start(p), 8, stride=RB_PITCH), :] = slab
            return carry
        lax.fori_loop(0, RB_SUB // RB_TOK_UNROLL, body, 0)

    @pl.when(s == RB_NSUB)
    def _experts():
        def copies(e, slot):
            return (pltpu.make_async_copy(wg_hbm.at[e], wg_buf.at[slot], sem.at[slot, 0]),
                    pltpu.make_async_copy(wu_hbm.at[e], wu_buf.at[slot], sem.at[slot, 1]),
                    pltpu.make_async_copy(wd_hbm.at[e], wd_buf.at[slot], sem.at[slot, 2]))

        def run_expert(e, slot):
            lo_row = start_ref[e]
            hi_row = lo_row + cnt_ref[b, e]

            def chunk_body(c, carry):
                base = pl.multiple_of(c * (8 * RB_PITCH), 8)
                lo = jnp.maximum(lo_row - c * RB_CHUNK, 0)
                hi = jnp.minimum(hi_row - c * RB_CHUNK, RB_CHUNK)
                x = jnp.concatenate(
                    [buf_ref[pl.ds(base + RB_PITCH * j, RB_CHUNK), :] for j in range(8)], axis=-1).astype(BF16)
                a = jnp.dot(x, wg_buf[slot], preferred_element_type=F32)
                u = jnp.dot(x, wu_buf[slot], preferred_element_type=F32)
                row = lax.broadcasted_iota(jnp.int32, (RB_CHUNK, 1), 0)
                z = jnp.where((row >= lo) & (row < hi), _silu(a) * u, 0.0).astype(BF16)
                y = jnp.dot(z, wd_buf[slot], preferred_element_type=F32)

                @pl.when(lo == 0)
                def _():
                    ytmp_ref[...] = y

                @pl.when(lo > 0)
                def _():
                    ytmp_ref[...] += y

                @pl.when(hi == RB_CHUNK)
                def _():
                    for j in range(8):
                        buf_ref[pl.ds(base + RB_PITCH * j, RB_CHUNK), :] = ytmp_ref[:, LANES * j:LANES * (j + 1)]
                return carry

            lax.fori_loop(lax.shift_right_logical(lo_row, RB_CHUNK_BITS),
                          lax.shift_right_logical(hi_row + (RB_CHUNK - 1), RB_CHUNK_BITS), chunk_body, 0)

        for c in copies(0, 0):
            c.start()

        def pair_body(i, carry):
            e0 = 2 * i
            for c in copies(e0 + 1, 1):
                c.start()
            for c in copies(e0, 0):
                c.wait()
            run_expert(e0, 0)

            @pl.when(i < N_EXPERTS // 2 - 1)
            def _():
                for c in copies(e0 + 2, 0):
                    c.start()
            for c in copies(e0 + 1, 1):
                c.wait()
            run_expert(e0 + 1, 1)
            return carry
        lax.fori_loop(0, N_EXPERTS // 2, pair_body, 0)

    @pl.when(s > RB_NSUB)
    def _combine():
        def body(i, carry):
            for k in range(RB_TOK_UNROLL):
                t = i * RB_TOK_UNROLL + k
                acc = None
                for a in range(2):
                    p = start_ref[ints_ref[0, a, t]] + ints_ref[0, 2 + a, t]
                    term = wts_ref[0, a, t] * buf_ref[pl.ds(slab_start(p), 8, stride=RB_PITCH), :]
                    acc = term if acc is None else acc + term
                stage_ref[pl.ds(t, 8, stride=RB_SPITCH), :] = acc
            return carry
        lax.fori_loop(0, RB_SUB // RB_TOK_UNROLL, body, 0)
        gt2 = mod_ref[0, 5:6, :]
        f = jnp.concatenate([stage_ref[RB_SPITCH * j:RB_SPITCH * j + RB_SUB, :] for j in range(8)], axis=-1)
        o_ref[...] = _layer_norm(DEEPNORM_ALPHA * x1_ref[...] + (1.0 + gt2) * f, ln2g_ref[...], ln2b_ref[...])


def _rb_moe(x1, modp, cnt, ints, wts, wg_bf16, wu_bf16, wd_bf16, ln2g, ln2b):
    n, d = x1.shape
    bsz = n // SEQ
    nsteps = 2 * RB_NSUB + 1

    def sub_index(s):
        return jnp.where(s < RB_NSUB, s, jnp.where(s == RB_NSUB, RB_NSUB - 1, s - RB_NSUB - 1))

    def tile_map(b, s, cnt_r):
        return (b * RB_NSUB + sub_index(s), 0)

    def tile_map3(b, s, cnt_r):
        return (b * RB_NSUB + sub_index(s), 0, 0)

    def out_map(b, s, cnt_r):
        return (b * RB_NSUB + jnp.maximum(s - RB_NSUB - 1, 0), 0)

    const = lambda shp: pl.BlockSpec(shp, lambda b, s, cnt_r: tuple(0 for _ in shp))
    anyspec = pl.BlockSpec(memory_space=pl.ANY)
    grid_spec = pltpu.PrefetchScalarGridSpec(
        num_scalar_prefetch=1,
        grid=(bsz, nsteps),
        in_specs=[pl.BlockSpec((RB_SUB, d), tile_map),
                  pl.BlockSpec((1, 6, d), lambda b, s, cnt_r: (b, 0, 0)),
                  pl.BlockSpec((1, 4, RB_SUB), tile_map3, memory_space=pltpu.SMEM),
                  pl.BlockSpec((1, 2, RB_SUB), tile_map3, memory_space=pltpu.SMEM),
                  anyspec, anyspec, anyspec,
                  const((1, d)), const((1, d))],
        out_specs=pl.BlockSpec((RB_SUB, d), out_map),
        scratch_shapes=[pltpu.VMEM((RB_NCHUNK * 8 * RB_PITCH, LANES), F32),
                        pltpu.VMEM((8 * RB_SPITCH, LANES), F32),
                        pltpu.VMEM((RB_CHUNK, d), F32),
                        pltpu.VMEM((2, d, D_EXPERT), BF16),
                        pltpu.VMEM((2, d, D_EXPERT), BF16),
                        pltpu.VMEM((2, D_EXPERT, d), BF16),
                        pltpu.SMEM((N_EXPERTS,), jnp.int32),
                        pltpu.SemaphoreType.DMA((2, 3))])
    return pl.pallas_call(
        _rb_kernel,
        grid_spec=grid_spec,
        out_shape=jax.ShapeDtypeStruct((n, d), F32),
        compiler_params=_cparams(("arbitrary", "arbitrary")),
        name="moe_routed_ln",
    )(cnt, x1, modp, ints, wts, wg_bf16, wu_bf16, wd_bf16, ln2g, ln2b)


def _sample_in_kernel(x_ref, sh1_ref, sc1_ref, win_ref, ctx_ref, h0_ref, convw_ref, convb_ref,
                      wlo_ref, whi_ref, bgate_ref, lam_ref,
                      ylru_ref, q_ref, k_ref, v_ref, cstate_ref, hnew_ref):
    h = x_ref[...] * (1.0 + sc1_ref[...]) + sh1_ref[...]
    z = _dot(h, win_ref[...], True)
    xb = z[:, :LRU_WIDTH]
    gate = z[:, LRU_WIDTH:2 * LRU_WIDTH]
    c0 = ctx_ref[:, 0, :]
    c1 = ctx_ref[:, 1, :]
    c2 = ctx_ref[:, 2, :]
    xc = (convb_ref[...] + convw_ref[0:1, :] * c0 + convw_ref[1:2, :] * c1
          + convw_ref[2:3, :] * c2 + convw_ref[3:4, :] * xb)
    cstate_ref[:, 0, :] = c1
    cstate_ref[:, 1, :] = c2
    cstate_ref[:, 2, :] = xb
    sp = _softplus(-lam_ref[...])
    a, bterm = _lru_gates(xc, wlo_ref[...], whi_ref[...], bgate_ref[...], sp, True)
    hn = a * h0_ref[...] + bterm
    hnew_ref[...] = hn
    ylru_ref[...] = hn * _gelu_tanh(gate)
    low = lax.broadcasted_iota(jnp.int32, (DEC_BATCH, LANES), 1) < HEAD_DIM
    for c in range(4):
        qc = z[:, 2 * LRU_WIDTH + LANES * c:2 * LRU_WIDTH + LANES * (c + 1)]
        q_ref[pl.ds(c, DEC_BATCH, stride=N_HEADS), :] = jnp.where(low, qc, 0.0)
        q_ref[pl.ds(c + 4, DEC_BATCH, stride=N_HEADS), :] = jnp.where(low, 0.0, qc)
    k_ref[...] = z[:, 2 * LRU_WIDTH + ATTN_WIDTH:2 * LRU_WIDTH + ATTN_WIDTH + KV_WIDTH]
    v_ref[...] = z[:, 2 * LRU_WIDTH + ATTN_WIDTH + KV_WIDTH:]


def _sample_in(x, sh1, sc1, w_in_p, ctx, h0, conv_w, conv_b, wlo, whi, bgate, lam):
    n = DEC_BATCH
    outs = [jax.ShapeDtypeStruct((n, LRU_WIDTH), F32),
            jax.ShapeDtypeStruct((n * N_HEADS, LANES), F32),
            jax.ShapeDtypeStruct((n, KV_WIDTH), F32),
            jax.ShapeDtypeStruct((n, KV_WIDTH), F32),
            jax.ShapeDtypeStruct((n, CONV_WIDTH - 1, LRU_WIDTH), F32),
            jax.ShapeDtypeStruct((n, LRU_WIDTH), F32)]
    return pl.pallas_call(
        _sample_in_kernel,
        out_shape=outs,
        compiler_params=pltpu.CompilerParams(vmem_limit_bytes=VMEM_LIMIT),
        name="sample_inproj_rglru",
    )(x, sh1, sc1, w_in_p, ctx, h0, conv_w, conv_b, wlo, whi, bgate, lam)


def _sample_attn_kernel(q_ref, kn_ref, vn_ref, ck_ref, cv_ref, sink_ref, y_ref, nk_ref, nv_ref, *, bb):
    rows = lax.broadcasted_iota(jnp.int32, (WINDOW, KV_WIDTH), 0)
    sink = sink_ref[...]
    for b in range(bb):
        q8 = q_ref[b]
        kb = ck_ref[b]
        vb = cv_ref[b]
        kn = kn_ref[b:b + 1, :]
        vn = vn_ref[b:b + 1, :]
        s = _dot_nt(q8, kb, True) * ATTN_SCALE
        s_self = jnp.sum(q8 * kn, axis=-1, keepdims=True) * ATTN_SCALE
        m = jnp.maximum(jnp.maximum(jnp.max(s, axis=-1, keepdims=True), s_self), sink)
        e = jnp.exp(s - m)
        e_self = jnp.exp(s_self - m)
        den = jnp.sum(e, axis=-1, keepdims=True) + e_self + jnp.exp(sink - m)
        inv = 1.0 / den
        y_ref[b] = _dot(e * inv, vb, True) + (e_self * inv) * vn
        nk_ref[b] = jnp.where(rows == WINDOW - 1, kn, pltpu.roll(kb, WINDOW - 1, axis=0))
        nv_ref[b] = jnp.where(rows == WINDOW - 1, vn, pltpu.roll(vb, WINDOW - 1, axis=0))


def _sample_attn(q3, kn, vn, cache_k, cache_v, sinks, bb=8):
    n = DEC_BATCH
    kern = functools.partial(_sample_attn_kernel, bb=bb)
    return pl.pallas_call(
        kern,
        grid=(n // bb,),
        in_specs=[pl.BlockSpec((bb, N_HEADS, LANES), lambda i: (i, 0, 0)),
                  pl.BlockSpec((bb, KV_WIDTH), lambda i: (i, 0)),
                  pl.BlockSpec((bb, KV_WIDTH), lambda i: (i, 0)),
                  pl.BlockSpec((bb, WINDOW, KV_WIDTH), lambda i: (i, 0, 0)),
                  pl.BlockSpec((bb, WINDOW, KV_WIDTH), lambda i: (i, 0, 0)),
                  pl.BlockSpec((N_HEADS, 1), lambda i: (0, 0))],
        out_specs=[pl.BlockSpec((bb, N_HEADS, LANES), lambda i: (i, 0, 0)),
                   pl.BlockSpec((bb, WINDOW, KV_WIDTH), lambda i: (i, 0, 0)),
                   pl.BlockSpec((bb, WINDOW, KV_WIDTH), lambda i: (i, 0, 0))],
        out_shape=[jax.ShapeDtypeStruct((n, N_HEADS, LANES), F32),
                   jax.ShapeDtypeStruct((n, WINDOW, KV_WIDTH), F32),
                   jax.ShapeDtypeStruct((n, WINDOW, KV_WIDTH), F32)],
        compiler_params=_cparams(("arbitrary",)),
        name="sample_cache_attention",
    )(q3, kn, vn, cache_k, cache_v, sinks.reshape(N_HEADS, 1))


def _sample_out_kernel(x_ref, ylru_ref, yatt_ref, sh2_ref, sc2_ref, gt1_ref, glru_ref, gattn_ref, wout_ref,
                       ln1g_ref, ln1b_ref, wr_ref, br_ref, x1_ref, comb_ref):
    low = lax.broadcasted_iota(jnp.int32, (DEC_BATCH, LANES), 1) < HEAD_DIM
    yatt = jnp.concatenate(
        [jnp.where(low, yatt_ref[pl.ds(c, DEC_BATCH, stride=N_HEADS), :],
                   yatt_ref[pl.ds(c + 4, DEC_BATCH, stride=N_HEADS), :]) for c in range(4)], axis=-1)
    x1, comb = _outproj_body(x_ref[...], ylru_ref[...], yatt, sh2_ref[...], sc2_ref[...], gt1_ref[...],
                             glru_ref[...], gattn_ref[...], wout_ref[...], ln1g_ref[...], ln1b_ref[...],
                             wr_ref[...], br_ref[...], True)
    x1_ref[...] = x1
    comb_ref[...] = comb


def _sample_out(x, ylru, yatt2d, sh2, sc2, gt1, glru, gattn, wout_p, ln1g, ln1b, wr, br):
    n = DEC_BATCH
    return pl.pallas_call(
        _sample_out_kernel,
        out_shape=[jax.ShapeDtypeStruct((n, D_MODEL), F32), jax.ShapeDtypeStruct((n, ROUTE_LANES), F32)],
        compiler_params=pltpu.CompilerParams(vmem_limit_bytes=VMEM_LIMIT),
        name="sample_outproj_ln_route",
    )(x, ylru, yatt2d, sh2, sc2, gt1, glru, gattn, wout_p, ln1g, ln1b, wr, br)


def _block_diag_halves(w_a, w_x):
    def bd(w4):
        eye = jnp.eye(4, dtype=w4.dtype)
        return (w4[:, :, None, :] * eye[:, None, :, None]).reshape(256, 256)
    lo = jnp.concatenate([bd(w_a[:4]), bd(w_x[:4])], axis=1)
    hi = jnp.concatenate([bd(w_a[4:]), bd(w_x[4:])], axis=1)
    return lo, hi


def kernel(x_prompt, x_sample, c_prompt, c_sample, state_conv, state_h, cache_k, cache_v, w_ada, b_ada, w_in,
           conv_w, conv_b, w_rg_a, b_rg_a, w_rg_x, b_rg_x, lru_lambda, sinks, g_lru, g_attn, w_out, ln1_g, ln1_b,
           w_group, b_group, w_router, b_router, w_gate, w_up, w_down, ln2_g, ln2_b):
    d = D_MODEL
    perm = jnp.asarray(HEAD_PERM)
    w_in0 = w_in[0]
    q0 = 2 * LRU_WIDTH
    w_in_p = jnp.concatenate([w_in0[:, :q0], w_in0[:, q0:q0 + ATTN_WIDTH][:, perm], w_in0[:, q0 + ATTN_WIDTH:]],
                             axis=1)
    w_out0 = w_out[0]
    w_out_p = jnp.concatenate([w_out0[:LRU_WIDTH], w_out0[LRU_WIDTH:][perm]], axis=0)
    g_attn_p = g_attn[0][perm].reshape(1, -1)
    glru = g_lru[0].reshape(1, -1)
    wlo, whi = _block_diag_halves(w_rg_a[0], w_rg_x[0])
    bgate = jnp.concatenate([b_rg_a[0].reshape(-1), b_rg_x[0].reshape(-1)]).reshape(1, -1)
    lam = lru_lambda[0].reshape(1, -1)
    convw = conv_w[0]
    convb = conv_b[0].reshape(1, -1)
    ln1g, ln1b = ln1_g[0].reshape(1, -1), ln1_b[0].reshape(1, -1)
    ln2g, ln2b = ln2_g[0].reshape(1, -1), ln2_b[0].reshape(1, -1)
    wr = jnp.concatenate([jnp.transpose(w_router[0], (1, 0, 2)).reshape(d, N_EXPERTS), w_group[0],
                          jnp.zeros((d, ROUTE_LANES - N_EXPERTS - N_GROUPS), F32)], axis=1)
    br = jnp.concatenate([b_router[0].reshape(-1), b_group[0],
                          jnp.zeros((ROUTE_LANES - N_EXPERTS - N_GROUPS,), F32)]).reshape(1, -1)
    sink_p = sinks[0]

    c_all = jnp.concatenate([c_prompt, jnp.zeros((8 - BATCH, d), F32), c_sample], axis=0)
    mod = _ada(c_all, w_ada[0], b_ada[0])
    modp = mod[:BATCH].reshape(BATCH, 6, d)
    mods = mod[8:]
    sh1_s, sc1_s, gt1_s, sh2_s, sc2_s, gt2_s = (mods[:, k * d:(k + 1) * d] for k in range(6))

    zin = _inproj(x_prompt, modp, w_in_p.astype(BF16))
    ylru, cstate8, hlast8 = _lru(zin, convw, convb, wlo.astype(BF16), whi.astype(BF16), bgate, lam)
    yatt = _attn(zin, sink_p)
    n_p = BATCH * SEQ
    x1_p, info, cntf = _outproj_prompt(x_prompt.reshape(n_p, d), ylru.reshape(n_p, LRU_WIDTH),
                                       yatt.reshape(n_p, ATTN_WIDTH), modp, glru, g_attn_p, w_out_p.astype(BF16),
                                       ln1g, ln1b, wr, br, tm=RB_SUB)
    ints = jnp.concatenate([info[:, 0:2], info[:, 4:6]], axis=1).astype(jnp.int32)
    wts = info[:, 2:4]
    cnt = cntf[:, 0, :N_EXPERTS].astype(jnp.int32)
    y_p = _rb_moe(x1_p, modp, cnt, ints, wts, w_gate[0].astype(BF16), w_up[0].astype(BF16),
                  w_down[0].astype(BF16), ln2g, ln2b)

    ylru_s, q2d, kn, vn, cstate_s, hnew_s = _sample_in(
        x_sample.reshape(DEC_BATCH, d), sh1_s, sc1_s, w_in_p, state_conv[0], state_h[0],
        convw, convb, wlo, whi, bgate, lam)
    yatt3, newk, newv = _sample_attn(q2d.reshape(DEC_BATCH, N_HEADS, LANES), kn, vn,
                                     cache_k[0].reshape(DEC_BATCH, WINDOW, KV_WIDTH),
                                     cache_v[0].reshape(DEC_BATCH, WINDOW, KV_WIDTH), sink_p)
    x1_s, comb_s = _sample_out(x_sample.reshape(DEC_BATCH, d), ylru_s, yatt3.reshape(DEC_BATCH * N_HEADS, LANES),
                               sh2_s, sc2_s, gt1_s, glru, g_attn_p, w_out_p, ln1g, ln1b, wr, br)
    y_s = _moe_dense(x1_s, comb_s, sh2_s, sc2_s, gt2_s, 0, w_gate[0], w_up[0], w_down[0], ln2g, ln2b, DEC_BATCH)

    kq = 2 * LRU_WIDTH + ATTN_WIDTH
    return (y_p.reshape(BATCH, SEQ, d),
            y_s.reshape(DEC_BATCH, 1, d),
            cstate8[:, 5:8][None],
            hlast8[:, 7][None],
            zin[:, SEQ - WINDOW:, kq:kq + KV_WIDTH].reshape(1, BATCH, WINDOW, N_KV_HEADS, HEAD_DIM),
            zin[:, SEQ - WINDOW:, kq + KV_WIDTH:].reshape(1, BATCH, WINDOW, N_KV_HEADS, HEAD_DIM),
            cstate_s[None],
            hnew_s[None],
            newk.reshape(1, DEC_BATCH, WINDOW, N_KV_HEADS, HEAD_DIM),
            newv.reshape(1, DEC_BATCH, WINDOW, N_KV_HEADS, HEAD_DIM))
```

```python
import functools

import jax
import jax.numpy as jnp
import numpy as np
from jax import lax
from jax.experimental import pallas as pl
from jax.experimental.pallas import tpu as pltpu

F32 = jnp.float32
BF16 = jnp.bfloat16
HIGHEST = lax.Precision.HIGHEST

D_MODEL = 1024
BATCH = 4
SEQ = 4096
DEC_BATCH = 128
LRU_WIDTH = 512
LRU_BLOCKS = 8
LRU_BLOCK = 64
CONV_WIDTH = 4
LRU_C = 8.0
N_HEADS = 8
N_KV_HEADS = 2
HEAD_DIM = 64
ATTN_WIDTH = 512
KV_WIDTH = 128
WINDOW = 128
IN_WIDTH = 2 * LRU_WIDTH + ATTN_WIDTH + 2 * KV_WIDTH
N_GROUPS = 4
EXPERTS_PER_GROUP = 8
N_EXPERTS = 32
D_EXPERT = 256
DEEPNORM_ALPHA = 2.0 ** 0.25
LN_EPS = 1e-5
RMS_EPS = 1e-6
ATTN_SCALE = HEAD_DIM ** -0.5

LANES = 128
ROUTE_LANES = 128
ROUTE_INFO = 40
VMEM_LIMIT = 56 * 1024 * 1024

HEAD_PERM = np.concatenate(
    [np.concatenate([np.arange(64 * c, 64 * c + 64), np.arange(64 * (c + 4), 64 * (c + 4) + 64)])
     for c in range(4)])


def _cparams(sem):
    return pltpu.CompilerParams(dimension_semantics=sem, vmem_limit_bytes=VMEM_LIMIT)


def _dot(a, b, exact):
    if exact:
        return jnp.dot(a, b, precision=HIGHEST, preferred_element_type=F32)
    return jnp.dot(a.astype(BF16), b.astype(BF16), preferred_element_type=F32)


def _dot_nt(a, b, exact):
    dn = (((1,), (1,)), ((), ()))
    if exact:
        return lax.dot_general(a, b, dn, precision=HIGHEST, preferred_element_type=F32)
    return lax.dot_general(a.astype(BF16), b.astype(BF16), dn, preferred_element_type=F32)


def _sigmoid(x):
    return 1.0 / (1.0 + jnp.exp(-x))


def _silu(x):
    return x * _sigmoid(x)


def _gelu_tanh(x):
    return 0.5 * x * (1.0 + jnp.tanh(np.sqrt(2.0 / np.pi).astype(np.float32) * (x + 0.044715 * (x * x * x))))


def _softplus(x):
    return jnp.maximum(x, 0.0) + jnp.log1p(jnp.exp(-jnp.abs(x)))


def _layer_norm(x, g, b):
    mu = jnp.mean(x, axis=-1, keepdims=True)
    xc = x - mu
    var = jnp.mean(xc * xc, axis=-1, keepdims=True)
    return xc * lax.rsqrt(var + LN_EPS) * g + b


def _rms_norm(x, g):
    return x * lax.rsqrt(jnp.mean(x * x, axis=-1, keepdims=True) + RMS_EPS) * g


def _ada_kernel(c_ref, w_ref, b_ref, o_ref):
    o_ref[...] = _dot(_silu(c_ref[...]), w_ref[...], True) + b_ref[...]


def _ada(c_all, w_ada, b_ada):
    rows = c_all.shape[0]
    bn = 512
    return pl.pallas_call(
        _ada_kernel,
        grid=(6 * D_MODEL // bn,),
        in_specs=[pl.BlockSpec((rows, D_MODEL), lambda j: (0, 0)),
                  pl.BlockSpec((D_MODEL, bn), lambda j: (0, j)),
                  pl.BlockSpec((1, bn), lambda j: (0, j))],
        out_specs=pl.BlockSpec((rows, bn), lambda j: (0, j)),
        out_shape=jax.ShapeDtypeStruct((rows, 6 * D_MODEL), F32),
        compiler_params=_cparams(("arbitrary",)),
        name="ada_modulation",
    )(c_all, w_ada, b_ada.reshape(1, -1))


def _inproj_kernel(x_ref, mod_ref, w_ref, o_ref):
    sh1 = mod_ref[0, 0:1, :]
    sc1 = mod_ref[0, 1:2, :]
    h = x_ref[0] * (1.0 + sc1) + sh1
    o_ref[0] = _dot(h, w_ref[...], False)


def _inproj(x, modp, w_in_bf16, tm=512):
    b, t, d = x.shape
    return pl.pallas_call(
        _inproj_kernel,
        grid=(b, t // tm),
        in_specs=[pl.BlockSpec((1, tm, d), lambda i, j: (i, j, 0)),
                  pl.BlockSpec((1, 6, d), lambda i, j: (i, 0, 0)),
                  pl.BlockSpec((d, IN_WIDTH), lambda i, j: (0, 0))],
        out_specs=pl.BlockSpec((1, tm, IN_WIDTH), lambda i, j: (i, j, 0)),
        out_shape=jax.ShapeDtypeStruct((b, t, IN_WIDTH), F32),
        compiler_params=_cparams(("arbitrary", "arbitrary")),
        name="prompt_inproj",
    )(x, modp, w_in_bf16)


def _lru_gates(xc, wlo, whi, bgate, sp_neg_lam, exact):
    g_lo = _dot(xc[:, :256], wlo, exact)
    g_hi = _dot(xc[:, 256:], whi, exact)
    ga = jnp.concatenate([g_lo[:, :256], g_hi[:, :256]], axis=-1) + bgate[:, :LRU_WIDTH]
    gx = jnp.concatenate([g_lo[:, 256:], g_hi[:, 256:]], axis=-1) + bgate[:, LRU_WIDTH:]
    r = _sigmoid(ga)
    i = _sigmoid(gx)
    log_a = -LRU_C * r * sp_neg_lam
    a = jnp.exp(log_a)
    one_minus_a2 = -jnp.tanh(log_a) * (a * a + 1.0)
    bterm = jnp.sqrt(one_minus_a2) * (i * xc)
    return a, bterm


def _lru_kernel(z_ref, convw_ref, convb_ref, wlo_ref, whi_ref, bgate_ref, lam_ref,
                y_ref, cstate_ref, hlast_ref, tail_ref, carry_ref, *, tl):
    j = pl.program_id(1)

    @pl.when(j == 0)
    def _():
        tail_ref[...] = jnp.zeros_like(tail_ref)
        carry_ref[...] = jnp.zeros_like(carry_ref)

    xb = z_ref[0, :, :LRU_WIDTH]
    gate = z_ref[0, :, LRU_WIDTH:]
    rows = lax.broadcasted_iota(jnp.int32, (tl, LRU_WIDTH), 0)

    xc = convb_ref[...] + convw_ref[3:4, :] * xb
    rows8 = lax.broadcasted_iota(jnp.int32, (8, LRU_WIDTH), 0)
    tail = tail_ref[...]
    for back in (1, 2, 3):
        rolled = pltpu.roll(xb, back, axis=0)
        top = jnp.where(rows8 >= back, rolled[:8], pltpu.roll(tail, back, axis=0))
        shifted = jnp.concatenate([top, rolled[8:]], axis=0)
        xc = xc + convw_ref[3 - back:4 - back, :] * shifted
    tail_ref[...] = xb[tl - 8:, :]
    cstate_ref[0] = xb[tl - 8:, :]

    sp = _softplus(-lam_ref[...])
    a, bterm = _lru_gates(xc, wlo_ref[...], whi_ref[...], bgate_ref[...], sp, False)

    s = 1
    while s < tl:
        a_sh = jnp.where(rows >= s, pltpu.roll(a, s, axis=0), 1.0)
        b_sh = jnp.where(rows >= s, pltpu.roll(bterm, s, axis=0), 0.0)
        bterm = a * b_sh + bterm
        a = a * a_sh
        s *= 2
    h = a * carry_ref[7:8, :] + bterm
    carry_ref[...] = h[tl - 8:, :]
    hlast_ref[0] = h[tl - 8:, :]
    y_ref[0] = h * _gelu_tanh(gate)


def _lru(zin, conv_w, conv_b, wlo, whi, bgate, lam, tl=512):
    b, t, _ = zin.shape
    kern = functools.partial(_lru_kernel, tl=tl)
    full = lambda shp: pl.BlockSpec(shp, lambda i, j: tuple(0 for _ in shp))
    return pl.pallas_call(
        kern,
        grid=(b, t // tl),
        in_specs=[pl.BlockSpec((1, tl, 2 * LRU_WIDTH), lambda i, j: (i, j, 0)),
                  full((CONV_WIDTH, LRU_WIDTH)), full((1, LRU_WIDTH)),
                  full((256, 512)), full((256, 512)), full((1, 2 * LRU_WIDTH)), full((1, LRU_WIDTH))],
        out_specs=[pl.BlockSpec((1, tl, LRU_WIDTH), lambda i, j: (i, j, 0)),
                   pl.BlockSpec((1, 8, LRU_WIDTH), lambda i, j: (i, 0, 0)),
                   pl.BlockSpec((1, 8, LRU_WIDTH), lambda i, j: (i, 0, 0))],
        out_shape=[jax.ShapeDtypeStruct((b, t, LRU_WIDTH), F32),
                   jax.ShapeDtypeStruct((b, 8, LRU_WIDTH), F32),
                   jax.ShapeDtypeStruct((b, 8, LRU_WIDTH), F32)],
        scratch_shapes=[pltpu.VMEM((8, LRU_WIDTH), F32), pltpu.VMEM((8, LRU_WIDTH), F32)],
        compiler_params=_cparams(("arbitrary", "arbitrary")),
        name="prompt_rglru",
    )(zin, conv_w, conv_b, wlo, whi, bgate, lam)


def _attn_kernel(q_ref, k_ref, v_ref, sink_ref, o_ref, kprev_ref, vprev_ref):
    j = pl.program_id(1)

    @pl.when(j == 0)
    def _():
        kprev_ref[...] = jnp.zeros_like(kprev_ref)
        vprev_ref[...] = jnp.zeros_like(vprev_ref)

    blk = WINDOW
    lane = lax.broadcasted_iota(jnp.int32, (blk, LANES), 1)
    low = lane < HEAD_DIM
    q = q_ref[0]
    pieces = []
    for half in (0, 1):
        for c in range(4):
            qc = q[:, LANES * c:LANES * (c + 1)]
            pieces.append(jnp.where(low if half == 0 else ~low, qc, 0.0).astype(BF16))
    q8 = jnp.concatenate(pieces, axis=0)
    k_cur = k_ref[0]
    v_cur = v_ref[0]
    k_band = jnp.concatenate([kprev_ref[...], k_cur], axis=0)
    v_band = jnp.concatenate([vprev_ref[...], v_cur], axis=0)
    s = _dot_nt(q8, k_band, False) * ATTN_SCALE
    s = s.reshape(N_HEADS, blk, 2 * blk)
    qi = lax.broadcasted_iota(jnp.int32, (blk, 2 * blk), 0)
    sj = lax.broadcasted_iota(jnp.int32, (blk, 2 * blk), 1)
    rel = blk + qi - sj
    valid = (rel >= 0) & (rel <= WINDOW) & ((sj >= blk) | (j > 0))
    s = jnp.where(valid[None], s, -jnp.inf)
    sink = sink_ref[...].reshape(N_HEADS, blk, 1)
    m = jnp.maximum(jnp.max(s, axis=-1, keepdims=True), sink)
    e = jnp.exp(s - m)
    den = jnp.sum(e, axis=-1, keepdims=True) + jnp.exp(sink - m)
    p = (e * (1.0 / den)).reshape(N_HEADS * blk, 2 * blk)
    o8 = _dot(p, v_band, False)
    cols = []
    for c in range(4):
        cols.append(jnp.where(low, o8[blk * c:blk * (c + 1)], o8[blk * (c + 4):blk * (c + 5)]))
    o_ref[0] = jnp.concatenate(cols, axis=-1)
    kprev_ref[...] = k_cur
    vprev_ref[...] = v_cur


def _attn(zin, sinks):
    b, t, _ = zin.shape
    blk = WINDOW
    sink_col = jnp.repeat(sinks.astype(F32), blk).reshape(N_HEADS * blk, 1)
    return pl.pallas_call(
        _attn_kernel,
        grid=(b, t // blk),
        in_specs=[pl.BlockSpec((1, blk, ATTN_WIDTH), lambda i, j: (i, j, 2)),
                  pl.BlockSpec((1, blk, KV_WIDTH), lambda i, j: (i, j, 12)),
                  pl.BlockSpec((1, blk, KV_WIDTH), lambda i, j: (i, j, 13)),
                  pl.BlockSpec((N_HEADS * blk, 1), lambda i, j: (0, 0))],
        out_specs=pl.BlockSpec((1, blk, ATTN_WIDTH), lambda i, j: (i, j, 0)),
        out_shape=jax.ShapeDtypeStruct((b, t, ATTN_WIDTH), F32),
        scratch_shapes=[pltpu.VMEM((blk, KV_WIDTH), F32), pltpu.VMEM((blk, KV_WIDTH), F32)],
        compiler_params=_cparams(("arbitrary", "arbitrary")),
        name="prompt_window_attention",
    )(zin, zin, zin, sink_col)


def _route(h2, wr, br, exact_unused=True):
    t = h2.shape[0]
    logits = _dot(h2, wr, True) + br
    lane = lax.broadcasted_iota(jnp.int32, (t, ROUTE_LANES), 1).astype(F32)
    neg = -jnp.inf
    big = float(ROUTE_LANES)
    is_g = (lane >= N_EXPERTS) & (lane < N_EXPERTS + N_GROUPS)
    lg = jnp.where(is_g, logits, neg)
    mg = jnp.max(lg, axis=-1, keepdims=True)
    eg = jnp.where(is_g, jnp.exp(lg - mg), 0.0)
    pg = eg / jnp.sum(eg, axis=-1, keepdims=True)
    g_val = jnp.max(pg, axis=-1, keepdims=True)
    g_lane = jnp.min(jnp.where((pg == g_val) & is_g, lane, big), axis=-1, keepdims=True)
    g_idx = g_lane - N_EXPERTS
    in_grp = (lane >= g_idx * EXPERTS_PER_GROUP) & (lane < (g_idx + 1.0) * EXPERTS_PER_GROUP)
    le = jnp.where(in_grp, logits, neg)
    me = jnp.max(le, axis=-1, keepdims=True)
    ee = jnp.where(in_grp, jnp.exp(le - me), 0.0)
    pe = ee / jnp.sum(ee, axis=-1, keepdims=True)
    v1 = jnp.max(pe, axis=-1, keepdims=True)
    l1 = jnp.min(jnp.where((pe == v1) & in_grp, lane, big), axis=-1, keepdims=True)
    rest = in_grp & (lane != l1)
    pe2 = jnp.where(rest, pe, -1.0)
    v2 = jnp.max(pe2, axis=-1, keepdims=True)
    l2 = jnp.min(jnp.where((pe2 == v2) & rest, lane, big), axis=-1, keepdims=True)
    tot = v1 + v2
    w1 = g_val * v1 / tot
    w2 = g_val * v2 / tot
    comb = jnp.where(lane == l1, w1, 0.0) + jnp.where(lane == l2, w2, 0.0)
    return (comb + jnp.where(lane == ROUTE_INFO, l1, 0.0) + jnp.where(lane == ROUTE_INFO + 1, l2, 0.0)
            + jnp.where(lane == ROUTE_INFO + 2, w1, 0.0) + jnp.where(lane == ROUTE_INFO + 3, w2, 0.0))


def _outproj_body(x, ylru, yatt, sh2, sc2, gt1, glru, gattn, wout, ln1g, ln1b, wr, br, exact):
    mixin = jnp.concatenate([_rms_norm(ylru, glru), _rms_norm(yatt, gattn)], axis=-1)
    mix = _dot(mixin, wout, exact)
    x1 = _layer_norm(DEEPNORM_ALPHA * x + (1.0 + gt1) * mix, ln1g, ln1b)
    h2 = x1 * (1.0 + sc2) + sh2
    return x1, _route(h2, wr, br)


def _outproj_prompt_kernel(x_ref, ylru_ref, yatt_ref, mod_ref, glru_ref, gattn_ref, wout_ref,
                           ln1g_ref, ln1b_ref, wr_ref, br_ref, x1_ref, info_ref, cnt_ref, tri_ref, carry_ref,
                           *, tm, per_seq):
    i = pl.program_id(0)

    @pl.when(i == 0)
    def _():
        r = lax.broadcasted_iota(jnp.int32, (tm, tm), 0)
        c = lax.broadcasted_iota(jnp.int32, (tm, tm), 1)
        tri_ref[...] = jnp.where(c < r, 1.0, 0.0).astype(BF16)

    @pl.when(i % per_seq == 0)
    def _():
        carry_ref[...] = jnp.zeros_like(carry_ref)

    gt1 = mod_ref[0, 2:3, :]
    sh2 = mod_ref[0, 3:4, :]
    sc2 = mod_ref[0, 4:5, :]
    x1, comb = _outproj_body(x_ref[...], ylru_ref[...], yatt_ref[...], sh2, sc2, gt1,
                             glru_ref[...], gattn_ref[...], wout_ref[...], ln1g_ref[...], ln1b_ref[...],
                             wr_ref[...], br_ref[...], False)
    x1_ref[...] = x1
    lane = lax.broadcasted_iota(jnp.int32, (tm, ROUTE_LANES), 1).astype(F32)
    l1 = jnp.sum(jnp.where(lane == ROUTE_INFO, comb, 0.0), axis=-1, keepdims=True)
    l2 = jnp.sum(jnp.where(lane == ROUTE_INFO + 1, comb, 0.0), axis=-1, keepdims=True)
    o1 = lane == l1
    o2 = lane == l2
    onehot = jnp.where(o1 | o2, 1.0, 0.0)
    before = jnp.dot(tri_ref[...], onehot.astype(BF16), preferred_element_type=F32) + carry_ref[0:1, :]
    rank1 = jnp.sum(jnp.where(o1, before, 0.0), axis=-1, keepdims=True)
    rank2 = jnp.sum(jnp.where(o2, before, 0.0), axis=-1, keepdims=True)
    total = carry_ref[0:1, :] + jnp.sum(onehot, axis=0, keepdims=True)
    carry_ref[...] = jnp.broadcast_to(total, carry_ref.shape)
    cnt_ref[0] = jnp.broadcast_to(total, (8, ROUTE_LANES))
    info = (comb + jnp.where(lane == ROUTE_INFO + 4, rank1, 0.0) + jnp.where(lane == ROUTE_INFO + 5, rank2, 0.0))
    info_ref[0] = jnp.transpose(info)[ROUTE_INFO:ROUTE_INFO + 8, :]


def _outproj_prompt(x2d, ylru2d, yatt2d, modp, glru, gattn, wout_bf16, ln1g, ln1b, wr, br, tm=512):
    n, d = x2d.shape
    per_seq = SEQ // tm
    full = lambda shp: pl.BlockSpec(shp, lambda i: tuple(0 for _ in shp))
    kern = functools.partial(_outproj_prompt_kernel, tm=tm, per_seq=per_seq)
    return pl.pallas_call(
        kern,
        grid=(n // tm,),
        in_specs=[pl.BlockSpec((tm, d), lambda i: (i, 0)),
                  pl.BlockSpec((tm, LRU_WIDTH), lambda i: (i, 0)),
                  pl.BlockSpec((tm, ATTN_WIDTH), lambda i: (i, 0)),
                  pl.BlockSpec((1, 6, d), lambda i: (i // per_seq, 0, 0)),
                  full((1, LRU_WIDTH)), full((1, ATTN_WIDTH)), full((d, d)),
                  full((1, d)), full((1, d)), full((d, ROUTE_LANES)), full((1, ROUTE_LANES))],
        out_specs=[pl.BlockSpec((tm, d), lambda i: (i, 0)),
                   pl.BlockSpec((1, 8, tm), lambda i: (i, 0, 0)),
                   pl.BlockSpec((1, 8, ROUTE_LANES), lambda i: (i // per_seq, 0, 0))],
        out_shape=[jax.ShapeDtypeStruct((n, d), F32),
                   jax.ShapeDtypeStruct((n // tm, 8, tm), F32),
                   jax.ShapeDtypeStruct((n // SEQ, 8, ROUTE_LANES), F32)],
        scratch_shapes=[pltpu.VMEM((tm, tm), BF16), pltpu.VMEM((8, ROUTE_LANES), F32)],
        compiler_params=_cparams(("arbitrary",)),
        name="prompt_outproj_ln_route",
    )(x2d, ylru2d, yatt2d, modp, glru, gattn, wout_bf16, ln1g, ln1b, wr, br)


def _moe_kernel(x1_ref, comb_ref, sh2_ref, sc2_ref, gt2_ref, wg_ref, wu_ref, wd_ref, ln2g_ref, ln2b_ref,
                o_ref, h2_ref, acc_ref):
    e = pl.program_id(1)

    @pl.when(e == 0)
    def _():
        h2_ref[...] = (x1_ref[...] * (1.0 + sc2_ref[...]) + sh2_ref[...]).astype(BF16)
        acc_ref[...] = jnp.zeros_like(acc_ref)

    h2 = h2_ref[...]
    a = jnp.dot(h2, wg_ref[0].astype(BF16), preferred_element_type=F32)
    u = jnp.dot(h2, wu_ref[0].astype(BF16), preferred_element_type=F32)
    comb = comb_ref[...]
    lane = lax.broadcasted_iota(jnp.int32, comb.shape, 1)
    c_e = jnp.sum(jnp.where(lane == e, comb, 0.0), axis=-1, keepdims=True)
    z = _silu(a) * u * c_e
    acc_ref[...] += jnp.dot(z.astype(BF16), wd_ref[0].astype(BF16), preferred_element_type=F32)

    @pl.when(e == N_EXPERTS - 1)
    def _():
        o_ref[...] = _layer_norm(DEEPNORM_ALPHA * x1_ref[...] + (1.0 + gt2_ref[...]) * acc_ref[...],
                                 ln2g_ref[...], ln2b_ref[...])


def _moe_dense(x1, comb, sh2, sc2, gt2, mod_rows_per_tile, w_gate, w_up, w_down, ln2g, ln2b, tm):
    n, d = x1.shape
    if mod_rows_per_tile:
        mspec = pl.BlockSpec((1, 1, d), lambda i, e: (i // mod_rows_per_tile, 0, 0))
        sh2, sc2, gt2 = (m.reshape(-1, 1, d) for m in (sh2, sc2, gt2))
        kern = lambda x1r, cr, s1, s2, s3, *rest: _moe_kernel(x1r, cr, s1.at[0], s2.at[0], s3.at[0], *rest)
    else:
        mspec = pl.BlockSpec((tm, d), lambda i, e: (i, 0))
        kern = _moe_kernel
    full = lambda shp: pl.BlockSpec(shp, lambda i, e: tuple(0 for _ in shp))
    return pl.pallas_call(
        kern,
        grid=(n // tm, N_EXPERTS),
        in_specs=[pl.BlockSpec((tm, d), lambda i, e: (i, 0)),
                  pl.BlockSpec((tm, ROUTE_LANES), lambda i, e: (i, 0)),
                  mspec, mspec, mspec,
                  pl.BlockSpec((1, d, D_EXPERT), lambda i, e: (e, 0, 0)),
                  pl.BlockSpec((1, d, D_EXPERT), lambda i, e: (e, 0, 0)),
                  pl.BlockSpec((1, D_EXPERT, d), lambda i, e: (e, 0, 0)),
                  full((1, d)), full((1, d))],
        out_specs=pl.BlockSpec((tm, d), lambda i, e: (i, 0)),
        out_shape=jax.ShapeDtypeStruct((n, d), F32),
        scratch_shapes=[pltpu.VMEM((tm, d), BF16), pltpu.VMEM((tm, d), F32)],
        compiler_params=_cparams(("arbitrary", "arbitrary")),
        name="moe_dense_ln",
    )(x1, comb, sh2, sc2, gt2, w_gate, w_up, w_down, ln2g, ln2b)


RB_SUB = 512
RB_NSUB = SEQ // RB_SUB
RB_CHUNK = 128
RB_CHUNK_BITS = 7
RB_NCHUNK = 2 * SEQ // RB_CHUNK
RB_PITCH = RB_CHUNK + 8
RB_SPITCH = RB_SUB + 8
RB_TOK_UNROLL = 8


def _rb_kernel(cnt_ref, x1_ref, mod_ref, offs_ref, wts_ref, wg_hbm, wu_hbm, wd_hbm, ln2g_ref, ln2b_ref,
               o_ref, buf_ref, stage_ref, wg_buf, wu_buf, wd_buf, start_ref, sem):
    b = pl.program_id(0)
    s = pl.program_id(1)

    @pl.when(s == 0)
    def _starts():
        def body(e, run):
            start_ref[e] = run
            return run + cnt_ref[b, e]
        lax.fori_loop(0, N_EXPERTS, body, jnp.int32(0))
        buf_ref[RB_NCHUNK * 8 * RB_PITCH:(RB_NCHUNK + 1) * 8 * RB_PITCH, :] = jnp.zeros((8 * RB_PITCH, LANES), F32)

    @pl.when(s < RB_NSUB)
    def _dispatch():
        sh2 = mod_ref[0, 3:4, :]
        sc2 = mod_ref[0, 4:5, :]
        h2 = x1_ref[...] * (1.0 + sc2) + sh2
        for j in range(8):
            stage_ref[RB_SPITCH * j:RB_SPITCH * j + RB_SUB, :] = h2[:, LANES * j:LANES * (j + 1)]

        def body(i, carry):
            for k in range(RB_TOK_UNROLL):
                t = i * RB_TOK_UNROLL + k
                slab = stage_ref[pl.ds(t, 8, stride=RB_SPITCH), :]
                for a in range(2):
                    buf_ref[pl.ds(offs_ref[0, a, t], 8, stride=RB_PITCH), :] = slab
            return carry
        lax.fori_loop(0, RB_SUB // RB_TOK_UNROLL, body, 0)

    @pl.when(s == RB_NSUB)
    def _experts():
        def copies(e, slot):
            return (pltpu.make_async_copy(wg_hbm.at[e], wg_buf.at[slot], sem.at[slot, 0]),
                    pltpu.make_async_copy(wu_hbm.at[e], wu_buf.at[slot], sem.at[slot, 1]),
                    pltpu.make_async_copy(wd_hbm.at[e], wd_buf.at[slot], sem.at[slot, 2]))

        def run_expert(e, slot):
            lo_row = start_ref[e]
            hi_row = lo_row + cnt_ref[b, e]

            c_lo = lax.shift_right_logical(lo_row, RB_CHUNK_BITS)
            c_hi = lax.shift_right_logical(hi_row + (RB_CHUNK - 1), RB_CHUNK_BITS)
            row = lax.broadcasted_iota(jnp.int32, (RB_CHUNK, 1), 0)

            def load(c):
                base = pl.multiple_of(c * (8 * RB_PITCH), 8)
                return [buf_ref[pl.ds(base + RB_PITCH * j, RB_CHUNK), :] for j in range(8)]

            def ffn(tiles):
                x = jnp.concatenate(tiles, axis=-1).astype(BF16)
                a = jnp.dot(x, wg_buf[slot], preferred_element_type=F32)
                u = jnp.dot(x, wu_buf[slot], preferred_element_type=F32)
                z = (_silu(a) * u).astype(BF16)
                return jnp.dot(z, wd_buf[slot], preferred_element_type=F32)

            def store(c, tiles, y):
                base = pl.multiple_of(c * (8 * RB_PITCH), 8)
                mine = (row >= lo_row - c * RB_CHUNK) & (row < hi_row - c * RB_CHUNK)
                for j in range(8):
                    buf_ref[pl.ds(base + RB_PITCH * j, RB_CHUNK), :] = jnp.where(
                        mine, y[:, LANES * j:LANES * (j + 1)], tiles[j])

            def pair(i, carry):
                c0 = c_lo + 2 * i
                c1 = jnp.where(c0 + 1 < c_hi, c0 + 1, RB_NCHUNK)
                t0 = load(c0)
                t1 = load(c1)
                y0 = ffn(t0)
                y1 = ffn(t1)
                store(c0, t0, y0)
                store(c1, t1, y1)
                return carry

            lax.fori_loop(0, lax.shift_right_logical(c_hi - c_lo + 1, 1), pair, 0)

        for c in copies(0, 0):
            c.start()

        def pair_body(i, carry):
            e0 = 2 * i
            for c in copies(e0 + 1, 1):
                c.start()
            for c in copies(e0, 0):
                c.wait()
            run_expert(e0, 0)

            @pl.when(i < N_EXPERTS // 2 - 1)
            def _():
                for c in copies(e0 + 2, 0):
                    c.start()
            for c in copies(e0 + 1, 1):
                c.wait()
            run_expert(e0 + 1, 1)
            return carry
        lax.fori_loop(0, N_EXPERTS // 2, pair_body, 0)

    @pl.when(s > RB_NSUB)
    def _combine():
        def body(i, carry):
            for k in range(RB_TOK_UNROLL):
                t = i * RB_TOK_UNROLL + k
                acc = None
                for a in range(2):
                    term = wts_ref[0, a, t] * buf_ref[pl.ds(offs_ref[0, a, t], 8, stride=RB_PITCH), :]
                    acc = term if acc is None else acc + term
                stage_ref[pl.ds(t, 8, stride=RB_SPITCH), :] = acc
            return carry
        lax.fori_loop(0, RB_SUB // RB_TOK_UNROLL, body, 0)
        gt2 = mod_ref[0, 5:6, :]
        f = jnp.concatenate([stage_ref[RB_SPITCH * j:RB_SPITCH * j + RB_SUB, :] for j in range(8)], axis=-1)
        o_ref[...] = _layer_norm(DEEPNORM_ALPHA * x1_ref[...] + (1.0 + gt2) * f, ln2g_ref[...], ln2b_ref[...])


def _rb_offsets(cnt, e12, rank12):
    bsz = cnt.shape[0]
    start = jnp.cumsum(cnt, axis=-1) - cnt
    start_t = jnp.repeat(start, RB_NSUB, axis=0)[:, None, None, :]
    hit = e12[..., None] == jnp.arange(N_EXPERTS, dtype=jnp.int32)
    p = jnp.sum(jnp.where(hit, start_t, 0), axis=-1) + rank12
    return lax.shift_right_logical(p, RB_CHUNK_BITS) * (8 * RB_PITCH) + (p & (RB_CHUNK - 1))


def _rb_moe(x1, modp, cnt, offs, wts, wg_bf16, wu_bf16, wd_bf16, ln2g, ln2b):
    n, d = x1.shape
    bsz = n // SEQ
    nsteps = 2 * RB_NSUB + 1

    def sub_index(s):
        return jnp.where(s < RB_NSUB, s, jnp.where(s == RB_NSUB, RB_NSUB - 1, s - RB_NSUB - 1))

    def tile_map(b, s, cnt_r):
        return (b * RB_NSUB + sub_index(s), 0)

    def tile_map3(b, s, cnt_r):
        return (b * RB_NSUB + sub_index(s), 0, 0)

    def out_map(b, s, cnt_r):
        return (b * RB_NSUB + jnp.maximum(s - RB_NSUB - 1, 0), 0)

    const = lambda shp: pl.BlockSpec(shp, lambda b, s, cnt_r: tuple(0 for _ in shp))
    anyspec = pl.BlockSpec(memory_space=pl.ANY)
    grid_spec = pltpu.PrefetchScalarGridSpec(
        num_scalar_prefetch=1,
        grid=(bsz, nsteps),
        in_specs=[pl.BlockSpec((RB_SUB, d), tile_map),
                  pl.BlockSpec((1, 6, d), lambda b, s, cnt_r: (b, 0, 0)),
                  pl.BlockSpec((1, 2, RB_SUB), tile_map3, memory_space=pltpu.SMEM),
                  pl.BlockSpec((1, 2, RB_SUB), tile_map3, memory_space=pltpu.SMEM),
                  anyspec, anyspec, anyspec,
                  const((1, d)), const((1, d))],
        out_specs=pl.BlockSpec((RB_SUB, d), out_map),
        scratch_shapes=[pltpu.VMEM(((RB_NCHUNK + 1) * 8 * RB_PITCH, LANES), F32),
                        pltpu.VMEM((8 * RB_SPITCH, LANES), F32),
                        pltpu.VMEM((2, d, D_EXPERT), BF16),
                        pltpu.VMEM((2, d, D_EXPERT), BF16),
                        pltpu.VMEM((2, D_EXPERT, d), BF16),
                        pltpu.SMEM((N_EXPERTS,), jnp.int32),
                        pltpu.SemaphoreType.DMA((2, 3))])
    return pl.pallas_call(
        _rb_kernel,
        grid_spec=grid_spec,
        out_shape=jax.ShapeDtypeStruct((n, d), F32),
        compiler_params=_cparams(("arbitrary", "arbitrary")),
        name="moe_routed_ln",
    )(cnt, x1, modp, offs, wts, wg_bf16, wu_bf16, wd_bf16, ln2g, ln2b)


def _sample_in_kernel(x_ref, sh1_ref, sc1_ref, win_ref, ctx_ref, h0_ref, convw_ref, convb_ref,
                      wlo_ref, whi_ref, bgate_ref, lam_ref,
                      ylru_ref, q_ref, k_ref, v_ref, cstate_ref, hnew_ref):
    h = x_ref[...] * (1.0 + sc1_ref[...]) + sh1_ref[...]
    z = _dot(h, win_ref[...], True)
    xb = z[:, :LRU_WIDTH]
    gate = z[:, LRU_WIDTH:2 * LRU_WIDTH]
    c0 = ctx_ref[:, 0, :]
    c1 = ctx_ref[:, 1, :]
    c2 = ctx_ref[:, 2, :]
    xc = (convb_ref[...] + convw_ref[0:1, :] * c0 + convw_ref[1:2, :] * c1
          + convw_ref[2:3, :] * c2 + convw_ref[3:4, :] * xb)
    cstate_ref[:, 0, :] = c1
    cstate_ref[:, 1, :] = c2
    cstate_ref[:, 2, :] = xb
    sp = _softplus(-lam_ref[...])
    a, bterm = _lru_gates(xc, wlo_ref[...], whi_ref[...], bgate_ref[...], sp, True)
    hn = a * h0_ref[...] + bterm
    hnew_ref[...] = hn
    ylru_ref[...] = hn * _gelu_tanh(gate)
    low = lax.broadcasted_iota(jnp.int32, (DEC_BATCH, LANES), 1) < HEAD_DIM
    for c in range(4):
        qc = z[:, 2 * LRU_WIDTH + LANES * c:2 * LRU_WIDTH + LANES * (c + 1)]
        q_ref[pl.ds(c, DEC_BATCH, stride=N_HEADS), :] = jnp.where(low, qc, 0.0)
        q_ref[pl.ds(c + 4, DEC_BATCH, stride=N_HEADS), :] = jnp.where(low, 0.0, qc)
    k_ref[...] = z[:, 2 * LRU_WIDTH + ATTN_WIDTH:2 * LRU_WIDTH + ATTN_WIDTH + KV_WIDTH]
    v_ref[...] = z[:, 2 * LRU_WIDTH + ATTN_WIDTH + KV_WIDTH:]


def _sample_in(x, sh1, sc1, w_in_p, ctx, h0, conv_w, conv_b, wlo, whi, bgate, lam):
    n = DEC_BATCH
    outs = [jax.ShapeDtypeStruct((n, LRU_WIDTH), F32),
            jax.ShapeDtypeStruct((n * N_HEADS, LANES), F32),
            jax.ShapeDtypeStruct((n, KV_WIDTH), F32),
            jax.ShapeDtypeStruct((n, KV_WIDTH), F32),
            jax.ShapeDtypeStruct((n, CONV_WIDTH - 1, LRU_WIDTH), F32),
            jax.ShapeDtypeStruct((n, LRU_WIDTH), F32)]
    return pl.pallas_call(
        _sample_in_kernel,
        out_shape=outs,
        compiler_params=pltpu.CompilerParams(vmem_limit_bytes=VMEM_LIMIT),
        name="sample_inproj_rglru",
    )(x, sh1, sc1, w_in_p, ctx, h0, conv_w, conv_b, wlo, whi, bgate, lam)


def _sample_attn_kernel(q_ref, kn_ref, vn_ref, ck_ref, cv_ref, sink_ref, y_ref, nk_ref, nv_ref, *, bb):
    rows = lax.broadcasted_iota(jnp.int32, (WINDOW, KV_WIDTH), 0)
    sink = sink_ref[...]
    for b in range(bb):
        q8 = q_ref[b]
        kb = ck_ref[b]
        vb = cv_ref[b]
        kn = kn_ref[b:b + 1, :]
        vn = vn_ref[b:b + 1, :]
        s = _dot_nt(q8, kb, True) * ATTN_SCALE
        s_self = jnp.sum(q8 * kn, axis=-1, keepdims=True) * ATTN_SCALE
        m = jnp.maximum(jnp.maximum(jnp.max(s, axis=-1, keepdims=True), s_self), sink)
        e = jnp.exp(s - m)
        e_self = jnp.exp(s_self - m)
        den = jnp.sum(e, axis=-1, keepdims=True) + e_self + jnp.exp(sink - m)
        inv = 1.0 / den
        y_ref[b] = _dot(e * inv, vb, True) + (e_self * inv) * vn
        nk_ref[b] = jnp.where(rows == WINDOW - 1, kn, pltpu.roll(kb, WINDOW - 1, axis=0))
        nv_ref[b] = jnp.where(rows == WINDOW - 1, vn, pltpu.roll(vb, WINDOW - 1, axis=0))


def _sample_attn(q3, kn, vn, cache_k, cache_v, sinks, bb=8):
    n = DEC_BATCH
    kern = functools.partial(_sample_attn_kernel, bb=bb)
    return pl.pallas_call(
        kern,
        grid=(n // bb,),
        in_specs=[pl.BlockSpec((bb, N_HEADS, LANES), lambda i: (i, 0, 0)),
                  pl.BlockSpec((bb, KV_WIDTH), lambda i: (i, 0)),
                  pl.BlockSpec((bb, KV_WIDTH), lambda i: (i, 0)),
                  pl.BlockSpec((bb, WINDOW, KV_WIDTH), lambda i: (i, 0, 0)),
                  pl.BlockSpec((bb, WINDOW, KV_WIDTH), lambda i: (i, 0, 0)),
                  pl.BlockSpec((N_HEADS, 1), lambda i: (0, 0))],
        out_specs=[pl.BlockSpec((bb, N_HEADS, LANES), lambda i: (i, 0, 0)),
                   pl.BlockSpec((bb, WINDOW, KV_WIDTH), lambda i: (i, 0, 0)),
                   pl.BlockSpec((bb, WINDOW, KV_WIDTH), lambda i: (i, 0, 0))],
        out_shape=[jax.ShapeDtypeStruct((n, N_HEADS, LANES), F32),
                   jax.ShapeDtypeStruct((n, WINDOW, KV_WIDTH), F32),
                   jax.ShapeDtypeStruct((n, WINDOW, KV_WIDTH), F32)],
        compiler_params=_cparams(("arbitrary",)),
        name="sample_cache_attention",
    )(q3, kn, vn, cache_k, cache_v, sinks.reshape(N_HEADS, 1))


def _sample_out_kernel(x_ref, ylru_ref, yatt_ref, sh2_ref, sc2_ref, gt1_ref, glru_ref, gattn_ref, wout_ref,
                       ln1g_ref, ln1b_ref, wr_ref, br_ref, x1_ref, comb_ref):
    low = lax.broadcasted_iota(jnp.int32, (DEC_BATCH, LANES), 1) < HEAD_DIM
    yatt = jnp.concatenate(
        [jnp.where(low, yatt_ref[pl.ds(c, DEC_BATCH, stride=N_HEADS), :],
                   yatt_ref[pl.ds(c + 4, DEC_BATCH, stride=N_HEADS), :]) for c in range(4)], axis=-1)
    x1, comb = _outproj_body(x_ref[...], ylru_ref[...], yatt, sh2_ref[...], sc2_ref[...], gt1_ref[...],
                             glru_ref[...], gattn_ref[...], wout_ref[...], ln1g_ref[...], ln1b_ref[...],
                             wr_ref[...], br_ref[...], True)
    x1_ref[...] = x1
    comb_ref[...] = comb


def _sample_out(x, ylru, yatt2d, sh2, sc2, gt1, glru, gattn, wout_p, ln1g, ln1b, wr, br):
    n = DEC_BATCH
    return pl.pallas_call(
        _sample_out_kernel,
        out_shape=[jax.ShapeDtypeStruct((n, D_MODEL), F32), jax.ShapeDtypeStruct((n, ROUTE_LANES), F32)],
        compiler_params=pltpu.CompilerParams(vmem_limit_bytes=VMEM_LIMIT),
        name="sample_outproj_ln_route",
    )(x, ylru, yatt2d, sh2, sc2, gt1, glru, gattn, wout_p, ln1g, ln1b, wr, br)


def _block_diag_halves(w_a, w_x):
    def bd(w4):
        eye = jnp.eye(4, dtype=w4.dtype)
        return (w4[:, :, None, :] * eye[:, None, :, None]).reshape(256, 256)
    lo = jnp.concatenate([bd(w_a[:4]), bd(w_x[:4])], axis=1)
    hi = jnp.concatenate([bd(w_a[4:]), bd(w_x[4:])], axis=1)
    return lo, hi


def kernel(x_prompt, x_sample, c_prompt, c_sample, state_conv, state_h, cache_k, cache_v, w_ada, b_ada, w_in,
           conv_w, conv_b, w_rg_a, b_rg_a, w_rg_x, b_rg_x, lru_lambda, sinks, g_lru, g_attn, w_out, ln1_g, ln1_b,
           w_group, b_group, w_router, b_router, w_gate, w_up, w_down, ln2_g, ln2_b):
    d = D_MODEL
    perm = jnp.asarray(HEAD_PERM)
    w_in0 = w_in[0]
    q0 = 2 * LRU_WIDTH
    w_in_p = jnp.concatenate([w_in0[:, :q0], w_in0[:, q0:q0 + ATTN_WIDTH][:, perm], w_in0[:, q0 + ATTN_WIDTH:]],
                             axis=1)
    w_out0 = w_out[0]
    w_out_p = jnp.concatenate([w_out0[:LRU_WIDTH], w_out0[LRU_WIDTH:][perm]], axis=0)
    g_attn_p = g_attn[0][perm].reshape(1, -1)
    glru = g_lru[0].reshape(1, -1)
    wlo, whi = _block_diag_halves(w_rg_a[0], w_rg_x[0])
    bgate = jnp.concatenate([b_rg_a[0].reshape(-1), b_rg_x[0].reshape(-1)]).reshape(1, -1)
    lam = lru_lambda[0].reshape(1, -1)
    convw = conv_w[0]
    convb = conv_b[0].reshape(1, -1)
    ln1g, ln1b = ln1_g[0].reshape(1, -1), ln1_b[0].reshape(1, -1)
    ln2g, ln2b = ln2_g[0].reshape(1, -1), ln2_b[0].reshape(1, -1)
    wr = jnp.concatenate([jnp.transpose(w_router[0], (1, 0, 2)).reshape(d, N_EXPERTS), w_group[0],
                          jnp.zeros((d, ROUTE_LANES - N_EXPERTS - N_GROUPS), F32)], axis=1)
    br = jnp.concatenate([b_router[0].reshape(-1), b_group[0],
                          jnp.zeros((ROUTE_LANES - N_EXPERTS - N_GROUPS,), F32)]).reshape(1, -1)
    sink_p = sinks[0]

    c_all = jnp.concatenate([c_prompt, jnp.zeros((8 - BATCH, d), F32), c_sample], axis=0)
    mod = _ada(c_all, w_ada[0], b_ada[0])
    modp = mod[:BATCH].reshape(BATCH, 6, d)
    mods = mod[8:]
    sh1_s, sc1_s, gt1_s, sh2_s, sc2_s, gt2_s = (mods[:, k * d:(k + 1) * d] for k in range(6))

    zin = _inproj(x_prompt, modp, w_in_p.astype(BF16))
    ylru, cstate8, hlast8 = _lru(zin, convw, convb, wlo.astype(BF16), whi.astype(BF16), bgate, lam)
    yatt = _attn(zin, sink_p)
    n_p = BATCH * SEQ
    x1_p, info, cntf = _outproj_prompt(x_prompt.reshape(n_p, d), ylru.reshape(n_p, LRU_WIDTH),
                                       yatt.reshape(n_p, ATTN_WIDTH), modp, glru, g_attn_p, w_out_p.astype(BF16),
                                       ln1g, ln1b, wr, br, tm=RB_SUB)
    cnt = cntf[:, 0, :N_EXPERTS].astype(jnp.int32)
    offs = _rb_offsets(cnt, info[:, 0:2].astype(jnp.int32), info[:, 4:6].astype(jnp.int32))
    y_p = _rb_moe(x1_p, modp, cnt, offs, info[:, 2:4], w_gate[0].astype(BF16), w_up[0].astype(BF16),
                  w_down[0].astype(BF16), ln2g, ln2b)

    ylru_s, q2d, kn, vn, cstate_s, hnew_s = _sample_in(
        x_sample.reshape(DEC_BATCH, d), sh1_s, sc1_s, w_in_p, state_conv[0], state_h[0],
        convw, convb, wlo, whi, bgate, lam)
    yatt3, newk, newv = _sample_attn(q2d.reshape(DEC_BATCH, N_HEADS, LANES), kn, vn,
                                     cache_k[0].reshape(DEC_BATCH, WINDOW, KV_WIDTH),
                                     cache_v[0].reshape(DEC_BATCH, WINDOW, KV_WIDTH), sink_p)
    x1_s, comb_s = _sample_out(x_sample.reshape(DEC_BATCH, d), ylru_s, yatt3.reshape(DEC_BATCH * N_HEADS, LANES),
                               sh2_s, sc2_s, gt1_s, glru, g_attn_p, w_out_p, ln1g, ln1b, wr, br)
    y_s = _moe_dense(x1_s, comb_s, sh2_s, sc2_s, gt2_s, 0, w_gate[0], w_up[0], w_down[0], ln2g, ln2b, DEC_BATCH)

    kq = 2 * LRU_WIDTH + ATTN_WIDTH
    return (y_p.reshape(BATCH, SEQ, d),
            y_s.reshape(DEC_BATCH, 1, d),
            cstate8[:, 5:8][None],
            hlast8[:, 7][None],
            zin[:, SEQ - WINDOW:, kq:kq + KV_WIDTH].reshape(1, BATCH, WINDOW, N_KV_HEADS, HEAD_DIM),
            zin[:, SEQ - WINDOW:, kq + KV_WIDTH:].reshape(1, BATCH, WINDOW, N_KV_HEADS, HEAD_DIM),
            cstate_s[None],
            hnew_s[None],
            newk.reshape(1, DEC_BATCH, WINDOW, N_KV_HEADS, HEAD_DIM),
            newv.reshape(1, DEC_BATCH, WINDOW, N_KV_HEADS, HEAD_DIM))
```

```python
import functools

import jax
import jax.numpy as jnp
import numpy as np
from jax import lax
from jax.experimental import pallas as pl
from jax.experimental.pallas import tpu as pltpu

F32 = jnp.float32
BF16 = jnp.bfloat16
HIGHEST = lax.Precision.HIGHEST

D_MODEL = 1024
BATCH = 4
SEQ = 4096
DEC_BATCH = 128
LRU_WIDTH = 512
LRU_BLOCKS = 8
LRU_BLOCK = 64
CONV_WIDTH = 4
LRU_C = 8.0
N_HEADS = 8
N_KV_HEADS = 2
HEAD_DIM = 64
ATTN_WIDTH = 512
KV_WIDTH = 128
WINDOW = 128
IN_WIDTH = 2 * LRU_WIDTH + ATTN_WIDTH + 2 * KV_WIDTH
N_GROUPS = 4
EXPERTS_PER_GROUP = 8
N_EXPERTS = 32
D_EXPERT = 256
DEEPNORM_ALPHA = 2.0 ** 0.25
LN_EPS = 1e-5
RMS_EPS = 1e-6
ATTN_SCALE = HEAD_DIM ** -0.5

LANES = 128
ROUTE_LANES = 128
ROUTE_INFO = 40
VMEM_LIMIT = 56 * 1024 * 1024

HEAD_PERM = np.concatenate(
    [np.concatenate([np.arange(64 * c, 64 * c + 64), np.arange(64 * (c + 4), 64 * (c + 4) + 64)])
     for c in range(4)])


def _cparams(sem):
    return pltpu.CompilerParams(dimension_semantics=sem, vmem_limit_bytes=VMEM_LIMIT)


def _dot(a, b, exact):
    if exact:
        return jnp.dot(a, b, precision=HIGHEST, preferred_element_type=F32)
    return jnp.dot(a.astype(BF16), b.astype(BF16), preferred_element_type=F32)


def _dot_nt(a, b, exact):
    dn = (((1,), (1,)), ((), ()))
    if exact:
        return lax.dot_general(a, b, dn, precision=HIGHEST, preferred_element_type=F32)
    return lax.dot_general(a.astype(BF16), b.astype(BF16), dn, preferred_element_type=F32)


def _sigmoid(x):
    return 1.0 / (1.0 + jnp.exp(-x))


def _silu(x):
    return x * _sigmoid(x)


def _gelu_tanh(x):
    return 0.5 * x * (1.0 + jnp.tanh(np.sqrt(2.0 / np.pi).astype(np.float32) * (x + 0.044715 * (x * x * x))))


def _softplus(x):
    return jnp.maximum(x, 0.0) + jnp.log1p(jnp.exp(-jnp.abs(x)))


def _layer_norm(x, g, b):
    mu = jnp.mean(x, axis=-1, keepdims=True)
    xc = x - mu
    var = jnp.mean(xc * xc, axis=-1, keepdims=True)
    return xc * lax.rsqrt(var + LN_EPS) * g + b


def _rms_norm(x, g):
    return x * lax.rsqrt(jnp.mean(x * x, axis=-1, keepdims=True) + RMS_EPS) * g


def _ada_kernel(c_ref, w_ref, b_ref, o_ref):
    o_ref[...] = _dot(_silu(c_ref[...]), w_ref[...], True) + b_ref[...]


def _ada(c_all, w_ada, b_ada):
    rows = c_all.shape[0]
    bn = 512
    return pl.pallas_call(
        _ada_kernel,
        grid=(6 * D_MODEL // bn,),
        in_specs=[pl.BlockSpec((rows, D_MODEL), lambda j: (0, 0)),
                  pl.BlockSpec((D_MODEL, bn), lambda j: (0, j)),
                  pl.BlockSpec((1, bn), lambda j: (0, j))],
        out_specs=pl.BlockSpec((rows, bn), lambda j: (0, j)),
        out_shape=jax.ShapeDtypeStruct((rows, 6 * D_MODEL), F32),
        compiler_params=_cparams(("arbitrary",)),
        name="ada_modulation",
    )(c_all, w_ada, b_ada.reshape(1, -1))


def _inproj_kernel(x_ref, mod_ref, w_ref, o_ref):
    sh1 = mod_ref[0, 0:1, :]
    sc1 = mod_ref[0, 1:2, :]
    h = x_ref[0] * (1.0 + sc1) + sh1
    o_ref[0] = _dot(h, w_ref[...], False)


def _inproj(x, modp, w_in_bf16, tm=512):
    b, t, d = x.shape
    return pl.pallas_call(
        _inproj_kernel,
        grid=(b, t // tm),
        in_specs=[pl.BlockSpec((1, tm, d), lambda i, j: (i, j, 0)),
                  pl.BlockSpec((1, 6, d), lambda i, j: (i, 0, 0)),
                  pl.BlockSpec((d, IN_WIDTH), lambda i, j: (0, 0))],
        out_specs=pl.BlockSpec((1, tm, IN_WIDTH), lambda i, j: (i, j, 0)),
        out_shape=jax.ShapeDtypeStruct((b, t, IN_WIDTH), F32),
        compiler_params=_cparams(("arbitrary", "arbitrary")),
        name="prompt_inproj",
    )(x, modp, w_in_bf16)


def _lru_gates(xc, wlo, whi, bgate, sp_neg_lam, exact):
    g_lo = _dot(xc[:, :256], wlo, exact)
    g_hi = _dot(xc[:, 256:], whi, exact)
    ga = jnp.concatenate([g_lo[:, :256], g_hi[:, :256]], axis=-1) + bgate[:, :LRU_WIDTH]
    gx = jnp.concatenate([g_lo[:, 256:], g_hi[:, 256:]], axis=-1) + bgate[:, LRU_WIDTH:]
    r = _sigmoid(ga)
    i = _sigmoid(gx)
    log_a = -LRU_C * r * sp_neg_lam
    a = jnp.exp(log_a)
    one_minus_a2 = -jnp.tanh(log_a) * (a * a + 1.0)
    bterm = jnp.sqrt(one_minus_a2) * (i * xc)
    return a, bterm


def _lru_kernel(z_ref, convw_ref, convb_ref, wlo_ref, whi_ref, bgate_ref, lam_ref,
                y_ref, cstate_ref, hlast_ref, tail_ref, carry_ref, *, tl):
    j = pl.program_id(1)

    @pl.when(j == 0)
    def _():
        tail_ref[...] = jnp.zeros_like(tail_ref)
        carry_ref[...] = jnp.zeros_like(carry_ref)

    xb = z_ref[0, :, :LRU_WIDTH]
    gate = z_ref[0, :, LRU_WIDTH:]
    rows = lax.broadcasted_iota(jnp.int32, (tl, LRU_WIDTH), 0)

    xc = convb_ref[...] + convw_ref[3:4, :] * xb
    rows8 = lax.broadcasted_iota(jnp.int32, (8, LRU_WIDTH), 0)
    tail = tail_ref[...]
    for back in (1, 2, 3):
        rolled = pltpu.roll(xb, back, axis=0)
        top = jnp.where(rows8 >= back, rolled[:8], pltpu.roll(tail, back, axis=0))
        shifted = jnp.concatenate([top, rolled[8:]], axis=0)
        xc = xc + convw_ref[3 - back:4 - back, :] * shifted
    tail_ref[...] = xb[tl - 8:, :]
    cstate_ref[0] = xb[tl - 8:, :]

    sp = _softplus(-lam_ref[...])
    a, bterm = _lru_gates(xc, wlo_ref[...], whi_ref[...], bgate_ref[...], sp, False)

    s = 1
    while s < tl:
        a_sh = jnp.where(rows >= s, pltpu.roll(a, s, axis=0), 1.0)
        b_sh = jnp.where(rows >= s, pltpu.roll(bterm, s, axis=0), 0.0)
        bterm = a * b_sh + bterm
        a = a * a_sh
        s *= 2
    h = a * carry_ref[7:8, :] + bterm
    carry_ref[...] = h[tl - 8:, :]
    hlast_ref[0] = h[tl - 8:, :]
    y_ref[0] = h * _gelu_tanh(gate)


def _lru(zin, conv_w, conv_b, wlo, whi, bgate, lam, tl=512):
    b, t, _ = zin.shape
    kern = functools.partial(_lru_kernel, tl=tl)
    full = lambda shp: pl.BlockSpec(shp, lambda i, j: tuple(0 for _ in shp))
    return pl.pallas_call(
        kern,
        grid=(b, t // tl),
        in_specs=[pl.BlockSpec((1, tl, 2 * LRU_WIDTH), lambda i, j: (i, j, 0)),
                  full((CONV_WIDTH, LRU_WIDTH)), full((1, LRU_WIDTH)),
                  full((256, 512)), full((256, 512)), full((1, 2 * LRU_WIDTH)), full((1, LRU_WIDTH))],
        out_specs=[pl.BlockSpec((1, tl, LRU_WIDTH), lambda i, j: (i, j, 0)),
                   pl.BlockSpec((1, 8, LRU_WIDTH), lambda i, j: (i, 0, 0)),
                   pl.BlockSpec((1, 8, LRU_WIDTH), lambda i, j: (i, 0, 0))],
        out_shape=[jax.ShapeDtypeStruct((b, t, LRU_WIDTH), F32),
                   jax.ShapeDtypeStruct((b, 8, LRU_WIDTH), F32),
                   jax.ShapeDtypeStruct((b, 8, LRU_WIDTH), F32)],
        scratch_shapes=[pltpu.VMEM((8, LRU_WIDTH), F32), pltpu.VMEM((8, LRU_WIDTH), F32)],
        compiler_params=_cparams(("arbitrary", "arbitrary")),
        name="prompt_rglru",
    )(zin, conv_w, conv_b, wlo, whi, bgate, lam)


ATTN_BLOCKS = 4


def _attn_kernel(q_ref, k_ref, v_ref, sink_ref, o_ref, kprev_ref, vprev_ref):
    j = pl.program_id(1)

    @pl.when(j == 0)
    def _():
        kprev_ref[...] = jnp.zeros_like(kprev_ref)
        vprev_ref[...] = jnp.zeros_like(vprev_ref)

    blk = WINDOW
    lane = lax.broadcasted_iota(jnp.int32, (blk, LANES), 1)
    low = lane < HEAD_DIM
    qi = lax.broadcasted_iota(jnp.int32, (blk, 2 * blk), 0)
    sj = lax.broadcasted_iota(jnp.int32, (blk, 2 * blk), 1)
    rel = blk + qi - sj
    in_window = (rel >= 0) & (rel <= WINDOW)
    sink = sink_ref[...].reshape(N_HEADS, blk, 1)
    k_ext = jnp.concatenate([kprev_ref[...], k_ref[0]], axis=0).astype(BF16)
    v_ext = jnp.concatenate([vprev_ref[...], v_ref[0]], axis=0).astype(BF16)
    for n in range(ATTN_BLOCKS):
        q = q_ref[0, blk * n:blk * (n + 1), :]
        pieces = []
        for half in (0, 1):
            for c in range(4):
                qc = q[:, LANES * c:LANES * (c + 1)]
                pieces.append(jnp.where(low if half == 0 else ~low, qc, 0.0).astype(BF16))
        q8 = jnp.concatenate(pieces, axis=0)
        k_band = k_ext[blk * n:blk * (n + 2)]
        v_band = v_ext[blk * n:blk * (n + 2)]
        s = _dot_nt(q8, k_band, False) * ATTN_SCALE
        s = s.reshape(N_HEADS, blk, 2 * blk)
        valid = in_window & ((sj >= blk) | (j > 0)) if n == 0 else in_window
        s = jnp.where(valid[None], s, -jnp.inf)
        m = jnp.maximum(jnp.max(s, axis=-1, keepdims=True), sink)
        e = jnp.exp(s - m)
        den = jnp.sum(e, axis=-1, keepdims=True) + jnp.exp(sink - m)
        p = (e * (1.0 / den)).reshape(N_HEADS * blk, 2 * blk)
        o8 = _dot(p, v_band, False)
        cols = []
        for c in range(4):
            cols.append(jnp.where(low, o8[blk * c:blk * (c + 1)], o8[blk * (c + 4):blk * (c + 5)]))
        o_ref[0, blk * n:blk * (n + 1), :] = jnp.concatenate(cols, axis=-1)
    kprev_ref[...] = k_ref[0, blk * (ATTN_BLOCKS - 1):, :]
    vprev_ref[...] = v_ref[0, blk * (ATTN_BLOCKS - 1):, :]


def _attn(zin, sinks):
    b, t, _ = zin.shape
    blk = WINDOW
    tq = blk * ATTN_BLOCKS
    sink_col = jnp.repeat(sinks.astype(F32), blk).reshape(N_HEADS * blk, 1)
    return pl.pallas_call(
        _attn_kernel,
        grid=(b, t // tq),
        in_specs=[pl.BlockSpec((1, tq, ATTN_WIDTH), lambda i, j: (i, j, 2)),
                  pl.BlockSpec((1, tq, KV_WIDTH), lambda i, j: (i, j, 12)),
                  pl.BlockSpec((1, tq, KV_WIDTH), lambda i, j: (i, j, 13)),
                  pl.BlockSpec((N_HEADS * blk, 1), lambda i, j: (0, 0))],
        out_specs=pl.BlockSpec((1, tq, ATTN_WIDTH), lambda i, j: (i, j, 0)),
        out_shape=jax.ShapeDtypeStruct((b, t, ATTN_WIDTH), F32),
        scratch_shapes=[pltpu.VMEM((blk, KV_WIDTH), F32), pltpu.VMEM((blk, KV_WIDTH), F32)],
        compiler_params=_cparams(("arbitrary", "arbitrary")),
        name="prompt_window_attention",
    )(zin, zin, zin, sink_col)


def _dot_split3(a, b):
    a_hi = a.astype(BF16)
    b_hi = b.astype(BF16)
    a_lo = (a - a_hi.astype(F32)).astype(BF16)
    b_lo = (b - b_hi.astype(F32)).astype(BF16)
    return (jnp.dot(a_hi, b_hi, preferred_element_type=F32) + jnp.dot(a_hi, b_lo, preferred_element_type=F32)
            + jnp.dot(a_lo, b_hi, preferred_element_type=F32))


def _route(h2, wr, br, exact):
    t = h2.shape[0]
    logits = (_dot(h2, wr, True) if exact else _dot_split3(h2, wr)) + br
    lane = lax.broadcasted_iota(jnp.int32, (t, ROUTE_LANES), 1).astype(F32)
    neg = -jnp.inf
    big = float(ROUTE_LANES)
    is_g = (lane >= N_EXPERTS) & (lane < N_EXPERTS + N_GROUPS)
    lg = jnp.where(is_g, logits, neg)
    mg = jnp.max(lg, axis=-1, keepdims=True)
    eg = jnp.where(is_g, jnp.exp(lg - mg), 0.0)
    pg = eg / jnp.sum(eg, axis=-1, keepdims=True)
    g_val = jnp.max(pg, axis=-1, keepdims=True)
    g_lane = jnp.min(jnp.where((pg == g_val) & is_g, lane, big), axis=-1, keepdims=True)
    g_idx = g_lane - N_EXPERTS
    in_grp = (lane >= g_idx * EXPERTS_PER_GROUP) & (lane < (g_idx + 1.0) * EXPERTS_PER_GROUP)
    le = jnp.where(in_grp, logits, neg)
    me = jnp.max(le, axis=-1, keepdims=True)
    ee = jnp.where(in_grp, jnp.exp(le - me), 0.0)
    pe = ee / jnp.sum(ee, axis=-1, keepdims=True)
    v1 = jnp.max(pe, axis=-1, keepdims=True)
    l1 = jnp.min(jnp.where((pe == v1) & in_grp, lane, big), axis=-1, keepdims=True)
    rest = in_grp & (lane != l1)
    pe2 = jnp.where(rest, pe, -1.0)
    v2 = jnp.max(pe2, axis=-1, keepdims=True)
    l2 = jnp.min(jnp.where((pe2 == v2) & rest, lane, big), axis=-1, keepdims=True)
    tot = v1 + v2
    w1 = g_val * v1 / tot
    w2 = g_val * v2 / tot
    comb = jnp.where(lane == l1, w1, 0.0) + jnp.where(lane == l2, w2, 0.0)
    return (comb + jnp.where(lane == ROUTE_INFO, l1, 0.0) + jnp.where(lane == ROUTE_INFO + 1, l2, 0.0)
            + jnp.where(lane == ROUTE_INFO + 2, w1, 0.0) + jnp.where(lane == ROUTE_INFO + 3, w2, 0.0))


def _outproj_body(x, ylru, yatt, sh2, sc2, gt1, glru, gattn, wout, ln1g, ln1b, wr, br, exact):
    mixin = jnp.concatenate([_rms_norm(ylru, glru), _rms_norm(yatt, gattn)], axis=-1)
    mix = _dot(mixin, wout, exact)
    x1 = _layer_norm(DEEPNORM_ALPHA * x + (1.0 + gt1) * mix, ln1g, ln1b)
    h2 = x1 * (1.0 + sc2) + sh2
    return x1, _route(h2, wr, br, exact)


def _outproj_prompt_kernel(x_ref, ylru_ref, yatt_ref, mod_ref, glru_ref, gattn_ref, wout_ref,
                           ln1g_ref, ln1b_ref, wr_ref, br_ref, x1_ref, info_ref, cnt_ref, tri_ref, carry_ref,
                           *, tm, per_seq):
    i = pl.program_id(0)

    @pl.when(i == 0)
    def _():
        r = lax.broadcasted_iota(jnp.int32, (tm, tm), 0)
        c = lax.broadcasted_iota(jnp.int32, (tm, tm), 1)
        tri_ref[...] = jnp.where(c < r, 1.0, 0.0).astype(BF16)

    @pl.when(i % per_seq == 0)
    def _():
        carry_ref[...] = jnp.zeros_like(carry_ref)

    gt1 = mod_ref[0, 2:3, :]
    sh2 = mod_ref[0, 3:4, :]
    sc2 = mod_ref[0, 4:5, :]
    combs = []
    nsplit = 2
    for h in range(nsplit):
        rows = slice(h * (tm // nsplit), (h + 1) * (tm // nsplit))
        x1_h, comb_h = _outproj_body(x_ref[rows, :], ylru_ref[rows, :], yatt_ref[rows, :], sh2, sc2, gt1,
                                     glru_ref[...], gattn_ref[...], wout_ref[...], ln1g_ref[...], ln1b_ref[...],
                                     wr_ref[...], br_ref[...], False)
        x1_ref[rows, :] = x1_h
        combs.append(comb_h)
    comb = jnp.concatenate(combs, axis=0)
    lane = lax.broadcasted_iota(jnp.int32, (tm, ROUTE_LANES), 1).astype(F32)
    l1 = jnp.sum(jnp.where(lane == ROUTE_INFO, comb, 0.0), axis=-1, keepdims=True)
    l2 = jnp.sum(jnp.where(lane == ROUTE_INFO + 1, comb, 0.0), axis=-1, keepdims=True)
    o1 = lane == l1
    o2 = lane == l2
    onehot = jnp.where(o1 | o2, 1.0, 0.0)
    before = jnp.dot(tri_ref[...], onehot.astype(BF16), preferred_element_type=F32) + carry_ref[0:1, :]
    rank1 = jnp.sum(jnp.where(o1, before, 0.0), axis=-1, keepdims=True)
    rank2 = jnp.sum(jnp.where(o2, before, 0.0), axis=-1, keepdims=True)
    total = carry_ref[0:1, :] + jnp.sum(onehot, axis=0, keepdims=True)
    carry_ref[...] = jnp.broadcast_to(total, carry_ref.shape)
    cnt_ref[0] = jnp.broadcast_to(total, (8, ROUTE_LANES))
    info = (comb + jnp.where(lane == ROUTE_INFO + 4, rank1, 0.0) + jnp.where(lane == ROUTE_INFO + 5, rank2, 0.0))
    info_ref[0] = jnp.transpose(info)[ROUTE_INFO:ROUTE_INFO + 8, :]


def _outproj_prompt(x2d, ylru2d, yatt2d, modp, glru, gattn, wout_bf16, ln1g, ln1b, wr, br, tm=512):
    n, d = x2d.shape
    per_seq = SEQ // tm
    full = lambda shp: pl.BlockSpec(shp, lambda i: tuple(0 for _ in shp))
    kern = functools.partial(_outproj_prompt_kernel, tm=tm, per_seq=per_seq)
    return pl.pallas_call(
        kern,
        grid=(n // tm,),
        in_specs=[pl.BlockSpec((tm, d), lambda i: (i, 0)),
                  pl.BlockSpec((tm, LRU_WIDTH), lambda i: (i, 0)),
                  pl.BlockSpec((tm, ATTN_WIDTH), lambda i: (i, 0)),
                  pl.BlockSpec((1, 6, d), lambda i: (i // per_seq, 0, 0)),
                  full((1, LRU_WIDTH)), full((1, ATTN_WIDTH)), full((d, d)),
                  full((1, d)), full((1, d)), full((d, ROUTE_LANES)), full((1, ROUTE_LANES))],
        out_specs=[pl.BlockSpec((tm, d), lambda i: (i, 0)),
                   pl.BlockSpec((1, 8, tm), lambda i: (i, 0, 0)),
                   pl.BlockSpec((1, 8, ROUTE_LANES), lambda i: (i // per_seq, 0, 0))],
        out_shape=[jax.ShapeDtypeStruct((n, d), F32),
                   jax.ShapeDtypeStruct((n // tm, 8, tm), F32),
                   jax.ShapeDtypeStruct((n // SEQ, 8, ROUTE_LANES), F32)],
        scratch_shapes=[pltpu.VMEM((tm, tm), BF16), pltpu.VMEM((8, ROUTE_LANES), F32)],
        compiler_params=_cparams(("arbitrary",)),
        name="prompt_outproj_ln_route",
    )(x2d, ylru2d, yatt2d, modp, glru, gattn, wout_bf16, ln1g, ln1b, wr, br)


def _moe_kernel(x1_ref, comb_ref, sh2_ref, sc2_ref, gt2_ref, wg_ref, wu_ref, wd_ref, ln2g_ref, ln2b_ref,
                o_ref, h2_ref, acc_ref):
    e = pl.program_id(1)

    @pl.when(e == 0)
    def _():
        h2_ref[...] = (x1_ref[...] * (1.0 + sc2_ref[...]) + sh2_ref[...]).astype(BF16)
        acc_ref[...] = jnp.zeros_like(acc_ref)

    h2 = h2_ref[...]
    a = jnp.dot(h2, wg_ref[0].astype(BF16), preferred_element_type=F32)
    u = jnp.dot(h2, wu_ref[0].astype(BF16), preferred_element_type=F32)
    comb = comb_ref[...]
    lane = lax.broadcasted_iota(jnp.int32, comb.shape, 1)
    c_e = jnp.sum(jnp.where(lane == e, comb, 0.0), axis=-1, keepdims=True)
    z = _silu(a) * u * c_e
    acc_ref[...] += jnp.dot(z.astype(BF16), wd_ref[0].astype(BF16), preferred_element_type=F32)

    @pl.when(e == N_EXPERTS - 1)
    def _():
        o_ref[...] = _layer_norm(DEEPNORM_ALPHA * x1_ref[...] + (1.0 + gt2_ref[...]) * acc_ref[...],
                                 ln2g_ref[...], ln2b_ref[...])


def _moe_dense(x1, comb, sh2, sc2, gt2, mod_rows_per_tile, w_gate, w_up, w_down, ln2g, ln2b, tm):
    n, d = x1.shape
    if mod_rows_per_tile:
        mspec = pl.BlockSpec((1, 1, d), lambda i, e: (i // mod_rows_per_tile, 0, 0))
        sh2, sc2, gt2 = (m.reshape(-1, 1, d) for m in (sh2, sc2, gt2))
        kern = lambda x1r, cr, s1, s2, s3, *rest: _moe_kernel(x1r, cr, s1.at[0], s2.at[0], s3.at[0], *rest)
    else:
        mspec = pl.BlockSpec((tm, d), lambda i, e: (i, 0))
        kern = _moe_kernel
    full = lambda shp: pl.BlockSpec(shp, lambda i, e: tuple(0 for _ in shp))
    return pl.pallas_call(
        kern,
        grid=(n // tm, N_EXPERTS),
        in_specs=[pl.BlockSpec((tm, d), lambda i, e: (i, 0)),
                  pl.BlockSpec((tm, ROUTE_LANES), lambda i, e: (i, 0)),
                  mspec, mspec, mspec,
                  pl.BlockSpec((1, d, D_EXPERT), lambda i, e: (e, 0, 0)),
                  pl.BlockSpec((1, d, D_EXPERT), lambda i, e: (e, 0, 0)),
                  pl.BlockSpec((1, D_EXPERT, d), lambda i, e: (e, 0, 0)),
                  full((1, d)), full((1, d))],
        out_specs=pl.BlockSpec((tm, d), lambda i, e: (i, 0)),
        out_shape=jax.ShapeDtypeStruct((n, d), F32),
        scratch_shapes=[pltpu.VMEM((tm, d), BF16), pltpu.VMEM((tm, d), F32)],
        compiler_params=_cparams(("arbitrary", "arbitrary")),
        name="moe_dense_ln",
    )(x1, comb, sh2, sc2, gt2, w_gate, w_up, w_down, ln2g, ln2b)


RB_SUB = 512
RB_NSUB = SEQ // RB_SUB
RB_CHUNK = 128
RB_CHUNK_BITS = 7
RB_NCHUNK = 2 * SEQ // RB_CHUNK
RB_PITCH = RB_CHUNK + 8
RB_SPITCH = RB_SUB + 8
RB_TOK_UNROLL = 8


def _rb_kernel(cnt_ref, x1_ref, mod_ref, offs_ref, wts_ref, wg_hbm, wu_hbm, wd_hbm, ln2g_ref, ln2b_ref,
               o_ref, buf_ref, stage_ref, wg_buf, wu_buf, wd_buf, start_ref, sem):
    b = pl.program_id(0)
    s = pl.program_id(1)

    @pl.when(s == 0)
    def _starts():
        def body(e, run):
            start_ref[e] = run
            return run + cnt_ref[b, e]
        lax.fori_loop(0, N_EXPERTS, body, jnp.int32(0))
        buf_ref[RB_NCHUNK * 8 * RB_PITCH:(RB_NCHUNK + 1) * 8 * RB_PITCH, :] = jnp.zeros((8 * RB_PITCH, LANES), F32)

    @pl.when(s < RB_NSUB)
    def _dispatch():
        sh2 = mod_ref[0, 3:4, :]
        sc2 = mod_ref[0, 4:5, :]
        h2 = x1_ref[...] * (1.0 + sc2) + sh2
        for j in range(8):
            stage_ref[RB_SPITCH * j:RB_SPITCH * j + RB_SUB, :] = h2[:, LANES * j:LANES * (j + 1)]

        def body(i, carry):
            for k in range(RB_TOK_UNROLL):
                t = i * RB_TOK_UNROLL + k
                slab = stage_ref[pl.ds(t, 8, stride=RB_SPITCH), :]
                for a in range(2):
                    buf_ref[pl.ds(offs_ref[0, a, t], 8, stride=RB_PITCH), :] = slab
            return carry
        lax.fori_loop(0, RB_SUB // RB_TOK_UNROLL, body, 0)

    @pl.when(s == RB_NSUB)
    def _experts():
        def copies(e, slot):
            return (pltpu.make_async_copy(wg_hbm.at[e], wg_buf.at[slot], sem.at[slot, 0]),
                    pltpu.make_async_copy(wu_hbm.at[e], wu_buf.at[slot], sem.at[slot, 1]),
                    pltpu.make_async_copy(wd_hbm.at[e], wd_buf.at[slot], sem.at[slot, 2]))

        def run_expert(e, slot):
            lo_row = start_ref[e]
            hi_row = lo_row + cnt_ref[b, e]

            c_lo = lax.shift_right_logical(lo_row, RB_CHUNK_BITS)
            c_hi = lax.shift_right_logical(hi_row + (RB_CHUNK - 1), RB_CHUNK_BITS)
            row = lax.broadcasted_iota(jnp.int32, (RB_CHUNK, 1), 0)

            def load(c):
                base = pl.multiple_of(c * (8 * RB_PITCH), 8)
                return [buf_ref[pl.ds(base + RB_PITCH * j, RB_CHUNK), :] for j in range(8)]

            def ffn(tiles):
                x = jnp.concatenate(tiles, axis=-1).astype(BF16)
                a = jnp.dot(x, wg_buf[slot], preferred_element_type=F32)
                u = jnp.dot(x, wu_buf[slot], preferred_element_type=F32)
                z = (_silu(a) * u).astype(BF16)
                return jnp.dot(z, wd_buf[slot], preferred_element_type=F32)

            def store(c, tiles, y):
                base = pl.multiple_of(c * (8 * RB_PITCH), 8)
                mine = (row >= lo_row - c * RB_CHUNK) & (row < hi_row - c * RB_CHUNK)
                for j in range(8):
                    buf_ref[pl.ds(base + RB_PITCH * j, RB_CHUNK), :] = jnp.where(
                        mine, y[:, LANES * j:LANES * (j + 1)], tiles[j])

            def pair(i, carry):
                c0 = c_lo + 2 * i
                c1 = jnp.where(c0 + 1 < c_hi, c0 + 1, RB_NCHUNK)
                t0 = load(c0)
                t1 = load(c1)
                y0 = ffn(t0)
                y1 = ffn(t1)
                store(c0, t0, y0)
                store(c1, t1, y1)
                return carry

            lax.fori_loop(0, lax.shift_right_logical(c_hi - c_lo + 1, 1), pair, 0)

        for c in copies(0, 0):
            c.start()

        def pair_body(i, carry):
            e0 = 2 * i
            for c in copies(e0 + 1, 1):
                c.start()
            for c in copies(e0, 0):
                c.wait()
            run_expert(e0, 0)

            @pl.when(i < N_EXPERTS // 2 - 1)
            def _():
                for c in copies(e0 + 2, 0):
                    c.start()
            for c in copies(e0 + 1, 1):
                c.wait()
            run_expert(e0 + 1, 1)
            return carry
        lax.fori_loop(0, N_EXPERTS // 2, pair_body, 0)

    @pl.when(s > RB_NSUB)
    def _combine():
        def body(i, carry):
            for k in range(RB_TOK_UNROLL):
                t = i * RB_TOK_UNROLL + k
                acc = None
                for a in range(2):
                    term = wts_ref[0, a, t] * buf_ref[pl.ds(offs_ref[0, a, t], 8, stride=RB_PITCH), :]
                    acc = term if acc is None else acc + term
                stage_ref[pl.ds(t, 8, stride=RB_SPITCH), :] = acc
            return carry
        lax.fori_loop(0, RB_SUB // RB_TOK_UNROLL, body, 0)
        gt2 = mod_ref[0, 5:6, :]
        f = jnp.concatenate([stage_ref[RB_SPITCH * j:RB_SPITCH * j + RB_SUB, :] for j in range(8)], axis=-1)
        o_ref[...] = _layer_norm(DEEPNORM_ALPHA * x1_ref[...] + (1.0 + gt2) * f, ln2g_ref[...], ln2b_ref[...])


def _rb_offsets(cnt, e12, rank12):
    bsz = cnt.shape[0]
    start = jnp.cumsum(cnt, axis=-1) - cnt
    start_t = jnp.repeat(start, RB_NSUB, axis=0)[:, None, None, :]
    hit = e12[..., None] == jnp.arange(N_EXPERTS, dtype=jnp.int32)
    p = jnp.sum(jnp.where(hit, start_t, 0), axis=-1) + rank12
    return lax.shift_right_logical(p, RB_CHUNK_BITS) * (8 * RB_PITCH) + (p & (RB_CHUNK - 1))


def _rb_moe(x1, modp, cnt, offs, wts, wg_bf16, wu_bf16, wd_bf16, ln2g, ln2b):
    n, d = x1.shape
    bsz = n // SEQ
    nsteps = 2 * RB_NSUB + 1

    def sub_index(s):
        return jnp.where(s < RB_NSUB, s, jnp.where(s == RB_NSUB, RB_NSUB - 1, s - RB_NSUB - 1))

    def tile_map(b, s, cnt_r):
        return (b * RB_NSUB + sub_index(s), 0)

    def tile_map3(b, s, cnt_r):
        return (b * RB_NSUB + sub_index(s), 0, 0)

    def out_map(b, s, cnt_r):
        return (b * RB_NSUB + jnp.maximum(s - RB_NSUB - 1, 0), 0)

    const = lambda shp: pl.BlockSpec(shp, lambda b, s, cnt_r: tuple(0 for _ in shp))
    anyspec = pl.BlockSpec(memory_space=pl.ANY)
    grid_spec = pltpu.PrefetchScalarGridSpec(
        num_scalar_prefetch=1,
        grid=(bsz, nsteps),
        in_specs=[pl.BlockSpec((RB_SUB, d), tile_map),
                  pl.BlockSpec((1, 6, d), lambda b, s, cnt_r: (b, 0, 0)),
                  pl.BlockSpec((1, 2, RB_SUB), tile_map3, memory_space=pltpu.SMEM),
                  pl.BlockSpec((1, 2, RB_SUB), tile_map3, memory_space=pltpu.SMEM),
                  anyspec, anyspec, anyspec,
                  const((1, d)), const((1, d))],
        out_specs=pl.BlockSpec((RB_SUB, d), out_map),
        scratch_shapes=[pltpu.VMEM(((RB_NCHUNK + 1) * 8 * RB_PITCH, LANES), F32),
                        pltpu.VMEM((8 * RB_SPITCH, LANES), F32),
                        pltpu.VMEM((2, d, D_EXPERT), BF16),
                        pltpu.VMEM((2, d, D_EXPERT), BF16),
                        pltpu.VMEM((2, D_EXPERT, d), BF16),
                        pltpu.SMEM((N_EXPERTS,), jnp.int32),
                        pltpu.SemaphoreType.DMA((2, 3))])
    return pl.pallas_call(
        _rb_kernel,
        grid_spec=grid_spec,
        out_shape=jax.ShapeDtypeStruct((n, d), F32),
        compiler_params=_cparams(("arbitrary", "arbitrary")),
        name="moe_routed_ln",
    )(cnt, x1, modp, offs, wts, wg_bf16, wu_bf16, wd_bf16, ln2g, ln2b)


def _sample_in_kernel(x_ref, sh1_ref, sc1_ref, win_ref, ctx_ref, h0_ref, convw_ref, convb_ref,
                      wlo_ref, whi_ref, bgate_ref, lam_ref,
                      ylru_ref, q_ref, k_ref, v_ref, cstate_ref, hnew_ref):
    h = x_ref[...] * (1.0 + sc1_ref[...]) + sh1_ref[...]
    z = _dot(h, win_ref[...], True)
    xb = z[:, :LRU_WIDTH]
    gate = z[:, LRU_WIDTH:2 * LRU_WIDTH]
    c0 = ctx_ref[:, 0, :]
    c1 = ctx_ref[:, 1, :]
    c2 = ctx_ref[:, 2, :]
    xc = (convb_ref[...] + convw_ref[0:1, :] * c0 + convw_ref[1:2, :] * c1
          + convw_ref[2:3, :] * c2 + convw_ref[3:4, :] * xb)
    cstate_ref[:, 0, :] = c1
    cstate_ref[:, 1, :] = c2
    cstate_ref[:, 2, :] = xb
    sp = _softplus(-lam_ref[...])
    a, bterm = _lru_gates(xc, wlo_ref[...], whi_ref[...], bgate_ref[...], sp, True)
    hn = a * h0_ref[...] + bterm
    hnew_ref[...] = hn
    ylru_ref[...] = hn * _gelu_tanh(gate)
    low = lax.broadcasted_iota(jnp.int32, (DEC_BATCH, LANES), 1) < HEAD_DIM
    for c in range(4):
        qc = z[:, 2 * LRU_WIDTH + LANES * c:2 * LRU_WIDTH + LANES * (c + 1)]
        q_ref[pl.ds(c, DEC_BATCH, stride=N_HEADS), :] = jnp.where(low, qc, 0.0)
        q_ref[pl.ds(c + 4, DEC_BATCH, stride=N_HEADS), :] = jnp.where(low, 0.0, qc)
    k_ref[...] = z[:, 2 * LRU_WIDTH + ATTN_WIDTH:2 * LRU_WIDTH + ATTN_WIDTH + KV_WIDTH]
    v_ref[...] = z[:, 2 * LRU_WIDTH + ATTN_WIDTH + KV_WIDTH:]


def _sample_in(x, sh1, sc1, w_in_p, ctx, h0, conv_w, conv_b, wlo, whi, bgate, lam):
    n = DEC_BATCH
    outs = [jax.ShapeDtypeStruct((n, LRU_WIDTH), F32),
            jax.ShapeDtypeStruct((n * N_HEADS, LANES), F32),
            jax.ShapeDtypeStruct((n, KV_WIDTH), F32),
            jax.ShapeDtypeStruct((n, KV_WIDTH), F32),
            jax.ShapeDtypeStruct((n, CONV_WIDTH - 1, LRU_WIDTH), F32),
            jax.ShapeDtypeStruct((n, LRU_WIDTH), F32)]
    return pl.pallas_call(
        _sample_in_kernel,
        out_shape=outs,
        compiler_params=pltpu.CompilerParams(vmem_limit_bytes=VMEM_LIMIT),
        name="sample_inproj_rglru",
    )(x, sh1, sc1, w_in_p, ctx, h0, conv_w, conv_b, wlo, whi, bgate, lam)


def _sample_attn_kernel(q_ref, kn_ref, vn_ref, ck_ref, cv_ref, sink_ref, y_ref, nk_ref, nv_ref, *, bb):
    rows = lax.broadcasted_iota(jnp.int32, (WINDOW, KV_WIDTH), 0)
    sink = sink_ref[...]
    for b in range(bb):
        q8 = q_ref[b]
        kb = ck_ref[b]
        vb = cv_ref[b]
        kn = kn_ref[b:b + 1, :]
        vn = vn_ref[b:b + 1, :]
        s = _dot_nt(q8, kb, True) * ATTN_SCALE
        s_self = jnp.sum(q8 * kn, axis=-1, keepdims=True) * ATTN_SCALE
        m = jnp.maximum(jnp.maximum(jnp.max(s, axis=-1, keepdims=True), s_self), sink)
        e = jnp.exp(s - m)
        e_self = jnp.exp(s_self - m)
        den = jnp.sum(e, axis=-1, keepdims=True) + e_self + jnp.exp(sink - m)
        inv = 1.0 / den
        y_ref[b] = _dot(e * inv, vb, True) + (e_self * inv) * vn
        nk_ref[b] = jnp.where(rows == WINDOW - 1, kn, pltpu.roll(kb, WINDOW - 1, axis=0))
        nv_ref[b] = jnp.where(rows == WINDOW - 1, vn, pltpu.roll(vb, WINDOW - 1, axis=0))


def _sample_attn(q3, kn, vn, cache_k, cache_v, sinks, bb=8):
    n = DEC_BATCH
    kern = functools.partial(_sample_attn_kernel, bb=bb)
    return pl.pallas_call(
        kern,
        grid=(n // bb,),
        in_specs=[pl.BlockSpec((bb, N_HEADS, LANES), lambda i: (i, 0, 0)),
                  pl.BlockSpec((bb, KV_WIDTH), lambda i: (i, 0)),
                  pl.BlockSpec((bb, KV_WIDTH), lambda i: (i, 0)),
                  pl.BlockSpec((bb, WINDOW, KV_WIDTH), lambda i: (i, 0, 0)),
                  pl.BlockSpec((bb, WINDOW, KV_WIDTH), lambda i: (i, 0, 0)),
                  pl.BlockSpec((N_HEADS, 1), lambda i: (0, 0))],
        out_specs=[pl.BlockSpec((bb, N_HEADS, LANES), lambda i: (i, 0, 0)),
                   pl.BlockSpec((bb, WINDOW, KV_WIDTH), lambda i: (i, 0, 0)),
                   pl.BlockSpec((bb, WINDOW, KV_WIDTH), lambda i: (i, 0, 0))],
        out_shape=[jax.ShapeDtypeStruct((n, N_HEADS, LANES), F32),
                   jax.ShapeDtypeStruct((n, WINDOW, KV_WIDTH), F32),
                   jax.ShapeDtypeStruct((n, WINDOW, KV_WIDTH), F32)],
        compiler_params=_cparams(("arbitrary",)),
        name="sample_cache_attention",
    )(q3, kn, vn, cache_k, cache_v, sinks.reshape(N_HEADS, 1))


def _sample_out_kernel(x_ref, ylru_ref, yatt_ref, sh2_ref, sc2_ref, gt1_ref, glru_ref, gattn_ref, wout_ref,
                       ln1g_ref, ln1b_ref, wr_ref, br_ref, x1_ref, comb_ref):
    low = lax.broadcasted_iota(jnp.int32, (DEC_BATCH, LANES), 1) < HEAD_DIM
    yatt = jnp.concatenate(
        [jnp.where(low, yatt_ref[pl.ds(c, DEC_BATCH, stride=N_HEADS), :],
                   yatt_ref[pl.ds(c + 4, DEC_BATCH, stride=N_HEADS), :]) for c in range(4)], axis=-1)
    x1, comb = _outproj_body(x_ref[...], ylru_ref[...], yatt, sh2_ref[...], sc2_ref[...], gt1_ref[...],
                             glru_ref[...], gattn_ref[...], wout_ref[...], ln1g_ref[...], ln1b_ref[...],
                             wr_ref[...], br_ref[...], True)
    x1_ref[...] = x1
    comb_ref[...] = comb


def _sample_out(x, ylru, yatt2d, sh2, sc2, gt1, glru, gattn, wout_p, ln1g, ln1b, wr, br):
    n = DEC_BATCH
    return pl.pallas_call(
        _sample_out_kernel,
        out_shape=[jax.ShapeDtypeStruct((n, D_MODEL), F32), jax.ShapeDtypeStruct((n, ROUTE_LANES), F32)],
        compiler_params=pltpu.CompilerParams(vmem_limit_bytes=VMEM_LIMIT),
        name="sample_outproj_ln_route",
    )(x, ylru, yatt2d, sh2, sc2, gt1, glru, gattn, wout_p, ln1g, ln1b, wr, br)


def _block_diag_halves(w_a, w_x):
    def bd(w4):
        eye = jnp.eye(4, dtype=w4.dtype)
        return (w4[:, :, None, :] * eye[:, None, :, None]).reshape(256, 256)
    lo = jnp.concatenate([bd(w_a[:4]), bd(w_x[:4])], axis=1)
    hi = jnp.concatenate([bd(w_a[4:]), bd(w_x[4:])], axis=1)
    return lo, hi


def kernel(x_prompt, x_sample, c_prompt, c_sample, state_conv, state_h, cache_k, cache_v, w_ada, b_ada, w_in,
           conv_w, conv_b, w_rg_a, b_rg_a, w_rg_x, b_rg_x, lru_lambda, sinks, g_lru, g_attn, w_out, ln1_g, ln1_b,
           w_group, b_group, w_router, b_router, w_gate, w_up, w_down, ln2_g, ln2_b):
    d = D_MODEL
    perm = jnp.asarray(HEAD_PERM)
    w_in0 = w_in[0]
    q0 = 2 * LRU_WIDTH
    w_in_p = jnp.concatenate([w_in0[:, :q0], w_in0[:, q0:q0 + ATTN_WIDTH][:, perm], w_in0[:, q0 + ATTN_WIDTH:]],
                             axis=1)
    w_out0 = w_out[0]
    w_out_p = jnp.concatenate([w_out0[:LRU_WIDTH], w_out0[LRU_WIDTH:][perm]], axis=0)
    g_attn_p = g_attn[0][perm].reshape(1, -1)
    glru = g_lru[0].reshape(1, -1)
    wlo, whi = _block_diag_halves(w_rg_a[0], w_rg_x[0])
    bgate = jnp.concatenate([b_rg_a[0].reshape(-1), b_rg_x[0].reshape(-1)]).reshape(1, -1)
    lam = lru_lambda[0].reshape(1, -1)
    convw = conv_w[0]
    convb = conv_b[0].reshape(1, -1)
    ln1g, ln1b = ln1_g[0].reshape(1, -1), ln1_b[0].reshape(1, -1)
    ln2g, ln2b = ln2_g[0].reshape(1, -1), ln2_b[0].reshape(1, -1)
    wr = jnp.concatenate([jnp.transpose(w_router[0], (1, 0, 2)).reshape(d, N_EXPERTS), w_group[0],
                          jnp.zeros((d, ROUTE_LANES - N_EXPERTS - N_GROUPS), F32)], axis=1)
    br = jnp.concatenate([b_router[0].reshape(-1), b_group[0],
                          jnp.zeros((ROUTE_LANES - N_EXPERTS - N_GROUPS,), F32)]).reshape(1, -1)
    sink_p = sinks[0]

    c_all = jnp.concatenate([c_prompt, jnp.zeros((8 - BATCH, d), F32), c_sample], axis=0)
    mod = _ada(c_all, w_ada[0], b_ada[0])
    modp = mod[:BATCH].reshape(BATCH, 6, d)
    mods = mod[8:]
    sh1_s, sc1_s, gt1_s, sh2_s, sc2_s, gt2_s = (mods[:, k * d:(k + 1) * d] for k in range(6))

    zin = _inproj(x_prompt, modp, w_in_p.astype(BF16))
    ylru, cstate8, hlast8 = _lru(zin, convw, convb, wlo.astype(BF16), whi.astype(BF16), bgate, lam)
    yatt = _attn(zin, sink_p)
    n_p = BATCH * SEQ
    x1_p, info, cntf = _outproj_prompt(x_prompt.reshape(n_p, d), ylru.reshape(n_p, LRU_WIDTH),
                                       yatt.reshape(n_p, ATTN_WIDTH), modp, glru, g_attn_p, w_out_p.astype(BF16),
                                       ln1g, ln1b, wr, br, tm=RB_SUB)
    cnt = cntf[:, 0, :N_EXPERTS].astype(jnp.int32)
    offs = _rb_offsets(cnt, info[:, 0:2].astype(jnp.int32), info[:, 4:6].astype(jnp.int32))
    y_p = _rb_moe(x1_p, modp, cnt, offs, info[:, 2:4], w_gate[0].astype(BF16), w_up[0].astype(BF16),
                  w_down[0].astype(BF16), ln2g, ln2b)

    ylru_s, q2d, kn, vn, cstate_s, hnew_s = _sample_in(
        x_sample.reshape(DEC_BATCH, d), sh1_s, sc1_s, w_in_p, state_conv[0], state_h[0],
        convw, convb, wlo, whi, bgate, lam)
    yatt3, newk, newv = _sample_attn(q2d.reshape(DEC_BATCH, N_HEADS, LANES), kn, vn,
                                     cache_k[0].reshape(DEC_BATCH, WINDOW, KV_WIDTH),
                                     cache_v[0].reshape(DEC_BATCH, WINDOW, KV_WIDTH), sink_p)
    x1_s, comb_s = _sample_out(x_sample.reshape(DEC_BATCH, d), ylru_s, yatt3.reshape(DEC_BATCH * N_HEADS, LANES),
                               sh2_s, sc2_s, gt1_s, glru, g_attn_p, w_out_p, ln1g, ln1b, wr, br)
    y_s = _moe_dense(x1_s, comb_s, sh2_s, sc2_s, gt2_s, 0, w_gate[0], w_up[0], w_down[0], ln2g, ln2b, DEC_BATCH)

    kq = 2 * LRU_WIDTH + ATTN_WIDTH
    return (y_p.reshape(BATCH, SEQ, d),
            y_s.reshape(DEC_BATCH, 1, d),
            cstate8[:, 5:8][None],
            hlast8[:, 7][None],
            zin[:, SEQ - WINDOW:, kq:kq + KV_WIDTH].reshape(1, BATCH, WINDOW, N_KV_HEADS, HEAD_DIM),
            zin[:, SEQ - WINDOW:, kq + KV_WIDTH:].reshape(1, BATCH, WINDOW, N_KV_HEADS, HEAD_DIM),
            cstate_s[None],
            hnew_s[None],
            newk.reshape(1, DEC_BATCH, WINDOW, N_KV_HEADS, HEAD_DIM),
            newv.reshape(1, DEC_BATCH, WINDOW, N_KV_HEADS, HEAD_DIM))
```

```python
import functools

import jax
import jax.numpy as jnp
import numpy as np
from jax import lax
from jax.experimental import pallas as pl
from jax.experimental.pallas import tpu as pltpu

F32 = jnp.float32
BF16 = jnp.bfloat16
HIGHEST = lax.Precision.HIGHEST

D_MODEL = 1024
BATCH = 4
SEQ = 4096
DEC_BATCH = 128
LRU_WIDTH = 512
LRU_BLOCKS = 8
LRU_BLOCK = 64
CONV_WIDTH = 4
LRU_C = 8.0
N_HEADS = 8
N_KV_HEADS = 2
HEAD_DIM = 64
ATTN_WIDTH = 512
KV_WIDTH = 128
WINDOW = 128
IN_WIDTH = 2 * LRU_WIDTH + ATTN_WIDTH + 2 * KV_WIDTH
N_GROUPS = 4
EXPERTS_PER_GROUP = 8
N_EXPERTS = 32
D_EXPERT = 256
DEEPNORM_ALPHA = 2.0 ** 0.25
LN_EPS = 1e-5
RMS_EPS = 1e-6
ATTN_SCALE = HEAD_DIM ** -0.5

LANES = 128
ROUTE_LANES = 128
ROUTE_INFO = 40
VMEM_LIMIT = 56 * 1024 * 1024

HEAD_PERM = np.concatenate(
    [np.concatenate([np.arange(64 * c, 64 * c + 64), np.arange(64 * (c + 4), 64 * (c + 4) + 64)])
     for c in range(4)])


def _cparams(sem):
    return pltpu.CompilerParams(dimension_semantics=sem, vmem_limit_bytes=VMEM_LIMIT)


def _dot(a, b, exact):
    if exact:
        return jnp.dot(a, b, precision=HIGHEST, preferred_element_type=F32)
    return jnp.dot(a.astype(BF16), b.astype(BF16), preferred_element_type=F32)


def _dot_nt(a, b, exact):
    dn = (((1,), (1,)), ((), ()))
    if exact:
        return lax.dot_general(a, b, dn, precision=HIGHEST, preferred_element_type=F32)
    return lax.dot_general(a.astype(BF16), b.astype(BF16), dn, preferred_element_type=F32)


def _sigmoid(x):
    return 1.0 / (1.0 + jnp.exp(-x))


def _silu(x):
    return x * _sigmoid(x)


def _gelu_tanh(x):
    return 0.5 * x * (1.0 + jnp.tanh(np.sqrt(2.0 / np.pi).astype(np.float32) * (x + 0.044715 * (x * x * x))))


def _softplus(x):
    return jnp.maximum(x, 0.0) + jnp.log1p(jnp.exp(-jnp.abs(x)))


def _layer_norm(x, g, b):
    mu = jnp.mean(x, axis=-1, keepdims=True)
    xc = x - mu
    var = jnp.mean(xc * xc, axis=-1, keepdims=True)
    return xc * lax.rsqrt(var + LN_EPS) * g + b


def _rms_norm(x, g):
    return x * lax.rsqrt(jnp.mean(x * x, axis=-1, keepdims=True) + RMS_EPS) * g


def _ada_kernel(c_ref, w_ref, b_ref, o_ref):
    o_ref[...] = _dot(_silu(c_ref[...]), w_ref[...], True) + b_ref[...]


def _ada(c_all, w_ada, b_ada):
    rows = c_all.shape[0]
    bn = 512
    return pl.pallas_call(
        _ada_kernel,
        grid=(6 * D_MODEL // bn,),
        in_specs=[pl.BlockSpec((rows, D_MODEL), lambda j: (0, 0)),
                  pl.BlockSpec((D_MODEL, bn), lambda j: (0, j)),
                  pl.BlockSpec((1, bn), lambda j: (0, j))],
        out_specs=pl.BlockSpec((rows, bn), lambda j: (0, j)),
        out_shape=jax.ShapeDtypeStruct((rows, 6 * D_MODEL), F32),
        compiler_params=_cparams(("arbitrary",)),
        name="ada_modulation",
    )(c_all, w_ada, b_ada.reshape(1, -1))


def _inproj_kernel(x_ref, mod_ref, w_ref, o_ref):
    sh1 = mod_ref[0, 0:1, :]
    sc1 = mod_ref[0, 1:2, :]
    h = x_ref[0] * (1.0 + sc1) + sh1
    o_ref[0] = _dot(h, w_ref[...], False)


def _inproj(x, modp, w_in_bf16, tm=512):
    b, t, d = x.shape
    return pl.pallas_call(
        _inproj_kernel,
        grid=(b, t // tm),
        in_specs=[pl.BlockSpec((1, tm, d), lambda i, j: (i, j, 0)),
                  pl.BlockSpec((1, 6, d), lambda i, j: (i, 0, 0)),
                  pl.BlockSpec((d, IN_WIDTH), lambda i, j: (0, 0))],
        out_specs=pl.BlockSpec((1, tm, IN_WIDTH), lambda i, j: (i, j, 0)),
        out_shape=jax.ShapeDtypeStruct((b, t, IN_WIDTH), F32),
        compiler_params=_cparams(("arbitrary", "arbitrary")),
        name="prompt_inproj",
    )(x, modp, w_in_bf16)


def _lru_gates(xc, wlo, whi, bgate, sp_neg_lam, exact):
    g_lo = _dot(xc[:, :256], wlo, exact)
    g_hi = _dot(xc[:, 256:], whi, exact)
    ga = jnp.concatenate([g_lo[:, :256], g_hi[:, :256]], axis=-1) + bgate[:, :LRU_WIDTH]
    gx = jnp.concatenate([g_lo[:, 256:], g_hi[:, 256:]], axis=-1) + bgate[:, LRU_WIDTH:]
    r = _sigmoid(ga)
    i = _sigmoid(gx)
    log_a = -LRU_C * r * sp_neg_lam
    a = jnp.exp(log_a)
    one_minus_a2 = -jnp.tanh(log_a) * (a * a + 1.0)
    bterm = jnp.sqrt(one_minus_a2) * (i * xc)
    return a, bterm


def _lru_kernel(z_ref, convw_ref, convb_ref, wlo_ref, whi_ref, bgate_ref, lam_ref,
                y_ref, cstate_ref, hlast_ref, tail_ref, carry_ref, *, tl):
    j = pl.program_id(1)

    @pl.when(j == 0)
    def _():
        tail_ref[...] = jnp.zeros_like(tail_ref)
        carry_ref[...] = jnp.zeros_like(carry_ref)

    xb = z_ref[0, :, :LRU_WIDTH]
    gate = z_ref[0, :, LRU_WIDTH:]
    rows = lax.broadcasted_iota(jnp.int32, (tl, LRU_WIDTH), 0)

    xc = convb_ref[...] + convw_ref[3:4, :] * xb
    rows8 = lax.broadcasted_iota(jnp.int32, (8, LRU_WIDTH), 0)
    tail = tail_ref[...]
    for back in (1, 2, 3):
        rolled = pltpu.roll(xb, back, axis=0)
        top = jnp.where(rows8 >= back, rolled[:8], pltpu.roll(tail, back, axis=0))
        shifted = jnp.concatenate([top, rolled[8:]], axis=0)
        xc = xc + convw_ref[3 - back:4 - back, :] * shifted
    tail_ref[...] = xb[tl - 8:, :]
    cstate_ref[0] = xb[tl - 8:, :]

    sp = _softplus(-lam_ref[...])
    a, bterm = _lru_gates(xc, wlo_ref[...], whi_ref[...], bgate_ref[...], sp, False)

    s = 1
    while s < tl:
        a_sh = jnp.where(rows >= s, pltpu.roll(a, s, axis=0), 1.0)
        b_sh = jnp.where(rows >= s, pltpu.roll(bterm, s, axis=0), 0.0)
        bterm = a * b_sh + bterm
        a = a * a_sh
        s *= 2
    h = a * carry_ref[7:8, :] + bterm
    carry_ref[...] = h[tl - 8:, :]
    hlast_ref[0] = h[tl - 8:, :]
    y_ref[0] = h * _gelu_tanh(gate)


def _lru(zin, conv_w, conv_b, wlo, whi, bgate, lam, tl=512):
    b, t, _ = zin.shape
    kern = functools.partial(_lru_kernel, tl=tl)
    full = lambda shp: pl.BlockSpec(shp, lambda i, j: tuple(0 for _ in shp))
    return pl.pallas_call(
        kern,
        grid=(b, t // tl),
        in_specs=[pl.BlockSpec((1, tl, 2 * LRU_WIDTH), lambda i, j: (i, j, 0)),
                  full((CONV_WIDTH, LRU_WIDTH)), full((1, LRU_WIDTH)),
                  full((256, 512)), full((256, 512)), full((1, 2 * LRU_WIDTH)), full((1, LRU_WIDTH))],
        out_specs=[pl.BlockSpec((1, tl, LRU_WIDTH), lambda i, j: (i, j, 0)),
                   pl.BlockSpec((1, 8, LRU_WIDTH), lambda i, j: (i, 0, 0)),
                   pl.BlockSpec((1, 8, LRU_WIDTH), lambda i, j: (i, 0, 0))],
        out_shape=[jax.ShapeDtypeStruct((b, t, LRU_WIDTH), F32),
                   jax.ShapeDtypeStruct((b, 8, LRU_WIDTH), F32),
                   jax.ShapeDtypeStruct((b, 8, LRU_WIDTH), F32)],
        scratch_shapes=[pltpu.VMEM((8, LRU_WIDTH), F32), pltpu.VMEM((8, LRU_WIDTH), F32)],
        compiler_params=_cparams(("arbitrary", "arbitrary")),
        name="prompt_rglru",
    )(zin, conv_w, conv_b, wlo, whi, bgate, lam)


ATTN_BLOCKS = 4


def _attn_kernel(q_ref, k_ref, v_ref, sink_ref, o_ref, kprev_ref, vprev_ref):
    j = pl.program_id(1)

    @pl.when(j == 0)
    def _():
        kprev_ref[...] = jnp.zeros_like(kprev_ref)
        vprev_ref[...] = jnp.zeros_like(vprev_ref)

    blk = WINDOW
    lane = lax.broadcasted_iota(jnp.int32, (blk, LANES), 1)
    low = lane < HEAD_DIM
    qi = lax.broadcasted_iota(jnp.int32, (blk, 2 * blk), 0)
    sj = lax.broadcasted_iota(jnp.int32, (blk, 2 * blk), 1)
    rel = blk + qi - sj
    in_window = (rel >= 0) & (rel <= WINDOW)
    sink = sink_ref[...].reshape(N_HEADS, blk, 1)
    k_ext = jnp.concatenate([kprev_ref[...], k_ref[0]], axis=0).astype(BF16)
    v_ext = jnp.concatenate([vprev_ref[...], v_ref[0]], axis=0).astype(BF16)
    for n in range(ATTN_BLOCKS):
        q = q_ref[0, blk * n:blk * (n + 1), :]
        pieces = []
        for half in (0, 1):
            for c in range(4):
                qc = q[:, LANES * c:LANES * (c + 1)]
                pieces.append(jnp.where(low if half == 0 else ~low, qc, 0.0).astype(BF16))
        q8 = jnp.concatenate(pieces, axis=0)
        k_band = k_ext[blk * n:blk * (n + 2)]
        v_band = v_ext[blk * n:blk * (n + 2)]
        s = _dot_nt(q8, k_band, False) * ATTN_SCALE
        s = s.reshape(N_HEADS, blk, 2 * blk)
        valid = in_window & ((sj >= blk) | (j > 0)) if n == 0 else in_window
        s = jnp.where(valid[None], s, -jnp.inf)
        m = jnp.maximum(jnp.max(s, axis=-1, keepdims=True), sink)
        e = jnp.exp(s - m)
        den = jnp.sum(e, axis=-1, keepdims=True) + jnp.exp(sink - m)
        p = (e * (1.0 / den)).reshape(N_HEADS * blk, 2 * blk)
        o8 = _dot(p, v_band, False)
        cols = []
        for c in range(4):
            cols.append(jnp.where(low, o8[blk * c:blk * (c + 1)], o8[blk * (c + 4):blk * (c + 5)]))
        o_ref[0, blk * n:blk * (n + 1), :] = jnp.concatenate(cols, axis=-1)
    kprev_ref[...] = k_ref[0, blk * (ATTN_BLOCKS - 1):, :]
    vprev_ref[...] = v_ref[0, blk * (ATTN_BLOCKS - 1):, :]


def _attn(zin, sinks):
    b, t, _ = zin.shape
    blk = WINDOW
    tq = blk * ATTN_BLOCKS
    sink_col = jnp.repeat(sinks.astype(F32), blk).reshape(N_HEADS * blk, 1)
    return pl.pallas_call(
        _attn_kernel,
        grid=(b, t // tq),
        in_specs=[pl.BlockSpec((1, tq, ATTN_WIDTH), lambda i, j: (i, j, 2)),
                  pl.BlockSpec((1, tq, KV_WIDTH), lambda i, j: (i, j, 12)),
                  pl.BlockSpec((1, tq, KV_WIDTH), lambda i, j: (i, j, 13)),
                  pl.BlockSpec((N_HEADS * blk, 1), lambda i, j: (0, 0))],
        out_specs=pl.BlockSpec((1, tq, ATTN_WIDTH), lambda i, j: (i, j, 0)),
        out_shape=jax.ShapeDtypeStruct((b, t, ATTN_WIDTH), F32),
        scratch_shapes=[pltpu.VMEM((blk, KV_WIDTH), F32), pltpu.VMEM((blk, KV_WIDTH), F32)],
        compiler_params=_cparams(("arbitrary", "arbitrary")),
        name="prompt_window_attention",
    )(zin, zin, zin, sink_col)


def _dot_split3(a, b):
    a_hi = a.astype(BF16)
    b_hi = b.astype(BF16)
    a_lo = (a - a_hi.astype(F32)).astype(BF16)
    b_lo = (b - b_hi.astype(F32)).astype(BF16)
    return (jnp.dot(a_hi, b_hi, preferred_element_type=F32) + jnp.dot(a_hi, b_lo, preferred_element_type=F32)
            + jnp.dot(a_lo, b_hi, preferred_element_type=F32))


def _route(h2, wr, br, exact):
    t = h2.shape[0]
    logits = (_dot(h2, wr, True) if exact else _dot_split3(h2, wr)) + br
    lane = lax.broadcasted_iota(jnp.int32, (t, ROUTE_LANES), 1).astype(F32)
    neg = -jnp.inf
    big = float(ROUTE_LANES)
    is_g = (lane >= N_EXPERTS) & (lane < N_EXPERTS + N_GROUPS)
    lg = jnp.where(is_g, logits, neg)
    mg = jnp.max(lg, axis=-1, keepdims=True)
    eg = jnp.where(is_g, jnp.exp(lg - mg), 0.0)
    pg = eg / jnp.sum(eg, axis=-1, keepdims=True)
    g_val = jnp.max(pg, axis=-1, keepdims=True)
    g_lane = jnp.min(jnp.where((pg == g_val) & is_g, lane, big), axis=-1, keepdims=True)
    g_idx = g_lane - N_EXPERTS
    in_grp = (lane >= g_idx * EXPERTS_PER_GROUP) & (lane < (g_idx + 1.0) * EXPERTS_PER_GROUP)
    le = jnp.where(in_grp, logits, neg)
    me = jnp.max(le, axis=-1, keepdims=True)
    ee = jnp.where(in_grp, jnp.exp(le - me), 0.0)
    pe = ee / jnp.sum(ee, axis=-1, keepdims=True)
    v1 = jnp.max(pe, axis=-1, keepdims=True)
    l1 = jnp.min(jnp.where((pe == v1) & in_grp, lane, big), axis=-1, keepdims=True)
    rest = in_grp & (lane != l1)
    pe2 = jnp.where(rest, pe, -1.0)
    v2 = jnp.max(pe2, axis=-1, keepdims=True)
    l2 = jnp.min(jnp.where((pe2 == v2) & rest, lane, big), axis=-1, keepdims=True)
    tot = v1 + v2
    w1 = g_val * v1 / tot
    w2 = g_val * v2 / tot
    comb = jnp.where(lane == l1, w1, 0.0) + jnp.where(lane == l2, w2, 0.0)
    return (comb + jnp.where(lane == ROUTE_INFO, l1, 0.0) + jnp.where(lane == ROUTE_INFO + 1, l2, 0.0)
            + jnp.where(lane == ROUTE_INFO + 2, w1, 0.0) + jnp.where(lane == ROUTE_INFO + 3, w2, 0.0))


def _outproj_body(x, ylru, yatt, sh2, sc2, gt1, glru, gattn, wout, ln1g, ln1b, wr, br, exact):
    mixin = jnp.concatenate([_rms_norm(ylru, glru), _rms_norm(yatt, gattn)], axis=-1)
    mix = _dot(mixin, wout, exact)
    x1 = _layer_norm(DEEPNORM_ALPHA * x + (1.0 + gt1) * mix, ln1g, ln1b)
    h2 = x1 * (1.0 + sc2) + sh2
    return x1, _route(h2, wr, br, exact)


def _outproj_prompt_kernel(x_ref, ylru_ref, yatt_ref, mod_ref, glru_ref, gattn_ref, wout_ref,
                           ln1g_ref, ln1b_ref, wr_ref, br_ref, x1_ref, info_ref, cnt_ref, tri_ref, carry_ref,
                           *, tm, per_seq):
    i = pl.program_id(0)

    @pl.when(i == 0)
    def _():
        r = lax.broadcasted_iota(jnp.int32, (tm, tm), 0)
        c = lax.broadcasted_iota(jnp.int32, (tm, tm), 1)
        tri_ref[...] = jnp.where(c < r, 1.0, 0.0).astype(BF16)

    @pl.when(i % per_seq == 0)
    def _():
        carry_ref[...] = jnp.zeros_like(carry_ref)

    gt1 = mod_ref[0, 2:3, :]
    sh2 = mod_ref[0, 3:4, :]
    sc2 = mod_ref[0, 4:5, :]
    combs = []
    nsplit = 2
    for h in range(nsplit):
        rows = slice(h * (tm // nsplit), (h + 1) * (tm // nsplit))
        x1_h, comb_h = _outproj_body(x_ref[rows, :], ylru_ref[rows, :], yatt_ref[rows, :], sh2, sc2, gt1,
                                     glru_ref[...], gattn_ref[...], wout_ref[...], ln1g_ref[...], ln1b_ref[...],
                                     wr_ref[...], br_ref[...], False)
        x1_ref[rows, :] = x1_h
        combs.append(comb_h)
    comb = jnp.concatenate(combs, axis=0)
    lane = lax.broadcasted_iota(jnp.int32, (tm, ROUTE_LANES), 1).astype(F32)
    l1 = jnp.sum(jnp.where(lane == ROUTE_INFO, comb, 0.0), axis=-1, keepdims=True)
    l2 = jnp.sum(jnp.where(lane == ROUTE_INFO + 1, comb, 0.0), axis=-1, keepdims=True)
    o1 = lane == l1
    o2 = lane == l2
    onehot = jnp.where(o1 | o2, 1.0, 0.0)
    before = jnp.dot(tri_ref[...], onehot.astype(BF16), preferred_element_type=F32) + carry_ref[0:1, :]
    rank1 = jnp.sum(jnp.where(o1, before, 0.0), axis=-1, keepdims=True)
    rank2 = jnp.sum(jnp.where(o2, before, 0.0), axis=-1, keepdims=True)
    total = carry_ref[0:1, :] + jnp.sum(onehot, axis=0, keepdims=True)
    carry_ref[...] = jnp.broadcast_to(total, carry_ref.shape)
    cnt_ref[0] = jnp.broadcast_to(total, (8, ROUTE_LANES))
    info = (comb + jnp.where(lane == ROUTE_INFO + 4, rank1, 0.0) + jnp.where(lane == ROUTE_INFO + 5, rank2, 0.0))
    info_ref[0] = jnp.transpose(info)[ROUTE_INFO:ROUTE_INFO + 8, :]


def _outproj_prompt(x2d, ylru2d, yatt2d, modp, glru, gattn, wout_bf16, ln1g, ln1b, wr, br, tm=512):
    n, d = x2d.shape
    per_seq = SEQ // tm
    full = lambda shp: pl.BlockSpec(shp, lambda i: tuple(0 for _ in shp))
    kern = functools.partial(_outproj_prompt_kernel, tm=tm, per_seq=per_seq)
    return pl.pallas_call(
        kern,
        grid=(n // tm,),
        in_specs=[pl.BlockSpec((tm, d), lambda i: (i, 0)),
                  pl.BlockSpec((tm, LRU_WIDTH), lambda i: (i, 0)),
                  pl.BlockSpec((tm, ATTN_WIDTH), lambda i: (i, 0)),
                  pl.BlockSpec((1, 6, d), lambda i: (i // per_seq, 0, 0)),
                  full((1, LRU_WIDTH)), full((1, ATTN_WIDTH)), full((d, d)),
                  full((1, d)), full((1, d)), full((d, ROUTE_LANES)), full((1, ROUTE_LANES))],
        out_specs=[pl.BlockSpec((tm, d), lambda i: (i, 0)),
                   pl.BlockSpec((1, 8, tm), lambda i: (i, 0, 0)),
                   pl.BlockSpec((1, 8, ROUTE_LANES), lambda i: (i // per_seq, 0, 0))],
        out_shape=[jax.ShapeDtypeStruct((n, d), F32),
                   jax.ShapeDtypeStruct((n // tm, 8, tm), F32),
                   jax.ShapeDtypeStruct((n // SEQ, 8, ROUTE_LANES), F32)],
        scratch_shapes=[pltpu.VMEM((tm, tm), BF16), pltpu.VMEM((8, ROUTE_LANES), F32)],
        compiler_params=_cparams(("arbitrary",)),
        name="prompt_outproj_ln_route",
    )(x2d, ylru2d, yatt2d, modp, glru, gattn, wout_bf16, ln1g, ln1b, wr, br)


def _moe_kernel(x1_ref, comb_ref, sh2_ref, sc2_ref, gt2_ref, wg_ref, wu_ref, wd_ref, ln2g_ref, ln2b_ref,
                o_ref, h2_ref, acc_ref):
    e = pl.program_id(1)

    @pl.when(e == 0)
    def _():
        h2_ref[...] = (x1_ref[...] * (1.0 + sc2_ref[...]) + sh2_ref[...]).astype(BF16)
        acc_ref[...] = jnp.zeros_like(acc_ref)

    h2 = h2_ref[...]
    a = jnp.dot(h2, wg_ref[0].astype(BF16), preferred_element_type=F32)
    u = jnp.dot(h2, wu_ref[0].astype(BF16), preferred_element_type=F32)
    comb = comb_ref[...]
    lane = lax.broadcasted_iota(jnp.int32, comb.shape, 1)
    c_e = jnp.sum(jnp.where(lane == e, comb, 0.0), axis=-1, keepdims=True)
    z = _silu(a) * u * c_e
    acc_ref[...] += jnp.dot(z.astype(BF16), wd_ref[0].astype(BF16), preferred_element_type=F32)

    @pl.when(e == N_EXPERTS - 1)
    def _():
        o_ref[...] = _layer_norm(DEEPNORM_ALPHA * x1_ref[...] + (1.0 + gt2_ref[...]) * acc_ref[...],
                                 ln2g_ref[...], ln2b_ref[...])


def _moe_dense(x1, comb, sh2, sc2, gt2, mod_rows_per_tile, w_gate, w_up, w_down, ln2g, ln2b, tm):
    n, d = x1.shape
    if mod_rows_per_tile:
        mspec = pl.BlockSpec((1, 1, d), lambda i, e: (i // mod_rows_per_tile, 0, 0))
        sh2, sc2, gt2 = (m.reshape(-1, 1, d) for m in (sh2, sc2, gt2))
        kern = lambda x1r, cr, s1, s2, s3, *rest: _moe_kernel(x1r, cr, s1.at[0], s2.at[0], s3.at[0], *rest)
    else:
        mspec = pl.BlockSpec((tm, d), lambda i, e: (i, 0))
        kern = _moe_kernel
    full = lambda shp: pl.BlockSpec(shp, lambda i, e: tuple(0 for _ in shp))
    return pl.pallas_call(
        kern,
        grid=(n // tm, N_EXPERTS),
        in_specs=[pl.BlockSpec((tm, d), lambda i, e: (i, 0)),
                  pl.BlockSpec((tm, ROUTE_LANES), lambda i, e: (i, 0)),
                  mspec, mspec, mspec,
                  pl.BlockSpec((1, d, D_EXPERT), lambda i, e: (e, 0, 0)),
                  pl.BlockSpec((1, d, D_EXPERT), lambda i, e: (e, 0, 0)),
                  pl.BlockSpec((1, D_EXPERT, d), lambda i, e: (e, 0, 0)),
                  full((1, d)), full((1, d))],
        out_specs=pl.BlockSpec((tm, d), lambda i, e: (i, 0)),
        out_shape=jax.ShapeDtypeStruct((n, d), F32),
        scratch_shapes=[pltpu.VMEM((tm, d), BF16), pltpu.VMEM((tm, d), F32)],
        compiler_params=_cparams(("arbitrary", "arbitrary")),
        name="moe_dense_ln",
    )(x1, comb, sh2, sc2, gt2, w_gate, w_up, w_down, ln2g, ln2b)


RB_SUB = 512
RB_NSUB = SEQ // RB_SUB
RB_CHUNK = 128
RB_CHUNK_BITS = 7
RB_NCHUNK = 2 * SEQ // RB_CHUNK
RB_PITCH = RB_CHUNK + 8
RB_SPITCH = RB_SUB + 8
RB_GROUP = 3


def _rb_kernel(cnt_ref, x1_ref, mod_ref, offs_ref, wts_ref, wg_hbm, wu_hbm, wd_hbm, ln2g_ref, ln2b_ref,
               o_ref, buf_ref, stage_ref, wg_buf, wu_buf, wd_buf, start_ref, sem):
    b = pl.program_id(0)
    s = pl.program_id(1)

    @pl.when(s == 0)
    def _starts():
        def body(e, run):
            start_ref[e] = run
            return run + cnt_ref[b, e]
        lax.fori_loop(0, N_EXPERTS, body, jnp.int32(0))
        buf_ref[RB_NCHUNK * 8 * RB_PITCH:(RB_NCHUNK + RB_GROUP) * 8 * RB_PITCH, :] = jnp.zeros(
            (RB_GROUP * 8 * RB_PITCH, LANES), F32)

    @pl.when(s < RB_NSUB)
    def _dispatch():
        sh2 = mod_ref[0, 3:4, :]
        sc2 = mod_ref[0, 4:5, :]
        h2 = x1_ref[...] * (1.0 + sc2) + sh2
        for j in range(8):
            stage_ref[RB_SPITCH * j:RB_SPITCH * j + RB_SUB, :] = h2[:, LANES * j:LANES * (j + 1)]

        for t in range(RB_SUB):
            slab = stage_ref[pl.ds(t, 8, stride=RB_SPITCH), :]
            for a in range(2):
                buf_ref[pl.ds(offs_ref[0, a, t], 8, stride=RB_PITCH), :] = slab

    @pl.when(s == RB_NSUB)
    def _experts():
        def copies(e, slot):
            return (pltpu.make_async_copy(wg_hbm.at[e], wg_buf.at[slot], sem.at[slot, 0]),
                    pltpu.make_async_copy(wu_hbm.at[e], wu_buf.at[slot], sem.at[slot, 1]),
                    pltpu.make_async_copy(wd_hbm.at[e], wd_buf.at[slot], sem.at[slot, 2]))

        def run_expert(e, slot):
            lo_row = start_ref[e]
            hi_row = lo_row + cnt_ref[b, e]

            c_lo = lax.shift_right_logical(lo_row, RB_CHUNK_BITS)
            c_hi = lax.shift_right_logical(hi_row + (RB_CHUNK - 1), RB_CHUNK_BITS)
            row = lax.broadcasted_iota(jnp.int32, (RB_CHUNK, 1), 0)

            def load(c):
                base = pl.multiple_of(c * (8 * RB_PITCH), 8)
                return [buf_ref[pl.ds(base + RB_PITCH * j, RB_CHUNK), :] for j in range(8)]

            def store(c, tiles, y):
                base = pl.multiple_of(c * (8 * RB_PITCH), 8)
                mine = (row >= lo_row - c * RB_CHUNK) & (row < hi_row - c * RB_CHUNK)
                for j in range(8):
                    buf_ref[pl.ds(base + RB_PITCH * j, RB_CHUNK), :] = jnp.where(
                        mine, y[:, LANES * j:LANES * (j + 1)], tiles[j])

            def group(i, carry):
                cs = [c_lo + RB_GROUP * i]
                for k in range(1, RB_GROUP):
                    cs.append(jnp.where(cs[0] + k < c_hi, cs[0] + k, RB_NCHUNK + k))
                tiles = [load(c) for c in cs]
                x = jnp.concatenate([jnp.concatenate(t, axis=-1) for t in tiles], axis=0).astype(BF16)
                a = jnp.dot(x, wg_buf[slot], preferred_element_type=F32)
                u = jnp.dot(x, wu_buf[slot], preferred_element_type=F32)
                z = (_silu(a) * u).astype(BF16)
                y = jnp.dot(z, wd_buf[slot], preferred_element_type=F32)
                for k, c in enumerate(cs):
                    store(c, tiles[k], y[RB_CHUNK * k:RB_CHUNK * (k + 1)])
                return carry

            lax.fori_loop(0, lax.div(c_hi - c_lo + (RB_GROUP - 1), RB_GROUP), group, 0)

        for c in copies(0, 0):
            c.start()

        def pair_body(i, carry):
            e0 = 2 * i
            for c in copies(e0 + 1, 1):
                c.start()
            for c in copies(e0, 0):
                c.wait()
            run_expert(e0, 0)

            @pl.when(i < N_EXPERTS // 2 - 1)
            def _():
                for c in copies(e0 + 2, 0):
                    c.start()
            for c in copies(e0 + 1, 1):
                c.wait()
            run_expert(e0 + 1, 1)
            return carry
        lax.fori_loop(0, N_EXPERTS // 2, pair_body, 0)

    @pl.when(s > RB_NSUB)
    def _combine():
        for t in range(RB_SUB):
            acc = None
            for a in range(2):
                term = wts_ref[0, a, t] * buf_ref[pl.ds(offs_ref[0, a, t], 8, stride=RB_PITCH), :]
                acc = term if acc is None else acc + term
            stage_ref[pl.ds(t, 8, stride=RB_SPITCH), :] = acc
        gt2 = mod_ref[0, 5:6, :]
        f = jnp.concatenate([stage_ref[RB_SPITCH * j:RB_SPITCH * j + RB_SUB, :] for j in range(8)], axis=-1)
        o_ref[...] = _layer_norm(DEEPNORM_ALPHA * x1_ref[...] + (1.0 + gt2) * f, ln2g_ref[...], ln2b_ref[...])


def _rb_offsets(cnt, e12, rank12):
    bsz = cnt.shape[0]
    start = jnp.cumsum(cnt, axis=-1) - cnt
    start_t = jnp.repeat(start, RB_NSUB, axis=0)[:, None, None, :]
    hit = e12[..., None] == jnp.arange(N_EXPERTS, dtype=jnp.int32)
    p = jnp.sum(jnp.where(hit, start_t, 0), axis=-1) + rank12
    return lax.shift_right_logical(p, RB_CHUNK_BITS) * (8 * RB_PITCH) + (p & (RB_CHUNK - 1))


def _rb_moe(x1, modp, cnt, offs, wts, wg_bf16, wu_bf16, wd_bf16, ln2g, ln2b):
    n, d = x1.shape
    bsz = n // SEQ
    nsteps = 2 * RB_NSUB + 1

    def sub_index(s):
        return jnp.where(s < RB_NSUB, s, jnp.where(s == RB_NSUB, RB_NSUB - 1, s - RB_NSUB - 1))

    def tile_map(b, s, cnt_r):
        return (b * RB_NSUB + sub_index(s), 0)

    def tile_map3(b, s, cnt_r):
        return (b * RB_NSUB + sub_index(s), 0, 0)

    def out_map(b, s, cnt_r):
        return (b * RB_NSUB + jnp.maximum(s - RB_NSUB - 1, 0), 0)

    const = lambda shp: pl.BlockSpec(shp, lambda b, s, cnt_r: tuple(0 for _ in shp))
    anyspec = pl.BlockSpec(memory_space=pl.ANY)
    grid_spec = pltpu.PrefetchScalarGridSpec(
        num_scalar_prefetch=1,
        grid=(bsz, nsteps),
        in_specs=[pl.BlockSpec((RB_SUB, d), tile_map),
                  pl.BlockSpec((1, 6, d), lambda b, s, cnt_r: (b, 0, 0)),
                  pl.BlockSpec((1, 2, RB_SUB), tile_map3, memory_space=pltpu.SMEM),
                  pl.BlockSpec((1, 2, RB_SUB), tile_map3, memory_space=pltpu.SMEM),
                  anyspec, anyspec, anyspec,
                  const((1, d)), const((1, d))],
        out_specs=pl.BlockSpec((RB_SUB, d), out_map),
        scratch_shapes=[pltpu.VMEM(((RB_NCHUNK + RB_GROUP) * 8 * RB_PITCH, LANES), F32),
                        pltpu.VMEM((8 * RB_SPITCH, LANES), F32),
                        pltpu.VMEM((2, d, D_EXPERT), BF16),
                        pltpu.VMEM((2, d, D_EXPERT), BF16),
                        pltpu.VMEM((2, D_EXPERT, d), BF16),
                        pltpu.SMEM((N_EXPERTS,), jnp.int32),
                        pltpu.SemaphoreType.DMA((2, 3))])
    return pl.pallas_call(
        _rb_kernel,
        grid_spec=grid_spec,
        out_shape=jax.ShapeDtypeStruct((n, d), F32),
        compiler_params=_cparams(("arbitrary", "arbitrary")),
        name="moe_routed_ln",
    )(cnt, x1, modp, offs, wts, wg_bf16, wu_bf16, wd_bf16, ln2g, ln2b)


def _sample_in_kernel(x_ref, sh1_ref, sc1_ref, win_ref, ctx_ref, h0_ref, convw_ref, convb_ref,
                      wlo_ref, whi_ref, bgate_ref, lam_ref,
                      ylru_ref, q_ref, k_ref, v_ref, cstate_ref, hnew_ref):
    h = x_ref[...] * (1.0 + sc1_ref[...]) + sh1_ref[...]
    z = _dot(h, win_ref[...], True)
    xb = z[:, :LRU_WIDTH]
    gate = z[:, LRU_WIDTH:2 * LRU_WIDTH]
    c0 = ctx_ref[:, 0, :]
    c1 = ctx_ref[:, 1, :]
    c2 = ctx_ref[:, 2, :]
    xc = (convb_ref[...] + convw_ref[0:1, :] * c0 + convw_ref[1:2, :] * c1
          + convw_ref[2:3, :] * c2 + convw_ref[3:4, :] * xb)
    cstate_ref[:, 0, :] = c1
    cstate_ref[:, 1, :] = c2
    cstate_ref[:, 2, :] = xb
    sp = _softplus(-lam_ref[...])
    a, bterm = _lru_gates(xc, wlo_ref[...], whi_ref[...], bgate_ref[...], sp, True)
    hn = a * h0_ref[...] + bterm
    hnew_ref[...] = hn
    ylru_ref[...] = hn * _gelu_tanh(gate)
    low = lax.broadcasted_iota(jnp.int32, (DEC_BATCH, LANES), 1) < HEAD_DIM
    for c in range(4):
        qc = z[:, 2 * LRU_WIDTH + LANES * c:2 * LRU_WIDTH + LANES * (c + 1)]
        q_ref[pl.ds(c, DEC_BATCH, stride=N_HEADS), :] = jnp.where(low, qc, 0.0)
        q_ref[pl.ds(c + 4, DEC_BATCH, stride=N_HEADS), :] = jnp.where(low, 0.0, qc)
    k_ref[...] = z[:, 2 * LRU_WIDTH + ATTN_WIDTH:2 * LRU_WIDTH + ATTN_WIDTH + KV_WIDTH]
    v_ref[...] = z[:, 2 * LRU_WIDTH + ATTN_WIDTH + KV_WIDTH:]


def _sample_in(x, sh1, sc1, w_in_p, ctx, h0, conv_w, conv_b, wlo, whi, bgate, lam):
    n = DEC_BATCH
    outs = [jax.ShapeDtypeStruct((n, LRU_WIDTH), F32),
            jax.ShapeDtypeStruct((n * N_HEADS, LANES), F32),
            jax.ShapeDtypeStruct((n, KV_WIDTH), F32),
            jax.ShapeDtypeStruct((n, KV_WIDTH), F32),
            jax.ShapeDtypeStruct((n, CONV_WIDTH - 1, LRU_WIDTH), F32),
            jax.ShapeDtypeStruct((n, LRU_WIDTH), F32)]
    return pl.pallas_call(
        _sample_in_kernel,
        out_shape=outs,
        compiler_params=pltpu.CompilerParams(vmem_limit_bytes=VMEM_LIMIT),
        name="sample_inproj_rglru",
    )(x, sh1, sc1, w_in_p, ctx, h0, conv_w, conv_b, wlo, whi, bgate, lam)


def _sample_attn_kernel(q_ref, kn_ref, vn_ref, ck_ref, cv_ref, sink_ref, y_ref, nk_ref, nv_ref, *, bb):
    rows = lax.broadcasted_iota(jnp.int32, (WINDOW, KV_WIDTH), 0)
    nh = N_HEADS
    q_all = q_ref[...].reshape(bb * nh, LANES)
    kcat = ck_ref[...].reshape(bb * WINDOW, KV_WIDTH)
    vcat = cv_ref[...].reshape(bb * WINDOW, KV_WIDTH)
    kn_rep = jnp.broadcast_to(kn_ref[...][:, None, :], (bb, nh, KV_WIDTH)).reshape(bb * nh, KV_WIDTH)
    vn_rep = jnp.broadcast_to(vn_ref[...][:, None, :], (bb, nh, KV_WIDTH)).reshape(bb * nh, KV_WIDTH)
    sink = jnp.concatenate([sink_ref[...]] * bb, axis=0)
    s_full = _dot_nt(q_all, kcat, True)
    s = jnp.concatenate([s_full[nh * b:nh * (b + 1), WINDOW * b:WINDOW * (b + 1)] for b in range(bb)],
                        axis=0) * ATTN_SCALE
    s_self = jnp.sum(q_all * kn_rep, axis=-1, keepdims=True) * ATTN_SCALE
    m = jnp.maximum(jnp.maximum(jnp.max(s, axis=-1, keepdims=True), s_self), sink)
    e = jnp.exp(s - m)
    e_self = jnp.exp(s_self - m)
    den = jnp.sum(e, axis=-1, keepdims=True) + e_self + jnp.exp(sink - m)
    inv = 1.0 / den
    p = e * inv
    zero = jnp.zeros((nh, WINDOW), F32)
    p_wide = jnp.concatenate(
        [jnp.concatenate([p[nh * b:nh * (b + 1)] if c == b else zero for c in range(bb)], axis=-1)
         for b in range(bb)], axis=0)
    o = _dot(p_wide, vcat, True) + (e_self * inv) * vn_rep
    y_ref[...] = o.reshape(bb, nh, LANES)
    for b in range(bb):
        nk_ref[b] = jnp.where(rows == WINDOW - 1, kn_ref[b:b + 1, :], pltpu.roll(ck_ref[b], WINDOW - 1, axis=0))
        nv_ref[b] = jnp.where(rows == WINDOW - 1, vn_ref[b:b + 1, :], pltpu.roll(cv_ref[b], WINDOW - 1, axis=0))


def _sample_attn(q3, kn, vn, cache_k, cache_v, sinks, bb=16):
    n = DEC_BATCH
    kern = functools.partial(_sample_attn_kernel, bb=bb)
    return pl.pallas_call(
        kern,
        grid=(n // bb,),
        in_specs=[pl.BlockSpec((bb, N_HEADS, LANES), lambda i: (i, 0, 0)),
                  pl.BlockSpec((bb, KV_WIDTH), lambda i: (i, 0)),
                  pl.BlockSpec((bb, KV_WIDTH), lambda i: (i, 0)),
                  pl.BlockSpec((bb, WINDOW, KV_WIDTH), lambda i: (i, 0, 0)),
                  pl.BlockSpec((bb, WINDOW, KV_WIDTH), lambda i: (i, 0, 0)),
                  pl.BlockSpec((N_HEADS, 1), lambda i: (0, 0))],
        out_specs=[pl.BlockSpec((bb, N_HEADS, LANES), lambda i: (i, 0, 0)),
                   pl.BlockSpec((bb, WINDOW, KV_WIDTH), lambda i: (i, 0, 0)),
                   pl.BlockSpec((bb, WINDOW, KV_WIDTH), lambda i: (i, 0, 0))],
        out_shape=[jax.ShapeDtypeStruct((n, N_HEADS, LANES), F32),
                   jax.ShapeDtypeStruct((n, WINDOW, KV_WIDTH), F32),
                   jax.ShapeDtypeStruct((n, WINDOW, KV_WIDTH), F32)],
        compiler_params=_cparams(("arbitrary",)),
        name="sample_cache_attention",
    )(q3, kn, vn, cache_k, cache_v, sinks.reshape(N_HEADS, 1))


def _sample_out_kernel(x_ref, ylru_ref, yatt_ref, sh2_ref, sc2_ref, gt1_ref, glru_ref, gattn_ref, wout_ref,
                       ln1g_ref, ln1b_ref, wr_ref, br_ref, x1_ref, comb_ref):
    low = lax.broadcasted_iota(jnp.int32, (DEC_BATCH, LANES), 1) < HEAD_DIM
    yatt = jnp.concatenate(
        [jnp.where(low, yatt_ref[pl.ds(c, DEC_BATCH, stride=N_HEADS), :],
                   yatt_ref[pl.ds(c + 4, DEC_BATCH, stride=N_HEADS), :]) for c in range(4)], axis=-1)
    x1, comb = _outproj_body(x_ref[...], ylru_ref[...], yatt, sh2_ref[...], sc2_ref[...], gt1_ref[...],
                             glru_ref[...], gattn_ref[...], wout_ref[...], ln1g_ref[...], ln1b_ref[...],
                             wr_ref[...], br_ref[...], True)
    x1_ref[...] = x1
    comb_ref[...] = comb


def _sample_out(x, ylru, yatt2d, sh2, sc2, gt1, glru, gattn, wout_p, ln1g, ln1b, wr, br):
    n = DEC_BATCH
    return pl.pallas_call(
        _sample_out_kernel,
        out_shape=[jax.ShapeDtypeStruct((n, D_MODEL), F32), jax.ShapeDtypeStruct((n, ROUTE_LANES), F32)],
        compiler_params=pltpu.CompilerParams(vmem_limit_bytes=VMEM_LIMIT),
        name="sample_outproj_ln_route",
    )(x, ylru, yatt2d, sh2, sc2, gt1, glru, gattn, wout_p, ln1g, ln1b, wr, br)


def _block_diag_halves(w_a, w_x):
    def bd(w4):
        eye = jnp.eye(4, dtype=w4.dtype)
        return (w4[:, :, None, :] * eye[:, None, :, None]).reshape(256, 256)
    lo = jnp.concatenate([bd(w_a[:4]), bd(w_x[:4])], axis=1)
    hi = jnp.concatenate([bd(w_a[4:]), bd(w_x[4:])], axis=1)
    return lo, hi


def kernel(x_prompt, x_sample, c_prompt, c_sample, state_conv, state_h, cache_k, cache_v, w_ada, b_ada, w_in,
           conv_w, conv_b, w_rg_a, b_rg_a, w_rg_x, b_rg_x, lru_lambda, sinks, g_lru, g_attn, w_out, ln1_g, ln1_b,
           w_group, b_group, w_router, b_router, w_gate, w_up, w_down, ln2_g, ln2_b):
    d = D_MODEL
    perm = jnp.asarray(HEAD_PERM)
    w_in0 = w_in[0]
    q0 = 2 * LRU_WIDTH
    w_in_p = jnp.concatenate([w_in0[:, :q0], w_in0[:, q0:q0 + ATTN_WIDTH][:, perm], w_in0[:, q0 + ATTN_WIDTH:]],
                             axis=1)
    w_out0 = w_out[0]
    w_out_p = jnp.concatenate([w_out0[:LRU_WIDTH], w_out0[LRU_WIDTH:][perm]], axis=0)
    g_attn_p = g_attn[0][perm].reshape(1, -1)
    glru = g_lru[0].reshape(1, -1)
    wlo, whi = _block_diag_halves(w_rg_a[0], w_rg_x[0])
    bgate = jnp.concatenate([b_rg_a[0].reshape(-1), b_rg_x[0].reshape(-1)]).reshape(1, -1)
    lam = lru_lambda[0].reshape(1, -1)
    convw = conv_w[0]
    convb = conv_b[0].reshape(1, -1)
    ln1g, ln1b = ln1_g[0].reshape(1, -1), ln1_b[0].reshape(1, -1)
    ln2g, ln2b = ln2_g[0].reshape(1, -1), ln2_b[0].reshape(1, -1)
    wr = jnp.concatenate([jnp.transpose(w_router[0], (1, 0, 2)).reshape(d, N_EXPERTS), w_group[0],
                          jnp.zeros((d, ROUTE_LANES - N_EXPERTS - N_GROUPS), F32)], axis=1)
    br = jnp.concatenate([b_router[0].reshape(-1), b_group[0],
                          jnp.zeros((ROUTE_LANES - N_EXPERTS - N_GROUPS,), F32)]).reshape(1, -1)
    sink_p = sinks[0]

    c_all = jnp.concatenate([c_prompt, jnp.zeros((8 - BATCH, d), F32), c_sample], axis=0)
    mod = _ada(c_all, w_ada[0], b_ada[0])
    modp = mod[:BATCH].reshape(BATCH, 6, d)
    mods = mod[8:]
    sh1_s, sc1_s, gt1_s, sh2_s, sc2_s, gt2_s = (mods[:, k * d:(k + 1) * d] for k in range(6))

    zin = _inproj(x_prompt, modp, w_in_p.astype(BF16))
    ylru, cstate8, hlast8 = _lru(zin, convw, convb, wlo.astype(BF16), whi.astype(BF16), bgate, lam)
    yatt = _attn(zin, sink_p)
    n_p = BATCH * SEQ
    x1_p, info, cntf = _outproj_prompt(x_prompt.reshape(n_p, d), ylru.reshape(n_p, LRU_WIDTH),
                                       yatt.reshape(n_p, ATTN_WIDTH), modp, glru, g_attn_p, w_out_p.astype(BF16),
                                       ln1g, ln1b, wr, br, tm=RB_SUB)
    cnt = cntf[:, 0, :N_EXPERTS].astype(jnp.int32)
    offs = _rb_offsets(cnt, info[:, 0:2].astype(jnp.int32), info[:, 4:6].astype(jnp.int32))
    y_p = _rb_moe(x1_p, modp, cnt, offs, info[:, 2:4], w_gate[0].astype(BF16), w_up[0].astype(BF16),
                  w_down[0].astype(BF16), ln2g, ln2b)

    ylru_s, q2d, kn, vn, cstate_s, hnew_s = _sample_in(
        x_sample.reshape(DEC_BATCH, d), sh1_s, sc1_s, w_in_p, state_conv[0], state_h[0],
        convw, convb, wlo, whi, bgate, lam)
    yatt3, newk, newv = _sample_attn(q2d.reshape(DEC_BATCH, N_HEADS, LANES), kn, vn,
                                     cache_k[0].reshape(DEC_BATCH, WINDOW, KV_WIDTH),
                                     cache_v[0].reshape(DEC_BATCH, WINDOW, KV_WIDTH), sink_p)
    x1_s, comb_s = _sample_out(x_sample.reshape(DEC_BATCH, d), ylru_s, yatt3.reshape(DEC_BATCH * N_HEADS, LANES),
                               sh2_s, sc2_s, gt1_s, glru, g_attn_p, w_out_p, ln1g, ln1b, wr, br)
    y_s = _moe_dense(x1_s, comb_s, sh2_s, sc2_s, gt2_s, 0, w_gate[0], w_up[0], w_down[0], ln2g, ln2b, DEC_BATCH)

    kq = 2 * LRU_WIDTH + ATTN_WIDTH
    return (y_p.reshape(BATCH, SEQ, d),
            y_s.reshape(DEC_BATCH, 1, d),
            cstate8[:, 5:8][None],
            hlast8[:, 7][None],
            zin[:, SEQ - WINDOW:, kq:kq + KV_WIDTH].reshape(1, BATCH, WINDOW, N_KV_HEADS, HEAD_DIM),
            zin[:, SEQ - WINDOW:, kq + KV_WIDTH:].reshape(1, BATCH, WINDOW, N_KV_HEADS, HEAD_DIM),
            cstate_s[None],
            hnew_s[None],
            newk.reshape(1, DEC_BATCH, WINDOW, N_KV_HEADS, HEAD_DIM),
            newv.reshape(1, DEC_BATCH, WINDOW, N_KV_HEADS, HEAD_DIM))
```

```python
import functools

import jax
import jax.numpy as jnp
import numpy as np
from jax import lax
from jax.experimental import pallas as pl
from jax.experimental.pallas import tpu as pltpu

F32 = jnp.float32
BF16 = jnp.bfloat16
HIGHEST = lax.Precision.HIGHEST

D_MODEL = 1024
BATCH = 4
SEQ = 4096
DEC_BATCH = 128
LRU_WIDTH = 512
LRU_BLOCKS = 8
LRU_BLOCK = 64
CONV_WIDTH = 4
LRU_C = 8.0
N_HEADS = 8
N_KV_HEADS = 2
HEAD_DIM = 64
ATTN_WIDTH = 512
KV_WIDTH = 128
WINDOW = 128
IN_WIDTH = 2 * LRU_WIDTH + ATTN_WIDTH + 2 * KV_WIDTH
N_GROUPS = 4
EXPERTS_PER_GROUP = 8
N_EXPERTS = 32
D_EXPERT = 256
DEEPNORM_ALPHA = 2.0 ** 0.25
LN_EPS = 1e-5
RMS_EPS = 1e-6
ATTN_SCALE = HEAD_DIM ** -0.5

LANES = 128
ROUTE_LANES = 128
ROUTE_INFO = 40
VMEM_LIMIT = 56 * 1024 * 1024

HEAD_PERM = np.concatenate(
    [np.concatenate([np.arange(64 * c, 64 * c + 64), np.arange(64 * (c + 4), 64 * (c + 4) + 64)])
     for c in range(4)])


def _cparams(sem):
    return pltpu.CompilerParams(dimension_semantics=sem, vmem_limit_bytes=VMEM_LIMIT)


def _dot(a, b, exact):
    if exact:
        return jnp.dot(a, b, precision=HIGHEST, preferred_element_type=F32)
    return jnp.dot(a.astype(BF16), b.astype(BF16), preferred_element_type=F32)


def _dot_nt(a, b, exact):
    dn = (((1,), (1,)), ((), ()))
    if exact:
        return lax.dot_general(a, b, dn, precision=HIGHEST, preferred_element_type=F32)
    return lax.dot_general(a.astype(BF16), b.astype(BF16), dn, preferred_element_type=F32)


def _sigmoid(x):
    return 1.0 / (1.0 + jnp.exp(-x))


def _silu(x):
    return x * _sigmoid(x)


def _gelu_tanh(x):
    return 0.5 * x * (1.0 + jnp.tanh(np.sqrt(2.0 / np.pi).astype(np.float32) * (x + 0.044715 * (x * x * x))))


def _softplus(x):
    return jnp.maximum(x, 0.0) + jnp.log1p(jnp.exp(-jnp.abs(x)))


def _layer_norm(x, g, b):
    mu = jnp.mean(x, axis=-1, keepdims=True)
    xc = x - mu
    var = jnp.mean(xc * xc, axis=-1, keepdims=True)
    return xc * lax.rsqrt(var + LN_EPS) * g + b


def _rms_norm(x, g):
    return x * lax.rsqrt(jnp.mean(x * x, axis=-1, keepdims=True) + RMS_EPS) * g


def _ada_kernel(c_ref, w_ref, b_ref, o_ref):
    o_ref[...] = _dot(_silu(c_ref[...]), w_ref[...], True) + b_ref[...]


def _ada(c_all, w_ada, b_ada):
    rows = c_all.shape[0]
    bn = 512
    return pl.pallas_call(
        _ada_kernel,
        grid=(6 * D_MODEL // bn,),
        in_specs=[pl.BlockSpec((rows, D_MODEL), lambda j: (0, 0)),
                  pl.BlockSpec((D_MODEL, bn), lambda j: (0, j)),
                  pl.BlockSpec((1, bn), lambda j: (0, j))],
        out_specs=pl.BlockSpec((rows, bn), lambda j: (0, j)),
        out_shape=jax.ShapeDtypeStruct((rows, 6 * D_MODEL), F32),
        compiler_params=_cparams(("arbitrary",)),
        name="ada_modulation",
    )(c_all, w_ada, b_ada.reshape(1, -1))


def _inproj_kernel(x_ref, mod_ref, w_ref, o_ref):
    sh1 = mod_ref[0, 0:1, :]
    sc1 = mod_ref[0, 1:2, :]
    h = x_ref[0] * (1.0 + sc1) + sh1
    o_ref[0] = _dot(h, w_ref[...], False)


def _inproj(x, modp, w_in_bf16, tm=512):
    b, t, d = x.shape
    return pl.pallas_call(
        _inproj_kernel,
        grid=(b, t // tm),
        in_specs=[pl.BlockSpec((1, tm, d), lambda i, j: (i, j, 0)),
                  pl.BlockSpec((1, 6, d), lambda i, j: (i, 0, 0)),
                  pl.BlockSpec((d, IN_WIDTH), lambda i, j: (0, 0))],
        out_specs=pl.BlockSpec((1, tm, IN_WIDTH), lambda i, j: (i, j, 0)),
        out_shape=jax.ShapeDtypeStruct((b, t, IN_WIDTH), F32),
        compiler_params=_cparams(("arbitrary", "arbitrary")),
        name="prompt_inproj",
    )(x, modp, w_in_bf16)


def _lru_gates(xc, wlo, whi, bgate, sp_neg_lam, exact):
    g_lo = _dot(xc[:, :256], wlo, exact)
    g_hi = _dot(xc[:, 256:], whi, exact)
    ga = jnp.concatenate([g_lo[:, :256], g_hi[:, :256]], axis=-1) + bgate[:, :LRU_WIDTH]
    gx = jnp.concatenate([g_lo[:, 256:], g_hi[:, 256:]], axis=-1) + bgate[:, LRU_WIDTH:]
    r = _sigmoid(ga)
    i = _sigmoid(gx)
    log_a = -LRU_C * r * sp_neg_lam
    a = jnp.exp(log_a)
    one_minus_a2 = -jnp.tanh(log_a) * (a * a + 1.0)
    bterm = jnp.sqrt(one_minus_a2) * (i * xc)
    return a, bterm


def _lru_kernel(z_ref, convw_ref, convb_ref, wlo_ref, whi_ref, bgate_ref, lam_ref,
                y_ref, cstate_ref, hlast_ref, tail_ref, carry_ref, *, tl):
    j = pl.program_id(1)

    @pl.when(j == 0)
    def _():
        tail_ref[...] = jnp.zeros_like(tail_ref)
        carry_ref[...] = jnp.zeros_like(carry_ref)

    xb = z_ref[0, :, :LRU_WIDTH]
    gate = z_ref[0, :, LRU_WIDTH:]
    rows = lax.broadcasted_iota(jnp.int32, (tl, LRU_WIDTH), 0)

    xc = convb_ref[...] + convw_ref[3:4, :] * xb
    rows8 = lax.broadcasted_iota(jnp.int32, (8, LRU_WIDTH), 0)
    tail = tail_ref[...]
    for back in (1, 2, 3):
        rolled = pltpu.roll(xb, back, axis=0)
        top = jnp.where(rows8 >= back, rolled[:8], pltpu.roll(tail, back, axis=0))
        shifted = jnp.concatenate([top, rolled[8:]], axis=0)
        xc = xc + convw_ref[3 - back:4 - back, :] * shifted
    tail_ref[...] = xb[tl - 8:, :]
    cstate_ref[0] = xb[tl - 8:, :]

    sp = _softplus(-lam_ref[...])
    a, bterm = _lru_gates(xc, wlo_ref[...], whi_ref[...], bgate_ref[...], sp, False)

    s = 1
    while s < tl:
        a_sh = jnp.where(rows >= s, pltpu.roll(a, s, axis=0), 1.0)
        b_sh = jnp.where(rows >= s, pltpu.roll(bterm, s, axis=0), 0.0)
        bterm = a * b_sh + bterm
        a = a * a_sh
        s *= 2
    h = a * carry_ref[7:8, :] + bterm
    carry_ref[...] = h[tl - 8:, :]
    hlast_ref[0] = h[tl - 8:, :]
    y_ref[0] = h * _gelu_tanh(gate)


def _lru(zin, conv_w, conv_b, wlo, whi, bgate, lam, tl=512):
    b, t, _ = zin.shape
    kern = functools.partial(_lru_kernel, tl=tl)
    full = lambda shp: pl.BlockSpec(shp, lambda i, j: tuple(0 for _ in shp))
    return pl.pallas_call(
        kern,
        grid=(b, t // tl),
        in_specs=[pl.BlockSpec((1, tl, 2 * LRU_WIDTH), lambda i, j: (i, j, 0)),
                  full((CONV_WIDTH, LRU_WIDTH)), full((1, LRU_WIDTH)),
                  full((256, 512)), full((256, 512)), full((1, 2 * LRU_WIDTH)), full((1, LRU_WIDTH))],
        out_specs=[pl.BlockSpec((1, tl, LRU_WIDTH), lambda i, j: (i, j, 0)),
                   pl.BlockSpec((1, 8, LRU_WIDTH), lambda i, j: (i, 0, 0)),
                   pl.BlockSpec((1, 8, LRU_WIDTH), lambda i, j: (i, 0, 0))],
        out_shape=[jax.ShapeDtypeStruct((b, t, LRU_WIDTH), F32),
                   jax.ShapeDtypeStruct((b, 8, LRU_WIDTH), F32),
                   jax.ShapeDtypeStruct((b, 8, LRU_WIDTH), F32)],
        scratch_shapes=[pltpu.VMEM((8, LRU_WIDTH), F32), pltpu.VMEM((8, LRU_WIDTH), F32)],
        compiler_params=_cparams(("arbitrary", "arbitrary")),
        name="prompt_rglru",
    )(zin, conv_w, conv_b, wlo, whi, bgate, lam)


ATTN_BLOCKS = 4


def _attn_kernel(q_ref, k_ref, v_ref, sink_ref, o_ref, kprev_ref, vprev_ref):
    j = pl.program_id(1)

    @pl.when(j == 0)
    def _():
        kprev_ref[...] = jnp.zeros_like(kprev_ref)
        vprev_ref[...] = jnp.zeros_like(vprev_ref)

    blk = WINDOW
    lane = lax.broadcasted_iota(jnp.int32, (blk, LANES), 1)
    low = lane < HEAD_DIM
    qi = lax.broadcasted_iota(jnp.int32, (blk, 2 * blk), 0)
    sj = lax.broadcasted_iota(jnp.int32, (blk, 2 * blk), 1)
    rel = blk + qi - sj
    in_window = (rel >= 0) & (rel <= WINDOW)
    sink = sink_ref[...].reshape(N_HEADS, blk, 1)
    k_ext = jnp.concatenate([kprev_ref[...], k_ref[0]], axis=0).astype(BF16)
    v_ext = jnp.concatenate([vprev_ref[...], v_ref[0]], axis=0).astype(BF16)
    v_ext = jnp.concatenate([v_ext, jnp.ones_like(v_ext)], axis=-1)
    for n in range(ATTN_BLOCKS):
        q = q_ref[0, blk * n:blk * (n + 1), :]
        pieces = []
        for half in (0, 1):
            for c in range(4):
                qc = q[:, LANES * c:LANES * (c + 1)]
                pieces.append(jnp.where(low if half == 0 else ~low, qc, 0.0).astype(BF16))
        q8 = jnp.concatenate(pieces, axis=0)
        k_band = k_ext[blk * n:blk * (n + 2)]
        v_band = v_ext[blk * n:blk * (n + 2)]
        s = _dot_nt(q8, k_band, False) * ATTN_SCALE
        s = s.reshape(N_HEADS, blk, 2 * blk)
        valid = in_window & ((sj >= blk) | (j > 0)) if n == 0 else in_window
        s = jnp.where(valid[None], s, -jnp.inf)
        m = jnp.maximum(jnp.max(s, axis=-1, keepdims=True), sink)
        e = jnp.exp(s - m).reshape(N_HEADS * blk, 2 * blk)
        ov = _dot(e, v_band, False)
        den = ov[:, KV_WIDTH:] + jnp.exp(sink - m).reshape(N_HEADS * blk, 1)
        o8 = ov[:, :KV_WIDTH] * (1.0 / den)
        cols = []
        for c in range(4):
            cols.append(jnp.where(low, o8[blk * c:blk * (c + 1)], o8[blk * (c + 4):blk * (c + 5)]))
        o_ref[0, blk * n:blk * (n + 1), :] = jnp.concatenate(cols, axis=-1)
    kprev_ref[...] = k_ref[0, blk * (ATTN_BLOCKS - 1):, :]
    vprev_ref[...] = v_ref[0, blk * (ATTN_BLOCKS - 1):, :]


def _attn(zin, sinks):
    b, t, _ = zin.shape
    blk = WINDOW
    tq = blk * ATTN_BLOCKS
    sink_col = jnp.repeat(sinks.astype(F32), blk).reshape(N_HEADS * blk, 1)
    return pl.pallas_call(
        _attn_kernel,
        grid=(b, t // tq),
        in_specs=[pl.BlockSpec((1, tq, ATTN_WIDTH), lambda i, j: (i, j, 2)),
                  pl.BlockSpec((1, tq, KV_WIDTH), lambda i, j: (i, j, 12)),
                  pl.BlockSpec((1, tq, KV_WIDTH), lambda i, j: (i, j, 13)),
                  pl.BlockSpec((N_HEADS * blk, 1), lambda i, j: (0, 0))],
        out_specs=pl.BlockSpec((1, tq, ATTN_WIDTH), lambda i, j: (i, j, 0)),
        out_shape=jax.ShapeDtypeStruct((b, t, ATTN_WIDTH), F32),
        scratch_shapes=[pltpu.VMEM((blk, KV_WIDTH), F32), pltpu.VMEM((blk, KV_WIDTH), F32)],
        compiler_params=_cparams(("arbitrary", "arbitrary")),
        name="prompt_window_attention",
    )(zin, zin, zin, sink_col)


def _dot_split3(a, b):
    a_hi = a.astype(BF16)
    b_hi = b.astype(BF16)
    a_lo = (a - a_hi.astype(F32)).astype(BF16)
    b_lo = (b - b_hi.astype(F32)).astype(BF16)
    return (jnp.dot(a_hi, b_hi, preferred_element_type=F32) + jnp.dot(a_hi, b_lo, preferred_element_type=F32)
            + jnp.dot(a_lo, b_hi, preferred_element_type=F32))


def _route(h2, wr, br, exact):
    t = h2.shape[0]
    logits = (_dot(h2, wr, True) if exact else _dot_split3(h2, wr)) + br
    lane = lax.broadcasted_iota(jnp.int32, (t, ROUTE_LANES), 1).astype(F32)
    neg = -jnp.inf
    big = float(ROUTE_LANES)
    is_g = (lane >= N_EXPERTS) & (lane < N_EXPERTS + N_GROUPS)
    lg = jnp.where(is_g, logits, neg)
    mg = jnp.max(lg, axis=-1, keepdims=True)
    eg = jnp.where(is_g, jnp.exp(lg - mg), 0.0)
    pg = eg / jnp.sum(eg, axis=-1, keepdims=True)
    g_val = jnp.max(pg, axis=-1, keepdims=True)
    g_lane = jnp.min(jnp.where((pg == g_val) & is_g, lane, big), axis=-1, keepdims=True)
    g_idx = g_lane - N_EXPERTS
    in_grp = (lane >= g_idx * EXPERTS_PER_GROUP) & (lane < (g_idx + 1.0) * EXPERTS_PER_GROUP)
    le = jnp.where(in_grp, logits, neg)
    me = jnp.max(le, axis=-1, keepdims=True)
    ee = jnp.where(in_grp, jnp.exp(le - me), 0.0)
    pe = ee / jnp.sum(ee, axis=-1, keepdims=True)
    v1 = jnp.max(pe, axis=-1, keepdims=True)
    l1 = jnp.min(jnp.where((pe == v1) & in_grp, lane, big), axis=-1, keepdims=True)
    rest = in_grp & (lane != l1)
    pe2 = jnp.where(rest, pe, -1.0)
    v2 = jnp.max(pe2, axis=-1, keepdims=True)
    l2 = jnp.min(jnp.where((pe2 == v2) & rest, lane, big), axis=-1, keepdims=True)
    tot = v1 + v2
    w1 = g_val * v1 / tot
    w2 = g_val * v2 / tot
    comb = jnp.where(lane == l1, w1, 0.0) + jnp.where(lane == l2, w2, 0.0)
    return (comb + jnp.where(lane == ROUTE_INFO, l1, 0.0) + jnp.where(lane == ROUTE_INFO + 1, l2, 0.0)
            + jnp.where(lane == ROUTE_INFO + 2, w1, 0.0) + jnp.where(lane == ROUTE_INFO + 3, w2, 0.0))


def _outproj_body(x, ylru, yatt, sh2, sc2, gt1, glru, gattn, wout, ln1g, ln1b, wr, br, exact):
    mixin = jnp.concatenate([_rms_norm(ylru, glru), _rms_norm(yatt, gattn)], axis=-1)
    mix = _dot(mixin, wout, exact)
    x1 = _layer_norm(DEEPNORM_ALPHA * x + (1.0 + gt1) * mix, ln1g, ln1b)
    h2 = x1 * (1.0 + sc2) + sh2
    return x1, _route(h2, wr, br, exact)


def _outproj_prompt_kernel(x_ref, ylru_ref, yatt_ref, mod_ref, glru_ref, gattn_ref, wout_ref,
                           ln1g_ref, ln1b_ref, wr_ref, br_ref, x1_ref, info_ref, cnt_ref, tri_ref, carry_ref,
                           *, tm, per_seq):
    i = pl.program_id(0)

    @pl.when(i == 0)
    def _():
        r = lax.broadcasted_iota(jnp.int32, (tm, tm), 0)
        c = lax.broadcasted_iota(jnp.int32, (tm, tm), 1)
        tri_ref[...] = jnp.where(c < r, 1.0, 0.0).astype(BF16)

    @pl.when(i % per_seq == 0)
    def _():
        carry_ref[...] = jnp.zeros_like(carry_ref)

    gt1 = mod_ref[0, 2:3, :]
    sh2 = mod_ref[0, 3:4, :]
    sc2 = mod_ref[0, 4:5, :]
    combs = []
    nsplit = 2
    for h in range(nsplit):
        rows = slice(h * (tm // nsplit), (h + 1) * (tm // nsplit))
        x1_h, comb_h = _outproj_body(x_ref[rows, :], ylru_ref[rows, :], yatt_ref[rows, :], sh2, sc2, gt1,
                                     glru_ref[...], gattn_ref[...], wout_ref[...], ln1g_ref[...], ln1b_ref[...],
                                     wr_ref[...], br_ref[...], False)
        x1_ref[rows, :] = x1_h
        combs.append(comb_h)
    comb = jnp.concatenate(combs, axis=0)
    lane = lax.broadcasted_iota(jnp.int32, (tm, ROUTE_LANES), 1).astype(F32)
    l1 = jnp.sum(jnp.where(lane == ROUTE_INFO, comb, 0.0), axis=-1, keepdims=True)
    l2 = jnp.sum(jnp.where(lane == ROUTE_INFO + 1, comb, 0.0), axis=-1, keepdims=True)
    o1 = lane == l1
    o2 = lane == l2
    onehot = jnp.where(o1 | o2, 1.0, 0.0)
    before = jnp.dot(tri_ref[...], onehot.astype(BF16), preferred_element_type=F32) + carry_ref[0:1, :]
    rank1 = jnp.sum(jnp.where(o1, before, 0.0), axis=-1, keepdims=True)
    rank2 = jnp.sum(jnp.where(o2, before, 0.0), axis=-1, keepdims=True)
    total = carry_ref[0:1, :] + jnp.sum(onehot, axis=0, keepdims=True)
    carry_ref[...] = jnp.broadcast_to(total, carry_ref.shape)
    cnt_ref[0] = jnp.broadcast_to(total, (8, ROUTE_LANES))
    info = (comb + jnp.where(lane == ROUTE_INFO + 4, rank1, 0.0) + jnp.where(lane == ROUTE_INFO + 5, rank2, 0.0))
    info_ref[0] = jnp.transpose(info)[ROUTE_INFO:ROUTE_INFO + 8, :]


def _outproj_prompt(x2d, ylru2d, yatt2d, modp, glru, gattn, wout_bf16, ln1g, ln1b, wr, br, tm=512):
    n, d = x2d.shape
    per_seq = SEQ // tm
    full = lambda shp: pl.BlockSpec(shp, lambda i: tuple(0 for _ in shp))
    kern = functools.partial(_outproj_prompt_kernel, tm=tm, per_seq=per_seq)
    return pl.pallas_call(
        kern,
        grid=(n // tm,),
        in_specs=[pl.BlockSpec((tm, d), lambda i: (i, 0)),
                  pl.BlockSpec((tm, LRU_WIDTH), lambda i: (i, 0)),
                  pl.BlockSpec((tm, ATTN_WIDTH), lambda i: (i, 0)),
                  pl.BlockSpec((1, 6, d), lambda i: (i // per_seq, 0, 0)),
                  full((1, LRU_WIDTH)), full((1, ATTN_WIDTH)), full((d, d)),
                  full((1, d)), full((1, d)), full((d, ROUTE_LANES)), full((1, ROUTE_LANES))],
        out_specs=[pl.BlockSpec((tm, d), lambda i: (i, 0)),
                   pl.BlockSpec((1, 8, tm), lambda i: (i, 0, 0)),
                   pl.BlockSpec((1, 8, ROUTE_LANES), lambda i: (i // per_seq, 0, 0))],
        out_shape=[jax.ShapeDtypeStruct((n, d), F32),
                   jax.ShapeDtypeStruct((n // tm, 8, tm), F32),
                   jax.ShapeDtypeStruct((n // SEQ, 8, ROUTE_LANES), F32)],
        scratch_shapes=[pltpu.VMEM((tm, tm), BF16), pltpu.VMEM((8, ROUTE_LANES), F32)],
        compiler_params=_cparams(("arbitrary",)),
        name="prompt_outproj_ln_route",
    )(x2d, ylru2d, yatt2d, modp, glru, gattn, wout_bf16, ln1g, ln1b, wr, br)


def _moe_kernel(x1_ref, comb_ref, sh2_ref, sc2_ref, gt2_ref, wg_ref, wu_ref, wd_ref, ln2g_ref, ln2b_ref,
                o_ref, h2_ref, acc_ref):
    e = pl.program_id(1)

    @pl.when(e == 0)
    def _():
        h2_ref[...] = (x1_ref[...] * (1.0 + sc2_ref[...]) + sh2_ref[...]).astype(BF16)
        acc_ref[...] = jnp.zeros_like(acc_ref)

    h2 = h2_ref[...]
    a = jnp.dot(h2, wg_ref[0].astype(BF16), preferred_element_type=F32)
    u = jnp.dot(h2, wu_ref[0].astype(BF16), preferred_element_type=F32)
    comb = comb_ref[...]
    lane = lax.broadcasted_iota(jnp.int32, comb.shape, 1)
    c_e = jnp.sum(jnp.where(lane == e, comb, 0.0), axis=-1, keepdims=True)
    z = _silu(a) * u * c_e
    acc_ref[...] += jnp.dot(z.astype(BF16), wd_ref[0].astype(BF16), preferred_element_type=F32)

    @pl.when(e == N_EXPERTS - 1)
    def _():
        o_ref[...] = _layer_norm(DEEPNORM_ALPHA * x1_ref[...] + (1.0 + gt2_ref[...]) * acc_ref[...],
                                 ln2g_ref[...], ln2b_ref[...])


def _moe_dense(x1, comb, sh2, sc2, gt2, mod_rows_per_tile, w_gate, w_up, w_down, ln2g, ln2b, tm):
    n, d = x1.shape
    if mod_rows_per_tile:
        mspec = pl.BlockSpec((1, 1, d), lambda i, e: (i // mod_rows_per_tile, 0, 0))
        sh2, sc2, gt2 = (m.reshape(-1, 1, d) for m in (sh2, sc2, gt2))
        kern = lambda x1r, cr, s1, s2, s3, *rest: _moe_kernel(x1r, cr, s1.at[0], s2.at[0], s3.at[0], *rest)
    else:
        mspec = pl.BlockSpec((tm, d), lambda i, e: (i, 0))
        kern = _moe_kernel
    full = lambda shp: pl.BlockSpec(shp, lambda i, e: tuple(0 for _ in shp))
    return pl.pallas_call(
        kern,
        grid=(n // tm, N_EXPERTS),
        in_specs=[pl.BlockSpec((tm, d), lambda i, e: (i, 0)),
                  pl.BlockSpec((tm, ROUTE_LANES), lambda i, e: (i, 0)),
                  mspec, mspec, mspec,
                  pl.BlockSpec((1, d, D_EXPERT), lambda i, e: (e, 0, 0)),
                  pl.BlockSpec((1, d, D_EXPERT), lambda i, e: (e, 0, 0)),
                  pl.BlockSpec((1, D_EXPERT, d), lambda i, e: (e, 0, 0)),
                  full((1, d)), full((1, d))],
        out_specs=pl.BlockSpec((tm, d), lambda i, e: (i, 0)),
        out_shape=jax.ShapeDtypeStruct((n, d), F32),
        scratch_shapes=[pltpu.VMEM((tm, d), BF16), pltpu.VMEM((tm, d), F32)],
        compiler_params=_cparams(("arbitrary", "arbitrary")),
        name="moe_dense_ln",
    )(x1, comb, sh2, sc2, gt2, w_gate, w_up, w_down, ln2g, ln2b)


RB_SUB = 512
RB_NSUB = SEQ // RB_SUB
RB_CHUNK = 128
RB_CHUNK_BITS = 7
RB_NCHUNK = 2 * SEQ // RB_CHUNK
RB_PITCH = RB_CHUNK + 8
RB_SPITCH = RB_SUB + 8
RB_GROUP = 3
RB_WSLOTS = 4


def _rb_kernel(cnt_ref, x1_ref, mod_ref, offs_ref, wts_ref, wg_hbm, wu_hbm, wd_hbm, ln2g_ref, ln2b_ref,
               o_ref, buf_ref, stage_ref, wg_buf, wu_buf, wd_buf, start_ref, sem):
    b = pl.program_id(0)
    s = pl.program_id(1)

    @pl.when(s == 0)
    def _starts():
        def body(e, run):
            start_ref[e] = run
            return run + cnt_ref[b, e]
        lax.fori_loop(0, N_EXPERTS, body, jnp.int32(0))
        buf_ref[RB_NCHUNK * 8 * RB_PITCH:(RB_NCHUNK + RB_GROUP) * 8 * RB_PITCH, :] = jnp.zeros(
            (RB_GROUP * 8 * RB_PITCH, LANES), F32)

    @pl.when(s < RB_NSUB)
    def _dispatch():
        sh2 = mod_ref[0, 3:4, :]
        sc2 = mod_ref[0, 4:5, :]
        h2 = x1_ref[...] * (1.0 + sc2) + sh2
        for j in range(8):
            stage_ref[RB_SPITCH * j:RB_SPITCH * j + RB_SUB, :] = h2[:, LANES * j:LANES * (j + 1)]

        for t in range(RB_SUB):
            slab = stage_ref[pl.ds(t, 8, stride=RB_SPITCH), :]
            for a in range(2):
                buf_ref[pl.ds(offs_ref[0, a, t], 8, stride=RB_PITCH), :] = slab

    @pl.when(s == RB_NSUB)
    def _experts():
        def copies(e, slot):
            return (pltpu.make_async_copy(wg_hbm.at[e], wg_buf.at[slot], sem.at[slot, 0]),
                    pltpu.make_async_copy(wu_hbm.at[e], wu_buf.at[slot], sem.at[slot, 1]),
                    pltpu.make_async_copy(wd_hbm.at[e], wd_buf.at[slot], sem.at[slot, 2]))

        def run_expert(e, slot):
            lo_row = start_ref[e]
            hi_row = lo_row + cnt_ref[b, e]

            c_lo = lax.shift_right_logical(lo_row, RB_CHUNK_BITS)
            c_hi = lax.shift_right_logical(hi_row + (RB_CHUNK - 1), RB_CHUNK_BITS)
            row = lax.broadcasted_iota(jnp.int32, (RB_CHUNK, 1), 0)

            def load(c):
                base = pl.multiple_of(c * (8 * RB_PITCH), 8)
                return [buf_ref[pl.ds(base + RB_PITCH * j, RB_CHUNK), :] for j in range(8)]

            def store(c, tiles, y):
                base = pl.multiple_of(c * (8 * RB_PITCH), 8)
                mine = (row >= lo_row - c * RB_CHUNK) & (row < hi_row - c * RB_CHUNK)
                for j in range(8):
                    buf_ref[pl.ds(base + RB_PITCH * j, RB_CHUNK), :] = jnp.where(
                        mine, y[:, LANES * j:LANES * (j + 1)], tiles[j])

            def group(i, carry):
                cs = [c_lo + RB_GROUP * i]
                for k in range(1, RB_GROUP):
                    cs.append(jnp.where(cs[0] + k < c_hi, cs[0] + k, RB_NCHUNK + k))
                tiles = [load(c) for c in cs]
                x = jnp.concatenate([jnp.concatenate(t, axis=-1) for t in tiles], axis=0).astype(BF16)
                a = jnp.dot(x, wg_buf[slot], preferred_element_type=F32)
                u = jnp.dot(x, wu_buf[slot], preferred_element_type=F32)
                z = (_silu(a) * u).astype(BF16)
                y = jnp.dot(z, wd_buf[slot], preferred_element_type=F32)
                for k, c in enumerate(cs):
                    store(c, tiles[k], y[RB_CHUNK * k:RB_CHUNK * (k + 1)])
                return carry

            lax.fori_loop(0, lax.div(c_hi - c_lo + (RB_GROUP - 1), RB_GROUP), group, 0)

        for e in range(RB_WSLOTS - 1):
            for c in copies(e, e):
                c.start()

        def ring_body(i, carry):
            for k in range(RB_WSLOTS):
                e = RB_WSLOTS * i + k
                ahead = e + RB_WSLOTS - 1

                @pl.when(ahead < N_EXPERTS)
                def _():
                    for c in copies(ahead, (k + RB_WSLOTS - 1) % RB_WSLOTS):
                        c.start()
                for c in copies(e, k):
                    c.wait()
                run_expert(e, k)
            return carry
        lax.fori_loop(0, N_EXPERTS // RB_WSLOTS, ring_body, 0)

    @pl.when(s > RB_NSUB)
    def _combine():
        for t in range(RB_SUB):
            acc = None
            for a in range(2):
                term = wts_ref[0, a, t] * buf_ref[pl.ds(offs_ref[0, a, t], 8, stride=RB_PITCH), :]
                acc = term if acc is None else acc + term
            stage_ref[pl.ds(t, 8, stride=RB_SPITCH), :] = acc
        gt2 = mod_ref[0, 5:6, :]
        f = jnp.concatenate([stage_ref[RB_SPITCH * j:RB_SPITCH * j + RB_SUB, :] for j in range(8)], axis=-1)
        o_ref[...] = _layer_norm(DEEPNORM_ALPHA * x1_ref[...] + (1.0 + gt2) * f, ln2g_ref[...], ln2b_ref[...])


def _rb_offsets(cnt, e12, rank12):
    bsz = cnt.shape[0]
    start = jnp.cumsum(cnt, axis=-1) - cnt
    start_t = jnp.repeat(start, RB_NSUB, axis=0)[:, None, None, :]
    hit = e12[..., None] == jnp.arange(N_EXPERTS, dtype=jnp.int32)
    p = jnp.sum(jnp.where(hit, start_t, 0), axis=-1) + rank12
    return lax.shift_right_logical(p, RB_CHUNK_BITS) * (8 * RB_PITCH) + (p & (RB_CHUNK - 1))


def _rb_moe(x1, modp, cnt, offs, wts, wg_bf16, wu_bf16, wd_bf16, ln2g, ln2b):
    n, d = x1.shape
    bsz = n // SEQ
    nsteps = 2 * RB_NSUB + 1

    def sub_index(s):
        return jnp.where(s < RB_NSUB, s, jnp.where(s == RB_NSUB, RB_NSUB - 1, s - RB_NSUB - 1))

    def tile_map(b, s, cnt_r):
        return (b * RB_NSUB + sub_index(s), 0)

    def tile_map3(b, s, cnt_r):
        return (b * RB_NSUB + sub_index(s), 0, 0)

    def out_map(b, s, cnt_r):
        return (b * RB_NSUB + jnp.maximum(s - RB_NSUB - 1, 0), 0)

    const = lambda shp: pl.BlockSpec(shp, lambda b, s, cnt_r: tuple(0 for _ in shp))
    anyspec = pl.BlockSpec(memory_space=pl.ANY)
    grid_spec = pltpu.PrefetchScalarGridSpec(
        num_scalar_prefetch=1,
        grid=(bsz, nsteps),
        in_specs=[pl.BlockSpec((RB_SUB, d), tile_map),
                  pl.BlockSpec((1, 6, d), lambda b, s, cnt_r: (b, 0, 0)),
                  pl.BlockSpec((1, 2, RB_SUB), tile_map3, memory_space=pltpu.SMEM),
                  pl.BlockSpec((1, 2, RB_SUB), tile_map3, memory_space=pltpu.SMEM),
                  anyspec, anyspec, anyspec,
                  const((1, d)), const((1, d))],
        out_specs=pl.BlockSpec((RB_SUB, d), out_map),
        scratch_shapes=[pltpu.VMEM(((RB_NCHUNK + RB_GROUP) * 8 * RB_PITCH, LANES), F32),
                        pltpu.VMEM((8 * RB_SPITCH, LANES), F32),
                        pltpu.VMEM((RB_WSLOTS, d, D_EXPERT), BF16),
                        pltpu.VMEM((RB_WSLOTS, d, D_EXPERT), BF16),
                        pltpu.VMEM((RB_WSLOTS, D_EXPERT, d), BF16),
                        pltpu.SMEM((N_EXPERTS,), jnp.int32),
                        pltpu.SemaphoreType.DMA((RB_WSLOTS, 3))])
    return pl.pallas_call(
        _rb_kernel,
        grid_spec=grid_spec,
        out_shape=jax.ShapeDtypeStruct((n, d), F32),
        compiler_params=_cparams(("arbitrary", "arbitrary")),
        name="moe_routed_ln",
    )(cnt, x1, modp, offs, wts, wg_bf16, wu_bf16, wd_bf16, ln2g, ln2b)


def _sample_in_kernel(x_ref, sh1_ref, sc1_ref, win_ref, ctx_ref, h0_ref, convw_ref, convb_ref,
                      wlo_ref, whi_ref, bgate_ref, lam_ref,
                      ylru_ref, q_ref, k_ref, v_ref, cstate_ref, hnew_ref):
    h = x_ref[...] * (1.0 + sc1_ref[...]) + sh1_ref[...]
    z = _dot(h, win_ref[...], True)
    xb = z[:, :LRU_WIDTH]
    gate = z[:, LRU_WIDTH:2 * LRU_WIDTH]
    c0 = ctx_ref[:, 0, :]
    c1 = ctx_ref[:, 1, :]
    c2 = ctx_ref[:, 2, :]
    xc = (convb_ref[...] + convw_ref[0:1, :] * c0 + convw_ref[1:2, :] * c1
          + convw_ref[2:3, :] * c2 + convw_ref[3:4, :] * xb)
    cstate_ref[:, 0, :] = c1
    cstate_ref[:, 1, :] = c2
    cstate_ref[:, 2, :] = xb
    sp = _softplus(-lam_ref[...])
    a, bterm = _lru_gates(xc, wlo_ref[...], whi_ref[...], bgate_ref[...], sp, True)
    hn = a * h0_ref[...] + bterm
    hnew_ref[...] = hn
    ylru_ref[...] = hn * _gelu_tanh(gate)
    low = lax.broadcasted_iota(jnp.int32, (DEC_BATCH, LANES), 1) < HEAD_DIM
    for c in range(4):
        qc = z[:, 2 * LRU_WIDTH + LANES * c:2 * LRU_WIDTH + LANES * (c + 1)]
        q_ref[pl.ds(c, DEC_BATCH, stride=N_HEADS), :] = jnp.where(low, qc, 0.0)
        q_ref[pl.ds(c + 4, DEC_BATCH, stride=N_HEADS), :] = jnp.where(low, 0.0, qc)
    k_ref[...] = z[:, 2 * LRU_WIDTH + ATTN_WIDTH:2 * LRU_WIDTH + ATTN_WIDTH + KV_WIDTH]
    v_ref[...] = z[:, 2 * LRU_WIDTH + ATTN_WIDTH + KV_WIDTH:]


def _sample_in(x, sh1, sc1, w_in_p, ctx, h0, conv_w, conv_b, wlo, whi, bgate, lam):
    n = DEC_BATCH
    outs = [jax.ShapeDtypeStruct((n, LRU_WIDTH), F32),
            jax.ShapeDtypeStruct((n * N_HEADS, LANES), F32),
            jax.ShapeDtypeStruct((n, KV_WIDTH), F32),
            jax.ShapeDtypeStruct((n, KV_WIDTH), F32),
            jax.ShapeDtypeStruct((n, CONV_WIDTH - 1, LRU_WIDTH), F32),
            jax.ShapeDtypeStruct((n, LRU_WIDTH), F32)]
    return pl.pallas_call(
        _sample_in_kernel,
        out_shape=outs,
        compiler_params=pltpu.CompilerParams(vmem_limit_bytes=VMEM_LIMIT),
        name="sample_inproj_rglru",
    )(x, sh1, sc1, w_in_p, ctx, h0, conv_w, conv_b, wlo, whi, bgate, lam)


def _sample_attn_kernel(q_ref, kn_ref, vn_ref, ck_ref, cv_ref, sink_ref, y_ref, nk_ref, nv_ref, *, bb):
    rows = lax.broadcasted_iota(jnp.int32, (WINDOW, KV_WIDTH), 0)
    nh = N_HEADS
    q_all = q_ref[...].reshape(bb * nh, LANES)
    kcat = ck_ref[...].reshape(bb * WINDOW, KV_WIDTH)
    vcat = cv_ref[...].reshape(bb * WINDOW, KV_WIDTH)
    kn_rep = jnp.broadcast_to(kn_ref[...][:, None, :], (bb, nh, KV_WIDTH)).reshape(bb * nh, KV_WIDTH)
    vn_rep = jnp.broadcast_to(vn_ref[...][:, None, :], (bb, nh, KV_WIDTH)).reshape(bb * nh, KV_WIDTH)
    sink = jnp.concatenate([sink_ref[...]] * bb, axis=0)
    s_full = _dot_nt(q_all, kcat, True)
    s = jnp.concatenate([s_full[nh * b:nh * (b + 1), WINDOW * b:WINDOW * (b + 1)] for b in range(bb)],
                        axis=0) * ATTN_SCALE
    s_self = jnp.sum(q_all * kn_rep, axis=-1, keepdims=True) * ATTN_SCALE
    m = jnp.maximum(jnp.maximum(jnp.max(s, axis=-1, keepdims=True), s_self), sink)
    e = jnp.exp(s - m)
    e_self = jnp.exp(s_self - m)
    den = jnp.sum(e, axis=-1, keepdims=True) + e_self + jnp.exp(sink - m)
    inv = 1.0 / den
    p = e * inv
    zero = jnp.zeros((nh, WINDOW), F32)
    p_wide = jnp.concatenate(
        [jnp.concatenate([p[nh * b:nh * (b + 1)] if c == b else zero for c in range(bb)], axis=-1)
         for b in range(bb)], axis=0)
    o = _dot(p_wide, vcat, True) + (e_self * inv) * vn_rep
    y_ref[...] = o.reshape(bb, nh, LANES)
    for b in range(bb):
        nk_ref[b] = jnp.where(rows == WINDOW - 1, kn_ref[b:b + 1, :], pltpu.roll(ck_ref[b], WINDOW - 1, axis=0))
        nv_ref[b] = jnp.where(rows == WINDOW - 1, vn_ref[b:b + 1, :], pltpu.roll(cv_ref[b], WINDOW - 1, axis=0))


def _sample_attn(q3, kn, vn, cache_k, cache_v, sinks, bb=16):
    n = DEC_BATCH
    kern = functools.partial(_sample_attn_kernel, bb=bb)
    return pl.pallas_call(
        kern,
        grid=(n // bb,),
        in_specs=[pl.BlockSpec((bb, N_HEADS, LANES), lambda i: (i, 0, 0)),
                  pl.BlockSpec((bb, KV_WIDTH), lambda i: (i, 0)),
                  pl.BlockSpec((bb, KV_WIDTH), lambda i: (i, 0)),
                  pl.BlockSpec((bb, WINDOW, KV_WIDTH), lambda i: (i, 0, 0)),
                  pl.BlockSpec((bb, WINDOW, KV_WIDTH), lambda i: (i, 0, 0)),
                  pl.BlockSpec((N_HEADS, 1), lambda i: (0, 0))],
        out_specs=[pl.BlockSpec((bb, N_HEADS, LANES), lambda i: (i, 0, 0)),
                   pl.BlockSpec((bb, WINDOW, KV_WIDTH), lambda i: (i, 0, 0)),
                   pl.BlockSpec((bb, WINDOW, KV_WIDTH), lambda i: (i, 0, 0))],
        out_shape=[jax.ShapeDtypeStruct((n, N_HEADS, LANES), F32),
                   jax.ShapeDtypeStruct((n, WINDOW, KV_WIDTH), F32),
                   jax.ShapeDtypeStruct((n, WINDOW, KV_WIDTH), F32)],
        compiler_params=_cparams(("arbitrary",)),
        name="sample_cache_attention",
    )(q3, kn, vn, cache_k, cache_v, sinks.reshape(N_HEADS, 1))


def _sample_out_kernel(x_ref, ylru_ref, yatt_ref, sh2_ref, sc2_ref, gt1_ref, glru_ref, gattn_ref, wout_ref,
                       ln1g_ref, ln1b_ref, wr_ref, br_ref, x1_ref, comb_ref):
    low = lax.broadcasted_iota(jnp.int32, (DEC_BATCH, LANES), 1) < HEAD_DIM
    yatt = jnp.concatenate(
        [jnp.where(low, yatt_ref[pl.ds(c, DEC_BATCH, stride=N_HEADS), :],
                   yatt_ref[pl.ds(c + 4, DEC_BATCH, stride=N_HEADS), :]) for c in range(4)], axis=-1)
    x1, comb = _outproj_body(x_ref[...], ylru_ref[...], yatt, sh2_ref[...], sc2_ref[...], gt1_ref[...],
                             glru_ref[...], gattn_ref[...], wout_ref[...], ln1g_ref[...], ln1b_ref[...],
                             wr_ref[...], br_ref[...], True)
    x1_ref[...] = x1
    comb_ref[...] = comb


def _sample_out(x, ylru, yatt2d, sh2, sc2, gt1, glru, gattn, wout_p, ln1g, ln1b, wr, br):
    n = DEC_BATCH
    return pl.pallas_call(
        _sample_out_kernel,
        out_shape=[jax.ShapeDtypeStruct((n, D_MODEL), F32), jax.ShapeDtypeStruct((n, ROUTE_LANES), F32)],
        compiler_params=pltpu.CompilerParams(vmem_limit_bytes=VMEM_LIMIT),
        name="sample_outproj_ln_route",
    )(x, ylru, yatt2d, sh2, sc2, gt1, glru, gattn, wout_p, ln1g, ln1b, wr, br)


def _block_diag_halves(w_a, w_x):
    def bd(w4):
        eye = jnp.eye(4, dtype=w4.dtype)
        return (w4[:, :, None, :] * eye[:, None, :, None]).reshape(256, 256)
    lo = jnp.concatenate([bd(w_a[:4]), bd(w_x[:4])], axis=1)
    hi = jnp.concatenate([bd(w_a[4:]), bd(w_x[4:])], axis=1)
    return lo, hi


def kernel(x_prompt, x_sample, c_prompt, c_sample, state_conv, state_h, cache_k, cache_v, w_ada, b_ada, w_in,
           conv_w, conv_b, w_rg_a, b_rg_a, w_rg_x, b_rg_x, lru_lambda, sinks, g_lru, g_attn, w_out, ln1_g, ln1_b,
           w_group, b_group, w_router, b_router, w_gate, w_up, w_down, ln2_g, ln2_b):
    d = D_MODEL
    perm = jnp.asarray(HEAD_PERM)
    w_in0 = w_in[0]
    q0 = 2 * LRU_WIDTH
    w_in_p = jnp.concatenate([w_in0[:, :q0], w_in0[:, q0:q0 + ATTN_WIDTH][:, perm], w_in0[:, q0 + ATTN_WIDTH:]],
                             axis=1)
    w_out0 = w_out[0]
    w_out_p = jnp.concatenate([w_out0[:LRU_WIDTH], w_out0[LRU_WIDTH:][perm]], axis=0)
    g_attn_p = g_attn[0][perm].reshape(1, -1)
    glru = g_lru[0].reshape(1, -1)
    wlo, whi = _block_diag_halves(w_rg_a[0], w_rg_x[0])
    bgate = jnp.concatenate([b_rg_a[0].reshape(-1), b_rg_x[0].reshape(-1)]).reshape(1, -1)
    lam = lru_lambda[0].reshape(1, -1)
    convw = conv_w[0]
    convb = conv_b[0].reshape(1, -1)
    ln1g, ln1b = ln1_g[0].reshape(1, -1), ln1_b[0].reshape(1, -1)
    ln2g, ln2b = ln2_g[0].reshape(1, -1), ln2_b[0].reshape(1, -1)
    wr = jnp.concatenate([jnp.transpose(w_router[0], (1, 0, 2)).reshape(d, N_EXPERTS), w_group[0],
                          jnp.zeros((d, ROUTE_LANES - N_EXPERTS - N_GROUPS), F32)], axis=1)
    br = jnp.concatenate([b_router[0].reshape(-1), b_group[0],
                          jnp.zeros((ROUTE_LANES - N_EXPERTS - N_GROUPS,), F32)]).reshape(1, -1)
    sink_p = sinks[0]

    c_all = jnp.concatenate([c_prompt, jnp.zeros((8 - BATCH, d), F32), c_sample], axis=0)
    mod = _ada(c_all, w_ada[0], b_ada[0])
    modp = mod[:BATCH].reshape(BATCH, 6, d)
    mods = mod[8:]
    sh1_s, sc1_s, gt1_s, sh2_s, sc2_s, gt2_s = (mods[:, k * d:(k + 1) * d] for k in range(6))

    zin = _inproj(x_prompt, modp, w_in_p.astype(BF16))
    ylru, cstate8, hlast8 = _lru(zin, convw, convb, wlo.astype(BF16), whi.astype(BF16), bgate, lam)
    yatt = _attn(zin, sink_p)
    n_p = BATCH * SEQ
    x1_p, info, cntf = _outproj_prompt(x_prompt.reshape(n_p, d), ylru.reshape(n_p, LRU_WIDTH),
                                       yatt.reshape(n_p, ATTN_WIDTH), modp, glru, g_attn_p, w_out_p.astype(BF16),
                                       ln1g, ln1b, wr, br, tm=RB_SUB)
    cnt = cntf[:, 0, :N_EXPERTS].astype(jnp.int32)
    offs = _rb_offsets(cnt, info[:, 0:2].astype(jnp.int32), info[:, 4:6].astype(jnp.int32))
    wg_b, wu_b, wd_b = w_gate[0].astype(BF16), w_up[0].astype(BF16), w_down[0].astype(BF16)
    y_p = _rb_moe(x1_p, modp, cnt, offs, info[:, 2:4], wg_b, wu_b, wd_b, ln2g, ln2b)

    ylru_s, q2d, kn, vn, cstate_s, hnew_s = _sample_in(
        x_sample.reshape(DEC_BATCH, d), sh1_s, sc1_s, w_in_p, state_conv[0], state_h[0],
        convw, convb, wlo, whi, bgate, lam)
    yatt3, newk, newv = _sample_attn(q2d.reshape(DEC_BATCH, N_HEADS, LANES), kn, vn,
                                     cache_k[0].reshape(DEC_BATCH, WINDOW, KV_WIDTH),
                                     cache_v[0].reshape(DEC_BATCH, WINDOW, KV_WIDTH), sink_p)
    x1_s, comb_s = _sample_out(x_sample.reshape(DEC_BATCH, d), ylru_s, yatt3.reshape(DEC_BATCH * N_HEADS, LANES),
                               sh2_s, sc2_s, gt1_s, glru, g_attn_p, w_out_p, ln1g, ln1b, wr, br)
    y_s = _moe_dense(x1_s, comb_s, sh2_s, sc2_s, gt2_s, 0, wg_b, wu_b, wd_b, ln2g, ln2b, DEC_BATCH)

    kq = 2 * LRU_WIDTH + ATTN_WIDTH
    return (y_p.reshape(BATCH, SEQ, d),
            y_s.reshape(DEC_BATCH, 1, d),
            cstate8[:, 5:8][None],
            hlast8[:, 7][None],
            zin[:, SEQ - WINDOW:, kq:kq + KV_WIDTH].reshape(1, BATCH, WINDOW, N_KV_HEADS, HEAD_DIM),
            zin[:, SEQ - WINDOW:, kq + KV_WIDTH:].reshape(1, BATCH, WINDOW, N_KV_HEADS, HEAD_DIM),
            cstate_s[None],
            hnew_s[None],
            newk.reshape(1, DEC_BATCH, WINDOW, N_KV_HEADS, HEAD_DIM),
            newv.reshape(1, DEC_BATCH, WINDOW, N_KV_HEADS, HEAD_DIM))
```

```python
import functools

import jax
import jax.numpy as jnp
import numpy as np
from jax import lax
from jax.experimental import pallas as pl
from jax.experimental.pallas import tpu as pltpu

F32 = jnp.float32
BF16 = jnp.bfloat16
HIGHEST = lax.Precision.HIGHEST

D_MODEL = 1024
BATCH = 4
SEQ = 4096
DEC_BATCH = 128
LRU_WIDTH = 512
LRU_BLOCKS = 8
LRU_BLOCK = 64
CONV_WIDTH = 4
LRU_C = 8.0
N_HEADS = 8
N_KV_HEADS = 2
HEAD_DIM = 64
ATTN_WIDTH = 512
KV_WIDTH = 128
WINDOW = 128
IN_WIDTH = 2 * LRU_WIDTH + ATTN_WIDTH + 2 * KV_WIDTH
N_GROUPS = 4
EXPERTS_PER_GROUP = 8
N_EXPERTS = 32
D_EXPERT = 256
DEEPNORM_ALPHA = 2.0 ** 0.25
LN_EPS = 1e-5
RMS_EPS = 1e-6
ATTN_SCALE = HEAD_DIM ** -0.5

LANES = 128
ROUTE_LANES = 128
ROUTE_INFO = 40
VMEM_LIMIT = 56 * 1024 * 1024

HEAD_PERM = np.concatenate(
    [np.concatenate([np.arange(64 * c, 64 * c + 64), np.arange(64 * (c + 4), 64 * (c + 4) + 64)])
     for c in range(4)])


def _cparams(sem):
    return pltpu.CompilerParams(dimension_semantics=sem, vmem_limit_bytes=VMEM_LIMIT)


def _dot(a, b, exact):
    if exact:
        return jnp.dot(a, b, precision=HIGHEST, preferred_element_type=F32)
    return jnp.dot(a.astype(BF16), b.astype(BF16), preferred_element_type=F32)


def _dot_nt(a, b, exact):
    dn = (((1,), (1,)), ((), ()))
    if exact:
        return lax.dot_general(a, b, dn, precision=HIGHEST, preferred_element_type=F32)
    return lax.dot_general(a.astype(BF16), b.astype(BF16), dn, preferred_element_type=F32)


def _sigmoid(x):
    return 1.0 / (1.0 + jnp.exp(-x))


def _silu(x):
    return x * _sigmoid(x)


def _gelu_tanh(x):
    return 0.5 * x * (1.0 + jnp.tanh(np.sqrt(2.0 / np.pi).astype(np.float32) * (x + 0.044715 * (x * x * x))))


def _softplus(x):
    return jnp.maximum(x, 0.0) + jnp.log1p(jnp.exp(-jnp.abs(x)))


def _layer_norm(x, g, b):
    mu = jnp.mean(x, axis=-1, keepdims=True)
    xc = x - mu
    var = jnp.mean(xc * xc, axis=-1, keepdims=True)
    return xc * lax.rsqrt(var + LN_EPS) * g + b


def _rms_norm(x, g):
    return x * lax.rsqrt(jnp.mean(x * x, axis=-1, keepdims=True) + RMS_EPS) * g


def _ada_kernel(c_ref, w_ref, b_ref, o_ref):
    o_ref[...] = _dot(_silu(c_ref[...]), w_ref[...], True) + b_ref[...]


def _ada(c_all, w_ada, b_ada):
    rows = c_all.shape[0]
    bn = 512
    return pl.pallas_call(
        _ada_kernel,
        grid=(6 * D_MODEL // bn,),
        in_specs=[pl.BlockSpec((rows, D_MODEL), lambda j: (0, 0)),
                  pl.BlockSpec((D_MODEL, bn), lambda j: (0, j)),
                  pl.BlockSpec((1, bn), lambda j: (0, j))],
        out_specs=pl.BlockSpec((rows, bn), lambda j: (0, j)),
        out_shape=jax.ShapeDtypeStruct((rows, 6 * D_MODEL), F32),
        compiler_params=_cparams(("arbitrary",)),
        name="ada_modulation",
    )(c_all, w_ada, b_ada.reshape(1, -1))


def _inproj_kernel(x_ref, mod_ref, w_ref, o_ref):
    sh1 = mod_ref[0, 0:1, :]
    sc1 = mod_ref[0, 1:2, :]
    h = x_ref[0] * (1.0 + sc1) + sh1
    o_ref[0] = _dot(h, w_ref[...], False)


def _inproj(x, modp, w_in_bf16, tm=512):
    b, t, d = x.shape
    return pl.pallas_call(
        _inproj_kernel,
        grid=(b, t // tm),
        in_specs=[pl.BlockSpec((1, tm, d), lambda i, j: (i, j, 0)),
                  pl.BlockSpec((1, 6, d), lambda i, j: (i, 0, 0)),
                  pl.BlockSpec((d, IN_WIDTH), lambda i, j: (0, 0))],
        out_specs=pl.BlockSpec((1, tm, IN_WIDTH), lambda i, j: (i, j, 0)),
        out_shape=jax.ShapeDtypeStruct((b, t, IN_WIDTH), F32),
        compiler_params=_cparams(("arbitrary", "arbitrary")),
        name="prompt_inproj",
    )(x, modp, w_in_bf16)


def _lru_gates(xc, wlo, whi, bgate, sp_neg_lam, exact):
    g_lo = _dot(xc[:, :256], wlo, exact)
    g_hi = _dot(xc[:, 256:], whi, exact)
    ga = jnp.concatenate([g_lo[:, :256], g_hi[:, :256]], axis=-1) + bgate[:, :LRU_WIDTH]
    gx = jnp.concatenate([g_lo[:, 256:], g_hi[:, 256:]], axis=-1) + bgate[:, LRU_WIDTH:]
    r = _sigmoid(ga)
    i = _sigmoid(gx)
    log_a = -LRU_C * r * sp_neg_lam
    a = jnp.exp(log_a)
    one_minus_a2 = -jnp.tanh(log_a) * (a * a + 1.0)
    bterm = jnp.sqrt(one_minus_a2) * (i * xc)
    return a, bterm


def _lru_kernel(z_ref, convw_ref, convb_ref, wlo_ref, whi_ref, bgate_ref, lam_ref,
                y_ref, cstate_ref, hlast_ref, tail_ref, carry_ref, *, tl):
    j = pl.program_id(1)

    @pl.when(j == 0)
    def _():
        tail_ref[...] = jnp.zeros_like(tail_ref)
        carry_ref[...] = jnp.zeros_like(carry_ref)

    xb = z_ref[0, :, :LRU_WIDTH]
    gate = z_ref[0, :, LRU_WIDTH:]
    rows = lax.broadcasted_iota(jnp.int32, (tl, LRU_WIDTH), 0)

    xc = convb_ref[...] + convw_ref[3:4, :] * xb
    rows8 = lax.broadcasted_iota(jnp.int32, (8, LRU_WIDTH), 0)
    tail = tail_ref[...]
    for back in (1, 2, 3):
        rolled = pltpu.roll(xb, back, axis=0)
        top = jnp.where(rows8 >= back, rolled[:8], pltpu.roll(tail, back, axis=0))
        shifted = jnp.concatenate([top, rolled[8:]], axis=0)
        xc = xc + convw_ref[3 - back:4 - back, :] * shifted
    tail_ref[...] = xb[tl - 8:, :]
    cstate_ref[0] = xb[tl - 8:, :]

    sp = _softplus(-lam_ref[...])
    a, bterm = _lru_gates(xc, wlo_ref[...], whi_ref[...], bgate_ref[...], sp, False)

    s = 1
    while s < tl:
        if s < 8:
            a_sh = jnp.where(rows >= s, pltpu.roll(a, s, axis=0), 1.0)
            b_sh = jnp.where(rows >= s, pltpu.roll(bterm, s, axis=0), 0.0)
        else:
            a_sh = jnp.concatenate([jnp.ones((s, LRU_WIDTH), F32), a[:tl - s]], axis=0)
            b_sh = jnp.concatenate([jnp.zeros((s, LRU_WIDTH), F32), bterm[:tl - s]], axis=0)
        bterm = a * b_sh + bterm
        a = a * a_sh
        s *= 2
    h = a * carry_ref[7:8, :] + bterm
    carry_ref[...] = h[tl - 8:, :]
    hlast_ref[0] = h[tl - 8:, :]
    y_ref[0] = h * _gelu_tanh(gate)


def _lru(zin, conv_w, conv_b, wlo, whi, bgate, lam, tl=512):
    b, t, _ = zin.shape
    kern = functools.partial(_lru_kernel, tl=tl)
    full = lambda shp: pl.BlockSpec(shp, lambda i, j: tuple(0 for _ in shp))
    return pl.pallas_call(
        kern,
        grid=(b, t // tl),
        in_specs=[pl.BlockSpec((1, tl, 2 * LRU_WIDTH), lambda i, j: (i, j, 0)),
                  full((CONV_WIDTH, LRU_WIDTH)), full((1, LRU_WIDTH)),
                  full((256, 512)), full((256, 512)), full((1, 2 * LRU_WIDTH)), full((1, LRU_WIDTH))],
        out_specs=[pl.BlockSpec((1, tl, LRU_WIDTH), lambda i, j: (i, j, 0)),
                   pl.BlockSpec((1, 8, LRU_WIDTH), lambda i, j: (i, 0, 0)),
                   pl.BlockSpec((1, 8, LRU_WIDTH), lambda i, j: (i, 0, 0))],
        out_shape=[jax.ShapeDtypeStruct((b, t, LRU_WIDTH), F32),
                   jax.ShapeDtypeStruct((b, 8, LRU_WIDTH), F32),
                   jax.ShapeDtypeStruct((b, 8, LRU_WIDTH), F32)],
        scratch_shapes=[pltpu.VMEM((8, LRU_WIDTH), F32), pltpu.VMEM((8, LRU_WIDTH), F32)],
        compiler_params=_cparams(("arbitrary", "arbitrary")),
        name="prompt_rglru",
    )(zin, conv_w, conv_b, wlo, whi, bgate, lam)


ATTN_BLOCKS = 8


def _attn_kernel(q_ref, k_ref, v_ref, sink_ref, o_ref, kprev_ref, vprev_ref):
    j = pl.program_id(1)

    @pl.when(j == 0)
    def _():
        kprev_ref[...] = jnp.zeros_like(kprev_ref)
        vprev_ref[...] = jnp.zeros_like(vprev_ref)

    blk = WINDOW
    lane = lax.broadcasted_iota(jnp.int32, (blk, LANES), 1)
    low = lane < HEAD_DIM
    qi = lax.broadcasted_iota(jnp.int32, (blk, 2 * blk), 0)
    sj = lax.broadcasted_iota(jnp.int32, (blk, 2 * blk), 1)
    rel = blk + qi - sj
    in_window = (rel >= 0) & (rel <= WINDOW)
    sink = sink_ref[...].reshape(N_HEADS, blk, 1)
    k_ext = jnp.concatenate([kprev_ref[...], k_ref[0]], axis=0).astype(BF16)
    v_ext = jnp.concatenate([vprev_ref[...], v_ref[0]], axis=0).astype(BF16)
    v_ext = jnp.concatenate([v_ext, jnp.ones_like(v_ext)], axis=-1)
    for n in range(ATTN_BLOCKS):
        q = q_ref[0, blk * n:blk * (n + 1), :]
        pieces = []
        for half in (0, 1):
            for c in range(4):
                qc = q[:, LANES * c:LANES * (c + 1)]
                pieces.append(jnp.where(low if half == 0 else ~low, qc, 0.0).astype(BF16))
        q8 = jnp.concatenate(pieces, axis=0)
        k_band = k_ext[blk * n:blk * (n + 2)]
        v_band = v_ext[blk * n:blk * (n + 2)]
        s = _dot_nt(q8, k_band, False) * ATTN_SCALE
        s = s.reshape(N_HEADS, blk, 2 * blk)
        valid = in_window & ((sj >= blk) | (j > 0)) if n == 0 else in_window
        s = jnp.where(valid[None], s, -jnp.inf)
        m = jnp.maximum(jnp.max(s, axis=-1, keepdims=True), sink)
        e = jnp.exp(s - m).reshape(N_HEADS * blk, 2 * blk)
        ov = _dot(e, v_band, False)
        den = ov[:, KV_WIDTH:] + jnp.exp(sink - m).reshape(N_HEADS * blk, 1)
        o8 = ov[:, :KV_WIDTH] * (1.0 / den)
        cols = []
        for c in range(4):
            cols.append(jnp.where(low, o8[blk * c:blk * (c + 1)], o8[blk * (c + 4):blk * (c + 5)]))
        o_ref[0, blk * n:blk * (n + 1), :] = jnp.concatenate(cols, axis=-1)
    kprev_ref[...] = k_ref[0, blk * (ATTN_BLOCKS - 1):, :]
    vprev_ref[...] = v_ref[0, blk * (ATTN_BLOCKS - 1):, :]


def _attn(zin, sinks):
    b, t, _ = zin.shape
    blk = WINDOW
    tq = blk * ATTN_BLOCKS
    sink_col = jnp.repeat(sinks.astype(F32), blk).reshape(N_HEADS * blk, 1)
    return pl.pallas_call(
        _attn_kernel,
        grid=(b, t // tq),
        in_specs=[pl.BlockSpec((1, tq, ATTN_WIDTH), lambda i, j: (i, j, 2)),
                  pl.BlockSpec((1, tq, KV_WIDTH), lambda i, j: (i, j, 12)),
                  pl.BlockSpec((1, tq, KV_WIDTH), lambda i, j: (i, j, 13)),
                  pl.BlockSpec((N_HEADS * blk, 1), lambda i, j: (0, 0))],
        out_specs=pl.BlockSpec((1, tq, ATTN_WIDTH), lambda i, j: (i, j, 0)),
        out_shape=jax.ShapeDtypeStruct((b, t, ATTN_WIDTH), F32),
        scratch_shapes=[pltpu.VMEM((blk, KV_WIDTH), F32), pltpu.VMEM((blk, KV_WIDTH), F32)],
        compiler_params=_cparams(("arbitrary", "arbitrary")),
        name="prompt_window_attention",
    )(zin, zin, zin, sink_col)


def _dot_split3(a, b):
    a_hi = a.astype(BF16)
    b_hi = b.astype(BF16)
    a_lo = (a - a_hi.astype(F32)).astype(BF16)
    b_lo = (b - b_hi.astype(F32)).astype(BF16)
    return (jnp.dot(a_hi, b_hi, preferred_element_type=F32) + jnp.dot(a_hi, b_lo, preferred_element_type=F32)
            + jnp.dot(a_lo, b_hi, preferred_element_type=F32))


def _route(h2, wr, br, exact):
    t = h2.shape[0]
    logits = (_dot(h2, wr, True) if exact else _dot_split3(h2, wr)) + br
    lane = lax.broadcasted_iota(jnp.int32, (t, ROUTE_LANES), 1).astype(F32)
    neg = -jnp.inf
    big = float(ROUTE_LANES)
    is_g = (lane >= N_EXPERTS) & (lane < N_EXPERTS + N_GROUPS)
    lg = jnp.where(is_g, logits, neg)
    mg = jnp.max(lg, axis=-1, keepdims=True)
    eg = jnp.where(is_g, jnp.exp(lg - mg), 0.0)
    pg = eg / jnp.sum(eg, axis=-1, keepdims=True)
    g_val = jnp.max(pg, axis=-1, keepdims=True)
    g_lane = jnp.min(jnp.where((pg == g_val) & is_g, lane, big), axis=-1, keepdims=True)
    g_idx = g_lane - N_EXPERTS
    in_grp = (lane >= g_idx * EXPERTS_PER_GROUP) & (lane < (g_idx + 1.0) * EXPERTS_PER_GROUP)
    le = jnp.where(in_grp, logits, neg)
    me = jnp.max(le, axis=-1, keepdims=True)
    ee = jnp.where(in_grp, jnp.exp(le - me), 0.0)
    pe = ee / jnp.sum(ee, axis=-1, keepdims=True)
    v1 = jnp.max(pe, axis=-1, keepdims=True)
    l1 = jnp.min(jnp.where((pe == v1) & in_grp, lane, big), axis=-1, keepdims=True)
    rest = in_grp & (lane != l1)
    pe2 = jnp.where(rest, pe, -1.0)
    v2 = jnp.max(pe2, axis=-1, keepdims=True)
    l2 = jnp.min(jnp.where((pe2 == v2) & rest, lane, big), axis=-1, keepdims=True)
    tot = v1 + v2
    w1 = g_val * v1 / tot
    w2 = g_val * v2 / tot
    comb = jnp.where(lane == l1, w1, 0.0) + jnp.where(lane == l2, w2, 0.0)
    return (comb + jnp.where(lane == ROUTE_INFO, l1, 0.0) + jnp.where(lane == ROUTE_INFO + 1, l2, 0.0)
            + jnp.where(lane == ROUTE_INFO + 2, w1, 0.0) + jnp.where(lane == ROUTE_INFO + 3, w2, 0.0))


def _outproj_body(x, ylru, yatt, sh2, sc2, gt1, glru, gattn, wout, ln1g, ln1b, wr, br, exact):
    mixin = jnp.concatenate([_rms_norm(ylru, glru), _rms_norm(yatt, gattn)], axis=-1)
    mix = _dot(mixin, wout, exact)
    x1 = _layer_norm(DEEPNORM_ALPHA * x + (1.0 + gt1) * mix, ln1g, ln1b)
    h2 = x1 * (1.0 + sc2) + sh2
    return x1, _route(h2, wr, br, exact)


def _outproj_prompt_kernel(x_ref, ylru_ref, yatt_ref, mod_ref, glru_ref, gattn_ref, wout_ref,
                           ln1g_ref, ln1b_ref, wr_ref, br_ref, x1_ref, info_ref, cnt_ref, tri_ref, carry_ref,
                           *, tm, per_seq):
    i = pl.program_id(0)

    @pl.when(i == 0)
    def _():
        r = lax.broadcasted_iota(jnp.int32, (tm, tm), 0)
        c = lax.broadcasted_iota(jnp.int32, (tm, tm), 1)
        tri_ref[...] = jnp.where(c < r, 1.0, 0.0).astype(BF16)

    @pl.when(i % per_seq == 0)
    def _():
        carry_ref[...] = jnp.zeros_like(carry_ref)

    gt1 = mod_ref[0, 2:3, :]
    sh2 = mod_ref[0, 3:4, :]
    sc2 = mod_ref[0, 4:5, :]
    combs = []
    nsplit = 2
    for h in range(nsplit):
        rows = slice(h * (tm // nsplit), (h + 1) * (tm // nsplit))
        x1_h, comb_h = _outproj_body(x_ref[rows, :], ylru_ref[rows, :], yatt_ref[rows, :], sh2, sc2, gt1,
                                     glru_ref[...], gattn_ref[...], wout_ref[...], ln1g_ref[...], ln1b_ref[...],
                                     wr_ref[...], br_ref[...], False)
        x1_ref[rows, :] = x1_h
        combs.append(comb_h)
    comb = jnp.concatenate(combs, axis=0)
    lane = lax.broadcasted_iota(jnp.int32, (tm, ROUTE_LANES), 1).astype(F32)
    l1 = jnp.sum(jnp.where(lane == ROUTE_INFO, comb, 0.0), axis=-1, keepdims=True)
    l2 = jnp.sum(jnp.where(lane == ROUTE_INFO + 1, comb, 0.0), axis=-1, keepdims=True)
    o1 = lane == l1
    o2 = lane == l2
    onehot = jnp.where(o1 | o2, 1.0, 0.0)
    before = jnp.dot(tri_ref[...], onehot.astype(BF16), preferred_element_type=F32) + carry_ref[0:1, :]
    rank1 = jnp.sum(jnp.where(o1, before, 0.0), axis=-1, keepdims=True)
    rank2 = jnp.sum(jnp.where(o2, before, 0.0), axis=-1, keepdims=True)
    total = carry_ref[0:1, :] + jnp.sum(onehot, axis=0, keepdims=True)
    carry_ref[...] = jnp.broadcast_to(total, carry_ref.shape)
    cnt_ref[0] = jnp.broadcast_to(total, (8, ROUTE_LANES))
    info = (comb + jnp.where(lane == ROUTE_INFO + 4, rank1, 0.0) + jnp.where(lane == ROUTE_INFO + 5, rank2, 0.0))
    info_ref[0] = jnp.transpose(info)[ROUTE_INFO:ROUTE_INFO + 8, :]


OUTPROJ_TILE = 1024


def _outproj_prompt(x2d, ylru2d, yatt2d, modp, glru, gattn, wout_bf16, ln1g, ln1b, wr, br, tm=OUTPROJ_TILE):
    n, d = x2d.shape
    per_seq = SEQ // tm
    full = lambda shp: pl.BlockSpec(shp, lambda i: tuple(0 for _ in shp))
    kern = functools.partial(_outproj_prompt_kernel, tm=tm, per_seq=per_seq)
    return pl.pallas_call(
        kern,
        grid=(n // tm,),
        in_specs=[pl.BlockSpec((tm, d), lambda i: (i, 0)),
                  pl.BlockSpec((tm, LRU_WIDTH), lambda i: (i, 0)),
                  pl.BlockSpec((tm, ATTN_WIDTH), lambda i: (i, 0)),
                  pl.BlockSpec((1, 6, d), lambda i: (i // per_seq, 0, 0)),
                  full((1, LRU_WIDTH)), full((1, ATTN_WIDTH)), full((d, d)),
                  full((1, d)), full((1, d)), full((d, ROUTE_LANES)), full((1, ROUTE_LANES))],
        out_specs=[pl.BlockSpec((tm, d), lambda i: (i, 0)),
                   pl.BlockSpec((1, 8, tm), lambda i: (i, 0, 0)),
                   pl.BlockSpec((1, 8, ROUTE_LANES), lambda i: (i // per_seq, 0, 0))],
        out_shape=[jax.ShapeDtypeStruct((n, d), F32),
                   jax.ShapeDtypeStruct((n // tm, 8, tm), F32),
                   jax.ShapeDtypeStruct((n // SEQ, 8, ROUTE_LANES), F32)],
        scratch_shapes=[pltpu.VMEM((tm, tm), BF16), pltpu.VMEM((8, ROUTE_LANES), F32)],
        compiler_params=_cparams(("arbitrary",)),
        name="prompt_outproj_ln_route",
    )(x2d, ylru2d, yatt2d, modp, glru, gattn, wout_bf16, ln1g, ln1b, wr, br)


def _moe_kernel(x1_ref, comb_ref, sh2_ref, sc2_ref, gt2_ref, wg_ref, wu_ref, wd_ref, ln2g_ref, ln2b_ref,
                o_ref, h2_ref, acc_ref):
    e = pl.program_id(1)

    @pl.when(e == 0)
    def _():
        h2_ref[...] = (x1_ref[...] * (1.0 + sc2_ref[...]) + sh2_ref[...]).astype(BF16)
        acc_ref[...] = jnp.zeros_like(acc_ref)

    h2 = h2_ref[...]
    a = jnp.dot(h2, wg_ref[0].astype(BF16), preferred_element_type=F32)
    u = jnp.dot(h2, wu_ref[0].astype(BF16), preferred_element_type=F32)
    comb = comb_ref[...]
    lane = lax.broadcasted_iota(jnp.int32, comb.shape, 1)
    c_e = jnp.sum(jnp.where(lane == e, comb, 0.0), axis=-1, keepdims=True)
    z = _silu(a) * u * c_e
    acc_ref[...] += jnp.dot(z.astype(BF16), wd_ref[0].astype(BF16), preferred_element_type=F32)

    @pl.when(e == N_EXPERTS - 1)
    def _():
        o_ref[...] = _layer_norm(DEEPNORM_ALPHA * x1_ref[...] + (1.0 + gt2_ref[...]) * acc_ref[...],
                                 ln2g_ref[...], ln2b_ref[...])


def _moe_dense(x1, comb, sh2, sc2, gt2, mod_rows_per_tile, w_gate, w_up, w_down, ln2g, ln2b, tm):
    n, d = x1.shape
    if mod_rows_per_tile:
        mspec = pl.BlockSpec((1, 1, d), lambda i, e: (i // mod_rows_per_tile, 0, 0))
        sh2, sc2, gt2 = (m.reshape(-1, 1, d) for m in (sh2, sc2, gt2))
        kern = lambda x1r, cr, s1, s2, s3, *rest: _moe_kernel(x1r, cr, s1.at[0], s2.at[0], s3.at[0], *rest)
    else:
        mspec = pl.BlockSpec((tm, d), lambda i, e: (i, 0))
        kern = _moe_kernel
    full = lambda shp: pl.BlockSpec(shp, lambda i, e: tuple(0 for _ in shp))
    return pl.pallas_call(
        kern,
        grid=(n // tm, N_EXPERTS),
        in_specs=[pl.BlockSpec((tm, d), lambda i, e: (i, 0)),
                  pl.BlockSpec((tm, ROUTE_LANES), lambda i, e: (i, 0)),
                  mspec, mspec, mspec,
                  pl.BlockSpec((1, d, D_EXPERT), lambda i, e: (e, 0, 0)),
                  pl.BlockSpec((1, d, D_EXPERT), lambda i, e: (e, 0, 0)),
                  pl.BlockSpec((1, D_EXPERT, d), lambda i, e: (e, 0, 0)),
                  full((1, d)), full((1, d))],
        out_specs=pl.BlockSpec((tm, d), lambda i, e: (i, 0)),
        out_shape=jax.ShapeDtypeStruct((n, d), F32),
        scratch_shapes=[pltpu.VMEM((tm, d), BF16), pltpu.VMEM((tm, d), F32)],
        compiler_params=_cparams(("arbitrary", "arbitrary")),
        name="moe_dense_ln",
    )(x1, comb, sh2, sc2, gt2, w_gate, w_up, w_down, ln2g, ln2b)


RB_SUB = 512
RB_NSUB = SEQ // RB_SUB
RB_CHUNK = 128
RB_CHUNK_BITS = 7
RB_NCHUNK = 2 * SEQ // RB_CHUNK
RB_PITCH = RB_CHUNK + 8
RB_SPITCH = RB_SUB + 8
RB_GROUP = 3
RB_WSLOTS = 4


def _rb_kernel(cnt_ref, x1_ref, mod_ref, offs_ref, wts_ref, wg_hbm, wu_hbm, wd_hbm, ln2g_ref, ln2b_ref,
               o_ref, buf_ref, stage_ref, wg_buf, wu_buf, wd_buf, start_ref, sem):
    b = pl.program_id(0)
    s = pl.program_id(1)

    @pl.when(s == 0)
    def _starts():
        def body(e, run):
            start_ref[e] = run
            return run + cnt_ref[b, e]
        lax.fori_loop(0, N_EXPERTS, body, jnp.int32(0))
        buf_ref[RB_NCHUNK * 8 * RB_PITCH:(RB_NCHUNK + RB_GROUP) * 8 * RB_PITCH, :] = jnp.zeros(
            (RB_GROUP * 8 * RB_PITCH, LANES), F32)

    @pl.when(s < RB_NSUB)
    def _dispatch():
        sh2 = mod_ref[0, 3:4, :]
        sc2 = mod_ref[0, 4:5, :]
        h2 = x1_ref[...] * (1.0 + sc2) + sh2
        for j in range(8):
            stage_ref[RB_SPITCH * j:RB_SPITCH * j + RB_SUB, :] = h2[:, LANES * j:LANES * (j + 1)]

        for t in range(RB_SUB):
            slab = stage_ref[pl.ds(t, 8, stride=RB_SPITCH), :]
            for a in range(2):
                buf_ref[pl.ds(offs_ref[0, a, t], 8, stride=RB_PITCH), :] = slab

    @pl.when(s == RB_NSUB)
    def _experts():
        def copies(e, slot):
            return (pltpu.make_async_copy(wg_hbm.at[e], wg_buf.at[slot], sem.at[slot, 0]),
                    pltpu.make_async_copy(wu_hbm.at[e], wu_buf.at[slot], sem.at[slot, 1]),
                    pltpu.make_async_copy(wd_hbm.at[e], wd_buf.at[slot], sem.at[slot, 2]))

        def run_expert(e, slot):
            lo_row = start_ref[e]
            hi_row = lo_row + cnt_ref[b, e]

            c_lo = lax.shift_right_logical(lo_row, RB_CHUNK_BITS)
            c_hi = lax.shift_right_logical(hi_row + (RB_CHUNK - 1), RB_CHUNK_BITS)
            row = lax.broadcasted_iota(jnp.int32, (RB_CHUNK, 1), 0)

            def load(c):
                base = pl.multiple_of(c * (8 * RB_PITCH), 8)
                return [buf_ref[pl.ds(base + RB_PITCH * j, RB_CHUNK), :] for j in range(8)]

            def store(c, tiles, y):
                base = pl.multiple_of(c * (8 * RB_PITCH), 8)
                mine = (row >= lo_row - c * RB_CHUNK) & (row < hi_row - c * RB_CHUNK)
                for j in range(8):
                    buf_ref[pl.ds(base + RB_PITCH * j, RB_CHUNK), :] = jnp.where(
                        mine, y[:, LANES * j:LANES * (j + 1)], tiles[j])

            def group(i, carry):
                cs = [c_lo + RB_GROUP * i]
                for k in range(1, RB_GROUP):
                    cs.append(jnp.where(cs[0] + k < c_hi, cs[0] + k, RB_NCHUNK + k))
                tiles = [load(c) for c in cs]
                x = jnp.concatenate([jnp.concatenate(t, axis=-1) for t in tiles], axis=0).astype(BF16)
                a = jnp.dot(x, wg_buf[slot], preferred_element_type=F32)
                u = jnp.dot(x, wu_buf[slot], preferred_element_type=F32)
                z = (_silu(a) * u).astype(BF16)
                y = jnp.dot(z, wd_buf[slot], preferred_element_type=F32)
                for k, c in enumerate(cs):
                    store(c, tiles[k], y[RB_CHUNK * k:RB_CHUNK * (k + 1)])
                return carry

            lax.fori_loop(0, lax.div(c_hi - c_lo + (RB_GROUP - 1), RB_GROUP), group, 0)

        for e in range(RB_WSLOTS - 1):
            for c in copies(e, e):
                c.start()

        def ring_body(i, carry):
            for k in range(RB_WSLOTS):
                e = RB_WSLOTS * i + k
                ahead = e + RB_WSLOTS - 1

                @pl.when(ahead < N_EXPERTS)
                def _():
                    for c in copies(ahead, (k + RB_WSLOTS - 1) % RB_WSLOTS):
                        c.start()
                for c in copies(e, k):
                    c.wait()
                run_expert(e, k)
            return carry
        lax.fori_loop(0, N_EXPERTS // RB_WSLOTS, ring_body, 0)

    @pl.when(s > RB_NSUB)
    def _combine():
        for t in range(RB_SUB):
            acc = None
            for a in range(2):
                term = wts_ref[0, a, t] * buf_ref[pl.ds(offs_ref[0, a, t], 8, stride=RB_PITCH), :]
                acc = term if acc is None else acc + term
            stage_ref[pl.ds(t, 8, stride=RB_SPITCH), :] = acc
        gt2 = mod_ref[0, 5:6, :]
        f = jnp.concatenate([stage_ref[RB_SPITCH * j:RB_SPITCH * j + RB_SUB, :] for j in range(8)], axis=-1)
        o_ref[...] = _layer_norm(DEEPNORM_ALPHA * x1_ref[...] + (1.0 + gt2) * f, ln2g_ref[...], ln2b_ref[...])


def _rb_retile(a):
    tiles, two, t = a.shape
    return a.reshape(tiles, two, t // RB_SUB, RB_SUB).transpose(0, 2, 1, 3).reshape(-1, two, RB_SUB)


def _rb_offsets(cnt, e12, rank12):
    start = jnp.cumsum(cnt, axis=-1) - cnt
    start_t = jnp.repeat(start, e12.shape[0] // cnt.shape[0], axis=0)[:, None, None, :]
    hit = e12[..., None] == jnp.arange(N_EXPERTS, dtype=jnp.int32)
    p = jnp.sum(jnp.where(hit, start_t, 0), axis=-1) + rank12
    return lax.shift_right_logical(p, RB_CHUNK_BITS) * (8 * RB_PITCH) + (p & (RB_CHUNK - 1))


def _rb_moe(x1, modp, cnt, offs, wts, wg_bf16, wu_bf16, wd_bf16, ln2g, ln2b):
    n, d = x1.shape
    bsz = n // SEQ
    nsteps = 2 * RB_NSUB + 1

    def sub_index(s):
        return jnp.where(s < RB_NSUB, s, jnp.where(s == RB_NSUB, RB_NSUB - 1, s - RB_NSUB - 1))

    def tile_map(b, s, cnt_r):
        return (b * RB_NSUB + sub_index(s), 0)

    def tile_map3(b, s, cnt_r):
        return (b * RB_NSUB + sub_index(s), 0, 0)

    def out_map(b, s, cnt_r):
        return (b * RB_NSUB + jnp.maximum(s - RB_NSUB - 1, 0), 0)

    const = lambda shp: pl.BlockSpec(shp, lambda b, s, cnt_r: tuple(0 for _ in shp))
    anyspec = pl.BlockSpec(memory_space=pl.ANY)
    grid_spec = pltpu.PrefetchScalarGridSpec(
        num_scalar_prefetch=1,
        grid=(bsz, nsteps),
        in_specs=[pl.BlockSpec((RB_SUB, d), tile_map),
                  pl.BlockSpec((1, 6, d), lambda b, s, cnt_r: (b, 0, 0)),
                  pl.BlockSpec((1, 2, RB_SUB), tile_map3, memory_space=pltpu.SMEM),
                  pl.BlockSpec((1, 2, RB_SUB), tile_map3, memory_space=pltpu.SMEM),
                  anyspec, anyspec, anyspec,
                  const((1, d)), const((1, d))],
        out_specs=pl.BlockSpec((RB_SUB, d), out_map),
        scratch_shapes=[pltpu.VMEM(((RB_NCHUNK + RB_GROUP) * 8 * RB_PITCH, LANES), F32),
                        pltpu.VMEM((8 * RB_SPITCH, LANES), F32),
                        pltpu.VMEM((RB_WSLOTS, d, D_EXPERT), BF16),
                        pltpu.VMEM((RB_WSLOTS, d, D_EXPERT), BF16),
                        pltpu.VMEM((RB_WSLOTS, D_EXPERT, d), BF16),
                        pltpu.SMEM((N_EXPERTS,), jnp.int32),
                        pltpu.SemaphoreType.DMA((RB_WSLOTS, 3))])
    return pl.pallas_call(
        _rb_kernel,
        grid_spec=grid_spec,
        out_shape=jax.ShapeDtypeStruct((n, d), F32),
        compiler_params=_cparams(("arbitrary", "arbitrary")),
        name="moe_routed_ln",
    )(cnt, x1, modp, offs, wts, wg_bf16, wu_bf16, wd_bf16, ln2g, ln2b)


def _sample_in_kernel(x_ref, sh1_ref, sc1_ref, win_ref, ctx_ref, h0_ref, convw_ref, convb_ref,
                      wlo_ref, whi_ref, bgate_ref, lam_ref,
                      ylru_ref, q_ref, k_ref, v_ref, cstate_ref, hnew_ref):
    h = x_ref[...] * (1.0 + sc1_ref[...]) + sh1_ref[...]
    z = _dot(h, win_ref[...], True)
    xb = z[:, :LRU_WIDTH]
    gate = z[:, LRU_WIDTH:2 * LRU_WIDTH]
    c0 = ctx_ref[:, 0, :]
    c1 = ctx_ref[:, 1, :]
    c2 = ctx_ref[:, 2, :]
    xc = (convb_ref[...] + convw_ref[0:1, :] * c0 + convw_ref[1:2, :] * c1
          + convw_ref[2:3, :] * c2 + convw_ref[3:4, :] * xb)
    cstate_ref[:, 0, :] = c1
    cstate_ref[:, 1, :] = c2
    cstate_ref[:, 2, :] = xb
    sp = _softplus(-lam_ref[...])
    a, bterm = _lru_gates(xc, wlo_ref[...], whi_ref[...], bgate_ref[...], sp, True)
    hn = a * h0_ref[...] + bterm
    hnew_ref[...] = hn
    ylru_ref[...] = hn * _gelu_tanh(gate)
    low = lax.broadcasted_iota(jnp.int32, (DEC_BATCH, LANES), 1) < HEAD_DIM
    for c in range(4):
        qc = z[:, 2 * LRU_WIDTH + LANES * c:2 * LRU_WIDTH + LANES * (c + 1)]
        q_ref[pl.ds(c, DEC_BATCH, stride=N_HEADS), :] = jnp.where(low, qc, 0.0)
        q_ref[pl.ds(c + 4, DEC_BATCH, stride=N_HEADS), :] = jnp.where(low, 0.0, qc)
    k_ref[...] = z[:, 2 * LRU_WIDTH + ATTN_WIDTH:2 * LRU_WIDTH + ATTN_WIDTH + KV_WIDTH]
    v_ref[...] = z[:, 2 * LRU_WIDTH + ATTN_WIDTH + KV_WIDTH:]


def _sample_in(x, sh1, sc1, w_in_p, ctx, h0, conv_w, conv_b, wlo, whi, bgate, lam):
    n = DEC_BATCH
    outs = [jax.ShapeDtypeStruct((n, LRU_WIDTH), F32),
            jax.ShapeDtypeStruct((n * N_HEADS, LANES), F32),
            jax.ShapeDtypeStruct((n, KV_WIDTH), F32),
            jax.ShapeDtypeStruct((n, KV_WIDTH), F32),
            jax.ShapeDtypeStruct((n, CONV_WIDTH - 1, LRU_WIDTH), F32),
            jax.ShapeDtypeStruct((n, LRU_WIDTH), F32)]
    return pl.pallas_call(
        _sample_in_kernel,
        out_shape=outs,
        compiler_params=pltpu.CompilerParams(vmem_limit_bytes=VMEM_LIMIT),
        name="sample_inproj_rglru",
    )(x, sh1, sc1, w_in_p, ctx, h0, conv_w, conv_b, wlo, whi, bgate, lam)


def _sample_attn_kernel(q_ref, kn_ref, vn_ref, ck_ref, cv_ref, sink_ref, y_ref, nk_ref, nv_ref, *, bb):
    rows = lax.broadcasted_iota(jnp.int32, (WINDOW, KV_WIDTH), 0)
    nh = N_HEADS
    q_all = q_ref[...].reshape(bb * nh, LANES)
    kcat = ck_ref[...].reshape(bb * WINDOW, KV_WIDTH)
    vcat = cv_ref[...].reshape(bb * WINDOW, KV_WIDTH)
    kn_rep = jnp.broadcast_to(kn_ref[...][:, None, :], (bb, nh, KV_WIDTH)).reshape(bb * nh, KV_WIDTH)
    vn_rep = jnp.broadcast_to(vn_ref[...][:, None, :], (bb, nh, KV_WIDTH)).reshape(bb * nh, KV_WIDTH)
    sink = jnp.concatenate([sink_ref[...]] * bb, axis=0)
    s_full = _dot_nt(q_all, kcat, True)
    s = jnp.concatenate([s_full[nh * b:nh * (b + 1), WINDOW * b:WINDOW * (b + 1)] for b in range(bb)],
                        axis=0) * ATTN_SCALE
    s_self = jnp.sum(q_all * kn_rep, axis=-1, keepdims=True) * ATTN_SCALE
    m = jnp.maximum(jnp.maximum(jnp.max(s, axis=-1, keepdims=True), s_self), sink)
    e = jnp.exp(s - m)
    e_self = jnp.exp(s_self - m)
    den = jnp.sum(e, axis=-1, keepdims=True) + e_self + jnp.exp(sink - m)
    inv = 1.0 / den
    p = e * inv
    zero = jnp.zeros((nh, WINDOW), F32)
    p_wide = jnp.concatenate(
        [jnp.concatenate([p[nh * b:nh * (b + 1)] if c == b else zero for c in range(bb)], axis=-1)
         for b in range(bb)], axis=0)
    o = _dot(p_wide, vcat, True) + (e_self * inv) * vn_rep
    y_ref[...] = o.reshape(bb, nh, LANES)
    for b in range(bb):
        nk_ref[b] = jnp.where(rows == WINDOW - 1, kn_ref[b:b + 1, :], pltpu.roll(ck_ref[b], WINDOW - 1, axis=0))
        nv_ref[b] = jnp.where(rows == WINDOW - 1, vn_ref[b:b + 1, :], pltpu.roll(cv_ref[b], WINDOW - 1, axis=0))


def _sample_attn(q3, kn, vn, cache_k, cache_v, sinks, bb=16):
    n = DEC_BATCH
    kern = functools.partial(_sample_attn_kernel, bb=bb)
    return pl.pallas_call(
        kern,
        grid=(n // bb,),
        in_specs=[pl.BlockSpec((bb, N_HEADS, LANES), lambda i: (i, 0, 0)),
                  pl.BlockSpec((bb, KV_WIDTH), lambda i: (i, 0)),
                  pl.BlockSpec((bb, KV_WIDTH), lambda i: (i, 0)),
                  pl.BlockSpec((bb, WINDOW, KV_WIDTH), lambda i: (i, 0, 0)),
                  pl.BlockSpec((bb, WINDOW, KV_WIDTH), lambda i: (i, 0, 0)),
                  pl.BlockSpec((N_HEADS, 1), lambda i: (0, 0))],
        out_specs=[pl.BlockSpec((bb, N_HEADS, LANES), lambda i: (i, 0, 0)),
                   pl.BlockSpec((bb, WINDOW, KV_WIDTH), lambda i: (i, 0, 0)),
                   pl.BlockSpec((bb, WINDOW, KV_WIDTH), lambda i: (i, 0, 0))],
        out_shape=[jax.ShapeDtypeStruct((n, N_HEADS, LANES), F32),
                   jax.ShapeDtypeStruct((n, WINDOW, KV_WIDTH), F32),
                   jax.ShapeDtypeStruct((n, WINDOW, KV_WIDTH), F32)],
        compiler_params=_cparams(("arbitrary",)),
        name="sample_cache_attention",
    )(q3, kn, vn, cache_k, cache_v, sinks.reshape(N_HEADS, 1))


def _sample_out_kernel(x_ref, ylru_ref, yatt_ref, sh2_ref, sc2_ref, gt1_ref, glru_ref, gattn_ref, wout_ref,
                       ln1g_ref, ln1b_ref, wr_ref, br_ref, x1_ref, comb_ref):
    low = lax.broadcasted_iota(jnp.int32, (DEC_BATCH, LANES), 1) < HEAD_DIM
    yatt = jnp.concatenate(
        [jnp.where(low, yatt_ref[pl.ds(c, DEC_BATCH, stride=N_HEADS), :],
                   yatt_ref[pl.ds(c + 4, DEC_BATCH, stride=N_HEADS), :]) for c in range(4)], axis=-1)
    x1, comb = _outproj_body(x_ref[...], ylru_ref[...], yatt, sh2_ref[...], sc2_ref[...], gt1_ref[...],
                             glru_ref[...], gattn_ref[...], wout_ref[...], ln1g_ref[...], ln1b_ref[...],
                             wr_ref[...], br_ref[...], True)
    x1_ref[...] = x1
    comb_ref[...] = comb


def _sample_out(x, ylru, yatt2d, sh2, sc2, gt1, glru, gattn, wout_p, ln1g, ln1b, wr, br):
    n = DEC_BATCH
    return pl.pallas_call(
        _sample_out_kernel,
        out_shape=[jax.ShapeDtypeStruct((n, D_MODEL), F32), jax.ShapeDtypeStruct((n, ROUTE_LANES), F32)],
        compiler_params=pltpu.CompilerParams(vmem_limit_bytes=VMEM_LIMIT),
        name="sample_outproj_ln_route",
    )(x, ylru, yatt2d, sh2, sc2, gt1, glru, gattn, wout_p, ln1g, ln1b, wr, br)


def _block_diag_halves(w_a, w_x):
    def bd(w4):
        eye = jnp.eye(4, dtype=w4.dtype)
        return (w4[:, :, None, :] * eye[:, None, :, None]).reshape(256, 256)
    lo = jnp.concatenate([bd(w_a[:4]), bd(w_x[:4])], axis=1)
    hi = jnp.concatenate([bd(w_a[4:]), bd(w_x[4:])], axis=1)
    return lo, hi


def kernel(x_prompt, x_sample, c_prompt, c_sample, state_conv, state_h, cache_k, cache_v, w_ada, b_ada, w_in,
           conv_w, conv_b, w_rg_a, b_rg_a, w_rg_x, b_rg_x, lru_lambda, sinks, g_lru, g_attn, w_out, ln1_g, ln1_b,
           w_group, b_group, w_router, b_router, w_gate, w_up, w_down, ln2_g, ln2_b):
    d = D_MODEL
    perm = jnp.asarray(HEAD_PERM)
    w_in0 = w_in[0]
    q0 = 2 * LRU_WIDTH
    w_in_p = jnp.concatenate([w_in0[:, :q0], w_in0[:, q0:q0 + ATTN_WIDTH][:, perm], w_in0[:, q0 + ATTN_WIDTH:]],
                             axis=1)
    w_out0 = w_out[0]
    w_out_p = jnp.concatenate([w_out0[:LRU_WIDTH], w_out0[LRU_WIDTH:][perm]], axis=0)
    g_attn_p = g_attn[0][perm].reshape(1, -1)
    glru = g_lru[0].reshape(1, -1)
    wlo, whi = _block_diag_halves(w_rg_a[0], w_rg_x[0])
    bgate = jnp.concatenate([b_rg_a[0].reshape(-1), b_rg_x[0].reshape(-1)]).reshape(1, -1)
    lam = lru_lambda[0].reshape(1, -1)
    convw = conv_w[0]
    convb = conv_b[0].reshape(1, -1)
    ln1g, ln1b = ln1_g[0].reshape(1, -1), ln1_b[0].reshape(1, -1)
    ln2g, ln2b = ln2_g[0].reshape(1, -1), ln2_b[0].reshape(1, -1)
    wr = jnp.concatenate([jnp.transpose(w_router[0], (1, 0, 2)).reshape(d, N_EXPERTS), w_group[0],
                          jnp.zeros((d, ROUTE_LANES - N_EXPERTS - N_GROUPS), F32)], axis=1)
    br = jnp.concatenate([b_router[0].reshape(-1), b_group[0],
                          jnp.zeros((ROUTE_LANES - N_EXPERTS - N_GROUPS,), F32)]).reshape(1, -1)
    sink_p = sinks[0]

    c_all = jnp.concatenate([c_prompt, jnp.zeros((8 - BATCH, d), F32), c_sample], axis=0)
    mod = _ada(c_all, w_ada[0], b_ada[0])
    modp = mod[:BATCH].reshape(BATCH, 6, d)
    mods = mod[8:]
    sh1_s, sc1_s, gt1_s, sh2_s, sc2_s, gt2_s = (mods[:, k * d:(k + 1) * d] for k in range(6))

    zin = _inproj(x_prompt, modp, w_in_p.astype(BF16))
    ylru, cstate8, hlast8 = _lru(zin, convw, convb, wlo.astype(BF16), whi.astype(BF16), bgate, lam)
    yatt = _attn(zin, sink_p)
    n_p = BATCH * SEQ
    x1_p, info, cntf = _outproj_prompt(x_prompt.reshape(n_p, d), ylru.reshape(n_p, LRU_WIDTH),
                                       yatt.reshape(n_p, ATTN_WIDTH), modp, glru, g_attn_p, w_out_p.astype(BF16),
                                       ln1g, ln1b, wr, br, tm=OUTPROJ_TILE)
    cnt = cntf[:, 0, :N_EXPERTS].astype(jnp.int32)
    offs = _rb_retile(_rb_offsets(cnt, info[:, 0:2].astype(jnp.int32), info[:, 4:6].astype(jnp.int32)))
    wg_b, wu_b, wd_b = w_gate[0].astype(BF16), w_up[0].astype(BF16), w_down[0].astype(BF16)
    y_p = _rb_moe(x1_p, modp, cnt, offs, _rb_retile(info[:, 2:4]), wg_b, wu_b, wd_b, ln2g, ln2b)

    ylru_s, q2d, kn, vn, cstate_s, hnew_s = _sample_in(
        x_sample.reshape(DEC_BATCH, d), sh1_s, sc1_s, w_in_p, state_conv[0], state_h[0],
        convw, convb, wlo, whi, bgate, lam)
    yatt3, newk, newv = _sample_attn(q2d.reshape(DEC_BATCH, N_HEADS, LANES), kn, vn,
                                     cache_k[0].reshape(DEC_BATCH, WINDOW, KV_WIDTH),
                                     cache_v[0].reshape(DEC_BATCH, WINDOW, KV_WIDTH), sink_p)
    x1_s, comb_s = _sample_out(x_sample.reshape(DEC_BATCH, d), ylru_s, yatt3.reshape(DEC_BATCH * N_HEADS, LANES),
                               sh2_s, sc2_s, gt1_s, glru, g_attn_p, w_out_p, ln1g, ln1b, wr, br)
    y_s = _moe_dense(x1_s, comb_s, sh2_s, sc2_s, gt2_s, 0, wg_b, wu_b, wd_b, ln2g, ln2b, DEC_BATCH)

    kq = 2 * LRU_WIDTH + ATTN_WIDTH
    return (y_p.reshape(BATCH, SEQ, d),
            y_s.reshape(DEC_BATCH, 1, d),
            cstate8[:, 5:8][None],
            hlast8[:, 7][None],
            zin[:, SEQ - WINDOW:, kq:kq + KV_WIDTH].reshape(1, BATCH, WINDOW, N_KV_HEADS, HEAD_DIM),
            zin[:, SEQ - WINDOW:, kq + KV_WIDTH:].reshape(1, BATCH, WINDOW, N_KV_HEADS, HEAD_DIM),
            cstate_s[None],
            hnew_s[None],
            newk.reshape(1, DEC_BATCH, WINDOW, N_KV_HEADS, HEAD_DIM),
            newv.reshape(1, DEC_BATCH, WINDOW, N_KV_HEADS, HEAD_DIM))
```

```python
import functools

import jax
import jax.numpy as jnp
import numpy as np
from jax import lax
from jax.experimental import pallas as pl
from jax.experimental.pallas import tpu as pltpu

F32 = jnp.float32
BF16 = jnp.bfloat16
HIGHEST = lax.Precision.HIGHEST

D_MODEL = 1024
BATCH = 4
SEQ = 4096
DEC_BATCH = 128
LRU_WIDTH = 512
LRU_BLOCKS = 8
LRU_BLOCK = 64
CONV_WIDTH = 4
LRU_C = 8.0
N_HEADS = 8
N_KV_HEADS = 2
HEAD_DIM = 64
ATTN_WIDTH = 512
KV_WIDTH = 128
WINDOW = 128
IN_WIDTH = 2 * LRU_WIDTH + ATTN_WIDTH + 2 * KV_WIDTH
N_GROUPS = 4
EXPERTS_PER_GROUP = 8
N_EXPERTS = 32
D_EXPERT = 256
DEEPNORM_ALPHA = 2.0 ** 0.25
LN_EPS = 1e-5
RMS_EPS = 1e-6
ATTN_SCALE = HEAD_DIM ** -0.5

LANES = 128
ROUTE_LANES = 128
ROUTE_INFO = 40
VMEM_LIMIT = 56 * 1024 * 1024

HEAD_PERM = np.concatenate(
    [np.concatenate([np.arange(64 * c, 64 * c + 64), np.arange(64 * (c + 4), 64 * (c + 4) + 64)])
     for c in range(4)])


def _cparams(sem):
    return pltpu.CompilerParams(dimension_semantics=sem, vmem_limit_bytes=VMEM_LIMIT)


def _dot(a, b, exact):
    if exact:
        return jnp.dot(a, b, precision=HIGHEST, preferred_element_type=F32)
    return jnp.dot(a.astype(BF16), b.astype(BF16), preferred_element_type=F32)


def _dot_nt(a, b, exact):
    dn = (((1,), (1,)), ((), ()))
    if exact:
        return lax.dot_general(a, b, dn, precision=HIGHEST, preferred_element_type=F32)
    return lax.dot_general(a.astype(BF16), b.astype(BF16), dn, preferred_element_type=F32)


def _sigmoid(x):
    return 1.0 / (1.0 + jnp.exp(-x))


def _silu(x):
    return x * _sigmoid(x)


def _gelu_tanh(x):
    return 0.5 * x * (1.0 + jnp.tanh(np.sqrt(2.0 / np.pi).astype(np.float32) * (x + 0.044715 * (x * x * x))))


def _softplus(x):
    return jnp.maximum(x, 0.0) + jnp.log1p(jnp.exp(-jnp.abs(x)))


def _layer_norm(x, g, b):
    mu = jnp.mean(x, axis=-1, keepdims=True)
    xc = x - mu
    var = jnp.mean(xc * xc, axis=-1, keepdims=True)
    return xc * lax.rsqrt(var + LN_EPS) * g + b


def _rms_norm(x, g):
    return x * lax.rsqrt(jnp.mean(x * x, axis=-1, keepdims=True) + RMS_EPS) * g


def _ada_kernel(c_ref, w_ref, b_ref, o_ref):
    o_ref[...] = _dot(_silu(c_ref[...]), w_ref[...], True) + b_ref[...]


def _ada(c_all, w_ada, b_ada):
    rows = c_all.shape[0]
    bn = 1024
    return pl.pallas_call(
        _ada_kernel,
        grid=(6 * D_MODEL // bn,),
        in_specs=[pl.BlockSpec((rows, D_MODEL), lambda j: (0, 0)),
                  pl.BlockSpec((D_MODEL, bn), lambda j: (0, j)),
                  pl.BlockSpec((1, bn), lambda j: (0, j))],
        out_specs=pl.BlockSpec((rows, bn), lambda j: (0, j)),
        out_shape=jax.ShapeDtypeStruct((rows, 6 * D_MODEL), F32),
        compiler_params=_cparams(("arbitrary",)),
        name="ada_modulation",
    )(c_all, w_ada, b_ada.reshape(1, -1))


QKV_WIDTH = ATTN_WIDTH + 2 * KV_WIDTH


def _inproj_kernel(x_ref, mod_ref, w_ref, lru_ref, qkv_ref, kvlast_ref):
    sh1 = mod_ref[0, 0:1, :]
    sc1 = mod_ref[0, 1:2, :]
    h = x_ref[0] * (1.0 + sc1) + sh1
    z = _dot(h, w_ref[...], False)
    lru_ref[0] = z[:, :2 * LRU_WIDTH]
    qkv_ref[0] = z[:, 2 * LRU_WIDTH:].astype(BF16)
    kvlast_ref[0] = z[z.shape[0] - WINDOW:, 2 * LRU_WIDTH + ATTN_WIDTH:]


def _inproj(x, modp, w_in_bf16, tm=512):
    b, t, d = x.shape
    return pl.pallas_call(
        _inproj_kernel,
        grid=(b, t // tm),
        in_specs=[pl.BlockSpec((1, tm, d), lambda i, j: (i, j, 0)),
                  pl.BlockSpec((1, 6, d), lambda i, j: (i, 0, 0)),
                  pl.BlockSpec((d, IN_WIDTH), lambda i, j: (0, 0))],
        out_specs=[pl.BlockSpec((1, tm, 2 * LRU_WIDTH), lambda i, j: (i, j, 0)),
                   pl.BlockSpec((1, tm, QKV_WIDTH), lambda i, j: (i, j, 0)),
                   pl.BlockSpec((1, WINDOW, 2 * KV_WIDTH), lambda i, j: (i, 0, 0))],
        out_shape=[jax.ShapeDtypeStruct((b, t, 2 * LRU_WIDTH), F32),
                   jax.ShapeDtypeStruct((b, t, QKV_WIDTH), BF16),
                   jax.ShapeDtypeStruct((b, WINDOW, 2 * KV_WIDTH), F32)],
        compiler_params=_cparams(("arbitrary", "arbitrary")),
        name="prompt_inproj",
    )(x, modp, w_in_bf16)


def _lru_gates(xc, wlo, whi, bgate, sp_neg_lam, exact):
    g_lo = _dot(xc[:, :256], wlo, exact)
    g_hi = _dot(xc[:, 256:], whi, exact)
    ga = jnp.concatenate([g_lo[:, :256], g_hi[:, :256]], axis=-1) + bgate[:, :LRU_WIDTH]
    gx = jnp.concatenate([g_lo[:, 256:], g_hi[:, 256:]], axis=-1) + bgate[:, LRU_WIDTH:]
    r = _sigmoid(ga)
    i = _sigmoid(gx)
    log_a = -LRU_C * r * sp_neg_lam
    a = jnp.exp(log_a)
    one_minus_a2 = -jnp.tanh(log_a) * (a * a + 1.0) if exact else 1.0 - a * a
    bterm = jnp.sqrt(one_minus_a2) * (i * xc)
    return a, bterm


def _lru_kernel(z_ref, convw_ref, convb_ref, wlo_ref, whi_ref, bgate_ref, lam_ref,
                y_ref, cstate_ref, hlast_ref, tail_ref, carry_ref, *, tl):
    j = pl.program_id(1)

    @pl.when(j == 0)
    def _():
        tail_ref[...] = jnp.zeros_like(tail_ref)
        carry_ref[...] = jnp.zeros_like(carry_ref)

    xb = z_ref[0, :, :LRU_WIDTH]
    gate = z_ref[0, :, LRU_WIDTH:]
    rows = lax.broadcasted_iota(jnp.int32, (tl, LRU_WIDTH), 0)

    xc = convb_ref[...] + convw_ref[3:4, :] * xb
    rows8 = lax.broadcasted_iota(jnp.int32, (8, LRU_WIDTH), 0)
    tail = tail_ref[...]
    for back in (1, 2, 3):
        rolled = pltpu.roll(xb, back, axis=0)
        top = jnp.where(rows8 >= back, rolled[:8], pltpu.roll(tail, back, axis=0))
        shifted = jnp.concatenate([top, rolled[8:]], axis=0)
        xc = xc + convw_ref[3 - back:4 - back, :] * shifted
    tail_ref[...] = xb[tl - 8:, :]
    cstate_ref[0] = xb[tl - 8:, :]

    sp = _softplus(-lam_ref[...])
    a, bterm = _lru_gates(xc, wlo_ref[...], whi_ref[...], bgate_ref[...], sp, False)

    s = 1
    while s < tl:
        if s < 8:
            a_sh = jnp.where(rows >= s, pltpu.roll(a, s, axis=0), 1.0)
            b_sh = jnp.where(rows >= s, pltpu.roll(bterm, s, axis=0), 0.0)
        else:
            a_sh = jnp.concatenate([jnp.ones((s, LRU_WIDTH), F32), a[:tl - s]], axis=0)
            b_sh = jnp.concatenate([jnp.zeros((s, LRU_WIDTH), F32), bterm[:tl - s]], axis=0)
        bterm = a * b_sh + bterm
        a = a * a_sh
        s *= 2
    h = a * carry_ref[7:8, :] + bterm
    carry_ref[...] = h[tl - 8:, :]
    hlast_ref[0] = h[tl - 8:, :]
    y_ref[0] = h * _gelu_tanh(gate)


def _lru(zin, conv_w, conv_b, wlo, whi, bgate, lam, tl=512):
    b, t, _ = zin.shape
    kern = functools.partial(_lru_kernel, tl=tl)
    full = lambda shp: pl.BlockSpec(shp, lambda i, j: tuple(0 for _ in shp))
    return pl.pallas_call(
        kern,
        grid=(b, t // tl),
        in_specs=[pl.BlockSpec((1, tl, 2 * LRU_WIDTH), lambda i, j: (i, j, 0)),
                  full((CONV_WIDTH, LRU_WIDTH)), full((1, LRU_WIDTH)),
                  full((256, 512)), full((256, 512)), full((1, 2 * LRU_WIDTH)), full((1, LRU_WIDTH))],
        out_specs=[pl.BlockSpec((1, tl, LRU_WIDTH), lambda i, j: (i, j, 0)),
                   pl.BlockSpec((1, 8, LRU_WIDTH), lambda i, j: (i, 0, 0)),
                   pl.BlockSpec((1, 8, LRU_WIDTH), lambda i, j: (i, 0, 0))],
        out_shape=[jax.ShapeDtypeStruct((b, t, LRU_WIDTH), F32),
                   jax.ShapeDtypeStruct((b, 8, LRU_WIDTH), F32),
                   jax.ShapeDtypeStruct((b, 8, LRU_WIDTH), F32)],
        scratch_shapes=[pltpu.VMEM((8, LRU_WIDTH), F32), pltpu.VMEM((8, LRU_WIDTH), F32)],
        compiler_params=_cparams(("arbitrary", "arbitrary")),
        name="prompt_rglru",
    )(zin, conv_w, conv_b, wlo, whi, bgate, lam)


ATTN_BLOCKS = 16


def _attn_kernel(q_ref, k_ref, v_ref, sink_ref, o_ref, kprev_ref, vprev_ref):
    j = pl.program_id(1)

    @pl.when(j == 0)
    def _():
        kprev_ref[...] = jnp.zeros_like(kprev_ref)
        vprev_ref[...] = jnp.zeros_like(vprev_ref)

    blk = WINDOW
    lane = lax.broadcasted_iota(jnp.int32, (blk, LANES), 1)
    low = lane < HEAD_DIM
    qi = lax.broadcasted_iota(jnp.int32, (blk, 2 * blk), 0)
    sj = lax.broadcasted_iota(jnp.int32, (blk, 2 * blk), 1)
    rel = blk + qi - sj
    in_window = (rel >= 0) & (rel <= WINDOW)
    sink = sink_ref[...].reshape(N_HEADS, blk, 1)
    k_ext = jnp.concatenate([kprev_ref[...], k_ref[0]], axis=0)
    v_ext = jnp.concatenate([vprev_ref[...], v_ref[0]], axis=0)
    v_ext = jnp.concatenate([v_ext, jnp.ones_like(v_ext)], axis=-1)
    for n in range(ATTN_BLOCKS):
        q = q_ref[0, blk * n:blk * (n + 1), :]
        pieces = []
        for half in (0, 1):
            for c in range(4):
                qc = q[:, LANES * c:LANES * (c + 1)]
                pieces.append(jnp.where(low if half == 0 else ~low, qc, 0.0).astype(BF16))
        q8 = jnp.concatenate(pieces, axis=0)
        k_band = k_ext[blk * n:blk * (n + 2)]
        v_band = v_ext[blk * n:blk * (n + 2)]
        s = _dot_nt(q8, k_band, False) * ATTN_SCALE
        s = s.reshape(N_HEADS, blk, 2 * blk)
        valid = in_window & ((sj >= blk) | (j > 0)) if n == 0 else in_window
        s = jnp.where(valid[None], s, -jnp.inf)
        m = jnp.maximum(jnp.max(s, axis=-1, keepdims=True), sink)
        e = jnp.exp(s - m).reshape(N_HEADS * blk, 2 * blk)
        ov = _dot(e, v_band, False)
        den = ov[:, KV_WIDTH:] + jnp.exp(sink - m).reshape(N_HEADS * blk, 1)
        o8 = ov[:, :KV_WIDTH] * (1.0 / den)
        cols = []
        for c in range(4):
            cols.append(jnp.where(low, o8[blk * c:blk * (c + 1)], o8[blk * (c + 4):blk * (c + 5)]))
        o_ref[0, blk * n:blk * (n + 1), :] = jnp.concatenate(cols, axis=-1)
    kprev_ref[...] = k_ref[0, blk * (ATTN_BLOCKS - 1):, :]
    vprev_ref[...] = v_ref[0, blk * (ATTN_BLOCKS - 1):, :]


def _attn(qkv, sinks):
    b, t, _ = qkv.shape
    blk = WINDOW
    tq = blk * ATTN_BLOCKS
    sink_col = jnp.repeat(sinks.astype(F32), blk).reshape(N_HEADS * blk, 1)
    kcol = ATTN_WIDTH // KV_WIDTH
    return pl.pallas_call(
        _attn_kernel,
        grid=(b, t // tq),
        in_specs=[pl.BlockSpec((1, tq, ATTN_WIDTH), lambda i, j: (i, j, 0)),
                  pl.BlockSpec((1, tq, KV_WIDTH), lambda i, j: (i, j, kcol)),
                  pl.BlockSpec((1, tq, KV_WIDTH), lambda i, j: (i, j, kcol + 1)),
                  pl.BlockSpec((N_HEADS * blk, 1), lambda i, j: (0, 0))],
        out_specs=pl.BlockSpec((1, tq, ATTN_WIDTH), lambda i, j: (i, j, 0)),
        out_shape=jax.ShapeDtypeStruct((b, t, ATTN_WIDTH), F32),
        scratch_shapes=[pltpu.VMEM((blk, KV_WIDTH), BF16), pltpu.VMEM((blk, KV_WIDTH), BF16)],
        compiler_params=_cparams(("arbitrary", "arbitrary")),
        name="prompt_window_attention",
    )(qkv, qkv, qkv, sink_col)


def _dot_split3(a, b):
    a_hi = a.astype(BF16)
    b_hi = b.astype(BF16)
    a_lo = (a - a_hi.astype(F32)).astype(BF16)
    b_lo = (b - b_hi.astype(F32)).astype(BF16)
    return (jnp.dot(a_hi, b_hi, preferred_element_type=F32) + jnp.dot(a_hi, b_lo, preferred_element_type=F32)
            + jnp.dot(a_lo, b_hi, preferred_element_type=F32))


def _route(h2, wr, br, exact):
    t = h2.shape[0]
    logits = (_dot(h2, wr, True) if exact else _dot_split3(h2, wr)) + br
    lane = lax.broadcasted_iota(jnp.int32, (t, ROUTE_LANES), 1).astype(F32)
    neg = -jnp.inf
    big = float(ROUTE_LANES)
    is_g = (lane >= N_EXPERTS) & (lane < N_EXPERTS + N_GROUPS)
    lg = jnp.where(is_g, logits, neg)
    mg = jnp.max(lg, axis=-1, keepdims=True)
    eg = jnp.where(is_g, jnp.exp(lg - mg), 0.0)
    pg = eg / jnp.sum(eg, axis=-1, keepdims=True)
    g_val = jnp.max(pg, axis=-1, keepdims=True)
    g_lane = jnp.min(jnp.where((pg == g_val) & is_g, lane, big), axis=-1, keepdims=True)
    g_idx = g_lane - N_EXPERTS
    in_grp = (lane >= g_idx * EXPERTS_PER_GROUP) & (lane < (g_idx + 1.0) * EXPERTS_PER_GROUP)
    le = jnp.where(in_grp, logits, neg)
    me = jnp.max(le, axis=-1, keepdims=True)
    ee = jnp.where(in_grp, jnp.exp(le - me), 0.0)
    pe = ee / jnp.sum(ee, axis=-1, keepdims=True)
    v1 = jnp.max(pe, axis=-1, keepdims=True)
    l1 = jnp.min(jnp.where((pe == v1) & in_grp, lane, big), axis=-1, keepdims=True)
    rest = in_grp & (lane != l1)
    pe2 = jnp.where(rest, pe, -1.0)
    v2 = jnp.max(pe2, axis=-1, keepdims=True)
    l2 = jnp.min(jnp.where((pe2 == v2) & rest, lane, big), axis=-1, keepdims=True)
    tot = v1 + v2
    w1 = g_val * v1 / tot
    w2 = g_val * v2 / tot
    comb = jnp.where(lane == l1, w1, 0.0) + jnp.where(lane == l2, w2, 0.0)
    return (comb + jnp.where(lane == ROUTE_INFO, l1, 0.0) + jnp.where(lane == ROUTE_INFO + 1, l2, 0.0)
            + jnp.where(lane == ROUTE_INFO + 2, w1, 0.0) + jnp.where(lane == ROUTE_INFO + 3, w2, 0.0))


def _outproj_body(x, ylru, yatt, sh2, sc2, gt1, glru, gattn, wout, ln1g, ln1b, wr, br, exact):
    mixin = jnp.concatenate([_rms_norm(ylru, glru), _rms_norm(yatt, gattn)], axis=-1)
    mix = _dot(mixin, wout, exact)
    x1 = _layer_norm(DEEPNORM_ALPHA * x + (1.0 + gt1) * mix, ln1g, ln1b)
    h2 = x1 * (1.0 + sc2) + sh2
    return x1, _route(h2, wr, br, exact)


def _outproj_prompt_kernel(x_ref, ylru_ref, yatt_ref, mod_ref, glru_ref, gattn_ref, wout_ref,
                           ln1g_ref, ln1b_ref, wr_ref, br_ref, x1_ref, info_ref, cnt_ref, tri_ref, carry_ref,
                           *, tm, per_seq):
    i = pl.program_id(0)

    @pl.when(i == 0)
    def _():
        r = lax.broadcasted_iota(jnp.int32, (tm, tm), 0)
        c = lax.broadcasted_iota(jnp.int32, (tm, tm), 1)
        tri_ref[...] = jnp.where(c < r, 1.0, 0.0).astype(BF16)

    @pl.when(i % per_seq == 0)
    def _():
        carry_ref[...] = jnp.zeros_like(carry_ref)

    gt1 = mod_ref[0, 2:3, :]
    sh2 = mod_ref[0, 3:4, :]
    sc2 = mod_ref[0, 4:5, :]
    combs = []
    nsplit = 2
    for h in range(nsplit):
        rows = slice(h * (tm // nsplit), (h + 1) * (tm // nsplit))
        x1_h, comb_h = _outproj_body(x_ref[rows, :], ylru_ref[rows, :], yatt_ref[rows, :], sh2, sc2, gt1,
                                     glru_ref[...], gattn_ref[...], wout_ref[...], ln1g_ref[...], ln1b_ref[...],
                                     wr_ref[...], br_ref[...], False)
        x1_ref[rows, :] = x1_h
        combs.append(comb_h)
    comb = jnp.concatenate(combs, axis=0)
    lane = lax.broadcasted_iota(jnp.int32, (tm, ROUTE_LANES), 1).astype(F32)
    l1 = jnp.sum(jnp.where(lane == ROUTE_INFO, comb, 0.0), axis=-1, keepdims=True)
    l2 = jnp.sum(jnp.where(lane == ROUTE_INFO + 1, comb, 0.0), axis=-1, keepdims=True)
    o1 = lane == l1
    o2 = lane == l2
    onehot = jnp.where(o1 | o2, 1.0, 0.0)
    before = jnp.dot(tri_ref[...], onehot.astype(BF16), preferred_element_type=F32) + carry_ref[0:1, :]
    rank1 = jnp.sum(jnp.where(o1, before, 0.0), axis=-1, keepdims=True)
    rank2 = jnp.sum(jnp.where(o2, before, 0.0), axis=-1, keepdims=True)
    total = carry_ref[0:1, :] + jnp.sum(onehot, axis=0, keepdims=True)
    carry_ref[...] = jnp.broadcast_to(total, carry_ref.shape)
    cnt_ref[0] = jnp.broadcast_to(total, (8, ROUTE_LANES))
    info = (comb + jnp.where(lane == ROUTE_INFO + 4, rank1, 0.0) + jnp.where(lane == ROUTE_INFO + 5, rank2, 0.0))
    info_ref[0] = jnp.transpose(info)[ROUTE_INFO:ROUTE_INFO + 8, :]


OUTPROJ_TILE = 1024


def _outproj_prompt(x2d, ylru2d, yatt2d, modp, glru, gattn, wout_bf16, ln1g, ln1b, wr, br, tm=OUTPROJ_TILE):
    n, d = x2d.shape
    per_seq = SEQ // tm
    full = lambda shp: pl.BlockSpec(shp, lambda i: tuple(0 for _ in shp))
    kern = functools.partial(_outproj_prompt_kernel, tm=tm, per_seq=per_seq)
    return pl.pallas_call(
        kern,
        grid=(n // tm,),
        in_specs=[pl.BlockSpec((tm, d), lambda i: (i, 0)),
                  pl.BlockSpec((tm, LRU_WIDTH), lambda i: (i, 0)),
                  pl.BlockSpec((tm, ATTN_WIDTH), lambda i: (i, 0)),
                  pl.BlockSpec((1, 6, d), lambda i: (i // per_seq, 0, 0)),
                  full((1, LRU_WIDTH)), full((1, ATTN_WIDTH)), full((d, d)),
                  full((1, d)), full((1, d)), full((d, ROUTE_LANES)), full((1, ROUTE_LANES))],
        out_specs=[pl.BlockSpec((tm, d), lambda i: (i, 0)),
                   pl.BlockSpec((1, 8, tm), lambda i: (i, 0, 0)),
                   pl.BlockSpec((1, 8, ROUTE_LANES), lambda i: (i // per_seq, 0, 0))],
        out_shape=[jax.ShapeDtypeStruct((n, d), F32),
                   jax.ShapeDtypeStruct((n // tm, 8, tm), F32),
                   jax.ShapeDtypeStruct((n // SEQ, 8, ROUTE_LANES), F32)],
        scratch_shapes=[pltpu.VMEM((tm, tm), BF16), pltpu.VMEM((8, ROUTE_LANES), F32)],
        compiler_params=_cparams(("arbitrary",)),
        name="prompt_outproj_ln_route",
    )(x2d, ylru2d, yatt2d, modp, glru, gattn, wout_bf16, ln1g, ln1b, wr, br)


DENSE_EXPERTS_PER_STEP = 4


def _moe_kernel(x1_ref, comb_ref, sh2_ref, sc2_ref, gt2_ref, wg_ref, wu_ref, wd_ref, ln2g_ref, ln2b_ref,
                o_ref, h2_ref, acc_ref):
    g = pl.program_id(1)

    @pl.when(g == 0)
    def _():
        h2_ref[...] = (x1_ref[...] * (1.0 + sc2_ref[...]) + sh2_ref[...]).astype(BF16)
        acc_ref[...] = jnp.zeros_like(acc_ref)

    h2 = h2_ref[...]
    comb = comb_ref[...]
    lane = lax.broadcasted_iota(jnp.int32, comb.shape, 1)
    part = None
    for k in range(DENSE_EXPERTS_PER_STEP):
        a = jnp.dot(h2, wg_ref[k].astype(BF16), preferred_element_type=F32)
        u = jnp.dot(h2, wu_ref[k].astype(BF16), preferred_element_type=F32)
        c_e = jnp.sum(jnp.where(lane == g * DENSE_EXPERTS_PER_STEP + k, comb, 0.0), axis=-1, keepdims=True)
        z = _silu(a) * u * c_e
        y = jnp.dot(z.astype(BF16), wd_ref[k].astype(BF16), preferred_element_type=F32)
        part = y if part is None else part + y
    acc_ref[...] += part

    @pl.when(g == N_EXPERTS // DENSE_EXPERTS_PER_STEP - 1)
    def _():
        o_ref[...] = _layer_norm(DEEPNORM_ALPHA * x1_ref[...] + (1.0 + gt2_ref[...]) * acc_ref[...],
                                 ln2g_ref[...], ln2b_ref[...])


def _moe_dense(x1, comb, sh2, sc2, gt2, w_gate, w_up, w_down, ln2g, ln2b, tm):
    n, d = x1.shape
    eg = DENSE_EXPERTS_PER_STEP
    mspec = pl.BlockSpec((tm, d), lambda i, e: (i, 0))
    full = lambda shp: pl.BlockSpec(shp, lambda i, e: tuple(0 for _ in shp))
    return pl.pallas_call(
        _moe_kernel,
        grid=(n // tm, N_EXPERTS // eg),
        in_specs=[pl.BlockSpec((tm, d), lambda i, e: (i, 0)),
                  pl.BlockSpec((tm, ROUTE_LANES), lambda i, e: (i, 0)),
                  mspec, mspec, mspec,
                  pl.BlockSpec((eg, d, D_EXPERT), lambda i, e: (e, 0, 0)),
                  pl.BlockSpec((eg, d, D_EXPERT), lambda i, e: (e, 0, 0)),
                  pl.BlockSpec((eg, D_EXPERT, d), lambda i, e: (e, 0, 0)),
                  full((1, d)), full((1, d))],
        out_specs=pl.BlockSpec((tm, d), lambda i, e: (i, 0)),
        out_shape=jax.ShapeDtypeStruct((n, d), F32),
        scratch_shapes=[pltpu.VMEM((tm, d), BF16), pltpu.VMEM((tm, d), F32)],
        compiler_params=_cparams(("arbitrary", "arbitrary")),
        name="moe_dense_ln",
    )(x1, comb, sh2, sc2, gt2, w_gate, w_up, w_down, ln2g, ln2b)


RB_SUB = 512
RB_NSUB = SEQ // RB_SUB
RB_CHUNK = 128
RB_CHUNK_BITS = 7
RB_NCHUNK = 2 * SEQ // RB_CHUNK
RB_PITCH = RB_CHUNK + 8
RB_SPITCH = RB_SUB + 8
RB_GROUP = 3
RB_WSLOTS = 4


def _rb_kernel(cnt_ref, x1_ref, mod_ref, offs_ref, wts_ref, wg_hbm, wu_hbm, wd_hbm, ln2g_ref, ln2b_ref,
               o_ref, buf_ref, stage_ref, wg_buf, wu_buf, wd_buf, start_ref, sem):
    b = pl.program_id(0)
    s = pl.program_id(1)

    @pl.when(s == 0)
    def _starts():
        def body(e, run):
            start_ref[e] = run
            return run + cnt_ref[b, e]
        lax.fori_loop(0, N_EXPERTS, body, jnp.int32(0))
        buf_ref[RB_NCHUNK * 8 * RB_PITCH:(RB_NCHUNK + RB_GROUP) * 8 * RB_PITCH, :] = jnp.zeros(
            (RB_GROUP * 8 * RB_PITCH, LANES), F32)

    @pl.when(s < RB_NSUB)
    def _dispatch():
        sh2 = mod_ref[0, 3:4, :]
        sc2 = mod_ref[0, 4:5, :]
        h2 = x1_ref[...] * (1.0 + sc2) + sh2
        for j in range(8):
            stage_ref[RB_SPITCH * j:RB_SPITCH * j + RB_SUB, :] = h2[:, LANES * j:LANES * (j + 1)]

        for t in range(RB_SUB):
            slab = stage_ref[pl.ds(t, 8, stride=RB_SPITCH), :]
            for a in range(2):
                buf_ref[pl.ds(offs_ref[0, a, t], 8, stride=RB_PITCH), :] = slab

    @pl.when(s == RB_NSUB)
    def _experts():
        def copies(e, slot):
            return (pltpu.make_async_copy(wg_hbm.at[e], wg_buf.at[slot], sem.at[slot, 0]),
                    pltpu.make_async_copy(wu_hbm.at[e], wu_buf.at[slot], sem.at[slot, 1]),
                    pltpu.make_async_copy(wd_hbm.at[e], wd_buf.at[slot], sem.at[slot, 2]))

        def run_expert(e, slot):
            lo_row = start_ref[e]
            hi_row = lo_row + cnt_ref[b, e]

            c_lo = lax.shift_right_logical(lo_row, RB_CHUNK_BITS)
            c_hi = lax.shift_right_logical(hi_row + (RB_CHUNK - 1), RB_CHUNK_BITS)
            row = lax.broadcasted_iota(jnp.int32, (RB_CHUNK, 1), 0)

            def load(c):
                base = pl.multiple_of(c * (8 * RB_PITCH), 8)
                return [buf_ref[pl.ds(base + RB_PITCH * j, RB_CHUNK), :] for j in range(8)]

            def store(c, tiles, y):
                base = pl.multiple_of(c * (8 * RB_PITCH), 8)
                mine = (row >= lo_row - c * RB_CHUNK) & (row < hi_row - c * RB_CHUNK)
                for j in range(8):
                    buf_ref[pl.ds(base + RB_PITCH * j, RB_CHUNK), :] = jnp.where(
                        mine, y[:, LANES * j:LANES * (j + 1)], tiles[j])

            def group(i, carry):
                cs = [c_lo + RB_GROUP * i]
                for k in range(1, RB_GROUP):
                    cs.append(jnp.where(cs[0] + k < c_hi, cs[0] + k, RB_NCHUNK + k))
                tiles = [load(c) for c in cs]
                x = jnp.concatenate([jnp.concatenate(t, axis=-1) for t in tiles], axis=0).astype(BF16)
                a = jnp.dot(x, wg_buf[slot], preferred_element_type=F32)
                u = jnp.dot(x, wu_buf[slot], preferred_element_type=F32)
                z = (_silu(a) * u).astype(BF16)
                y = jnp.dot(z, wd_buf[slot], preferred_element_type=F32)
                for k, c in enumerate(cs):
                    store(c, tiles[k], y[RB_CHUNK * k:RB_CHUNK * (k + 1)])
                return carry

            lax.fori_loop(0, lax.div(c_hi - c_lo + (RB_GROUP - 1), RB_GROUP), group, 0)

        for e in range(RB_WSLOTS - 1):
            for c in copies(e, e):
                c.start()

        def ring_body(i, carry):
            for k in range(RB_WSLOTS):
                e = RB_WSLOTS * i + k
                ahead = e + RB_WSLOTS - 1

                @pl.when(ahead < N_EXPERTS)
                def _():
                    for c in copies(ahead, (k + RB_WSLOTS - 1) % RB_WSLOTS):
                        c.start()
                for c in copies(e, k):
                    c.wait()
                run_expert(e, k)
            return carry
        lax.fori_loop(0, N_EXPERTS // RB_WSLOTS, ring_body, 0)

    @pl.when(s > RB_NSUB)
    def _combine():
        for t in range(RB_SUB):
            acc = None
            for a in range(2):
                term = wts_ref[0, a, t] * buf_ref[pl.ds(offs_ref[0, a, t], 8, stride=RB_PITCH), :]
                acc = term if acc is None else acc + term
            stage_ref[pl.ds(t, 8, stride=RB_SPITCH), :] = acc
        gt2 = mod_ref[0, 5:6, :]
        f = jnp.concatenate([stage_ref[RB_SPITCH * j:RB_SPITCH * j + RB_SUB, :] for j in range(8)], axis=-1)
        o_ref[...] = _layer_norm(DEEPNORM_ALPHA * x1_ref[...] + (1.0 + gt2) * f, ln2g_ref[...], ln2b_ref[...])


def _rb_retile(a):
    tiles, two, t = a.shape
    return a.reshape(tiles, two, t // RB_SUB, RB_SUB).transpose(0, 2, 1, 3).reshape(-1, two, RB_SUB)


def _rb_offsets(cnt, e12, rank12):
    start = jnp.cumsum(cnt, axis=-1) - cnt
    start_t = jnp.repeat(start, e12.shape[0] // cnt.shape[0], axis=0)[:, None, None, :]
    hit = e12[..., None] == jnp.arange(N_EXPERTS, dtype=jnp.int32)
    p = jnp.sum(jnp.where(hit, start_t, 0), axis=-1) + rank12
    return lax.shift_right_logical(p, RB_CHUNK_BITS) * (8 * RB_PITCH) + (p & (RB_CHUNK - 1))


def _rb_moe(x1, modp, cnt, offs, wts, wg_bf16, wu_bf16, wd_bf16, ln2g, ln2b):
    n, d = x1.shape
    bsz = n // SEQ
    nsteps = 2 * RB_NSUB + 1

    def sub_index(s):
        return jnp.where(s < RB_NSUB, s, jnp.where(s == RB_NSUB, RB_NSUB - 1, s - RB_NSUB - 1))

    def tile_map(b, s, cnt_r):
        return (b * RB_NSUB + sub_index(s), 0)

    def tile_map3(b, s, cnt_r):
        return (b * RB_NSUB + sub_index(s), 0, 0)

    def out_map(b, s, cnt_r):
        return (b * RB_NSUB + jnp.maximum(s - RB_NSUB - 1, 0), 0)

    const = lambda shp: pl.BlockSpec(shp, lambda b, s, cnt_r: tuple(0 for _ in shp))
    anyspec = pl.BlockSpec(memory_space=pl.ANY)
    grid_spec = pltpu.PrefetchScalarGridSpec(
        num_scalar_prefetch=1,
        grid=(bsz, nsteps),
        in_specs=[pl.BlockSpec((RB_SUB, d), tile_map),
                  pl.BlockSpec((1, 6, d), lambda b, s, cnt_r: (b, 0, 0)),
                  pl.BlockSpec((1, 2, RB_SUB), tile_map3, memory_space=pltpu.SMEM),
                  pl.BlockSpec((1, 2, RB_SUB), tile_map3, memory_space=pltpu.SMEM),
                  anyspec, anyspec, anyspec,
                  const((1, d)), const((1, d))],
        out_specs=pl.BlockSpec((RB_SUB, d), out_map),
        scratch_shapes=[pltpu.VMEM(((RB_NCHUNK + RB_GROUP) * 8 * RB_PITCH, LANES), F32),
                        pltpu.VMEM((8 * RB_SPITCH, LANES), F32),
                        pltpu.VMEM((RB_WSLOTS, d, D_EXPERT), BF16),
                        pltpu.VMEM((RB_WSLOTS, d, D_EXPERT), BF16),
                        pltpu.VMEM((RB_WSLOTS, D_EXPERT, d), BF16),
                        pltpu.SMEM((N_EXPERTS,), jnp.int32),
                        pltpu.SemaphoreType.DMA((RB_WSLOTS, 3))])
    return pl.pallas_call(
        _rb_kernel,
        grid_spec=grid_spec,
        out_shape=jax.ShapeDtypeStruct((n, d), F32),
        compiler_params=_cparams(("arbitrary", "arbitrary")),
        name="moe_routed_ln",
    )(cnt, x1, modp, offs, wts, wg_bf16, wu_bf16, wd_bf16, ln2g, ln2b)


def _sample_in_kernel(x_ref, sh1_ref, sc1_ref, win_ref, ctx_ref, h0_ref, convw_ref, convb_ref,
                      wlo_ref, whi_ref, bgate_ref, lam_ref,
                      ylru_ref, q_ref, k_ref, v_ref, cstate_ref, hnew_ref):
    h = x_ref[...] * (1.0 + sc1_ref[...]) + sh1_ref[...]
    z = _dot(h, win_ref[...], True)
    xb = z[:, :LRU_WIDTH]
    gate = z[:, LRU_WIDTH:2 * LRU_WIDTH]
    c0 = ctx_ref[:, 0, :]
    c1 = ctx_ref[:, 1, :]
    c2 = ctx_ref[:, 2, :]
    xc = (convb_ref[...] + convw_ref[0:1, :] * c0 + convw_ref[1:2, :] * c1
          + convw_ref[2:3, :] * c2 + convw_ref[3:4, :] * xb)
    cstate_ref[:, 0, :] = c1
    cstate_ref[:, 1, :] = c2
    cstate_ref[:, 2, :] = xb
    sp = _softplus(-lam_ref[...])
    a, bterm = _lru_gates(xc, wlo_ref[...], whi_ref[...], bgate_ref[...], sp, True)
    hn = a * h0_ref[...] + bterm
    hnew_ref[...] = hn
    ylru_ref[...] = hn * _gelu_tanh(gate)
    low = lax.broadcasted_iota(jnp.int32, (DEC_BATCH, LANES), 1) < HEAD_DIM
    for c in range(4):
        qc = z[:, 2 * LRU_WIDTH + LANES * c:2 * LRU_WIDTH + LANES * (c + 1)]
        q_ref[pl.ds(c, DEC_BATCH, stride=N_HEADS), :] = jnp.where(low, qc, 0.0)
        q_ref[pl.ds(c + 4, DEC_BATCH, stride=N_HEADS), :] = jnp.where(low, 0.0, qc)
    k_ref[...] = z[:, 2 * LRU_WIDTH + ATTN_WIDTH:2 * LRU_WIDTH + ATTN_WIDTH + KV_WIDTH]
    v_ref[...] = z[:, 2 * LRU_WIDTH + ATTN_WIDTH + KV_WIDTH:]


def _sample_in(x, sh1, sc1, w_in_p, ctx, h0, conv_w, conv_b, wlo, whi, bgate, lam):
    n = DEC_BATCH
    outs = [jax.ShapeDtypeStruct((n, LRU_WIDTH), F32),
            jax.ShapeDtypeStruct((n * N_HEADS, LANES), F32),
            jax.ShapeDtypeStruct((n, KV_WIDTH), F32),
            jax.ShapeDtypeStruct((n, KV_WIDTH), F32),
            jax.ShapeDtypeStruct((n, CONV_WIDTH - 1, LRU_WIDTH), F32),
            jax.ShapeDtypeStruct((n, LRU_WIDTH), F32)]
    return pl.pallas_call(
        _sample_in_kernel,
        out_shape=outs,
        compiler_params=pltpu.CompilerParams(vmem_limit_bytes=VMEM_LIMIT),
        name="sample_inproj_rglru",
    )(x, sh1, sc1, w_in_p, ctx, h0, conv_w, conv_b, wlo, whi, bgate, lam)


def _sample_attn_kernel(q_ref, kn_ref, vn_ref, ck_ref, cv_ref, sink_ref, y_ref, nk_ref, nv_ref, *, bb):
    rows = lax.broadcasted_iota(jnp.int32, (WINDOW, KV_WIDTH), 0)
    nh = N_HEADS
    q_all = q_ref[...].reshape(bb * nh, LANES)
    kcat = ck_ref[...].reshape(bb * WINDOW, KV_WIDTH)
    vcat = cv_ref[...].reshape(bb * WINDOW, KV_WIDTH)
    kn_rep = jnp.broadcast_to(kn_ref[...][:, None, :], (bb, nh, KV_WIDTH)).reshape(bb * nh, KV_WIDTH)
    vn_rep = jnp.broadcast_to(vn_ref[...][:, None, :], (bb, nh, KV_WIDTH)).reshape(bb * nh, KV_WIDTH)
    sink = jnp.concatenate([sink_ref[...]] * bb, axis=0)
    s_full = _dot_nt(q_all, kcat, True)
    s = jnp.concatenate([s_full[nh * b:nh * (b + 1), WINDOW * b:WINDOW * (b + 1)] for b in range(bb)],
                        axis=0) * ATTN_SCALE
    s_self = jnp.sum(q_all * kn_rep, axis=-1, keepdims=True) * ATTN_SCALE
    m = jnp.maximum(jnp.maximum(jnp.max(s, axis=-1, keepdims=True), s_self), sink)
    e = jnp.exp(s - m)
    e_self = jnp.exp(s_self - m)
    den = jnp.sum(e, axis=-1, keepdims=True) + e_self + jnp.exp(sink - m)
    inv = 1.0 / den
    p = e * inv
    zero = jnp.zeros((nh, WINDOW), F32)
    p_wide = jnp.concatenate(
        [jnp.concatenate([p[nh * b:nh * (b + 1)] if c == b else zero for c in range(bb)], axis=-1)
         for b in range(bb)], axis=0)
    o = _dot(p_wide, vcat, True) + (e_self * inv) * vn_rep
    y_ref[...] = o.reshape(bb, nh, LANES)
    for b in range(bb):
        nk_ref[b] = jnp.where(rows == WINDOW - 1, kn_ref[b:b + 1, :], pltpu.roll(ck_ref[b], WINDOW - 1, axis=0))
        nv_ref[b] = jnp.where(rows == WINDOW - 1, vn_ref[b:b + 1, :], pltpu.roll(cv_ref[b], WINDOW - 1, axis=0))


def _sample_attn(q3, kn, vn, cache_k, cache_v, sinks, bb=16):
    n = DEC_BATCH
    kern = functools.partial(_sample_attn_kernel, bb=bb)
    return pl.pallas_call(
        kern,
        grid=(n // bb,),
        in_specs=[pl.BlockSpec((bb, N_HEADS, LANES), lambda i: (i, 0, 0)),
                  pl.BlockSpec((bb, KV_WIDTH), lambda i: (i, 0)),
                  pl.BlockSpec((bb, KV_WIDTH), lambda i: (i, 0)),
                  pl.BlockSpec((bb, WINDOW, KV_WIDTH), lambda i: (i, 0, 0)),
                  pl.BlockSpec((bb, WINDOW, KV_WIDTH), lambda i: (i, 0, 0)),
                  pl.BlockSpec((N_HEADS, 1), lambda i: (0, 0))],
        out_specs=[pl.BlockSpec((bb, N_HEADS, LANES), lambda i: (i, 0, 0)),
                   pl.BlockSpec((bb, WINDOW, KV_WIDTH), lambda i: (i, 0, 0)),
                   pl.BlockSpec((bb, WINDOW, KV_WIDTH), lambda i: (i, 0, 0))],
        out_shape=[jax.ShapeDtypeStruct((n, N_HEADS, LANES), F32),
                   jax.ShapeDtypeStruct((n, WINDOW, KV_WIDTH), F32),
                   jax.ShapeDtypeStruct((n, WINDOW, KV_WIDTH), F32)],
        compiler_params=_cparams(("arbitrary",)),
        name="sample_cache_attention",
    )(q3, kn, vn, cache_k, cache_v, sinks.reshape(N_HEADS, 1))


def _sample_out_kernel(x_ref, ylru_ref, yatt_ref, sh2_ref, sc2_ref, gt1_ref, glru_ref, gattn_ref, wout_ref,
                       ln1g_ref, ln1b_ref, wr_ref, br_ref, x1_ref, comb_ref):
    low = lax.broadcasted_iota(jnp.int32, (DEC_BATCH, LANES), 1) < HEAD_DIM
    yatt = jnp.concatenate(
        [jnp.where(low, yatt_ref[pl.ds(c, DEC_BATCH, stride=N_HEADS), :],
                   yatt_ref[pl.ds(c + 4, DEC_BATCH, stride=N_HEADS), :]) for c in range(4)], axis=-1)
    x1, comb = _outproj_body(x_ref[...], ylru_ref[...], yatt, sh2_ref[...], sc2_ref[...], gt1_ref[...],
                             glru_ref[...], gattn_ref[...], wout_ref[...], ln1g_ref[...], ln1b_ref[...],
                             wr_ref[...], br_ref[...], True)
    x1_ref[...] = x1
    comb_ref[...] = comb


def _sample_out(x, ylru, yatt2d, sh2, sc2, gt1, glru, gattn, wout_p, ln1g, ln1b, wr, br):
    n = DEC_BATCH
    return pl.pallas_call(
        _sample_out_kernel,
        out_shape=[jax.ShapeDtypeStruct((n, D_MODEL), F32), jax.ShapeDtypeStruct((n, ROUTE_LANES), F32)],
        compiler_params=pltpu.CompilerParams(vmem_limit_bytes=VMEM_LIMIT),
        name="sample_outproj_ln_route",
    )(x, ylru, yatt2d, sh2, sc2, gt1, glru, gattn, wout_p, ln1g, ln1b, wr, br)


def _block_diag_halves(w_a, w_x):
    def bd(w4):
        eye = jnp.eye(4, dtype=w4.dtype)
        return (w4[:, :, None, :] * eye[:, None, :, None]).reshape(256, 256)
    lo = jnp.concatenate([bd(w_a[:4]), bd(w_x[:4])], axis=1)
    hi = jnp.concatenate([bd(w_a[4:]), bd(w_x[4:])], axis=1)
    return lo, hi


def kernel(x_prompt, x_sample, c_prompt, c_sample, state_conv, state_h, cache_k, cache_v, w_ada, b_ada, w_in,
           conv_w, conv_b, w_rg_a, b_rg_a, w_rg_x, b_rg_x, lru_lambda, sinks, g_lru, g_attn, w_out, ln1_g, ln1_b,
           w_group, b_group, w_router, b_router, w_gate, w_up, w_down, ln2_g, ln2_b):
    d = D_MODEL
    perm = jnp.asarray(HEAD_PERM)
    w_in0 = w_in[0]
    q0 = 2 * LRU_WIDTH
    w_in_p = jnp.concatenate([w_in0[:, :q0], w_in0[:, q0:q0 + ATTN_WIDTH][:, perm], w_in0[:, q0 + ATTN_WIDTH:]],
                             axis=1)
    w_out0 = w_out[0]
    w_out_p = jnp.concatenate([w_out0[:LRU_WIDTH], w_out0[LRU_WIDTH:][perm]], axis=0)
    g_attn_p = g_attn[0][perm].reshape(1, -1)
    glru = g_lru[0].reshape(1, -1)
    wlo, whi = _block_diag_halves(w_rg_a[0], w_rg_x[0])
    bgate = jnp.concatenate([b_rg_a[0].reshape(-1), b_rg_x[0].reshape(-1)]).reshape(1, -1)
    lam = lru_lambda[0].reshape(1, -1)
    convw = conv_w[0]
    convb = conv_b[0].reshape(1, -1)
    ln1g, ln1b = ln1_g[0].reshape(1, -1), ln1_b[0].reshape(1, -1)
    ln2g, ln2b = ln2_g[0].reshape(1, -1), ln2_b[0].reshape(1, -1)
    wr = jnp.concatenate([jnp.transpose(w_router[0], (1, 0, 2)).reshape(d, N_EXPERTS), w_group[0],
                          jnp.zeros((d, ROUTE_LANES - N_EXPERTS - N_GROUPS), F32)], axis=1)
    br = jnp.concatenate([b_router[0].reshape(-1), b_group[0],
                          jnp.zeros((ROUTE_LANES - N_EXPERTS - N_GROUPS,), F32)]).reshape(1, -1)
    sink_p = sinks[0]

    c_all = jnp.concatenate([c_prompt, jnp.zeros((8 - BATCH, d), F32), c_sample], axis=0)
    mod = _ada(c_all, w_ada[0], b_ada[0])
    modp = mod[:BATCH].reshape(BATCH, 6, d)
    mods = mod[8:]
    sh1_s, sc1_s, gt1_s, sh2_s, sc2_s, gt2_s = (mods[:, k * d:(k + 1) * d] for k in range(6))

    zlru, zqkv, kvlast = _inproj(x_prompt, modp, w_in_p.astype(BF16))
    ylru, cstate8, hlast8 = _lru(zlru, convw, convb, wlo.astype(BF16), whi.astype(BF16), bgate, lam)
    yatt = _attn(zqkv, sink_p)
    n_p = BATCH * SEQ
    x1_p, info, cntf = _outproj_prompt(x_prompt.reshape(n_p, d), ylru.reshape(n_p, LRU_WIDTH),
                                       yatt.reshape(n_p, ATTN_WIDTH), modp, glru, g_attn_p, w_out_p.astype(BF16),
                                       ln1g, ln1b, wr, br, tm=OUTPROJ_TILE)
    cnt = cntf[:, 0, :N_EXPERTS].astype(jnp.int32)
    offs = _rb_retile(_rb_offsets(cnt, info[:, 0:2].astype(jnp.int32), info[:, 4:6].astype(jnp.int32)))
    wg_b, wu_b, wd_b = w_gate[0].astype(BF16), w_up[0].astype(BF16), w_down[0].astype(BF16)
    y_p = _rb_moe(x1_p, modp, cnt, offs, _rb_retile(info[:, 2:4]), wg_b, wu_b, wd_b, ln2g, ln2b)

    ylru_s, q2d, kn, vn, cstate_s, hnew_s = _sample_in(
        x_sample.reshape(DEC_BATCH, d), sh1_s, sc1_s, w_in_p, state_conv[0], state_h[0],
        convw, convb, wlo, whi, bgate, lam)
    yatt3, newk, newv = _sample_attn(q2d.reshape(DEC_BATCH, N_HEADS, LANES), kn, vn,
                                     cache_k[0].reshape(DEC_BATCH, WINDOW, KV_WIDTH),
                                     cache_v[0].reshape(DEC_BATCH, WINDOW, KV_WIDTH), sink_p)
    x1_s, comb_s = _sample_out(x_sample.reshape(DEC_BATCH, d), ylru_s, yatt3.reshape(DEC_BATCH * N_HEADS, LANES),
                               sh2_s, sc2_s, gt1_s, glru, g_attn_p, w_out_p, ln1g, ln1b, wr, br)
    y_s = _moe_dense(x1_s, comb_s, sh2_s, sc2_s, gt2_s, wg_b, wu_b, wd_b, ln2g, ln2b, DEC_BATCH)

    return (y_p.reshape(BATCH, SEQ, d),
            y_s.reshape(DEC_BATCH, 1, d),
            cstate8[:, 5:8][None],
            hlast8[:, 7][None],
            kvlast[:, :, :KV_WIDTH].reshape(1, BATCH, WINDOW, N_KV_HEADS, HEAD_DIM),
            kvlast[:, :, KV_WIDTH:].reshape(1, BATCH, WINDOW, N_KV_HEADS, HEAD_DIM),
            cstate_s[None],
            hnew_s[None],
            newk.reshape(1, DEC_BATCH, WINDOW, N_KV_HEADS, HEAD_DIM),
            newv.reshape(1, DEC_BATCH, WINDOW, N_KV_HEADS, HEAD_DIM))
```

```python
import functools

import jax
import jax.numpy as jnp
import numpy as np
from jax import lax
from jax.experimental import pallas as pl
from jax.experimental.pallas import tpu as pltpu

F32 = jnp.float32
BF16 = jnp.bfloat16
HIGHEST = lax.Precision.HIGHEST

D_MODEL = 1024
BATCH = 4
SEQ = 4096
DEC_BATCH = 128
LRU_WIDTH = 512
LRU_BLOCKS = 8
LRU_BLOCK = 64
CONV_WIDTH = 4
LRU_C = 8.0
N_HEADS = 8
N_KV_HEADS = 2
HEAD_DIM = 64
ATTN_WIDTH = 512
KV_WIDTH = 128
WINDOW = 128
IN_WIDTH = 2 * LRU_WIDTH + ATTN_WIDTH + 2 * KV_WIDTH
N_GROUPS = 4
EXPERTS_PER_GROUP = 8
N_EXPERTS = 32
D_EXPERT = 256
DEEPNORM_ALPHA = 2.0 ** 0.25
LN_EPS = 1e-5
RMS_EPS = 1e-6
ATTN_SCALE = HEAD_DIM ** -0.5

LANES = 128
ROUTE_LANES = 128
ROUTE_INFO = 40
VMEM_LIMIT = 56 * 1024 * 1024

HEAD_PERM = np.concatenate(
    [np.concatenate([np.arange(64 * c, 64 * c + 64), np.arange(64 * (c + 4), 64 * (c + 4) + 64)])
     for c in range(4)])


def _cparams(sem):
    return pltpu.CompilerParams(dimension_semantics=sem, vmem_limit_bytes=VMEM_LIMIT)


def _dot(a, b, exact):
    if exact:
        return jnp.dot(a, b, precision=HIGHEST, preferred_element_type=F32)
    return jnp.dot(a.astype(BF16), b.astype(BF16), preferred_element_type=F32)


def _dot_nt(a, b, exact):
    dn = (((1,), (1,)), ((), ()))
    if exact:
        return lax.dot_general(a, b, dn, precision=HIGHEST, preferred_element_type=F32)
    return lax.dot_general(a.astype(BF16), b.astype(BF16), dn, preferred_element_type=F32)


def _sigmoid(x):
    return 1.0 / (1.0 + jnp.exp(-x))


def _silu(x):
    return x * _sigmoid(x)


def _gelu_tanh(x):
    return 0.5 * x * (1.0 + jnp.tanh(np.sqrt(2.0 / np.pi).astype(np.float32) * (x + 0.044715 * (x * x * x))))


def _softplus(x):
    return jnp.maximum(x, 0.0) + jnp.log1p(jnp.exp(-jnp.abs(x)))


def _layer_norm(x, g, b):
    mu = jnp.mean(x, axis=-1, keepdims=True)
    xc = x - mu
    var = jnp.mean(xc * xc, axis=-1, keepdims=True)
    return xc * lax.rsqrt(var + LN_EPS) * g + b


def _rms_norm(x, g):
    return x * lax.rsqrt(jnp.mean(x * x, axis=-1, keepdims=True) + RMS_EPS) * g


def _ada_kernel(c_ref, w_ref, b_ref, o_ref):
    o_ref[...] = _dot(_silu(c_ref[...]), w_ref[...], True) + b_ref[...]


def _ada(c_all, w_ada, b_ada):
    rows = c_all.shape[0]
    bn = 1024
    return pl.pallas_call(
        _ada_kernel,
        grid=(6 * D_MODEL // bn,),
        in_specs=[pl.BlockSpec((rows, D_MODEL), lambda j: (0, 0)),
                  pl.BlockSpec((D_MODEL, bn), lambda j: (0, j)),
                  pl.BlockSpec((1, bn), lambda j: (0, j))],
        out_specs=pl.BlockSpec((rows, bn), lambda j: (0, j)),
        out_shape=jax.ShapeDtypeStruct((rows, 6 * D_MODEL), F32),
        compiler_params=_cparams(("arbitrary",)),
        name="ada_modulation",
    )(c_all, w_ada, b_ada.reshape(1, -1))


QKV_WIDTH = ATTN_WIDTH + 2 * KV_WIDTH


def _inproj_kernel(x_ref, mod_ref, w_ref, lru_ref, qkv_ref, kvlast_ref):
    sh1 = mod_ref[0, 0:1, :]
    sc1 = mod_ref[0, 1:2, :]
    h = x_ref[0] * (1.0 + sc1) + sh1
    z = _dot(h, w_ref[...], False)
    lru_ref[0] = z[:, :2 * LRU_WIDTH]
    qkv_ref[0] = z[:, 2 * LRU_WIDTH:].astype(BF16)
    kvlast_ref[0] = z[z.shape[0] - WINDOW:, 2 * LRU_WIDTH + ATTN_WIDTH:]


def _inproj(x, modp, w_in_bf16, tm=512):
    b, t, d = x.shape
    return pl.pallas_call(
        _inproj_kernel,
        grid=(b, t // tm),
        in_specs=[pl.BlockSpec((1, tm, d), lambda i, j: (i, j, 0)),
                  pl.BlockSpec((1, 6, d), lambda i, j: (i, 0, 0)),
                  pl.BlockSpec((d, IN_WIDTH), lambda i, j: (0, 0))],
        out_specs=[pl.BlockSpec((1, tm, 2 * LRU_WIDTH), lambda i, j: (i, j, 0)),
                   pl.BlockSpec((1, tm, QKV_WIDTH), lambda i, j: (i, j, 0)),
                   pl.BlockSpec((1, WINDOW, 2 * KV_WIDTH), lambda i, j: (i, 0, 0))],
        out_shape=[jax.ShapeDtypeStruct((b, t, 2 * LRU_WIDTH), F32),
                   jax.ShapeDtypeStruct((b, t, QKV_WIDTH), BF16),
                   jax.ShapeDtypeStruct((b, WINDOW, 2 * KV_WIDTH), F32)],
        compiler_params=_cparams(("arbitrary", "arbitrary")),
        name="prompt_inproj",
    )(x, modp, w_in_bf16)


def _lru_gates(xc, wlo, whi, bgate, sp_neg_lam, exact):
    g_lo = _dot(xc[:, :256], wlo, exact)
    g_hi = _dot(xc[:, 256:], whi, exact)
    ga = jnp.concatenate([g_lo[:, :256], g_hi[:, :256]], axis=-1) + bgate[:, :LRU_WIDTH]
    gx = jnp.concatenate([g_lo[:, 256:], g_hi[:, 256:]], axis=-1) + bgate[:, LRU_WIDTH:]
    r = _sigmoid(ga)
    i = _sigmoid(gx)
    log_a = -LRU_C * r * sp_neg_lam
    a = jnp.exp(log_a)
    one_minus_a2 = -jnp.tanh(log_a) * (a * a + 1.0) if exact else 1.0 - a * a
    bterm = jnp.sqrt(one_minus_a2) * (i * xc)
    return a, bterm


def _lru_kernel(z_ref, convw_ref, convb_ref, wlo_ref, whi_ref, bgate_ref, lam_ref, wg_ref, wu_ref, wd_ref,
                y_ref, cstate_ref, hlast_ref, wgb_ref, wub_ref, wdb_ref, tail_ref, carry_ref, *, tl):
    j = pl.program_id(1)

    @pl.when(j == 0)
    def _():
        tail_ref[...] = jnp.zeros_like(tail_ref)
        carry_ref[...] = jnp.zeros_like(carry_ref)

    wgb_ref[...] = wg_ref[...].astype(BF16)
    wub_ref[...] = wu_ref[...].astype(BF16)
    wdb_ref[...] = wd_ref[...].astype(BF16)

    xb = z_ref[0, :, :LRU_WIDTH]
    gate = z_ref[0, :, LRU_WIDTH:]
    rows = lax.broadcasted_iota(jnp.int32, (tl, LRU_WIDTH), 0)

    xc = convb_ref[...] + convw_ref[3:4, :] * xb
    rows8 = lax.broadcasted_iota(jnp.int32, (8, LRU_WIDTH), 0)
    tail = tail_ref[...]
    for back in (1, 2, 3):
        rolled = pltpu.roll(xb, back, axis=0)
        top = jnp.where(rows8 >= back, rolled[:8], pltpu.roll(tail, back, axis=0))
        shifted = jnp.concatenate([top, rolled[8:]], axis=0)
        xc = xc + convw_ref[3 - back:4 - back, :] * shifted
    tail_ref[...] = xb[tl - 8:, :]
    cstate_ref[0] = xb[tl - 8:, :]

    sp = _softplus(-lam_ref[...])
    a, bterm = _lru_gates(xc, wlo_ref[...], whi_ref[...], bgate_ref[...], sp, False)

    s = 1
    while s < tl:
        if s < 8:
            a_sh = jnp.where(rows >= s, pltpu.roll(a, s, axis=0), 1.0)
            b_sh = jnp.where(rows >= s, pltpu.roll(bterm, s, axis=0), 0.0)
        else:
            a_sh = jnp.concatenate([jnp.ones((s, LRU_WIDTH), F32), a[:tl - s]], axis=0)
            b_sh = jnp.concatenate([jnp.zeros((s, LRU_WIDTH), F32), bterm[:tl - s]], axis=0)
        bterm = a * b_sh + bterm
        a = a * a_sh
        s *= 2
    h = a * carry_ref[7:8, :] + bterm
    carry_ref[...] = h[tl - 8:, :]
    hlast_ref[0] = h[tl - 8:, :]
    y_ref[0] = h * _gelu_tanh(gate)


def _lru(zin, conv_w, conv_b, wlo, whi, bgate, lam, w_gate, w_up, w_down, tl=512):
    b, t, _ = zin.shape
    steps = t // tl
    assert b * steps == N_EXPERTS
    d = D_MODEL
    kern = functools.partial(_lru_kernel, tl=tl)
    full = lambda shp: pl.BlockSpec(shp, lambda i, j: tuple(0 for _ in shp))
    per_step = lambda shp: pl.BlockSpec(shp, lambda i, j: (i * steps + j, 0, 0))
    return pl.pallas_call(
        kern,
        grid=(b, steps),
        in_specs=[pl.BlockSpec((1, tl, 2 * LRU_WIDTH), lambda i, j: (i, j, 0)),
                  full((CONV_WIDTH, LRU_WIDTH)), full((1, LRU_WIDTH)),
                  full((256, 512)), full((256, 512)), full((1, 2 * LRU_WIDTH)), full((1, LRU_WIDTH)),
                  per_step((1, d, D_EXPERT)), per_step((1, d, D_EXPERT)), per_step((1, D_EXPERT, d))],
        out_specs=[pl.BlockSpec((1, tl, LRU_WIDTH), lambda i, j: (i, j, 0)),
                   pl.BlockSpec((1, 8, LRU_WIDTH), lambda i, j: (i, 0, 0)),
                   pl.BlockSpec((1, 8, LRU_WIDTH), lambda i, j: (i, 0, 0)),
                   per_step((1, d, D_EXPERT)), per_step((1, d, D_EXPERT)), per_step((1, D_EXPERT, d))],
        out_shape=[jax.ShapeDtypeStruct((b, t, LRU_WIDTH), F32),
                   jax.ShapeDtypeStruct((b, 8, LRU_WIDTH), F32),
                   jax.ShapeDtypeStruct((b, 8, LRU_WIDTH), F32),
                   jax.ShapeDtypeStruct(w_gate.shape, BF16),
                   jax.ShapeDtypeStruct(w_up.shape, BF16),
                   jax.ShapeDtypeStruct(w_down.shape, BF16)],
        scratch_shapes=[pltpu.VMEM((8, LRU_WIDTH), F32), pltpu.VMEM((8, LRU_WIDTH), F32)],
        compiler_params=_cparams(("arbitrary", "arbitrary")),
        name="prompt_rglru",
    )(zin, conv_w, conv_b, wlo, whi, bgate, lam, w_gate, w_up, w_down)


ATTN_BLOCKS = 16


def _attn_kernel(q_ref, k_ref, v_ref, sink_ref, o_ref, kprev_ref, vprev_ref):
    j = pl.program_id(1)

    @pl.when(j == 0)
    def _():
        kprev_ref[...] = jnp.zeros_like(kprev_ref)
        vprev_ref[...] = jnp.zeros_like(vprev_ref)

    blk = WINDOW
    lane = lax.broadcasted_iota(jnp.int32, (blk, LANES), 1)
    low = lane < HEAD_DIM
    qi = lax.broadcasted_iota(jnp.int32, (blk, 2 * blk), 0)
    sj = lax.broadcasted_iota(jnp.int32, (blk, 2 * blk), 1)
    rel = blk + qi - sj
    in_window = (rel >= 0) & (rel <= WINDOW)
    sink = sink_ref[...].reshape(N_HEADS, blk, 1)
    k_ext = jnp.concatenate([kprev_ref[...], k_ref[0]], axis=0)
    v_ext = jnp.concatenate([vprev_ref[...], v_ref[0]], axis=0)
    v_ext = jnp.concatenate([v_ext, jnp.ones_like(v_ext)], axis=-1)
    for n in range(ATTN_BLOCKS):
        q = q_ref[0, blk * n:blk * (n + 1), :]
        pieces = []
        for half in (0, 1):
            for c in range(4):
                qc = q[:, LANES * c:LANES * (c + 1)]
                pieces.append(jnp.where(low if half == 0 else ~low, qc, 0.0).astype(BF16))
        q8 = jnp.concatenate(pieces, axis=0)
        k_band = k_ext[blk * n:blk * (n + 2)]
        v_band = v_ext[blk * n:blk * (n + 2)]
        s = _dot_nt(q8, k_band, False) * ATTN_SCALE
        s = s.reshape(N_HEADS, blk, 2 * blk)
        valid = in_window & ((sj >= blk) | (j > 0)) if n == 0 else in_window
        s = jnp.where(valid[None], s, -jnp.inf)
        m = jnp.maximum(jnp.max(s, axis=-1, keepdims=True), sink)
        e = jnp.exp(s - m).reshape(N_HEADS * blk, 2 * blk)
        ov = _dot(e, v_band, False)
        den = ov[:, KV_WIDTH:] + jnp.exp(sink - m).reshape(N_HEADS * blk, 1)
        o8 = ov[:, :KV_WIDTH] * (1.0 / den)
        cols = []
        for c in range(4):
            cols.append(jnp.where(low, o8[blk * c:blk * (c + 1)], o8[blk * (c + 4):blk * (c + 5)]))
        o_ref[0, blk * n:blk * (n + 1), :] = jnp.concatenate(cols, axis=-1)
    kprev_ref[...] = k_ref[0, blk * (ATTN_BLOCKS - 1):, :]
    vprev_ref[...] = v_ref[0, blk * (ATTN_BLOCKS - 1):, :]


def _attn(qkv, sinks):
    b, t, _ = qkv.shape
    blk = WINDOW
    tq = blk * ATTN_BLOCKS
    sink_col = jnp.repeat(sinks.astype(F32), blk).reshape(N_HEADS * blk, 1)
    kcol = ATTN_WIDTH // KV_WIDTH
    return pl.pallas_call(
        _attn_kernel,
        grid=(b, t // tq),
        in_specs=[pl.BlockSpec((1, tq, ATTN_WIDTH), lambda i, j: (i, j, 0)),
                  pl.BlockSpec((1, tq, KV_WIDTH), lambda i, j: (i, j, kcol)),
                  pl.BlockSpec((1, tq, KV_WIDTH), lambda i, j: (i, j, kcol + 1)),
                  pl.BlockSpec((N_HEADS * blk, 1), lambda i, j: (0, 0))],
        out_specs=pl.BlockSpec((1, tq, ATTN_WIDTH), lambda i, j: (i, j, 0)),
        out_shape=jax.ShapeDtypeStruct((b, t, ATTN_WIDTH), F32),
        scratch_shapes=[pltpu.VMEM((blk, KV_WIDTH), BF16), pltpu.VMEM((blk, KV_WIDTH), BF16)],
        compiler_params=_cparams(("arbitrary", "arbitrary")),
        name="prompt_window_attention",
    )(qkv, qkv, qkv, sink_col)


def _dot_split3(a, b):
    a_hi = a.astype(BF16)
    b_hi = b.astype(BF16)
    a_lo = (a - a_hi.astype(F32)).astype(BF16)
    b_lo = (b - b_hi.astype(F32)).astype(BF16)
    return (jnp.dot(a_hi, b_hi, preferred_element_type=F32) + jnp.dot(a_hi, b_lo, preferred_element_type=F32)
            + jnp.dot(a_lo, b_hi, preferred_element_type=F32))


def _route(h2, wr, br, exact):
    t = h2.shape[0]
    logits = (_dot(h2, wr, True) if exact else _dot_split3(h2, wr)) + br
    lane = lax.broadcasted_iota(jnp.int32, (t, ROUTE_LANES), 1).astype(F32)
    neg = -jnp.inf
    big = float(ROUTE_LANES)
    is_g = (lane >= N_EXPERTS) & (lane < N_EXPERTS + N_GROUPS)
    lg = jnp.where(is_g, logits, neg)
    mg = jnp.max(lg, axis=-1, keepdims=True)
    eg = jnp.where(is_g, jnp.exp(lg - mg), 0.0)
    pg = eg / jnp.sum(eg, axis=-1, keepdims=True)
    g_val = jnp.max(pg, axis=-1, keepdims=True)
    g_lane = jnp.min(jnp.where((pg == g_val) & is_g, lane, big), axis=-1, keepdims=True)
    g_idx = g_lane - N_EXPERTS
    in_grp = (lane >= g_idx * EXPERTS_PER_GROUP) & (lane < (g_idx + 1.0) * EXPERTS_PER_GROUP)
    le = jnp.where(in_grp, logits, neg)
    me = jnp.max(le, axis=-1, keepdims=True)
    ee = jnp.where(in_grp, jnp.exp(le - me), 0.0)
    pe = ee / jnp.sum(ee, axis=-1, keepdims=True)
    v1 = jnp.max(pe, axis=-1, keepdims=True)
    l1 = jnp.min(jnp.where((pe == v1) & in_grp, lane, big), axis=-1, keepdims=True)
    rest = in_grp & (lane != l1)
    pe2 = jnp.where(rest, pe, -1.0)
    v2 = jnp.max(pe2, axis=-1, keepdims=True)
    l2 = jnp.min(jnp.where((pe2 == v2) & rest, lane, big), axis=-1, keepdims=True)
    tot = v1 + v2
    w1 = g_val * v1 / tot
    w2 = g_val * v2 / tot
    comb = jnp.where(lane == l1, w1, 0.0) + jnp.where(lane == l2, w2, 0.0)
    return (comb + jnp.where(lane == ROUTE_INFO, l1, 0.0) + jnp.where(lane == ROUTE_INFO + 1, l2, 0.0)
            + jnp.where(lane == ROUTE_INFO + 2, w1, 0.0) + jnp.where(lane == ROUTE_INFO + 3, w2, 0.0))


def _outproj_body(x, ylru, yatt, sh2, sc2, gt1, glru, gattn, wout, ln1g, ln1b, wr, br, exact):
    mixin = jnp.concatenate([_rms_norm(ylru, glru), _rms_norm(yatt, gattn)], axis=-1)
    mix = _dot(mixin, wout, exact)
    x1 = _layer_norm(DEEPNORM_ALPHA * x + (1.0 + gt1) * mix, ln1g, ln1b)
    h2 = x1 * (1.0 + sc2) + sh2
    return x1, _route(h2, wr, br, exact)


def _outproj_prompt_kernel(x_ref, ylru_ref, yatt_ref, mod_ref, glru_ref, gattn_ref, wout_ref,
                           ln1g_ref, ln1b_ref, wr_ref, br_ref, x1_ref, info_ref, cnt_ref, tri_ref, carry_ref,
                           *, tm, per_seq):
    i = pl.program_id(0)

    @pl.when(i == 0)
    def _():
        r = lax.broadcasted_iota(jnp.int32, (tm, tm), 0)
        c = lax.broadcasted_iota(jnp.int32, (tm, tm), 1)
        tri_ref[...] = jnp.where(c < r, 1.0, 0.0).astype(BF16)

    @pl.when(i % per_seq == 0)
    def _():
        carry_ref[...] = jnp.zeros_like(carry_ref)

    gt1 = mod_ref[0, 2:3, :]
    sh2 = mod_ref[0, 3:4, :]
    sc2 = mod_ref[0, 4:5, :]
    combs = []
    nsplit = 2
    for h in range(nsplit):
        rows = slice(h * (tm // nsplit), (h + 1) * (tm // nsplit))
        x1_h, comb_h = _outproj_body(x_ref[rows, :], ylru_ref[rows, :], yatt_ref[rows, :], sh2, sc2, gt1,
                                     glru_ref[...], gattn_ref[...], wout_ref[...], ln1g_ref[...], ln1b_ref[...],
                                     wr_ref[...], br_ref[...], False)
        x1_ref[rows, :] = x1_h
        combs.append(comb_h)
    comb = jnp.concatenate(combs, axis=0)
    lane = lax.broadcasted_iota(jnp.int32, (tm, ROUTE_LANES), 1).astype(F32)
    l1 = jnp.sum(jnp.where(lane == ROUTE_INFO, comb, 0.0), axis=-1, keepdims=True)
    l2 = jnp.sum(jnp.where(lane == ROUTE_INFO + 1, comb, 0.0), axis=-1, keepdims=True)
    o1 = lane == l1
    o2 = lane == l2
    onehot = jnp.where(o1 | o2, 1.0, 0.0)
    before = jnp.dot(tri_ref[...], onehot.astype(BF16), preferred_element_type=F32) + carry_ref[0:1, :]
    rank1 = jnp.sum(jnp.where(o1, before, 0.0), axis=-1, keepdims=True)
    rank2 = jnp.sum(jnp.where(o2, before, 0.0), axis=-1, keepdims=True)
    total = carry_ref[0:1, :] + jnp.sum(onehot, axis=0, keepdims=True)
    carry_ref[...] = jnp.broadcast_to(total, carry_ref.shape)
    cnt_ref[0] = jnp.broadcast_to(total, (8, ROUTE_LANES))
    info = (comb + jnp.where(lane == ROUTE_INFO + 4, rank1, 0.0) + jnp.where(lane == ROUTE_INFO + 5, rank2, 0.0))
    info_ref[0] = jnp.transpose(info)[ROUTE_INFO:ROUTE_INFO + 8, :]


OUTPROJ_TILE = 1024


def _outproj_prompt(x2d, ylru2d, yatt2d, modp, glru, gattn, wout_bf16, ln1g, ln1b, wr, br, tm=OUTPROJ_TILE):
    n, d = x2d.shape
    per_seq = SEQ // tm
    full = lambda shp: pl.BlockSpec(shp, lambda i: tuple(0 for _ in shp))
    kern = functools.partial(_outproj_prompt_kernel, tm=tm, per_seq=per_seq)
    return pl.pallas_call(
        kern,
        grid=(n // tm,),
        in_specs=[pl.BlockSpec((tm, d), lambda i: (i, 0)),
                  pl.BlockSpec((tm, LRU_WIDTH), lambda i: (i, 0)),
                  pl.BlockSpec((tm, ATTN_WIDTH), lambda i: (i, 0)),
                  pl.BlockSpec((1, 6, d), lambda i: (i // per_seq, 0, 0)),
                  full((1, LRU_WIDTH)), full((1, ATTN_WIDTH)), full((d, d)),
                  full((1, d)), full((1, d)), full((d, ROUTE_LANES)), full((1, ROUTE_LANES))],
        out_specs=[pl.BlockSpec((tm, d), lambda i: (i, 0)),
                   pl.BlockSpec((1, 8, tm), lambda i: (i, 0, 0)),
                   pl.BlockSpec((1, 8, ROUTE_LANES), lambda i: (i // per_seq, 0, 0))],
        out_shape=[jax.ShapeDtypeStruct((n, d), F32),
                   jax.ShapeDtypeStruct((n // tm, 8, tm), F32),
                   jax.ShapeDtypeStruct((n // SEQ, 8, ROUTE_LANES), F32)],
        scratch_shapes=[pltpu.VMEM((tm, tm), BF16), pltpu.VMEM((8, ROUTE_LANES), F32)],
        compiler_params=_cparams(("arbitrary",)),
        name="prompt_outproj_ln_route",
    )(x2d, ylru2d, yatt2d, modp, glru, gattn, wout_bf16, ln1g, ln1b, wr, br)


DENSE_EXPERTS_PER_STEP = 4


def _moe_kernel(x1_ref, comb_ref, sh2_ref, sc2_ref, gt2_ref, wg_ref, wu_ref, wd_ref, ln2g_ref, ln2b_ref,
                o_ref, h2_ref, acc_ref):
    g = pl.program_id(1)

    @pl.when(g == 0)
    def _():
        h2_ref[...] = (x1_ref[...] * (1.0 + sc2_ref[...]) + sh2_ref[...]).astype(BF16)
        acc_ref[...] = jnp.zeros_like(acc_ref)

    h2 = h2_ref[...]
    comb = comb_ref[...]
    lane = lax.broadcasted_iota(jnp.int32, comb.shape, 1)
    part = None
    for k in range(DENSE_EXPERTS_PER_STEP):
        a = jnp.dot(h2, wg_ref[k].astype(BF16), preferred_element_type=F32)
        u = jnp.dot(h2, wu_ref[k].astype(BF16), preferred_element_type=F32)
        c_e = jnp.sum(jnp.where(lane == g * DENSE_EXPERTS_PER_STEP + k, comb, 0.0), axis=-1, keepdims=True)
        z = _silu(a) * u * c_e
        y = jnp.dot(z.astype(BF16), wd_ref[k].astype(BF16), preferred_element_type=F32)
        part = y if part is None else part + y
    acc_ref[...] += part

    @pl.when(g == N_EXPERTS // DENSE_EXPERTS_PER_STEP - 1)
    def _():
        o_ref[...] = _layer_norm(DEEPNORM_ALPHA * x1_ref[...] + (1.0 + gt2_ref[...]) * acc_ref[...],
                                 ln2g_ref[...], ln2b_ref[...])


def _moe_dense(x1, comb, sh2, sc2, gt2, w_gate, w_up, w_down, ln2g, ln2b, tm):
    n, d = x1.shape
    eg = DENSE_EXPERTS_PER_STEP
    mspec = pl.BlockSpec((tm, d), lambda i, e: (i, 0))
    full = lambda shp: pl.BlockSpec(shp, lambda i, e: tuple(0 for _ in shp))
    return pl.pallas_call(
        _moe_kernel,
        grid=(n // tm, N_EXPERTS // eg),
        in_specs=[pl.BlockSpec((tm, d), lambda i, e: (i, 0)),
                  pl.BlockSpec((tm, ROUTE_LANES), lambda i, e: (i, 0)),
                  mspec, mspec, mspec,
                  pl.BlockSpec((eg, d, D_EXPERT), lambda i, e: (e, 0, 0)),
                  pl.BlockSpec((eg, d, D_EXPERT), lambda i, e: (e, 0, 0)),
                  pl.BlockSpec((eg, D_EXPERT, d), lambda i, e: (e, 0, 0)),
                  full((1, d)), full((1, d))],
        out_specs=pl.BlockSpec((tm, d), lambda i, e: (i, 0)),
        out_shape=jax.ShapeDtypeStruct((n, d), F32),
        scratch_shapes=[pltpu.VMEM((tm, d), BF16), pltpu.VMEM((tm, d), F32)],
        compiler_params=_cparams(("arbitrary", "arbitrary")),
        name="moe_dense_ln",
    )(x1, comb, sh2, sc2, gt2, w_gate, w_up, w_down, ln2g, ln2b)


RB_SUB = 512
RB_NSUB = SEQ // RB_SUB
RB_CHUNK = 128
RB_CHUNK_BITS = 7
RB_NCHUNK = 2 * SEQ // RB_CHUNK
RB_PITCH = RB_CHUNK + 8
RB_SPITCH = RB_SUB + 8
RB_GROUP = 3
RB_WSLOTS = 4


def _rb_kernel(cnt_ref, x1_ref, mod_ref, offs_ref, wts_ref, wg_hbm, wu_hbm, wd_hbm, ln2g_ref, ln2b_ref,
               o_ref, buf_ref, stage_ref, wg_buf, wu_buf, wd_buf, start_ref, sem):
    b = pl.program_id(0)
    s = pl.program_id(1)

    @pl.when(s == 0)
    def _starts():
        def body(e, run):
            start_ref[e] = run
            return run + cnt_ref[b, e]
        lax.fori_loop(0, N_EXPERTS, body, jnp.int32(0))
        buf_ref[RB_NCHUNK * 8 * RB_PITCH:(RB_NCHUNK + RB_GROUP) * 8 * RB_PITCH, :] = jnp.zeros(
            (RB_GROUP * 8 * RB_PITCH, LANES), F32)

    @pl.when(s < RB_NSUB)
    def _dispatch():
        sh2 = mod_ref[0, 3:4, :]
        sc2 = mod_ref[0, 4:5, :]
        h2 = x1_ref[...] * (1.0 + sc2) + sh2
        for j in range(8):
            stage_ref[RB_SPITCH * j:RB_SPITCH * j + RB_SUB, :] = h2[:, LANES * j:LANES * (j + 1)]

        for t in range(RB_SUB):
            slab = stage_ref[pl.ds(t, 8, stride=RB_SPITCH), :]
            for a in range(2):
                buf_ref[pl.ds(offs_ref[0, a, t], 8, stride=RB_PITCH), :] = slab

    @pl.when(s == RB_NSUB)
    def _experts():
        def copies(e, slot):
            return (pltpu.make_async_copy(wg_hbm.at[e], wg_buf.at[slot], sem.at[slot, 0]),
                    pltpu.make_async_copy(wu_hbm.at[e], wu_buf.at[slot], sem.at[slot, 1]),
                    pltpu.make_async_copy(wd_hbm.at[e], wd_buf.at[slot], sem.at[slot, 2]))

        def run_expert(e, slot):
            lo_row = start_ref[e]
            hi_row = lo_row + cnt_ref[b, e]

            c_lo = lax.shift_right_logical(lo_row, RB_CHUNK_BITS)
            c_hi = lax.shift_right_logical(hi_row + (RB_CHUNK - 1), RB_CHUNK_BITS)
            row = lax.broadcasted_iota(jnp.int32, (RB_CHUNK, 1), 0)

            def load(c):
                base = pl.multiple_of(c * (8 * RB_PITCH), 8)
                return [buf_ref[pl.ds(base + RB_PITCH * j, RB_CHUNK), :] for j in range(8)]

            def store(c, tiles, y):
                base = pl.multiple_of(c * (8 * RB_PITCH), 8)
                mine = (row >= lo_row - c * RB_CHUNK) & (row < hi_row - c * RB_CHUNK)
                for j in range(8):
                    buf_ref[pl.ds(base + RB_PITCH * j, RB_CHUNK), :] = jnp.where(
                        mine, y[:, LANES * j:LANES * (j + 1)], tiles[j])

            def group(i, carry):
                cs = [c_lo + RB_GROUP * i]
                for k in range(1, RB_GROUP):
                    cs.append(jnp.where(cs[0] + k < c_hi, cs[0] + k, RB_NCHUNK + k))
                tiles = [load(c) for c in cs]
                x = jnp.concatenate([jnp.concatenate(t, axis=-1) for t in tiles], axis=0).astype(BF16)
                a = jnp.dot(x, wg_buf[slot], preferred_element_type=F32)
                u = jnp.dot(x, wu_buf[slot], preferred_element_type=F32)
                z = (_silu(a) * u).astype(BF16)
                y = jnp.dot(z, wd_buf[slot], preferred_element_type=F32)
                for k, c in enumerate(cs):
                    store(c, tiles[k], y[RB_CHUNK * k:RB_CHUNK * (k + 1)])
                return carry

            lax.fori_loop(0, lax.div(c_hi - c_lo + (RB_GROUP - 1), RB_GROUP), group, 0)

        for e in range(RB_WSLOTS - 1):
            for c in copies(e, e):
                c.start()

        def ring_body(i, carry):
            for k in range(RB_WSLOTS):
                e = RB_WSLOTS * i + k
                ahead = e + RB_WSLOTS - 1

                @pl.when(ahead < N_EXPERTS)
                def _():
                    for c in copies(ahead, (k + RB_WSLOTS - 1) % RB_WSLOTS):
                        c.start()
                for c in copies(e, k):
                    c.wait()
                run_expert(e, k)
            return carry
        lax.fori_loop(0, N_EXPERTS // RB_WSLOTS, ring_body, 0)

    @pl.when(s > RB_NSUB)
    def _combine():
        for t in range(RB_SUB):
            acc = None
            for a in range(2):
                term = wts_ref[0, a, t] * buf_ref[pl.ds(offs_ref[0, a, t], 8, stride=RB_PITCH), :]
                acc = term if acc is None else acc + term
            stage_ref[pl.ds(t, 8, stride=RB_SPITCH), :] = acc
        gt2 = mod_ref[0, 5:6, :]
        f = jnp.concatenate([stage_ref[RB_SPITCH * j:RB_SPITCH * j + RB_SUB, :] for j in range(8)], axis=-1)
        o_ref[...] = _layer_norm(DEEPNORM_ALPHA * x1_ref[...] + (1.0 + gt2) * f, ln2g_ref[...], ln2b_ref[...])


def _rb_retile(a):
    tiles, two, t = a.shape
    return a.reshape(tiles, two, t // RB_SUB, RB_SUB).transpose(0, 2, 1, 3).reshape(-1, two, RB_SUB)


def _rb_offsets(cnt, e12, rank12):
    start = jnp.cumsum(cnt, axis=-1) - cnt
    start_t = jnp.repeat(start, e12.shape[0] // cnt.shape[0], axis=0)[:, None, None, :]
    hit = e12[..., None] == jnp.arange(N_EXPERTS, dtype=jnp.int32)
    p = jnp.sum(jnp.where(hit, start_t, 0), axis=-1) + rank12
    return lax.shift_right_logical(p, RB_CHUNK_BITS) * (8 * RB_PITCH) + (p & (RB_CHUNK - 1))


def _rb_moe(x1, modp, cnt, offs, wts, wg_bf16, wu_bf16, wd_bf16, ln2g, ln2b):
    n, d = x1.shape
    bsz = n // SEQ
    nsteps = 2 * RB_NSUB + 1

    def sub_index(s):
        return jnp.where(s < RB_NSUB, s, jnp.where(s == RB_NSUB, RB_NSUB - 1, s - RB_NSUB - 1))

    def tile_map(b, s, cnt_r):
        return (b * RB_NSUB + sub_index(s), 0)

    def tile_map3(b, s, cnt_r):
        return (b * RB_NSUB + sub_index(s), 0, 0)

    def out_map(b, s, cnt_r):
        return (b * RB_NSUB + jnp.maximum(s - RB_NSUB - 1, 0), 0)

    const = lambda shp: pl.BlockSpec(shp, lambda b, s, cnt_r: tuple(0 for _ in shp))
    anyspec = pl.BlockSpec(memory_space=pl.ANY)
    grid_spec = pltpu.PrefetchScalarGridSpec(
        num_scalar_prefetch=1,
        grid=(bsz, nsteps),
        in_specs=[pl.BlockSpec((RB_SUB, d), tile_map),
                  pl.BlockSpec((1, 6, d), lambda b, s, cnt_r: (b, 0, 0)),
                  pl.BlockSpec((1, 2, RB_SUB), tile_map3, memory_space=pltpu.SMEM),
                  pl.BlockSpec((1, 2, RB_SUB), tile_map3, memory_space=pltpu.SMEM),
                  anyspec, anyspec, anyspec,
                  const((1, d)), const((1, d))],
        out_specs=pl.BlockSpec((RB_SUB, d), out_map),
        scratch_shapes=[pltpu.VMEM(((RB_NCHUNK + RB_GROUP) * 8 * RB_PITCH, LANES), F32),
                        pltpu.VMEM((8 * RB_SPITCH, LANES), F32),
                        pltpu.VMEM((RB_WSLOTS, d, D_EXPERT), BF16),
                        pltpu.VMEM((RB_WSLOTS, d, D_EXPERT), BF16),
                        pltpu.VMEM((RB_WSLOTS, D_EXPERT, d), BF16),
                        pltpu.SMEM((N_EXPERTS,), jnp.int32),
                        pltpu.SemaphoreType.DMA((RB_WSLOTS, 3))])
    return pl.pallas_call(
        _rb_kernel,
        grid_spec=grid_spec,
        out_shape=jax.ShapeDtypeStruct((n, d), F32),
        compiler_params=_cparams(("arbitrary", "arbitrary")),
        name="moe_routed_ln",
    )(cnt, x1, modp, offs, wts, wg_bf16, wu_bf16, wd_bf16, ln2g, ln2b)


def _sample_in_kernel(x_ref, sh1_ref, sc1_ref, win_ref, ctx_ref, h0_ref, convw_ref, convb_ref,
                      wlo_ref, whi_ref, bgate_ref, lam_ref,
                      ylru_ref, q_ref, k_ref, v_ref, cstate_ref, hnew_ref):
    h = x_ref[...] * (1.0 + sc1_ref[...]) + sh1_ref[...]
    z = _dot(h, win_ref[...], True)
    xb = z[:, :LRU_WIDTH]
    gate = z[:, LRU_WIDTH:2 * LRU_WIDTH]
    c0 = ctx_ref[:, 0, :]
    c1 = ctx_ref[:, 1, :]
    c2 = ctx_ref[:, 2, :]
    xc = (convb_ref[...] + convw_ref[0:1, :] * c0 + convw_ref[1:2, :] * c1
          + convw_ref[2:3, :] * c2 + convw_ref[3:4, :] * xb)
    cstate_ref[:, 0, :] = c1
    cstate_ref[:, 1, :] = c2
    cstate_ref[:, 2, :] = xb
    sp = _softplus(-lam_ref[...])
    a, bterm = _lru_gates(xc, wlo_ref[...], whi_ref[...], bgate_ref[...], sp, True)
    hn = a * h0_ref[...] + bterm
    hnew_ref[...] = hn
    ylru_ref[...] = hn * _gelu_tanh(gate)
    low = lax.broadcasted_iota(jnp.int32, (DEC_BATCH, LANES), 1) < HEAD_DIM
    for c in range(4):
        qc = z[:, 2 * LRU_WIDTH + LANES * c:2 * LRU_WIDTH + LANES * (c + 1)]
        q_ref[pl.ds(c, DEC_BATCH, stride=N_HEADS), :] = jnp.where(low, qc, 0.0)
        q_ref[pl.ds(c + 4, DEC_BATCH, stride=N_HEADS), :] = jnp.where(low, 0.0, qc)
    k_ref[...] = z[:, 2 * LRU_WIDTH + ATTN_WIDTH:2 * LRU_WIDTH + ATTN_WIDTH + KV_WIDTH]
    v_ref[...] = z[:, 2 * LRU_WIDTH + ATTN_WIDTH + KV_WIDTH:]


def _sample_in(x, sh1, sc1, w_in_p, ctx, h0, conv_w, conv_b, wlo, whi, bgate, lam):
    n = DEC_BATCH
    outs = [jax.ShapeDtypeStruct((n, LRU_WIDTH), F32),
            jax.ShapeDtypeStruct((n * N_HEADS, LANES), F32),
            jax.ShapeDtypeStruct((n, KV_WIDTH), F32),
            jax.ShapeDtypeStruct((n, KV_WIDTH), F32),
            jax.ShapeDtypeStruct((n, CONV_WIDTH - 1, LRU_WIDTH), F32),
            jax.ShapeDtypeStruct((n, LRU_WIDTH), F32)]
    return pl.pallas_call(
        _sample_in_kernel,
        out_shape=outs,
        compiler_params=pltpu.CompilerParams(vmem_limit_bytes=VMEM_LIMIT),
        name="sample_inproj_rglru",
    )(x, sh1, sc1, w_in_p, ctx, h0, conv_w, conv_b, wlo, whi, bgate, lam)


def _sample_attn_kernel(q_ref, kn_ref, vn_ref, ck_ref, cv_ref, sink_ref, y_ref, nk_ref, nv_ref, *, bb):
    rows = lax.broadcasted_iota(jnp.int32, (WINDOW, KV_WIDTH), 0)
    nh = N_HEADS
    q_all = q_ref[...].reshape(bb * nh, LANES)
    kcat = ck_ref[...].reshape(bb * WINDOW, KV_WIDTH)
    vcat = cv_ref[...].reshape(bb * WINDOW, KV_WIDTH)
    kn_rep = jnp.broadcast_to(kn_ref[...][:, None, :], (bb, nh, KV_WIDTH)).reshape(bb * nh, KV_WIDTH)
    vn_rep = jnp.broadcast_to(vn_ref[...][:, None, :], (bb, nh, KV_WIDTH)).reshape(bb * nh, KV_WIDTH)
    sink = jnp.concatenate([sink_ref[...]] * bb, axis=0)
    s_full = _dot_nt(q_all, kcat, True)
    s = jnp.concatenate([s_full[nh * b:nh * (b + 1), WINDOW * b:WINDOW * (b + 1)] for b in range(bb)],
                        axis=0) * ATTN_SCALE
    s_self = jnp.sum(q_all * kn_rep, axis=-1, keepdims=True) * ATTN_SCALE
    m = jnp.maximum(jnp.maximum(jnp.max(s, axis=-1, keepdims=True), s_self), sink)
    e = jnp.exp(s - m)
    e_self = jnp.exp(s_self - m)
    den = jnp.sum(e, axis=-1, keepdims=True) + e_self + jnp.exp(sink - m)
    inv = 1.0 / den
    p = e * inv
    zero = jnp.zeros((nh, WINDOW), F32)
    p_wide = jnp.concatenate(
        [jnp.concatenate([p[nh * b:nh * (b + 1)] if c == b else zero for c in range(bb)], axis=-1)
         for b in range(bb)], axis=0)
    o = _dot(p_wide, vcat, True) + (e_self * inv) * vn_rep
    y_ref[...] = o.reshape(bb, nh, LANES)
    for b in range(bb):
        nk_ref[b] = jnp.where(rows == WINDOW - 1, kn_ref[b:b + 1, :], pltpu.roll(ck_ref[b], WINDOW - 1, axis=0))
        nv_ref[b] = jnp.where(rows == WINDOW - 1, vn_ref[b:b + 1, :], pltpu.roll(cv_ref[b], WINDOW - 1, axis=0))


def _sample_attn(q3, kn, vn, cache_k, cache_v, sinks, bb=16):
    n = DEC_BATCH
    kern = functools.partial(_sample_attn_kernel, bb=bb)
    return pl.pallas_call(
        kern,
        grid=(n // bb,),
        in_specs=[pl.BlockSpec((bb, N_HEADS, LANES), lambda i: (i, 0, 0)),
                  pl.BlockSpec((bb, KV_WIDTH), lambda i: (i, 0)),
                  pl.BlockSpec((bb, KV_WIDTH), lambda i: (i, 0)),
                  pl.BlockSpec((bb, WINDOW, KV_WIDTH), lambda i: (i, 0, 0)),
                  pl.BlockSpec((bb, WINDOW, KV_WIDTH), lambda i: (i, 0, 0)),
                  pl.BlockSpec((N_HEADS, 1), lambda i: (0, 0))],
        out_specs=[pl.BlockSpec((bb, N_HEADS, LANES), lambda i: (i, 0, 0)),
                   pl.BlockSpec((bb, WINDOW, KV_WIDTH), lambda i: (i, 0, 0)),
                   pl.BlockSpec((bb, WINDOW, KV_WIDTH), lambda i: (i, 0, 0))],
        out_shape=[jax.ShapeDtypeStruct((n, N_HEADS, LANES), F32),
                   jax.ShapeDtypeStruct((n, WINDOW, KV_WIDTH), F32),
                   jax.ShapeDtypeStruct((n, WINDOW, KV_WIDTH), F32)],
        compiler_params=_cparams(("arbitrary",)),
        name="sample_cache_attention",
    )(q3, kn, vn, cache_k, cache_v, sinks.reshape(N_HEADS, 1))


def _sample_out_kernel(x_ref, ylru_ref, yatt_ref, sh2_ref, sc2_ref, gt1_ref, glru_ref, gattn_ref, wout_ref,
                       ln1g_ref, ln1b_ref, wr_ref, br_ref, x1_ref, comb_ref):
    low = lax.broadcasted_iota(jnp.int32, (DEC_BATCH, LANES), 1) < HEAD_DIM
    yatt = jnp.concatenate(
        [jnp.where(low, yatt_ref[pl.ds(c, DEC_BATCH, stride=N_HEADS), :],
                   yatt_ref[pl.ds(c + 4, DEC_BATCH, stride=N_HEADS), :]) for c in range(4)], axis=-1)
    x1, comb = _outproj_body(x_ref[...], ylru_ref[...], yatt, sh2_ref[...], sc2_ref[...], gt1_ref[...],
                             glru_ref[...], gattn_ref[...], wout_ref[...], ln1g_ref[...], ln1b_ref[...],
                             wr_ref[...], br_ref[...], True)
    x1_ref[...] = x1
    comb_ref[...] = comb


def _sample_out(x, ylru, yatt2d, sh2, sc2, gt1, glru, gattn, wout_p, ln1g, ln1b, wr, br):
    n = DEC_BATCH
    return pl.pallas_call(
        _sample_out_kernel,
        out_shape=[jax.ShapeDtypeStruct((n, D_MODEL), F32), jax.ShapeDtypeStruct((n, ROUTE_LANES), F32)],
        compiler_params=pltpu.CompilerParams(vmem_limit_bytes=VMEM_LIMIT),
        name="sample_outproj_ln_route",
    )(x, ylru, yatt2d, sh2, sc2, gt1, glru, gattn, wout_p, ln1g, ln1b, wr, br)


def _block_diag_halves(w_a, w_x):
    def bd(w4):
        eye = jnp.eye(4, dtype=w4.dtype)
        return (w4[:, :, None, :] * eye[:, None, :, None]).reshape(256, 256)
    lo = jnp.concatenate([bd(w_a[:4]), bd(w_x[:4])], axis=1)
    hi = jnp.concatenate([bd(w_a[4:]), bd(w_x[4:])], axis=1)
    return lo, hi


def kernel(x_prompt, x_sample, c_prompt, c_sample, state_conv, state_h, cache_k, cache_v, w_ada, b_ada, w_in,
           conv_w, conv_b, w_rg_a, b_rg_a, w_rg_x, b_rg_x, lru_lambda, sinks, g_lru, g_attn, w_out, ln1_g, ln1_b,
           w_group, b_group, w_router, b_router, w_gate, w_up, w_down, ln2_g, ln2_b):
    d = D_MODEL
    perm = jnp.asarray(HEAD_PERM)
    w_in0 = w_in[0]
    q0 = 2 * LRU_WIDTH
    w_in_p = jnp.concatenate([w_in0[:, :q0], w_in0[:, q0:q0 + ATTN_WIDTH][:, perm], w_in0[:, q0 + ATTN_WIDTH:]],
                             axis=1)
    w_out0 = w_out[0]
    w_out_p = jnp.concatenate([w_out0[:LRU_WIDTH], w_out0[LRU_WIDTH:][perm]], axis=0)
    g_attn_p = g_attn[0][perm].reshape(1, -1)
    glru = g_lru[0].reshape(1, -1)
    wlo, whi = _block_diag_halves(w_rg_a[0], w_rg_x[0])
    bgate = jnp.concatenate([b_rg_a[0].reshape(-1), b_rg_x[0].reshape(-1)]).reshape(1, -1)
    lam = lru_lambda[0].reshape(1, -1)
    convw = conv_w[0]
    convb = conv_b[0].reshape(1, -1)
    ln1g, ln1b = ln1_g[0].reshape(1, -1), ln1_b[0].reshape(1, -1)
    ln2g, ln2b = ln2_g[0].reshape(1, -1), ln2_b[0].reshape(1, -1)
    wr = jnp.concatenate([jnp.transpose(w_router[0], (1, 0, 2)).reshape(d, N_EXPERTS), w_group[0],
                          jnp.zeros((d, ROUTE_LANES - N_EXPERTS - N_GROUPS), F32)], axis=1)
    br = jnp.concatenate([b_router[0].reshape(-1), b_group[0],
                          jnp.zeros((ROUTE_LANES - N_EXPERTS - N_GROUPS,), F32)]).reshape(1, -1)
    sink_p = sinks[0]

    c_all = jnp.concatenate([c_prompt, jnp.zeros((8 - BATCH, d), F32), c_sample], axis=0)
    mod = _ada(c_all, w_ada[0], b_ada[0])
    modp = mod[:BATCH].reshape(BATCH, 6, d)
    mods = mod[8:]
    sh1_s, sc1_s, gt1_s, sh2_s, sc2_s, gt2_s = (mods[:, k * d:(k + 1) * d] for k in range(6))

    zlru, zqkv, kvlast = _inproj(x_prompt, modp, w_in_p.astype(BF16))
    ylru, cstate8, hlast8, wg_b, wu_b, wd_b = _lru(zlru, convw, convb, wlo.astype(BF16), whi.astype(BF16), bgate, lam,
                                                   w_gate[0], w_up[0], w_down[0])
    yatt = _attn(zqkv, sink_p)
    n_p = BATCH * SEQ
    x1_p, info, cntf = _outproj_prompt(x_prompt.reshape(n_p, d), ylru.reshape(n_p, LRU_WIDTH),
                                       yatt.reshape(n_p, ATTN_WIDTH), modp, glru, g_attn_p, w_out_p.astype(BF16),
                                       ln1g, ln1b, wr, br, tm=OUTPROJ_TILE)
    cnt = cntf[:, 0, :N_EXPERTS].astype(jnp.int32)
    offs = _rb_retile(_rb_offsets(cnt, info[:, 0:2].astype(jnp.int32), info[:, 4:6].astype(jnp.int32)))
    y_p = _rb_moe(x1_p, modp, cnt, offs, _rb_retile(info[:, 2:4]), wg_b, wu_b, wd_b, ln2g, ln2b)

    ylru_s, q2d, kn, vn, cstate_s, hnew_s = _sample_in(
        x_sample.reshape(DEC_BATCH, d), sh1_s, sc1_s, w_in_p, state_conv[0], state_h[0],
        convw, convb, wlo, whi, bgate, lam)
    yatt3, newk, newv = _sample_attn(q2d.reshape(DEC_BATCH, N_HEADS, LANES), kn, vn,
                                     cache_k[0].reshape(DEC_BATCH, WINDOW, KV_WIDTH),
                                     cache_v[0].reshape(DEC_BATCH, WINDOW, KV_WIDTH), sink_p)
    x1_s, comb_s = _sample_out(x_sample.reshape(DEC_BATCH, d), ylru_s, yatt3.reshape(DEC_BATCH * N_HEADS, LANES),
                               sh2_s, sc2_s, gt1_s, glru, g_attn_p, w_out_p, ln1g, ln1b, wr, br)
    y_s = _moe_dense(x1_s, comb_s, sh2_s, sc2_s, gt2_s, wg_b, wu_b, wd_b, ln2g, ln2b, DEC_BATCH)

    return (y_p.reshape(BATCH, SEQ, d),
            y_s.reshape(DEC_BATCH, 1, d),
            cstate8[:, 5:8][None],
            hlast8[:, 7][None],
            kvlast[:, :, :KV_WIDTH].reshape(1, BATCH, WINDOW, N_KV_HEADS, HEAD_DIM),
            kvlast[:, :, KV_WIDTH:].reshape(1, BATCH, WINDOW, N_KV_HEADS, HEAD_DIM),
            cstate_s[None],
            hnew_s[None],
            newk.reshape(1, DEC_BATCH, WINDOW, N_KV_HEADS, HEAD_DIM),
            newv.reshape(1, DEC_BATCH, WINDOW, N_KV_HEADS, HEAD_DIM))
```

```python
import functools

import jax
import jax.numpy as jnp
import numpy as np
from jax import lax
from jax.experimental import pallas as pl
from jax.experimental.pallas import tpu as pltpu

F32 = jnp.float32
BF16 = jnp.bfloat16
HIGHEST = lax.Precision.HIGHEST

D_MODEL = 1024
BATCH = 4
SEQ = 4096
DEC_BATCH = 128
LRU_WIDTH = 512
LRU_BLOCKS = 8
LRU_BLOCK = 64
CONV_WIDTH = 4
LRU_C = 8.0
N_HEADS = 8
N_KV_HEADS = 2
HEAD_DIM = 64
ATTN_WIDTH = 512
KV_WIDTH = 128
WINDOW = 128
IN_WIDTH = 2 * LRU_WIDTH + ATTN_WIDTH + 2 * KV_WIDTH
N_GROUPS = 4
EXPERTS_PER_GROUP = 8
N_EXPERTS = 32
D_EXPERT = 256
DEEPNORM_ALPHA = 2.0 ** 0.25
LN_EPS = 1e-5
RMS_EPS = 1e-6
ATTN_SCALE = HEAD_DIM ** -0.5

LANES = 128
ROUTE_LANES = 128
ROUTE_INFO = 40
VMEM_LIMIT = 56 * 1024 * 1024

HEAD_PERM = np.concatenate(
    [np.concatenate([np.arange(64 * c, 64 * c + 64), np.arange(64 * (c + 4), 64 * (c + 4) + 64)])
     for c in range(4)])


def _cparams(sem):
    return pltpu.CompilerParams(dimension_semantics=sem, vmem_limit_bytes=VMEM_LIMIT)


def _dot(a, b, exact):
    if exact:
        return jnp.dot(a, b, precision=HIGHEST, preferred_element_type=F32)
    return jnp.dot(a.astype(BF16), b.astype(BF16), preferred_element_type=F32)


def _dot_nt(a, b, exact):
    dn = (((1,), (1,)), ((), ()))
    if exact:
        return lax.dot_general(a, b, dn, precision=HIGHEST, preferred_element_type=F32)
    return lax.dot_general(a.astype(BF16), b.astype(BF16), dn, preferred_element_type=F32)


def _sigmoid(x):
    return 1.0 / (1.0 + jnp.exp(-x))


def _silu(x):
    return x * _sigmoid(x)


def _gelu_tanh(x):
    return 0.5 * x * (1.0 + jnp.tanh(np.sqrt(2.0 / np.pi).astype(np.float32) * (x + 0.044715 * (x * x * x))))


def _softplus(x):
    return jnp.maximum(x, 0.0) + jnp.log1p(jnp.exp(-jnp.abs(x)))


def _layer_norm(x, g, b):
    mu = jnp.mean(x, axis=-1, keepdims=True)
    xc = x - mu
    var = jnp.mean(xc * xc, axis=-1, keepdims=True)
    return xc * lax.rsqrt(var + LN_EPS) * g + b


def _rms_norm(x, g):
    return x * lax.rsqrt(jnp.mean(x * x, axis=-1, keepdims=True) + RMS_EPS) * g


def _ada_kernel(c_ref, w_ref, b_ref, o_ref):
    o_ref[...] = _dot(_silu(c_ref[...]), w_ref[...], True) + b_ref[...]


def _ada(c_all, w_ada, b_ada):
    rows = c_all.shape[0]
    bn = 1024
    return pl.pallas_call(
        _ada_kernel,
        grid=(6 * D_MODEL // bn,),
        in_specs=[pl.BlockSpec((rows, D_MODEL), lambda j: (0, 0)),
                  pl.BlockSpec((D_MODEL, bn), lambda j: (0, j)),
                  pl.BlockSpec((1, bn), lambda j: (0, j))],
        out_specs=pl.BlockSpec((rows, bn), lambda j: (0, j)),
        out_shape=jax.ShapeDtypeStruct((rows, 6 * D_MODEL), F32),
        compiler_params=_cparams(("arbitrary",)),
        name="ada_modulation",
    )(c_all, w_ada, b_ada.reshape(1, -1))


QKV_WIDTH = ATTN_WIDTH + 2 * KV_WIDTH


def _inproj_kernel(x_ref, mod_ref, w_ref, lru_ref, qkv_ref, kvlast_ref):
    sh1 = mod_ref[0, 0:1, :]
    sc1 = mod_ref[0, 1:2, :]
    h = x_ref[0] * (1.0 + sc1) + sh1
    z = _dot(h, w_ref[...], False)
    lru_ref[0] = z[:, :2 * LRU_WIDTH]
    qkv_ref[0] = z[:, 2 * LRU_WIDTH:].astype(BF16)
    kvlast_ref[0] = z[z.shape[0] - WINDOW:, 2 * LRU_WIDTH + ATTN_WIDTH:]


def _inproj(x, modp, w_in_bf16, tm=512):
    b, t, d = x.shape
    return pl.pallas_call(
        _inproj_kernel,
        grid=(b, t // tm),
        in_specs=[pl.BlockSpec((1, tm, d), lambda i, j: (i, j, 0)),
                  pl.BlockSpec((1, 6, d), lambda i, j: (i, 0, 0)),
                  pl.BlockSpec((d, IN_WIDTH), lambda i, j: (0, 0))],
        out_specs=[pl.BlockSpec((1, tm, 2 * LRU_WIDTH), lambda i, j: (i, j, 0)),
                   pl.BlockSpec((1, tm, QKV_WIDTH), lambda i, j: (i, j, 0)),
                   pl.BlockSpec((1, WINDOW, 2 * KV_WIDTH), lambda i, j: (i, 0, 0))],
        out_shape=[jax.ShapeDtypeStruct((b, t, 2 * LRU_WIDTH), F32),
                   jax.ShapeDtypeStruct((b, t, QKV_WIDTH), BF16),
                   jax.ShapeDtypeStruct((b, WINDOW, 2 * KV_WIDTH), F32)],
        compiler_params=_cparams(("arbitrary", "arbitrary")),
        name="prompt_inproj",
    )(x, modp, w_in_bf16)


def _lru_gates(xc, wlo, whi, bgate, sp_neg_lam, exact):
    g_lo = _dot(xc[:, :256], wlo, exact)
    g_hi = _dot(xc[:, 256:], whi, exact)
    ga = jnp.concatenate([g_lo[:, :256], g_hi[:, :256]], axis=-1) + bgate[:, :LRU_WIDTH]
    gx = jnp.concatenate([g_lo[:, 256:], g_hi[:, 256:]], axis=-1) + bgate[:, LRU_WIDTH:]
    r = _sigmoid(ga)
    i = _sigmoid(gx)
    log_a = -LRU_C * r * sp_neg_lam
    a = jnp.exp(log_a)
    one_minus_a2 = -jnp.tanh(log_a) * (a * a + 1.0) if exact else 1.0 - a * a
    bterm = jnp.sqrt(one_minus_a2) * (i * xc)
    return a, bterm


def _lru_kernel(z_ref, convw_ref, convb_ref, wlo_ref, whi_ref, bgate_ref, lam_ref, wg_ref, wu_ref, wd_ref,
                y_ref, cstate_ref, hlast_ref, wgub_ref, wdb_ref, tail_ref, carry_ref, *, tl):
    j = pl.program_id(1)

    @pl.when(j == 0)
    def _():
        tail_ref[...] = jnp.zeros_like(tail_ref)
        carry_ref[...] = jnp.zeros_like(carry_ref)

    wgub_ref[0, :, :D_EXPERT] = wg_ref[0].astype(BF16)
    wgub_ref[0, :, D_EXPERT:] = wu_ref[0].astype(BF16)
    wdb_ref[...] = wd_ref[...].astype(BF16)

    xb = z_ref[0, :, :LRU_WIDTH]
    gate = z_ref[0, :, LRU_WIDTH:]
    rows = lax.broadcasted_iota(jnp.int32, (tl, LRU_WIDTH), 0)

    xc = convb_ref[...] + convw_ref[3:4, :] * xb
    rows8 = lax.broadcasted_iota(jnp.int32, (8, LRU_WIDTH), 0)
    tail = tail_ref[...]
    for back in (1, 2, 3):
        rolled = pltpu.roll(xb, back, axis=0)
        top = jnp.where(rows8 >= back, rolled[:8], pltpu.roll(tail, back, axis=0))
        shifted = jnp.concatenate([top, rolled[8:]], axis=0)
        xc = xc + convw_ref[3 - back:4 - back, :] * shifted
    tail_ref[...] = xb[tl - 8:, :]
    cstate_ref[0] = xb[tl - 8:, :]

    sp = _softplus(-lam_ref[...])
    a, bterm = _lru_gates(xc, wlo_ref[...], whi_ref[...], bgate_ref[...], sp, False)

    s = 1
    while s < tl:
        if s < 8:
            a_sh = jnp.where(rows >= s, pltpu.roll(a, s, axis=0), 1.0)
            b_sh = jnp.where(rows >= s, pltpu.roll(bterm, s, axis=0), 0.0)
        else:
            a_sh = jnp.concatenate([jnp.ones((s, LRU_WIDTH), F32), a[:tl - s]], axis=0)
            b_sh = jnp.concatenate([jnp.zeros((s, LRU_WIDTH), F32), bterm[:tl - s]], axis=0)
        bterm = a * b_sh + bterm
        a = a * a_sh
        s *= 2
    h = a * carry_ref[7:8, :] + bterm
    carry_ref[...] = h[tl - 8:, :]
    hlast_ref[0] = h[tl - 8:, :]
    y_ref[0] = h * _gelu_tanh(gate)


def _lru(zin, conv_w, conv_b, wlo, whi, bgate, lam, w_gate, w_up, w_down, tl=512):
    b, t, _ = zin.shape
    steps = t // tl
    assert b * steps == N_EXPERTS
    d = D_MODEL
    kern = functools.partial(_lru_kernel, tl=tl)
    full = lambda shp: pl.BlockSpec(shp, lambda i, j: tuple(0 for _ in shp))
    per_step = lambda shp: pl.BlockSpec(shp, lambda i, j: (i * steps + j, 0, 0))
    return pl.pallas_call(
        kern,
        grid=(b, steps),
        in_specs=[pl.BlockSpec((1, tl, 2 * LRU_WIDTH), lambda i, j: (i, j, 0)),
                  full((CONV_WIDTH, LRU_WIDTH)), full((1, LRU_WIDTH)),
                  full((256, 512)), full((256, 512)), full((1, 2 * LRU_WIDTH)), full((1, LRU_WIDTH)),
                  per_step((1, d, D_EXPERT)), per_step((1, d, D_EXPERT)), per_step((1, D_EXPERT, d))],
        out_specs=[pl.BlockSpec((1, tl, LRU_WIDTH), lambda i, j: (i, j, 0)),
                   pl.BlockSpec((1, 8, LRU_WIDTH), lambda i, j: (i, 0, 0)),
                   pl.BlockSpec((1, 8, LRU_WIDTH), lambda i, j: (i, 0, 0)),
                   per_step((1, d, 2 * D_EXPERT)), per_step((1, D_EXPERT, d))],
        out_shape=[jax.ShapeDtypeStruct((b, t, LRU_WIDTH), F32),
                   jax.ShapeDtypeStruct((b, 8, LRU_WIDTH), F32),
                   jax.ShapeDtypeStruct((b, 8, LRU_WIDTH), F32),
                   jax.ShapeDtypeStruct((N_EXPERTS, d, 2 * D_EXPERT), BF16),
                   jax.ShapeDtypeStruct(w_down.shape, BF16)],
        scratch_shapes=[pltpu.VMEM((8, LRU_WIDTH), F32), pltpu.VMEM((8, LRU_WIDTH), F32)],
        compiler_params=_cparams(("arbitrary", "arbitrary")),
        name="prompt_rglru",
    )(zin, conv_w, conv_b, wlo, whi, bgate, lam, w_gate, w_up, w_down)


ATTN_BLOCKS = 16


def _attn_kernel(q_ref, k_ref, v_ref, sink_ref, o_ref, kprev_ref, vprev_ref):
    j = pl.program_id(1)

    @pl.when(j == 0)
    def _():
        kprev_ref[...] = jnp.zeros_like(kprev_ref)
        vprev_ref[...] = jnp.zeros_like(vprev_ref)

    blk = WINDOW
    lane = lax.broadcasted_iota(jnp.int32, (blk, LANES), 1)
    low = lane < HEAD_DIM
    qi = lax.broadcasted_iota(jnp.int32, (blk, 2 * blk), 0)
    sj = lax.broadcasted_iota(jnp.int32, (blk, 2 * blk), 1)
    rel = blk + qi - sj
    in_window = (rel >= 0) & (rel <= WINDOW)
    sink = sink_ref[...].reshape(N_HEADS, blk, 1)
    k_ext = jnp.concatenate([kprev_ref[...], k_ref[0]], axis=0)
    v_ext = jnp.concatenate([vprev_ref[...], v_ref[0]], axis=0)
    v_ext = jnp.concatenate([v_ext, jnp.ones_like(v_ext)], axis=-1)
    for n in range(ATTN_BLOCKS):
        q = q_ref[0, blk * n:blk * (n + 1), :]
        pieces = []
        for half in (0, 1):
            for c in range(4):
                qc = q[:, LANES * c:LANES * (c + 1)]
                pieces.append(jnp.where(low if half == 0 else ~low, qc, 0.0).astype(BF16))
        q8 = jnp.concatenate(pieces, axis=0)
        k_band = k_ext[blk * n:blk * (n + 2)]
        v_band = v_ext[blk * n:blk * (n + 2)]
        s = _dot_nt(q8, k_band, False) * ATTN_SCALE
        s = s.reshape(N_HEADS, blk, 2 * blk)
        valid = in_window & ((sj >= blk) | (j > 0)) if n == 0 else in_window
        s = jnp.where(valid[None], s, -jnp.inf)
        m = jnp.maximum(jnp.max(s, axis=-1, keepdims=True), sink)
        e = jnp.exp(s - m).reshape(N_HEADS * blk, 2 * blk)
        ov = _dot(e, v_band, False)
        den = ov[:, KV_WIDTH:] + jnp.exp(sink - m).reshape(N_HEADS * blk, 1)
        o8 = ov[:, :KV_WIDTH] * (1.0 / den)
        cols = []
        for c in range(4):
            cols.append(jnp.where(low, o8[blk * c:blk * (c + 1)], o8[blk * (c + 4):blk * (c + 5)]))
        o_ref[0, blk * n:blk * (n + 1), :] = jnp.concatenate(cols, axis=-1)
    kprev_ref[...] = k_ref[0, blk * (ATTN_BLOCKS - 1):, :]
    vprev_ref[...] = v_ref[0, blk * (ATTN_BLOCKS - 1):, :]


def _attn(qkv, sinks):
    b, t, _ = qkv.shape
    blk = WINDOW
    tq = blk * ATTN_BLOCKS
    sink_col = jnp.repeat(sinks.astype(F32), blk).reshape(N_HEADS * blk, 1)
    kcol = ATTN_WIDTH // KV_WIDTH
    return pl.pallas_call(
        _attn_kernel,
        grid=(b, t // tq),
        in_specs=[pl.BlockSpec((1, tq, ATTN_WIDTH), lambda i, j: (i, j, 0)),
                  pl.BlockSpec((1, tq, KV_WIDTH), lambda i, j: (i, j, kcol)),
                  pl.BlockSpec((1, tq, KV_WIDTH), lambda i, j: (i, j, kcol + 1)),
                  pl.BlockSpec((N_HEADS * blk, 1), lambda i, j: (0, 0))],
        out_specs=pl.BlockSpec((1, tq, ATTN_WIDTH), lambda i, j: (i, j, 0)),
        out_shape=jax.ShapeDtypeStruct((b, t, ATTN_WIDTH), F32),
        scratch_shapes=[pltpu.VMEM((blk, KV_WIDTH), BF16), pltpu.VMEM((blk, KV_WIDTH), BF16)],
        compiler_params=_cparams(("arbitrary", "arbitrary")),
        name="prompt_window_attention",
    )(qkv, qkv, qkv, sink_col)


def _dot_split3(a, b):
    a_hi = a.astype(BF16)
    b_hi = b.astype(BF16)
    a_lo = (a - a_hi.astype(F32)).astype(BF16)
    b_lo = (b - b_hi.astype(F32)).astype(BF16)
    return (jnp.dot(a_hi, b_hi, preferred_element_type=F32) + jnp.dot(a_hi, b_lo, preferred_element_type=F32)
            + jnp.dot(a_lo, b_hi, preferred_element_type=F32))


def _route(h2, wr, br, exact):
    t = h2.shape[0]
    logits = (_dot(h2, wr, True) if exact else _dot_split3(h2, wr)) + br
    lane = lax.broadcasted_iota(jnp.int32, (t, ROUTE_LANES), 1).astype(F32)
    neg = -jnp.inf
    big = float(ROUTE_LANES)
    is_g = (lane >= N_EXPERTS) & (lane < N_EXPERTS + N_GROUPS)
    lg = jnp.where(is_g, logits, neg)
    mg = jnp.max(lg, axis=-1, keepdims=True)
    eg = jnp.where(is_g, jnp.exp(lg - mg), 0.0)
    pg = eg / jnp.sum(eg, axis=-1, keepdims=True)
    g_val = jnp.max(pg, axis=-1, keepdims=True)
    g_lane = jnp.min(jnp.where((pg == g_val) & is_g, lane, big), axis=-1, keepdims=True)
    g_idx = g_lane - N_EXPERTS
    in_grp = (lane >= g_idx * EXPERTS_PER_GROUP) & (lane < (g_idx + 1.0) * EXPERTS_PER_GROUP)
    le = jnp.where(in_grp, logits, neg)
    me = jnp.max(le, axis=-1, keepdims=True)
    ee = jnp.where(in_grp, jnp.exp(le - me), 0.0)
    pe = ee / jnp.sum(ee, axis=-1, keepdims=True)
    v1 = jnp.max(pe, axis=-1, keepdims=True)
    l1 = jnp.min(jnp.where((pe == v1) & in_grp, lane, big), axis=-1, keepdims=True)
    rest = in_grp & (lane != l1)
    pe2 = jnp.where(rest, pe, -1.0)
    v2 = jnp.max(pe2, axis=-1, keepdims=True)
    l2 = jnp.min(jnp.where((pe2 == v2) & rest, lane, big), axis=-1, keepdims=True)
    tot = v1 + v2
    w1 = g_val * v1 / tot
    w2 = g_val * v2 / tot
    comb = jnp.where(lane == l1, w1, 0.0) + jnp.where(lane == l2, w2, 0.0)
    return (comb + jnp.where(lane == ROUTE_INFO, l1, 0.0) + jnp.where(lane == ROUTE_INFO + 1, l2, 0.0)
            + jnp.where(lane == ROUTE_INFO + 2, w1, 0.0) + jnp.where(lane == ROUTE_INFO + 3, w2, 0.0))


def _outproj_body(x, ylru, yatt, sh2, sc2, gt1, glru, gattn, wout, ln1g, ln1b, wr, br, exact):
    mixin = jnp.concatenate([_rms_norm(ylru, glru), _rms_norm(yatt, gattn)], axis=-1)
    mix = _dot(mixin, wout, exact)
    x1 = _layer_norm(DEEPNORM_ALPHA * x + (1.0 + gt1) * mix, ln1g, ln1b)
    h2 = x1 * (1.0 + sc2) + sh2
    return x1, _route(h2, wr, br, exact)


def _outproj_prompt_kernel(x_ref, ylru_ref, yatt_ref, mod_ref, glru_ref, gattn_ref, wout_ref,
                           ln1g_ref, ln1b_ref, wr_ref, br_ref, x1_ref, info_ref, cnt_ref, tri_ref, carry_ref,
                           *, tm, per_seq):
    i = pl.program_id(0)

    @pl.when(i == 0)
    def _():
        r = lax.broadcasted_iota(jnp.int32, (tm, tm), 0)
        c = lax.broadcasted_iota(jnp.int32, (tm, tm), 1)
        tri_ref[...] = jnp.where(c < r, 1.0, 0.0).astype(BF16)

    @pl.when(i % per_seq == 0)
    def _():
        carry_ref[...] = jnp.zeros_like(carry_ref)

    gt1 = mod_ref[0, 2:3, :]
    sh2 = mod_ref[0, 3:4, :]
    sc2 = mod_ref[0, 4:5, :]
    combs = []
    nsplit = 2
    for h in range(nsplit):
        rows = slice(h * (tm // nsplit), (h + 1) * (tm // nsplit))
        x1_h, comb_h = _outproj_body(x_ref[rows, :], ylru_ref[rows, :], yatt_ref[rows, :], sh2, sc2, gt1,
                                     glru_ref[...], gattn_ref[...], wout_ref[...], ln1g_ref[...], ln1b_ref[...],
                                     wr_ref[...], br_ref[...], False)
        x1_ref[rows, :] = x1_h
        combs.append(comb_h)
    comb = jnp.concatenate(combs, axis=0)
    lane = lax.broadcasted_iota(jnp.int32, (tm, ROUTE_LANES), 1).astype(F32)
    l1 = jnp.sum(jnp.where(lane == ROUTE_INFO, comb, 0.0), axis=-1, keepdims=True)
    l2 = jnp.sum(jnp.where(lane == ROUTE_INFO + 1, comb, 0.0), axis=-1, keepdims=True)
    o1 = lane == l1
    o2 = lane == l2
    onehot = jnp.where(o1 | o2, 1.0, 0.0)
    before = jnp.dot(tri_ref[...], onehot.astype(BF16), preferred_element_type=F32) + carry_ref[0:1, :]
    rank1 = jnp.sum(jnp.where(o1, before, 0.0), axis=-1, keepdims=True)
    rank2 = jnp.sum(jnp.where(o2, before, 0.0), axis=-1, keepdims=True)
    total = carry_ref[0:1, :] + jnp.sum(onehot, axis=0, keepdims=True)
    carry_ref[...] = jnp.broadcast_to(total, carry_ref.shape)
    cnt_ref[0] = jnp.broadcast_to(total, (8, ROUTE_LANES))
    info = (comb + jnp.where(lane == ROUTE_INFO + 4, rank1, 0.0) + jnp.where(lane == ROUTE_INFO + 5, rank2, 0.0))
    info_ref[0] = jnp.transpose(info)[ROUTE_INFO:ROUTE_INFO + 8, :]


OUTPROJ_TILE = 1024


def _outproj_prompt(x2d, ylru2d, yatt2d, modp, glru, gattn, wout_bf16, ln1g, ln1b, wr, br, tm=OUTPROJ_TILE):
    n, d = x2d.shape
    per_seq = SEQ // tm
    full = lambda shp: pl.BlockSpec(shp, lambda i: tuple(0 for _ in shp))
    kern = functools.partial(_outproj_prompt_kernel, tm=tm, per_seq=per_seq)
    return pl.pallas_call(
        kern,
        grid=(n // tm,),
        in_specs=[pl.BlockSpec((tm, d), lambda i: (i, 0)),
                  pl.BlockSpec((tm, LRU_WIDTH), lambda i: (i, 0)),
                  pl.BlockSpec((tm, ATTN_WIDTH), lambda i: (i, 0)),
                  pl.BlockSpec((1, 6, d), lambda i: (i // per_seq, 0, 0)),
                  full((1, LRU_WIDTH)), full((1, ATTN_WIDTH)), full((d, d)),
                  full((1, d)), full((1, d)), full((d, ROUTE_LANES)), full((1, ROUTE_LANES))],
        out_specs=[pl.BlockSpec((tm, d), lambda i: (i, 0)),
                   pl.BlockSpec((1, 8, tm), lambda i: (i, 0, 0)),
                   pl.BlockSpec((1, 8, ROUTE_LANES), lambda i: (i // per_seq, 0, 0))],
        out_shape=[jax.ShapeDtypeStruct((n, d), F32),
                   jax.ShapeDtypeStruct((n // tm, 8, tm), F32),
                   jax.ShapeDtypeStruct((n // SEQ, 8, ROUTE_LANES), F32)],
        scratch_shapes=[pltpu.VMEM((tm, tm), BF16), pltpu.VMEM((8, ROUTE_LANES), F32)],
        compiler_params=_cparams(("arbitrary",)),
        name="prompt_outproj_ln_route",
    )(x2d, ylru2d, yatt2d, modp, glru, gattn, wout_bf16, ln1g, ln1b, wr, br)


DENSE_EXPERTS_PER_STEP = 4


def _moe_kernel(x1_ref, comb_ref, sh2_ref, sc2_ref, gt2_ref, wgu_ref, wd_ref, ln2g_ref, ln2b_ref,
                o_ref, h2_ref, acc_ref):
    g = pl.program_id(1)

    @pl.when(g == 0)
    def _():
        h2_ref[...] = (x1_ref[...] * (1.0 + sc2_ref[...]) + sh2_ref[...]).astype(BF16)
        acc_ref[...] = jnp.zeros_like(acc_ref)

    h2 = h2_ref[...]
    comb = comb_ref[...]
    lane = lax.broadcasted_iota(jnp.int32, comb.shape, 1)
    part = None
    for k in range(DENSE_EXPERTS_PER_STEP):
        au = jnp.dot(h2, wgu_ref[k], preferred_element_type=F32)
        c_e = jnp.sum(jnp.where(lane == g * DENSE_EXPERTS_PER_STEP + k, comb, 0.0), axis=-1, keepdims=True)
        z = _silu(au[:, :D_EXPERT]) * au[:, D_EXPERT:] * c_e
        y = jnp.dot(z.astype(BF16), wd_ref[k], preferred_element_type=F32)
        part = y if part is None else part + y
    acc_ref[...] += part

    @pl.when(g == N_EXPERTS // DENSE_EXPERTS_PER_STEP - 1)
    def _():
        o_ref[...] = _layer_norm(DEEPNORM_ALPHA * x1_ref[...] + (1.0 + gt2_ref[...]) * acc_ref[...],
                                 ln2g_ref[...], ln2b_ref[...])


def _moe_dense(x1, comb, sh2, sc2, gt2, wgu_bf16, wd_bf16, ln2g, ln2b, tm):
    n, d = x1.shape
    eg = DENSE_EXPERTS_PER_STEP
    mspec = pl.BlockSpec((tm, d), lambda i, e: (i, 0))
    full = lambda shp: pl.BlockSpec(shp, lambda i, e: tuple(0 for _ in shp))
    return pl.pallas_call(
        _moe_kernel,
        grid=(n // tm, N_EXPERTS // eg),
        in_specs=[pl.BlockSpec((tm, d), lambda i, e: (i, 0)),
                  pl.BlockSpec((tm, ROUTE_LANES), lambda i, e: (i, 0)),
                  mspec, mspec, mspec,
                  pl.BlockSpec((eg, d, 2 * D_EXPERT), lambda i, e: (e, 0, 0)),
                  pl.BlockSpec((eg, D_EXPERT, d), lambda i, e: (e, 0, 0)),
                  full((1, d)), full((1, d))],
        out_specs=pl.BlockSpec((tm, d), lambda i, e: (i, 0)),
        out_shape=jax.ShapeDtypeStruct((n, d), F32),
        scratch_shapes=[pltpu.VMEM((tm, d), BF16), pltpu.VMEM((tm, d), F32)],
        compiler_params=_cparams(("arbitrary", "arbitrary")),
        name="moe_dense_ln",
    )(x1, comb, sh2, sc2, gt2, wgu_bf16, wd_bf16, ln2g, ln2b)


RB_SUB = 512
RB_NSUB = SEQ // RB_SUB
RB_CHUNK = 128
RB_CHUNK_BITS = 7
RB_NCHUNK = 2 * SEQ // RB_CHUNK
RB_PITCH = RB_CHUNK + 8
RB_SPITCH = RB_SUB + 8
RB_GROUP = 3
RB_WSLOTS = 4


def _rb_kernel(cnt_ref, x1_ref, mod_ref, offs_ref, wts_ref, wgu_hbm, wd_hbm, ln2g_ref, ln2b_ref,
               o_ref, buf_ref, stage_ref, wgu_buf, wd_buf, start_ref, sem):
    b = pl.program_id(0)
    s = pl.program_id(1)

    @pl.when(s == 0)
    def _starts():
        def body(e, run):
            start_ref[e] = run
            return run + cnt_ref[b, e]
        lax.fori_loop(0, N_EXPERTS, body, jnp.int32(0))
        buf_ref[RB_NCHUNK * 8 * RB_PITCH:(RB_NCHUNK + RB_GROUP) * 8 * RB_PITCH, :] = jnp.zeros(
            (RB_GROUP * 8 * RB_PITCH, LANES), F32)

    @pl.when(s < RB_NSUB)
    def _dispatch():
        sh2 = mod_ref[0, 3:4, :]
        sc2 = mod_ref[0, 4:5, :]
        h2 = x1_ref[...] * (1.0 + sc2) + sh2
        for j in range(8):
            stage_ref[RB_SPITCH * j:RB_SPITCH * j + RB_SUB, :] = h2[:, LANES * j:LANES * (j + 1)]

        for t in range(RB_SUB):
            slab = stage_ref[pl.ds(t, 8, stride=RB_SPITCH), :]
            for a in range(2):
                buf_ref[pl.ds(offs_ref[0, a, t], 8, stride=RB_PITCH), :] = slab

    @pl.when(s == RB_NSUB)
    def _experts():
        def copies(e, slot):
            return (pltpu.make_async_copy(wgu_hbm.at[e], wgu_buf.at[slot], sem.at[slot, 0]),
                    pltpu.make_async_copy(wd_hbm.at[e], wd_buf.at[slot], sem.at[slot, 1]))

        def run_expert(e, slot):
            lo_row = start_ref[e]
            hi_row = lo_row + cnt_ref[b, e]

            c_lo = lax.shift_right_logical(lo_row, RB_CHUNK_BITS)
            c_hi = lax.shift_right_logical(hi_row + (RB_CHUNK - 1), RB_CHUNK_BITS)
            row = lax.broadcasted_iota(jnp.int32, (RB_CHUNK, 1), 0)

            def load(c):
                base = pl.multiple_of(c * (8 * RB_PITCH), 8)
                return [buf_ref[pl.ds(base + RB_PITCH * j, RB_CHUNK), :] for j in range(8)]

            def store(c, tiles, y):
                base = pl.multiple_of(c * (8 * RB_PITCH), 8)
                mine = (row >= lo_row - c * RB_CHUNK) & (row < hi_row - c * RB_CHUNK)
                for j in range(8):
                    buf_ref[pl.ds(base + RB_PITCH * j, RB_CHUNK), :] = jnp.where(
                        mine, y[:, LANES * j:LANES * (j + 1)], tiles[j])

            def group(i, carry):
                cs = [c_lo + RB_GROUP * i]
                for k in range(1, RB_GROUP):
                    cs.append(jnp.where(cs[0] + k < c_hi, cs[0] + k, RB_NCHUNK + k))
                tiles = [load(c) for c in cs]
                x = jnp.concatenate([jnp.concatenate(t, axis=-1) for t in tiles], axis=0).astype(BF16)
                au = jnp.dot(x, wgu_buf[slot], preferred_element_type=F32)
                z = (_silu(au[:, :D_EXPERT]) * au[:, D_EXPERT:]).astype(BF16)
                y = jnp.dot(z, wd_buf[slot], preferred_element_type=F32)
                for k, c in enumerate(cs):
                    store(c, tiles[k], y[RB_CHUNK * k:RB_CHUNK * (k + 1)])
                return carry

            lax.fori_loop(0, lax.div(c_hi - c_lo + (RB_GROUP - 1), RB_GROUP), group, 0)

        for e in range(RB_WSLOTS - 1):
            for c in copies(e, e):
                c.start()

        def ring_body(i, carry):
            for k in range(RB_WSLOTS):
                e = RB_WSLOTS * i + k
                ahead = e + RB_WSLOTS - 1

                @pl.when(ahead < N_EXPERTS)
                def _():
                    for c in copies(ahead, (k + RB_WSLOTS - 1) % RB_WSLOTS):
                        c.start()
                for c in copies(e, k):
                    c.wait()
                run_expert(e, k)
            return carry
        lax.fori_loop(0, N_EXPERTS // RB_WSLOTS, ring_body, 0)

    @pl.when(s > RB_NSUB)
    def _combine():
        for t in range(RB_SUB):
            acc = None
            for a in range(2):
                term = wts_ref[0, a, t] * buf_ref[pl.ds(offs_ref[0, a, t], 8, stride=RB_PITCH), :]
                acc = term if acc is None else acc + term
            stage_ref[pl.ds(t, 8, stride=RB_SPITCH), :] = acc
        gt2 = mod_ref[0, 5:6, :]
        f = jnp.concatenate([stage_ref[RB_SPITCH * j:RB_SPITCH * j + RB_SUB, :] for j in range(8)], axis=-1)
        o_ref[...] = _layer_norm(DEEPNORM_ALPHA * x1_ref[...] + (1.0 + gt2) * f, ln2g_ref[...], ln2b_ref[...])


def _rb_retile(a):
    tiles, two, t = a.shape
    return a.reshape(tiles, two, t // RB_SUB, RB_SUB).transpose(0, 2, 1, 3).reshape(-1, two, RB_SUB)


def _rb_offsets(cnt, e12, rank12):
    start = jnp.cumsum(cnt, axis=-1) - cnt
    start_t = jnp.repeat(start, e12.shape[0] // cnt.shape[0], axis=0)[:, None, None, :]
    hit = e12[..., None] == jnp.arange(N_EXPERTS, dtype=jnp.int32)
    p = jnp.sum(jnp.where(hit, start_t, 0), axis=-1) + rank12
    return lax.shift_right_logical(p, RB_CHUNK_BITS) * (8 * RB_PITCH) + (p & (RB_CHUNK - 1))


def _rb_moe(x1, modp, cnt, offs, wts, wgu_bf16, wd_bf16, ln2g, ln2b):
    n, d = x1.shape
    bsz = n // SEQ
    nsteps = 2 * RB_NSUB + 1

    def sub_index(s):
        return jnp.where(s < RB_NSUB, s, jnp.where(s == RB_NSUB, RB_NSUB - 1, s - RB_NSUB - 1))

    def tile_map(b, s, cnt_r):
        return (b * RB_NSUB + sub_index(s), 0)

    def tile_map3(b, s, cnt_r):
        return (b * RB_NSUB + sub_index(s), 0, 0)

    def out_map(b, s, cnt_r):
        return (b * RB_NSUB + jnp.maximum(s - RB_NSUB - 1, 0), 0)

    const = lambda shp: pl.BlockSpec(shp, lambda b, s, cnt_r: tuple(0 for _ in shp))
    anyspec = pl.BlockSpec(memory_space=pl.ANY)
    grid_spec = pltpu.PrefetchScalarGridSpec(
        num_scalar_prefetch=1,
        grid=(bsz, nsteps),
        in_specs=[pl.BlockSpec((RB_SUB, d), tile_map),
                  pl.BlockSpec((1, 6, d), lambda b, s, cnt_r: (b, 0, 0)),
                  pl.BlockSpec((1, 2, RB_SUB), tile_map3, memory_space=pltpu.SMEM),
                  pl.BlockSpec((1, 2, RB_SUB), tile_map3, memory_space=pltpu.SMEM),
                  anyspec, anyspec,
                  const((1, d)), const((1, d))],
        out_specs=pl.BlockSpec((RB_SUB, d), out_map),
        scratch_shapes=[pltpu.VMEM(((RB_NCHUNK + RB_GROUP) * 8 * RB_PITCH, LANES), F32),
                        pltpu.VMEM((8 * RB_SPITCH, LANES), F32),
                        pltpu.VMEM((RB_WSLOTS, d, 2 * D_EXPERT), BF16),
                        pltpu.VMEM((RB_WSLOTS, D_EXPERT, d), BF16),
                        pltpu.SMEM((N_EXPERTS,), jnp.int32),
                        pltpu.SemaphoreType.DMA((RB_WSLOTS, 2))])
    return pl.pallas_call(
        _rb_kernel,
        grid_spec=grid_spec,
        out_shape=jax.ShapeDtypeStruct((n, d), F32),
        compiler_params=_cparams(("arbitrary", "arbitrary")),
        name="moe_routed_ln",
    )(cnt, x1, modp, offs, wts, wgu_bf16, wd_bf16, ln2g, ln2b)


def _sample_in_kernel(x_ref, sh1_ref, sc1_ref, win_ref, ctx_ref, h0_ref, convw_ref, convb_ref,
                      wlo_ref, whi_ref, bgate_ref, lam_ref,
                      ylru_ref, q_ref, k_ref, v_ref, cstate_ref, hnew_ref):
    h = x_ref[...] * (1.0 + sc1_ref[...]) + sh1_ref[...]
    z = _dot(h, win_ref[...], True)
    xb = z[:, :LRU_WIDTH]
    gate = z[:, LRU_WIDTH:2 * LRU_WIDTH]
    c0 = ctx_ref[:, 0, :]
    c1 = ctx_ref[:, 1, :]
    c2 = ctx_ref[:, 2, :]
    xc = (convb_ref[...] + convw_ref[0:1, :] * c0 + convw_ref[1:2, :] * c1
          + convw_ref[2:3, :] * c2 + convw_ref[3:4, :] * xb)
    cstate_ref[:, 0, :] = c1
    cstate_ref[:, 1, :] = c2
    cstate_ref[:, 2, :] = xb
    sp = _softplus(-lam_ref[...])
    a, bterm = _lru_gates(xc, wlo_ref[...], whi_ref[...], bgate_ref[...], sp, True)
    hn = a * h0_ref[...] + bterm
    hnew_ref[...] = hn
    ylru_ref[...] = hn * _gelu_tanh(gate)
    low = lax.broadcasted_iota(jnp.int32, (DEC_BATCH, LANES), 1) < HEAD_DIM
    for c in range(4):
        qc = z[:, 2 * LRU_WIDTH + LANES * c:2 * LRU_WIDTH + LANES * (c + 1)]
        q_ref[pl.ds(c, DEC_BATCH, stride=N_HEADS), :] = jnp.where(low, qc, 0.0)
        q_ref[pl.ds(c + 4, DEC_BATCH, stride=N_HEADS), :] = jnp.where(low, 0.0, qc)
    k_ref[...] = z[:, 2 * LRU_WIDTH + ATTN_WIDTH:2 * LRU_WIDTH + ATTN_WIDTH + KV_WIDTH]
    v_ref[...] = z[:, 2 * LRU_WIDTH + ATTN_WIDTH + KV_WIDTH:]


def _sample_in(x, sh1, sc1, w_in_p, ctx, h0, conv_w, conv_b, wlo, whi, bgate, lam):
    n = DEC_BATCH
    outs = [jax.ShapeDtypeStruct((n, LRU_WIDTH), F32),
            jax.ShapeDtypeStruct((n * N_HEADS, LANES), F32),
            jax.ShapeDtypeStruct((n, KV_WIDTH), F32),
            jax.ShapeDtypeStruct((n, KV_WIDTH), F32),
            jax.ShapeDtypeStruct((n, CONV_WIDTH - 1, LRU_WIDTH), F32),
            jax.ShapeDtypeStruct((n, LRU_WIDTH), F32)]
    return pl.pallas_call(
        _sample_in_kernel,
        out_shape=outs,
        compiler_params=pltpu.CompilerParams(vmem_limit_bytes=VMEM_LIMIT),
        name="sample_inproj_rglru",
    )(x, sh1, sc1, w_in_p, ctx, h0, conv_w, conv_b, wlo, whi, bgate, lam)


def _sample_attn_kernel(q_ref, kn_ref, vn_ref, ck_ref, cv_ref, sink_ref, y_ref, nk_ref, nv_ref, *, bb):
    rows = lax.broadcasted_iota(jnp.int32, (WINDOW, KV_WIDTH), 0)
    nh = N_HEADS
    q_all = q_ref[...].reshape(bb * nh, LANES)
    kcat = ck_ref[...].reshape(bb * WINDOW, KV_WIDTH)
    vcat = cv_ref[...].reshape(bb * WINDOW, KV_WIDTH)
    kn_rep = jnp.broadcast_to(kn_ref[...][:, None, :], (bb, nh, KV_WIDTH)).reshape(bb * nh, KV_WIDTH)
    vn_rep = jnp.broadcast_to(vn_ref[...][:, None, :], (bb, nh, KV_WIDTH)).reshape(bb * nh, KV_WIDTH)
    sink = jnp.concatenate([sink_ref[...]] * bb, axis=0)
    s_full = _dot_nt(q_all, kcat, True)
    s = jnp.concatenate([s_full[nh * b:nh * (b + 1), WINDOW * b:WINDOW * (b + 1)] for b in range(bb)],
                        axis=0) * ATTN_SCALE
    s_self = jnp.sum(q_all * kn_rep, axis=-1, keepdims=True) * ATTN_SCALE
    m = jnp.maximum(jnp.maximum(jnp.max(s, axis=-1, keepdims=True), s_self), sink)
    e = jnp.exp(s - m)
    e_self = jnp.exp(s_self - m)
    den = jnp.sum(e, axis=-1, keepdims=True) + e_self + jnp.exp(sink - m)
    inv = 1.0 / den
    p = e * inv
    zero = jnp.zeros((nh, WINDOW), F32)
    p_wide = jnp.concatenate(
        [jnp.concatenate([p[nh * b:nh * (b + 1)] if c == b else zero for c in range(bb)], axis=-1)
         for b in range(bb)], axis=0)
    o = _dot(p_wide, vcat, True) + (e_self * inv) * vn_rep
    y_ref[...] = o.reshape(bb, nh, LANES)
    for b in range(bb):
        nk_ref[b] = jnp.where(rows == WINDOW - 1, kn_ref[b:b + 1, :], pltpu.roll(ck_ref[b], WINDOW - 1, axis=0))
        nv_ref[b] = jnp.where(rows == WINDOW - 1, vn_ref[b:b + 1, :], pltpu.roll(cv_ref[b], WINDOW - 1, axis=0))


def _sample_attn(q3, kn, vn, cache_k, cache_v, sinks, bb=16):
    n = DEC_BATCH
    kern = functools.partial(_sample_attn_kernel, bb=bb)
    return pl.pallas_call(
        kern,
        grid=(n // bb,),
        in_specs=[pl.BlockSpec((bb, N_HEADS, LANES), lambda i: (i, 0, 0)),
                  pl.BlockSpec((bb, KV_WIDTH), lambda i: (i, 0)),
                  pl.BlockSpec((bb, KV_WIDTH), lambda i: (i, 0)),
                  pl.BlockSpec((bb, WINDOW, KV_WIDTH), lambda i: (i, 0, 0)),
                  pl.BlockSpec((bb, WINDOW, KV_WIDTH), lambda i: (i, 0, 0)),
                  pl.BlockSpec((N_HEADS, 1), lambda i: (0, 0))],
        out_specs=[pl.BlockSpec((bb, N_HEADS, LANES), lambda i: (i, 0, 0)),
                   pl.BlockSpec((bb, WINDOW, KV_WIDTH), lambda i: (i, 0, 0)),
                   pl.BlockSpec((bb, WINDOW, KV_WIDTH), lambda i: (i, 0, 0))],
        out_shape=[jax.ShapeDtypeStruct((n, N_HEADS, LANES), F32),
                   jax.ShapeDtypeStruct((n, WINDOW, KV_WIDTH), F32),
                   jax.ShapeDtypeStruct((n, WINDOW, KV_WIDTH), F32)],
        compiler_params=_cparams(("arbitrary",)),
        name="sample_cache_attention",
    )(q3, kn, vn, cache_k, cache_v, sinks.reshape(N_HEADS, 1))


def _sample_out_kernel(x_ref, ylru_ref, yatt_ref, sh2_ref, sc2_ref, gt1_ref, glru_ref, gattn_ref, wout_ref,
                       ln1g_ref, ln1b_ref, wr_ref, br_ref, x1_ref, comb_ref):
    low = lax.broadcasted_iota(jnp.int32, (DEC_BATCH, LANES), 1) < HEAD_DIM
    yatt = jnp.concatenate(
        [jnp.where(low, yatt_ref[pl.ds(c, DEC_BATCH, stride=N_HEADS), :],
                   yatt_ref[pl.ds(c + 4, DEC_BATCH, stride=N_HEADS), :]) for c in range(4)], axis=-1)
    x1, comb = _outproj_body(x_ref[...], ylru_ref[...], yatt, sh2_ref[...], sc2_ref[...], gt1_ref[...],
                             glru_ref[...], gattn_ref[...], wout_ref[...], ln1g_ref[...], ln1b_ref[...],
                             wr_ref[...], br_ref[...], True)
    x1_ref[...] = x1
    comb_ref[...] = comb


def _sample_out(x, ylru, yatt2d, sh2, sc2, gt1, glru, gattn, wout_p, ln1g, ln1b, wr, br):
    n = DEC_BATCH
    return pl.pallas_call(
        _sample_out_kernel,
        out_shape=[jax.ShapeDtypeStruct((n, D_MODEL), F32), jax.ShapeDtypeStruct((n, ROUTE_LANES), F32)],
        compiler_params=pltpu.CompilerParams(vmem_limit_bytes=VMEM_LIMIT),
        name="sample_outproj_ln_route",
    )(x, ylru, yatt2d, sh2, sc2, gt1, glru, gattn, wout_p, ln1g, ln1b, wr, br)


def _block_diag_halves(w_a, w_x):
    def bd(w4):
        eye = jnp.eye(4, dtype=w4.dtype)
        return (w4[:, :, None, :] * eye[:, None, :, None]).reshape(256, 256)
    lo = jnp.concatenate([bd(w_a[:4]), bd(w_x[:4])], axis=1)
    hi = jnp.concatenate([bd(w_a[4:]), bd(w_x[4:])], axis=1)
    return lo, hi


def kernel(x_prompt, x_sample, c_prompt, c_sample, state_conv, state_h, cache_k, cache_v, w_ada, b_ada, w_in,
           conv_w, conv_b, w_rg_a, b_rg_a, w_rg_x, b_rg_x, lru_lambda, sinks, g_lru, g_attn, w_out, ln1_g, ln1_b,
           w_group, b_group, w_router, b_router, w_gate, w_up, w_down, ln2_g, ln2_b):
    d = D_MODEL
    perm = jnp.asarray(HEAD_PERM)
    w_in0 = w_in[0]
    q0 = 2 * LRU_WIDTH
    w_in_p = jnp.concatenate([w_in0[:, :q0], w_in0[:, q0:q0 + ATTN_WIDTH][:, perm], w_in0[:, q0 + ATTN_WIDTH:]],
                             axis=1)
    w_out0 = w_out[0]
    w_out_p = jnp.concatenate([w_out0[:LRU_WIDTH], w_out0[LRU_WIDTH:][perm]], axis=0)
    g_attn_p = g_attn[0][perm].reshape(1, -1)
    glru = g_lru[0].reshape(1, -1)
    wlo, whi = _block_diag_halves(w_rg_a[0], w_rg_x[0])
    bgate = jnp.concatenate([b_rg_a[0].reshape(-1), b_rg_x[0].reshape(-1)]).reshape(1, -1)
    lam = lru_lambda[0].reshape(1, -1)
    convw = conv_w[0]
    convb = conv_b[0].reshape(1, -1)
    ln1g, ln1b = ln1_g[0].reshape(1, -1), ln1_b[0].reshape(1, -1)
    ln2g, ln2b = ln2_g[0].reshape(1, -1), ln2_b[0].reshape(1, -1)
    wr = jnp.concatenate([jnp.transpose(w_router[0], (1, 0, 2)).reshape(d, N_EXPERTS), w_group[0],
                          jnp.zeros((d, ROUTE_LANES - N_EXPERTS - N_GROUPS), F32)], axis=1)
    br = jnp.concatenate([b_router[0].reshape(-1), b_group[0],
                          jnp.zeros((ROUTE_LANES - N_EXPERTS - N_GROUPS,), F32)]).reshape(1, -1)
    sink_p = sinks[0]

    c_all = jnp.concatenate([c_prompt, jnp.zeros((8 - BATCH, d), F32), c_sample], axis=0)
    mod = _ada(c_all, w_ada[0], b_ada[0])
    modp = mod[:BATCH].reshape(BATCH, 6, d)
    mods = mod[8:]
    sh1_s, sc1_s, gt1_s, sh2_s, sc2_s, gt2_s = (mods[:, k * d:(k + 1) * d] for k in range(6))

    zlru, zqkv, kvlast = _inproj(x_prompt, modp, w_in_p.astype(BF16))
    ylru, cstate8, hlast8, wgu_b, wd_b = _lru(zlru, convw, convb, wlo.astype(BF16), whi.astype(BF16), bgate, lam,
                                                   w_gate[0], w_up[0], w_down[0])
    yatt = _attn(zqkv, sink_p)
    n_p = BATCH * SEQ
    x1_p, info, cntf = _outproj_prompt(x_prompt.reshape(n_p, d), ylru.reshape(n_p, LRU_WIDTH),
                                       yatt.reshape(n_p, ATTN_WIDTH), modp, glru, g_attn_p, w_out_p.astype(BF16),
                                       ln1g, ln1b, wr, br, tm=OUTPROJ_TILE)
    cnt = cntf[:, 0, :N_EXPERTS].astype(jnp.int32)
    offs = _rb_retile(_rb_offsets(cnt, info[:, 0:2].astype(jnp.int32), info[:, 4:6].astype(jnp.int32)))
    y_p = _rb_moe(x1_p, modp, cnt, offs, _rb_retile(info[:, 2:4]), wgu_b, wd_b, ln2g, ln2b)

    ylru_s, q2d, kn, vn, cstate_s, hnew_s = _sample_in(
        x_sample.reshape(DEC_BATCH, d), sh1_s, sc1_s, w_in_p, state_conv[0], state_h[0],
        convw, convb, wlo, whi, bgate, lam)
    yatt3, newk, newv = _sample_attn(q2d.reshape(DEC_BATCH, N_HEADS, LANES), kn, vn,
                                     cache_k[0].reshape(DEC_BATCH, WINDOW, KV_WIDTH),
                                     cache_v[0].reshape(DEC_BATCH, WINDOW, KV_WIDTH), sink_p)
    x1_s, comb_s = _sample_out(x_sample.reshape(DEC_BATCH, d), ylru_s, yatt3.reshape(DEC_BATCH * N_HEADS, LANES),
                               sh2_s, sc2_s, gt1_s, glru, g_attn_p, w_out_p, ln1g, ln1b, wr, br)
    y_s = _moe_dense(x1_s, comb_s, sh2_s, sc2_s, gt2_s, wgu_b, wd_b, ln2g, ln2b, DEC_BATCH)

    return (y_p.reshape(BATCH, SEQ, d),
            y_s.reshape(DEC_BATCH, 1, d),
            cstate8[:, 5:8][None],
            hlast8[:, 7][None],
            kvlast[:, :, :KV_WIDTH].reshape(1, BATCH, WINDOW, N_KV_HEADS, HEAD_DIM),
            kvlast[:, :, KV_WIDTH:].reshape(1, BATCH, WINDOW, N_KV_HEADS, HEAD_DIM),
            cstate_s[None],
            hnew_s[None],
            newk.reshape(1, DEC_BATCH, WINDOW, N_KV_HEADS, HEAD_DIM),
            newv.reshape(1, DEC_BATCH, WINDOW, N_KV_HEADS, HEAD_DIM))
```

```python
import functools

import jax
import jax.numpy as jnp
import numpy as np
from jax import lax
from jax.experimental import pallas as pl
from jax.experimental.pallas import tpu as pltpu

F32 = jnp.float32
BF16 = jnp.bfloat16
HIGHEST = lax.Precision.HIGHEST

D_MODEL = 1024
BATCH = 4
SEQ = 4096
DEC_BATCH = 128
LRU_WIDTH = 512
LRU_BLOCKS = 8
LRU_BLOCK = 64
CONV_WIDTH = 4
LRU_C = 8.0
N_HEADS = 8
N_KV_HEADS = 2
HEAD_DIM = 64
ATTN_WIDTH = 512
KV_WIDTH = 128
WINDOW = 128
IN_WIDTH = 2 * LRU_WIDTH + ATTN_WIDTH + 2 * KV_WIDTH
N_GROUPS = 4
EXPERTS_PER_GROUP = 8
N_EXPERTS = 32
D_EXPERT = 256
DEEPNORM_ALPHA = 2.0 ** 0.25
LN_EPS = 1e-5
RMS_EPS = 1e-6
ATTN_SCALE = HEAD_DIM ** -0.5

LANES = 128
ROUTE_LANES = 128
ROUTE_INFO = 40
VMEM_LIMIT = 56 * 1024 * 1024

HEAD_PERM = np.concatenate(
    [np.concatenate([np.arange(64 * c, 64 * c + 64), np.arange(64 * (c + 4), 64 * (c + 4) + 64)])
     for c in range(4)])


def _cparams(sem):
    return pltpu.CompilerParams(dimension_semantics=sem, vmem_limit_bytes=VMEM_LIMIT)


def _dot(a, b, exact):
    if exact:
        return jnp.dot(a, b, precision=HIGHEST, preferred_element_type=F32)
    return jnp.dot(a.astype(BF16), b.astype(BF16), preferred_element_type=F32)


def _dot_nt(a, b, exact):
    dn = (((1,), (1,)), ((), ()))
    if exact:
        return lax.dot_general(a, b, dn, precision=HIGHEST, preferred_element_type=F32)
    return lax.dot_general(a.astype(BF16), b.astype(BF16), dn, preferred_element_type=F32)


def _sigmoid(x):
    return 1.0 / (1.0 + jnp.exp(-x))


def _silu(x):
    return x * _sigmoid(x)


def _gelu_tanh(x):
    return 0.5 * x * (1.0 + jnp.tanh(np.sqrt(2.0 / np.pi).astype(np.float32) * (x + 0.044715 * (x * x * x))))


def _softplus(x):
    return jnp.maximum(x, 0.0) + jnp.log1p(jnp.exp(-jnp.abs(x)))


def _layer_norm(x, g, b):
    mu = jnp.mean(x, axis=-1, keepdims=True)
    xc = x - mu
    var = jnp.mean(xc * xc, axis=-1, keepdims=True)
    return xc * lax.rsqrt(var + LN_EPS) * g + b


def _rms_norm(x, g):
    return x * lax.rsqrt(jnp.mean(x * x, axis=-1, keepdims=True) + RMS_EPS) * g


def _ada_kernel(c_ref, w_ref, b_ref, o_ref):
    o_ref[...] = _dot(_silu(c_ref[...]), w_ref[...], True) + b_ref[...]


def _ada(c_all, w_ada, b_ada):
    rows = c_all.shape[0]
    bn = 1024
    return pl.pallas_call(
        _ada_kernel,
        grid=(6 * D_MODEL // bn,),
        in_specs=[pl.BlockSpec((rows, D_MODEL), lambda j: (0, 0)),
                  pl.BlockSpec((D_MODEL, bn), lambda j: (0, j)),
                  pl.BlockSpec((1, bn), lambda j: (0, j))],
        out_specs=pl.BlockSpec((rows, bn), lambda j: (0, j)),
        out_shape=jax.ShapeDtypeStruct((rows, 6 * D_MODEL), F32),
        compiler_params=_cparams(("arbitrary",)),
        name="ada_modulation",
    )(c_all, w_ada, b_ada.reshape(1, -1))


QKV_WIDTH = ATTN_WIDTH + 2 * KV_WIDTH


def _inproj_kernel(x_ref, mod_ref, w_ref, lru_ref, qkv_ref, kvlast_ref):
    sh1 = mod_ref[0, 0:1, :]
    sc1 = mod_ref[0, 1:2, :]
    h = x_ref[0] * (1.0 + sc1) + sh1
    z = _dot(h, w_ref[...], False)
    lru_ref[0] = z[:, :2 * LRU_WIDTH]
    qkv_ref[0] = z[:, 2 * LRU_WIDTH:].astype(BF16)
    kvlast_ref[0] = z[z.shape[0] - WINDOW:, 2 * LRU_WIDTH + ATTN_WIDTH:]


def _inproj(x, modp, w_in_bf16, tm=512):
    b, t, d = x.shape
    return pl.pallas_call(
        _inproj_kernel,
        grid=(b, t // tm),
        in_specs=[pl.BlockSpec((1, tm, d), lambda i, j: (i, j, 0)),
                  pl.BlockSpec((1, 6, d), lambda i, j: (i, 0, 0)),
                  pl.BlockSpec((d, IN_WIDTH), lambda i, j: (0, 0))],
        out_specs=[pl.BlockSpec((1, tm, 2 * LRU_WIDTH), lambda i, j: (i, j, 0)),
                   pl.BlockSpec((1, tm, QKV_WIDTH), lambda i, j: (i, j, 0)),
                   pl.BlockSpec((1, WINDOW, 2 * KV_WIDTH), lambda i, j: (i, 0, 0))],
        out_shape=[jax.ShapeDtypeStruct((b, t, 2 * LRU_WIDTH), F32),
                   jax.ShapeDtypeStruct((b, t, QKV_WIDTH), BF16),
                   jax.ShapeDtypeStruct((b, WINDOW, 2 * KV_WIDTH), F32)],
        compiler_params=_cparams(("arbitrary", "arbitrary")),
        name="prompt_inproj",
    )(x, modp, w_in_bf16)


def _lru_gates(xc, wlo, whi, bgate, sp_neg_lam, exact):
    g_lo = _dot(xc[:, :256], wlo, exact)
    g_hi = _dot(xc[:, 256:], whi, exact)
    ga = jnp.concatenate([g_lo[:, :256], g_hi[:, :256]], axis=-1) + bgate[:, :LRU_WIDTH]
    gx = jnp.concatenate([g_lo[:, 256:], g_hi[:, 256:]], axis=-1) + bgate[:, LRU_WIDTH:]
    r = _sigmoid(ga)
    i = _sigmoid(gx)
    log_a = -LRU_C * r * sp_neg_lam
    a = jnp.exp(log_a)
    one_minus_a2 = -jnp.tanh(log_a) * (a * a + 1.0) if exact else 1.0 - a * a
    bterm = jnp.sqrt(one_minus_a2) * (i * xc)
    return a, bterm


def _lru_kernel(z_ref, convw_ref, convb_ref, wlo_ref, whi_ref, bgate_ref, lam_ref, wg_ref, wu_ref, wd_ref,
                y_ref, cstate_ref, hlast_ref, wgub_ref, wdb_ref, tail_ref, carry_ref, *, tl):
    j = pl.program_id(1)

    @pl.when(j == 0)
    def _():
        tail_ref[...] = jnp.zeros_like(tail_ref)
        carry_ref[...] = jnp.zeros_like(carry_ref)

    wgub_ref[0, :, :D_EXPERT] = wg_ref[0].astype(BF16)
    wgub_ref[0, :, D_EXPERT:] = wu_ref[0].astype(BF16)
    wdb_ref[...] = wd_ref[...].astype(BF16)

    xb = z_ref[0, :, :LRU_WIDTH]
    gate = z_ref[0, :, LRU_WIDTH:]
    rows = lax.broadcasted_iota(jnp.int32, (tl, LRU_WIDTH), 0)

    xc = convb_ref[...] + convw_ref[3:4, :] * xb
    rows8 = lax.broadcasted_iota(jnp.int32, (8, LRU_WIDTH), 0)
    tail = tail_ref[...]
    for back in (1, 2, 3):
        rolled = pltpu.roll(xb, back, axis=0)
        top = jnp.where(rows8 >= back, rolled[:8], pltpu.roll(tail, back, axis=0))
        shifted = jnp.concatenate([top, rolled[8:]], axis=0)
        xc = xc + convw_ref[3 - back:4 - back, :] * shifted
    tail_ref[...] = xb[tl - 8:, :]
    cstate_ref[0] = xb[tl - 8:, :]

    sp = _softplus(-lam_ref[...])
    a, bterm = _lru_gates(xc, wlo_ref[...], whi_ref[...], bgate_ref[...], sp, False)

    s = 1
    while s < tl:
        if s < 8:
            a_sh = jnp.where(rows >= s, pltpu.roll(a, s, axis=0), 1.0)
            b_sh = jnp.where(rows >= s, pltpu.roll(bterm, s, axis=0), 0.0)
        else:
            a_sh = jnp.concatenate([jnp.ones((s, LRU_WIDTH), F32), a[:tl - s]], axis=0)
            b_sh = jnp.concatenate([jnp.zeros((s, LRU_WIDTH), F32), bterm[:tl - s]], axis=0)
        bterm = a * b_sh + bterm
        a = a * a_sh
        s *= 2
    h = a * carry_ref[7:8, :] + bterm
    carry_ref[...] = h[tl - 8:, :]
    hlast_ref[0] = h[tl - 8:, :]
    y_ref[0] = h * _gelu_tanh(gate)


def _lru(zin, conv_w, conv_b, wlo, whi, bgate, lam, w_gate, w_up, w_down, tl=512):
    b, t, _ = zin.shape
    steps = t // tl
    assert b * steps == N_EXPERTS
    d = D_MODEL
    kern = functools.partial(_lru_kernel, tl=tl)
    full = lambda shp: pl.BlockSpec(shp, lambda i, j: tuple(0 for _ in shp))
    per_step = lambda shp: pl.BlockSpec(shp, lambda i, j: (i * steps + j, 0, 0))
    return pl.pallas_call(
        kern,
        grid=(b, steps),
        in_specs=[pl.BlockSpec((1, tl, 2 * LRU_WIDTH), lambda i, j: (i, j, 0)),
                  full((CONV_WIDTH, LRU_WIDTH)), full((1, LRU_WIDTH)),
                  full((256, 512)), full((256, 512)), full((1, 2 * LRU_WIDTH)), full((1, LRU_WIDTH)),
                  per_step((1, d, D_EXPERT)), per_step((1, d, D_EXPERT)), per_step((1, D_EXPERT, d))],
        out_specs=[pl.BlockSpec((1, tl, LRU_WIDTH), lambda i, j: (i, j, 0)),
                   pl.BlockSpec((1, 8, LRU_WIDTH), lambda i, j: (i, 0, 0)),
                   pl.BlockSpec((1, 8, LRU_WIDTH), lambda i, j: (i, 0, 0)),
                   per_step((1, d, 2 * D_EXPERT)), per_step((1, D_EXPERT, d))],
        out_shape=[jax.ShapeDtypeStruct((b, t, LRU_WIDTH), F32),
                   jax.ShapeDtypeStruct((b, 8, LRU_WIDTH), F32),
                   jax.ShapeDtypeStruct((b, 8, LRU_WIDTH), F32),
                   jax.ShapeDtypeStruct((N_EXPERTS, d, 2 * D_EXPERT), BF16),
                   jax.ShapeDtypeStruct(w_down.shape, BF16)],
        scratch_shapes=[pltpu.VMEM((8, LRU_WIDTH), F32), pltpu.VMEM((8, LRU_WIDTH), F32)],
        compiler_params=_cparams(("arbitrary", "arbitrary")),
        name="prompt_rglru",
    )(zin, conv_w, conv_b, wlo, whi, bgate, lam, w_gate, w_up, w_down)


ATTN_BLOCKS = 16


def _attn_kernel(q_ref, k_ref, v_ref, sink_ref, o_ref, kprev_ref, vprev_ref):
    j = pl.program_id(1)

    @pl.when(j == 0)
    def _():
        kprev_ref[...] = jnp.zeros_like(kprev_ref)
        vprev_ref[...] = jnp.zeros_like(vprev_ref)

    blk = WINDOW
    lane = lax.broadcasted_iota(jnp.int32, (blk, LANES), 1)
    low = lane < HEAD_DIM
    qi = lax.broadcasted_iota(jnp.int32, (blk, 2 * blk), 0)
    sj = lax.broadcasted_iota(jnp.int32, (blk, 2 * blk), 1)
    rel = blk + qi - sj
    in_window = (rel >= 0) & (rel <= WINDOW)
    sink = sink_ref[...].reshape(N_HEADS, blk, 1)
    k_ext = jnp.concatenate([kprev_ref[...], k_ref[0]], axis=0)
    v_ext = jnp.concatenate([vprev_ref[...], v_ref[0]], axis=0)
    v_ext = jnp.concatenate([v_ext, jnp.ones_like(v_ext)], axis=-1)
    for n in range(ATTN_BLOCKS):
        q = q_ref[0, blk * n:blk * (n + 1), :]
        pieces = []
        for half in (0, 1):
            for c in range(4):
                qc = q[:, LANES * c:LANES * (c + 1)]
                pieces.append(jnp.where(low if half == 0 else ~low, qc, 0.0).astype(BF16))
        q8 = jnp.concatenate(pieces, axis=0)
        k_band = k_ext[blk * n:blk * (n + 2)]
        v_band = v_ext[blk * n:blk * (n + 2)]
        s = _dot_nt(q8, k_band, False) * ATTN_SCALE
        s = s.reshape(N_HEADS, blk, 2 * blk)
        valid = in_window & ((sj >= blk) | (j > 0)) if n == 0 else in_window
        s = jnp.where(valid[None], s, -jnp.inf)
        m = jnp.maximum(jnp.max(s, axis=-1, keepdims=True), sink)
        e = jnp.exp(s - m).reshape(N_HEADS * blk, 2 * blk)
        ov = _dot(e, v_band, False)
        den = ov[:, KV_WIDTH:] + jnp.exp(sink - m).reshape(N_HEADS * blk, 1)
        o8 = ov[:, :KV_WIDTH] * (1.0 / den)
        cols = []
        for c in range(4):
            cols.append(jnp.where(low, o8[blk * c:blk * (c + 1)], o8[blk * (c + 4):blk * (c + 5)]))
        o_ref[0, blk * n:blk * (n + 1), :] = jnp.concatenate(cols, axis=-1)
    kprev_ref[...] = k_ref[0, blk * (ATTN_BLOCKS - 1):, :]
    vprev_ref[...] = v_ref[0, blk * (ATTN_BLOCKS - 1):, :]


def _attn(qkv, sinks):
    b, t, _ = qkv.shape
    blk = WINDOW
    tq = blk * ATTN_BLOCKS
    sink_col = jnp.repeat(sinks.astype(F32), blk).reshape(N_HEADS * blk, 1)
    kcol = ATTN_WIDTH // KV_WIDTH
    return pl.pallas_call(
        _attn_kernel,
        grid=(b, t // tq),
        in_specs=[pl.BlockSpec((1, tq, ATTN_WIDTH), lambda i, j: (i, j, 0)),
                  pl.BlockSpec((1, tq, KV_WIDTH), lambda i, j: (i, j, kcol)),
                  pl.BlockSpec((1, tq, KV_WIDTH), lambda i, j: (i, j, kcol + 1)),
                  pl.BlockSpec((N_HEADS * blk, 1), lambda i, j: (0, 0))],
        out_specs=pl.BlockSpec((1, tq, ATTN_WIDTH), lambda i, j: (i, j, 0)),
        out_shape=jax.ShapeDtypeStruct((b, t, ATTN_WIDTH), F32),
        scratch_shapes=[pltpu.VMEM((blk, KV_WIDTH), BF16), pltpu.VMEM((blk, KV_WIDTH), BF16)],
        compiler_params=_cparams(("arbitrary", "arbitrary")),
        name="prompt_window_attention",
    )(qkv, qkv, qkv, sink_col)


def _route(h2, wr, br, exact):
    t = h2.shape[0]
    logits = _dot(h2, wr, exact) + br
    lane = lax.broadcasted_iota(jnp.int32, (t, ROUTE_LANES), 1).astype(F32)
    neg = -jnp.inf
    big = float(ROUTE_LANES)
    is_g = (lane >= N_EXPERTS) & (lane < N_EXPERTS + N_GROUPS)
    lg = jnp.where(is_g, logits, neg)
    mg = jnp.max(lg, axis=-1, keepdims=True)
    eg = jnp.where(is_g, jnp.exp(lg - mg), 0.0)
    pg = eg / jnp.sum(eg, axis=-1, keepdims=True)
    g_val = jnp.max(pg, axis=-1, keepdims=True)
    g_lane = jnp.min(jnp.where((pg == g_val) & is_g, lane, big), axis=-1, keepdims=True)
    g_idx = g_lane - N_EXPERTS
    in_grp = (lane >= g_idx * EXPERTS_PER_GROUP) & (lane < (g_idx + 1.0) * EXPERTS_PER_GROUP)
    le = jnp.where(in_grp, logits, neg)
    me = jnp.max(le, axis=-1, keepdims=True)
    ee = jnp.where(in_grp, jnp.exp(le - me), 0.0)
    pe = ee / jnp.sum(ee, axis=-1, keepdims=True)
    v1 = jnp.max(pe, axis=-1, keepdims=True)
    l1 = jnp.min(jnp.where((pe == v1) & in_grp, lane, big), axis=-1, keepdims=True)
    rest = in_grp & (lane != l1)
    pe2 = jnp.where(rest, pe, -1.0)
    v2 = jnp.max(pe2, axis=-1, keepdims=True)
    l2 = jnp.min(jnp.where((pe2 == v2) & rest, lane, big), axis=-1, keepdims=True)
    tot = v1 + v2
    w1 = g_val * v1 / tot
    w2 = g_val * v2 / tot
    comb = jnp.where(lane == l1, w1, 0.0) + jnp.where(lane == l2, w2, 0.0)
    return (comb + jnp.where(lane == ROUTE_INFO, l1, 0.0) + jnp.where(lane == ROUTE_INFO + 1, l2, 0.0)
            + jnp.where(lane == ROUTE_INFO + 2, w1, 0.0) + jnp.where(lane == ROUTE_INFO + 3, w2, 0.0))


def _outproj_body(x, ylru, yatt, sh2, sc2, gt1, glru, gattn, wout, ln1g, ln1b, wr, br, exact):
    mixin = jnp.concatenate([_rms_norm(ylru, glru), _rms_norm(yatt, gattn)], axis=-1)
    mix = _dot(mixin, wout, exact)
    x1 = _layer_norm(DEEPNORM_ALPHA * x + (1.0 + gt1) * mix, ln1g, ln1b)
    h2 = x1 * (1.0 + sc2) + sh2
    return x1, _route(h2, wr, br, exact)


def _outproj_prompt_kernel(x_ref, ylru_ref, yatt_ref, mod_ref, glru_ref, gattn_ref, wout_ref,
                           ln1g_ref, ln1b_ref, wr_ref, br_ref, x1_ref, info_ref, cnt_ref, tri_ref, carry_ref,
                           *, tm, per_seq):
    i = pl.program_id(0)

    @pl.when(i == 0)
    def _():
        r = lax.broadcasted_iota(jnp.int32, (tm, tm), 0)
        c = lax.broadcasted_iota(jnp.int32, (tm, tm), 1)
        tri_ref[...] = jnp.where(c < r, 1.0, 0.0).astype(BF16)

    @pl.when(i % per_seq == 0)
    def _():
        carry_ref[...] = jnp.zeros_like(carry_ref)

    gt1 = mod_ref[0, 2:3, :]
    sh2 = mod_ref[0, 3:4, :]
    sc2 = mod_ref[0, 4:5, :]
    combs = []
    nsplit = 2
    for h in range(nsplit):
        rows = slice(h * (tm // nsplit), (h + 1) * (tm // nsplit))
        x1_h, comb_h = _outproj_body(x_ref[rows, :], ylru_ref[rows, :], yatt_ref[rows, :], sh2, sc2, gt1,
                                     glru_ref[...], gattn_ref[...], wout_ref[...], ln1g_ref[...], ln1b_ref[...],
                                     wr_ref[...], br_ref[...], False)
        x1_ref[rows, :] = x1_h
        combs.append(comb_h)
    comb = jnp.concatenate(combs, axis=0)
    lane = lax.broadcasted_iota(jnp.int32, (tm, ROUTE_LANES), 1).astype(F32)
    l1 = jnp.sum(jnp.where(lane == ROUTE_INFO, comb, 0.0), axis=-1, keepdims=True)
    l2 = jnp.sum(jnp.where(lane == ROUTE_INFO + 1, comb, 0.0), axis=-1, keepdims=True)
    o1 = lane == l1
    o2 = lane == l2
    onehot = jnp.where(o1 | o2, 1.0, 0.0)
    before = jnp.dot(tri_ref[...], onehot.astype(BF16), preferred_element_type=F32) + carry_ref[0:1, :]
    rank1 = jnp.sum(jnp.where(o1, before, 0.0), axis=-1, keepdims=True)
    rank2 = jnp.sum(jnp.where(o2, before, 0.0), axis=-1, keepdims=True)
    total = carry_ref[0:1, :] + jnp.sum(onehot, axis=0, keepdims=True)
    carry_ref[...] = jnp.broadcast_to(total, carry_ref.shape)
    cnt_ref[0] = jnp.broadcast_to(total, (8, ROUTE_LANES))
    info = (comb + jnp.where(lane == ROUTE_INFO + 4, rank1, 0.0) + jnp.where(lane == ROUTE_INFO + 5, rank2, 0.0))
    info_ref[0] = jnp.transpose(info)[ROUTE_INFO:ROUTE_INFO + 8, :]


OUTPROJ_TILE = 1024


def _outproj_prompt(x2d, ylru2d, yatt2d, modp, glru, gattn, wout_bf16, ln1g, ln1b, wr, br, tm=OUTPROJ_TILE):
    n, d = x2d.shape
    per_seq = SEQ // tm
    full = lambda shp: pl.BlockSpec(shp, lambda i: tuple(0 for _ in shp))
    kern = functools.partial(_outproj_prompt_kernel, tm=tm, per_seq=per_seq)
    return pl.pallas_call(
        kern,
        grid=(n // tm,),
        in_specs=[pl.BlockSpec((tm, d), lambda i: (i, 0)),
                  pl.BlockSpec((tm, LRU_WIDTH), lambda i: (i, 0)),
                  pl.BlockSpec((tm, ATTN_WIDTH), lambda i: (i, 0)),
                  pl.BlockSpec((1, 6, d), lambda i: (i // per_seq, 0, 0)),
                  full((1, LRU_WIDTH)), full((1, ATTN_WIDTH)), full((d, d)),
                  full((1, d)), full((1, d)), full((d, ROUTE_LANES)), full((1, ROUTE_LANES))],
        out_specs=[pl.BlockSpec((tm, d), lambda i: (i, 0)),
                   pl.BlockSpec((1, 8, tm), lambda i: (i, 0, 0)),
                   pl.BlockSpec((1, 8, ROUTE_LANES), lambda i: (i // per_seq, 0, 0))],
        out_shape=[jax.ShapeDtypeStruct((n, d), F32),
                   jax.ShapeDtypeStruct((n // tm, 8, tm), F32),
                   jax.ShapeDtypeStruct((n // SEQ, 8, ROUTE_LANES), F32)],
        scratch_shapes=[pltpu.VMEM((tm, tm), BF16), pltpu.VMEM((8, ROUTE_LANES), F32)],
        compiler_params=_cparams(("arbitrary",)),
        name="prompt_outproj_ln_route",
    )(x2d, ylru2d, yatt2d, modp, glru, gattn, wout_bf16, ln1g, ln1b, wr, br)


DENSE_EXPERTS_PER_STEP = 4


def _moe_kernel(x1_ref, comb_ref, sh2_ref, sc2_ref, gt2_ref, wgu_ref, wd_ref, ln2g_ref, ln2b_ref,
                o_ref, h2_ref, acc_ref):
    g = pl.program_id(1)

    @pl.when(g == 0)
    def _():
        h2_ref[...] = (x1_ref[...] * (1.0 + sc2_ref[...]) + sh2_ref[...]).astype(BF16)
        acc_ref[...] = jnp.zeros_like(acc_ref)

    h2 = h2_ref[...]
    comb = comb_ref[...]
    lane = lax.broadcasted_iota(jnp.int32, comb.shape, 1)
    part = None
    for k in range(DENSE_EXPERTS_PER_STEP):
        au = jnp.dot(h2, wgu_ref[k], preferred_element_type=F32)
        c_e = jnp.sum(jnp.where(lane == g * DENSE_EXPERTS_PER_STEP + k, comb, 0.0), axis=-1, keepdims=True)
        z = _silu(au[:, :D_EXPERT]) * au[:, D_EXPERT:] * c_e
        y = jnp.dot(z.astype(BF16), wd_ref[k], preferred_element_type=F32)
        part = y if part is None else part + y
    acc_ref[...] += part

    @pl.when(g == N_EXPERTS // DENSE_EXPERTS_PER_STEP - 1)
    def _():
        o_ref[...] = _layer_norm(DEEPNORM_ALPHA * x1_ref[...] + (1.0 + gt2_ref[...]) * acc_ref[...],
                                 ln2g_ref[...], ln2b_ref[...])


def _moe_dense(x1, comb, mod, wgu_bf16, wd_bf16, ln2g, ln2b, tm):
    n, d = x1.shape
    eg = DENSE_EXPERTS_PER_STEP
    mspec = lambda k: pl.BlockSpec((tm, d), lambda i, e: (i, k))
    full = lambda shp: pl.BlockSpec(shp, lambda i, e: tuple(0 for _ in shp))
    return pl.pallas_call(
        _moe_kernel,
        grid=(n // tm, N_EXPERTS // eg),
        in_specs=[pl.BlockSpec((tm, d), lambda i, e: (i, 0)),
                  pl.BlockSpec((tm, ROUTE_LANES), lambda i, e: (i, 0)),
                  mspec(3), mspec(4), mspec(5),
                  pl.BlockSpec((eg, d, 2 * D_EXPERT), lambda i, e: (e, 0, 0)),
                  pl.BlockSpec((eg, D_EXPERT, d), lambda i, e: (e, 0, 0)),
                  full((1, d)), full((1, d))],
        out_specs=pl.BlockSpec((tm, d), lambda i, e: (i, 0)),
        out_shape=jax.ShapeDtypeStruct((n, d), F32),
        scratch_shapes=[pltpu.VMEM((tm, d), BF16), pltpu.VMEM((tm, d), F32)],
        compiler_params=_cparams(("arbitrary", "arbitrary")),
        name="moe_dense_ln",
    )(x1, comb, mod, mod, mod, wgu_bf16, wd_bf16, ln2g, ln2b)


RB_SUB = 512
RB_NSUB = SEQ // RB_SUB
RB_CHUNK = 128
RB_CHUNK_BITS = 7
RB_NCHUNK = 2 * SEQ // RB_CHUNK
RB_PITCH = RB_CHUNK + 8
RB_SPITCH = RB_SUB + 8
RB_GROUP = 3
RB_WSLOTS = 4


def _rb_kernel(cnt_ref, x1_ref, mod_ref, offs_ref, wts_ref, wgu_hbm, wd_hbm, ln2g_ref, ln2b_ref,
               o_ref, buf_ref, stage_ref, wgu_buf, wd_buf, start_ref, sem):
    b = pl.program_id(0)
    s = pl.program_id(1)

    @pl.when(s == 0)
    def _starts():
        def body(e, run):
            start_ref[e] = run
            return run + cnt_ref[b, e]
        lax.fori_loop(0, N_EXPERTS, body, jnp.int32(0))
        buf_ref[RB_NCHUNK * 8 * RB_PITCH:(RB_NCHUNK + RB_GROUP) * 8 * RB_PITCH, :] = jnp.zeros(
            (RB_GROUP * 8 * RB_PITCH, LANES), F32)

    @pl.when(s < RB_NSUB)
    def _dispatch():
        sh2 = mod_ref[0, 3:4, :]
        sc2 = mod_ref[0, 4:5, :]
        h2 = x1_ref[...] * (1.0 + sc2) + sh2
        for j in range(8):
            stage_ref[RB_SPITCH * j:RB_SPITCH * j + RB_SUB, :] = h2[:, LANES * j:LANES * (j + 1)]

        for t in range(RB_SUB):
            slab = stage_ref[pl.ds(t, 8, stride=RB_SPITCH), :]
            for a in range(2):
                buf_ref[pl.ds(offs_ref[0, a, t], 8, stride=RB_PITCH), :] = slab

    @pl.when(s == RB_NSUB)
    def _experts():
        def copies(e, slot):
            return (pltpu.make_async_copy(wgu_hbm.at[e], wgu_buf.at[slot], sem.at[slot, 0]),
                    pltpu.make_async_copy(wd_hbm.at[e], wd_buf.at[slot], sem.at[slot, 1]))

        def run_expert(e, slot):
            lo_row = start_ref[e]
            hi_row = lo_row + cnt_ref[b, e]

            c_lo = lax.shift_right_logical(lo_row, RB_CHUNK_BITS)
            c_hi = lax.shift_right_logical(hi_row + (RB_CHUNK - 1), RB_CHUNK_BITS)
            row = lax.broadcasted_iota(jnp.int32, (RB_CHUNK, 1), 0)

            def load(c):
                base = pl.multiple_of(c * (8 * RB_PITCH), 8)
                return [buf_ref[pl.ds(base + RB_PITCH * j, RB_CHUNK), :] for j in range(8)]

            def store(c, tiles, y):
                base = pl.multiple_of(c * (8 * RB_PITCH), 8)
                mine = (row >= lo_row - c * RB_CHUNK) & (row < hi_row - c * RB_CHUNK)
                for j in range(8):
                    buf_ref[pl.ds(base + RB_PITCH * j, RB_CHUNK), :] = jnp.where(
                        mine, y[:, LANES * j:LANES * (j + 1)], tiles[j])

            def group(i, carry):
                cs = [c_lo + RB_GROUP * i]
                for k in range(1, RB_GROUP):
                    cs.append(jnp.where(cs[0] + k < c_hi, cs[0] + k, RB_NCHUNK + k))
                tiles = [load(c) for c in cs]
                x = jnp.concatenate([jnp.concatenate(t, axis=-1) for t in tiles], axis=0).astype(BF16)
                au = jnp.dot(x, wgu_buf[slot], preferred_element_type=F32)
                z = (_silu(au[:, :D_EXPERT]) * au[:, D_EXPERT:]).astype(BF16)
                y = jnp.dot(z, wd_buf[slot], preferred_element_type=F32)
                for k, c in enumerate(cs):
                    store(c, tiles[k], y[RB_CHUNK * k:RB_CHUNK * (k + 1)])
                return carry

            lax.fori_loop(0, lax.div(c_hi - c_lo + (RB_GROUP - 1), RB_GROUP), group, 0)

        for e in range(RB_WSLOTS - 1):
            for c in copies(e, e):
                c.start()

        def ring_body(i, carry):
            for k in range(RB_WSLOTS):
                e = RB_WSLOTS * i + k
                ahead = e + RB_WSLOTS - 1

                @pl.when(ahead < N_EXPERTS)
                def _():
                    for c in copies(ahead, (k + RB_WSLOTS - 1) % RB_WSLOTS):
                        c.start()
                for c in copies(e, k):
                    c.wait()
                run_expert(e, k)
            return carry
        lax.fori_loop(0, N_EXPERTS // RB_WSLOTS, ring_body, 0)

    @pl.when(s > RB_NSUB)
    def _combine():
        for t in range(RB_SUB):
            acc = None
            for a in range(2):
                term = wts_ref[0, a, t] * buf_ref[pl.ds(offs_ref[0, a, t], 8, stride=RB_PITCH), :]
                acc = term if acc is None else acc + term
            stage_ref[pl.ds(t, 8, stride=RB_SPITCH), :] = acc
        gt2 = mod_ref[0, 5:6, :]
        f = jnp.concatenate([stage_ref[RB_SPITCH * j:RB_SPITCH * j + RB_SUB, :] for j in range(8)], axis=-1)
        o_ref[...] = _layer_norm(DEEPNORM_ALPHA * x1_ref[...] + (1.0 + gt2) * f, ln2g_ref[...], ln2b_ref[...])


def _rb_retile(a):
    tiles, two, t = a.shape
    return a.reshape(tiles, two, t // RB_SUB, RB_SUB).transpose(0, 2, 1, 3).reshape(-1, two, RB_SUB)


def _rb_offsets(cnt, e12, rank12):
    start = jnp.cumsum(cnt, axis=-1) - cnt
    start_t = jnp.repeat(start, e12.shape[0] // cnt.shape[0], axis=0)[:, None, None, :]
    hit = e12[..., None] == jnp.arange(N_EXPERTS, dtype=jnp.int32)
    p = jnp.sum(jnp.where(hit, start_t, 0), axis=-1) + rank12
    return lax.shift_right_logical(p, RB_CHUNK_BITS) * (8 * RB_PITCH) + (p & (RB_CHUNK - 1))


def _rb_moe(x1, modp, cnt, offs, wts, wgu_bf16, wd_bf16, ln2g, ln2b):
    n, d = x1.shape
    bsz = n // SEQ
    nsteps = 2 * RB_NSUB + 1

    def sub_index(s):
        return jnp.where(s < RB_NSUB, s, jnp.where(s == RB_NSUB, RB_NSUB - 1, s - RB_NSUB - 1))

    def tile_map(b, s, cnt_r):
        return (b * RB_NSUB + sub_index(s), 0)

    def tile_map3(b, s, cnt_r):
        return (b * RB_NSUB + sub_index(s), 0, 0)

    def out_map(b, s, cnt_r):
        return (b * RB_NSUB + jnp.maximum(s - RB_NSUB - 1, 0), 0)

    const = lambda shp: pl.BlockSpec(shp, lambda b, s, cnt_r: tuple(0 for _ in shp))
    anyspec = pl.BlockSpec(memory_space=pl.ANY)
    grid_spec = pltpu.PrefetchScalarGridSpec(
        num_scalar_prefetch=1,
        grid=(bsz, nsteps),
        in_specs=[pl.BlockSpec((RB_SUB, d), tile_map),
                  pl.BlockSpec((1, 6, d), lambda b, s, cnt_r: (b, 0, 0)),
                  pl.BlockSpec((1, 2, RB_SUB), tile_map3, memory_space=pltpu.SMEM),
                  pl.BlockSpec((1, 2, RB_SUB), tile_map3, memory_space=pltpu.SMEM),
                  anyspec, anyspec,
                  const((1, d)), const((1, d))],
        out_specs=pl.BlockSpec((RB_SUB, d), out_map),
        scratch_shapes=[pltpu.VMEM(((RB_NCHUNK + RB_GROUP) * 8 * RB_PITCH, LANES), F32),
                        pltpu.VMEM((8 * RB_SPITCH, LANES), F32),
                        pltpu.VMEM((RB_WSLOTS, d, 2 * D_EXPERT), BF16),
                        pltpu.VMEM((RB_WSLOTS, D_EXPERT, d), BF16),
                        pltpu.SMEM((N_EXPERTS,), jnp.int32),
                        pltpu.SemaphoreType.DMA((RB_WSLOTS, 2))])
    return pl.pallas_call(
        _rb_kernel,
        grid_spec=grid_spec,
        out_shape=jax.ShapeDtypeStruct((n, d), F32),
        compiler_params=_cparams(("arbitrary", "arbitrary")),
        name="moe_routed_ln",
    )(cnt, x1, modp, offs, wts, wgu_bf16, wd_bf16, ln2g, ln2b)


def _sample_in_kernel(x_ref, mod_ref, win_ref, ctx_ref, h0_ref, convw_ref, convb_ref,
                      wlo_ref, whi_ref, bgate_ref, lam_ref,
                      ylru_ref, q_ref, k_ref, v_ref, cstate_ref, hnew_ref):
    sh1 = mod_ref[0:DEC_BATCH, 0:D_MODEL]
    sc1 = mod_ref[0:DEC_BATCH, D_MODEL:2 * D_MODEL]
    h = x_ref[...] * (1.0 + sc1) + sh1
    z = _dot(h, win_ref[...], True)
    xb = z[:, :LRU_WIDTH]
    gate = z[:, LRU_WIDTH:2 * LRU_WIDTH]
    c0 = ctx_ref[:, 0, :]
    c1 = ctx_ref[:, 1, :]
    c2 = ctx_ref[:, 2, :]
    xc = (convb_ref[...] + convw_ref[0:1, :] * c0 + convw_ref[1:2, :] * c1
          + convw_ref[2:3, :] * c2 + convw_ref[3:4, :] * xb)
    cstate_ref[:, 0, :] = c1
    cstate_ref[:, 1, :] = c2
    cstate_ref[:, 2, :] = xb
    sp = _softplus(-lam_ref[...])
    a, bterm = _lru_gates(xc, wlo_ref[...], whi_ref[...], bgate_ref[...], sp, True)
    hn = a * h0_ref[...] + bterm
    hnew_ref[...] = hn
    ylru_ref[...] = hn * _gelu_tanh(gate)
    low = lax.broadcasted_iota(jnp.int32, (DEC_BATCH, LANES), 1) < HEAD_DIM
    for c in range(4):
        qc = z[:, 2 * LRU_WIDTH + LANES * c:2 * LRU_WIDTH + LANES * (c + 1)]
        q_ref[pl.ds(c, DEC_BATCH, stride=N_HEADS), :] = jnp.where(low, qc, 0.0)
        q_ref[pl.ds(c + 4, DEC_BATCH, stride=N_HEADS), :] = jnp.where(low, 0.0, qc)
    k_ref[...] = z[:, 2 * LRU_WIDTH + ATTN_WIDTH:2 * LRU_WIDTH + ATTN_WIDTH + KV_WIDTH]
    v_ref[...] = z[:, 2 * LRU_WIDTH + ATTN_WIDTH + KV_WIDTH:]


def _sample_in(x, mod, w_in_p, ctx, h0, conv_w, conv_b, wlo, whi, bgate, lam):
    n = DEC_BATCH
    outs = [jax.ShapeDtypeStruct((n, LRU_WIDTH), F32),
            jax.ShapeDtypeStruct((n * N_HEADS, LANES), F32),
            jax.ShapeDtypeStruct((n, KV_WIDTH), F32),
            jax.ShapeDtypeStruct((n, KV_WIDTH), F32),
            jax.ShapeDtypeStruct((n, CONV_WIDTH - 1, LRU_WIDTH), F32),
            jax.ShapeDtypeStruct((n, LRU_WIDTH), F32)]
    return pl.pallas_call(
        _sample_in_kernel,
        out_shape=outs,
        compiler_params=pltpu.CompilerParams(vmem_limit_bytes=VMEM_LIMIT),
        name="sample_inproj_rglru",
    )(x, mod, w_in_p, ctx, h0, conv_w, conv_b, wlo, whi, bgate, lam)


def _sample_attn_kernel(q_ref, kn_ref, vn_ref, ck_ref, cv_ref, sink_ref, y_ref, nk_ref, nv_ref, *, bb):
    rows = lax.broadcasted_iota(jnp.int32, (WINDOW, KV_WIDTH), 0)
    nh = N_HEADS
    q_all = q_ref[...].reshape(bb * nh, LANES)
    kcat = ck_ref[...].reshape(bb * WINDOW, KV_WIDTH)
    vcat = cv_ref[...].reshape(bb * WINDOW, KV_WIDTH)
    kn_rep = jnp.broadcast_to(kn_ref[...][:, None, :], (bb, nh, KV_WIDTH)).reshape(bb * nh, KV_WIDTH)
    vn_rep = jnp.broadcast_to(vn_ref[...][:, None, :], (bb, nh, KV_WIDTH)).reshape(bb * nh, KV_WIDTH)
    sink = jnp.concatenate([sink_ref[...]] * bb, axis=0)
    s_full = _dot_nt(q_all, kcat, True)
    s = jnp.concatenate([s_full[nh * b:nh * (b + 1), WINDOW * b:WINDOW * (b + 1)] for b in range(bb)],
                        axis=0) * ATTN_SCALE
    s_self = jnp.sum(q_all * kn_rep, axis=-1, keepdims=True) * ATTN_SCALE
    m = jnp.maximum(jnp.maximum(jnp.max(s, axis=-1, keepdims=True), s_self), sink)
    e = jnp.exp(s - m)
    e_self = jnp.exp(s_self - m)
    den = jnp.sum(e, axis=-1, keepdims=True) + e_self + jnp.exp(sink - m)
    inv = 1.0 / den
    p = e * inv
    zero = jnp.zeros((nh, WINDOW), F32)
    p_wide = jnp.concatenate(
        [jnp.concatenate([p[nh * b:nh * (b + 1)] if c == b else zero for c in range(bb)], axis=-1)
         for b in range(bb)], axis=0)
    o = _dot(p_wide, vcat, True) + (e_self * inv) * vn_rep
    y_ref[...] = o.reshape(bb, nh, LANES)
    for b in range(bb):
        nk_ref[b] = jnp.where(rows == WINDOW - 1, kn_ref[b:b + 1, :], pltpu.roll(ck_ref[b], WINDOW - 1, axis=0))
        nv_ref[b] = jnp.where(rows == WINDOW - 1, vn_ref[b:b + 1, :], pltpu.roll(cv_ref[b], WINDOW - 1, axis=0))


def _sample_attn(q3, kn, vn, cache_k, cache_v, sinks, bb=16):
    n = DEC_BATCH
    kern = functools.partial(_sample_attn_kernel, bb=bb)
    return pl.pallas_call(
        kern,
        grid=(n // bb,),
        in_specs=[pl.BlockSpec((bb, N_HEADS, LANES), lambda i: (i, 0, 0)),
                  pl.BlockSpec((bb, KV_WIDTH), lambda i: (i, 0)),
                  pl.BlockSpec((bb, KV_WIDTH), lambda i: (i, 0)),
                  pl.BlockSpec((bb, WINDOW, KV_WIDTH), lambda i: (i, 0, 0)),
                  pl.BlockSpec((bb, WINDOW, KV_WIDTH), lambda i: (i, 0, 0)),
                  pl.BlockSpec((N_HEADS, 1), lambda i: (0, 0))],
        out_specs=[pl.BlockSpec((bb, N_HEADS, LANES), lambda i: (i, 0, 0)),
                   pl.BlockSpec((bb, WINDOW, KV_WIDTH), lambda i: (i, 0, 0)),
                   pl.BlockSpec((bb, WINDOW, KV_WIDTH), lambda i: (i, 0, 0))],
        out_shape=[jax.ShapeDtypeStruct((n, N_HEADS, LANES), F32),
                   jax.ShapeDtypeStruct((n, WINDOW, KV_WIDTH), F32),
                   jax.ShapeDtypeStruct((n, WINDOW, KV_WIDTH), F32)],
        compiler_params=_cparams(("arbitrary",)),
        name="sample_cache_attention",
    )(q3, kn, vn, cache_k, cache_v, sinks.reshape(N_HEADS, 1))


def _sample_out_kernel(x_ref, ylru_ref, yatt_ref, mod_ref, glru_ref, gattn_ref, wout_ref,
                       ln1g_ref, ln1b_ref, wr_ref, br_ref, x1_ref, comb_ref):
    low = lax.broadcasted_iota(jnp.int32, (DEC_BATCH, LANES), 1) < HEAD_DIM
    yatt = jnp.concatenate(
        [jnp.where(low, yatt_ref[pl.ds(c, DEC_BATCH, stride=N_HEADS), :],
                   yatt_ref[pl.ds(c + 4, DEC_BATCH, stride=N_HEADS), :]) for c in range(4)], axis=-1)
    gt1 = mod_ref[0:DEC_BATCH, 2 * D_MODEL:3 * D_MODEL]
    sh2 = mod_ref[0:DEC_BATCH, 3 * D_MODEL:4 * D_MODEL]
    sc2 = mod_ref[0:DEC_BATCH, 4 * D_MODEL:5 * D_MODEL]
    x1, comb = _outproj_body(x_ref[...], ylru_ref[...], yatt, sh2, sc2, gt1,
                             glru_ref[...], gattn_ref[...], wout_ref[...], ln1g_ref[...], ln1b_ref[...],
                             wr_ref[...], br_ref[...], True)
    x1_ref[...] = x1
    comb_ref[...] = comb


def _sample_out(x, ylru, yatt2d, mod, glru, gattn, wout_p, ln1g, ln1b, wr, br):
    n = DEC_BATCH
    return pl.pallas_call(
        _sample_out_kernel,
        out_shape=[jax.ShapeDtypeStruct((n, D_MODEL), F32), jax.ShapeDtypeStruct((n, ROUTE_LANES), F32)],
        compiler_params=pltpu.CompilerParams(vmem_limit_bytes=VMEM_LIMIT),
        name="sample_outproj_ln_route",
    )(x, ylru, yatt2d, mod, glru, gattn, wout_p, ln1g, ln1b, wr, br)


def _block_diag_halves(w_a, w_x):
    def bd(w4):
        eye = jnp.eye(4, dtype=w4.dtype)
        return (w4[:, :, None, :] * eye[:, None, :, None]).reshape(256, 256)
    lo = jnp.concatenate([bd(w_a[:4]), bd(w_x[:4])], axis=1)
    hi = jnp.concatenate([bd(w_a[4:]), bd(w_x[4:])], axis=1)
    return lo, hi


def kernel(x_prompt, x_sample, c_prompt, c_sample, state_conv, state_h, cache_k, cache_v, w_ada, b_ada, w_in,
           conv_w, conv_b, w_rg_a, b_rg_a, w_rg_x, b_rg_x, lru_lambda, sinks, g_lru, g_attn, w_out, ln1_g, ln1_b,
           w_group, b_group, w_router, b_router, w_gate, w_up, w_down, ln2_g, ln2_b):
    d = D_MODEL
    perm = jnp.asarray(HEAD_PERM)
    w_in0 = w_in[0]
    q0 = 2 * LRU_WIDTH
    w_in_p = jnp.concatenate([w_in0[:, :q0], w_in0[:, q0:q0 + ATTN_WIDTH][:, perm], w_in0[:, q0 + ATTN_WIDTH:]],
                             axis=1)
    w_out0 = w_out[0]
    w_out_p = jnp.concatenate([w_out0[:LRU_WIDTH], w_out0[LRU_WIDTH:][perm]], axis=0)
    g_attn_p = g_attn[0][perm].reshape(1, -1)
    glru = g_lru[0].reshape(1, -1)
    wlo, whi = _block_diag_halves(w_rg_a[0], w_rg_x[0])
    bgate = jnp.concatenate([b_rg_a[0].reshape(-1), b_rg_x[0].reshape(-1)]).reshape(1, -1)
    lam = lru_lambda[0].reshape(1, -1)
    convw = conv_w[0]
    convb = conv_b[0].reshape(1, -1)
    ln1g, ln1b = ln1_g[0].reshape(1, -1), ln1_b[0].reshape(1, -1)
    ln2g, ln2b = ln2_g[0].reshape(1, -1), ln2_b[0].reshape(1, -1)
    wr = jnp.concatenate([jnp.transpose(w_router[0], (1, 0, 2)).reshape(d, N_EXPERTS), w_group[0],
                          jnp.zeros((d, ROUTE_LANES - N_EXPERTS - N_GROUPS), F32)], axis=1)
    br = jnp.concatenate([b_router[0].reshape(-1), b_group[0],
                          jnp.zeros((ROUTE_LANES - N_EXPERTS - N_GROUPS,), F32)]).reshape(1, -1)
    sink_p = sinks[0]

    c_all = jnp.concatenate([c_sample, c_prompt, jnp.zeros((8 - BATCH, d), F32)], axis=0)
    mod = _ada(c_all, w_ada[0], b_ada[0])
    modp = mod[DEC_BATCH:DEC_BATCH + BATCH].reshape(BATCH, 6, d)

    zlru, zqkv, kvlast = _inproj(x_prompt, modp, w_in_p.astype(BF16))
    ylru, cstate8, hlast8, wgu_b, wd_b = _lru(zlru, convw, convb, wlo.astype(BF16), whi.astype(BF16), bgate, lam,
                                                   w_gate[0], w_up[0], w_down[0])
    yatt = _attn(zqkv, sink_p)
    n_p = BATCH * SEQ
    x1_p, info, cntf = _outproj_prompt(x_prompt.reshape(n_p, d), ylru.reshape(n_p, LRU_WIDTH),
                                       yatt.reshape(n_p, ATTN_WIDTH), modp, glru, g_attn_p, w_out_p.astype(BF16),
                                       ln1g, ln1b, wr, br, tm=OUTPROJ_TILE)
    cnt = cntf[:, 0, :N_EXPERTS].astype(jnp.int32)
    offs = _rb_retile(_rb_offsets(cnt, info[:, 0:2].astype(jnp.int32), info[:, 4:6].astype(jnp.int32)))
    y_p = _rb_moe(x1_p, modp, cnt, offs, _rb_retile(info[:, 2:4]), wgu_b, wd_b, ln2g, ln2b)

    ylru_s, q2d, kn, vn, cstate_s, hnew_s = _sample_in(
        x_sample.reshape(DEC_BATCH, d), mod, w_in_p, state_conv[0], state_h[0],
        convw, convb, wlo, whi, bgate, lam)
    yatt3, newk, newv = _sample_attn(q2d.reshape(DEC_BATCH, N_HEADS, LANES), kn, vn,
                                     cache_k[0].reshape(DEC_BATCH, WINDOW, KV_WIDTH),
                                     cache_v[0].reshape(DEC_BATCH, WINDOW, KV_WIDTH), sink_p)
    x1_s, comb_s = _sample_out(x_sample.reshape(DEC_BATCH, d), ylru_s, yatt3.reshape(DEC_BATCH * N_HEADS, LANES),
                               mod, glru, g_attn_p, w_out_p, ln1g, ln1b, wr, br)
    y_s = _moe_dense(x1_s, comb_s, mod, wgu_b, wd_b, ln2g, ln2b, DEC_BATCH)

    return (y_p.reshape(BATCH, SEQ, d),
            y_s.reshape(DEC_BATCH, 1, d),
            cstate8[:, 5:8][None],
            hlast8[:, 7][None],
            kvlast[:, :, :KV_WIDTH].reshape(1, BATCH, WINDOW, N_KV_HEADS, HEAD_DIM),
            kvlast[:, :, KV_WIDTH:].reshape(1, BATCH, WINDOW, N_KV_HEADS, HEAD_DIM),
            cstate_s[None],
            hnew_s[None],
            newk.reshape(1, DEC_BATCH, WINDOW, N_KV_HEADS, HEAD_DIM),
            newv.reshape(1, DEC_BATCH, WINDOW, N_KV_HEADS, HEAD_DIM))
```

```python
import functools

import jax
import jax.numpy as jnp
import numpy as np
from jax import lax
from jax.experimental import pallas as pl
from jax.experimental.pallas import tpu as pltpu

F32 = jnp.float32
BF16 = jnp.bfloat16
HIGHEST = lax.Precision.HIGHEST

D_MODEL = 1024
BATCH = 4
SEQ = 4096
DEC_BATCH = 128
LRU_WIDTH = 512
LRU_BLOCKS = 8
LRU_BLOCK = 64
CONV_WIDTH = 4
LRU_C = 8.0
N_HEADS = 8
N_KV_HEADS = 2
HEAD_DIM = 64
ATTN_WIDTH = 512
KV_WIDTH = 128
WINDOW = 128
IN_WIDTH = 2 * LRU_WIDTH + ATTN_WIDTH + 2 * KV_WIDTH
N_GROUPS = 4
EXPERTS_PER_GROUP = 8
N_EXPERTS = 32
D_EXPERT = 256
DEEPNORM_ALPHA = 2.0 ** 0.25
LN_EPS = 1e-5
RMS_EPS = 1e-6
ATTN_SCALE = HEAD_DIM ** -0.5

LANES = 128
ROUTE_LANES = 128
ROUTE_INFO = 40
VMEM_LIMIT = 56 * 1024 * 1024

HEAD_PERM = np.concatenate(
    [np.concatenate([np.arange(64 * c, 64 * c + 64), np.arange(64 * (c + 4), 64 * (c + 4) + 64)])
     for c in range(4)])


def _cparams(sem):
    return pltpu.CompilerParams(dimension_semantics=sem, vmem_limit_bytes=VMEM_LIMIT)


def _dot(a, b, exact):
    if exact:
        return jnp.dot(a, b, precision=HIGHEST, preferred_element_type=F32)
    return jnp.dot(a.astype(BF16), b.astype(BF16), preferred_element_type=F32)


def _dot_nt(a, b, exact):
    dn = (((1,), (1,)), ((), ()))
    if exact:
        return lax.dot_general(a, b, dn, precision=HIGHEST, preferred_element_type=F32)
    return lax.dot_general(a.astype(BF16), b.astype(BF16), dn, preferred_element_type=F32)


def _sigmoid(x):
    return 1.0 / (1.0 + jnp.exp(-x))


def _silu(x):
    return x * _sigmoid(x)


def _gelu_tanh(x):
    return 0.5 * x * (1.0 + jnp.tanh(np.sqrt(2.0 / np.pi).astype(np.float32) * (x + 0.044715 * (x * x * x))))


def _softplus(x):
    return jnp.maximum(x, 0.0) + jnp.log1p(jnp.exp(-jnp.abs(x)))


def _layer_norm(x, g, b):
    mu = jnp.mean(x, axis=-1, keepdims=True)
    xc = x - mu
    var = jnp.mean(xc * xc, axis=-1, keepdims=True)
    return xc * lax.rsqrt(var + LN_EPS) * g + b


def _rms_norm(x, g):
    return x * lax.rsqrt(jnp.mean(x * x, axis=-1, keepdims=True) + RMS_EPS) * g


def _ada_kernel(c_ref, w_ref, b_ref, o_ref):
    o_ref[...] = _dot(_silu(c_ref[...]), w_ref[...], True) + b_ref[...]


def _ada(c_all, w_ada, b_ada):
    rows = c_all.shape[0]
    bn = 1024
    return pl.pallas_call(
        _ada_kernel,
        grid=(6 * D_MODEL // bn,),
        in_specs=[pl.BlockSpec((rows, D_MODEL), lambda j: (0, 0)),
                  pl.BlockSpec((D_MODEL, bn), lambda j: (0, j)),
                  pl.BlockSpec((1, bn), lambda j: (0, j))],
        out_specs=pl.BlockSpec((rows, bn), lambda j: (0, j)),
        out_shape=jax.ShapeDtypeStruct((rows, 6 * D_MODEL), F32),
        compiler_params=_cparams(("arbitrary",)),
        name="ada_modulation",
    )(c_all, w_ada, b_ada.reshape(1, -1))


QKV_WIDTH = ATTN_WIDTH + 2 * KV_WIDTH


def _inproj_kernel(x_ref, mod_ref, w_ref, lru_ref, qkv_ref, kvlast_ref):
    sh1 = mod_ref[0, 0:1, :]
    sc1 = mod_ref[0, 1:2, :]
    h = x_ref[0] * (1.0 + sc1) + sh1
    z = _dot(h, w_ref[...], False)
    lru_ref[0] = z[:, :2 * LRU_WIDTH]
    qkv_ref[0] = z[:, 2 * LRU_WIDTH:].astype(BF16)
    kvlast_ref[0] = z[z.shape[0] - WINDOW:, 2 * LRU_WIDTH + ATTN_WIDTH:]


def _inproj(x, modp, w_in_bf16, tm=512):
    b, t, d = x.shape
    return pl.pallas_call(
        _inproj_kernel,
        grid=(b, t // tm),
        in_specs=[pl.BlockSpec((1, tm, d), lambda i, j: (i, j, 0)),
                  pl.BlockSpec((1, 6, d), lambda i, j: (i, 0, 0)),
                  pl.BlockSpec((d, IN_WIDTH), lambda i, j: (0, 0))],
        out_specs=[pl.BlockSpec((1, tm, 2 * LRU_WIDTH), lambda i, j: (i, j, 0)),
                   pl.BlockSpec((1, tm, QKV_WIDTH), lambda i, j: (i, j, 0)),
                   pl.BlockSpec((1, WINDOW, 2 * KV_WIDTH), lambda i, j: (i, 0, 0))],
        out_shape=[jax.ShapeDtypeStruct((b, t, 2 * LRU_WIDTH), F32),
                   jax.ShapeDtypeStruct((b, t, QKV_WIDTH), BF16),
                   jax.ShapeDtypeStruct((b, WINDOW, 2 * KV_WIDTH), F32)],
        compiler_params=_cparams(("arbitrary", "arbitrary")),
        name="prompt_inproj",
    )(x, modp, w_in_bf16)


def _lru_gates(xc, wlo, whi, bgate, sp_neg_lam, exact):
    g_lo = _dot(xc[:, :256], wlo, exact)
    g_hi = _dot(xc[:, 256:], whi, exact)
    ga = jnp.concatenate([g_lo[:, :256], g_hi[:, :256]], axis=-1) + bgate[:, :LRU_WIDTH]
    gx = jnp.concatenate([g_lo[:, 256:], g_hi[:, 256:]], axis=-1) + bgate[:, LRU_WIDTH:]
    r = _sigmoid(ga)
    i = _sigmoid(gx)
    log_a = -LRU_C * r * sp_neg_lam
    a = jnp.exp(log_a)
    one_minus_a2 = -jnp.tanh(log_a) * (a * a + 1.0) if exact else 1.0 - a * a
    bterm = jnp.sqrt(one_minus_a2) * (i * xc)
    return a, bterm


def _lru_kernel(z_ref, convw_ref, convb_ref, wlo_ref, whi_ref, bgate_ref, lam_ref, wg_ref, wu_ref, wd_ref,
                y_ref, cstate_ref, hlast_ref, wgub_ref, wdb_ref, tail_ref, carry_ref, *, tl):
    j = pl.program_id(1)

    @pl.when(j == 0)
    def _():
        tail_ref[...] = jnp.zeros_like(tail_ref)
        carry_ref[...] = jnp.zeros_like(carry_ref)

    wgub_ref[0, :, :D_EXPERT] = wg_ref[0].astype(BF16)
    wgub_ref[0, :, D_EXPERT:] = wu_ref[0].astype(BF16)
    wdb_ref[...] = wd_ref[...].astype(BF16)

    xb = z_ref[0, :, :LRU_WIDTH]
    gate = z_ref[0, :, LRU_WIDTH:]
    xc = convb_ref[...] + convw_ref[3:4, :] * xb
    rows8 = lax.broadcasted_iota(jnp.int32, (8, LRU_WIDTH), 0)
    tail = tail_ref[...]
    for back in (1, 2, 3):
        rolled = pltpu.roll(xb, back, axis=0)
        top = jnp.where(rows8 >= back, rolled[:8], pltpu.roll(tail, back, axis=0))
        shifted = jnp.concatenate([top, rolled[8:]], axis=0)
        xc = xc + convw_ref[3 - back:4 - back, :] * shifted
    tail_ref[...] = xb[tl - 8:, :]
    cstate_ref[0] = xb[tl - 8:, :]

    sp = _softplus(-lam_ref[...])
    a, bterm = _lru_gates(xc, wlo_ref[...], whi_ref[...], bgate_ref[...], sp, False)

    groups = tl // 8
    a = a.reshape(groups, 8, LRU_WIDTH)
    bterm = bterm.reshape(groups, 8, LRU_WIDTH)
    r8 = lax.broadcasted_iota(jnp.int32, (groups, 8, LRU_WIDTH), 1)
    s = 1
    while s < 8:
        a_sh = jnp.where(r8 >= s, pltpu.roll(a, s, axis=1), 1.0)
        b_sh = jnp.where(r8 >= s, pltpu.roll(bterm, s, axis=1), 0.0)
        bterm = a * b_sh + bterm
        a = a * a_sh
        s *= 2
    a_tot = jnp.broadcast_to(a[:, 7:8, :], (groups, 8, LRU_WIDTH))
    b_tot = jnp.broadcast_to(bterm[:, 7:8, :], (groups, 8, LRU_WIDTH))
    h_in = jnp.broadcast_to(carry_ref[7:8, :], (8, LRU_WIDTH))
    pieces = []
    for g in range(groups):
        pieces.append(a[g] * h_in + bterm[g])
        h_in = a_tot[g] * h_in + b_tot[g]
    h = jnp.concatenate(pieces, axis=0)
    carry_ref[...] = h_in
    hlast_ref[0] = h_in
    y_ref[0] = h * _gelu_tanh(gate)


def _lru(zin, conv_w, conv_b, wlo, whi, bgate, lam, w_gate, w_up, w_down, tl=512):
    b, t, _ = zin.shape
    steps = t // tl
    assert b * steps == N_EXPERTS
    d = D_MODEL
    kern = functools.partial(_lru_kernel, tl=tl)
    full = lambda shp: pl.BlockSpec(shp, lambda i, j: tuple(0 for _ in shp))
    per_step = lambda shp: pl.BlockSpec(shp, lambda i, j: (i * steps + j, 0, 0))
    return pl.pallas_call(
        kern,
        grid=(b, steps),
        in_specs=[pl.BlockSpec((1, tl, 2 * LRU_WIDTH), lambda i, j: (i, j, 0)),
                  full((CONV_WIDTH, LRU_WIDTH)), full((1, LRU_WIDTH)),
                  full((256, 512)), full((256, 512)), full((1, 2 * LRU_WIDTH)), full((1, LRU_WIDTH)),
                  per_step((1, d, D_EXPERT)), per_step((1, d, D_EXPERT)), per_step((1, D_EXPERT, d))],
        out_specs=[pl.BlockSpec((1, tl, LRU_WIDTH), lambda i, j: (i, j, 0)),
                   pl.BlockSpec((1, 8, LRU_WIDTH), lambda i, j: (i, 0, 0)),
                   pl.BlockSpec((1, 8, LRU_WIDTH), lambda i, j: (i, 0, 0)),
                   per_step((1, d, 2 * D_EXPERT)), per_step((1, D_EXPERT, d))],
        out_shape=[jax.ShapeDtypeStruct((b, t, LRU_WIDTH), F32),
                   jax.ShapeDtypeStruct((b, 8, LRU_WIDTH), F32),
                   jax.ShapeDtypeStruct((b, 8, LRU_WIDTH), F32),
                   jax.ShapeDtypeStruct((N_EXPERTS, d, 2 * D_EXPERT), BF16),
                   jax.ShapeDtypeStruct(w_down.shape, BF16)],
        scratch_shapes=[pltpu.VMEM((8, LRU_WIDTH), F32), pltpu.VMEM((8, LRU_WIDTH), F32)],
        compiler_params=_cparams(("arbitrary", "arbitrary")),
        name="prompt_rglru",
    )(zin, conv_w, conv_b, wlo, whi, bgate, lam, w_gate, w_up, w_down)


ATTN_BLOCKS = 16


def _attn_kernel(q_ref, k_ref, v_ref, sink_ref, o_ref, kprev_ref, vprev_ref):
    j = pl.program_id(1)

    @pl.when(j == 0)
    def _():
        kprev_ref[...] = jnp.zeros_like(kprev_ref)
        vprev_ref[...] = jnp.zeros_like(vprev_ref)

    blk = WINDOW
    lane = lax.broadcasted_iota(jnp.int32, (blk, LANES), 1)
    low = lane < HEAD_DIM
    qi = lax.broadcasted_iota(jnp.int32, (blk, 2 * blk), 0)
    sj = lax.broadcasted_iota(jnp.int32, (blk, 2 * blk), 1)
    rel = blk + qi - sj
    in_window = (rel >= 0) & (rel <= WINDOW)
    sink = sink_ref[...].reshape(N_HEADS, blk, 1)
    k_ext = jnp.concatenate([kprev_ref[...], k_ref[0]], axis=0)
    v_ext = jnp.concatenate([vprev_ref[...], v_ref[0]], axis=0)
    v_ext = jnp.concatenate([v_ext, jnp.ones_like(v_ext)], axis=-1)
    for n in range(ATTN_BLOCKS):
        q = q_ref[0, blk * n:blk * (n + 1), :]
        pieces = []
        for half in (0, 1):
            for c in range(4):
                qc = q[:, LANES * c:LANES * (c + 1)]
                pieces.append(jnp.where(low if half == 0 else ~low, qc, 0.0).astype(BF16))
        q8 = jnp.concatenate(pieces, axis=0)
        k_band = k_ext[blk * n:blk * (n + 2)]
        v_band = v_ext[blk * n:blk * (n + 2)]
        s = _dot_nt(q8, k_band, False) * ATTN_SCALE
        s = s.reshape(N_HEADS, blk, 2 * blk)
        valid = in_window & ((sj >= blk) | (j > 0)) if n == 0 else in_window
        s = jnp.where(valid[None], s, -jnp.inf)
        m = jnp.maximum(jnp.max(s, axis=-1, keepdims=True), sink)
        e = jnp.exp(s - m).reshape(N_HEADS * blk, 2 * blk)
        ov = _dot(e, v_band, False)
        den = ov[:, KV_WIDTH:] + jnp.exp(sink - m).reshape(N_HEADS * blk, 1)
        o8 = ov[:, :KV_WIDTH] * (1.0 / den)
        cols = []
        for c in range(4):
            cols.append(jnp.where(low, o8[blk * c:blk * (c + 1)], o8[blk * (c + 4):blk * (c + 5)]))
        o_ref[0, blk * n:blk * (n + 1), :] = jnp.concatenate(cols, axis=-1)
    kprev_ref[...] = k_ref[0, blk * (ATTN_BLOCKS - 1):, :]
    vprev_ref[...] = v_ref[0, blk * (ATTN_BLOCKS - 1):, :]


def _attn(qkv, sinks):
    b, t, _ = qkv.shape
    blk = WINDOW
    tq = blk * ATTN_BLOCKS
    sink_col = jnp.repeat(sinks.astype(F32), blk).reshape(N_HEADS * blk, 1)
    kcol = ATTN_WIDTH // KV_WIDTH
    return pl.pallas_call(
        _attn_kernel,
        grid=(b, t // tq),
        in_specs=[pl.BlockSpec((1, tq, ATTN_WIDTH), lambda i, j: (i, j, 0)),
                  pl.BlockSpec((1, tq, KV_WIDTH), lambda i, j: (i, j, kcol)),
                  pl.BlockSpec((1, tq, KV_WIDTH), lambda i, j: (i, j, kcol + 1)),
                  pl.BlockSpec((N_HEADS * blk, 1), lambda i, j: (0, 0))],
        out_specs=pl.BlockSpec((1, tq, ATTN_WIDTH), lambda i, j: (i, j, 0)),
        out_shape=jax.ShapeDtypeStruct((b, t, ATTN_WIDTH), F32),
        scratch_shapes=[pltpu.VMEM((blk, KV_WIDTH), BF16), pltpu.VMEM((blk, KV_WIDTH), BF16)],
        compiler_params=_cparams(("arbitrary", "arbitrary")),
        name="prompt_window_attention",
    )(qkv, qkv, qkv, sink_col)


def _route(h2, wr, br, exact):
    t = h2.shape[0]
    logits = _dot(h2, wr, exact) + br
    lane = lax.broadcasted_iota(jnp.int32, (t, ROUTE_LANES), 1).astype(F32)
    neg = -jnp.inf
    big = float(ROUTE_LANES)
    is_g = (lane >= N_EXPERTS) & (lane < N_EXPERTS + N_GROUPS)
    lg = jnp.where(is_g, logits, neg)
    mg = jnp.max(lg, axis=-1, keepdims=True)
    eg = jnp.where(is_g, jnp.exp(lg - mg), 0.0)
    pg = eg / jnp.sum(eg, axis=-1, keepdims=True)
    g_val = jnp.max(pg, axis=-1, keepdims=True)
    g_lane = jnp.min(jnp.where((pg == g_val) & is_g, lane, big), axis=-1, keepdims=True)
    g_idx = g_lane - N_EXPERTS
    in_grp = (lane >= g_idx * EXPERTS_PER_GROUP) & (lane < (g_idx + 1.0) * EXPERTS_PER_GROUP)
    le = jnp.where(in_grp, logits, neg)
    me = jnp.max(le, axis=-1, keepdims=True)
    ee = jnp.where(in_grp, jnp.exp(le - me), 0.0)
    pe = ee / jnp.sum(ee, axis=-1, keepdims=True)
    v1 = jnp.max(pe, axis=-1, keepdims=True)
    l1 = jnp.min(jnp.where((pe == v1) & in_grp, lane, big), axis=-1, keepdims=True)
    rest = in_grp & (lane != l1)
    pe2 = jnp.where(rest, pe, -1.0)
    v2 = jnp.max(pe2, axis=-1, keepdims=True)
    l2 = jnp.min(jnp.where((pe2 == v2) & rest, lane, big), axis=-1, keepdims=True)
    tot = v1 + v2
    w1 = g_val * v1 / tot
    w2 = g_val * v2 / tot
    comb = jnp.where(lane == l1, w1, 0.0) + jnp.where(lane == l2, w2, 0.0)
    return (comb + jnp.where(lane == ROUTE_INFO, l1, 0.0) + jnp.where(lane == ROUTE_INFO + 1, l2, 0.0)
            + jnp.where(lane == ROUTE_INFO + 2, w1, 0.0) + jnp.where(lane == ROUTE_INFO + 3, w2, 0.0))


def _outproj_body(x, ylru, yatt, sh2, sc2, gt1, glru, gattn, wout, ln1g, ln1b, wr, br, exact):
    mixin = jnp.concatenate([_rms_norm(ylru, glru), _rms_norm(yatt, gattn)], axis=-1)
    mix = _dot(mixin, wout, exact)
    x1 = _layer_norm(DEEPNORM_ALPHA * x + (1.0 + gt1) * mix, ln1g, ln1b)
    h2 = x1 * (1.0 + sc2) + sh2
    return x1, _route(h2, wr, br, exact)


def _outproj_prompt_kernel(x_ref, ylru_ref, yatt_ref, mod_ref, glru_ref, gattn_ref, wout_ref,
                           ln1g_ref, ln1b_ref, wr_ref, br_ref, x1_ref, info_ref, cnt_ref, tri_ref, carry_ref,
                           *, tm, per_seq):
    i = pl.program_id(0)

    @pl.when(i == 0)
    def _():
        r = lax.broadcasted_iota(jnp.int32, (tm, tm), 0)
        c = lax.broadcasted_iota(jnp.int32, (tm, tm), 1)
        tri_ref[...] = jnp.where(c < r, 1.0, 0.0).astype(BF16)

    @pl.when(i % per_seq == 0)
    def _():
        carry_ref[...] = jnp.zeros_like(carry_ref)

    gt1 = mod_ref[0, 2:3, :]
    sh2 = mod_ref[0, 3:4, :]
    sc2 = mod_ref[0, 4:5, :]
    combs = []
    nsplit = 2
    for h in range(nsplit):
        rows = slice(h * (tm // nsplit), (h + 1) * (tm // nsplit))
        x1_h, comb_h = _outproj_body(x_ref[rows, :], ylru_ref[rows, :], yatt_ref[rows, :], sh2, sc2, gt1,
                                     glru_ref[...], gattn_ref[...], wout_ref[...], ln1g_ref[...], ln1b_ref[...],
                                     wr_ref[...], br_ref[...], False)
        x1_ref[rows, :] = x1_h
        combs.append(comb_h)
    comb = jnp.concatenate(combs, axis=0)
    lane = lax.broadcasted_iota(jnp.int32, (tm, ROUTE_LANES), 1).astype(F32)
    l1 = jnp.sum(jnp.where(lane == ROUTE_INFO, comb, 0.0), axis=-1, keepdims=True)
    l2 = jnp.sum(jnp.where(lane == ROUTE_INFO + 1, comb, 0.0), axis=-1, keepdims=True)
    o1 = lane == l1
    o2 = lane == l2
    onehot = jnp.where(o1 | o2, 1.0, 0.0)
    before = jnp.dot(tri_ref[...], onehot.astype(BF16), preferred_element_type=F32) + carry_ref[0:1, :]
    rank1 = jnp.sum(jnp.where(o1, before, 0.0), axis=-1, keepdims=True)
    rank2 = jnp.sum(jnp.where(o2, before, 0.0), axis=-1, keepdims=True)
    total = carry_ref[0:1, :] + jnp.sum(onehot, axis=0, keepdims=True)
    carry_ref[...] = jnp.broadcast_to(total, carry_ref.shape)
    cnt_ref[0] = jnp.broadcast_to(total, (8, ROUTE_LANES))
    info = (comb + jnp.where(lane == ROUTE_INFO + 4, rank1, 0.0) + jnp.where(lane == ROUTE_INFO + 5, rank2, 0.0))
    info_ref[0] = jnp.transpose(info)[ROUTE_INFO:ROUTE_INFO + 8, :]


OUTPROJ_TILE = 1024


def _outproj_prompt(x2d, ylru2d, yatt2d, modp, glru, gattn, wout_bf16, ln1g, ln1b, wr, br, tm=OUTPROJ_TILE):
    n, d = x2d.shape
    per_seq = SEQ // tm
    full = lambda shp: pl.BlockSpec(shp, lambda i: tuple(0 for _ in shp))
    kern = functools.partial(_outproj_prompt_kernel, tm=tm, per_seq=per_seq)
    return pl.pallas_call(
        kern,
        grid=(n // tm,),
        in_specs=[pl.BlockSpec((tm, d), lambda i: (i, 0)),
                  pl.BlockSpec((tm, LRU_WIDTH), lambda i: (i, 0)),
                  pl.BlockSpec((tm, ATTN_WIDTH), lambda i: (i, 0)),
                  pl.BlockSpec((1, 6, d), lambda i: (i // per_seq, 0, 0)),
                  full((1, LRU_WIDTH)), full((1, ATTN_WIDTH)), full((d, d)),
                  full((1, d)), full((1, d)), full((d, ROUTE_LANES)), full((1, ROUTE_LANES))],
        out_specs=[pl.BlockSpec((tm, d), lambda i: (i, 0)),
                   pl.BlockSpec((1, 8, tm), lambda i: (i, 0, 0)),
                   pl.BlockSpec((1, 8, ROUTE_LANES), lambda i: (i // per_seq, 0, 0))],
        out_shape=[jax.ShapeDtypeStruct((n, d), F32),
                   jax.ShapeDtypeStruct((n // tm, 8, tm), F32),
                   jax.ShapeDtypeStruct((n // SEQ, 8, ROUTE_LANES), F32)],
        scratch_shapes=[pltpu.VMEM((tm, tm), BF16), pltpu.VMEM((8, ROUTE_LANES), F32)],
        compiler_params=_cparams(("arbitrary",)),
        name="prompt_outproj_ln_route",
    )(x2d, ylru2d, yatt2d, modp, glru, gattn, wout_bf16, ln1g, ln1b, wr, br)


DENSE_EXPERTS_PER_STEP = 4


def _moe_kernel(x1_ref, comb_ref, sh2_ref, sc2_ref, gt2_ref, wgu_ref, wd_ref, ln2g_ref, ln2b_ref,
                o_ref, h2_ref, acc_ref):
    g = pl.program_id(1)

    @pl.when(g == 0)
    def _():
        h2_ref[...] = (x1_ref[...] * (1.0 + sc2_ref[...]) + sh2_ref[...]).astype(BF16)
        acc_ref[...] = jnp.zeros_like(acc_ref)

    h2 = h2_ref[...]
    comb = comb_ref[...]
    lane = lax.broadcasted_iota(jnp.int32, comb.shape, 1)
    part = None
    for k in range(DENSE_EXPERTS_PER_STEP):
        au = jnp.dot(h2, wgu_ref[k], preferred_element_type=F32)
        c_e = jnp.sum(jnp.where(lane == g * DENSE_EXPERTS_PER_STEP + k, comb, 0.0), axis=-1, keepdims=True)
        z = _silu(au[:, :D_EXPERT]) * au[:, D_EXPERT:] * c_e
        y = jnp.dot(z.astype(BF16), wd_ref[k], preferred_element_type=F32)
        part = y if part is None else part + y
    acc_ref[...] += part

    @pl.when(g == N_EXPERTS // DENSE_EXPERTS_PER_STEP - 1)
    def _():
        o_ref[...] = _layer_norm(DEEPNORM_ALPHA * x1_ref[...] + (1.0 + gt2_ref[...]) * acc_ref[...],
                                 ln2g_ref[...], ln2b_ref[...])


def _moe_dense(x1, comb, mod, wgu_bf16, wd_bf16, ln2g, ln2b, tm):
    n, d = x1.shape
    eg = DENSE_EXPERTS_PER_STEP
    mspec = lambda k: pl.BlockSpec((tm, d), lambda i, e: (i, k))
    full = lambda shp: pl.BlockSpec(shp, lambda i, e: tuple(0 for _ in shp))
    return pl.pallas_call(
        _moe_kernel,
        grid=(n // tm, N_EXPERTS // eg),
        in_specs=[pl.BlockSpec((tm, d), lambda i, e: (i, 0)),
                  pl.BlockSpec((tm, ROUTE_LANES), lambda i, e: (i, 0)),
                  mspec(3), mspec(4), mspec(5),
                  pl.BlockSpec((eg, d, 2 * D_EXPERT), lambda i, e: (e, 0, 0)),
                  pl.BlockSpec((eg, D_EXPERT, d), lambda i, e: (e, 0, 0)),
                  full((1, d)), full((1, d))],
        out_specs=pl.BlockSpec((tm, d), lambda i, e: (i, 0)),
        out_shape=jax.ShapeDtypeStruct((n, d), F32),
        scratch_shapes=[pltpu.VMEM((tm, d), BF16), pltpu.VMEM((tm, d), F32)],
        compiler_params=_cparams(("arbitrary", "arbitrary")),
        name="moe_dense_ln",
    )(x1, comb, mod, mod, mod, wgu_bf16, wd_bf16, ln2g, ln2b)


RB_SUB = 512
RB_NSUB = SEQ // RB_SUB
RB_CHUNK = 128
RB_CHUNK_BITS = 7
RB_NCHUNK = 2 * SEQ // RB_CHUNK
RB_PITCH = RB_CHUNK + 8
RB_SPITCH = RB_SUB + 8
RB_GROUP = 3
RB_WSLOTS = 4


def _rb_kernel(cnt_ref, x1_ref, mod_ref, offs_ref, wts_ref, wgu_hbm, wd_hbm, ln2g_ref, ln2b_ref,
               o_ref, buf_ref, stage_ref, wgu_buf, wd_buf, start_ref, sem):
    b = pl.program_id(0)
    s = pl.program_id(1)

    @pl.when(s == 0)
    def _starts():
        def body(e, run):
            start_ref[e] = run
            return run + cnt_ref[b, e]
        lax.fori_loop(0, N_EXPERTS, body, jnp.int32(0))
        buf_ref[RB_NCHUNK * 8 * RB_PITCH:(RB_NCHUNK + RB_GROUP) * 8 * RB_PITCH, :] = jnp.zeros(
            (RB_GROUP * 8 * RB_PITCH, LANES), F32)

    @pl.when(s < RB_NSUB)
    def _dispatch():
        sh2 = mod_ref[0, 3:4, :]
        sc2 = mod_ref[0, 4:5, :]
        h2 = x1_ref[...] * (1.0 + sc2) + sh2
        for j in range(8):
            stage_ref[RB_SPITCH * j:RB_SPITCH * j + RB_SUB, :] = h2[:, LANES * j:LANES * (j + 1)]

        for t in range(RB_SUB):
            slab = stage_ref[pl.ds(t, 8, stride=RB_SPITCH), :]
            for a in range(2):
                buf_ref[pl.ds(offs_ref[0, a, t], 8, stride=RB_PITCH), :] = slab

    @pl.when(s == RB_NSUB)
    def _experts():
        def copies(e, slot):
            return (pltpu.make_async_copy(wgu_hbm.at[e], wgu_buf.at[slot], sem.at[slot, 0]),
                    pltpu.make_async_copy(wd_hbm.at[e], wd_buf.at[slot], sem.at[slot, 1]))

        def run_expert(e, slot):
            lo_row = start_ref[e]
            hi_row = lo_row + cnt_ref[b, e]

            c_lo = lax.shift_right_logical(lo_row, RB_CHUNK_BITS)
            c_hi = lax.shift_right_logical(hi_row + (RB_CHUNK - 1), RB_CHUNK_BITS)
            row = lax.broadcasted_iota(jnp.int32, (RB_CHUNK, 1), 0)

            def load(c):
                base = pl.multiple_of(c * (8 * RB_PITCH), 8)
                return [buf_ref[pl.ds(base + RB_PITCH * j, RB_CHUNK), :] for j in range(8)]

            def store(c, tiles, y):
                base = pl.multiple_of(c * (8 * RB_PITCH), 8)
                mine = (row >= lo_row - c * RB_CHUNK) & (row < hi_row - c * RB_CHUNK)
                for j in range(8):
                    buf_ref[pl.ds(base + RB_PITCH * j, RB_CHUNK), :] = jnp.where(
                        mine, y[:, LANES * j:LANES * (j + 1)], tiles[j])

            def group(i, carry):
                cs = [c_lo + RB_GROUP * i]
                for k in range(1, RB_GROUP):
                    cs.append(jnp.where(cs[0] + k < c_hi, cs[0] + k, RB_NCHUNK + k))
                tiles = [load(c) for c in cs]
                x = jnp.concatenate([jnp.concatenate(t, axis=-1) for t in tiles], axis=0).astype(BF16)
                au = jnp.dot(x, wgu_buf[slot], preferred_element_type=F32)
                z = (_silu(au[:, :D_EXPERT]) * au[:, D_EXPERT:]).astype(BF16)
                y = jnp.dot(z, wd_buf[slot], preferred_element_type=F32)
                for k, c in enumerate(cs):
                    store(c, tiles[k], y[RB_CHUNK * k:RB_CHUNK * (k + 1)])
                return carry

            lax.fori_loop(0, lax.div(c_hi - c_lo + (RB_GROUP - 1), RB_GROUP), group, 0)

        for e in range(RB_WSLOTS - 1):
            for c in copies(e, e):
                c.start()

        def ring_body(i, carry):
            for k in range(RB_WSLOTS):
                e = RB_WSLOTS * i + k
                ahead = e + RB_WSLOTS - 1

                @pl.when(ahead < N_EXPERTS)
                def _():
                    for c in copies(ahead, (k + RB_WSLOTS - 1) % RB_WSLOTS):
                        c.start()
                for c in copies(e, k):
                    c.wait()
                run_expert(e, k)
            return carry
        lax.fori_loop(0, N_EXPERTS // RB_WSLOTS, ring_body, 0)

    @pl.when(s > RB_NSUB)
    def _combine():
        for t in range(RB_SUB):
            acc = None
            for a in range(2):
                term = wts_ref[0, a, t] * buf_ref[pl.ds(offs_ref[0, a, t], 8, stride=RB_PITCH), :]
                acc = term if acc is None else acc + term
            stage_ref[pl.ds(t, 8, stride=RB_SPITCH), :] = acc
        gt2 = mod_ref[0, 5:6, :]
        f = jnp.concatenate([stage_ref[RB_SPITCH * j:RB_SPITCH * j + RB_SUB, :] for j in range(8)], axis=-1)
        o_ref[...] = _layer_norm(DEEPNORM_ALPHA * x1_ref[...] + (1.0 + gt2) * f, ln2g_ref[...], ln2b_ref[...])


def _rb_retile(a):
    tiles, two, t = a.shape
    return a.reshape(tiles, two, t // RB_SUB, RB_SUB).transpose(0, 2, 1, 3).reshape(-1, two, RB_SUB)


def _rb_offsets(cnt, e12, rank12):
    start = jnp.cumsum(cnt, axis=-1) - cnt
    start_t = jnp.repeat(start, e12.shape[0] // cnt.shape[0], axis=0)[:, None, None, :]
    hit = e12[..., None] == jnp.arange(N_EXPERTS, dtype=jnp.int32)
    p = jnp.sum(jnp.where(hit, start_t, 0), axis=-1) + rank12
    return lax.shift_right_logical(p, RB_CHUNK_BITS) * (8 * RB_PITCH) + (p & (RB_CHUNK - 1))


def _rb_moe(x1, modp, cnt, offs, wts, wgu_bf16, wd_bf16, ln2g, ln2b):
    n, d = x1.shape
    bsz = n // SEQ
    nsteps = 2 * RB_NSUB + 1

    def sub_index(s):
        return jnp.where(s < RB_NSUB, s, jnp.where(s == RB_NSUB, RB_NSUB - 1, s - RB_NSUB - 1))

    def tile_map(b, s, cnt_r):
        return (b * RB_NSUB + sub_index(s), 0)

    def tile_map3(b, s, cnt_r):
        return (b * RB_NSUB + sub_index(s), 0, 0)

    def out_map(b, s, cnt_r):
        return (b * RB_NSUB + jnp.maximum(s - RB_NSUB - 1, 0), 0)

    const = lambda shp: pl.BlockSpec(shp, lambda b, s, cnt_r: tuple(0 for _ in shp))
    anyspec = pl.BlockSpec(memory_space=pl.ANY)
    grid_spec = pltpu.PrefetchScalarGridSpec(
        num_scalar_prefetch=1,
        grid=(bsz, nsteps),
        in_specs=[pl.BlockSpec((RB_SUB, d), tile_map),
                  pl.BlockSpec((1, 6, d), lambda b, s, cnt_r: (b, 0, 0)),
                  pl.BlockSpec((1, 2, RB_SUB), tile_map3, memory_space=pltpu.SMEM),
                  pl.BlockSpec((1, 2, RB_SUB), tile_map3, memory_space=pltpu.SMEM),
                  anyspec, anyspec,
                  const((1, d)), const((1, d))],
        out_specs=pl.BlockSpec((RB_SUB, d), out_map),
        scratch_shapes=[pltpu.VMEM(((RB_NCHUNK + RB_GROUP) * 8 * RB_PITCH, LANES), F32),
                        pltpu.VMEM((8 * RB_SPITCH, LANES), F32),
                        pltpu.VMEM((RB_WSLOTS, d, 2 * D_EXPERT), BF16),
                        pltpu.VMEM((RB_WSLOTS, D_EXPERT, d), BF16),
                        pltpu.SMEM((N_EXPERTS,), jnp.int32),
                        pltpu.SemaphoreType.DMA((RB_WSLOTS, 2))])
    return pl.pallas_call(
        _rb_kernel,
        grid_spec=grid_spec,
        out_shape=jax.ShapeDtypeStruct((n, d), F32),
        compiler_params=_cparams(("arbitrary", "arbitrary")),
        name="moe_routed_ln",
    )(cnt, x1, modp, offs, wts, wgu_bf16, wd_bf16, ln2g, ln2b)


def _sample_in_kernel(x_ref, mod_ref, win_ref, ctx_ref, h0_ref, convw_ref, convb_ref,
                      wlo_ref, whi_ref, bgate_ref, lam_ref,
                      ylru_ref, q_ref, k_ref, v_ref, cstate_ref, hnew_ref):
    sh1 = mod_ref[0:DEC_BATCH, 0:D_MODEL]
    sc1 = mod_ref[0:DEC_BATCH, D_MODEL:2 * D_MODEL]
    h = x_ref[...] * (1.0 + sc1) + sh1
    z = _dot(h, win_ref[...], True)
    xb = z[:, :LRU_WIDTH]
    gate = z[:, LRU_WIDTH:2 * LRU_WIDTH]
    c0 = ctx_ref[:, 0, :]
    c1 = ctx_ref[:, 1, :]
    c2 = ctx_ref[:, 2, :]
    xc = (convb_ref[...] + convw_ref[0:1, :] * c0 + convw_ref[1:2, :] * c1
          + convw_ref[2:3, :] * c2 + convw_ref[3:4, :] * xb)
    cstate_ref[:, 0, :] = c1
    cstate_ref[:, 1, :] = c2
    cstate_ref[:, 2, :] = xb
    sp = _softplus(-lam_ref[...])
    a, bterm = _lru_gates(xc, wlo_ref[...], whi_ref[...], bgate_ref[...], sp, True)
    hn = a * h0_ref[...] + bterm
    hnew_ref[...] = hn
    ylru_ref[...] = hn * _gelu_tanh(gate)
    low = lax.broadcasted_iota(jnp.int32, (DEC_BATCH, LANES), 1) < HEAD_DIM
    for c in range(4):
        qc = z[:, 2 * LRU_WIDTH + LANES * c:2 * LRU_WIDTH + LANES * (c + 1)]
        q_ref[pl.ds(c, DEC_BATCH, stride=N_HEADS), :] = jnp.where(low, qc, 0.0)
        q_ref[pl.ds(c + 4, DEC_BATCH, stride=N_HEADS), :] = jnp.where(low, 0.0, qc)
    k_ref[...] = z[:, 2 * LRU_WIDTH + ATTN_WIDTH:2 * LRU_WIDTH + ATTN_WIDTH + KV_WIDTH]
    v_ref[...] = z[:, 2 * LRU_WIDTH + ATTN_WIDTH + KV_WIDTH:]


def _sample_in(x, mod, w_in_p, ctx, h0, conv_w, conv_b, wlo, whi, bgate, lam):
    n = DEC_BATCH
    outs = [jax.ShapeDtypeStruct((n, LRU_WIDTH), F32),
            jax.ShapeDtypeStruct((n * N_HEADS, LANES), F32),
            jax.ShapeDtypeStruct((n, KV_WIDTH), F32),
            jax.ShapeDtypeStruct((n, KV_WIDTH), F32),
            jax.ShapeDtypeStruct((n, CONV_WIDTH - 1, LRU_WIDTH), F32),
            jax.ShapeDtypeStruct((n, LRU_WIDTH), F32)]
    return pl.pallas_call(
        _sample_in_kernel,
        out_shape=outs,
        compiler_params=pltpu.CompilerParams(vmem_limit_bytes=VMEM_LIMIT),
        name="sample_inproj_rglru",
    )(x, mod, w_in_p, ctx, h0, conv_w, conv_b, wlo, whi, bgate, lam)


def _sample_attn_kernel(q_ref, kn_ref, vn_ref, ck_ref, cv_ref, sink_ref, y_ref, nk_ref, nv_ref, *, bb):
    rows = lax.broadcasted_iota(jnp.int32, (WINDOW, KV_WIDTH), 0)
    nh = N_HEADS
    q_all = q_ref[...].reshape(bb * nh, LANES)
    kcat = ck_ref[...].reshape(bb * WINDOW, KV_WIDTH)
    vcat = cv_ref[...].reshape(bb * WINDOW, KV_WIDTH)
    kn_rep = jnp.broadcast_to(kn_ref[...][:, None, :], (bb, nh, KV_WIDTH)).reshape(bb * nh, KV_WIDTH)
    vn_rep = jnp.broadcast_to(vn_ref[...][:, None, :], (bb, nh, KV_WIDTH)).reshape(bb * nh, KV_WIDTH)
    sink = jnp.concatenate([sink_ref[...]] * bb, axis=0)
    s_full = _dot_nt(q_all, kcat, True)
    s = jnp.concatenate([s_full[nh * b:nh * (b + 1), WINDOW * b:WINDOW * (b + 1)] for b in range(bb)],
                        axis=0) * ATTN_SCALE
    s_self = jnp.sum(q_all * kn_rep, axis=-1, keepdims=True) * ATTN_SCALE
    m = jnp.maximum(jnp.maximum(jnp.max(s, axis=-1, keepdims=True), s_self), sink)
    e = jnp.exp(s - m)
    e_self = jnp.exp(s_self - m)
    den = jnp.sum(e, axis=-1, keepdims=True) + e_self + jnp.exp(sink - m)
    inv = 1.0 / den
    p = e * inv
    zero = jnp.zeros((nh, WINDOW), F32)
    p_wide = jnp.concatenate(
        [jnp.concatenate([p[nh * b:nh * (b + 1)] if c == b else zero for c in range(bb)], axis=-1)
         for b in range(bb)], axis=0)
    o = _dot(p_wide, vcat, True) + (e_self * inv) * vn_rep
    y_ref[...] = o.reshape(bb, nh, LANES)
    for b in range(bb):
        nk_ref[b] = jnp.where(rows == WINDOW - 1, kn_ref[b:b + 1, :], pltpu.roll(ck_ref[b], WINDOW - 1, axis=0))
        nv_ref[b] = jnp.where(rows == WINDOW - 1, vn_ref[b:b + 1, :], pltpu.roll(cv_ref[b], WINDOW - 1, axis=0))


def _sample_attn(q3, kn, vn, cache_k, cache_v, sinks, bb=16):
    n = DEC_BATCH
    kern = functools.partial(_sample_attn_kernel, bb=bb)
    return pl.pallas_call(
        kern,
        grid=(n // bb,),
        in_specs=[pl.BlockSpec((bb, N_HEADS, LANES), lambda i: (i, 0, 0)),
                  pl.BlockSpec((bb, KV_WIDTH), lambda i: (i, 0)),
                  pl.BlockSpec((bb, KV_WIDTH), lambda i: (i, 0)),
                  pl.BlockSpec((bb, WINDOW, KV_WIDTH), lambda i: (i, 0, 0)),
                  pl.BlockSpec((bb, WINDOW, KV_WIDTH), lambda i: (i, 0, 0)),
                  pl.BlockSpec((N_HEADS, 1), lambda i: (0, 0))],
        out_specs=[pl.BlockSpec((bb, N_HEADS, LANES), lambda i: (i, 0, 0)),
                   pl.BlockSpec((bb, WINDOW, KV_WIDTH), lambda i: (i, 0, 0)),
                   pl.BlockSpec((bb, WINDOW, KV_WIDTH), lambda i: (i, 0, 0))],
        out_shape=[jax.ShapeDtypeStruct((n, N_HEADS, LANES), F32),
                   jax.ShapeDtypeStruct((n, WINDOW, KV_WIDTH), F32),
                   jax.ShapeDtypeStruct((n, WINDOW, KV_WIDTH), F32)],
        compiler_params=_cparams(("arbitrary",)),
        name="sample_cache_attention",
    )(q3, kn, vn, cache_k, cache_v, sinks.reshape(N_HEADS, 1))


def _sample_out_kernel(x_ref, ylru_ref, yatt_ref, mod_ref, glru_ref, gattn_ref, wout_ref,
                       ln1g_ref, ln1b_ref, wr_ref, br_ref, x1_ref, comb_ref):
    low = lax.broadcasted_iota(jnp.int32, (DEC_BATCH, LANES), 1) < HEAD_DIM
    yatt = jnp.concatenate(
        [jnp.where(low, yatt_ref[pl.ds(c, DEC_BATCH, stride=N_HEADS), :],
                   yatt_ref[pl.ds(c + 4, DEC_BATCH, stride=N_HEADS), :]) for c in range(4)], axis=-1)
    gt1 = mod_ref[0:DEC_BATCH, 2 * D_MODEL:3 * D_MODEL]
    sh2 = mod_ref[0:DEC_BATCH, 3 * D_MODEL:4 * D_MODEL]
    sc2 = mod_ref[0:DEC_BATCH, 4 * D_MODEL:5 * D_MODEL]
    x1, comb = _outproj_body(x_ref[...], ylru_ref[...], yatt, sh2, sc2, gt1,
                             glru_ref[...], gattn_ref[...], wout_ref[...], ln1g_ref[...], ln1b_ref[...],
                             wr_ref[...], br_ref[...], True)
    x1_ref[...] = x1
    comb_ref[...] = comb


def _sample_out(x, ylru, yatt2d, mod, glru, gattn, wout_p, ln1g, ln1b, wr, br):
    n = DEC_BATCH
    return pl.pallas_call(
        _sample_out_kernel,
        out_shape=[jax.ShapeDtypeStruct((n, D_MODEL), F32), jax.ShapeDtypeStruct((n, ROUTE_LANES), F32)],
        compiler_params=pltpu.CompilerParams(vmem_limit_bytes=VMEM_LIMIT),
        name="sample_outproj_ln_route",
    )(x, ylru, yatt2d, mod, glru, gattn, wout_p, ln1g, ln1b, wr, br)


def _block_diag_halves(w_a, w_x):
    def bd(w4):
        eye = jnp.eye(4, dtype=w4.dtype)
        return (w4[:, :, None, :] * eye[:, None, :, None]).reshape(256, 256)
    lo = jnp.concatenate([bd(w_a[:4]), bd(w_x[:4])], axis=1)
    hi = jnp.concatenate([bd(w_a[4:]), bd(w_x[4:])], axis=1)
    return lo, hi


def kernel(x_prompt, x_sample, c_prompt, c_sample, state_conv, state_h, cache_k, cache_v, w_ada, b_ada, w_in,
           conv_w, conv_b, w_rg_a, b_rg_a, w_rg_x, b_rg_x, lru_lambda, sinks, g_lru, g_attn, w_out, ln1_g, ln1_b,
           w_group, b_group, w_router, b_router, w_gate, w_up, w_down, ln2_g, ln2_b):
    d = D_MODEL
    perm = jnp.asarray(HEAD_PERM)
    w_in0 = w_in[0]
    q0 = 2 * LRU_WIDTH
    w_in_p = jnp.concatenate([w_in0[:, :q0], w_in0[:, q0:q0 + ATTN_WIDTH][:, perm], w_in0[:, q0 + ATTN_WIDTH:]],
                             axis=1)
    w_out0 = w_out[0]
    w_out_p = jnp.concatenate([w_out0[:LRU_WIDTH], w_out0[LRU_WIDTH:][perm]], axis=0)
    g_attn_p = g_attn[0][perm].reshape(1, -1)
    glru = g_lru[0].reshape(1, -1)
    wlo, whi = _block_diag_halves(w_rg_a[0], w_rg_x[0])
    bgate = jnp.concatenate([b_rg_a[0].reshape(-1), b_rg_x[0].reshape(-1)]).reshape(1, -1)
    lam = lru_lambda[0].reshape(1, -1)
    convw = conv_w[0]
    convb = conv_b[0].reshape(1, -1)
    ln1g, ln1b = ln1_g[0].reshape(1, -1), ln1_b[0].reshape(1, -1)
    ln2g, ln2b = ln2_g[0].reshape(1, -1), ln2_b[0].reshape(1, -1)
    wr = jnp.concatenate([jnp.transpose(w_router[0], (1, 0, 2)).reshape(d, N_EXPERTS), w_group[0],
                          jnp.zeros((d, ROUTE_LANES - N_EXPERTS - N_GROUPS), F32)], axis=1)
    br = jnp.concatenate([b_router[0].reshape(-1), b_group[0],
                          jnp.zeros((ROUTE_LANES - N_EXPERTS - N_GROUPS,), F32)]).reshape(1, -1)
    sink_p = sinks[0]

    c_all = jnp.concatenate([c_sample, c_prompt, jnp.zeros((8 - BATCH, d), F32)], axis=0)
    mod = _ada(c_all, w_ada[0], b_ada[0])
    modp = mod[DEC_BATCH:DEC_BATCH + BATCH].reshape(BATCH, 6, d)

    zlru, zqkv, kvlast = _inproj(x_prompt, modp, w_in_p.astype(BF16))
    ylru, cstate8, hlast8, wgu_b, wd_b = _lru(zlru, convw, convb, wlo.astype(BF16), whi.astype(BF16), bgate, lam,
                                                   w_gate[0], w_up[0], w_down[0])
    yatt = _attn(zqkv, sink_p)
    n_p = BATCH * SEQ
    x1_p, info, cntf = _outproj_prompt(x_prompt.reshape(n_p, d), ylru.reshape(n_p, LRU_WIDTH),
                                       yatt.reshape(n_p, ATTN_WIDTH), modp, glru, g_attn_p, w_out_p.astype(BF16),
                                       ln1g, ln1b, wr, br, tm=OUTPROJ_TILE)
    cnt = cntf[:, 0, :N_EXPERTS].astype(jnp.int32)
    offs = _rb_retile(_rb_offsets(cnt, info[:, 0:2].astype(jnp.int32), info[:, 4:6].astype(jnp.int32)))
    y_p = _rb_moe(x1_p, modp, cnt, offs, _rb_retile(info[:, 2:4]), wgu_b, wd_b, ln2g, ln2b)

    ylru_s, q2d, kn, vn, cstate_s, hnew_s = _sample_in(
        x_sample.reshape(DEC_BATCH, d), mod, w_in_p, state_conv[0], state_h[0],
        convw, convb, wlo, whi, bgate, lam)
    yatt3, newk, newv = _sample_attn(q2d.reshape(DEC_BATCH, N_HEADS, LANES), kn, vn,
                                     cache_k[0].reshape(DEC_BATCH, WINDOW, KV_WIDTH),
                                     cache_v[0].reshape(DEC_BATCH, WINDOW, KV_WIDTH), sink_p)
    x1_s, comb_s = _sample_out(x_sample.reshape(DEC_BATCH, d), ylru_s, yatt3.reshape(DEC_BATCH * N_HEADS, LANES),
                               mod, glru, g_attn_p, w_out_p, ln1g, ln1b, wr, br)
    y_s = _moe_dense(x1_s, comb_s, mod, wgu_b, wd_b, ln2g, ln2b, DEC_BATCH)

    return (y_p.reshape(BATCH, SEQ, d),
            y_s.reshape(DEC_BATCH, 1, d),
            cstate8[:, 5:8][None],
            hlast8[:, 7][None],
            kvlast[:, :, :KV_WIDTH].reshape(1, BATCH, WINDOW, N_KV_HEADS, HEAD_DIM),
            kvlast[:, :, KV_WIDTH:].reshape(1, BATCH, WINDOW, N_KV_HEADS, HEAD_DIM),
            cstate_s[None],
            hnew_s[None],
            newk.reshape(1, DEC_BATCH, WINDOW, N_KV_HEADS, HEAD_DIM),
            newv.reshape(1, DEC_BATCH, WINDOW, N_KV_HEADS, HEAD_DIM))
```

```python
import functools

import jax
import jax.numpy as jnp
import numpy as np
from jax import lax
from jax.experimental import pallas as pl
from jax.experimental.pallas import tpu as pltpu

F32 = jnp.float32
BF16 = jnp.bfloat16
HIGHEST = lax.Precision.HIGHEST

D_MODEL = 1024
BATCH = 4
SEQ = 4096
DEC_BATCH = 128
LRU_WIDTH = 512
LRU_BLOCKS = 8
LRU_BLOCK = 64
CONV_WIDTH = 4
LRU_C = 8.0
N_HEADS = 8
N_KV_HEADS = 2
HEAD_DIM = 64
ATTN_WIDTH = 512
KV_WIDTH = 128
WINDOW = 128
IN_WIDTH = 2 * LRU_WIDTH + ATTN_WIDTH + 2 * KV_WIDTH
N_GROUPS = 4
EXPERTS_PER_GROUP = 8
N_EXPERTS = 32
D_EXPERT = 256
DEEPNORM_ALPHA = 2.0 ** 0.25
LN_EPS = 1e-5
RMS_EPS = 1e-6
ATTN_SCALE = HEAD_DIM ** -0.5

LANES = 128
ROUTE_LANES = 128
ROUTE_INFO = 40
VMEM_LIMIT = 56 * 1024 * 1024

HEAD_PERM = np.concatenate(
    [np.concatenate([np.arange(64 * c, 64 * c + 64), np.arange(64 * (c + 4), 64 * (c + 4) + 64)])
     for c in range(4)])


def _cparams(sem):
    return pltpu.CompilerParams(dimension_semantics=sem, vmem_limit_bytes=VMEM_LIMIT)


def _dot(a, b, exact):
    if exact:
        return jnp.dot(a, b, precision=HIGHEST, preferred_element_type=F32)
    return jnp.dot(a.astype(BF16), b.astype(BF16), preferred_element_type=F32)


def _dot_nt(a, b, exact):
    dn = (((1,), (1,)), ((), ()))
    if exact:
        return lax.dot_general(a, b, dn, precision=HIGHEST, preferred_element_type=F32)
    return lax.dot_general(a.astype(BF16), b.astype(BF16), dn, preferred_element_type=F32)


def _sigmoid(x):
    return 1.0 / (1.0 + jnp.exp(-x))


def _silu(x):
    return x * _sigmoid(x)


def _gelu_tanh(x):
    return 0.5 * x * (1.0 + jnp.tanh(np.sqrt(2.0 / np.pi).astype(np.float32) * (x + 0.044715 * (x * x * x))))


def _softplus(x):
    return jnp.maximum(x, 0.0) + jnp.log1p(jnp.exp(-jnp.abs(x)))


def _layer_norm(x, g, b):
    mu = jnp.mean(x, axis=-1, keepdims=True)
    xc = x - mu
    var = jnp.mean(xc * xc, axis=-1, keepdims=True)
    return xc * lax.rsqrt(var + LN_EPS) * g + b


def _rms_norm(x, g):
    return x * lax.rsqrt(jnp.mean(x * x, axis=-1, keepdims=True) + RMS_EPS) * g


def _ada_kernel(c_ref, w_ref, b_ref, o_ref):
    o_ref[...] = _dot(_silu(c_ref[...]), w_ref[...], True) + b_ref[...]


def _ada(c_all, w_ada, b_ada):
    rows = c_all.shape[0]
    bn = 1024
    return pl.pallas_call(
        _ada_kernel,
        grid=(6 * D_MODEL // bn,),
        in_specs=[pl.BlockSpec((rows, D_MODEL), lambda j: (0, 0)),
                  pl.BlockSpec((D_MODEL, bn), lambda j: (0, j)),
                  pl.BlockSpec((1, bn), lambda j: (0, j))],
        out_specs=pl.BlockSpec((rows, bn), lambda j: (0, j)),
        out_shape=jax.ShapeDtypeStruct((rows, 6 * D_MODEL), F32),
        compiler_params=_cparams(("arbitrary",)),
        name="ada_modulation",
    )(c_all, w_ada, b_ada.reshape(1, -1))


QKV_WIDTH = ATTN_WIDTH + 2 * KV_WIDTH


def _inproj_kernel(x_ref, mod_ref, w_ref, lru_ref, qkv_ref, kvlast_ref):
    sh1 = mod_ref[0, 0:1, :]
    sc1 = mod_ref[0, 1:2, :]
    h = x_ref[0] * (1.0 + sc1) + sh1
    z = _dot(h, w_ref[...], False)
    lru_ref[0] = z[:, :2 * LRU_WIDTH]
    qkv_ref[0] = z[:, 2 * LRU_WIDTH:].astype(BF16)
    kvlast_ref[0] = z[z.shape[0] - WINDOW:, 2 * LRU_WIDTH + ATTN_WIDTH:]


def _inproj(x, modp, w_in_bf16, tm=1024):
    b, t, d = x.shape
    return pl.pallas_call(
        _inproj_kernel,
        grid=(b, t // tm),
        in_specs=[pl.BlockSpec((1, tm, d), lambda i, j: (i, j, 0)),
                  pl.BlockSpec((1, 6, d), lambda i, j: (i, 0, 0)),
                  pl.BlockSpec((d, IN_WIDTH), lambda i, j: (0, 0))],
        out_specs=[pl.BlockSpec((1, tm, 2 * LRU_WIDTH), lambda i, j: (i, j, 0)),
                   pl.BlockSpec((1, tm, QKV_WIDTH), lambda i, j: (i, j, 0)),
                   pl.BlockSpec((1, WINDOW, 2 * KV_WIDTH), lambda i, j: (i, 0, 0))],
        out_shape=[jax.ShapeDtypeStruct((b, t, 2 * LRU_WIDTH), F32),
                   jax.ShapeDtypeStruct((b, t, QKV_WIDTH), BF16),
                   jax.ShapeDtypeStruct((b, WINDOW, 2 * KV_WIDTH), F32)],
        compiler_params=_cparams(("arbitrary", "arbitrary")),
        name="prompt_inproj",
    )(x, modp, w_in_bf16)


def _lru_gates(xc, wlo, whi, bgate, sp_neg_lam, exact):
    g_lo = _dot(xc[:, :256], wlo, exact)
    g_hi = _dot(xc[:, 256:], whi, exact)
    ga = jnp.concatenate([g_lo[:, :256], g_hi[:, :256]], axis=-1) + bgate[:, :LRU_WIDTH]
    gx = jnp.concatenate([g_lo[:, 256:], g_hi[:, 256:]], axis=-1) + bgate[:, LRU_WIDTH:]
    r = _sigmoid(ga)
    i = _sigmoid(gx)
    log_a = -LRU_C * r * sp_neg_lam
    a = jnp.exp(log_a)
    one_minus_a2 = -jnp.tanh(log_a) * (a * a + 1.0) if exact else 1.0 - a * a
    bterm = jnp.sqrt(one_minus_a2) * (i * xc)
    return a, bterm


def _lru_kernel(z_ref, convw_ref, convb_ref, wlo_ref, whi_ref, bgate_ref, lam_ref, wg_ref, wu_ref, wd_ref,
                y_ref, cstate_ref, hlast_ref, wgub_ref, wdb_ref, tail_ref, carry_ref, *, tl):
    j = pl.program_id(1)

    @pl.when(j == 0)
    def _():
        tail_ref[...] = jnp.zeros_like(tail_ref)
        carry_ref[...] = jnp.zeros_like(carry_ref)

    wgub_ref[0, :, :D_EXPERT] = wg_ref[0].astype(BF16)
    wgub_ref[0, :, D_EXPERT:] = wu_ref[0].astype(BF16)
    wdb_ref[...] = wd_ref[...].astype(BF16)

    xb = z_ref[0, :, :LRU_WIDTH]
    gate = z_ref[0, :, LRU_WIDTH:]
    xc = convb_ref[...] + convw_ref[3:4, :] * xb
    rows8 = lax.broadcasted_iota(jnp.int32, (8, LRU_WIDTH), 0)
    tail = tail_ref[...]
    for back in (1, 2, 3):
        rolled = pltpu.roll(xb, back, axis=0)
        top = jnp.where(rows8 >= back, rolled[:8], pltpu.roll(tail, back, axis=0))
        shifted = jnp.concatenate([top, rolled[8:]], axis=0)
        xc = xc + convw_ref[3 - back:4 - back, :] * shifted
    tail_ref[...] = xb[tl - 8:, :]
    cstate_ref[0] = xb[tl - 8:, :]

    sp = _softplus(-lam_ref[...])
    a, bterm = _lru_gates(xc, wlo_ref[...], whi_ref[...], bgate_ref[...], sp, False)

    groups = tl // 8
    a = a.reshape(groups, 8, LRU_WIDTH)
    bterm = bterm.reshape(groups, 8, LRU_WIDTH)
    r8 = lax.broadcasted_iota(jnp.int32, (groups, 8, LRU_WIDTH), 1)
    s = 1
    while s < 8:
        a_sh = jnp.where(r8 >= s, pltpu.roll(a, s, axis=1), 1.0)
        b_sh = jnp.where(r8 >= s, pltpu.roll(bterm, s, axis=1), 0.0)
        bterm = a * b_sh + bterm
        a = a * a_sh
        s *= 2
    a_tot = jnp.broadcast_to(a[:, 7:8, :], (groups, 8, LRU_WIDTH))
    b_tot = jnp.broadcast_to(bterm[:, 7:8, :], (groups, 8, LRU_WIDTH))
    h_in = jnp.broadcast_to(carry_ref[7:8, :], (8, LRU_WIDTH))
    pieces = []
    for g in range(groups):
        pieces.append(a[g] * h_in + bterm[g])
        h_in = a_tot[g] * h_in + b_tot[g]
    h = jnp.concatenate(pieces, axis=0)
    carry_ref[...] = h_in
    hlast_ref[0] = h_in
    y_ref[0] = h * _gelu_tanh(gate)


def _lru(zin, conv_w, conv_b, wlo, whi, bgate, lam, w_gate, w_up, w_down, tl=512):
    b, t, _ = zin.shape
    steps = t // tl
    assert b * steps == N_EXPERTS
    d = D_MODEL
    kern = functools.partial(_lru_kernel, tl=tl)
    full = lambda shp: pl.BlockSpec(shp, lambda i, j: tuple(0 for _ in shp))
    per_step = lambda shp: pl.BlockSpec(shp, lambda i, j: (i * steps + j, 0, 0))
    return pl.pallas_call(
        kern,
        grid=(b, steps),
        in_specs=[pl.BlockSpec((1, tl, 2 * LRU_WIDTH), lambda i, j: (i, j, 0)),
                  full((CONV_WIDTH, LRU_WIDTH)), full((1, LRU_WIDTH)),
                  full((256, 512)), full((256, 512)), full((1, 2 * LRU_WIDTH)), full((1, LRU_WIDTH)),
                  per_step((1, d, D_EXPERT)), per_step((1, d, D_EXPERT)), per_step((1, D_EXPERT, d))],
        out_specs=[pl.BlockSpec((1, tl, LRU_WIDTH), lambda i, j: (i, j, 0)),
                   pl.BlockSpec((1, 8, LRU_WIDTH), lambda i, j: (i, 0, 0)),
                   pl.BlockSpec((1, 8, LRU_WIDTH), lambda i, j: (i, 0, 0)),
                   per_step((1, d, 2 * D_EXPERT)), per_step((1, D_EXPERT, d))],
        out_shape=[jax.ShapeDtypeStruct((b, t, LRU_WIDTH), F32),
                   jax.ShapeDtypeStruct((b, 8, LRU_WIDTH), F32),
                   jax.ShapeDtypeStruct((b, 8, LRU_WIDTH), F32),
                   jax.ShapeDtypeStruct((N_EXPERTS, d, 2 * D_EXPERT), BF16),
                   jax.ShapeDtypeStruct(w_down.shape, BF16)],
        scratch_shapes=[pltpu.VMEM((8, LRU_WIDTH), F32), pltpu.VMEM((8, LRU_WIDTH), F32)],
        compiler_params=_cparams(("arbitrary", "arbitrary")),
        name="prompt_rglru",
    )(zin, conv_w, conv_b, wlo, whi, bgate, lam, w_gate, w_up, w_down)


ATTN_BLOCKS = 16


def _attn_kernel(q_ref, k_ref, v_ref, sink_ref, o_ref, kprev_ref, vprev_ref):
    j = pl.program_id(1)

    @pl.when(j == 0)
    def _():
        kprev_ref[...] = jnp.zeros_like(kprev_ref)
        vprev_ref[...] = jnp.zeros_like(vprev_ref)

    blk = WINDOW
    lane = lax.broadcasted_iota(jnp.int32, (blk, LANES), 1)
    low = lane < HEAD_DIM
    qi = lax.broadcasted_iota(jnp.int32, (blk, 2 * blk), 0)
    sj = lax.broadcasted_iota(jnp.int32, (blk, 2 * blk), 1)
    rel = blk + qi - sj
    in_window = (rel >= 0) & (rel <= WINDOW)
    sink = sink_ref[...].reshape(N_HEADS, blk, 1)
    k_ext = jnp.concatenate([kprev_ref[...], k_ref[0]], axis=0)
    v_ext = jnp.concatenate([vprev_ref[...], v_ref[0]], axis=0)
    v_ext = jnp.concatenate([v_ext, jnp.ones_like(v_ext)], axis=-1)
    for n in range(ATTN_BLOCKS):
        q = q_ref[0, blk * n:blk * (n + 1), :]
        pieces = []
        for half in (0, 1):
            for c in range(4):
                qc = q[:, LANES * c:LANES * (c + 1)]
                pieces.append(jnp.where(low if half == 0 else ~low, qc, 0.0).astype(BF16))
        q8 = jnp.concatenate(pieces, axis=0)
        k_band = k_ext[blk * n:blk * (n + 2)]
        v_band = v_ext[blk * n:blk * (n + 2)]
        s = _dot_nt(q8, k_band, False) * ATTN_SCALE
        s = s.reshape(N_HEADS, blk, 2 * blk)
        valid = in_window & ((sj >= blk) | (j > 0)) if n == 0 else in_window
        s = jnp.where(valid[None], s, -jnp.inf)
        m = jnp.maximum(jnp.max(s, axis=-1, keepdims=True), sink)
        e = jnp.exp(s - m).reshape(N_HEADS * blk, 2 * blk)
        ov = _dot(e, v_band, False)
        den = ov[:, KV_WIDTH:] + jnp.exp(sink - m).reshape(N_HEADS * blk, 1)
        o8 = ov[:, :KV_WIDTH] * (1.0 / den)
        cols = []
        for c in range(4):
            cols.append(jnp.where(low, o8[blk * c:blk * (c + 1)], o8[blk * (c + 4):blk * (c + 5)]))
        o_ref[0, blk * n:blk * (n + 1), :] = jnp.concatenate(cols, axis=-1)
    kprev_ref[...] = k_ref[0, blk * (ATTN_BLOCKS - 1):, :]
    vprev_ref[...] = v_ref[0, blk * (ATTN_BLOCKS - 1):, :]


def _attn(qkv, sinks):
    b, t, _ = qkv.shape
    blk = WINDOW
    tq = blk * ATTN_BLOCKS
    sink_col = jnp.repeat(sinks.astype(F32), blk).reshape(N_HEADS * blk, 1)
    kcol = ATTN_WIDTH // KV_WIDTH
    return pl.pallas_call(
        _attn_kernel,
        grid=(b, t // tq),
        in_specs=[pl.BlockSpec((1, tq, ATTN_WIDTH), lambda i, j: (i, j, 0)),
                  pl.BlockSpec((1, tq, KV_WIDTH), lambda i, j: (i, j, kcol)),
                  pl.BlockSpec((1, tq, KV_WIDTH), lambda i, j: (i, j, kcol + 1)),
                  pl.BlockSpec((N_HEADS * blk, 1), lambda i, j: (0, 0))],
        out_specs=pl.BlockSpec((1, tq, ATTN_WIDTH), lambda i, j: (i, j, 0)),
        out_shape=jax.ShapeDtypeStruct((b, t, ATTN_WIDTH), F32),
        scratch_shapes=[pltpu.VMEM((blk, KV_WIDTH), BF16), pltpu.VMEM((blk, KV_WIDTH), BF16)],
        compiler_params=_cparams(("arbitrary", "arbitrary")),
        name="prompt_window_attention",
    )(qkv, qkv, qkv, sink_col)


def _route(h2, wr, br, exact):
    t = h2.shape[0]
    logits = _dot(h2, wr, exact) + br
    lane = lax.broadcasted_iota(jnp.int32, (t, ROUTE_LANES), 1).astype(F32)
    neg = -jnp.inf
    big = float(ROUTE_LANES)
    is_g = (lane >= N_EXPERTS) & (lane < N_EXPERTS + N_GROUPS)
    lg = jnp.where(is_g, logits, neg)
    mg = jnp.max(lg, axis=-1, keepdims=True)
    eg = jnp.where(is_g, jnp.exp(lg - mg), 0.0)
    pg = eg / jnp.sum(eg, axis=-1, keepdims=True)
    g_val = jnp.max(pg, axis=-1, keepdims=True)
    g_lane = jnp.min(jnp.where((pg == g_val) & is_g, lane, big), axis=-1, keepdims=True)
    g_idx = g_lane - N_EXPERTS
    in_grp = (lane >= g_idx * EXPERTS_PER_GROUP) & (lane < (g_idx + 1.0) * EXPERTS_PER_GROUP)
    le = jnp.where(in_grp, logits, neg)
    me = jnp.max(le, axis=-1, keepdims=True)
    ee = jnp.where(in_grp, jnp.exp(le - me), 0.0)
    pe = ee / jnp.sum(ee, axis=-1, keepdims=True)
    v1 = jnp.max(pe, axis=-1, keepdims=True)
    l1 = jnp.min(jnp.where((pe == v1) & in_grp, lane, big), axis=-1, keepdims=True)
    rest = in_grp & (lane != l1)
    pe2 = jnp.where(rest, pe, -1.0)
    v2 = jnp.max(pe2, axis=-1, keepdims=True)
    l2 = jnp.min(jnp.where((pe2 == v2) & rest, lane, big), axis=-1, keepdims=True)
    tot = v1 + v2
    w1 = g_val * v1 / tot
    w2 = g_val * v2 / tot
    comb = jnp.where(lane == l1, w1, 0.0) + jnp.where(lane == l2, w2, 0.0)
    return (comb + jnp.where(lane == ROUTE_INFO, l1, 0.0) + jnp.where(lane == ROUTE_INFO + 1, l2, 0.0)
            + jnp.where(lane == ROUTE_INFO + 2, w1, 0.0) + jnp.where(lane == ROUTE_INFO + 3, w2, 0.0))


def _outproj_body(x, ylru, yatt, sh2, sc2, gt1, glru, gattn, wout, ln1g, ln1b, wr, br, exact):
    mixin = jnp.concatenate([_rms_norm(ylru, glru), _rms_norm(yatt, gattn)], axis=-1)
    mix = _dot(mixin, wout, exact)
    x1 = _layer_norm(DEEPNORM_ALPHA * x + (1.0 + gt1) * mix, ln1g, ln1b)
    h2 = x1 * (1.0 + sc2) + sh2
    return x1, _route(h2, wr, br, exact)


def _outproj_prompt_kernel(x_ref, ylru_ref, yatt_ref, mod_ref, glru_ref, gattn_ref, wout_ref,
                           ln1g_ref, ln1b_ref, wr_ref, br_ref, x1_ref, info_ref, cnt_ref, tri_ref, carry_ref,
                           *, tm, per_seq):
    i = pl.program_id(0)

    @pl.when(i == 0)
    def _():
        r = lax.broadcasted_iota(jnp.int32, (tm, tm), 0)
        c = lax.broadcasted_iota(jnp.int32, (tm, tm), 1)
        tri_ref[...] = jnp.where(c < r, 1.0, 0.0).astype(BF16)

    @pl.when(i % per_seq == 0)
    def _():
        carry_ref[...] = jnp.zeros_like(carry_ref)

    gt1 = mod_ref[0, 2:3, :]
    sh2 = mod_ref[0, 3:4, :]
    sc2 = mod_ref[0, 4:5, :]
    combs = []
    nsplit = 2
    for h in range(nsplit):
        rows = slice(h * (tm // nsplit), (h + 1) * (tm // nsplit))
        x1_h, comb_h = _outproj_body(x_ref[rows, :], ylru_ref[rows, :], yatt_ref[rows, :], sh2, sc2, gt1,
                                     glru_ref[...], gattn_ref[...], wout_ref[...], ln1g_ref[...], ln1b_ref[...],
                                     wr_ref[...], br_ref[...], False)
        x1_ref[rows, :] = x1_h
        combs.append(comb_h)
    comb = jnp.concatenate(combs, axis=0)
    lane = lax.broadcasted_iota(jnp.int32, (tm, ROUTE_LANES), 1).astype(F32)
    l1 = jnp.sum(jnp.where(lane == ROUTE_INFO, comb, 0.0), axis=-1, keepdims=True)
    l2 = jnp.sum(jnp.where(lane == ROUTE_INFO + 1, comb, 0.0), axis=-1, keepdims=True)
    o1 = lane == l1
    o2 = lane == l2
    onehot = jnp.where(o1 | o2, 1.0, 0.0)
    before = jnp.dot(tri_ref[...], onehot.astype(BF16), preferred_element_type=F32) + carry_ref[0:1, :]
    rank1 = jnp.sum(jnp.where(o1, before, 0.0), axis=-1, keepdims=True)
    rank2 = jnp.sum(jnp.where(o2, before, 0.0), axis=-1, keepdims=True)
    total = carry_ref[0:1, :] + jnp.sum(onehot, axis=0, keepdims=True)
    carry_ref[...] = jnp.broadcast_to(total, carry_ref.shape)
    cnt_ref[0] = jnp.broadcast_to(total, (8, ROUTE_LANES))
    info = (comb + jnp.where(lane == ROUTE_INFO + 4, rank1, 0.0) + jnp.where(lane == ROUTE_INFO + 5, rank2, 0.0))
    info_ref[0] = jnp.transpose(info)[ROUTE_INFO:ROUTE_INFO + 8, :]


OUTPROJ_TILE = 1024


def _outproj_prompt(x2d, ylru2d, yatt2d, modp, glru, gattn, wout_bf16, ln1g, ln1b, wr, br, tm=OUTPROJ_TILE):
    n, d = x2d.shape
    per_seq = SEQ // tm
    full = lambda shp: pl.BlockSpec(shp, lambda i: tuple(0 for _ in shp))
    kern = functools.partial(_outproj_prompt_kernel, tm=tm, per_seq=per_seq)
    return pl.pallas_call(
        kern,
        grid=(n // tm,),
        in_specs=[pl.BlockSpec((tm, d), lambda i: (i, 0)),
                  pl.BlockSpec((tm, LRU_WIDTH), lambda i: (i, 0)),
                  pl.BlockSpec((tm, ATTN_WIDTH), lambda i: (i, 0)),
                  pl.BlockSpec((1, 6, d), lambda i: (i // per_seq, 0, 0)),
                  full((1, LRU_WIDTH)), full((1, ATTN_WIDTH)), full((d, d)),
                  full((1, d)), full((1, d)), full((d, ROUTE_LANES)), full((1, ROUTE_LANES))],
        out_specs=[pl.BlockSpec((tm, d), lambda i: (i, 0)),
                   pl.BlockSpec((1, 8, tm), lambda i: (i, 0, 0)),
                   pl.BlockSpec((1, 8, ROUTE_LANES), lambda i: (i // per_seq, 0, 0))],
        out_shape=[jax.ShapeDtypeStruct((n, d), F32),
                   jax.ShapeDtypeStruct((n // tm, 8, tm), F32),
                   jax.ShapeDtypeStruct((n // SEQ, 8, ROUTE_LANES), F32)],
        scratch_shapes=[pltpu.VMEM((tm, tm), BF16), pltpu.VMEM((8, ROUTE_LANES), F32)],
        compiler_params=_cparams(("arbitrary",)),
        name="prompt_outproj_ln_route",
    )(x2d, ylru2d, yatt2d, modp, glru, gattn, wout_bf16, ln1g, ln1b, wr, br)


DENSE_EXPERTS_PER_STEP = 8


def _moe_kernel(x1_ref, comb_ref, sh2_ref, sc2_ref, gt2_ref, wgu_ref, wd_ref, ln2g_ref, ln2b_ref,
                o_ref, h2_ref, acc_ref):
    g = pl.program_id(1)

    @pl.when(g == 0)
    def _():
        h2_ref[...] = (x1_ref[...] * (1.0 + sc2_ref[...]) + sh2_ref[...]).astype(BF16)
        acc_ref[...] = jnp.zeros_like(acc_ref)

    h2 = h2_ref[...]
    comb = comb_ref[...]
    lane = lax.broadcasted_iota(jnp.int32, comb.shape, 1)
    part = None
    for k in range(DENSE_EXPERTS_PER_STEP):
        au = jnp.dot(h2, wgu_ref[k], preferred_element_type=F32)
        c_e = jnp.sum(jnp.where(lane == g * DENSE_EXPERTS_PER_STEP + k, comb, 0.0), axis=-1, keepdims=True)
        z = _silu(au[:, :D_EXPERT]) * au[:, D_EXPERT:] * c_e
        y = jnp.dot(z.astype(BF16), wd_ref[k], preferred_element_type=F32)
        part = y if part is None else part + y
    acc_ref[...] += part

    @pl.when(g == N_EXPERTS // DENSE_EXPERTS_PER_STEP - 1)
    def _():
        o_ref[...] = _layer_norm(DEEPNORM_ALPHA * x1_ref[...] + (1.0 + gt2_ref[...]) * acc_ref[...],
                                 ln2g_ref[...], ln2b_ref[...])


def _moe_dense(x1, comb, mod, wgu_bf16, wd_bf16, ln2g, ln2b, tm):
    n, d = x1.shape
    eg = DENSE_EXPERTS_PER_STEP
    mspec = lambda k: pl.BlockSpec((tm, d), lambda i, e: (i, k))
    full = lambda shp: pl.BlockSpec(shp, lambda i, e: tuple(0 for _ in shp))
    return pl.pallas_call(
        _moe_kernel,
        grid=(n // tm, N_EXPERTS // eg),
        in_specs=[pl.BlockSpec((tm, d), lambda i, e: (i, 0)),
                  pl.BlockSpec((tm, ROUTE_LANES), lambda i, e: (i, 0)),
                  mspec(3), mspec(4), mspec(5),
                  pl.BlockSpec((eg, d, 2 * D_EXPERT), lambda i, e: (e, 0, 0)),
                  pl.BlockSpec((eg, D_EXPERT, d), lambda i, e: (e, 0, 0)),
                  full((1, d)), full((1, d))],
        out_specs=pl.BlockSpec((tm, d), lambda i, e: (i, 0)),
        out_shape=jax.ShapeDtypeStruct((n, d), F32),
        scratch_shapes=[pltpu.VMEM((tm, d), BF16), pltpu.VMEM((tm, d), F32)],
        compiler_params=_cparams(("arbitrary", "arbitrary")),
        name="moe_dense_ln",
    )(x1, comb, mod, mod, mod, wgu_bf16, wd_bf16, ln2g, ln2b)


RB_SUB = 512
RB_NSUB = SEQ // RB_SUB
RB_CHUNK = 128
RB_CHUNK_BITS = 7
RB_NCHUNK = 2 * SEQ // RB_CHUNK
RB_PITCH = RB_CHUNK + 8
RB_SPITCH = RB_SUB + 8
RB_GROUP = 3
RB_WSLOTS = 4


def _rb_kernel(cnt_ref, x1_ref, mod_ref, offs_ref, wts_ref, wgu_hbm, wd_hbm, ln2g_ref, ln2b_ref,
               o_ref, buf_ref, stage_ref, wgu_buf, wd_buf, start_ref, sem):
    b = pl.program_id(0)
    s = pl.program_id(1)

    @pl.when(s == 0)
    def _starts():
        def body(e, run):
            start_ref[e] = run
            return run + cnt_ref[b, e]
        lax.fori_loop(0, N_EXPERTS, body, jnp.int32(0))
        buf_ref[RB_NCHUNK * 8 * RB_PITCH:(RB_NCHUNK + RB_GROUP) * 8 * RB_PITCH, :] = jnp.zeros(
            (RB_GROUP * 8 * RB_PITCH, LANES), F32)

    @pl.when(s < RB_NSUB)
    def _dispatch():
        sh2 = mod_ref[0, 3:4, :]
        sc2 = mod_ref[0, 4:5, :]
        h2 = x1_ref[...] * (1.0 + sc2) + sh2
        for j in range(8):
            stage_ref[RB_SPITCH * j:RB_SPITCH * j + RB_SUB, :] = h2[:, LANES * j:LANES * (j + 1)]

        for t in range(RB_SUB):
            slab = stage_ref[pl.ds(t, 8, stride=RB_SPITCH), :]
            for a in range(2):
                buf_ref[pl.ds(offs_ref[0, a, t], 8, stride=RB_PITCH), :] = slab

    @pl.when(s == RB_NSUB)
    def _experts():
        def copies(e, slot):
            return (pltpu.make_async_copy(wgu_hbm.at[e], wgu_buf.at[slot], sem.at[slot, 0]),
                    pltpu.make_async_copy(wd_hbm.at[e], wd_buf.at[slot], sem.at[slot, 1]))

        def run_expert(e, slot):
            lo_row = start_ref[e]
            hi_row = lo_row + cnt_ref[b, e]

            c_lo = lax.shift_right_logical(lo_row, RB_CHUNK_BITS)
            c_hi = lax.shift_right_logical(hi_row + (RB_CHUNK - 1), RB_CHUNK_BITS)
            row = lax.broadcasted_iota(jnp.int32, (RB_CHUNK, 1), 0)

            def load(c):
                base = pl.multiple_of(c * (8 * RB_PITCH), 8)
                return [buf_ref[pl.ds(base + RB_PITCH * j, RB_CHUNK), :] for j in range(8)]

            def store(c, tiles, y):
                base = pl.multiple_of(c * (8 * RB_PITCH), 8)
                mine = (row >= lo_row - c * RB_CHUNK) & (row < hi_row - c * RB_CHUNK)
                for j in range(8):
                    buf_ref[pl.ds(base + RB_PITCH * j, RB_CHUNK), :] = jnp.where(
                        mine, y[:, LANES * j:LANES * (j + 1)], tiles[j])

            def group(i, carry):
                cs = [c_lo + RB_GROUP * i]
                for k in range(1, RB_GROUP):
                    cs.append(jnp.where(cs[0] + k < c_hi, cs[0] + k, RB_NCHUNK + k))
                tiles = [load(c) for c in cs]
                x = jnp.concatenate([jnp.concatenate(t, axis=-1) for t in tiles], axis=0).astype(BF16)
                au = jnp.dot(x, wgu_buf[slot], preferred_element_type=F32)
                z = (_silu(au[:, :D_EXPERT]) * au[:, D_EXPERT:]).astype(BF16)
                y = jnp.dot(z, wd_buf[slot], preferred_element_type=F32)
                for k, c in enumerate(cs):
                    store(c, tiles[k], y[RB_CHUNK * k:RB_CHUNK * (k + 1)])
                return carry

            lax.fori_loop(0, lax.div(c_hi - c_lo + (RB_GROUP - 1), RB_GROUP), group, 0)

        for e in range(RB_WSLOTS - 1):
            for c in copies(e, e):
                c.start()

        def ring_body(i, carry):
            for k in range(RB_WSLOTS):
                e = RB_WSLOTS * i + k
                ahead = e + RB_WSLOTS - 1

                @pl.when(ahead < N_EXPERTS)
                def _():
                    for c in copies(ahead, (k + RB_WSLOTS - 1) % RB_WSLOTS):
                        c.start()
                for c in copies(e, k):
                    c.wait()
                run_expert(e, k)
            return carry
        lax.fori_loop(0, N_EXPERTS // RB_WSLOTS, ring_body, 0)

    @pl.when(s > RB_NSUB)
    def _combine():
        for t in range(RB_SUB):
            acc = None
            for a in range(2):
                term = wts_ref[0, a, t] * buf_ref[pl.ds(offs_ref[0, a, t], 8, stride=RB_PITCH), :]
                acc = term if acc is None else acc + term
            stage_ref[pl.ds(t, 8, stride=RB_SPITCH), :] = acc
        gt2 = mod_ref[0, 5:6, :]
        f = jnp.concatenate([stage_ref[RB_SPITCH * j:RB_SPITCH * j + RB_SUB, :] for j in range(8)], axis=-1)
        o_ref[...] = _layer_norm(DEEPNORM_ALPHA * x1_ref[...] + (1.0 + gt2) * f, ln2g_ref[...], ln2b_ref[...])


def _rb_retile(a):
    tiles, two, t = a.shape
    return a.reshape(tiles, two, t // RB_SUB, RB_SUB).transpose(0, 2, 1, 3).reshape(-1, two, RB_SUB)


def _rb_offsets(cnt, e12, rank12):
    start = jnp.cumsum(cnt, axis=-1) - cnt
    start_t = jnp.repeat(start, e12.shape[0] // cnt.shape[0], axis=0)[:, None, None, :]
    hit = e12[..., None] == jnp.arange(N_EXPERTS, dtype=jnp.int32)
    p = jnp.sum(jnp.where(hit, start_t, 0), axis=-1) + rank12
    return lax.shift_right_logical(p, RB_CHUNK_BITS) * (8 * RB_PITCH) + (p & (RB_CHUNK - 1))


def _rb_moe(x1, modp, cnt, offs, wts, wgu_bf16, wd_bf16, ln2g, ln2b):
    n, d = x1.shape
    bsz = n // SEQ
    nsteps = 2 * RB_NSUB + 1

    def sub_index(s):
        return jnp.where(s < RB_NSUB, s, jnp.where(s == RB_NSUB, RB_NSUB - 1, s - RB_NSUB - 1))

    def tile_map(b, s, cnt_r):
        return (b * RB_NSUB + sub_index(s), 0)

    def tile_map3(b, s, cnt_r):
        return (b * RB_NSUB + sub_index(s), 0, 0)

    def out_map(b, s, cnt_r):
        return (b * RB_NSUB + jnp.maximum(s - RB_NSUB - 1, 0), 0)

    const = lambda shp: pl.BlockSpec(shp, lambda b, s, cnt_r: tuple(0 for _ in shp))
    anyspec = pl.BlockSpec(memory_space=pl.ANY)
    grid_spec = pltpu.PrefetchScalarGridSpec(
        num_scalar_prefetch=1,
        grid=(bsz, nsteps),
        in_specs=[pl.BlockSpec((RB_SUB, d), tile_map),
                  pl.BlockSpec((1, 6, d), lambda b, s, cnt_r: (b, 0, 0)),
                  pl.BlockSpec((1, 2, RB_SUB), tile_map3, memory_space=pltpu.SMEM),
                  pl.BlockSpec((1, 2, RB_SUB), tile_map3, memory_space=pltpu.SMEM),
                  anyspec, anyspec,
                  const((1, d)), const((1, d))],
        out_specs=pl.BlockSpec((RB_SUB, d), out_map),
        scratch_shapes=[pltpu.VMEM(((RB_NCHUNK + RB_GROUP) * 8 * RB_PITCH, LANES), F32),
                        pltpu.VMEM((8 * RB_SPITCH, LANES), F32),
                        pltpu.VMEM((RB_WSLOTS, d, 2 * D_EXPERT), BF16),
                        pltpu.VMEM((RB_WSLOTS, D_EXPERT, d), BF16),
                        pltpu.SMEM((N_EXPERTS,), jnp.int32),
                        pltpu.SemaphoreType.DMA((RB_WSLOTS, 2))])
    return pl.pallas_call(
        _rb_kernel,
        grid_spec=grid_spec,
        out_shape=jax.ShapeDtypeStruct((n, d), F32),
        compiler_params=_cparams(("arbitrary", "arbitrary")),
        name="moe_routed_ln",
    )(cnt, x1, modp, offs, wts, wgu_bf16, wd_bf16, ln2g, ln2b)


def _sample_in_kernel(x_ref, mod_ref, win_ref, ctx_ref, h0_ref, convw_ref, convb_ref,
                      wlo_ref, whi_ref, bgate_ref, lam_ref,
                      ylru_ref, q_ref, k_ref, v_ref, cstate_ref, hnew_ref):
    sh1 = mod_ref[0:DEC_BATCH, 0:D_MODEL]
    sc1 = mod_ref[0:DEC_BATCH, D_MODEL:2 * D_MODEL]
    h = x_ref[...] * (1.0 + sc1) + sh1
    z = _dot(h, win_ref[...], True)
    xb = z[:, :LRU_WIDTH]
    gate = z[:, LRU_WIDTH:2 * LRU_WIDTH]
    c0 = ctx_ref[:, 0, :]
    c1 = ctx_ref[:, 1, :]
    c2 = ctx_ref[:, 2, :]
    xc = (convb_ref[...] + convw_ref[0:1, :] * c0 + convw_ref[1:2, :] * c1
          + convw_ref[2:3, :] * c2 + convw_ref[3:4, :] * xb)
    cstate_ref[:, 0, :] = c1
    cstate_ref[:, 1, :] = c2
    cstate_ref[:, 2, :] = xb
    sp = _softplus(-lam_ref[...])
    a, bterm = _lru_gates(xc, wlo_ref[...], whi_ref[...], bgate_ref[...], sp, True)
    hn = a * h0_ref[...] + bterm
    hnew_ref[...] = hn
    ylru_ref[...] = hn * _gelu_tanh(gate)
    low = lax.broadcasted_iota(jnp.int32, (DEC_BATCH, LANES), 1) < HEAD_DIM
    for c in range(4):
        qc = z[:, 2 * LRU_WIDTH + LANES * c:2 * LRU_WIDTH + LANES * (c + 1)]
        q_ref[pl.ds(c, DEC_BATCH, stride=N_HEADS), :] = jnp.where(low, qc, 0.0)
        q_ref[pl.ds(c + 4, DEC_BATCH, stride=N_HEADS), :] = jnp.where(low, 0.0, qc)
    k_ref[...] = z[:, 2 * LRU_WIDTH + ATTN_WIDTH:2 * LRU_WIDTH + ATTN_WIDTH + KV_WIDTH]
    v_ref[...] = z[:, 2 * LRU_WIDTH + ATTN_WIDTH + KV_WIDTH:]


def _sample_in(x, mod, w_in_p, ctx, h0, conv_w, conv_b, wlo, whi, bgate, lam):
    n = DEC_BATCH
    outs = [jax.ShapeDtypeStruct((n, LRU_WIDTH), F32),
            jax.ShapeDtypeStruct((n * N_HEADS, LANES), F32),
            jax.ShapeDtypeStruct((n, KV_WIDTH), F32),
            jax.ShapeDtypeStruct((n, KV_WIDTH), F32),
            jax.ShapeDtypeStruct((n, CONV_WIDTH - 1, LRU_WIDTH), F32),
            jax.ShapeDtypeStruct((n, LRU_WIDTH), F32)]
    return pl.pallas_call(
        _sample_in_kernel,
        out_shape=outs,
        compiler_params=pltpu.CompilerParams(vmem_limit_bytes=VMEM_LIMIT),
        name="sample_inproj_rglru",
    )(x, mod, w_in_p, ctx, h0, conv_w, conv_b, wlo, whi, bgate, lam)


def _sample_attn_kernel(q_ref, kn_ref, vn_ref, ck_ref, cv_ref, sink_ref, y_ref, nk_ref, nv_ref, *, bb):
    rows = lax.broadcasted_iota(jnp.int32, (WINDOW, KV_WIDTH), 0)
    nh = N_HEADS
    q_all = q_ref[...].reshape(bb * nh, LANES)
    kcat = ck_ref[...].reshape(bb * WINDOW, KV_WIDTH)
    vcat = cv_ref[...].reshape(bb * WINDOW, KV_WIDTH)
    kn_rep = jnp.broadcast_to(kn_ref[...][:, None, :], (bb, nh, KV_WIDTH)).reshape(bb * nh, KV_WIDTH)
    vn_rep = jnp.broadcast_to(vn_ref[...][:, None, :], (bb, nh, KV_WIDTH)).reshape(bb * nh, KV_WIDTH)
    sink = jnp.concatenate([sink_ref[...]] * bb, axis=0)
    s_full = _dot_nt(q_all, kcat, True)
    s = jnp.concatenate([s_full[nh * b:nh * (b + 1), WINDOW * b:WINDOW * (b + 1)] for b in range(bb)],
                        axis=0) * ATTN_SCALE
    s_self = jnp.sum(q_all * kn_rep, axis=-1, keepdims=True) * ATTN_SCALE
    m = jnp.maximum(jnp.maximum(jnp.max(s, axis=-1, keepdims=True), s_self), sink)
    e = jnp.exp(s - m)
    e_self = jnp.exp(s_self - m)
    den = jnp.sum(e, axis=-1, keepdims=True) + e_self + jnp.exp(sink - m)
    inv = 1.0 / den
    p = e * inv
    zero = jnp.zeros((nh, WINDOW), F32)
    p_wide = jnp.concatenate(
        [jnp.concatenate([p[nh * b:nh * (b + 1)] if c == b else zero for c in range(bb)], axis=-1)
         for b in range(bb)], axis=0)
    o = _dot(p_wide, vcat, True) + (e_self * inv) * vn_rep
    y_ref[...] = o.reshape(bb, nh, LANES)
    for b in range(bb):
        nk_ref[b] = jnp.where(rows == WINDOW - 1, kn_ref[b:b + 1, :], pltpu.roll(ck_ref[b], WINDOW - 1, axis=0))
        nv_ref[b] = jnp.where(rows == WINDOW - 1, vn_ref[b:b + 1, :], pltpu.roll(cv_ref[b], WINDOW - 1, axis=0))


def _sample_attn(q3, kn, vn, cache_k, cache_v, sinks, bb=16):
    n = DEC_BATCH
    kern = functools.partial(_sample_attn_kernel, bb=bb)
    return pl.pallas_call(
        kern,
        grid=(n // bb,),
        in_specs=[pl.BlockSpec((bb, N_HEADS, LANES), lambda i: (i, 0, 0)),
                  pl.BlockSpec((bb, KV_WIDTH), lambda i: (i, 0)),
                  pl.BlockSpec((bb, KV_WIDTH), lambda i: (i, 0)),
                  pl.BlockSpec((bb, WINDOW, KV_WIDTH), lambda i: (i, 0, 0)),
                  pl.BlockSpec((bb, WINDOW, KV_WIDTH), lambda i: (i, 0, 0)),
                  pl.BlockSpec((N_HEADS, 1), lambda i: (0, 0))],
        out_specs=[pl.BlockSpec((bb, N_HEADS, LANES), lambda i: (i, 0, 0)),
                   pl.BlockSpec((bb, WINDOW, KV_WIDTH), lambda i: (i, 0, 0)),
                   pl.BlockSpec((bb, WINDOW, KV_WIDTH), lambda i: (i, 0, 0))],
        out_shape=[jax.ShapeDtypeStruct((n, N_HEADS, LANES), F32),
                   jax.ShapeDtypeStruct((n, WINDOW, KV_WIDTH), F32),
                   jax.ShapeDtypeStruct((n, WINDOW, KV_WIDTH), F32)],
        compiler_params=_cparams(("arbitrary",)),
        name="sample_cache_attention",
    )(q3, kn, vn, cache_k, cache_v, sinks.reshape(N_HEADS, 1))


def _sample_out_kernel(x_ref, ylru_ref, yatt_ref, mod_ref, glru_ref, gattn_ref, wout_ref,
                       ln1g_ref, ln1b_ref, wr_ref, br_ref, x1_ref, comb_ref):
    low = lax.broadcasted_iota(jnp.int32, (DEC_BATCH, LANES), 1) < HEAD_DIM
    yatt = jnp.concatenate(
        [jnp.where(low, yatt_ref[pl.ds(c, DEC_BATCH, stride=N_HEADS), :],
                   yatt_ref[pl.ds(c + 4, DEC_BATCH, stride=N_HEADS), :]) for c in range(4)], axis=-1)
    gt1 = mod_ref[0:DEC_BATCH, 2 * D_MODEL:3 * D_MODEL]
    sh2 = mod_ref[0:DEC_BATCH, 3 * D_MODEL:4 * D_MODEL]
    sc2 = mod_ref[0:DEC_BATCH, 4 * D_MODEL:5 * D_MODEL]
    x1, comb = _outproj_body(x_ref[...], ylru_ref[...], yatt, sh2, sc2, gt1,
                             glru_ref[...], gattn_ref[...], wout_ref[...], ln1g_ref[...], ln1b_ref[...],
                             wr_ref[...], br_ref[...], True)
    x1_ref[...] = x1
    comb_ref[...] = comb


def _sample_out(x, ylru, yatt2d, mod, glru, gattn, wout_p, ln1g, ln1b, wr, br):
    n = DEC_BATCH
    return pl.pallas_call(
        _sample_out_kernel,
        out_shape=[jax.ShapeDtypeStruct((n, D_MODEL), F32), jax.ShapeDtypeStruct((n, ROUTE_LANES), F32)],
        compiler_params=pltpu.CompilerParams(vmem_limit_bytes=VMEM_LIMIT),
        name="sample_outproj_ln_route",
    )(x, ylru, yatt2d, mod, glru, gattn, wout_p, ln1g, ln1b, wr, br)


def _block_diag_halves(w_a, w_x):
    def bd(w4):
        eye = jnp.eye(4, dtype=w4.dtype)
        return (w4[:, :, None, :] * eye[:, None, :, None]).reshape(256, 256)
    lo = jnp.concatenate([bd(w_a[:4]), bd(w_x[:4])], axis=1)
    hi = jnp.concatenate([bd(w_a[4:]), bd(w_x[4:])], axis=1)
    return lo, hi


def kernel(x_prompt, x_sample, c_prompt, c_sample, state_conv, state_h, cache_k, cache_v, w_ada, b_ada, w_in,
           conv_w, conv_b, w_rg_a, b_rg_a, w_rg_x, b_rg_x, lru_lambda, sinks, g_lru, g_attn, w_out, ln1_g, ln1_b,
           w_group, b_group, w_router, b_router, w_gate, w_up, w_down, ln2_g, ln2_b):
    d = D_MODEL
    perm = jnp.asarray(HEAD_PERM)
    q0 = 2 * LRU_WIDTH
    in_cols = np.concatenate([np.arange(q0), q0 + HEAD_PERM, np.arange(q0 + ATTN_WIDTH, IN_WIDTH)])
    w_in_p = w_in[0][:, jnp.asarray(in_cols)]
    out_rows = np.concatenate([np.arange(LRU_WIDTH), LRU_WIDTH + HEAD_PERM])
    w_out_p = w_out[0][jnp.asarray(out_rows)]
    g_attn_p = g_attn[0][perm].reshape(1, -1)
    glru = g_lru[0].reshape(1, -1)
    wlo, whi = _block_diag_halves(w_rg_a[0], w_rg_x[0])
    bgate = jnp.concatenate([b_rg_a[0].reshape(-1), b_rg_x[0].reshape(-1)]).reshape(1, -1)
    lam = lru_lambda[0].reshape(1, -1)
    convw = conv_w[0]
    convb = conv_b[0].reshape(1, -1)
    ln1g, ln1b = ln1_g[0].reshape(1, -1), ln1_b[0].reshape(1, -1)
    ln2g, ln2b = ln2_g[0].reshape(1, -1), ln2_b[0].reshape(1, -1)
    wr = jnp.concatenate([jnp.transpose(w_router[0], (1, 0, 2)).reshape(d, N_EXPERTS), w_group[0],
                          jnp.zeros((d, ROUTE_LANES - N_EXPERTS - N_GROUPS), F32)], axis=1)
    br = jnp.concatenate([b_router[0].reshape(-1), b_group[0],
                          jnp.zeros((ROUTE_LANES - N_EXPERTS - N_GROUPS,), F32)]).reshape(1, -1)
    sink_p = sinks[0]

    c_all = jnp.concatenate([c_sample, c_prompt, jnp.zeros((8 - BATCH, d), F32)], axis=0)
    mod = _ada(c_all, w_ada[0], b_ada[0])
    modp = mod[DEC_BATCH:DEC_BATCH + BATCH].reshape(BATCH, 6, d)

    zlru, zqkv, kvlast = _inproj(x_prompt, modp, w_in_p.astype(BF16))
    ylru, cstate8, hlast8, wgu_b, wd_b = _lru(zlru, convw, convb, wlo.astype(BF16), whi.astype(BF16), bgate, lam,
                                                   w_gate[0], w_up[0], w_down[0])
    yatt = _attn(zqkv, sink_p)
    n_p = BATCH * SEQ
    x1_p, info, cntf = _outproj_prompt(x_prompt.reshape(n_p, d), ylru.reshape(n_p, LRU_WIDTH),
                                       yatt.reshape(n_p, ATTN_WIDTH), modp, glru, g_attn_p, w_out_p.astype(BF16),
                                       ln1g, ln1b, wr, br, tm=OUTPROJ_TILE)
    cnt = cntf[:, 0, :N_EXPERTS].astype(jnp.int32)
    offs = _rb_retile(_rb_offsets(cnt, info[:, 0:2].astype(jnp.int32), info[:, 4:6].astype(jnp.int32)))
    y_p = _rb_moe(x1_p, modp, cnt, offs, _rb_retile(info[:, 2:4]), wgu_b, wd_b, ln2g, ln2b)

    ylru_s, q2d, kn, vn, cstate_s, hnew_s = _sample_in(
        x_sample.reshape(DEC_BATCH, d), mod, w_in_p, state_conv[0], state_h[0],
        convw, convb, wlo, whi, bgate, lam)
    yatt3, newk, newv = _sample_attn(q2d.reshape(DEC_BATCH, N_HEADS, LANES), kn, vn,
                                     cache_k[0].reshape(DEC_BATCH, WINDOW, KV_WIDTH),
                                     cache_v[0].reshape(DEC_BATCH, WINDOW, KV_WIDTH), sink_p)
    x1_s, comb_s = _sample_out(x_sample.reshape(DEC_BATCH, d), ylru_s, yatt3.reshape(DEC_BATCH * N_HEADS, LANES),
                               mod, glru, g_attn_p, w_out_p, ln1g, ln1b, wr, br)
    y_s = _moe_dense(x1_s, comb_s, mod, wgu_b, wd_b, ln2g, ln2b, DEC_BATCH)

    return (y_p.reshape(BATCH, SEQ, d),
            y_s.reshape(DEC_BATCH, 1, d),
            cstate8[:, 5:8][None],
            hlast8[:, 7][None],
            kvlast[:, :, :KV_WIDTH].reshape(1, BATCH, WINDOW, N_KV_HEADS, HEAD_DIM),
            kvlast[:, :, KV_WIDTH:].reshape(1, BATCH, WINDOW, N_KV_HEADS, HEAD_DIM),
            cstate_s[None],
            hnew_s[None],
            newk.reshape(1, DEC_BATCH, WINDOW, N_KV_HEADS, HEAD_DIM),
            newv.reshape(1, DEC_BATCH, WINDOW, N_KV_HEADS, HEAD_DIM))
```

```python
import functools

import jax
import jax.numpy as jnp
import numpy as np
from jax import lax
from jax.experimental import pallas as pl
from jax.experimental.pallas import tpu as pltpu

F32 = jnp.float32
BF16 = jnp.bfloat16
HIGHEST = lax.Precision.HIGHEST

D_MODEL = 1024
BATCH = 4
SEQ = 4096
DEC_BATCH = 128
LRU_WIDTH = 512
LRU_BLOCKS = 8
LRU_BLOCK = 64
CONV_WIDTH = 4
LRU_C = 8.0
N_HEADS = 8
N_KV_HEADS = 2
HEAD_DIM = 64
ATTN_WIDTH = 512
KV_WIDTH = 128
WINDOW = 128
IN_WIDTH = 2 * LRU_WIDTH + ATTN_WIDTH + 2 * KV_WIDTH
N_GROUPS = 4
EXPERTS_PER_GROUP = 8
N_EXPERTS = 32
D_EXPERT = 256
DEEPNORM_ALPHA = 2.0 ** 0.25
LN_EPS = 1e-5
RMS_EPS = 1e-6
ATTN_SCALE = HEAD_DIM ** -0.5

LANES = 128
ROUTE_LANES = 128
ROUTE_INFO = 40
VMEM_LIMIT = 56 * 1024 * 1024


def _cparams(sem):
    return pltpu.CompilerParams(dimension_semantics=sem, vmem_limit_bytes=VMEM_LIMIT)


def _dot(a, b, exact):
    if exact:
        return jnp.dot(a, b, precision=HIGHEST, preferred_element_type=F32)
    return jnp.dot(a.astype(BF16), b.astype(BF16), preferred_element_type=F32)


def _dot_nt(a, b, exact):
    dn = (((1,), (1,)), ((), ()))
    if exact:
        return lax.dot_general(a, b, dn, precision=HIGHEST, preferred_element_type=F32)
    return lax.dot_general(a.astype(BF16), b.astype(BF16), dn, preferred_element_type=F32)


def _sigmoid(x):
    return 1.0 / (1.0 + jnp.exp(-x))


def _silu(x):
    return x * _sigmoid(x)


def _gelu_tanh(x):
    return 0.5 * x * (1.0 + jnp.tanh(np.sqrt(2.0 / np.pi).astype(np.float32) * (x + 0.044715 * (x * x * x))))


def _softplus(x):
    return jnp.maximum(x, 0.0) + jnp.log1p(jnp.exp(-jnp.abs(x)))


def _layer_norm(x, g, b):
    mu = jnp.mean(x, axis=-1, keepdims=True)
    xc = x - mu
    var = jnp.mean(xc * xc, axis=-1, keepdims=True)
    return xc * lax.rsqrt(var + LN_EPS) * g + b


def _rms_norm(x, g):
    return x * lax.rsqrt(jnp.mean(x * x, axis=-1, keepdims=True) + RMS_EPS) * g


def _ada_kernel(c_ref, w_ref, b_ref, o_ref):
    o_ref[...] = _dot(_silu(c_ref[...]), w_ref[...], True) + b_ref[...]


def _ada(c_all, w_ada, b_ada):
    rows = c_all.shape[0]
    bn = 1024
    return pl.pallas_call(
        _ada_kernel,
        grid=(6 * D_MODEL // bn,),
        in_specs=[pl.BlockSpec((rows, D_MODEL), lambda j: (0, 0)),
                  pl.BlockSpec((D_MODEL, bn), lambda j: (0, j)),
                  pl.BlockSpec((1, bn), lambda j: (0, j))],
        out_specs=pl.BlockSpec((rows, bn), lambda j: (0, j)),
        out_shape=jax.ShapeDtypeStruct((rows, 6 * D_MODEL), F32),
        compiler_params=_cparams(("arbitrary",)),
        name="ada_modulation",
    )(c_all, w_ada, b_ada.reshape(1, -1))


QKV_WIDTH = ATTN_WIDTH + 2 * KV_WIDTH


def _inproj_kernel(x_ref, mod_ref, w_ref, lru_ref, qkv_ref, kvlast_ref):
    sh1 = mod_ref[0, 0:1, :]
    sc1 = mod_ref[0, 1:2, :]
    h = x_ref[0] * (1.0 + sc1) + sh1
    z = _dot(h, w_ref[...], False)
    lru_ref[0] = z[:, :2 * LRU_WIDTH]
    qkv_ref[0] = z[:, 2 * LRU_WIDTH:].astype(BF16)
    kvlast_ref[0] = z[z.shape[0] - WINDOW:, 2 * LRU_WIDTH + ATTN_WIDTH:]


def _inproj(x, modp, w_in_bf16, tm=1024):
    b, t, d = x.shape
    return pl.pallas_call(
        _inproj_kernel,
        grid=(b, t // tm),
        in_specs=[pl.BlockSpec((1, tm, d), lambda i, j: (i, j, 0)),
                  pl.BlockSpec((1, 6, d), lambda i, j: (i, 0, 0)),
                  pl.BlockSpec((d, IN_WIDTH), lambda i, j: (0, 0))],
        out_specs=[pl.BlockSpec((1, tm, 2 * LRU_WIDTH), lambda i, j: (i, j, 0)),
                   pl.BlockSpec((1, tm, QKV_WIDTH), lambda i, j: (i, j, 0)),
                   pl.BlockSpec((1, WINDOW, 2 * KV_WIDTH), lambda i, j: (i, 0, 0))],
        out_shape=[jax.ShapeDtypeStruct((b, t, 2 * LRU_WIDTH), F32),
                   jax.ShapeDtypeStruct((b, t, QKV_WIDTH), BF16),
                   jax.ShapeDtypeStruct((b, WINDOW, 2 * KV_WIDTH), F32)],
        compiler_params=_cparams(("arbitrary", "arbitrary")),
        name="prompt_inproj",
    )(x, modp, w_in_bf16)


def _lru_gates(xc, wlo, whi, bgate, sp_neg_lam, exact):
    g_lo = _dot(xc[:, :256], wlo, exact)
    g_hi = _dot(xc[:, 256:], whi, exact)
    ga = jnp.concatenate([g_lo[:, :256], g_hi[:, :256]], axis=-1) + bgate[:, :LRU_WIDTH]
    gx = jnp.concatenate([g_lo[:, 256:], g_hi[:, 256:]], axis=-1) + bgate[:, LRU_WIDTH:]
    r = _sigmoid(ga)
    i = _sigmoid(gx)
    log_a = -LRU_C * r * sp_neg_lam
    a = jnp.exp(log_a)
    one_minus_a2 = -jnp.tanh(log_a) * (a * a + 1.0) if exact else 1.0 - a * a
    bterm = jnp.sqrt(one_minus_a2) * (i * xc)
    return a, bterm


def _lru_kernel(z_ref, convw_ref, convb_ref, wlo_ref, whi_ref, bgate_ref, lam_ref, wg_ref, wu_ref, wd_ref,
                y_ref, cstate_ref, hlast_ref, wgub_ref, wdb_ref, tail_ref, carry_ref, *, tl):
    j = pl.program_id(1)

    @pl.when(j == 0)
    def _():
        tail_ref[...] = jnp.zeros_like(tail_ref)
        carry_ref[...] = jnp.zeros_like(carry_ref)

    wgub_ref[0, :, :D_EXPERT] = wg_ref[0].astype(BF16)
    wgub_ref[0, :, D_EXPERT:] = wu_ref[0].astype(BF16)
    wdb_ref[...] = wd_ref[...].astype(BF16)

    xb = z_ref[0, :, :LRU_WIDTH]
    gate = z_ref[0, :, LRU_WIDTH:]
    xc = convb_ref[...] + convw_ref[3:4, :] * xb
    rows8 = lax.broadcasted_iota(jnp.int32, (8, LRU_WIDTH), 0)
    tail = tail_ref[...]
    for back in (1, 2, 3):
        rolled = pltpu.roll(xb, back, axis=0)
        top = jnp.where(rows8 >= back, rolled[:8], pltpu.roll(tail, back, axis=0))
        shifted = jnp.concatenate([top, rolled[8:]], axis=0)
        xc = xc + convw_ref[3 - back:4 - back, :] * shifted
    tail_ref[...] = xb[tl - 8:, :]
    cstate_ref[0] = xb[tl - 8:, :]

    sp = _softplus(-lam_ref[...])
    a, bterm = _lru_gates(xc, wlo_ref[...], whi_ref[...], bgate_ref[...], sp, False)

    groups = tl // 8
    a = a.reshape(groups, 8, LRU_WIDTH)
    bterm = bterm.reshape(groups, 8, LRU_WIDTH)
    r8 = lax.broadcasted_iota(jnp.int32, (groups, 8, LRU_WIDTH), 1)
    s = 1
    while s < 8:
        a_sh = jnp.where(r8 >= s, pltpu.roll(a, s, axis=1), 1.0)
        b_sh = jnp.where(r8 >= s, pltpu.roll(bterm, s, axis=1), 0.0)
        bterm = a * b_sh + bterm
        a = a * a_sh
        s *= 2
    a_tot = jnp.broadcast_to(a[:, 7:8, :], (groups, 8, LRU_WIDTH))
    b_tot = jnp.broadcast_to(bterm[:, 7:8, :], (groups, 8, LRU_WIDTH))
    h_in = jnp.broadcast_to(carry_ref[7:8, :], (8, LRU_WIDTH))
    pieces = []
    for g in range(groups):
        pieces.append(a[g] * h_in + bterm[g])
        h_in = a_tot[g] * h_in + b_tot[g]
    h = jnp.concatenate(pieces, axis=0)
    carry_ref[...] = h_in
    hlast_ref[0] = h_in
    y_ref[0] = h * _gelu_tanh(gate)


def _lru(zin, conv_w, conv_b, wlo, whi, bgate, lam, w_gate, w_up, w_down, tl=512):
    b, t, _ = zin.shape
    steps = t // tl
    assert b * steps == N_EXPERTS
    d = D_MODEL
    kern = functools.partial(_lru_kernel, tl=tl)
    full = lambda shp: pl.BlockSpec(shp, lambda i, j: tuple(0 for _ in shp))
    per_step = lambda shp: pl.BlockSpec(shp, lambda i, j: (i * steps + j, 0, 0))
    return pl.pallas_call(
        kern,
        grid=(b, steps),
        in_specs=[pl.BlockSpec((1, tl, 2 * LRU_WIDTH), lambda i, j: (i, j, 0)),
                  full((CONV_WIDTH, LRU_WIDTH)), full((1, LRU_WIDTH)),
                  full((256, 512)), full((256, 512)), full((1, 2 * LRU_WIDTH)), full((1, LRU_WIDTH)),
                  per_step((1, d, D_EXPERT)), per_step((1, d, D_EXPERT)), per_step((1, D_EXPERT, d))],
        out_specs=[pl.BlockSpec((1, tl, LRU_WIDTH), lambda i, j: (i, j, 0)),
                   pl.BlockSpec((1, 8, LRU_WIDTH), lambda i, j: (i, 0, 0)),
                   pl.BlockSpec((1, 8, LRU_WIDTH), lambda i, j: (i, 0, 0)),
                   per_step((1, d, 2 * D_EXPERT)), per_step((1, D_EXPERT, d))],
        out_shape=[jax.ShapeDtypeStruct((b, t, LRU_WIDTH), F32),
                   jax.ShapeDtypeStruct((b, 8, LRU_WIDTH), F32),
                   jax.ShapeDtypeStruct((b, 8, LRU_WIDTH), F32),
                   jax.ShapeDtypeStruct((N_EXPERTS, d, 2 * D_EXPERT), BF16),
                   jax.ShapeDtypeStruct(w_down.shape, BF16)],
        scratch_shapes=[pltpu.VMEM((8, LRU_WIDTH), F32), pltpu.VMEM((8, LRU_WIDTH), F32)],
        compiler_params=_cparams(("arbitrary", "arbitrary")),
        name="prompt_rglru",
    )(zin, conv_w, conv_b, wlo, whi, bgate, lam, w_gate, w_up, w_down)


ATTN_BLOCKS = 16


def _attn_kernel(q_ref, k_ref, v_ref, sink_ref, o_ref, kprev_ref, vprev_ref):
    j = pl.program_id(1)

    @pl.when(j == 0)
    def _():
        kprev_ref[...] = jnp.zeros_like(kprev_ref)
        vprev_ref[...] = jnp.zeros_like(vprev_ref)

    blk = WINDOW
    lane = lax.broadcasted_iota(jnp.int32, (blk, LANES), 1)
    low = lane < HEAD_DIM
    qi = lax.broadcasted_iota(jnp.int32, (blk, 2 * blk), 0)
    sj = lax.broadcasted_iota(jnp.int32, (blk, 2 * blk), 1)
    rel = blk + qi - sj
    in_window = (rel >= 0) & (rel <= WINDOW)
    sink = sink_ref[...].reshape(N_HEADS, blk, 1)
    k_ext = jnp.concatenate([kprev_ref[...], k_ref[0]], axis=0)
    v_ext = jnp.concatenate([vprev_ref[...], v_ref[0]], axis=0)
    v_ext = jnp.concatenate([v_ext, jnp.ones_like(v_ext)], axis=-1)
    for n in range(ATTN_BLOCKS):
        q = q_ref[0, blk * n:blk * (n + 1), :]
        pieces = []
        for half in (0, 1):
            for c in range(4):
                qc = q[:, LANES * c:LANES * (c + 1)]
                pieces.append(jnp.where(low if half == 0 else ~low, qc, 0.0).astype(BF16))
        q8 = jnp.concatenate(pieces, axis=0)
        k_band = k_ext[blk * n:blk * (n + 2)]
        v_band = v_ext[blk * n:blk * (n + 2)]
        s = _dot_nt(q8, k_band, False) * ATTN_SCALE
        s = s.reshape(N_HEADS, blk, 2 * blk)
        valid = in_window & ((sj >= blk) | (j > 0)) if n == 0 else in_window
        s = jnp.where(valid[None], s, -jnp.inf)
        m = jnp.maximum(jnp.max(s, axis=-1, keepdims=True), sink)
        e = jnp.exp(s - m).reshape(N_HEADS * blk, 2 * blk)
        ov = _dot(e, v_band, False)
        den = ov[:, KV_WIDTH:] + jnp.exp(sink - m).reshape(N_HEADS * blk, 1)
        o8 = ov[:, :KV_WIDTH] * (1.0 / den)
        cols = []
        for c in range(4):
            cols.append(jnp.where(low, o8[blk * c:blk * (c + 1)], o8[blk * (c + 4):blk * (c + 5)]))
        o_ref[0, blk * n:blk * (n + 1), :] = jnp.concatenate(cols, axis=-1)
    kprev_ref[...] = k_ref[0, blk * (ATTN_BLOCKS - 1):, :]
    vprev_ref[...] = v_ref[0, blk * (ATTN_BLOCKS - 1):, :]


def _attn(qkv, sinks):
    b, t, _ = qkv.shape
    blk = WINDOW
    tq = blk * ATTN_BLOCKS
    sink_col = jnp.repeat(sinks.astype(F32), blk).reshape(N_HEADS * blk, 1)
    kcol = ATTN_WIDTH // KV_WIDTH
    return pl.pallas_call(
        _attn_kernel,
        grid=(b, t // tq),
        in_specs=[pl.BlockSpec((1, tq, ATTN_WIDTH), lambda i, j: (i, j, 0)),
                  pl.BlockSpec((1, tq, KV_WIDTH), lambda i, j: (i, j, kcol)),
                  pl.BlockSpec((1, tq, KV_WIDTH), lambda i, j: (i, j, kcol + 1)),
                  pl.BlockSpec((N_HEADS * blk, 1), lambda i, j: (0, 0))],
        out_specs=pl.BlockSpec((1, tq, ATTN_WIDTH), lambda i, j: (i, j, 0)),
        out_shape=jax.ShapeDtypeStruct((b, t, ATTN_WIDTH), F32),
        scratch_shapes=[pltpu.VMEM((blk, KV_WIDTH), BF16), pltpu.VMEM((blk, KV_WIDTH), BF16)],
        compiler_params=_cparams(("arbitrary", "arbitrary")),
        name="prompt_window_attention",
    )(qkv, qkv, qkv, sink_col)


def _route(h2, wr, br, exact):
    t = h2.shape[0]
    logits = _dot(h2, wr, exact) + br
    lane = lax.broadcasted_iota(jnp.int32, (t, ROUTE_LANES), 1).astype(F32)
    neg = -jnp.inf
    big = float(ROUTE_LANES)
    is_g = (lane >= N_EXPERTS) & (lane < N_EXPERTS + N_GROUPS)
    lg = jnp.where(is_g, logits, neg)
    mg = jnp.max(lg, axis=-1, keepdims=True)
    eg = jnp.where(is_g, jnp.exp(lg - mg), 0.0)
    pg = eg / jnp.sum(eg, axis=-1, keepdims=True)
    g_val = jnp.max(pg, axis=-1, keepdims=True)
    g_lane = jnp.min(jnp.where((pg == g_val) & is_g, lane, big), axis=-1, keepdims=True)
    g_idx = g_lane - N_EXPERTS
    in_grp = (lane >= g_idx * EXPERTS_PER_GROUP) & (lane < (g_idx + 1.0) * EXPERTS_PER_GROUP)
    le = jnp.where(in_grp, logits, neg)
    me = jnp.max(le, axis=-1, keepdims=True)
    ee = jnp.where(in_grp, jnp.exp(le - me), 0.0)
    pe = ee / jnp.sum(ee, axis=-1, keepdims=True)
    v1 = jnp.max(pe, axis=-1, keepdims=True)
    l1 = jnp.min(jnp.where((pe == v1) & in_grp, lane, big), axis=-1, keepdims=True)
    rest = in_grp & (lane != l1)
    pe2 = jnp.where(rest, pe, -1.0)
    v2 = jnp.max(pe2, axis=-1, keepdims=True)
    l2 = jnp.min(jnp.where((pe2 == v2) & rest, lane, big), axis=-1, keepdims=True)
    tot = v1 + v2
    w1 = g_val * v1 / tot
    w2 = g_val * v2 / tot
    comb = jnp.where(lane == l1, w1, 0.0) + jnp.where(lane == l2, w2, 0.0)
    return (comb + jnp.where(lane == ROUTE_INFO, l1, 0.0) + jnp.where(lane == ROUTE_INFO + 1, l2, 0.0)
            + jnp.where(lane == ROUTE_INFO + 2, w1, 0.0) + jnp.where(lane == ROUTE_INFO + 3, w2, 0.0))


def _outproj_body(x, ylru, yatt, sh2, sc2, gt1, glru, gattn, wout, ln1g, ln1b, wr, br, exact):
    mixin = jnp.concatenate([_rms_norm(ylru, glru), _rms_norm(yatt, gattn)], axis=-1)
    mix = _dot(mixin, wout, exact)
    x1 = _layer_norm(DEEPNORM_ALPHA * x + (1.0 + gt1) * mix, ln1g, ln1b)
    h2 = x1 * (1.0 + sc2) + sh2
    return x1, _route(h2, wr, br, exact)


def _outproj_prompt_kernel(x_ref, ylru_ref, yatt_ref, mod_ref, glru_ref, gattn_ref, wout_ref,
                           ln1g_ref, ln1b_ref, wr_ref, br_ref, x1_ref, info_ref, cnt_ref, tri_ref, carry_ref,
                           *, tm, per_seq):
    i = pl.program_id(0)

    @pl.when(i == 0)
    def _():
        r = lax.broadcasted_iota(jnp.int32, (tm, tm), 0)
        c = lax.broadcasted_iota(jnp.int32, (tm, tm), 1)
        tri_ref[...] = jnp.where(c < r, 1.0, 0.0).astype(BF16)

    @pl.when(i % per_seq == 0)
    def _():
        carry_ref[...] = jnp.zeros_like(carry_ref)

    gt1 = mod_ref[0, 2:3, :]
    sh2 = mod_ref[0, 3:4, :]
    sc2 = mod_ref[0, 4:5, :]
    combs = []
    nsplit = 2
    for h in range(nsplit):
        rows = slice(h * (tm // nsplit), (h + 1) * (tm // nsplit))
        x1_h, comb_h = _outproj_body(x_ref[rows, :], ylru_ref[rows, :], yatt_ref[rows, :], sh2, sc2, gt1,
                                     glru_ref[...], gattn_ref[...], wout_ref[...], ln1g_ref[...], ln1b_ref[...],
                                     wr_ref[...], br_ref[...], False)
        x1_ref[rows, :] = x1_h
        combs.append(comb_h)
    comb = jnp.concatenate(combs, axis=0)
    lane = lax.broadcasted_iota(jnp.int32, (tm, ROUTE_LANES), 1).astype(F32)
    l1 = jnp.sum(jnp.where(lane == ROUTE_INFO, comb, 0.0), axis=-1, keepdims=True)
    l2 = jnp.sum(jnp.where(lane == ROUTE_INFO + 1, comb, 0.0), axis=-1, keepdims=True)
    o1 = lane == l1
    o2 = lane == l2
    onehot = jnp.where(o1 | o2, 1.0, 0.0)
    before = jnp.dot(tri_ref[...], onehot.astype(BF16), preferred_element_type=F32) + carry_ref[0:1, :]
    rank1 = jnp.sum(jnp.where(o1, before, 0.0), axis=-1, keepdims=True)
    rank2 = jnp.sum(jnp.where(o2, before, 0.0), axis=-1, keepdims=True)
    total = carry_ref[0:1, :] + jnp.sum(onehot, axis=0, keepdims=True)
    carry_ref[...] = jnp.broadcast_to(total, carry_ref.shape)
    cnt_ref[0] = jnp.broadcast_to(total, (8, ROUTE_LANES))
    info = (comb + jnp.where(lane == ROUTE_INFO + 4, rank1, 0.0) + jnp.where(lane == ROUTE_INFO + 5, rank2, 0.0))
    info_ref[0] = jnp.transpose(info)[ROUTE_INFO:ROUTE_INFO + 8, :]


OUTPROJ_TILE = 1024


def _outproj_prompt(x2d, ylru2d, yatt2d, modp, glru, gattn, wout_bf16, ln1g, ln1b, wr, br, tm=OUTPROJ_TILE):
    n, d = x2d.shape
    per_seq = SEQ // tm
    full = lambda shp: pl.BlockSpec(shp, lambda i: tuple(0 for _ in shp))
    kern = functools.partial(_outproj_prompt_kernel, tm=tm, per_seq=per_seq)
    return pl.pallas_call(
        kern,
        grid=(n // tm,),
        in_specs=[pl.BlockSpec((tm, d), lambda i: (i, 0)),
                  pl.BlockSpec((tm, LRU_WIDTH), lambda i: (i, 0)),
                  pl.BlockSpec((tm, ATTN_WIDTH), lambda i: (i, 0)),
                  pl.BlockSpec((1, 6, d), lambda i: (i // per_seq, 0, 0)),
                  full((1, LRU_WIDTH)), full((1, ATTN_WIDTH)), full((d, d)),
                  full((1, d)), full((1, d)), full((d, ROUTE_LANES)), full((1, ROUTE_LANES))],
        out_specs=[pl.BlockSpec((tm, d), lambda i: (i, 0)),
                   pl.BlockSpec((1, 8, tm), lambda i: (i, 0, 0)),
                   pl.BlockSpec((1, 8, ROUTE_LANES), lambda i: (i // per_seq, 0, 0))],
        out_shape=[jax.ShapeDtypeStruct((n, d), F32),
                   jax.ShapeDtypeStruct((n // tm, 8, tm), F32),
                   jax.ShapeDtypeStruct((n // SEQ, 8, ROUTE_LANES), F32)],
        scratch_shapes=[pltpu.VMEM((tm, tm), BF16), pltpu.VMEM((8, ROUTE_LANES), F32)],
        compiler_params=_cparams(("arbitrary",)),
        name="prompt_outproj_ln_route",
    )(x2d, ylru2d, yatt2d, modp, glru, gattn, wout_bf16, ln1g, ln1b, wr, br)


DENSE_EXPERTS_PER_STEP = 8


def _moe_kernel(x1_ref, comb_ref, sh2_ref, sc2_ref, gt2_ref, wgu_ref, wd_ref, ln2g_ref, ln2b_ref,
                o_ref, h2_ref, acc_ref):
    g = pl.program_id(1)

    @pl.when(g == 0)
    def _():
        h2_ref[...] = (x1_ref[...] * (1.0 + sc2_ref[...]) + sh2_ref[...]).astype(BF16)
        acc_ref[...] = jnp.zeros_like(acc_ref)

    h2 = h2_ref[...]
    comb = comb_ref[...]
    lane = lax.broadcasted_iota(jnp.int32, comb.shape, 1)
    part = None
    for k in range(DENSE_EXPERTS_PER_STEP):
        au = jnp.dot(h2, wgu_ref[k], preferred_element_type=F32)
        c_e = jnp.sum(jnp.where(lane == g * DENSE_EXPERTS_PER_STEP + k, comb, 0.0), axis=-1, keepdims=True)
        z = _silu(au[:, :D_EXPERT]) * au[:, D_EXPERT:] * c_e
        y = jnp.dot(z.astype(BF16), wd_ref[k], preferred_element_type=F32)
        part = y if part is None else part + y
    acc_ref[...] += part

    @pl.when(g == N_EXPERTS // DENSE_EXPERTS_PER_STEP - 1)
    def _():
        o_ref[...] = _layer_norm(DEEPNORM_ALPHA * x1_ref[...] + (1.0 + gt2_ref[...]) * acc_ref[...],
                                 ln2g_ref[...], ln2b_ref[...])


def _moe_dense(x1, comb, mod, wgu_bf16, wd_bf16, ln2g, ln2b, tm):
    n, d = x1.shape
    eg = DENSE_EXPERTS_PER_STEP
    mspec = lambda k: pl.BlockSpec((tm, d), lambda i, e: (i, k))
    full = lambda shp: pl.BlockSpec(shp, lambda i, e: tuple(0 for _ in shp))
    return pl.pallas_call(
        _moe_kernel,
        grid=(n // tm, N_EXPERTS // eg),
        in_specs=[pl.BlockSpec((tm, d), lambda i, e: (i, 0)),
                  pl.BlockSpec((tm, ROUTE_LANES), lambda i, e: (i, 0)),
                  mspec(3), mspec(4), mspec(5),
                  pl.BlockSpec((eg, d, 2 * D_EXPERT), lambda i, e: (e, 0, 0)),
                  pl.BlockSpec((eg, D_EXPERT, d), lambda i, e: (e, 0, 0)),
                  full((1, d)), full((1, d))],
        out_specs=pl.BlockSpec((tm, d), lambda i, e: (i, 0)),
        out_shape=jax.ShapeDtypeStruct((n, d), F32),
        scratch_shapes=[pltpu.VMEM((tm, d), BF16), pltpu.VMEM((tm, d), F32)],
        compiler_params=_cparams(("arbitrary", "arbitrary")),
        name="moe_dense_ln",
    )(x1, comb, mod, mod, mod, wgu_bf16, wd_bf16, ln2g, ln2b)


RB_SUB = 512
RB_NSUB = SEQ // RB_SUB
RB_CHUNK = 128
RB_CHUNK_BITS = 7
RB_NCHUNK = 2 * SEQ // RB_CHUNK
RB_PITCH = RB_CHUNK + 8
RB_SPITCH = RB_SUB + 8
RB_GROUP = 3
RB_WSLOTS = 4


def _rb_kernel(cnt_ref, x1_ref, mod_ref, offs_ref, wts_ref, wgu_hbm, wd_hbm, ln2g_ref, ln2b_ref,
               o_ref, buf_ref, stage_ref, wgu_buf, wd_buf, start_ref, sem):
    b = pl.program_id(0)
    s = pl.program_id(1)

    @pl.when(s == 0)
    def _starts():
        def body(e, run):
            start_ref[e] = run
            return run + cnt_ref[b, e]
        lax.fori_loop(0, N_EXPERTS, body, jnp.int32(0))
        buf_ref[RB_NCHUNK * 8 * RB_PITCH:(RB_NCHUNK + RB_GROUP) * 8 * RB_PITCH, :] = jnp.zeros(
            (RB_GROUP * 8 * RB_PITCH, LANES), F32)

    @pl.when(s < RB_NSUB)
    def _dispatch():
        sh2 = mod_ref[0, 3:4, :]
        sc2 = mod_ref[0, 4:5, :]
        h2 = x1_ref[...] * (1.0 + sc2) + sh2
        for j in range(8):
            stage_ref[RB_SPITCH * j:RB_SPITCH * j + RB_SUB, :] = h2[:, LANES * j:LANES * (j + 1)]

        for t in range(RB_SUB):
            slab = stage_ref[pl.ds(t, 8, stride=RB_SPITCH), :]
            for a in range(2):
                buf_ref[pl.ds(offs_ref[0, a, t], 8, stride=RB_PITCH), :] = slab

    @pl.when(s == RB_NSUB)
    def _experts():
        def copies(e, slot):
            return (pltpu.make_async_copy(wgu_hbm.at[e], wgu_buf.at[slot], sem.at[slot, 0]),
                    pltpu.make_async_copy(wd_hbm.at[e], wd_buf.at[slot], sem.at[slot, 1]))

        def run_expert(e, slot):
            lo_row = start_ref[e]
            hi_row = lo_row + cnt_ref[b, e]

            c_lo = lax.shift_right_logical(lo_row, RB_CHUNK_BITS)
            c_hi = lax.shift_right_logical(hi_row + (RB_CHUNK - 1), RB_CHUNK_BITS)
            row = lax.broadcasted_iota(jnp.int32, (RB_CHUNK, 1), 0)

            def load(c):
                base = pl.multiple_of(c * (8 * RB_PITCH), 8)
                return [buf_ref[pl.ds(base + RB_PITCH * j, RB_CHUNK), :] for j in range(8)]

            def store(c, tiles, y):
                base = pl.multiple_of(c * (8 * RB_PITCH), 8)
                mine = (row >= lo_row - c * RB_CHUNK) & (row < hi_row - c * RB_CHUNK)
                for j in range(8):
                    buf_ref[pl.ds(base + RB_PITCH * j, RB_CHUNK), :] = jnp.where(
                        mine, y[:, LANES * j:LANES * (j + 1)], tiles[j])

            def group(i, carry):
                cs = [c_lo + RB_GROUP * i]
                for k in range(1, RB_GROUP):
                    cs.append(jnp.where(cs[0] + k < c_hi, cs[0] + k, RB_NCHUNK + k))
                tiles = [load(c) for c in cs]
                x = jnp.concatenate([jnp.concatenate(t, axis=-1) for t in tiles], axis=0).astype(BF16)
                au = jnp.dot(x, wgu_buf[slot], preferred_element_type=F32)
                z = (_silu(au[:, :D_EXPERT]) * au[:, D_EXPERT:]).astype(BF16)
                y = jnp.dot(z, wd_buf[slot], preferred_element_type=F32)
                for k, c in enumerate(cs):
                    store(c, tiles[k], y[RB_CHUNK * k:RB_CHUNK * (k + 1)])
                return carry

            lax.fori_loop(0, lax.div(c_hi - c_lo + (RB_GROUP - 1), RB_GROUP), group, 0)

        for e in range(RB_WSLOTS - 1):
            for c in copies(e, e):
                c.start()

        def ring_body(i, carry):
            for k in range(RB_WSLOTS):
                e = RB_WSLOTS * i + k
                ahead = e + RB_WSLOTS - 1

                @pl.when(ahead < N_EXPERTS)
                def _():
                    for c in copies(ahead, (k + RB_WSLOTS - 1) % RB_WSLOTS):
                        c.start()
                for c in copies(e, k):
                    c.wait()
                run_expert(e, k)
            return carry
        lax.fori_loop(0, N_EXPERTS // RB_WSLOTS, ring_body, 0)

    @pl.when(s > RB_NSUB)
    def _combine():
        for t in range(RB_SUB):
            acc = None
            for a in range(2):
                term = wts_ref[0, a, t] * buf_ref[pl.ds(offs_ref[0, a, t], 8, stride=RB_PITCH), :]
                acc = term if acc is None else acc + term
            stage_ref[pl.ds(t, 8, stride=RB_SPITCH), :] = acc
        gt2 = mod_ref[0, 5:6, :]
        f = jnp.concatenate([stage_ref[RB_SPITCH * j:RB_SPITCH * j + RB_SUB, :] for j in range(8)], axis=-1)
        o_ref[...] = _layer_norm(DEEPNORM_ALPHA * x1_ref[...] + (1.0 + gt2) * f, ln2g_ref[...], ln2b_ref[...])


def _rb_retile(a):
    tiles, two, t = a.shape
    return a.reshape(tiles, two, t // RB_SUB, RB_SUB).transpose(0, 2, 1, 3).reshape(-1, two, RB_SUB)


def _rb_offsets(cnt, e12, rank12):
    start = jnp.cumsum(cnt, axis=-1) - cnt
    start_t = jnp.repeat(start, e12.shape[0] // cnt.shape[0], axis=0)[:, None, None, :]
    hit = e12[..., None] == jnp.arange(N_EXPERTS, dtype=jnp.int32)
    p = jnp.sum(jnp.where(hit, start_t, 0), axis=-1) + rank12
    return lax.shift_right_logical(p, RB_CHUNK_BITS) * (8 * RB_PITCH) + (p & (RB_CHUNK - 1))


def _rb_moe(x1, modp, cnt, offs, wts, wgu_bf16, wd_bf16, ln2g, ln2b):
    n, d = x1.shape
    bsz = n // SEQ
    nsteps = 2 * RB_NSUB + 1

    def sub_index(s):
        return jnp.where(s < RB_NSUB, s, jnp.where(s == RB_NSUB, RB_NSUB - 1, s - RB_NSUB - 1))

    def tile_map(b, s, cnt_r):
        return (b * RB_NSUB + sub_index(s), 0)

    def tile_map3(b, s, cnt_r):
        return (b * RB_NSUB + sub_index(s), 0, 0)

    def out_map(b, s, cnt_r):
        return (b * RB_NSUB + jnp.maximum(s - RB_NSUB - 1, 0), 0)

    const = lambda shp: pl.BlockSpec(shp, lambda b, s, cnt_r: tuple(0 for _ in shp))
    anyspec = pl.BlockSpec(memory_space=pl.ANY)
    grid_spec = pltpu.PrefetchScalarGridSpec(
        num_scalar_prefetch=1,
        grid=(bsz, nsteps),
        in_specs=[pl.BlockSpec((RB_SUB, d), tile_map),
                  pl.BlockSpec((1, 6, d), lambda b, s, cnt_r: (b, 0, 0)),
                  pl.BlockSpec((1, 2, RB_SUB), tile_map3, memory_space=pltpu.SMEM),
                  pl.BlockSpec((1, 2, RB_SUB), tile_map3, memory_space=pltpu.SMEM),
                  anyspec, anyspec,
                  const((1, d)), const((1, d))],
        out_specs=pl.BlockSpec((RB_SUB, d), out_map),
        scratch_shapes=[pltpu.VMEM(((RB_NCHUNK + RB_GROUP) * 8 * RB_PITCH, LANES), F32),
                        pltpu.VMEM((8 * RB_SPITCH, LANES), F32),
                        pltpu.VMEM((RB_WSLOTS, d, 2 * D_EXPERT), BF16),
                        pltpu.VMEM((RB_WSLOTS, D_EXPERT, d), BF16),
                        pltpu.SMEM((N_EXPERTS,), jnp.int32),
                        pltpu.SemaphoreType.DMA((RB_WSLOTS, 2))])
    return pl.pallas_call(
        _rb_kernel,
        grid_spec=grid_spec,
        out_shape=jax.ShapeDtypeStruct((n, d), F32),
        compiler_params=_cparams(("arbitrary", "arbitrary")),
        name="moe_routed_ln",
    )(cnt, x1, modp, offs, wts, wgu_bf16, wd_bf16, ln2g, ln2b)


def _sample_in_kernel(x_ref, mod_ref, win_ref, ctx_ref, h0_ref, convw_ref, convb_ref,
                      wlo_ref, whi_ref, bgate_ref, lam_ref,
                      ylru_ref, q_ref, k_ref, v_ref, cstate_ref, hnew_ref):
    sh1 = mod_ref[0:DEC_BATCH, 0:D_MODEL]
    sc1 = mod_ref[0:DEC_BATCH, D_MODEL:2 * D_MODEL]
    h = x_ref[...] * (1.0 + sc1) + sh1
    z = _dot(h, win_ref[...], True)
    xb = z[:, :LRU_WIDTH]
    gate = z[:, LRU_WIDTH:2 * LRU_WIDTH]
    c0 = ctx_ref[:, 0, :]
    c1 = ctx_ref[:, 1, :]
    c2 = ctx_ref[:, 2, :]
    xc = (convb_ref[...] + convw_ref[0:1, :] * c0 + convw_ref[1:2, :] * c1
          + convw_ref[2:3, :] * c2 + convw_ref[3:4, :] * xb)
    cstate_ref[:, 0, :] = c1
    cstate_ref[:, 1, :] = c2
    cstate_ref[:, 2, :] = xb
    sp = _softplus(-lam_ref[...])
    a, bterm = _lru_gates(xc, wlo_ref[...], whi_ref[...], bgate_ref[...], sp, True)
    hn = a * h0_ref[...] + bterm
    hnew_ref[...] = hn
    ylru_ref[...] = hn * _gelu_tanh(gate)
    low = lax.broadcasted_iota(jnp.int32, (DEC_BATCH, LANES), 1) < HEAD_DIM
    for c in range(4):
        qc = z[:, 2 * LRU_WIDTH + LANES * c:2 * LRU_WIDTH + LANES * (c + 1)]
        q_ref[pl.ds(c, DEC_BATCH, stride=N_HEADS), :] = jnp.where(low, qc, 0.0)
        q_ref[pl.ds(c + 4, DEC_BATCH, stride=N_HEADS), :] = jnp.where(low, 0.0, qc)
    k_ref[...] = z[:, 2 * LRU_WIDTH + ATTN_WIDTH:2 * LRU_WIDTH + ATTN_WIDTH + KV_WIDTH]
    v_ref[...] = z[:, 2 * LRU_WIDTH + ATTN_WIDTH + KV_WIDTH:]


def _sample_in(x, mod, w_in_p, ctx, h0, conv_w, conv_b, wlo, whi, bgate, lam):
    n = DEC_BATCH
    outs = [jax.ShapeDtypeStruct((n, LRU_WIDTH), F32),
            jax.ShapeDtypeStruct((n * N_HEADS, LANES), F32),
            jax.ShapeDtypeStruct((n, KV_WIDTH), F32),
            jax.ShapeDtypeStruct((n, KV_WIDTH), F32),
            jax.ShapeDtypeStruct((n, CONV_WIDTH - 1, LRU_WIDTH), F32),
            jax.ShapeDtypeStruct((n, LRU_WIDTH), F32)]
    return pl.pallas_call(
        _sample_in_kernel,
        out_shape=outs,
        compiler_params=pltpu.CompilerParams(vmem_limit_bytes=VMEM_LIMIT),
        name="sample_inproj_rglru",
    )(x, mod, w_in_p, ctx, h0, conv_w, conv_b, wlo, whi, bgate, lam)


def _sample_attn_kernel(q_ref, kn_ref, vn_ref, ck_ref, cv_ref, sink_ref, y_ref, nk_ref, nv_ref, *, bb):
    rows = lax.broadcasted_iota(jnp.int32, (WINDOW, KV_WIDTH), 0)
    nh = N_HEADS
    q_all = q_ref[...].reshape(bb * nh, LANES)
    kcat = ck_ref[...].reshape(bb * WINDOW, KV_WIDTH)
    vcat = cv_ref[...].reshape(bb * WINDOW, KV_WIDTH)
    kn_rep = jnp.broadcast_to(kn_ref[...][:, None, :], (bb, nh, KV_WIDTH)).reshape(bb * nh, KV_WIDTH)
    vn_rep = jnp.broadcast_to(vn_ref[...][:, None, :], (bb, nh, KV_WIDTH)).reshape(bb * nh, KV_WIDTH)
    sink = jnp.concatenate([sink_ref[...]] * bb, axis=0)
    s_full = _dot_nt(q_all, kcat, True)
    s = jnp.concatenate([s_full[nh * b:nh * (b + 1), WINDOW * b:WINDOW * (b + 1)] for b in range(bb)],
                        axis=0) * ATTN_SCALE
    s_self = jnp.sum(q_all * kn_rep, axis=-1, keepdims=True) * ATTN_SCALE
    m = jnp.maximum(jnp.maximum(jnp.max(s, axis=-1, keepdims=True), s_self), sink)
    e = jnp.exp(s - m)
    e_self = jnp.exp(s_self - m)
    den = jnp.sum(e, axis=-1, keepdims=True) + e_self + jnp.exp(sink - m)
    inv = 1.0 / den
    p = e * inv
    zero = jnp.zeros((nh, WINDOW), F32)
    p_wide = jnp.concatenate(
        [jnp.concatenate([p[nh * b:nh * (b + 1)] if c == b else zero for c in range(bb)], axis=-1)
         for b in range(bb)], axis=0)
    o = _dot(p_wide, vcat, True) + (e_self * inv) * vn_rep
    y_ref[...] = o.reshape(bb, nh, LANES)
    for b in range(bb):
        nk_ref[b] = jnp.where(rows == WINDOW - 1, kn_ref[b:b + 1, :], pltpu.roll(ck_ref[b], WINDOW - 1, axis=0))
        nv_ref[b] = jnp.where(rows == WINDOW - 1, vn_ref[b:b + 1, :], pltpu.roll(cv_ref[b], WINDOW - 1, axis=0))


def _sample_attn(q3, kn, vn, cache_k, cache_v, sinks, bb=16):
    n = DEC_BATCH
    kern = functools.partial(_sample_attn_kernel, bb=bb)
    return pl.pallas_call(
        kern,
        grid=(n // bb,),
        in_specs=[pl.BlockSpec((bb, N_HEADS, LANES), lambda i: (i, 0, 0)),
                  pl.BlockSpec((bb, KV_WIDTH), lambda i: (i, 0)),
                  pl.BlockSpec((bb, KV_WIDTH), lambda i: (i, 0)),
                  pl.BlockSpec((bb, WINDOW, KV_WIDTH), lambda i: (i, 0, 0)),
                  pl.BlockSpec((bb, WINDOW, KV_WIDTH), lambda i: (i, 0, 0)),
                  pl.BlockSpec((N_HEADS, 1), lambda i: (0, 0))],
        out_specs=[pl.BlockSpec((bb, N_HEADS, LANES), lambda i: (i, 0, 0)),
                   pl.BlockSpec((bb, WINDOW, KV_WIDTH), lambda i: (i, 0, 0)),
                   pl.BlockSpec((bb, WINDOW, KV_WIDTH), lambda i: (i, 0, 0))],
        out_shape=[jax.ShapeDtypeStruct((n, N_HEADS, LANES), F32),
                   jax.ShapeDtypeStruct((n, WINDOW, KV_WIDTH), F32),
                   jax.ShapeDtypeStruct((n, WINDOW, KV_WIDTH), F32)],
        compiler_params=_cparams(("arbitrary",)),
        name="sample_cache_attention",
    )(q3, kn, vn, cache_k, cache_v, sinks.reshape(N_HEADS, 1))


def _sample_out_kernel(x_ref, ylru_ref, yatt_ref, mod_ref, glru_ref, gattn_ref, wout_ref,
                       ln1g_ref, ln1b_ref, wr_ref, br_ref, x1_ref, comb_ref):
    low = lax.broadcasted_iota(jnp.int32, (DEC_BATCH, LANES), 1) < HEAD_DIM
    yatt = jnp.concatenate(
        [jnp.where(low, yatt_ref[pl.ds(c, DEC_BATCH, stride=N_HEADS), :],
                   yatt_ref[pl.ds(c + 4, DEC_BATCH, stride=N_HEADS), :]) for c in range(4)], axis=-1)
    gt1 = mod_ref[0:DEC_BATCH, 2 * D_MODEL:3 * D_MODEL]
    sh2 = mod_ref[0:DEC_BATCH, 3 * D_MODEL:4 * D_MODEL]
    sc2 = mod_ref[0:DEC_BATCH, 4 * D_MODEL:5 * D_MODEL]
    x1, comb = _outproj_body(x_ref[...], ylru_ref[...], yatt, sh2, sc2, gt1,
                             glru_ref[...], gattn_ref[...], wout_ref[...], ln1g_ref[...], ln1b_ref[...],
                             wr_ref[...], br_ref[...], True)
    x1_ref[...] = x1
    comb_ref[...] = comb


def _sample_out(x, ylru, yatt2d, mod, glru, gattn, wout_p, ln1g, ln1b, wr, br):
    n = DEC_BATCH
    return pl.pallas_call(
        _sample_out_kernel,
        out_shape=[jax.ShapeDtypeStruct((n, D_MODEL), F32), jax.ShapeDtypeStruct((n, ROUTE_LANES), F32)],
        compiler_params=pltpu.CompilerParams(vmem_limit_bytes=VMEM_LIMIT),
        name="sample_outproj_ln_route",
    )(x, ylru, yatt2d, mod, glru, gattn, wout_p, ln1g, ln1b, wr, br)


def _block_diag_halves(w_a, w_x):
    def bd(w4):
        eye = jnp.eye(4, dtype=w4.dtype)
        return (w4[:, :, None, :] * eye[:, None, :, None]).reshape(256, 256)
    lo = jnp.concatenate([bd(w_a[:4]), bd(w_x[:4])], axis=1)
    hi = jnp.concatenate([bd(w_a[4:]), bd(w_x[4:])], axis=1)
    return lo, hi


def kernel(x_prompt, x_sample, c_prompt, c_sample, state_conv, state_h, cache_k, cache_v, w_ada, b_ada, w_in,
           conv_w, conv_b, w_rg_a, b_rg_a, w_rg_x, b_rg_x, lru_lambda, sinks, g_lru, g_attn, w_out, ln1_g, ln1_b,
           w_group, b_group, w_router, b_router, w_gate, w_up, w_down, ln2_g, ln2_b):
    d = D_MODEL
    q0 = 2 * LRU_WIDTH
    w_in0 = w_in[0]
    wq = w_in0[:, q0:q0 + ATTN_WIDTH].reshape(d, 2, 4, HEAD_DIM).transpose(0, 2, 1, 3).reshape(d, ATTN_WIDTH)
    w_in_p = jnp.concatenate([w_in0[:, :q0], wq, w_in0[:, q0 + ATTN_WIDTH:]], axis=1)
    w_out0 = w_out[0]
    wo_att = w_out0[LRU_WIDTH:].reshape(2, 4, HEAD_DIM, d).transpose(1, 0, 2, 3).reshape(ATTN_WIDTH, d)
    w_out_p = jnp.concatenate([w_out0[:LRU_WIDTH], wo_att], axis=0)
    g_attn_p = g_attn[0].reshape(2, 4, HEAD_DIM).transpose(1, 0, 2).reshape(1, ATTN_WIDTH)
    glru = g_lru[0].reshape(1, -1)
    wlo, whi = _block_diag_halves(w_rg_a[0], w_rg_x[0])
    bgate = jnp.concatenate([b_rg_a[0].reshape(-1), b_rg_x[0].reshape(-1)]).reshape(1, -1)
    lam = lru_lambda[0].reshape(1, -1)
    convw = conv_w[0]
    convb = conv_b[0].reshape(1, -1)
    ln1g, ln1b = ln1_g[0].reshape(1, -1), ln1_b[0].reshape(1, -1)
    ln2g, ln2b = ln2_g[0].reshape(1, -1), ln2_b[0].reshape(1, -1)
    wr = jnp.concatenate([jnp.transpose(w_router[0], (1, 0, 2)).reshape(d, N_EXPERTS), w_group[0],
                          jnp.zeros((d, ROUTE_LANES - N_EXPERTS - N_GROUPS), F32)], axis=1)
    br = jnp.concatenate([b_router[0].reshape(-1), b_group[0],
                          jnp.zeros((ROUTE_LANES - N_EXPERTS - N_GROUPS,), F32)]).reshape(1, -1)
    sink_p = sinks[0]

    c_all = jnp.concatenate([c_sample, c_prompt, jnp.zeros((8 - BATCH, d), F32)], axis=0)
    mod = _ada(c_all, w_ada[0], b_ada[0])
    modp = mod[DEC_BATCH:DEC_BATCH + BATCH].reshape(BATCH, 6, d)

    zlru, zqkv, kvlast = _inproj(x_prompt, modp, w_in_p.astype(BF16))
    ylru, cstate8, hlast8, wgu_b, wd_b = _lru(zlru, convw, convb, wlo.astype(BF16), whi.astype(BF16), bgate, lam,
                                                   w_gate[0], w_up[0], w_down[0])
    yatt = _attn(zqkv, sink_p)
    n_p = BATCH * SEQ
    x1_p, info, cntf = _outproj_prompt(x_prompt.reshape(n_p, d), ylru.reshape(n_p, LRU_WIDTH),
                                       yatt.reshape(n_p, ATTN_WIDTH), modp, glru, g_attn_p, w_out_p.astype(BF16),
                                       ln1g, ln1b, wr, br, tm=OUTPROJ_TILE)
    cnt = cntf[:, 0, :N_EXPERTS].astype(jnp.int32)
    offs = _rb_retile(_rb_offsets(cnt, info[:, 0:2].astype(jnp.int32), info[:, 4:6].astype(jnp.int32)))
    y_p = _rb_moe(x1_p, modp, cnt, offs, _rb_retile(info[:, 2:4]), wgu_b, wd_b, ln2g, ln2b)

    ylru_s, q2d, kn, vn, cstate_s, hnew_s = _sample_in(
        x_sample.reshape(DEC_BATCH, d), mod, w_in_p, state_conv[0], state_h[0],
        convw, convb, wlo, whi, bgate, lam)
    yatt3, newk, newv = _sample_attn(q2d.reshape(DEC_BATCH, N_HEADS, LANES), kn, vn,
                                     cache_k[0].reshape(DEC_BATCH, WINDOW, KV_WIDTH),
                                     cache_v[0].reshape(DEC_BATCH, WINDOW, KV_WIDTH), sink_p)
    x1_s, comb_s = _sample_out(x_sample.reshape(DEC_BATCH, d), ylru_s, yatt3.reshape(DEC_BATCH * N_HEADS, LANES),
                               mod, glru, g_attn_p, w_out_p, ln1g, ln1b, wr, br)
    y_s = _moe_dense(x1_s, comb_s, mod, wgu_b, wd_b, ln2g, ln2b, DEC_BATCH)

    return (y_p.reshape(BATCH, SEQ, d),
            y_s.reshape(DEC_BATCH, 1, d),
            cstate8[:, 5:8][None],
            hlast8[:, 7][None],
            kvlast[:, :, :KV_WIDTH].reshape(1, BATCH, WINDOW, N_KV_HEADS, HEAD_DIM),
            kvlast[:, :, KV_WIDTH:].reshape(1, BATCH, WINDOW, N_KV_HEADS, HEAD_DIM),
            cstate_s[None],
            hnew_s[None],
            newk.reshape(1, DEC_BATCH, WINDOW, N_KV_HEADS, HEAD_DIM),
            newv.reshape(1, DEC_BATCH, WINDOW, N_KV_HEADS, HEAD_DIM))
```

```python
import functools

import jax
import jax.numpy as jnp
import numpy as np
from jax import lax
from jax.experimental import pallas as pl
from jax.experimental.pallas import tpu as pltpu

F32 = jnp.float32
BF16 = jnp.bfloat16
HIGHEST = lax.Precision.HIGHEST

D_MODEL = 1024
BATCH = 4
SEQ = 4096
DEC_BATCH = 128
LRU_WIDTH = 512
LRU_BLOCKS = 8
LRU_BLOCK = 64
CONV_WIDTH = 4
LRU_C = 8.0
N_HEADS = 8
N_KV_HEADS = 2
HEAD_DIM = 64
ATTN_WIDTH = 512
KV_WIDTH = 128
WINDOW = 128
IN_WIDTH = 2 * LRU_WIDTH + ATTN_WIDTH + 2 * KV_WIDTH
N_GROUPS = 4
EXPERTS_PER_GROUP = 8
N_EXPERTS = 32
D_EXPERT = 256
DEEPNORM_ALPHA = 2.0 ** 0.25
LN_EPS = 1e-5
RMS_EPS = 1e-6
ATTN_SCALE = HEAD_DIM ** -0.5

LANES = 128
ROUTE_LANES = 128
ROUTE_INFO = 40
VMEM_LIMIT = 56 * 1024 * 1024

HEAD_PERM = np.concatenate(
    [np.concatenate([np.arange(64 * c, 64 * c + 64), np.arange(64 * (c + 4), 64 * (c + 4) + 64)])
     for c in range(4)])


def _cparams(sem):
    return pltpu.CompilerParams(dimension_semantics=sem, vmem_limit_bytes=VMEM_LIMIT)


def _dot(a, b, exact):
    if exact:
        return jnp.dot(a, b, precision=HIGHEST, preferred_element_type=F32)
    return jnp.dot(a.astype(BF16), b.astype(BF16), preferred_element_type=F32)


def _dot_nt(a, b, exact):
    dn = (((1,), (1,)), ((), ()))
    if exact:
        return lax.dot_general(a, b, dn, precision=HIGHEST, preferred_element_type=F32)
    return lax.dot_general(a.astype(BF16), b.astype(BF16), dn, preferred_element_type=F32)


def _sigmoid(x):
    return 1.0 / (1.0 + jnp.exp(-x))


def _silu(x):
    return x * _sigmoid(x)


def _gelu_tanh(x):
    return 0.5 * x * (1.0 + jnp.tanh(np.sqrt(2.0 / np.pi).astype(np.float32) * (x + 0.044715 * (x * x * x))))


def _softplus(x):
    return jnp.maximum(x, 0.0) + jnp.log1p(jnp.exp(-jnp.abs(x)))


def _layer_norm(x, g, b):
    mu = jnp.mean(x, axis=-1, keepdims=True)
    xc = x - mu
    var = jnp.mean(xc * xc, axis=-1, keepdims=True)
    return xc * lax.rsqrt(var + LN_EPS) * g + b


def _rms_norm(x, g):
    return x * lax.rsqrt(jnp.mean(x * x, axis=-1, keepdims=True) + RMS_EPS) * g


def _ada_kernel(c_ref, w_ref, b_ref, o_ref):
    o_ref[...] = _dot(_silu(c_ref[...]), w_ref[...], True) + b_ref[...]


def _ada(c_all, w_ada, b_ada):
    rows = c_all.shape[0]
    bn = 1024
    return pl.pallas_call(
        _ada_kernel,
        grid=(6 * D_MODEL // bn,),
        in_specs=[pl.BlockSpec((rows, D_MODEL), lambda j: (0, 0)),
                  pl.BlockSpec((D_MODEL, bn), lambda j: (0, j)),
                  pl.BlockSpec((1, bn), lambda j: (0, j))],
        out_specs=pl.BlockSpec((rows, bn), lambda j: (0, j)),
        out_shape=jax.ShapeDtypeStruct((rows, 6 * D_MODEL), F32),
        compiler_params=_cparams(("arbitrary",)),
        name="ada_modulation",
    )(c_all, w_ada, b_ada.reshape(1, -1))


QKV_WIDTH = ATTN_WIDTH + 2 * KV_WIDTH


def _inproj_kernel(x_ref, mod_ref, w_ref, lru_ref, qkv_ref, kvlast_ref):
    sh1 = mod_ref[0, 0:1, :]
    sc1 = mod_ref[0, 1:2, :]
    h = x_ref[0] * (1.0 + sc1) + sh1
    z = _dot(h, w_ref[...], False)
    lru_ref[0] = z[:, :2 * LRU_WIDTH]
    qkv_ref[0] = z[:, 2 * LRU_WIDTH:].astype(BF16)
    kvlast_ref[0] = z[z.shape[0] - WINDOW:, 2 * LRU_WIDTH + ATTN_WIDTH:]


def _inproj(x, modp, w_in_bf16, tm=1024):
    b, t, d = x.shape
    return pl.pallas_call(
        _inproj_kernel,
        grid=(b, t // tm),
        in_specs=[pl.BlockSpec((1, tm, d), lambda i, j: (i, j, 0)),
                  pl.BlockSpec((1, 6, d), lambda i, j: (i, 0, 0)),
                  pl.BlockSpec((d, IN_WIDTH), lambda i, j: (0, 0))],
        out_specs=[pl.BlockSpec((1, tm, 2 * LRU_WIDTH), lambda i, j: (i, j, 0)),
                   pl.BlockSpec((1, tm, QKV_WIDTH), lambda i, j: (i, j, 0)),
                   pl.BlockSpec((1, WINDOW, 2 * KV_WIDTH), lambda i, j: (i, 0, 0))],
        out_shape=[jax.ShapeDtypeStruct((b, t, 2 * LRU_WIDTH), F32),
                   jax.ShapeDtypeStruct((b, t, QKV_WIDTH), BF16),
                   jax.ShapeDtypeStruct((b, WINDOW, 2 * KV_WIDTH), F32)],
        compiler_params=_cparams(("arbitrary", "arbitrary")),
        name="prompt_inproj",
    )(x, modp, w_in_bf16)


def _lru_gates(xc, wlo, whi, bgate, sp_neg_lam, exact):
    g_lo = _dot(xc[:, :256], wlo, exact)
    g_hi = _dot(xc[:, 256:], whi, exact)
    ga = jnp.concatenate([g_lo[:, :256], g_hi[:, :256]], axis=-1) + bgate[:, :LRU_WIDTH]
    gx = jnp.concatenate([g_lo[:, 256:], g_hi[:, 256:]], axis=-1) + bgate[:, LRU_WIDTH:]
    r = _sigmoid(ga)
    i = _sigmoid(gx)
    log_a = -LRU_C * r * sp_neg_lam
    a = jnp.exp(log_a)
    one_minus_a2 = -jnp.tanh(log_a) * (a * a + 1.0) if exact else 1.0 - a * a
    root = jnp.where(one_minus_a2 > 0.0, one_minus_a2 * lax.rsqrt(one_minus_a2), 0.0)
    bterm = root * (i * xc)
    return a, bterm


def _lru_kernel(z_ref, convw_ref, convb_ref, wlo_ref, whi_ref, bgate_ref, lam_ref, wg_ref, wu_ref, wd_ref,
                y_ref, cstate_ref, hlast_ref, wgub_ref, wdb_ref, tail_ref, carry_ref, *, tl):
    j = pl.program_id(1)

    @pl.when(j == 0)
    def _():
        tail_ref[...] = jnp.zeros_like(tail_ref)
        carry_ref[...] = jnp.zeros_like(carry_ref)

    wgub_ref[0, :, :D_EXPERT] = wg_ref[0].astype(BF16)
    wgub_ref[0, :, D_EXPERT:] = wu_ref[0].astype(BF16)
    wdb_ref[...] = wd_ref[...].astype(BF16)

    xb = z_ref[0, :, :LRU_WIDTH]
    gate = z_ref[0, :, LRU_WIDTH:]
    xc = convb_ref[...] + convw_ref[3:4, :] * xb
    rows8 = lax.broadcasted_iota(jnp.int32, (8, LRU_WIDTH), 0)
    tail = tail_ref[...]
    for back in (1, 2, 3):
        rolled = pltpu.roll(xb, back, axis=0)
        top = jnp.where(rows8 >= back, rolled[:8], pltpu.roll(tail, back, axis=0))
        shifted = jnp.concatenate([top, rolled[8:]], axis=0)
        xc = xc + convw_ref[3 - back:4 - back, :] * shifted
    tail_ref[...] = xb[tl - 8:, :]
    cstate_ref[0] = xb[tl - 8:, :]

    sp = _softplus(-lam_ref[...])
    a, bterm = _lru_gates(xc, wlo_ref[...], whi_ref[...], bgate_ref[...], sp, False)

    groups = tl // 8
    a = a.reshape(groups, 8, LRU_WIDTH)
    bterm = bterm.reshape(groups, 8, LRU_WIDTH)
    r8 = lax.broadcasted_iota(jnp.int32, (groups, 8, LRU_WIDTH), 1)
    s = 1
    while s < 8:
        a_sh = jnp.where(r8 >= s, pltpu.roll(a, s, axis=1), 1.0)
        b_sh = jnp.where(r8 >= s, pltpu.roll(bterm, s, axis=1), 0.0)
        bterm = a * b_sh + bterm
        a = a * a_sh
        s *= 2
    a_tot = jnp.broadcast_to(a[:, 7:8, :], (groups, 8, LRU_WIDTH))
    b_tot = jnp.broadcast_to(bterm[:, 7:8, :], (groups, 8, LRU_WIDTH))
    h_in = jnp.broadcast_to(carry_ref[7:8, :], (8, LRU_WIDTH))
    pieces = []
    for g in range(groups):
        pieces.append(a[g] * h_in + bterm[g])
        h_in = a_tot[g] * h_in + b_tot[g]
    h = jnp.concatenate(pieces, axis=0)
    carry_ref[...] = h_in
    hlast_ref[0] = h_in
    y_ref[0] = h * _gelu_tanh(gate)


def _lru(zin, conv_w, conv_b, wlo, whi, bgate, lam, w_gate, w_up, w_down, tl=512):
    b, t, _ = zin.shape
    steps = t // tl
    assert b * steps == N_EXPERTS
    d = D_MODEL
    kern = functools.partial(_lru_kernel, tl=tl)
    full = lambda shp: pl.BlockSpec(shp, lambda i, j: tuple(0 for _ in shp))
    per_step = lambda shp: pl.BlockSpec(shp, lambda i, j: (i * steps + j, 0, 0))
    return pl.pallas_call(
        kern,
        grid=(b, steps),
        in_specs=[pl.BlockSpec((1, tl, 2 * LRU_WIDTH), lambda i, j: (i, j, 0)),
                  full((CONV_WIDTH, LRU_WIDTH)), full((1, LRU_WIDTH)),
                  full((256, 512)), full((256, 512)), full((1, 2 * LRU_WIDTH)), full((1, LRU_WIDTH)),
                  per_step((1, d, D_EXPERT)), per_step((1, d, D_EXPERT)), per_step((1, D_EXPERT, d))],
        out_specs=[pl.BlockSpec((1, tl, LRU_WIDTH), lambda i, j: (i, j, 0)),
                   pl.BlockSpec((1, 8, LRU_WIDTH), lambda i, j: (i, 0, 0)),
                   pl.BlockSpec((1, 8, LRU_WIDTH), lambda i, j: (i, 0, 0)),
                   per_step((1, d, 2 * D_EXPERT)), per_step((1, D_EXPERT, d))],
        out_shape=[jax.ShapeDtypeStruct((b, t, LRU_WIDTH), F32),
                   jax.ShapeDtypeStruct((b, 8, LRU_WIDTH), F32),
                   jax.ShapeDtypeStruct((b, 8, LRU_WIDTH), F32),
                   jax.ShapeDtypeStruct((N_EXPERTS, d, 2 * D_EXPERT), BF16),
                   jax.ShapeDtypeStruct(w_down.shape, BF16)],
        scratch_shapes=[pltpu.VMEM((8, LRU_WIDTH), F32), pltpu.VMEM((8, LRU_WIDTH), F32)],
        compiler_params=_cparams(("arbitrary", "arbitrary")),
        name="prompt_rglru",
    )(zin, conv_w, conv_b, wlo, whi, bgate, lam, w_gate, w_up, w_down)


ATTN_BLOCKS = 16


def _attn_kernel(q_ref, k_ref, v_ref, sink_ref, o_ref, kprev_ref, vprev_ref):
    j = pl.program_id(1)

    @pl.when(j == 0)
    def _():
        kprev_ref[...] = jnp.zeros_like(kprev_ref)
        vprev_ref[...] = jnp.zeros_like(vprev_ref)

    blk = WINDOW
    lane = lax.broadcasted_iota(jnp.int32, (blk, LANES), 1)
    low = lane < HEAD_DIM
    qi = lax.broadcasted_iota(jnp.int32, (blk, 2 * blk), 0)
    sj = lax.broadcasted_iota(jnp.int32, (blk, 2 * blk), 1)
    rel = blk + qi - sj
    in_window = (rel >= 0) & (rel <= WINDOW)
    sink = sink_ref[...].reshape(N_HEADS, blk, 1)
    k_ext = jnp.concatenate([kprev_ref[...], k_ref[0]], axis=0)
    v_ext = jnp.concatenate([vprev_ref[...], v_ref[0]], axis=0)
    v_ext = jnp.concatenate([v_ext, jnp.ones_like(v_ext)], axis=-1)
    for n in range(ATTN_BLOCKS):
        q = q_ref[0, blk * n:blk * (n + 1), :]
        pieces = []
        for half in (0, 1):
            for c in range(4):
                qc = q[:, LANES * c:LANES * (c + 1)]
                pieces.append(jnp.where(low if half == 0 else ~low, qc, 0.0).astype(BF16))
        q8 = jnp.concatenate(pieces, axis=0)
        k_band = k_ext[blk * n:blk * (n + 2)]
        v_band = v_ext[blk * n:blk * (n + 2)]
        s = _dot_nt(q8, k_band, False) * ATTN_SCALE
        s = s.reshape(N_HEADS, blk, 2 * blk)
        valid = in_window & ((sj >= blk) | (j > 0)) if n == 0 else in_window
        s = jnp.where(valid[None], s, -jnp.inf)
        m = jnp.maximum(jnp.max(s, axis=-1, keepdims=True), sink)
        e = jnp.exp(s - m).reshape(N_HEADS * blk, 2 * blk)
        ov = _dot(e, v_band, False)
        den = ov[:, KV_WIDTH:] + jnp.exp(sink - m).reshape(N_HEADS * blk, 1)
        o8 = ov[:, :KV_WIDTH] * (1.0 / den)
        cols = []
        for c in range(4):
            cols.append(jnp.where(low, o8[blk * c:blk * (c + 1)], o8[blk * (c + 4):blk * (c + 5)]))
        o_ref[0, blk * n:blk * (n + 1), :] = jnp.concatenate(cols, axis=-1)
    kprev_ref[...] = k_ref[0, blk * (ATTN_BLOCKS - 1):, :]
    vprev_ref[...] = v_ref[0, blk * (ATTN_BLOCKS - 1):, :]


def _attn(qkv, sinks):
    b, t, _ = qkv.shape
    blk = WINDOW
    tq = blk * ATTN_BLOCKS
    sink_col = jnp.repeat(sinks.astype(F32), blk).reshape(N_HEADS * blk, 1)
    kcol = ATTN_WIDTH // KV_WIDTH
    return pl.pallas_call(
        _attn_kernel,
        grid=(b, t // tq),
        in_specs=[pl.BlockSpec((1, tq, ATTN_WIDTH), lambda i, j: (i, j, 0)),
                  pl.BlockSpec((1, tq, KV_WIDTH), lambda i, j: (i, j, kcol)),
                  pl.BlockSpec((1, tq, KV_WIDTH), lambda i, j: (i, j, kcol + 1)),
                  pl.BlockSpec((N_HEADS * blk, 1), lambda i, j: (0, 0))],
        out_specs=pl.BlockSpec((1, tq, ATTN_WIDTH), lambda i, j: (i, j, 0)),
        out_shape=jax.ShapeDtypeStruct((b, t, ATTN_WIDTH), F32),
        scratch_shapes=[pltpu.VMEM((blk, KV_WIDTH), BF16), pltpu.VMEM((blk, KV_WIDTH), BF16)],
        compiler_params=_cparams(("arbitrary", "arbitrary")),
        name="prompt_window_attention",
    )(qkv, qkv, qkv, sink_col)


def _route(h2, wr, br, exact):
    t = h2.shape[0]
    logits = _dot(h2, wr, exact) + br
    lane = lax.broadcasted_iota(jnp.int32, (t, ROUTE_LANES), 1).astype(F32)
    neg = -jnp.inf
    big = float(ROUTE_LANES)
    is_g = (lane >= N_EXPERTS) & (lane < N_EXPERTS + N_GROUPS)
    lg = jnp.where(is_g, logits, neg)
    mg = jnp.max(lg, axis=-1, keepdims=True)
    eg = jnp.where(is_g, jnp.exp(lg - mg), 0.0)
    pg = eg / jnp.sum(eg, axis=-1, keepdims=True)
    g_val = jnp.max(pg, axis=-1, keepdims=True)
    g_lane = jnp.min(jnp.where((pg == g_val) & is_g, lane, big), axis=-1, keepdims=True)
    g_idx = g_lane - N_EXPERTS
    in_grp = (lane >= g_idx * EXPERTS_PER_GROUP) & (lane < (g_idx + 1.0) * EXPERTS_PER_GROUP)
    le = jnp.where(in_grp, logits, neg)
    me = jnp.max(le, axis=-1, keepdims=True)
    ee = jnp.where(in_grp, jnp.exp(le - me), 0.0)
    pe = ee / jnp.sum(ee, axis=-1, keepdims=True)
    v1 = jnp.max(pe, axis=-1, keepdims=True)
    l1 = jnp.min(jnp.where((pe == v1) & in_grp, lane, big), axis=-1, keepdims=True)
    rest = in_grp & (lane != l1)
    pe2 = jnp.where(rest, pe, -1.0)
    v2 = jnp.max(pe2, axis=-1, keepdims=True)
    l2 = jnp.min(jnp.where((pe2 == v2) & rest, lane, big), axis=-1, keepdims=True)
    tot = v1 + v2
    w1 = g_val * v1 / tot
    w2 = g_val * v2 / tot
    comb = jnp.where(lane == l1, w1, 0.0) + jnp.where(lane == l2, w2, 0.0)
    return (comb + jnp.where(lane == ROUTE_INFO, l1, 0.0) + jnp.where(lane == ROUTE_INFO + 1, l2, 0.0)
            + jnp.where(lane == ROUTE_INFO + 2, w1, 0.0) + jnp.where(lane == ROUTE_INFO + 3, w2, 0.0))


def _outproj_body(x, ylru, yatt, sh2, sc2, gt1, glru, gattn, wout, ln1g, ln1b, wr, br, exact):
    mixin = jnp.concatenate([_rms_norm(ylru, glru), _rms_norm(yatt, gattn)], axis=-1)
    mix = _dot(mixin, wout, exact)
    x1 = _layer_norm(DEEPNORM_ALPHA * x + (1.0 + gt1) * mix, ln1g, ln1b)
    h2 = x1 * (1.0 + sc2) + sh2
    return x1, _route(h2, wr, br, exact)


def _outproj_prompt_kernel(x_ref, ylru_ref, yatt_ref, mod_ref, glru_ref, gattn_ref, wout_ref,
                           ln1g_ref, ln1b_ref, wr_ref, br_ref, x1_ref, info_ref, cnt_ref, tri_ref, carry_ref,
                           *, tm, per_seq):
    i = pl.program_id(0)

    @pl.when(i == 0)
    def _():
        r = lax.broadcasted_iota(jnp.int32, (tm, tm), 0)
        c = lax.broadcasted_iota(jnp.int32, (tm, tm), 1)
        tri_ref[...] = jnp.where(c < r, 1.0, 0.0).astype(BF16)

    @pl.when(i % per_seq == 0)
    def _():
        carry_ref[...] = jnp.zeros_like(carry_ref)

    gt1 = mod_ref[0, 2:3, :]
    sh2 = mod_ref[0, 3:4, :]
    sc2 = mod_ref[0, 4:5, :]
    combs = []
    nsplit = 2
    for h in range(nsplit):
        rows = slice(h * (tm // nsplit), (h + 1) * (tm // nsplit))
        x1_h, comb_h = _outproj_body(x_ref[rows, :], ylru_ref[rows, :], yatt_ref[rows, :], sh2, sc2, gt1,
                                     glru_ref[...], gattn_ref[...], wout_ref[...], ln1g_ref[...], ln1b_ref[...],
                                     wr_ref[...], br_ref[...], False)
        x1_ref[rows, :] = x1_h
        combs.append(comb_h)
    comb = jnp.concatenate(combs, axis=0)
    lane = lax.broadcasted_iota(jnp.int32, (tm, ROUTE_LANES), 1).astype(F32)
    l1 = jnp.sum(jnp.where(lane == ROUTE_INFO, comb, 0.0), axis=-1, keepdims=True)
    l2 = jnp.sum(jnp.where(lane == ROUTE_INFO + 1, comb, 0.0), axis=-1, keepdims=True)
    o1 = lane == l1
    o2 = lane == l2
    onehot = jnp.where(o1 | o2, 1.0, 0.0)
    before = jnp.dot(tri_ref[...], onehot.astype(BF16), preferred_element_type=F32) + carry_ref[0:1, :]
    rank1 = jnp.sum(jnp.where(o1, before, 0.0), axis=-1, keepdims=True)
    rank2 = jnp.sum(jnp.where(o2, before, 0.0), axis=-1, keepdims=True)
    total = carry_ref[0:1, :] + jnp.sum(onehot, axis=0, keepdims=True)
    carry_ref[...] = jnp.broadcast_to(total, carry_ref.shape)
    cnt_ref[0] = jnp.broadcast_to(total, (8, ROUTE_LANES))
    info = (comb + jnp.where(lane == ROUTE_INFO + 4, rank1, 0.0) + jnp.where(lane == ROUTE_INFO + 5, rank2, 0.0))
    info_ref[0] = jnp.transpose(info)[ROUTE_INFO:ROUTE_INFO + 8, :]


OUTPROJ_TILE = 1024


def _outproj_prompt(x2d, ylru2d, yatt2d, modp, glru, gattn, wout_bf16, ln1g, ln1b, wr, br, tm=OUTPROJ_TILE):
    n, d = x2d.shape
    per_seq = SEQ // tm
    full = lambda shp: pl.BlockSpec(shp, lambda i: tuple(0 for _ in shp))
    kern = functools.partial(_outproj_prompt_kernel, tm=tm, per_seq=per_seq)
    return pl.pallas_call(
        kern,
        grid=(n // tm,),
        in_specs=[pl.BlockSpec((tm, d), lambda i: (i, 0)),
                  pl.BlockSpec((tm, LRU_WIDTH), lambda i: (i, 0)),
                  pl.BlockSpec((tm, ATTN_WIDTH), lambda i: (i, 0)),
                  pl.BlockSpec((1, 6, d), lambda i: (i // per_seq, 0, 0)),
                  full((1, LRU_WIDTH)), full((1, ATTN_WIDTH)), full((d, d)),
                  full((1, d)), full((1, d)), full((d, ROUTE_LANES)), full((1, ROUTE_LANES))],
        out_specs=[pl.BlockSpec((tm, d), lambda i: (i, 0)),
                   pl.BlockSpec((1, 8, tm), lambda i: (i, 0, 0)),
                   pl.BlockSpec((1, 8, ROUTE_LANES), lambda i: (i // per_seq, 0, 0))],
        out_shape=[jax.ShapeDtypeStruct((n, d), F32),
                   jax.ShapeDtypeStruct((n // tm, 8, tm), F32),
                   jax.ShapeDtypeStruct((n // SEQ, 8, ROUTE_LANES), F32)],
        scratch_shapes=[pltpu.VMEM((tm, tm), BF16), pltpu.VMEM((8, ROUTE_LANES), F32)],
        compiler_params=_cparams(("arbitrary",)),
        name="prompt_outproj_ln_route",
    )(x2d, ylru2d, yatt2d, modp, glru, gattn, wout_bf16, ln1g, ln1b, wr, br)


DENSE_EXPERTS_PER_STEP = 4


def _moe_kernel(x1_ref, comb_ref, sh2_ref, sc2_ref, gt2_ref, wgu_ref, wd_ref, ln2g_ref, ln2b_ref,
                o_ref, h2_ref, acc_ref):
    g = pl.program_id(1)

    @pl.when(g == 0)
    def _():
        h2_ref[...] = (x1_ref[...] * (1.0 + sc2_ref[...]) + sh2_ref[...]).astype(BF16)
        acc_ref[...] = jnp.zeros_like(acc_ref)

    h2 = h2_ref[...]
    comb = comb_ref[...]
    lane = lax.broadcasted_iota(jnp.int32, comb.shape, 1)
    part = None
    for k in range(DENSE_EXPERTS_PER_STEP):
        au = jnp.dot(h2, wgu_ref[k], preferred_element_type=F32)
        c_e = jnp.sum(jnp.where(lane == g * DENSE_EXPERTS_PER_STEP + k, comb, 0.0), axis=-1, keepdims=True)
        z = _silu(au[:, :D_EXPERT]) * au[:, D_EXPERT:] * c_e
        y = jnp.dot(z.astype(BF16), wd_ref[k], preferred_element_type=F32)
        part = y if part is None else part + y
    acc_ref[...] += part

    @pl.when(g == N_EXPERTS // DENSE_EXPERTS_PER_STEP - 1)
    def _():
        o_ref[...] = _layer_norm(DEEPNORM_ALPHA * x1_ref[...] + (1.0 + gt2_ref[...]) * acc_ref[...],
                                 ln2g_ref[...], ln2b_ref[...])


def _moe_dense(x1, comb, mod, wgu_bf16, wd_bf16, ln2g, ln2b, tm):
    n, d = x1.shape
    eg = DENSE_EXPERTS_PER_STEP
    mspec = lambda k: pl.BlockSpec((tm, d), lambda i, e: (i, k))
    full = lambda shp: pl.BlockSpec(shp, lambda i, e: tuple(0 for _ in shp))
    return pl.pallas_call(
        _moe_kernel,
        grid=(n // tm, N_EXPERTS // eg),
        in_specs=[pl.BlockSpec((tm, d), lambda i, e: (i, 0)),
                  pl.BlockSpec((tm, ROUTE_LANES), lambda i, e: (i, 0)),
                  mspec(3), mspec(4), mspec(5),
                  pl.BlockSpec((eg, d, 2 * D_EXPERT), lambda i, e: (e, 0, 0)),
                  pl.BlockSpec((eg, D_EXPERT, d), lambda i, e: (e, 0, 0)),
                  full((1, d)), full((1, d))],
        out_specs=pl.BlockSpec((tm, d), lambda i, e: (i, 0)),
        out_shape=jax.ShapeDtypeStruct((n, d), F32),
        scratch_shapes=[pltpu.VMEM((tm, d), BF16), pltpu.VMEM((tm, d), F32)],
        compiler_params=_cparams(("arbitrary", "arbitrary")),
        name="moe_dense_ln",
    )(x1, comb, mod, mod, mod, wgu_bf16, wd_bf16, ln2g, ln2b)


RB_SUB = 512
RB_NSUB = SEQ // RB_SUB
RB_CHUNK = 128
RB_CHUNK_BITS = 7
RB_NCHUNK = 2 * SEQ // RB_CHUNK
RB_PITCH = RB_CHUNK + 8
RB_SPITCH = RB_SUB + 8
RB_GROUP = 3
RB_WSLOTS = 4


def _rb_kernel(cnt_ref, x1_ref, mod_ref, offs_ref, wts_ref, wgu_hbm, wd_hbm, ln2g_ref, ln2b_ref,
               o_ref, buf_ref, stage_ref, wgu_buf, wd_buf, start_ref, sem):
    b = pl.program_id(0)
    s = pl.program_id(1)

    @pl.when(s == 0)
    def _starts():
        def body(e, run):
            start_ref[e] = run
            return run + cnt_ref[b, e]
        lax.fori_loop(0, N_EXPERTS, body, jnp.int32(0))
        buf_ref[RB_NCHUNK * 8 * RB_PITCH:(RB_NCHUNK + RB_GROUP) * 8 * RB_PITCH, :] = jnp.zeros(
            (RB_GROUP * 8 * RB_PITCH, LANES), F32)

    @pl.when(s < RB_NSUB)
    def _dispatch():
        sh2 = mod_ref[0, 3:4, :]
        sc2 = mod_ref[0, 4:5, :]
        h2 = x1_ref[...] * (1.0 + sc2) + sh2
        for j in range(8):
            stage_ref[RB_SPITCH * j:RB_SPITCH * j + RB_SUB, :] = h2[:, LANES * j:LANES * (j + 1)]

        for t in range(RB_SUB):
            slab = stage_ref[pl.ds(t, 8, stride=RB_SPITCH), :]
            for a in range(2):
                buf_ref[pl.ds(offs_ref[0, a, t], 8, stride=RB_PITCH), :] = slab

    @pl.when(s == RB_NSUB)
    def _experts():
        def copies(e, slot):
            return (pltpu.make_async_copy(wgu_hbm.at[e], wgu_buf.at[slot], sem.at[slot, 0]),
                    pltpu.make_async_copy(wd_hbm.at[e], wd_buf.at[slot], sem.at[slot, 1]))

        def run_expert(e, slot):
            lo_row = start_ref[e]
            hi_row = lo_row + cnt_ref[b, e]

            c_lo = lax.shift_right_logical(lo_row, RB_CHUNK_BITS)
            c_hi = lax.shift_right_logical(hi_row + (RB_CHUNK - 1), RB_CHUNK_BITS)
            row = lax.broadcasted_iota(jnp.int32, (RB_CHUNK, 1), 0)

            def load(c):
                base = pl.multiple_of(c * (8 * RB_PITCH), 8)
                return [buf_ref[pl.ds(base + RB_PITCH * j, RB_CHUNK), :] for j in range(8)]

            def store(c, tiles, y):
                base = pl.multiple_of(c * (8 * RB_PITCH), 8)
                mine = (row >= lo_row - c * RB_CHUNK) & (row < hi_row - c * RB_CHUNK)
                for j in range(8):
                    buf_ref[pl.ds(base + RB_PITCH * j, RB_CHUNK), :] = jnp.where(
                        mine, y[:, LANES * j:LANES * (j + 1)], tiles[j])

            def group(i, carry):
                cs = [c_lo + RB_GROUP * i]
                for k in range(1, RB_GROUP):
                    cs.append(jnp.where(cs[0] + k < c_hi, cs[0] + k, RB_NCHUNK + k))
                tiles = [load(c) for c in cs]
                x = jnp.concatenate([jnp.concatenate(t, axis=-1) for t in tiles], axis=0).astype(BF16)
                au = jnp.dot(x, wgu_buf[slot], preferred_element_type=F32)
                z = (_silu(au[:, :D_EXPERT]) * au[:, D_EXPERT:]).astype(BF16)
                y = jnp.dot(z, wd_buf[slot], preferred_element_type=F32)
                for k, c in enumerate(cs):
                    store(c, tiles[k], y[RB_CHUNK * k:RB_CHUNK * (k + 1)])
                return carry

            lax.fori_loop(0, lax.div(c_hi - c_lo + (RB_GROUP - 1), RB_GROUP), group, 0)

        for e in range(RB_WSLOTS - 1):
            for c in copies(e, e):
                c.start()

        def ring_body(i, carry):
            for k in range(RB_WSLOTS):
                e = RB_WSLOTS * i + k
                ahead = e + RB_WSLOTS - 1

                @pl.when(ahead < N_EXPERTS)
                def _():
                    for c in copies(ahead, (k + RB_WSLOTS - 1) % RB_WSLOTS):
                        c.start()
                for c in copies(e, k):
                    c.wait()
                run_expert(e, k)
            return carry
        lax.fori_loop(0, N_EXPERTS // RB_WSLOTS, ring_body, 0)

    @pl.when(s > RB_NSUB)
    def _combine():
        for t in range(RB_SUB):
            acc = None
            for a in range(2):
                term = wts_ref[0, a, t] * buf_ref[pl.ds(offs_ref[0, a, t], 8, stride=RB_PITCH), :]
                acc = term if acc is None else acc + term
            stage_ref[pl.ds(t, 8, stride=RB_SPITCH), :] = acc
        gt2 = mod_ref[0, 5:6, :]
        f = jnp.concatenate([stage_ref[RB_SPITCH * j:RB_SPITCH * j + RB_SUB, :] for j in range(8)], axis=-1)
        o_ref[...] = _layer_norm(DEEPNORM_ALPHA * x1_ref[...] + (1.0 + gt2) * f, ln2g_ref[...], ln2b_ref[...])


def _rb_retile(a):
    tiles, two, t = a.shape
    return a.reshape(tiles, two, t // RB_SUB, RB_SUB).transpose(0, 2, 1, 3).reshape(-1, two, RB_SUB)


def _rb_offsets(cnt, e12, rank12):
    start = jnp.cumsum(cnt, axis=-1) - cnt
    start_t = jnp.repeat(start, e12.shape[0] // cnt.shape[0], axis=0)[:, None, None, :]
    hit = e12[..., None] == jnp.arange(N_EXPERTS, dtype=jnp.int32)
    p = jnp.sum(jnp.where(hit, start_t, 0), axis=-1) + rank12
    return lax.shift_right_logical(p, RB_CHUNK_BITS) * (8 * RB_PITCH) + (p & (RB_CHUNK - 1))


def _rb_moe(x1, modp, cnt, offs, wts, wgu_bf16, wd_bf16, ln2g, ln2b):
    n, d = x1.shape
    bsz = n // SEQ
    nsteps = 2 * RB_NSUB + 1

    def sub_index(s):
        return jnp.where(s < RB_NSUB, s, jnp.where(s == RB_NSUB, RB_NSUB - 1, s - RB_NSUB - 1))

    def tile_map(b, s, cnt_r):
        return (b * RB_NSUB + sub_index(s), 0)

    def tile_map3(b, s, cnt_r):
        return (b * RB_NSUB + sub_index(s), 0, 0)

    def out_map(b, s, cnt_r):
        return (b * RB_NSUB + jnp.maximum(s - RB_NSUB - 1, 0), 0)

    const = lambda shp: pl.BlockSpec(shp, lambda b, s, cnt_r: tuple(0 for _ in shp))
    anyspec = pl.BlockSpec(memory_space=pl.ANY)
    grid_spec = pltpu.PrefetchScalarGridSpec(
        num_scalar_prefetch=1,
        grid=(bsz, nsteps),
        in_specs=[pl.BlockSpec((RB_SUB, d), tile_map),
                  pl.BlockSpec((1, 6, d), lambda b, s, cnt_r: (b, 0, 0)),
                  pl.BlockSpec((1, 2, RB_SUB), tile_map3, memory_space=pltpu.SMEM),
                  pl.BlockSpec((1, 2, RB_SUB), tile_map3, memory_space=pltpu.SMEM),
                  anyspec, anyspec,
                  const((1, d)), const((1, d))],
        out_specs=pl.BlockSpec((RB_SUB, d), out_map),
        scratch_shapes=[pltpu.VMEM(((RB_NCHUNK + RB_GROUP) * 8 * RB_PITCH, LANES), F32),
                        pltpu.VMEM((8 * RB_SPITCH, LANES), F32),
                        pltpu.VMEM((RB_WSLOTS, d, 2 * D_EXPERT), BF16),
                        pltpu.VMEM((RB_WSLOTS, D_EXPERT, d), BF16),
                        pltpu.SMEM((N_EXPERTS,), jnp.int32),
                        pltpu.SemaphoreType.DMA((RB_WSLOTS, 2))])
    return pl.pallas_call(
        _rb_kernel,
        grid_spec=grid_spec,
        out_shape=jax.ShapeDtypeStruct((n, d), F32),
        compiler_params=_cparams(("arbitrary", "arbitrary")),
        name="moe_routed_ln",
    )(cnt, x1, modp, offs, wts, wgu_bf16, wd_bf16, ln2g, ln2b)


def _sample_in_kernel(x_ref, mod_ref, win_ref, ctx_ref, h0_ref, convw_ref, convb_ref,
                      wlo_ref, whi_ref, bgate_ref, lam_ref,
                      ylru_ref, q_ref, k_ref, v_ref, cstate_ref, hnew_ref):
    sh1 = mod_ref[0:DEC_BATCH, 0:D_MODEL]
    sc1 = mod_ref[0:DEC_BATCH, D_MODEL:2 * D_MODEL]
    h = x_ref[...] * (1.0 + sc1) + sh1
    z = _dot(h, win_ref[...], True)
    xb = z[:, :LRU_WIDTH]
    gate = z[:, LRU_WIDTH:2 * LRU_WIDTH]
    c0 = ctx_ref[:, 0, :]
    c1 = ctx_ref[:, 1, :]
    c2 = ctx_ref[:, 2, :]
    xc = (convb_ref[...] + convw_ref[0:1, :] * c0 + convw_ref[1:2, :] * c1
          + convw_ref[2:3, :] * c2 + convw_ref[3:4, :] * xb)
    cstate_ref[:, 0, :] = c1
    cstate_ref[:, 1, :] = c2
    cstate_ref[:, 2, :] = xb
    sp = _softplus(-lam_ref[...])
    a, bterm = _lru_gates(xc, wlo_ref[...], whi_ref[...], bgate_ref[...], sp, True)
    hn = a * h0_ref[...] + bterm
    hnew_ref[...] = hn
    ylru_ref[...] = hn * _gelu_tanh(gate)
    low = lax.broadcasted_iota(jnp.int32, (DEC_BATCH, LANES), 1) < HEAD_DIM
    for c in range(4):
        qc = z[:, 2 * LRU_WIDTH + LANES * c:2 * LRU_WIDTH + LANES * (c + 1)]
        q_ref[pl.ds(c, DEC_BATCH, stride=N_HEADS), :] = jnp.where(low, qc, 0.0)
        q_ref[pl.ds(c + 4, DEC_BATCH, stride=N_HEADS), :] = jnp.where(low, 0.0, qc)
    k_ref[...] = z[:, 2 * LRU_WIDTH + ATTN_WIDTH:2 * LRU_WIDTH + ATTN_WIDTH + KV_WIDTH]
    v_ref[...] = z[:, 2 * LRU_WIDTH + ATTN_WIDTH + KV_WIDTH:]


def _sample_in(x, mod, w_in_p, ctx, h0, conv_w, conv_b, wlo, whi, bgate, lam):
    n = DEC_BATCH
    outs = [jax.ShapeDtypeStruct((n, LRU_WIDTH), F32),
            jax.ShapeDtypeStruct((n * N_HEADS, LANES), F32),
            jax.ShapeDtypeStruct((n, KV_WIDTH), F32),
            jax.ShapeDtypeStruct((n, KV_WIDTH), F32),
            jax.ShapeDtypeStruct((n, CONV_WIDTH - 1, LRU_WIDTH), F32),
            jax.ShapeDtypeStruct((n, LRU_WIDTH), F32)]
    return pl.pallas_call(
        _sample_in_kernel,
        out_shape=outs,
        compiler_params=pltpu.CompilerParams(vmem_limit_bytes=VMEM_LIMIT),
        name="sample_inproj_rglru",
    )(x, mod, w_in_p, ctx, h0, conv_w, conv_b, wlo, whi, bgate, lam)


def _sample_attn_kernel(q_ref, kn_ref, vn_ref, ck_ref, cv_ref, sink_ref, y_ref, nk_ref, nv_ref, *, bb):
    rows = lax.broadcasted_iota(jnp.int32, (WINDOW, KV_WIDTH), 0)
    nh = N_HEADS
    q_all = q_ref[...].reshape(bb * nh, LANES)
    kcat = ck_ref[...].reshape(bb * WINDOW, KV_WIDTH)
    vcat = cv_ref[...].reshape(bb * WINDOW, KV_WIDTH)
    kn_rep = jnp.broadcast_to(kn_ref[...][:, None, :], (bb, nh, KV_WIDTH)).reshape(bb * nh, KV_WIDTH)
    vn_rep = jnp.broadcast_to(vn_ref[...][:, None, :], (bb, nh, KV_WIDTH)).reshape(bb * nh, KV_WIDTH)
    sink = jnp.concatenate([sink_ref[...]] * bb, axis=0)
    s_full = _dot_nt(q_all, kcat, True)
    s = jnp.concatenate([s_full[nh * b:nh * (b + 1), WINDOW * b:WINDOW * (b + 1)] for b in range(bb)],
                        axis=0) * ATTN_SCALE
    s_self = jnp.sum(q_all * kn_rep, axis=-1, keepdims=True) * ATTN_SCALE
    m = jnp.maximum(jnp.maximum(jnp.max(s, axis=-1, keepdims=True), s_self), sink)
    e = jnp.exp(s - m)
    e_self = jnp.exp(s_self - m)
    den = jnp.sum(e, axis=-1, keepdims=True) + e_self + jnp.exp(sink - m)
    inv = 1.0 / den
    p = e * inv
    zero = jnp.zeros((nh, WINDOW), F32)
    p_wide = jnp.concatenate(
        [jnp.concatenate([p[nh * b:nh * (b + 1)] if c == b else zero for c in range(bb)], axis=-1)
         for b in range(bb)], axis=0)
    o = _dot(p_wide, vcat, True) + (e_self * inv) * vn_rep
    y_ref[...] = o.reshape(bb, nh, LANES)
    for b in range(bb):
        nk_ref[b] = jnp.where(rows == WINDOW - 1, kn_ref[b:b + 1, :], pltpu.roll(ck_ref[b], WINDOW - 1, axis=0))
        nv_ref[b] = jnp.where(rows == WINDOW - 1, vn_ref[b:b + 1, :], pltpu.roll(cv_ref[b], WINDOW - 1, axis=0))


def _sample_attn(q3, kn, vn, cache_k, cache_v, sinks, bb=16):
    n = DEC_BATCH
    kern = functools.partial(_sample_attn_kernel, bb=bb)
    return pl.pallas_call(
        kern,
        grid=(n // bb,),
        in_specs=[pl.BlockSpec((bb, N_HEADS, LANES), lambda i: (i, 0, 0)),
                  pl.BlockSpec((bb, KV_WIDTH), lambda i: (i, 0)),
                  pl.BlockSpec((bb, KV_WIDTH), lambda i: (i, 0)),
                  pl.BlockSpec((bb, WINDOW, KV_WIDTH), lambda i: (i, 0, 0)),
                  pl.BlockSpec((bb, WINDOW, KV_WIDTH), lambda i: (i, 0, 0)),
                  pl.BlockSpec((N_HEADS, 1), lambda i: (0, 0))],
        out_specs=[pl.BlockSpec((bb, N_HEADS, LANES), lambda i: (i, 0, 0)),
                   pl.BlockSpec((bb, WINDOW, KV_WIDTH), lambda i: (i, 0, 0)),
                   pl.BlockSpec((bb, WINDOW, KV_WIDTH), lambda i: (i, 0, 0))],
        out_shape=[jax.ShapeDtypeStruct((n, N_HEADS, LANES), F32),
                   jax.ShapeDtypeStruct((n, WINDOW, KV_WIDTH), F32),
                   jax.ShapeDtypeStruct((n, WINDOW, KV_WIDTH), F32)],
        compiler_params=_cparams(("arbitrary",)),
        name="sample_cache_attention",
    )(q3, kn, vn, cache_k, cache_v, sinks.reshape(N_HEADS, 1))


def _sample_out_kernel(x_ref, ylru_ref, yatt_ref, mod_ref, glru_ref, gattn_ref, wout_ref,
                       ln1g_ref, ln1b_ref, wr_ref, br_ref, x1_ref, comb_ref):
    low = lax.broadcasted_iota(jnp.int32, (DEC_BATCH, LANES), 1) < HEAD_DIM
    yatt = jnp.concatenate(
        [jnp.where(low, yatt_ref[pl.ds(c, DEC_BATCH, stride=N_HEADS), :],
                   yatt_ref[pl.ds(c + 4, DEC_BATCH, stride=N_HEADS), :]) for c in range(4)], axis=-1)
    gt1 = mod_ref[0:DEC_BATCH, 2 * D_MODEL:3 * D_MODEL]
    sh2 = mod_ref[0:DEC_BATCH, 3 * D_MODEL:4 * D_MODEL]
    sc2 = mod_ref[0:DEC_BATCH, 4 * D_MODEL:5 * D_MODEL]
    x1, comb = _outproj_body(x_ref[...], ylru_ref[...], yatt, sh2, sc2, gt1,
                             glru_ref[...], gattn_ref[...], wout_ref[...], ln1g_ref[...], ln1b_ref[...],
                             wr_ref[...], br_ref[...], True)
    x1_ref[...] = x1
    comb_ref[...] = comb


def _sample_out(x, ylru, yatt2d, mod, glru, gattn, wout_p, ln1g, ln1b, wr, br):
    n = DEC_BATCH
    return pl.pallas_call(
        _sample_out_kernel,
        out_shape=[jax.ShapeDtypeStruct((n, D_MODEL), F32), jax.ShapeDtypeStruct((n, ROUTE_LANES), F32)],
        compiler_params=pltpu.CompilerParams(vmem_limit_bytes=VMEM_LIMIT),
        name="sample_outproj_ln_route",
    )(x, ylru, yatt2d, mod, glru, gattn, wout_p, ln1g, ln1b, wr, br)


def _block_diag_halves(w_a, w_x):
    def bd(w4):
        eye = jnp.eye(4, dtype=w4.dtype)
        return (w4[:, :, None, :] * eye[:, None, :, None]).reshape(256, 256)
    lo = jnp.concatenate([bd(w_a[:4]), bd(w_x[:4])], axis=1)
    hi = jnp.concatenate([bd(w_a[4:]), bd(w_x[4:])], axis=1)
    return lo, hi


def kernel(x_prompt, x_sample, c_prompt, c_sample, state_conv, state_h, cache_k, cache_v, w_ada, b_ada, w_in,
           conv_w, conv_b, w_rg_a, b_rg_a, w_rg_x, b_rg_x, lru_lambda, sinks, g_lru, g_attn, w_out, ln1_g, ln1_b,
           w_group, b_group, w_router, b_router, w_gate, w_up, w_down, ln2_g, ln2_b):
    d = D_MODEL
    perm = jnp.asarray(HEAD_PERM)
    w_in0 = w_in[0]
    q0 = 2 * LRU_WIDTH
    w_in_p = jnp.concatenate([w_in0[:, :q0], w_in0[:, q0:q0 + ATTN_WIDTH][:, perm], w_in0[:, q0 + ATTN_WIDTH:]],
                             axis=1)
    w_out0 = w_out[0]
    w_out_p = jnp.concatenate([w_out0[:LRU_WIDTH], w_out0[LRU_WIDTH:][perm]], axis=0)
    g_attn_p = g_attn[0][perm].reshape(1, -1)
    glru = g_lru[0].reshape(1, -1)
    wlo, whi = _block_diag_halves(w_rg_a[0], w_rg_x[0])
    bgate = jnp.concatenate([b_rg_a[0].reshape(-1), b_rg_x[0].reshape(-1)]).reshape(1, -1)
    lam = lru_lambda[0].reshape(1, -1)
    convw = conv_w[0]
    convb = conv_b[0].reshape(1, -1)
    ln1g, ln1b = ln1_g[0].reshape(1, -1), ln1_b[0].reshape(1, -1)
    ln2g, ln2b = ln2_g[0].reshape(1, -1), ln2_b[0].reshape(1, -1)
    wr = jnp.concatenate([jnp.transpose(w_router[0], (1, 0, 2)).reshape(d, N_EXPERTS), w_group[0],
                          jnp.zeros((d, ROUTE_LANES - N_EXPERTS - N_GROUPS), F32)], axis=1)
    br = jnp.concatenate([b_router[0].reshape(-1), b_group[0],
                          jnp.zeros((ROUTE_LANES - N_EXPERTS - N_GROUPS,), F32)]).reshape(1, -1)
    sink_p = sinks[0]

    c_all = jnp.concatenate([c_sample, c_prompt, jnp.zeros((8 - BATCH, d), F32)], axis=0)
    mod = _ada(c_all, w_ada[0], b_ada[0])
    modp = mod[DEC_BATCH:DEC_BATCH + BATCH].reshape(BATCH, 6, d)

    zlru, zqkv, kvlast = _inproj(x_prompt, modp, w_in_p.astype(BF16))
    ylru, cstate8, hlast8, wgu_b, wd_b = _lru(zlru, convw, convb, wlo.astype(BF16), whi.astype(BF16), bgate, lam,
                                                   w_gate[0], w_up[0], w_down[0])
    yatt = _attn(zqkv, sink_p)
    n_p = BATCH * SEQ
    x1_p, info, cntf = _outproj_prompt(x_prompt.reshape(n_p, d), ylru.reshape(n_p, LRU_WIDTH),
                                       yatt.reshape(n_p, ATTN_WIDTH), modp, glru, g_attn_p, w_out_p.astype(BF16),
                                       ln1g, ln1b, wr, br, tm=OUTPROJ_TILE)
    cnt = cntf[:, 0, :N_EXPERTS].astype(jnp.int32)
    offs = _rb_retile(_rb_offsets(cnt, info[:, 0:2].astype(jnp.int32), info[:, 4:6].astype(jnp.int32)))
    y_p = _rb_moe(x1_p, modp, cnt, offs, _rb_retile(info[:, 2:4]), wgu_b, wd_b, ln2g, ln2b)

    ylru_s, q2d, kn, vn, cstate_s, hnew_s = _sample_in(
        x_sample.reshape(DEC_BATCH, d), mod, w_in_p, state_conv[0], state_h[0],
        convw, convb, wlo, whi, bgate, lam)
    yatt3, newk, newv = _sample_attn(q2d.reshape(DEC_BATCH, N_HEADS, LANES), kn, vn,
                                     cache_k[0].reshape(DEC_BATCH, WINDOW, KV_WIDTH),
                                     cache_v[0].reshape(DEC_BATCH, WINDOW, KV_WIDTH), sink_p)
    x1_s, comb_s = _sample_out(x_sample.reshape(DEC_BATCH, d), ylru_s, yatt3.reshape(DEC_BATCH * N_HEADS, LANES),
                               mod, glru, g_attn_p, w_out_p, ln1g, ln1b, wr, br)
    y_s = _moe_dense(x1_s, comb_s, mod, wgu_b, wd_b, ln2g, ln2b, DEC_BATCH)

    return (y_p.reshape(BATCH, SEQ, d),
            y_s.reshape(DEC_BATCH, 1, d),
            cstate8[:, 5:8][None],
            hlast8[:, 7][None],
            kvlast[:, :, :KV_WIDTH].reshape(1, BATCH, WINDOW, N_KV_HEADS, HEAD_DIM),
            kvlast[:, :, KV_WIDTH:].reshape(1, BATCH, WINDOW, N_KV_HEADS, HEAD_DIM),
            cstate_s[None],
            hnew_s[None],
            newk.reshape(1, DEC_BATCH, WINDOW, N_KV_HEADS, HEAD_DIM),
            newv.reshape(1, DEC_BATCH, WINDOW, N_KV_HEADS, HEAD_DIM))
```

```python
import functools

import jax
import jax.numpy as jnp
import numpy as np
from jax import lax
from jax.experimental import pallas as pl
from jax.experimental.pallas import tpu as pltpu

F32 = jnp.float32
BF16 = jnp.bfloat16
HIGHEST = lax.Precision.HIGHEST

D_MODEL = 1024
BATCH = 4
SEQ = 4096
DEC_BATCH = 128
LRU_WIDTH = 512
LRU_BLOCKS = 8
LRU_BLOCK = 64
CONV_WIDTH = 4
LRU_C = 8.0
N_HEADS = 8
N_KV_HEADS = 2
HEAD_DIM = 64
ATTN_WIDTH = 512
KV_WIDTH = 128
WINDOW = 128
IN_WIDTH = 2 * LRU_WIDTH + ATTN_WIDTH + 2 * KV_WIDTH
N_GROUPS = 4
EXPERTS_PER_GROUP = 8
N_EXPERTS = 32
D_EXPERT = 256
DEEPNORM_ALPHA = 2.0 ** 0.25
LN_EPS = 1e-5
RMS_EPS = 1e-6
ATTN_SCALE = HEAD_DIM ** -0.5

LANES = 128
ROUTE_LANES = 128
ROUTE_INFO = 40
VMEM_LIMIT = 56 * 1024 * 1024

HEAD_PERM = np.concatenate(
    [np.concatenate([np.arange(64 * c, 64 * c + 64), np.arange(64 * (c + 4), 64 * (c + 4) + 64)])
     for c in range(4)])


def _cparams(sem):
    return pltpu.CompilerParams(dimension_semantics=sem, vmem_limit_bytes=VMEM_LIMIT)


def _dot(a, b, exact):
    if exact:
        return jnp.dot(a, b, precision=HIGHEST, preferred_element_type=F32)
    return jnp.dot(a.astype(BF16), b.astype(BF16), preferred_element_type=F32)


def _dot_nt(a, b, exact):
    dn = (((1,), (1,)), ((), ()))
    if exact:
        return lax.dot_general(a, b, dn, precision=HIGHEST, preferred_element_type=F32)
    return lax.dot_general(a.astype(BF16), b.astype(BF16), dn, preferred_element_type=F32)


def _sigmoid(x):
    return 1.0 / (1.0 + jnp.exp(-x))


def _silu(x):
    return x * _sigmoid(x)


def _gelu_tanh(x):
    return 0.5 * x * (1.0 + jnp.tanh(np.sqrt(2.0 / np.pi).astype(np.float32) * (x + 0.044715 * (x * x * x))))


def _softplus(x):
    return jnp.maximum(x, 0.0) + jnp.log1p(jnp.exp(-jnp.abs(x)))


def _layer_norm(x, g, b):
    mu = jnp.mean(x, axis=-1, keepdims=True)
    xc = x - mu
    var = jnp.mean(xc * xc, axis=-1, keepdims=True)
    return xc * lax.rsqrt(var + LN_EPS) * g + b


def _rms_norm(x, g):
    return x * lax.rsqrt(jnp.mean(x * x, axis=-1, keepdims=True) + RMS_EPS) * g


def _ada_kernel(c_ref, w_ref, b_ref, o_ref):
    o_ref[...] = _dot(_silu(c_ref[...]), w_ref[...], True) + b_ref[...]


def _ada(c_all, w_ada, b_ada):
    rows = c_all.shape[0]
    bn = 1024
    return pl.pallas_call(
        _ada_kernel,
        grid=(6 * D_MODEL // bn,),
        in_specs=[pl.BlockSpec((rows, D_MODEL), lambda j: (0, 0)),
                  pl.BlockSpec((D_MODEL, bn), lambda j: (0, j)),
                  pl.BlockSpec((1, bn), lambda j: (0, j))],
        out_specs=pl.BlockSpec((rows, bn), lambda j: (0, j)),
        out_shape=jax.ShapeDtypeStruct((rows, 6 * D_MODEL), F32),
        compiler_params=_cparams(("arbitrary",)),
        name="ada_modulation",
    )(c_all, w_ada, b_ada.reshape(1, -1))


QKV_WIDTH = ATTN_WIDTH + 2 * KV_WIDTH


def _inproj_kernel(x_ref, mod_ref, w_ref, lru_ref, qkv_ref, kvlast_ref):
    sh1 = mod_ref[0, 0:1, :]
    sc1 = mod_ref[0, 1:2, :]
    h = x_ref[0] * (1.0 + sc1) + sh1
    z = _dot(h, w_ref[...], False)
    lru_ref[0] = z[:, :2 * LRU_WIDTH]
    qkv_ref[0] = z[:, 2 * LRU_WIDTH:].astype(BF16)
    kvlast_ref[0] = z[z.shape[0] - WINDOW:, 2 * LRU_WIDTH + ATTN_WIDTH:]


def _inproj(x, modp, w_in_bf16, tm=1024):
    b, t, d = x.shape
    return pl.pallas_call(
        _inproj_kernel,
        grid=(b, t // tm),
        in_specs=[pl.BlockSpec((1, tm, d), lambda i, j: (i, j, 0)),
                  pl.BlockSpec((1, 6, d), lambda i, j: (i, 0, 0)),
                  pl.BlockSpec((d, IN_WIDTH), lambda i, j: (0, 0))],
        out_specs=[pl.BlockSpec((1, tm, 2 * LRU_WIDTH), lambda i, j: (i, j, 0)),
                   pl.BlockSpec((1, tm, QKV_WIDTH), lambda i, j: (i, j, 0)),
                   pl.BlockSpec((1, WINDOW, 2 * KV_WIDTH), lambda i, j: (i, 0, 0))],
        out_shape=[jax.ShapeDtypeStruct((b, t, 2 * LRU_WIDTH), F32),
                   jax.ShapeDtypeStruct((b, t, QKV_WIDTH), BF16),
                   jax.ShapeDtypeStruct((b, WINDOW, 2 * KV_WIDTH), F32)],
        compiler_params=_cparams(("arbitrary", "arbitrary")),
        name="prompt_inproj",
    )(x, modp, w_in_bf16)


def _lru_gates(xc, wlo, whi, bgate, sp_neg_lam, exact):
    g_lo = _dot(xc[:, :256], wlo, exact)
    g_hi = _dot(xc[:, 256:], whi, exact)
    ga = jnp.concatenate([g_lo[:, :256], g_hi[:, :256]], axis=-1) + bgate[:, :LRU_WIDTH]
    gx = jnp.concatenate([g_lo[:, 256:], g_hi[:, 256:]], axis=-1) + bgate[:, LRU_WIDTH:]
    r = _sigmoid(ga)
    i = _sigmoid(gx)
    log_a = -LRU_C * r * sp_neg_lam
    a = jnp.exp(log_a)
    one_minus_a2 = -jnp.tanh(log_a) * (a * a + 1.0) if exact else 1.0 - a * a
    root = jnp.where(one_minus_a2 > 0.0, one_minus_a2 * lax.rsqrt(one_minus_a2), 0.0)
    bterm = root * (i * xc)
    return a, bterm


def _lru_kernel(z_ref, convw_ref, convb_ref, wlo_ref, whi_ref, bgate_ref, lam_ref, wg_ref, wu_ref, wd_ref,
                y_ref, cstate_ref, hlast_ref, wgub_ref, wdb_ref, tail_ref, carry_ref, *, tl):
    j = pl.program_id(1)

    @pl.when(j == 0)
    def _():
        tail_ref[...] = jnp.zeros_like(tail_ref)
        carry_ref[...] = jnp.zeros_like(carry_ref)

    wgub_ref[0, :, :D_EXPERT] = wg_ref[0].astype(BF16)
    wgub_ref[0, :, D_EXPERT:] = wu_ref[0].astype(BF16)
    wdb_ref[...] = wd_ref[...].astype(BF16)

    xb = z_ref[0, :, :LRU_WIDTH]
    gate = z_ref[0, :, LRU_WIDTH:]
    xc = convb_ref[...] + convw_ref[3:4, :] * xb
    rows8 = lax.broadcasted_iota(jnp.int32, (8, LRU_WIDTH), 0)
    tail = tail_ref[...]
    for back in (1, 2, 3):
        rolled = pltpu.roll(xb, back, axis=0)
        top = jnp.where(rows8 >= back, rolled[:8], pltpu.roll(tail, back, axis=0))
        shifted = jnp.concatenate([top, rolled[8:]], axis=0)
        xc = xc + convw_ref[3 - back:4 - back, :] * shifted
    tail_ref[...] = xb[tl - 8:, :]
    cstate_ref[0] = xb[tl - 8:, :]

    sp = _softplus(-lam_ref[...])
    a, bterm = _lru_gates(xc, wlo_ref[...], whi_ref[...], bgate_ref[...], sp, False)

    groups = tl // 8
    a = a.reshape(groups, 8, LRU_WIDTH)
    bterm = bterm.reshape(groups, 8, LRU_WIDTH)
    r8 = lax.broadcasted_iota(jnp.int32, (groups, 8, LRU_WIDTH), 1)
    s = 1
    while s < 8:
        a_sh = jnp.where(r8 >= s, pltpu.roll(a, s, axis=1), 1.0)
        b_sh = jnp.where(r8 >= s, pltpu.roll(bterm, s, axis=1), 0.0)
        bterm = a * b_sh + bterm
        a = a * a_sh
        s *= 2
    a_tot = jnp.broadcast_to(a[:, 7:8, :], (groups, 8, LRU_WIDTH))
    b_tot = jnp.broadcast_to(bterm[:, 7:8, :], (groups, 8, LRU_WIDTH))
    h_in = jnp.broadcast_to(carry_ref[7:8, :], (8, LRU_WIDTH))
    pieces = []
    for g in range(groups):
        pieces.append(a[g] * h_in + bterm[g])
        h_in = a_tot[g] * h_in + b_tot[g]
    h = jnp.concatenate(pieces, axis=0)
    carry_ref[...] = h_in
    hlast_ref[0] = h_in
    y_ref[0] = h * _gelu_tanh(gate)


def _lru(zin, conv_w, conv_b, wlo, whi, bgate, lam, w_gate, w_up, w_down, tl=512):
    b, t, _ = zin.shape
    steps = t // tl
    assert b * steps == N_EXPERTS
    d = D_MODEL
    kern = functools.partial(_lru_kernel, tl=tl)
    full = lambda shp: pl.BlockSpec(shp, lambda i, j: tuple(0 for _ in shp))
    per_step = lambda shp: pl.BlockSpec(shp, lambda i, j: (i * steps + j, 0, 0))
    return pl.pallas_call(
        kern,
        grid=(b, steps),
        in_specs=[pl.BlockSpec((1, tl, 2 * LRU_WIDTH), lambda i, j: (i, j, 0)),
                  full((CONV_WIDTH, LRU_WIDTH)), full((1, LRU_WIDTH)),
                  full((256, 512)), full((256, 512)), full((1, 2 * LRU_WIDTH)), full((1, LRU_WIDTH)),
                  per_step((1, d, D_EXPERT)), per_step((1, d, D_EXPERT)), per_step((1, D_EXPERT, d))],
        out_specs=[pl.BlockSpec((1, tl, LRU_WIDTH), lambda i, j: (i, j, 0)),
                   pl.BlockSpec((1, 8, LRU_WIDTH), lambda i, j: (i, 0, 0)),
                   pl.BlockSpec((1, 8, LRU_WIDTH), lambda i, j: (i, 0, 0)),
                   per_step((1, d, 2 * D_EXPERT)), per_step((1, D_EXPERT, d))],
        out_shape=[jax.ShapeDtypeStruct((b, t, LRU_WIDTH), F32),
                   jax.ShapeDtypeStruct((b, 8, LRU_WIDTH), F32),
                   jax.ShapeDtypeStruct((b, 8, LRU_WIDTH), F32),
                   jax.ShapeDtypeStruct((N_EXPERTS, d, 2 * D_EXPERT), BF16),
                   jax.ShapeDtypeStruct(w_down.shape, BF16)],
        scratch_shapes=[pltpu.VMEM((8, LRU_WIDTH), F32), pltpu.VMEM((8, LRU_WIDTH), F32)],
        compiler_params=_cparams(("arbitrary", "arbitrary")),
        name="prompt_rglru",
    )(zin, conv_w, conv_b, wlo, whi, bgate, lam, w_gate, w_up, w_down)


ATTN_BLOCKS = 16


def _attn_kernel(q_ref, k_ref, v_ref, sink_ref, o_ref, kprev_ref, vprev_ref):
    j = pl.program_id(1)

    @pl.when(j == 0)
    def _():
        kprev_ref[...] = jnp.zeros_like(kprev_ref)
        vprev_ref[...] = jnp.zeros_like(vprev_ref)

    blk = WINDOW
    lane = lax.broadcasted_iota(jnp.int32, (blk, LANES), 1)
    low = lane < HEAD_DIM
    qi = lax.broadcasted_iota(jnp.int32, (blk, 2 * blk), 0)
    sj = lax.broadcasted_iota(jnp.int32, (blk, 2 * blk), 1)
    rel = blk + qi - sj
    in_window = (rel >= 0) & (rel <= WINDOW)
    sink = sink_ref[...].reshape(N_HEADS, blk, 1)
    k_ext = jnp.concatenate([kprev_ref[...], k_ref[0]], axis=0)
    v_ext = jnp.concatenate([vprev_ref[...], v_ref[0]], axis=0)
    v_ext = jnp.concatenate([v_ext, jnp.ones_like(v_ext)], axis=-1)
    for n in range(ATTN_BLOCKS):
        q = q_ref[0, blk * n:blk * (n + 1), :]
        pieces = []
        for half in (0, 1):
            for c in range(4):
                qc = q[:, LANES * c:LANES * (c + 1)]
                pieces.append(jnp.where(low if half == 0 else ~low, qc, 0.0).astype(BF16))
        q8 = jnp.concatenate(pieces, axis=0)
        k_band = k_ext[blk * n:blk * (n + 2)]
        v_band = v_ext[blk * n:blk * (n + 2)]
        s = _dot_nt(q8, k_band, False) * ATTN_SCALE
        s = s.reshape(N_HEADS, blk, 2 * blk)
        valid = in_window & ((sj >= blk) | (j > 0)) if n == 0 else in_window
        s = jnp.where(valid[None], s, -jnp.inf)
        m = jnp.maximum(jnp.max(s, axis=-1, keepdims=True), sink)
        e = jnp.exp(s - m).reshape(N_HEADS * blk, 2 * blk)
        ov = _dot(e, v_band, False)
        den = ov[:, KV_WIDTH:] + jnp.exp(sink - m).reshape(N_HEADS * blk, 1)
        o8 = ov[:, :KV_WIDTH] * (1.0 / den)
        cols = []
        for c in range(4):
            cols.append(jnp.where(low, o8[blk * c:blk * (c + 1)], o8[blk * (c + 4):blk * (c + 5)]))
        o_ref[0, blk * n:blk * (n + 1), :] = jnp.concatenate(cols, axis=-1)
    kprev_ref[...] = k_ref[0, blk * (ATTN_BLOCKS - 1):, :]
    vprev_ref[...] = v_ref[0, blk * (ATTN_BLOCKS - 1):, :]


def _attn(qkv, sinks):
    b, t, _ = qkv.shape
    blk = WINDOW
    tq = blk * ATTN_BLOCKS
    sink_col = jnp.repeat(sinks.astype(F32), blk).reshape(N_HEADS * blk, 1)
    kcol = ATTN_WIDTH // KV_WIDTH
    return pl.pallas_call(
        _attn_kernel,
        grid=(b, t // tq),
        in_specs=[pl.BlockSpec((1, tq, ATTN_WIDTH), lambda i, j: (i, j, 0)),
                  pl.BlockSpec((1, tq, KV_WIDTH), lambda i, j: (i, j, kcol)),
                  pl.BlockSpec((1, tq, KV_WIDTH), lambda i, j: (i, j, kcol + 1)),
                  pl.BlockSpec((N_HEADS * blk, 1), lambda i, j: (0, 0))],
        out_specs=pl.BlockSpec((1, tq, ATTN_WIDTH), lambda i, j: (i, j, 0)),
        out_shape=jax.ShapeDtypeStruct((b, t, ATTN_WIDTH), F32),
        scratch_shapes=[pltpu.VMEM((blk, KV_WIDTH), BF16), pltpu.VMEM((blk, KV_WIDTH), BF16)],
        compiler_params=_cparams(("arbitrary", "arbitrary")),
        name="prompt_window_attention",
    )(qkv, qkv, qkv, sink_col)


def _route(h2, wr, br, exact):
    t = h2.shape[0]
    logits = _dot(h2, wr, exact) + br
    lane = lax.broadcasted_iota(jnp.int32, (t, ROUTE_LANES), 1).astype(F32)
    neg = -jnp.inf
    big = float(ROUTE_LANES)
    is_g = (lane >= N_EXPERTS) & (lane < N_EXPERTS + N_GROUPS)
    lg = jnp.where(is_g, logits, neg)
    mg = jnp.max(lg, axis=-1, keepdims=True)
    g_val = 1.0 / jnp.sum(jnp.where(is_g, jnp.exp(lg - mg), 0.0), axis=-1, keepdims=True)
    g_lane = jnp.min(jnp.where((lg == mg) & is_g, lane, big), axis=-1, keepdims=True)
    g_idx = g_lane - N_EXPERTS
    in_grp = (lane >= g_idx * EXPERTS_PER_GROUP) & (lane < (g_idx + 1.0) * EXPERTS_PER_GROUP)
    le = jnp.where(in_grp, logits, neg)
    me = jnp.max(le, axis=-1, keepdims=True)
    se = jnp.sum(jnp.where(in_grp, jnp.exp(le - me), 0.0), axis=-1, keepdims=True)
    l1 = jnp.min(jnp.where((le == me) & in_grp, lane, big), axis=-1, keepdims=True)
    rest = in_grp & (lane != l1)
    le2 = jnp.where(rest, le, neg)
    me2 = jnp.max(le2, axis=-1, keepdims=True)
    l2 = jnp.min(jnp.where((le2 == me2) & rest, lane, big), axis=-1, keepdims=True)
    v1 = 1.0 / se
    v2 = jnp.exp(me2 - me) / se
    tot = v1 + v2
    w1 = g_val * v1 / tot
    w2 = g_val * v2 / tot
    comb = jnp.where(lane == l1, w1, 0.0) + jnp.where(lane == l2, w2, 0.0)
    return (comb + jnp.where(lane == ROUTE_INFO, l1, 0.0) + jnp.where(lane == ROUTE_INFO + 1, l2, 0.0)
            + jnp.where(lane == ROUTE_INFO + 2, w1, 0.0) + jnp.where(lane == ROUTE_INFO + 3, w2, 0.0))


def _outproj_body(x, ylru, yatt, sh2, sc2, gt1, glru, gattn, wout, ln1g, ln1b, wr, br, exact):
    mixin = jnp.concatenate([_rms_norm(ylru, glru), _rms_norm(yatt, gattn)], axis=-1)
    mix = _dot(mixin, wout, exact)
    x1 = _layer_norm(DEEPNORM_ALPHA * x + (1.0 + gt1) * mix, ln1g, ln1b)
    h2 = x1 * (1.0 + sc2) + sh2
    return x1, _route(h2, wr, br, exact)


def _outproj_prompt_kernel(x_ref, ylru_ref, yatt_ref, mod_ref, glru_ref, gattn_ref, wout_ref,
                           ln1g_ref, ln1b_ref, wr_ref, br_ref, x1_ref, info_ref, cnt_ref, tri_ref, carry_ref,
                           *, tm, per_seq):
    i = pl.program_id(0)

    @pl.when(i == 0)
    def _():
        r = lax.broadcasted_iota(jnp.int32, (tm, tm), 0)
        c = lax.broadcasted_iota(jnp.int32, (tm, tm), 1)
        tri_ref[...] = jnp.where(c < r, 1.0, 0.0).astype(BF16)

    @pl.when(i % per_seq == 0)
    def _():
        carry_ref[...] = jnp.zeros_like(carry_ref)

    gt1 = mod_ref[0, 2:3, :]
    sh2 = mod_ref[0, 3:4, :]
    sc2 = mod_ref[0, 4:5, :]
    combs = []
    nsplit = 2
    for h in range(nsplit):
        rows = slice(h * (tm // nsplit), (h + 1) * (tm // nsplit))
        x1_h, comb_h = _outproj_body(x_ref[rows, :], ylru_ref[rows, :], yatt_ref[rows, :], sh2, sc2, gt1,
                                     glru_ref[...], gattn_ref[...], wout_ref[...], ln1g_ref[...], ln1b_ref[...],
                                     wr_ref[...], br_ref[...], False)
        x1_ref[rows, :] = x1_h
        combs.append(comb_h)
    comb = jnp.concatenate(combs, axis=0)
    lane = lax.broadcasted_iota(jnp.int32, (tm, ROUTE_LANES), 1).astype(F32)
    l1 = jnp.sum(jnp.where(lane == ROUTE_INFO, comb, 0.0), axis=-1, keepdims=True)
    l2 = jnp.sum(jnp.where(lane == ROUTE_INFO + 1, comb, 0.0), axis=-1, keepdims=True)
    o1 = lane == l1
    o2 = lane == l2
    onehot = jnp.where(o1 | o2, 1.0, 0.0)
    before = jnp.dot(tri_ref[...], onehot.astype(BF16), preferred_element_type=F32) + carry_ref[0:1, :]
    rank1 = jnp.sum(jnp.where(o1, before, 0.0), axis=-1, keepdims=True)
    rank2 = jnp.sum(jnp.where(o2, before, 0.0), axis=-1, keepdims=True)
    total = carry_ref[0:1, :] + jnp.sum(onehot, axis=0, keepdims=True)
    carry_ref[...] = jnp.broadcast_to(total, carry_ref.shape)
    cnt_ref[0] = jnp.broadcast_to(total, (8, ROUTE_LANES))
    info = (comb + jnp.where(lane == ROUTE_INFO + 4, rank1, 0.0) + jnp.where(lane == ROUTE_INFO + 5, rank2, 0.0))
    info_ref[0] = jnp.transpose(info)[ROUTE_INFO:ROUTE_INFO + 8, :]


OUTPROJ_TILE = 1024


def _outproj_prompt(x2d, ylru2d, yatt2d, modp, glru, gattn, wout_bf16, ln1g, ln1b, wr, br, tm=OUTPROJ_TILE):
    n, d = x2d.shape
    per_seq = SEQ // tm
    full = lambda shp: pl.BlockSpec(shp, lambda i: tuple(0 for _ in shp))
    kern = functools.partial(_outproj_prompt_kernel, tm=tm, per_seq=per_seq)
    return pl.pallas_call(
        kern,
        grid=(n // tm,),
        in_specs=[pl.BlockSpec((tm, d), lambda i: (i, 0)),
                  pl.BlockSpec((tm, LRU_WIDTH), lambda i: (i, 0)),
                  pl.BlockSpec((tm, ATTN_WIDTH), lambda i: (i, 0)),
                  pl.BlockSpec((1, 6, d), lambda i: (i // per_seq, 0, 0)),
                  full((1, LRU_WIDTH)), full((1, ATTN_WIDTH)), full((d, d)),
                  full((1, d)), full((1, d)), full((d, ROUTE_LANES)), full((1, ROUTE_LANES))],
        out_specs=[pl.BlockSpec((tm, d), lambda i: (i, 0)),
                   pl.BlockSpec((1, 8, tm), lambda i: (i, 0, 0)),
                   pl.BlockSpec((1, 8, ROUTE_LANES), lambda i: (i // per_seq, 0, 0))],
        out_shape=[jax.ShapeDtypeStruct((n, d), F32),
                   jax.ShapeDtypeStruct((n // tm, 8, tm), F32),
                   jax.ShapeDtypeStruct((n // SEQ, 8, ROUTE_LANES), F32)],
        scratch_shapes=[pltpu.VMEM((tm, tm), BF16), pltpu.VMEM((8, ROUTE_LANES), F32)],
        compiler_params=_cparams(("arbitrary",)),
        name="prompt_outproj_ln_route",
    )(x2d, ylru2d, yatt2d, modp, glru, gattn, wout_bf16, ln1g, ln1b, wr, br)


DENSE_EXPERTS_PER_STEP = 4


def _moe_kernel(x1_ref, comb_ref, sh2_ref, sc2_ref, gt2_ref, wgu_ref, wd_ref, ln2g_ref, ln2b_ref,
                o_ref, h2_ref, acc_ref):
    g = pl.program_id(1)

    @pl.when(g == 0)
    def _():
        h2_ref[...] = (x1_ref[...] * (1.0 + sc2_ref[...]) + sh2_ref[...]).astype(BF16)
        acc_ref[...] = jnp.zeros_like(acc_ref)

    h2 = h2_ref[...]
    comb = comb_ref[...]
    lane = lax.broadcasted_iota(jnp.int32, comb.shape, 1)
    part = None
    for k in range(DENSE_EXPERTS_PER_STEP):
        au = jnp.dot(h2, wgu_ref[k], preferred_element_type=F32)
        c_e = jnp.sum(jnp.where(lane == g * DENSE_EXPERTS_PER_STEP + k, comb, 0.0), axis=-1, keepdims=True)
        z = _silu(au[:, :D_EXPERT]) * au[:, D_EXPERT:] * c_e
        y = jnp.dot(z.astype(BF16), wd_ref[k], preferred_element_type=F32)
        part = y if part is None else part + y
    acc_ref[...] += part

    @pl.when(g == N_EXPERTS // DENSE_EXPERTS_PER_STEP - 1)
    def _():
        o_ref[...] = _layer_norm(DEEPNORM_ALPHA * x1_ref[...] + (1.0 + gt2_ref[...]) * acc_ref[...],
                                 ln2g_ref[...], ln2b_ref[...])


def _moe_dense(x1, comb, mod, wgu_bf16, wd_bf16, ln2g, ln2b, tm):
    n, d = x1.shape
    eg = DENSE_EXPERTS_PER_STEP
    mspec = lambda k: pl.BlockSpec((tm, d), lambda i, e: (i, k))
    full = lambda shp: pl.BlockSpec(shp, lambda i, e: tuple(0 for _ in shp))
    return pl.pallas_call(
        _moe_kernel,
        grid=(n // tm, N_EXPERTS // eg),
        in_specs=[pl.BlockSpec((tm, d), lambda i, e: (i, 0)),
                  pl.BlockSpec((tm, ROUTE_LANES), lambda i, e: (i, 0)),
                  mspec(3), mspec(4), mspec(5),
                  pl.BlockSpec((eg, d, 2 * D_EXPERT), lambda i, e: (e, 0, 0)),
                  pl.BlockSpec((eg, D_EXPERT, d), lambda i, e: (e, 0, 0)),
                  full((1, d)), full((1, d))],
        out_specs=pl.BlockSpec((tm, d), lambda i, e: (i, 0)),
        out_shape=jax.ShapeDtypeStruct((n, d), F32),
        scratch_shapes=[pltpu.VMEM((tm, d), BF16), pltpu.VMEM((tm, d), F32)],
        compiler_params=_cparams(("arbitrary", "arbitrary")),
        name="moe_dense_ln",
    )(x1, comb, mod, mod, mod, wgu_bf16, wd_bf16, ln2g, ln2b)


RB_SUB = 512
RB_NSUB = SEQ // RB_SUB
RB_CHUNK = 96
RB_NCHUNK = -(-2 * SEQ // RB_CHUNK)
RB_PITCH = RB_CHUNK + 8
RB_SPITCH = RB_SUB + 8
RB_GROUP = 4
RB_WSLOTS = 4


def _rb_kernel(cnt_ref, x1_ref, mod_ref, offs_ref, wts_ref, wgu_hbm, wd_hbm, ln2g_ref, ln2b_ref,
               o_ref, buf_ref, stage_ref, wgu_buf, wd_buf, start_ref, sem):
    b = pl.program_id(0)
    s = pl.program_id(1)

    @pl.when(s == 0)
    def _starts():
        def body(e, run):
            start_ref[e] = run
            return run + cnt_ref[b, e]
        lax.fori_loop(0, N_EXPERTS, body, jnp.int32(0))
        buf_ref[(RB_NCHUNK - 1) * 8 * RB_PITCH:(RB_NCHUNK + RB_GROUP) * 8 * RB_PITCH, :] = jnp.zeros(
            ((RB_GROUP + 1) * 8 * RB_PITCH, LANES), F32)

    @pl.when(s < RB_NSUB)
    def _dispatch():
        sh2 = mod_ref[0, 3:4, :]
        sc2 = mod_ref[0, 4:5, :]
        h2 = x1_ref[...] * (1.0 + sc2) + sh2
        for j in range(8):
            stage_ref[RB_SPITCH * j:RB_SPITCH * j + RB_SUB, :] = h2[:, LANES * j:LANES * (j + 1)]

        for t in range(RB_SUB):
            slab = stage_ref[pl.ds(t, 8, stride=RB_SPITCH), :]
            for a in range(2):
                buf_ref[pl.ds(offs_ref[0, a, t], 8, stride=RB_PITCH), :] = slab

    @pl.when(s == RB_NSUB)
    def _experts():
        def copies(e, slot):
            return (pltpu.make_async_copy(wgu_hbm.at[e], wgu_buf.at[slot], sem.at[slot, 0]),
                    pltpu.make_async_copy(wd_hbm.at[e], wd_buf.at[slot], sem.at[slot, 1]))

        def run_expert(e, slot):
            lo_row = start_ref[e]
            hi_row = lo_row + cnt_ref[b, e]

            c_lo = lax.div(lo_row, RB_CHUNK)
            c_hi = lax.div(hi_row + (RB_CHUNK - 1), RB_CHUNK)
            row = lax.broadcasted_iota(jnp.int32, (RB_CHUNK, 1), 0)

            def load(c):
                base = pl.multiple_of(c * (8 * RB_PITCH), 8)
                return [buf_ref[pl.ds(base + RB_PITCH * j, RB_CHUNK), :] for j in range(8)]

            def store(c, tiles, y):
                base = pl.multiple_of(c * (8 * RB_PITCH), 8)
                mine = (row >= lo_row - c * RB_CHUNK) & (row < hi_row - c * RB_CHUNK)
                for j in range(8):
                    buf_ref[pl.ds(base + RB_PITCH * j, RB_CHUNK), :] = jnp.where(
                        mine, y[:, LANES * j:LANES * (j + 1)], tiles[j])

            def group(first, nchunks):
                cs = [first]
                for k in range(1, nchunks):
                    cs.append(jnp.where(first + k < c_hi, first + k, RB_NCHUNK + k))
                tiles = [load(c) for c in cs]
                x = jnp.concatenate([jnp.concatenate(t, axis=-1) for t in tiles], axis=0).astype(BF16)
                au = jnp.dot(x, wgu_buf[slot], preferred_element_type=F32)
                z = (_silu(au[:, :D_EXPERT]) * au[:, D_EXPERT:]).astype(BF16)
                y = jnp.dot(z, wd_buf[slot], preferred_element_type=F32)
                for k, c in enumerate(cs):
                    store(c, tiles[k], y[RB_CHUNK * k:RB_CHUNK * (k + 1)])

            span = c_hi - c_lo

            @pl.when(span < RB_GROUP)
            def _():
                group(c_lo, RB_GROUP - 1)

            @pl.when(span >= RB_GROUP)
            def _():
                def body(i, carry):
                    group(c_lo + RB_GROUP * i, RB_GROUP)
                    return carry
                lax.fori_loop(0, lax.div(span + (RB_GROUP - 1), RB_GROUP), body, 0)

        for e in range(RB_WSLOTS - 1):
            for c in copies(e, e):
                c.start()

        def ring_body(i, carry):
            for k in range(RB_WSLOTS):
                e = RB_WSLOTS * i + k
                ahead = e + RB_WSLOTS - 1

                @pl.when(ahead < N_EXPERTS)
                def _():
                    for c in copies(ahead, (k + RB_WSLOTS - 1) % RB_WSLOTS):
                        c.start()
                for c in copies(e, k):
                    c.wait()
                run_expert(e, k)
            return carry
        lax.fori_loop(0, N_EXPERTS // RB_WSLOTS, ring_body, 0)

    @pl.when(s > RB_NSUB)
    def _combine():
        for t in range(RB_SUB):
            acc = None
            for a in range(2):
                term = wts_ref[0, a, t] * buf_ref[pl.ds(offs_ref[0, a, t], 8, stride=RB_PITCH), :]
                acc = term if acc is None else acc + term
            stage_ref[pl.ds(t, 8, stride=RB_SPITCH), :] = acc
        gt2 = mod_ref[0, 5:6, :]
        f = jnp.concatenate([stage_ref[RB_SPITCH * j:RB_SPITCH * j + RB_SUB, :] for j in range(8)], axis=-1)
        o_ref[...] = _layer_norm(DEEPNORM_ALPHA * x1_ref[...] + (1.0 + gt2) * f, ln2g_ref[...], ln2b_ref[...])


def _rb_retile(a):
    tiles, two, t = a.shape
    return a.reshape(tiles, two, t // RB_SUB, RB_SUB).transpose(0, 2, 1, 3).reshape(-1, two, RB_SUB)


def _rb_offsets(cnt, e12, rank12):
    start = jnp.cumsum(cnt, axis=-1) - cnt
    start_t = jnp.repeat(start, e12.shape[0] // cnt.shape[0], axis=0)[:, None, None, :]
    hit = e12[..., None] == jnp.arange(N_EXPERTS, dtype=jnp.int32)
    p = jnp.sum(jnp.where(hit, start_t, 0), axis=-1) + rank12
    return (p // RB_CHUNK) * (8 * RB_PITCH) + p % RB_CHUNK


def _rb_moe(x1, modp, cnt, offs, wts, wgu_bf16, wd_bf16, ln2g, ln2b):
    n, d = x1.shape
    bsz = n // SEQ
    nsteps = 2 * RB_NSUB + 1

    def sub_index(s):
        return jnp.where(s < RB_NSUB, s, jnp.where(s == RB_NSUB, RB_NSUB - 1, s - RB_NSUB - 1))

    def tile_map(b, s, cnt_r):
        return (b * RB_NSUB + sub_index(s), 0)

    def tile_map3(b, s, cnt_r):
        return (b * RB_NSUB + sub_index(s), 0, 0)

    def out_map(b, s, cnt_r):
        return (b * RB_NSUB + jnp.maximum(s - RB_NSUB - 1, 0), 0)

    const = lambda shp: pl.BlockSpec(shp, lambda b, s, cnt_r: tuple(0 for _ in shp))
    anyspec = pl.BlockSpec(memory_space=pl.ANY)
    grid_spec = pltpu.PrefetchScalarGridSpec(
        num_scalar_prefetch=1,
        grid=(bsz, nsteps),
        in_specs=[pl.BlockSpec((RB_SUB, d), tile_map),
                  pl.BlockSpec((1, 6, d), lambda b, s, cnt_r: (b, 0, 0)),
                  pl.BlockSpec((1, 2, RB_SUB), tile_map3, memory_space=pltpu.SMEM),
                  pl.BlockSpec((1, 2, RB_SUB), tile_map3, memory_space=pltpu.SMEM),
                  anyspec, anyspec,
                  const((1, d)), const((1, d))],
        out_specs=pl.BlockSpec((RB_SUB, d), out_map),
        scratch_shapes=[pltpu.VMEM(((RB_NCHUNK + RB_GROUP) * 8 * RB_PITCH, LANES), F32),
                        pltpu.VMEM((8 * RB_SPITCH, LANES), F32),
                        pltpu.VMEM((RB_WSLOTS, d, 2 * D_EXPERT), BF16),
                        pltpu.VMEM((RB_WSLOTS, D_EXPERT, d), BF16),
                        pltpu.SMEM((N_EXPERTS,), jnp.int32),
                        pltpu.SemaphoreType.DMA((RB_WSLOTS, 2))])
    return pl.pallas_call(
        _rb_kernel,
        grid_spec=grid_spec,
        out_shape=jax.ShapeDtypeStruct((n, d), F32),
        compiler_params=_cparams(("arbitrary", "arbitrary")),
        name="moe_routed_ln",
    )(cnt, x1, modp, offs, wts, wgu_bf16, wd_bf16, ln2g, ln2b)


def _sample_in_kernel(x_ref, mod_ref, win_ref, ctx_ref, h0_ref, convw_ref, convb_ref,
                      wlo_ref, whi_ref, bgate_ref, lam_ref,
                      ylru_ref, q_ref, k_ref, v_ref, cstate_ref, hnew_ref):
    sh1 = mod_ref[0:DEC_BATCH, 0:D_MODEL]
    sc1 = mod_ref[0:DEC_BATCH, D_MODEL:2 * D_MODEL]
    h = x_ref[...] * (1.0 + sc1) + sh1
    z = _dot(h, win_ref[...], True)
    xb = z[:, :LRU_WIDTH]
    gate = z[:, LRU_WIDTH:2 * LRU_WIDTH]
    c0 = ctx_ref[:, 0, :]
    c1 = ctx_ref[:, 1, :]
    c2 = ctx_ref[:, 2, :]
    xc = (convb_ref[...] + convw_ref[0:1, :] * c0 + convw_ref[1:2, :] * c1
          + convw_ref[2:3, :] * c2 + convw_ref[3:4, :] * xb)
    cstate_ref[:, 0, :] = c1
    cstate_ref[:, 1, :] = c2
    cstate_ref[:, 2, :] = xb
    sp = _softplus(-lam_ref[...])
    a, bterm = _lru_gates(xc, wlo_ref[...], whi_ref[...], bgate_ref[...], sp, True)
    hn = a * h0_ref[...] + bterm
    hnew_ref[...] = hn
    ylru_ref[...] = hn * _gelu_tanh(gate)
    low = lax.broadcasted_iota(jnp.int32, (DEC_BATCH, LANES), 1) < HEAD_DIM
    for c in range(4):
        qc = z[:, 2 * LRU_WIDTH + LANES * c:2 * LRU_WIDTH + LANES * (c + 1)]
        q_ref[pl.ds(c, DEC_BATCH, stride=N_HEADS), :] = jnp.where(low, qc, 0.0)
        q_ref[pl.ds(c + 4, DEC_BATCH, stride=N_HEADS), :] = jnp.where(low, 0.0, qc)
    k_ref[...] = z[:, 2 * LRU_WIDTH + ATTN_WIDTH:2 * LRU_WIDTH + ATTN_WIDTH + KV_WIDTH]
    v_ref[...] = z[:, 2 * LRU_WIDTH + ATTN_WIDTH + KV_WIDTH:]


def _sample_in(x, mod, w_in_p, ctx, h0, conv_w, conv_b, wlo, whi, bgate, lam):
    n = DEC_BATCH
    outs = [jax.ShapeDtypeStruct((n, LRU_WIDTH), F32),
            jax.ShapeDtypeStruct((n * N_HEADS, LANES), F32),
            jax.ShapeDtypeStruct((n, KV_WIDTH), F32),
            jax.ShapeDtypeStruct((n, KV_WIDTH), F32),
            jax.ShapeDtypeStruct((n, CONV_WIDTH - 1, LRU_WIDTH), F32),
            jax.ShapeDtypeStruct((n, LRU_WIDTH), F32)]
    return pl.pallas_call(
        _sample_in_kernel,
        out_shape=outs,
        compiler_params=pltpu.CompilerParams(vmem_limit_bytes=VMEM_LIMIT),
        name="sample_inproj_rglru",
    )(x, mod, w_in_p, ctx, h0, conv_w, conv_b, wlo, whi, bgate, lam)


def _sample_attn_kernel(q_ref, kn_ref, vn_ref, ck_ref, cv_ref, sink_ref, y_ref, nk_ref, nv_ref, *, bb):
    rows = lax.broadcasted_iota(jnp.int32, (WINDOW, KV_WIDTH), 0)
    nh = N_HEADS
    q_all = q_ref[...].reshape(bb * nh, LANES)
    kcat = ck_ref[...].reshape(bb * WINDOW, KV_WIDTH)
    vcat = cv_ref[...].reshape(bb * WINDOW, KV_WIDTH)
    kn_rep = jnp.broadcast_to(kn_ref[...][:, None, :], (bb, nh, KV_WIDTH)).reshape(bb * nh, KV_WIDTH)
    vn_rep = jnp.broadcast_to(vn_ref[...][:, None, :], (bb, nh, KV_WIDTH)).reshape(bb * nh, KV_WIDTH)
    sink = jnp.concatenate([sink_ref[...]] * bb, axis=0)
    s_full = _dot_nt(q_all, kcat, True)
    s = jnp.concatenate([s_full[nh * b:nh * (b + 1), WINDOW * b:WINDOW * (b + 1)] for b in range(bb)],
                        axis=0) * ATTN_SCALE
    s_self = jnp.sum(q_all * kn_rep, axis=-1, keepdims=True) * ATTN_SCALE
    m = jnp.maximum(jnp.maximum(jnp.max(s, axis=-1, keepdims=True), s_self), sink)
    e = jnp.exp(s - m)
    e_self = jnp.exp(s_self - m)
    den = jnp.sum(e, axis=-1, keepdims=True) + e_self + jnp.exp(sink - m)
    inv = 1.0 / den
    p = e * inv
    zero = jnp.zeros((nh, WINDOW), F32)
    p_wide = jnp.concatenate(
        [jnp.concatenate([p[nh * b:nh * (b + 1)] if c == b else zero for c in range(bb)], axis=-1)
         for b in range(bb)], axis=0)
    o = _dot(p_wide, vcat, True) + (e_self * inv) * vn_rep
    y_ref[...] = o.reshape(bb, nh, LANES)
    for b in range(bb):
        nk_ref[b] = jnp.where(rows == WINDOW - 1, kn_ref[b:b + 1, :], pltpu.roll(ck_ref[b], WINDOW - 1, axis=0))
        nv_ref[b] = jnp.where(rows == WINDOW - 1, vn_ref[b:b + 1, :], pltpu.roll(cv_ref[b], WINDOW - 1, axis=0))


def _sample_attn(q3, kn, vn, cache_k, cache_v, sinks, bb=16):
    n = DEC_BATCH
    kern = functools.partial(_sample_attn_kernel, bb=bb)
    return pl.pallas_call(
        kern,
        grid=(n // bb,),
        in_specs=[pl.BlockSpec((bb, N_HEADS, LANES), lambda i: (i, 0, 0)),
                  pl.BlockSpec((bb, KV_WIDTH), lambda i: (i, 0)),
                  pl.BlockSpec((bb, KV_WIDTH), lambda i: (i, 0)),
                  pl.BlockSpec((bb, WINDOW, KV_WIDTH), lambda i: (i, 0, 0)),
                  pl.BlockSpec((bb, WINDOW, KV_WIDTH), lambda i: (i, 0, 0)),
                  pl.BlockSpec((N_HEADS, 1), lambda i: (0, 0))],
        out_specs=[pl.BlockSpec((bb, N_HEADS, LANES), lambda i: (i, 0, 0)),
                   pl.BlockSpec((bb, WINDOW, KV_WIDTH), lambda i: (i, 0, 0)),
                   pl.BlockSpec((bb, WINDOW, KV_WIDTH), lambda i: (i, 0, 0))],
        out_shape=[jax.ShapeDtypeStruct((n, N_HEADS, LANES), F32),
                   jax.ShapeDtypeStruct((n, WINDOW, KV_WIDTH), F32),
                   jax.ShapeDtypeStruct((n, WINDOW, KV_WIDTH), F32)],
        compiler_params=_cparams(("arbitrary",)),
        name="sample_cache_attention",
    )(q3, kn, vn, cache_k, cache_v, sinks.reshape(N_HEADS, 1))


def _sample_out_kernel(x_ref, ylru_ref, yatt_ref, mod_ref, glru_ref, gattn_ref, wout_ref,
                       ln1g_ref, ln1b_ref, wr_ref, br_ref, x1_ref, comb_ref):
    low = lax.broadcasted_iota(jnp.int32, (DEC_BATCH, LANES), 1) < HEAD_DIM
    yatt = jnp.concatenate(
        [jnp.where(low, yatt_ref[pl.ds(c, DEC_BATCH, stride=N_HEADS), :],
                   yatt_ref[pl.ds(c + 4, DEC_BATCH, stride=N_HEADS), :]) for c in range(4)], axis=-1)
    gt1 = mod_ref[0:DEC_BATCH, 2 * D_MODEL:3 * D_MODEL]
    sh2 = mod_ref[0:DEC_BATCH, 3 * D_MODEL:4 * D_MODEL]
    sc2 = mod_ref[0:DEC_BATCH, 4 * D_MODEL:5 * D_MODEL]
    x1, comb = _outproj_body(x_ref[...], ylru_ref[...], yatt, sh2, sc2, gt1,
                             glru_ref[...], gattn_ref[...], wout_ref[...], ln1g_ref[...], ln1b_ref[...],
                             wr_ref[...], br_ref[...], True)
    x1_ref[...] = x1
    comb_ref[...] = comb


def _sample_out(x, ylru, yatt2d, mod, glru, gattn, wout_p, ln1g, ln1b, wr, br):
    n = DEC_BATCH
    return pl.pallas_call(
        _sample_out_kernel,
        out_shape=[jax.ShapeDtypeStruct((n, D_MODEL), F32), jax.ShapeDtypeStruct((n, ROUTE_LANES), F32)],
        compiler_params=pltpu.CompilerParams(vmem_limit_bytes=VMEM_LIMIT),
        name="sample_outproj_ln_route",
    )(x, ylru, yatt2d, mod, glru, gattn, wout_p, ln1g, ln1b, wr, br)


def _block_diag_halves(w_a, w_x):
    def bd(w4):
        eye = jnp.eye(4, dtype=w4.dtype)
        return (w4[:, :, None, :] * eye[:, None, :, None]).reshape(256, 256)
    lo = jnp.concatenate([bd(w_a[:4]), bd(w_x[:4])], axis=1)
    hi = jnp.concatenate([bd(w_a[4:]), bd(w_x[4:])], axis=1)
    return lo, hi


def kernel(x_prompt, x_sample, c_prompt, c_sample, state_conv, state_h, cache_k, cache_v, w_ada, b_ada, w_in,
           conv_w, conv_b, w_rg_a, b_rg_a, w_rg_x, b_rg_x, lru_lambda, sinks, g_lru, g_attn, w_out, ln1_g, ln1_b,
           w_group, b_group, w_router, b_router, w_gate, w_up, w_down, ln2_g, ln2_b):
    d = D_MODEL
    perm = jnp.asarray(HEAD_PERM)
    w_in0 = w_in[0]
    q0 = 2 * LRU_WIDTH
    w_in_p = jnp.concatenate([w_in0[:, :q0], w_in0[:, q0:q0 + ATTN_WIDTH][:, perm], w_in0[:, q0 + ATTN_WIDTH:]],
                             axis=1)
    w_out0 = w_out[0]
    w_out_p = jnp.concatenate([w_out0[:LRU_WIDTH], w_out0[LRU_WIDTH:][perm]], axis=0)
    g_attn_p = g_attn[0][perm].reshape(1, -1)
    glru = g_lru[0].reshape(1, -1)
    wlo, whi = _block_diag_halves(w_rg_a[0], w_rg_x[0])
    bgate = jnp.concatenate([b_rg_a[0].reshape(-1), b_rg_x[0].reshape(-1)]).reshape(1, -1)
    lam = lru_lambda[0].reshape(1, -1)
    convw = conv_w[0]
    convb = conv_b[0].reshape(1, -1)
    ln1g, ln1b = ln1_g[0].reshape(1, -1), ln1_b[0].reshape(1, -1)
    ln2g, ln2b = ln2_g[0].reshape(1, -1), ln2_b[0].reshape(1, -1)
    wr = jnp.concatenate([jnp.transpose(w_router[0], (1, 0, 2)).reshape(d, N_EXPERTS), w_group[0],
                          jnp.zeros((d, ROUTE_LANES - N_EXPERTS - N_GROUPS), F32)], axis=1)
    br = jnp.concatenate([b_router[0].reshape(-1), b_group[0],
                          jnp.zeros((ROUTE_LANES - N_EXPERTS - N_GROUPS,), F32)]).reshape(1, -1)
    sink_p = sinks[0]

    c_all = jnp.concatenate([c_sample, c_prompt, jnp.zeros((8 - BATCH, d), F32)], axis=0)
    mod = _ada(c_all, w_ada[0], b_ada[0])
    modp = mod[DEC_BATCH:DEC_BATCH + BATCH].reshape(BATCH, 6, d)

    zlru, zqkv, kvlast = _inproj(x_prompt, modp, w_in_p.astype(BF16))
    ylru, cstate8, hlast8, wgu_b, wd_b = _lru(zlru, convw, convb, wlo.astype(BF16), whi.astype(BF16), bgate, lam,
                                                   w_gate[0], w_up[0], w_down[0])
    yatt = _attn(zqkv, sink_p)
    n_p = BATCH * SEQ
    x1_p, info, cntf = _outproj_prompt(x_prompt.reshape(n_p, d), ylru.reshape(n_p, LRU_WIDTH),
                                       yatt.reshape(n_p, ATTN_WIDTH), modp, glru, g_attn_p, w_out_p.astype(BF16),
                                       ln1g, ln1b, wr, br, tm=OUTPROJ_TILE)
    cnt = cntf[:, 0, :N_EXPERTS].astype(jnp.int32)
    offs = _rb_retile(_rb_offsets(cnt, info[:, 0:2].astype(jnp.int32), info[:, 4:6].astype(jnp.int32)))
    y_p = _rb_moe(x1_p, modp, cnt, offs, _rb_retile(info[:, 2:4]), wgu_b, wd_b, ln2g, ln2b)

    ylru_s, q2d, kn, vn, cstate_s, hnew_s = _sample_in(
        x_sample.reshape(DEC_BATCH, d), mod, w_in_p, state_conv[0], state_h[0],
        convw, convb, wlo, whi, bgate, lam)
    yatt3, newk, newv = _sample_attn(q2d.reshape(DEC_BATCH, N_HEADS, LANES), kn, vn,
                                     cache_k[0].reshape(DEC_BATCH, WINDOW, KV_WIDTH),
                                     cache_v[0].reshape(DEC_BATCH, WINDOW, KV_WIDTH), sink_p)
    x1_s, comb_s = _sample_out(x_sample.reshape(DEC_BATCH, d), ylru_s, yatt3.reshape(DEC_BATCH * N_HEADS, LANES),
                               mod, glru, g_attn_p, w_out_p, ln1g, ln1b, wr, br)
    y_s = _moe_dense(x1_s, comb_s, mod, wgu_b, wd_b, ln2g, ln2b, DEC_BATCH)

    return (y_p.reshape(BATCH, SEQ, d),
            y_s.reshape(DEC_BATCH, 1, d),
            cstate8[:, 5:8][None],
            hlast8[:, 7][None],
            kvlast[:, :, :KV_WIDTH].reshape(1, BATCH, WINDOW, N_KV_HEADS, HEAD_DIM),
            kvlast[:, :, KV_WIDTH:].reshape(1, BATCH, WINDOW, N_KV_HEADS, HEAD_DIM),
            cstate_s[None],
            hnew_s[None],
            newk.reshape(1, DEC_BATCH, WINDOW, N_KV_HEADS, HEAD_DIM),
            newv.reshape(1, DEC_BATCH, WINDOW, N_KV_HEADS, HEAD_DIM))
```

```python
import functools

import jax
import jax.numpy as jnp
import numpy as np
from jax import lax
from jax.experimental import pallas as pl
from jax.experimental.pallas import tpu as pltpu

F32 = jnp.float32
BF16 = jnp.bfloat16
HIGHEST = lax.Precision.HIGHEST

D_MODEL = 1024
BATCH = 4
SEQ = 4096
DEC_BATCH = 128
LRU_WIDTH = 512
LRU_BLOCKS = 8
LRU_BLOCK = 64
CONV_WIDTH = 4
LRU_C = 8.0
N_HEADS = 8
N_KV_HEADS = 2
HEAD_DIM = 64
ATTN_WIDTH = 512
KV_WIDTH = 128
WINDOW = 128
IN_WIDTH = 2 * LRU_WIDTH + ATTN_WIDTH + 2 * KV_WIDTH
N_GROUPS = 4
EXPERTS_PER_GROUP = 8
N_EXPERTS = 32
D_EXPERT = 256
DEEPNORM_ALPHA = 2.0 ** 0.25
LN_EPS = 1e-5
RMS_EPS = 1e-6
ATTN_SCALE = HEAD_DIM ** -0.5

LANES = 128
ROUTE_LANES = 128
ROUTE_INFO = 40
VMEM_LIMIT = 56 * 1024 * 1024

HEAD_PERM = np.concatenate(
    [np.concatenate([np.arange(64 * c, 64 * c + 64), np.arange(64 * (c + 4), 64 * (c + 4) + 64)])
     for c in range(4)])


def _cparams(sem):
    return pltpu.CompilerParams(dimension_semantics=sem, vmem_limit_bytes=VMEM_LIMIT)


def _dot(a, b, exact):
    if exact:
        return jnp.dot(a, b, precision=HIGHEST, preferred_element_type=F32)
    return jnp.dot(a.astype(BF16), b.astype(BF16), preferred_element_type=F32)


def _dot_nt(a, b, exact):
    dn = (((1,), (1,)), ((), ()))
    if exact:
        return lax.dot_general(a, b, dn, precision=HIGHEST, preferred_element_type=F32)
    return lax.dot_general(a.astype(BF16), b.astype(BF16), dn, preferred_element_type=F32)


def _sigmoid(x):
    return 1.0 / (1.0 + jnp.exp(-x))


def _silu(x):
    return x * _sigmoid(x)


def _gelu_tanh(x):
    return 0.5 * x * (1.0 + jnp.tanh(np.sqrt(2.0 / np.pi).astype(np.float32) * (x + 0.044715 * (x * x * x))))


def _softplus(x):
    return jnp.maximum(x, 0.0) + jnp.log1p(jnp.exp(-jnp.abs(x)))


def _layer_norm(x, g, b):
    mu = jnp.mean(x, axis=-1, keepdims=True)
    xc = x - mu
    var = jnp.mean(xc * xc, axis=-1, keepdims=True)
    return xc * lax.rsqrt(var + LN_EPS) * g + b


def _rms_norm(x, g):
    return x * lax.rsqrt(jnp.mean(x * x, axis=-1, keepdims=True) + RMS_EPS) * g


def _ada_kernel(c_ref, w_ref, b_ref, o_ref):
    o_ref[...] = _dot(_silu(c_ref[...]), w_ref[...], True) + b_ref[...]


def _ada(c_all, w_ada, b_ada):
    rows = c_all.shape[0]
    bn = 1024
    return pl.pallas_call(
        _ada_kernel,
        grid=(6 * D_MODEL // bn,),
        in_specs=[pl.BlockSpec((rows, D_MODEL), lambda j: (0, 0)),
                  pl.BlockSpec((D_MODEL, bn), lambda j: (0, j)),
                  pl.BlockSpec((1, bn), lambda j: (0, j))],
        out_specs=pl.BlockSpec((rows, bn), lambda j: (0, j)),
        out_shape=jax.ShapeDtypeStruct((rows, 6 * D_MODEL), F32),
        compiler_params=_cparams(("arbitrary",)),
        name="ada_modulation",
    )(c_all, w_ada, b_ada.reshape(1, -1))


QKV_WIDTH = ATTN_WIDTH + 2 * KV_WIDTH


def _inproj_kernel(x_ref, mod_ref, w_ref, lru_ref, qkv_ref, kt_ref, kvlast_ref):
    sh1 = mod_ref[0, 0:1, :]
    sc1 = mod_ref[0, 1:2, :]
    h = x_ref[0] * (1.0 + sc1) + sh1
    z = _dot(h, w_ref[...], False)
    lru_ref[0] = z[:, :2 * LRU_WIDTH]
    qkv_ref[0] = z[:, 2 * LRU_WIDTH:].astype(BF16)
    kq = 2 * LRU_WIDTH + ATTN_WIDTH
    kt_ref[0] = jnp.transpose(z[:, kq:kq + KV_WIDTH]).astype(BF16)
    kvlast_ref[0] = z[z.shape[0] - WINDOW:, kq:]


def _inproj(x, modp, w_in_bf16, tm=1024):
    b, t, d = x.shape
    return pl.pallas_call(
        _inproj_kernel,
        grid=(b, t // tm),
        in_specs=[pl.BlockSpec((1, tm, d), lambda i, j: (i, j, 0)),
                  pl.BlockSpec((1, 6, d), lambda i, j: (i, 0, 0)),
                  pl.BlockSpec((d, IN_WIDTH), lambda i, j: (0, 0))],
        out_specs=[pl.BlockSpec((1, tm, 2 * LRU_WIDTH), lambda i, j: (i, j, 0)),
                   pl.BlockSpec((1, tm, QKV_WIDTH), lambda i, j: (i, j, 0)),
                   pl.BlockSpec((1, KV_WIDTH, tm), lambda i, j: (i, 0, j)),
                   pl.BlockSpec((1, WINDOW, 2 * KV_WIDTH), lambda i, j: (i, 0, 0))],
        out_shape=[jax.ShapeDtypeStruct((b, t, 2 * LRU_WIDTH), F32),
                   jax.ShapeDtypeStruct((b, t, QKV_WIDTH), BF16),
                   jax.ShapeDtypeStruct((b, KV_WIDTH, t), BF16),
                   jax.ShapeDtypeStruct((b, WINDOW, 2 * KV_WIDTH), F32)],
        compiler_params=_cparams(("arbitrary", "arbitrary")),
        name="prompt_inproj",
    )(x, modp, w_in_bf16)


def _lru_gates(xc, wlo, whi, bgate, sp_neg_lam, exact):
    g_lo = _dot(xc[:, :256], wlo, exact)
    g_hi = _dot(xc[:, 256:], whi, exact)
    ga = jnp.concatenate([g_lo[:, :256], g_hi[:, :256]], axis=-1) + bgate[:, :LRU_WIDTH]
    gx = jnp.concatenate([g_lo[:, 256:], g_hi[:, 256:]], axis=-1) + bgate[:, LRU_WIDTH:]
    r = _sigmoid(ga)
    i = _sigmoid(gx)
    log_a = -LRU_C * r * sp_neg_lam
    a = jnp.exp(log_a)
    one_minus_a2 = -jnp.tanh(log_a) * (a * a + 1.0) if exact else 1.0 - a * a
    root = jnp.where(one_minus_a2 > 0.0, one_minus_a2 * lax.rsqrt(one_minus_a2), 0.0)
    bterm = root * (i * xc)
    return a, bterm


def _lru_kernel(z_ref, convw_ref, convb_ref, wlo_ref, whi_ref, bgate_ref, lam_ref, wg_ref, wu_ref, wd_ref,
                y_ref, cstate_ref, hlast_ref, wgub_ref, wdb_ref, tail_ref, carry_ref, *, tl):
    j = pl.program_id(1)

    @pl.when(j == 0)
    def _():
        tail_ref[...] = jnp.zeros_like(tail_ref)
        carry_ref[...] = jnp.zeros_like(carry_ref)

    wgub_ref[0, :, :D_EXPERT] = wg_ref[0].astype(BF16)
    wgub_ref[0, :, D_EXPERT:] = wu_ref[0].astype(BF16)
    wdb_ref[...] = wd_ref[...].astype(BF16)

    xb = z_ref[0, :, :LRU_WIDTH]
    gate = z_ref[0, :, LRU_WIDTH:]
    xc = convb_ref[...] + convw_ref[3:4, :] * xb
    rows8 = lax.broadcasted_iota(jnp.int32, (8, LRU_WIDTH), 0)
    tail = tail_ref[...]
    for back in (1, 2, 3):
        rolled = pltpu.roll(xb, back, axis=0)
        top = jnp.where(rows8 >= back, rolled[:8], pltpu.roll(tail, back, axis=0))
        shifted = jnp.concatenate([top, rolled[8:]], axis=0)
        xc = xc + convw_ref[3 - back:4 - back, :] * shifted
    tail_ref[...] = xb[tl - 8:, :]
    cstate_ref[0] = xb[tl - 8:, :]

    sp = _softplus(-lam_ref[...])
    a, bterm = _lru_gates(xc, wlo_ref[...], whi_ref[...], bgate_ref[...], sp, False)

    groups = tl // 8
    a = a.reshape(groups, 8, LRU_WIDTH)
    bterm = bterm.reshape(groups, 8, LRU_WIDTH)
    r8 = lax.broadcasted_iota(jnp.int32, (groups, 8, LRU_WIDTH), 1)
    s = 1
    while s < 8:
        a_sh = jnp.where(r8 >= s, pltpu.roll(a, s, axis=1), 1.0)
        b_sh = jnp.where(r8 >= s, pltpu.roll(bterm, s, axis=1), 0.0)
        bterm = a * b_sh + bterm
        a = a * a_sh
        s *= 2
    a_tot = jnp.broadcast_to(a[:, 7:8, :], (groups, 8, LRU_WIDTH))
    b_tot = jnp.broadcast_to(bterm[:, 7:8, :], (groups, 8, LRU_WIDTH))
    h_in = jnp.broadcast_to(carry_ref[7:8, :], (8, LRU_WIDTH))
    pieces = []
    for g in range(groups):
        pieces.append(a[g] * h_in + bterm[g])
        h_in = a_tot[g] * h_in + b_tot[g]
    h = jnp.concatenate(pieces, axis=0)
    carry_ref[...] = h_in
    hlast_ref[0] = h_in
    y_ref[0] = h * _gelu_tanh(gate)


def _lru(zin, conv_w, conv_b, wlo, whi, bgate, lam, w_gate, w_up, w_down, tl=512):
    b, t, _ = zin.shape
    steps = t // tl
    assert b * steps == N_EXPERTS
    d = D_MODEL
    kern = functools.partial(_lru_kernel, tl=tl)
    full = lambda shp: pl.BlockSpec(shp, lambda i, j: tuple(0 for _ in shp))
    per_step = lambda shp: pl.BlockSpec(shp, lambda i, j: (i * steps + j, 0, 0))
    return pl.pallas_call(
        kern,
        grid=(b, steps),
        in_specs=[pl.BlockSpec((1, tl, 2 * LRU_WIDTH), lambda i, j: (i, j, 0)),
                  full((CONV_WIDTH, LRU_WIDTH)), full((1, LRU_WIDTH)),
                  full((256, 512)), full((256, 512)), full((1, 2 * LRU_WIDTH)), full((1, LRU_WIDTH)),
                  per_step((1, d, D_EXPERT)), per_step((1, d, D_EXPERT)), per_step((1, D_EXPERT, d))],
        out_specs=[pl.BlockSpec((1, tl, LRU_WIDTH), lambda i, j: (i, j, 0)),
                   pl.BlockSpec((1, 8, LRU_WIDTH), lambda i, j: (i, 0, 0)),
                   pl.BlockSpec((1, 8, LRU_WIDTH), lambda i, j: (i, 0, 0)),
                   per_step((1, d, 2 * D_EXPERT)), per_step((1, D_EXPERT, d))],
        out_shape=[jax.ShapeDtypeStruct((b, t, LRU_WIDTH), F32),
                   jax.ShapeDtypeStruct((b, 8, LRU_WIDTH), F32),
                   jax.ShapeDtypeStruct((b, 8, LRU_WIDTH), F32),
                   jax.ShapeDtypeStruct((N_EXPERTS, d, 2 * D_EXPERT), BF16),
                   jax.ShapeDtypeStruct(w_down.shape, BF16)],
        scratch_shapes=[pltpu.VMEM((8, LRU_WIDTH), F32), pltpu.VMEM((8, LRU_WIDTH), F32)],
        compiler_params=_cparams(("arbitrary", "arbitrary")),
        name="prompt_rglru",
    )(zin, conv_w, conv_b, wlo, whi, bgate, lam, w_gate, w_up, w_down)


ATTN_BLOCKS = 16


def _attn_kernel(q_ref, kt_ref, v_ref, sink_ref, o_ref, kprev_ref, vprev_ref):
    j = pl.program_id(1)

    @pl.when(j == 0)
    def _():
        kprev_ref[...] = jnp.zeros_like(kprev_ref)
        vprev_ref[...] = jnp.zeros_like(vprev_ref)

    blk = WINDOW
    lane = lax.broadcasted_iota(jnp.int32, (blk, LANES), 1)
    low = lane < HEAD_DIM
    qi = lax.broadcasted_iota(jnp.int32, (blk, 2 * blk), 0)
    sj = lax.broadcasted_iota(jnp.int32, (blk, 2 * blk), 1)
    rel = blk + qi - sj
    in_window = (rel >= 0) & (rel <= WINDOW)
    sink = sink_ref[...].reshape(N_HEADS, blk, 1)
    kt_ext = jnp.concatenate([kprev_ref[...], kt_ref[0]], axis=1)
    v_ext = jnp.concatenate([vprev_ref[...], v_ref[0]], axis=0)
    v_ext = jnp.concatenate([v_ext, jnp.ones_like(v_ext)], axis=-1)
    for n in range(ATTN_BLOCKS):
        q = q_ref[0, blk * n:blk * (n + 1), :]
        pieces = []
        for half in (0, 1):
            for c in range(4):
                qc = q[:, LANES * c:LANES * (c + 1)]
                pieces.append(jnp.where(low if half == 0 else ~low, qc, 0.0).astype(BF16))
        q8 = jnp.concatenate(pieces, axis=0)
        kt_band = kt_ext[:, blk * n:blk * (n + 2)]
        v_band = v_ext[blk * n:blk * (n + 2)]
        s = jnp.dot(q8, kt_band, preferred_element_type=F32) * ATTN_SCALE
        s = s.reshape(N_HEADS, blk, 2 * blk)
        valid = in_window & ((sj >= blk) | (j > 0)) if n == 0 else in_window
        s = jnp.where(valid[None], s, -jnp.inf)
        m = jnp.maximum(jnp.max(s, axis=-1, keepdims=True), sink)
        e = jnp.exp(s - m).reshape(N_HEADS * blk, 2 * blk)
        ov = _dot(e, v_band, False)
        den = ov[:, KV_WIDTH:] + jnp.exp(sink - m).reshape(N_HEADS * blk, 1)
        o8 = ov[:, :KV_WIDTH] * (1.0 / den)
        cols = []
        for c in range(4):
            cols.append(jnp.where(low, o8[blk * c:blk * (c + 1)], o8[blk * (c + 4):blk * (c + 5)]))
        o_ref[0, blk * n:blk * (n + 1), :] = jnp.concatenate(cols, axis=-1)
    kprev_ref[...] = kt_ref[0, :, blk * (ATTN_BLOCKS - 1):]
    vprev_ref[...] = v_ref[0, blk * (ATTN_BLOCKS - 1):, :]


def _attn(qkv, kt, sinks):
    b, t, _ = qkv.shape
    blk = WINDOW
    tq = blk * ATTN_BLOCKS
    sink_col = jnp.repeat(sinks.astype(F32), blk).reshape(N_HEADS * blk, 1)
    kcol = ATTN_WIDTH // KV_WIDTH
    return pl.pallas_call(
        _attn_kernel,
        grid=(b, t // tq),
        in_specs=[pl.BlockSpec((1, tq, ATTN_WIDTH), lambda i, j: (i, j, 0)),
                  pl.BlockSpec((1, KV_WIDTH, tq), lambda i, j: (i, 0, j)),
                  pl.BlockSpec((1, tq, KV_WIDTH), lambda i, j: (i, j, kcol + 1)),
                  pl.BlockSpec((N_HEADS * blk, 1), lambda i, j: (0, 0))],
        out_specs=pl.BlockSpec((1, tq, ATTN_WIDTH), lambda i, j: (i, j, 0)),
        out_shape=jax.ShapeDtypeStruct((b, t, ATTN_WIDTH), F32),
        scratch_shapes=[pltpu.VMEM((blk, KV_WIDTH), BF16), pltpu.VMEM((blk, KV_WIDTH), BF16)],
        compiler_params=_cparams(("arbitrary", "arbitrary")),
        name="prompt_window_attention",
    )(qkv, kt, qkv, sink_col)


def _route(h2, wr, br, exact):
    t = h2.shape[0]
    logits = _dot(h2, wr, exact) + br
    lane = lax.broadcasted_iota(jnp.int32, (t, ROUTE_LANES), 1).astype(F32)
    neg = -jnp.inf
    big = float(ROUTE_LANES)
    is_g = (lane >= N_EXPERTS) & (lane < N_EXPERTS + N_GROUPS)
    lg = jnp.where(is_g, logits, neg)
    mg = jnp.max(lg, axis=-1, keepdims=True)
    g_val = 1.0 / jnp.sum(jnp.where(is_g, jnp.exp(lg - mg), 0.0), axis=-1, keepdims=True)
    g_lane = jnp.min(jnp.where((lg == mg) & is_g, lane, big), axis=-1, keepdims=True)
    g_idx = g_lane - N_EXPERTS
    in_grp = (lane >= g_idx * EXPERTS_PER_GROUP) & (lane < (g_idx + 1.0) * EXPERTS_PER_GROUP)
    le = jnp.where(in_grp, logits, neg)
    me = jnp.max(le, axis=-1, keepdims=True)
    se = jnp.sum(jnp.where(in_grp, jnp.exp(le - me), 0.0), axis=-1, keepdims=True)
    l1 = jnp.min(jnp.where((le == me) & in_grp, lane, big), axis=-1, keepdims=True)
    rest = in_grp & (lane != l1)
    le2 = jnp.where(rest, le, neg)
    me2 = jnp.max(le2, axis=-1, keepdims=True)
    l2 = jnp.min(jnp.where((le2 == me2) & rest, lane, big), axis=-1, keepdims=True)
    v1 = 1.0 / se
    v2 = jnp.exp(me2 - me) / se
    tot = v1 + v2
    w1 = g_val * v1 / tot
    w2 = g_val * v2 / tot
    comb = jnp.where(lane == l1, w1, 0.0) + jnp.where(lane == l2, w2, 0.0)
    return (comb + jnp.where(lane == ROUTE_INFO, l1, 0.0) + jnp.where(lane == ROUTE_INFO + 1, l2, 0.0)
            + jnp.where(lane == ROUTE_INFO + 2, w1, 0.0) + jnp.where(lane == ROUTE_INFO + 3, w2, 0.0))


def _outproj_body(x, ylru, yatt, sh2, sc2, gt1, glru, gattn, wout, ln1g, ln1b, wr, br, exact):
    mixin = jnp.concatenate([_rms_norm(ylru, glru), _rms_norm(yatt, gattn)], axis=-1)
    mix = _dot(mixin, wout, exact)
    x1 = _layer_norm(DEEPNORM_ALPHA * x + (1.0 + gt1) * mix, ln1g, ln1b)
    h2 = x1 * (1.0 + sc2) + sh2
    return x1, _route(h2, wr, br, exact)


def _outproj_prompt_kernel(x_ref, ylru_ref, yatt_ref, mod_ref, glru_ref, gattn_ref, wout_ref,
                           ln1g_ref, ln1b_ref, wr_ref, br_ref, x1_ref, info_ref, cnt_ref, tri_ref, carry_ref,
                           *, tm, per_seq):
    i = pl.program_id(0)

    @pl.when(i == 0)
    def _():
        r = lax.broadcasted_iota(jnp.int32, (tm, tm), 0)
        c = lax.broadcasted_iota(jnp.int32, (tm, tm), 1)
        tri_ref[...] = jnp.where(c < r, 1.0, 0.0).astype(BF16)

    @pl.when(i % per_seq == 0)
    def _():
        carry_ref[...] = jnp.zeros_like(carry_ref)

    gt1 = mod_ref[0, 2:3, :]
    sh2 = mod_ref[0, 3:4, :]
    sc2 = mod_ref[0, 4:5, :]
    combs = []
    nsplit = 2
    for h in range(nsplit):
        rows = slice(h * (tm // nsplit), (h + 1) * (tm // nsplit))
        x1_h, comb_h = _outproj_body(x_ref[rows, :], ylru_ref[rows, :], yatt_ref[rows, :], sh2, sc2, gt1,
                                     glru_ref[...], gattn_ref[...], wout_ref[...], ln1g_ref[...], ln1b_ref[...],
                                     wr_ref[...], br_ref[...], False)
        x1_ref[rows, :] = x1_h
        combs.append(comb_h)
    comb = jnp.concatenate(combs, axis=0)
    lane = lax.broadcasted_iota(jnp.int32, (tm, ROUTE_LANES), 1).astype(F32)
    l1 = jnp.sum(jnp.where(lane == ROUTE_INFO, comb, 0.0), axis=-1, keepdims=True)
    l2 = jnp.sum(jnp.where(lane == ROUTE_INFO + 1, comb, 0.0), axis=-1, keepdims=True)
    o1 = lane == l1
    o2 = lane == l2
    onehot = jnp.where(o1 | o2, 1.0, 0.0)
    before = jnp.dot(tri_ref[...], onehot.astype(BF16), preferred_element_type=F32) + carry_ref[0:1, :]
    rank1 = jnp.sum(jnp.where(o1, before, 0.0), axis=-1, keepdims=True)
    rank2 = jnp.sum(jnp.where(o2, before, 0.0), axis=-1, keepdims=True)
    total = carry_ref[0:1, :] + jnp.sum(onehot, axis=0, keepdims=True)
    carry_ref[...] = jnp.broadcast_to(total, carry_ref.shape)
    cnt_ref[0] = jnp.broadcast_to(total, (8, ROUTE_LANES))
    info = (comb + jnp.where(lane == ROUTE_INFO + 4, rank1, 0.0) + jnp.where(lane == ROUTE_INFO + 5, rank2, 0.0))
    info_ref[0] = jnp.transpose(info)[ROUTE_INFO:ROUTE_INFO + 8, :]


OUTPROJ_TILE = 1024


def _outproj_prompt(x2d, ylru2d, yatt2d, modp, glru, gattn, wout_bf16, ln1g, ln1b, wr, br, tm=OUTPROJ_TILE):
    n, d = x2d.shape
    per_seq = SEQ // tm
    full = lambda shp: pl.BlockSpec(shp, lambda i: tuple(0 for _ in shp))
    kern = functools.partial(_outproj_prompt_kernel, tm=tm, per_seq=per_seq)
    return pl.pallas_call(
        kern,
        grid=(n // tm,),
        in_specs=[pl.BlockSpec((tm, d), lambda i: (i, 0)),
                  pl.BlockSpec((tm, LRU_WIDTH), lambda i: (i, 0)),
                  pl.BlockSpec((tm, ATTN_WIDTH), lambda i: (i, 0)),
                  pl.BlockSpec((1, 6, d), lambda i: (i // per_seq, 0, 0)),
                  full((1, LRU_WIDTH)), full((1, ATTN_WIDTH)), full((d, d)),
                  full((1, d)), full((1, d)), full((d, ROUTE_LANES)), full((1, ROUTE_LANES))],
        out_specs=[pl.BlockSpec((tm, d), lambda i: (i, 0)),
                   pl.BlockSpec((1, 8, tm), lambda i: (i, 0, 0)),
                   pl.BlockSpec((1, 8, ROUTE_LANES), lambda i: (i // per_seq, 0, 0))],
        out_shape=[jax.ShapeDtypeStruct((n, d), F32),
                   jax.ShapeDtypeStruct((n // tm, 8, tm), F32),
                   jax.ShapeDtypeStruct((n // SEQ, 8, ROUTE_LANES), F32)],
        scratch_shapes=[pltpu.VMEM((tm, tm), BF16), pltpu.VMEM((8, ROUTE_LANES), F32)],
        compiler_params=_cparams(("arbitrary",)),
        name="prompt_outproj_ln_route",
    )(x2d, ylru2d, yatt2d, modp, glru, gattn, wout_bf16, ln1g, ln1b, wr, br)


DENSE_EXPERTS_PER_STEP = 4


def _moe_kernel(x1_ref, comb_ref, sh2_ref, sc2_ref, gt2_ref, wgu_ref, wd_ref, ln2g_ref, ln2b_ref,
                o_ref, h2_ref, acc_ref):
    g = pl.program_id(1)

    @pl.when(g == 0)
    def _():
        h2_ref[...] = (x1_ref[...] * (1.0 + sc2_ref[...]) + sh2_ref[...]).astype(BF16)
        acc_ref[...] = jnp.zeros_like(acc_ref)

    h2 = h2_ref[...]
    comb = comb_ref[...]
    lane = lax.broadcasted_iota(jnp.int32, comb.shape, 1)
    part = None
    for k in range(DENSE_EXPERTS_PER_STEP):
        au = jnp.dot(h2, wgu_ref[k], preferred_element_type=F32)
        c_e = jnp.sum(jnp.where(lane == g * DENSE_EXPERTS_PER_STEP + k, comb, 0.0), axis=-1, keepdims=True)
        z = _silu(au[:, :D_EXPERT]) * au[:, D_EXPERT:] * c_e
        y = jnp.dot(z.astype(BF16), wd_ref[k], preferred_element_type=F32)
        part = y if part is None else part + y
    acc_ref[...] += part

    @pl.when(g == N_EXPERTS // DENSE_EXPERTS_PER_STEP - 1)
    def _():
        o_ref[...] = _layer_norm(DEEPNORM_ALPHA * x1_ref[...] + (1.0 + gt2_ref[...]) * acc_ref[...],
                                 ln2g_ref[...], ln2b_ref[...])


def _moe_dense(x1, comb, mod, wgu_bf16, wd_bf16, ln2g, ln2b, tm):
    n, d = x1.shape
    eg = DENSE_EXPERTS_PER_STEP
    mspec = lambda k: pl.BlockSpec((tm, d), lambda i, e: (i, k))
    full = lambda shp: pl.BlockSpec(shp, lambda i, e: tuple(0 for _ in shp))
    return pl.pallas_call(
        _moe_kernel,
        grid=(n // tm, N_EXPERTS // eg),
        in_specs=[pl.BlockSpec((tm, d), lambda i, e: (i, 0)),
                  pl.BlockSpec((tm, ROUTE_LANES), lambda i, e: (i, 0)),
                  mspec(3), mspec(4), mspec(5),
                  pl.BlockSpec((eg, d, 2 * D_EXPERT), lambda i, e: (e, 0, 0)),
                  pl.BlockSpec((eg, D_EXPERT, d), lambda i, e: (e, 0, 0)),
                  full((1, d)), full((1, d))],
        out_specs=pl.BlockSpec((tm, d), lambda i, e: (i, 0)),
        out_shape=jax.ShapeDtypeStruct((n, d), F32),
        scratch_shapes=[pltpu.VMEM((tm, d), BF16), pltpu.VMEM((tm, d), F32)],
        compiler_params=_cparams(("arbitrary", "arbitrary")),
        name="moe_dense_ln",
    )(x1, comb, mod, mod, mod, wgu_bf16, wd_bf16, ln2g, ln2b)


RB_SUB = 512
RB_NSUB = SEQ // RB_SUB
RB_CHUNK = 96
RB_NCHUNK = -(-2 * SEQ // RB_CHUNK)
RB_PITCH = RB_CHUNK + 8
RB_SPITCH = RB_SUB + 8
RB_GROUP = 4
RB_WSLOTS = 4


def _rb_kernel(cnt_ref, x1_ref, mod_ref, offs_ref, wts_ref, wgu_hbm, wd_hbm, ln2g_ref, ln2b_ref,
               o_ref, buf_ref, stage_ref, wgu_buf, wd_buf, start_ref, sem):
    b = pl.program_id(0)
    s = pl.program_id(1)

    @pl.when(s == 0)
    def _starts():
        def body(e, run):
            start_ref[e] = run
            return run + cnt_ref[b, e]
        lax.fori_loop(0, N_EXPERTS, body, jnp.int32(0))
        buf_ref[(RB_NCHUNK - 1) * 8 * RB_PITCH:(RB_NCHUNK + RB_GROUP) * 8 * RB_PITCH, :] = jnp.zeros(
            ((RB_GROUP + 1) * 8 * RB_PITCH, LANES), F32)

    @pl.when(s < RB_NSUB)
    def _dispatch():
        sh2 = mod_ref[0, 3:4, :]
        sc2 = mod_ref[0, 4:5, :]
        h2 = x1_ref[...] * (1.0 + sc2) + sh2
        for j in range(8):
            stage_ref[RB_SPITCH * j:RB_SPITCH * j + RB_SUB, :] = h2[:, LANES * j:LANES * (j + 1)]

        for t in range(RB_SUB):
            slab = stage_ref[pl.ds(t, 8, stride=RB_SPITCH), :]
            for a in range(2):
                buf_ref[pl.ds(offs_ref[0, a, t], 8, stride=RB_PITCH), :] = slab

    @pl.when(s == RB_NSUB)
    def _experts():
        def copies(e, slot):
            return (pltpu.make_async_copy(wgu_hbm.at[e], wgu_buf.at[slot], sem.at[slot, 0]),
                    pltpu.make_async_copy(wd_hbm.at[e], wd_buf.at[slot], sem.at[slot, 1]))

        def run_expert(e, slot):
            lo_row = start_ref[e]
            hi_row = lo_row + cnt_ref[b, e]

            c_lo = lax.div(lo_row, RB_CHUNK)
            c_hi = lax.div(hi_row + (RB_CHUNK - 1), RB_CHUNK)
            row = lax.broadcasted_iota(jnp.int32, (RB_CHUNK, 1), 0)

            def load(c):
                base = pl.multiple_of(c * (8 * RB_PITCH), 8)
                return [buf_ref[pl.ds(base + RB_PITCH * j, RB_CHUNK), :] for j in range(8)]

            def store(c, tiles, y):
                base = pl.multiple_of(c * (8 * RB_PITCH), 8)
                mine = (row >= lo_row - c * RB_CHUNK) & (row < hi_row - c * RB_CHUNK)
                for j in range(8):
                    buf_ref[pl.ds(base + RB_PITCH * j, RB_CHUNK), :] = jnp.where(
                        mine, y[:, LANES * j:LANES * (j + 1)], tiles[j])

            def group(first, nchunks):
                cs = [first]
                for k in range(1, nchunks):
                    cs.append(jnp.where(first + k < c_hi, first + k, RB_NCHUNK + k))
                tiles = [load(c) for c in cs]
                x = jnp.concatenate([jnp.concatenate(t, axis=-1) for t in tiles], axis=0).astype(BF16)
                au = jnp.dot(x, wgu_buf[slot], preferred_element_type=F32)
                z = (_silu(au[:, :D_EXPERT]) * au[:, D_EXPERT:]).astype(BF16)
                y = jnp.dot(z, wd_buf[slot], preferred_element_type=F32)
                for k, c in enumerate(cs):
                    store(c, tiles[k], y[RB_CHUNK * k:RB_CHUNK * (k + 1)])

            span = c_hi - c_lo

            @pl.when(span < RB_GROUP)
            def _():
                group(c_lo, RB_GROUP - 1)

            @pl.when(span >= RB_GROUP)
            def _():
                def body(i, carry):
                    group(c_lo + RB_GROUP * i, RB_GROUP)
                    return carry
                lax.fori_loop(0, lax.div(span + (RB_GROUP - 1), RB_GROUP), body, 0)

        for e in range(RB_WSLOTS - 1):
            for c in copies(e, e):
                c.start()

        def ring_body(i, carry):
            for k in range(RB_WSLOTS):
                e = RB_WSLOTS * i + k
                ahead = e + RB_WSLOTS - 1

                @pl.when(ahead < N_EXPERTS)
                def _():
                    for c in copies(ahead, (k + RB_WSLOTS - 1) % RB_WSLOTS):
                        c.start()
                for c in copies(e, k):
                    c.wait()
                run_expert(e, k)
            return carry
        lax.fori_loop(0, N_EXPERTS // RB_WSLOTS, ring_body, 0)

    @pl.when(s > RB_NSUB)
    def _combine():
        for t in range(RB_SUB):
            acc = None
            for a in range(2):
                term = wts_ref[0, a, t] * buf_ref[pl.ds(offs_ref[0, a, t], 8, stride=RB_PITCH), :]
                acc = term if acc is None else acc + term
            stage_ref[pl.ds(t, 8, stride=RB_SPITCH), :] = acc
        gt2 = mod_ref[0, 5:6, :]
        f = jnp.concatenate([stage_ref[RB_SPITCH * j:RB_SPITCH * j + RB_SUB, :] for j in range(8)], axis=-1)
        o_ref[...] = _layer_norm(DEEPNORM_ALPHA * x1_ref[...] + (1.0 + gt2) * f, ln2g_ref[...], ln2b_ref[...])


def _rb_retile(a):
    tiles, two, t = a.shape
    return a.reshape(tiles, two, t // RB_SUB, RB_SUB).transpose(0, 2, 1, 3).reshape(-1, two, RB_SUB)


def _rb_offsets(cnt, e12, rank12):
    start = jnp.cumsum(cnt, axis=-1) - cnt
    start_t = jnp.repeat(start, e12.shape[0] // cnt.shape[0], axis=0)[:, None, None, :]
    hit = e12[..., None] == jnp.arange(N_EXPERTS, dtype=jnp.int32)
    p = jnp.sum(jnp.where(hit, start_t, 0), axis=-1) + rank12
    return (p // RB_CHUNK) * (8 * RB_PITCH) + p % RB_CHUNK


def _rb_moe(x1, modp, cnt, offs, wts, wgu_bf16, wd_bf16, ln2g, ln2b):
    n, d = x1.shape
    bsz = n // SEQ
    nsteps = 2 * RB_NSUB + 1

    def sub_index(s):
        return jnp.where(s < RB_NSUB, s, jnp.where(s == RB_NSUB, RB_NSUB - 1, s - RB_NSUB - 1))

    def tile_map(b, s, cnt_r):
        return (b * RB_NSUB + sub_index(s), 0)

    def tile_map3(b, s, cnt_r):
        return (b * RB_NSUB + sub_index(s), 0, 0)

    def out_map(b, s, cnt_r):
        return (b * RB_NSUB + jnp.maximum(s - RB_NSUB - 1, 0), 0)

    const = lambda shp: pl.BlockSpec(shp, lambda b, s, cnt_r: tuple(0 for _ in shp))
    anyspec = pl.BlockSpec(memory_space=pl.ANY)
    grid_spec = pltpu.PrefetchScalarGridSpec(
        num_scalar_prefetch=1,
        grid=(bsz, nsteps),
        in_specs=[pl.BlockSpec((RB_SUB, d), tile_map),
                  pl.BlockSpec((1, 6, d), lambda b, s, cnt_r: (b, 0, 0)),
                  pl.BlockSpec((1, 2, RB_SUB), tile_map3, memory_space=pltpu.SMEM),
                  pl.BlockSpec((1, 2, RB_SUB), tile_map3, memory_space=pltpu.SMEM),
                  anyspec, anyspec,
                  const((1, d)), const((1, d))],
        out_specs=pl.BlockSpec((RB_SUB, d), out_map),
        scratch_shapes=[pltpu.VMEM(((RB_NCHUNK + RB_GROUP) * 8 * RB_PITCH, LANES), F32),
                        pltpu.VMEM((8 * RB_SPITCH, LANES), F32),
                        pltpu.VMEM((RB_WSLOTS, d, 2 * D_EXPERT), BF16),
                        pltpu.VMEM((RB_WSLOTS, D_EXPERT, d), BF16),
                        pltpu.SMEM((N_EXPERTS,), jnp.int32),
                        pltpu.SemaphoreType.DMA((RB_WSLOTS, 2))])
    return pl.pallas_call(
        _rb_kernel,
        grid_spec=grid_spec,
        out_shape=jax.ShapeDtypeStruct((n, d), F32),
        compiler_params=_cparams(("arbitrary", "arbitrary")),
        name="moe_routed_ln",
    )(cnt, x1, modp, offs, wts, wgu_bf16, wd_bf16, ln2g, ln2b)


def _sample_in_kernel(x_ref, mod_ref, win_ref, ctx_ref, h0_ref, convw_ref, convb_ref,
                      wlo_ref, whi_ref, bgate_ref, lam_ref,
                      ylru_ref, q_ref, k_ref, v_ref, cstate_ref, hnew_ref):
    sh1 = mod_ref[0:DEC_BATCH, 0:D_MODEL]
    sc1 = mod_ref[0:DEC_BATCH, D_MODEL:2 * D_MODEL]
    h = x_ref[...] * (1.0 + sc1) + sh1
    z = _dot(h, win_ref[...], True)
    xb = z[:, :LRU_WIDTH]
    gate = z[:, LRU_WIDTH:2 * LRU_WIDTH]
    c0 = ctx_ref[:, 0, :]
    c1 = ctx_ref[:, 1, :]
    c2 = ctx_ref[:, 2, :]
    xc = (convb_ref[...] + convw_ref[0:1, :] * c0 + convw_ref[1:2, :] * c1
          + convw_ref[2:3, :] * c2 + convw_ref[3:4, :] * xb)
    cstate_ref[:, 0, :] = c1
    cstate_ref[:, 1, :] = c2
    cstate_ref[:, 2, :] = xb
    sp = _softplus(-lam_ref[...])
    a, bterm = _lru_gates(xc, wlo_ref[...], whi_ref[...], bgate_ref[...], sp, True)
    hn = a * h0_ref[...] + bterm
    hnew_ref[...] = hn
    ylru_ref[...] = hn * _gelu_tanh(gate)
    low = lax.broadcasted_iota(jnp.int32, (DEC_BATCH, LANES), 1) < HEAD_DIM
    for c in range(4):
        qc = z[:, 2 * LRU_WIDTH + LANES * c:2 * LRU_WIDTH + LANES * (c + 1)]
        q_ref[pl.ds(c, DEC_BATCH, stride=N_HEADS), :] = jnp.where(low, qc, 0.0)
        q_ref[pl.ds(c + 4, DEC_BATCH, stride=N_HEADS), :] = jnp.where(low, 0.0, qc)
    k_ref[...] = z[:, 2 * LRU_WIDTH + ATTN_WIDTH:2 * LRU_WIDTH + ATTN_WIDTH + KV_WIDTH]
    v_ref[...] = z[:, 2 * LRU_WIDTH + ATTN_WIDTH + KV_WIDTH:]


def _sample_in(x, mod, w_in_p, ctx, h0, conv_w, conv_b, wlo, whi, bgate, lam):
    n = DEC_BATCH
    outs = [jax.ShapeDtypeStruct((n, LRU_WIDTH), F32),
            jax.ShapeDtypeStruct((n * N_HEADS, LANES), F32),
            jax.ShapeDtypeStruct((n, KV_WIDTH), F32),
            jax.ShapeDtypeStruct((n, KV_WIDTH), F32),
            jax.ShapeDtypeStruct((n, CONV_WIDTH - 1, LRU_WIDTH), F32),
            jax.ShapeDtypeStruct((n, LRU_WIDTH), F32)]
    return pl.pallas_call(
        _sample_in_kernel,
        out_shape=outs,
        compiler_params=pltpu.CompilerParams(vmem_limit_bytes=VMEM_LIMIT),
        name="sample_inproj_rglru",
    )(x, mod, w_in_p, ctx, h0, conv_w, conv_b, wlo, whi, bgate, lam)


def _sample_attn_kernel(q_ref, kn_ref, vn_ref, ck_ref, cv_ref, sink_ref, y_ref, nk_ref, nv_ref, *, bb):
    rows = lax.broadcasted_iota(jnp.int32, (WINDOW, KV_WIDTH), 0)
    nh = N_HEADS
    q_all = q_ref[...].reshape(bb * nh, LANES)
    kcat = ck_ref[...].reshape(bb * WINDOW, KV_WIDTH)
    vcat = cv_ref[...].reshape(bb * WINDOW, KV_WIDTH)
    kn_rep = jnp.broadcast_to(kn_ref[...][:, None, :], (bb, nh, KV_WIDTH)).reshape(bb * nh, KV_WIDTH)
    vn_rep = jnp.broadcast_to(vn_ref[...][:, None, :], (bb, nh, KV_WIDTH)).reshape(bb * nh, KV_WIDTH)
    sink = jnp.concatenate([sink_ref[...]] * bb, axis=0)
    s_full = _dot_nt(q_all, kcat, True)
    s = jnp.concatenate([s_full[nh * b:nh * (b + 1), WINDOW * b:WINDOW * (b + 1)] for b in range(bb)],
                        axis=0) * ATTN_SCALE
    s_self = jnp.sum(q_all * kn_rep, axis=-1, keepdims=True) * ATTN_SCALE
    m = jnp.maximum(jnp.maximum(jnp.max(s, axis=-1, keepdims=True), s_self), sink)
    e = jnp.exp(s - m)
    e_self = jnp.exp(s_self - m)
    den = jnp.sum(e, axis=-1, keepdims=True) + e_self + jnp.exp(sink - m)
    inv = 1.0 / den
    p = e * inv
    zero = jnp.zeros((nh, WINDOW), F32)
    p_wide = jnp.concatenate(
        [jnp.concatenate([p[nh * b:nh * (b + 1)] if c == b else zero for c in range(bb)], axis=-1)
         for b in range(bb)], axis=0)
    o = _dot(p_wide, vcat, True) + (e_self * inv) * vn_rep
    y_ref[...] = o.reshape(bb, nh, LANES)
    for b in range(bb):
        nk_ref[b] = jnp.where(rows == WINDOW - 1, kn_ref[b:b + 1, :], pltpu.roll(ck_ref[b], WINDOW - 1, axis=0))
        nv_ref[b] = jnp.where(rows == WINDOW - 1, vn_ref[b:b + 1, :], pltpu.roll(cv_ref[b], WINDOW - 1, axis=0))


def _sample_attn(q3, kn, vn, cache_k, cache_v, sinks, bb=16):
    n = DEC_BATCH
    kern = functools.partial(_sample_attn_kernel, bb=bb)
    return pl.pallas_call(
        kern,
        grid=(n // bb,),
        in_specs=[pl.BlockSpec((bb, N_HEADS, LANES), lambda i: (i, 0, 0)),
                  pl.BlockSpec((bb, KV_WIDTH), lambda i: (i, 0)),
                  pl.BlockSpec((bb, KV_WIDTH), lambda i: (i, 0)),
                  pl.BlockSpec((bb, WINDOW, KV_WIDTH), lambda i: (i, 0, 0)),
                  pl.BlockSpec((bb, WINDOW, KV_WIDTH), lambda i: (i, 0, 0)),
                  pl.BlockSpec((N_HEADS, 1), lambda i: (0, 0))],
        out_specs=[pl.BlockSpec((bb, N_HEADS, LANES), lambda i: (i, 0, 0)),
                   pl.BlockSpec((bb, WINDOW, KV_WIDTH), lambda i: (i, 0, 0)),
                   pl.BlockSpec((bb, WINDOW, KV_WIDTH), lambda i: (i, 0, 0))],
        out_shape=[jax.ShapeDtypeStruct((n, N_HEADS, LANES), F32),
                   jax.ShapeDtypeStruct((n, WINDOW, KV_WIDTH), F32),
                   jax.ShapeDtypeStruct((n, WINDOW, KV_WIDTH), F32)],
        compiler_params=_cparams(("arbitrary",)),
        name="sample_cache_attention",
    )(q3, kn, vn, cache_k, cache_v, sinks.reshape(N_HEADS, 1))


def _sample_out_kernel(x_ref, ylru_ref, yatt_ref, mod_ref, glru_ref, gattn_ref, wout_ref,
                       ln1g_ref, ln1b_ref, wr_ref, br_ref, x1_ref, comb_ref):
    low = lax.broadcasted_iota(jnp.int32, (DEC_BATCH, LANES), 1) < HEAD_DIM
    yatt = jnp.concatenate(
        [jnp.where(low, yatt_ref[pl.ds(c, DEC_BATCH, stride=N_HEADS), :],
                   yatt_ref[pl.ds(c + 4, DEC_BATCH, stride=N_HEADS), :]) for c in range(4)], axis=-1)
    gt1 = mod_ref[0:DEC_BATCH, 2 * D_MODEL:3 * D_MODEL]
    sh2 = mod_ref[0:DEC_BATCH, 3 * D_MODEL:4 * D_MODEL]
    sc2 = mod_ref[0:DEC_BATCH, 4 * D_MODEL:5 * D_MODEL]
    x1, comb = _outproj_body(x_ref[...], ylru_ref[...], yatt, sh2, sc2, gt1,
                             glru_ref[...], gattn_ref[...], wout_ref[...], ln1g_ref[...], ln1b_ref[...],
                             wr_ref[...], br_ref[...], True)
    x1_ref[...] = x1
    comb_ref[...] = comb


def _sample_out(x, ylru, yatt2d, mod, glru, gattn, wout_p, ln1g, ln1b, wr, br):
    n = DEC_BATCH
    return pl.pallas_call(
        _sample_out_kernel,
        out_shape=[jax.ShapeDtypeStruct((n, D_MODEL), F32), jax.ShapeDtypeStruct((n, ROUTE_LANES), F32)],
        compiler_params=pltpu.CompilerParams(vmem_limit_bytes=VMEM_LIMIT),
        name="sample_outproj_ln_route",
    )(x, ylru, yatt2d, mod, glru, gattn, wout_p, ln1g, ln1b, wr, br)


def _block_diag_halves(w_a, w_x):
    def bd(w4):
        eye = jnp.eye(4, dtype=w4.dtype)
        return (w4[:, :, None, :] * eye[:, None, :, None]).reshape(256, 256)
    lo = jnp.concatenate([bd(w_a[:4]), bd(w_x[:4])], axis=1)
    hi = jnp.concatenate([bd(w_a[4:]), bd(w_x[4:])], axis=1)
    return lo, hi


def kernel(x_prompt, x_sample, c_prompt, c_sample, state_conv, state_h, cache_k, cache_v, w_ada, b_ada, w_in,
           conv_w, conv_b, w_rg_a, b_rg_a, w_rg_x, b_rg_x, lru_lambda, sinks, g_lru, g_attn, w_out, ln1_g, ln1_b,
           w_group, b_group, w_router, b_router, w_gate, w_up, w_down, ln2_g, ln2_b):
    d = D_MODEL
    perm = jnp.asarray(HEAD_PERM)
    w_in0 = w_in[0]
    q0 = 2 * LRU_WIDTH
    w_in_p = jnp.concatenate([w_in0[:, :q0], w_in0[:, q0:q0 + ATTN_WIDTH][:, perm], w_in0[:, q0 + ATTN_WIDTH:]],
                             axis=1)
    w_out0 = w_out[0]
    w_out_p = jnp.concatenate([w_out0[:LRU_WIDTH], w_out0[LRU_WIDTH:][perm]], axis=0)
    g_attn_p = g_attn[0][perm].reshape(1, -1)
    glru = g_lru[0].reshape(1, -1)
    wlo, whi = _block_diag_halves(w_rg_a[0], w_rg_x[0])
    bgate = jnp.concatenate([b_rg_a[0].reshape(-1), b_rg_x[0].reshape(-1)]).reshape(1, -1)
    lam = lru_lambda[0].reshape(1, -1)
    convw = conv_w[0]
    convb = conv_b[0].reshape(1, -1)
    ln1g, ln1b = ln1_g[0].reshape(1, -1), ln1_b[0].reshape(1, -1)
    ln2g, ln2b = ln2_g[0].reshape(1, -1), ln2_b[0].reshape(1, -1)
    wr = jnp.concatenate([jnp.transpose(w_router[0], (1, 0, 2)).reshape(d, N_EXPERTS), w_group[0],
                          jnp.zeros((d, ROUTE_LANES - N_EXPERTS - N_GROUPS), F32)], axis=1)
    br = jnp.concatenate([b_router[0].reshape(-1), b_group[0],
                          jnp.zeros((ROUTE_LANES - N_EXPERTS - N_GROUPS,), F32)]).reshape(1, -1)
    sink_p = sinks[0]

    c_all = jnp.concatenate([c_sample, c_prompt, jnp.zeros((8 - BATCH, d), F32)], axis=0)
    mod = _ada(c_all, w_ada[0], b_ada[0])
    modp = mod[DEC_BATCH:DEC_BATCH + BATCH].reshape(BATCH, 6, d)

    zlru, zqkv, zkt, kvlast = _inproj(x_prompt, modp, w_in_p.astype(BF16))
    ylru, cstate8, hlast8, wgu_b, wd_b = _lru(zlru, convw, convb, wlo.astype(BF16), whi.astype(BF16), bgate, lam,
                                                   w_gate[0], w_up[0], w_down[0])
    yatt = _attn(zqkv, zkt, sink_p)
    n_p = BATCH * SEQ
    x1_p, info, cntf = _outproj_prompt(x_prompt.reshape(n_p, d), ylru.reshape(n_p, LRU_WIDTH),
                                       yatt.reshape(n_p, ATTN_WIDTH), modp, glru, g_attn_p, w_out_p.astype(BF16),
                                       ln1g, ln1b, wr, br, tm=OUTPROJ_TILE)
    cnt = cntf[:, 0, :N_EXPERTS].astype(jnp.int32)
    offs = _rb_retile(_rb_offsets(cnt, info[:, 0:2].astype(jnp.int32), info[:, 4:6].astype(jnp.int32)))
    y_p = _rb_moe(x1_p, modp, cnt, offs, _rb_retile(info[:, 2:4]), wgu_b, wd_b, ln2g, ln2b)

    ylru_s, q2d, kn, vn, cstate_s, hnew_s = _sample_in(
        x_sample.reshape(DEC_BATCH, d), mod, w_in_p, state_conv[0], state_h[0],
        convw, convb, wlo, whi, bgate, lam)
    yatt3, newk, newv = _sample_attn(q2d.reshape(DEC_BATCH, N_HEADS, LANES), kn, vn,
                                     cache_k[0].reshape(DEC_BATCH, WINDOW, KV_WIDTH),
                                     cache_v[0].reshape(DEC_BATCH, WINDOW, KV_WIDTH), sink_p)
    x1_s, comb_s = _sample_out(x_sample.reshape(DEC_BATCH, d), ylru_s, yatt3.reshape(DEC_BATCH * N_HEADS, LANES),
                               mod, glru, g_attn_p, w_out_p, ln1g, ln1b, wr, br)
    y_s = _moe_dense(x1_s, comb_s, mod, wgu_b, wd_b, ln2g, ln2b, DEC_BATCH)

    return (y_p.reshape(BATCH, SEQ, d),
            y_s.reshape(DEC_BATCH, 1, d),
            cstate8[:, 5:8][None],
            hlast8[:, 7][None],
            kvlast[:, :, :KV_WIDTH].reshape(1, BATCH, WINDOW, N_KV_HEADS, HEAD_DIM),
            kvlast[:, :, KV_WIDTH:].reshape(1, BATCH, WINDOW, N_KV_HEADS, HEAD_DIM),
            cstate_s[None],
            hnew_s[None],
            newk.reshape(1, DEC_BATCH, WINDOW, N_KV_HEADS, HEAD_DIM),
            newv.reshape(1, DEC_BATCH, WINDOW, N_KV_HEADS, HEAD_DIM))
```

```python
import functools

import jax
import jax.numpy as jnp
import numpy as np
from jax import lax
from jax.experimental import pallas as pl
from jax.experimental.pallas import tpu as pltpu

F32 = jnp.float32
BF16 = jnp.bfloat16
HIGHEST = lax.Precision.HIGHEST

D_MODEL = 1024
BATCH = 4
SEQ = 4096
DEC_BATCH = 128
LRU_WIDTH = 512
LRU_BLOCKS = 8
LRU_BLOCK = 64
CONV_WIDTH = 4
LRU_C = 8.0
N_HEADS = 8
N_KV_HEADS = 2
HEAD_DIM = 64
ATTN_WIDTH = 512
KV_WIDTH = 128
WINDOW = 128
IN_WIDTH = 2 * LRU_WIDTH + ATTN_WIDTH + 2 * KV_WIDTH
N_GROUPS = 4
EXPERTS_PER_GROUP = 8
N_EXPERTS = 32
D_EXPERT = 256
DEEPNORM_ALPHA = 2.0 ** 0.25
LN_EPS = 1e-5
RMS_EPS = 1e-6
ATTN_SCALE = HEAD_DIM ** -0.5

LANES = 128
ROUTE_LANES = 128
ROUTE_INFO = 40
VMEM_LIMIT = 56 * 1024 * 1024

HEAD_PERM = np.concatenate(
    [np.concatenate([np.arange(64 * c, 64 * c + 64), np.arange(64 * (c + 4), 64 * (c + 4) + 64)])
     for c in range(4)])


def _cparams(sem):
    return pltpu.CompilerParams(dimension_semantics=sem, vmem_limit_bytes=VMEM_LIMIT)


def _dot(a, b, exact):
    if exact:
        return jnp.dot(a, b, precision=HIGHEST, preferred_element_type=F32)
    return jnp.dot(a.astype(BF16), b.astype(BF16), preferred_element_type=F32)


def _dot_nt(a, b, exact):
    dn = (((1,), (1,)), ((), ()))
    if exact:
        return lax.dot_general(a, b, dn, precision=HIGHEST, preferred_element_type=F32)
    return lax.dot_general(a.astype(BF16), b.astype(BF16), dn, preferred_element_type=F32)


def _sigmoid(x):
    return 1.0 / (1.0 + jnp.exp(-x))


def _silu(x):
    return x * _sigmoid(x)


def _gelu_tanh(x):
    return 0.5 * x * (1.0 + jnp.tanh(np.sqrt(2.0 / np.pi).astype(np.float32) * (x + 0.044715 * (x * x * x))))


def _softplus(x):
    return jnp.maximum(x, 0.0) + jnp.log1p(jnp.exp(-jnp.abs(x)))


def _layer_norm(x, g, b):
    mu = jnp.mean(x, axis=-1, keepdims=True)
    xc = x - mu
    var = jnp.mean(xc * xc, axis=-1, keepdims=True)
    return xc * lax.rsqrt(var + LN_EPS) * g + b


def _rms_norm(x, g):
    return x * lax.rsqrt(jnp.mean(x * x, axis=-1, keepdims=True) + RMS_EPS) * g


def _ada_kernel(c_ref, w_ref, b_ref, o_ref):
    o_ref[...] = _dot(_silu(c_ref[...]), w_ref[...], True) + b_ref[...]


def _ada(c_all, w_ada, b_ada):
    rows = c_all.shape[0]
    bn = 1024
    return pl.pallas_call(
        _ada_kernel,
        grid=(6 * D_MODEL // bn,),
        in_specs=[pl.BlockSpec((rows, D_MODEL), lambda j: (0, 0)),
                  pl.BlockSpec((D_MODEL, bn), lambda j: (0, j)),
                  pl.BlockSpec((1, bn), lambda j: (0, j))],
        out_specs=pl.BlockSpec((rows, bn), lambda j: (0, j)),
        out_shape=jax.ShapeDtypeStruct((rows, 6 * D_MODEL), F32),
        compiler_params=_cparams(("arbitrary",)),
        name="ada_modulation",
    )(c_all, w_ada, b_ada.reshape(1, -1))


QKV_WIDTH = ATTN_WIDTH + 2 * KV_WIDTH


def _inproj_kernel(x_ref, mod_ref, w_ref, lru_ref, qkv_ref, kvlast_ref):
    sh1 = mod_ref[0, 0:1, :]
    sc1 = mod_ref[0, 1:2, :]
    h = x_ref[0] * (1.0 + sc1) + sh1
    z = _dot(h, w_ref[...], False)
    lru_ref[0] = z[:, :2 * LRU_WIDTH]
    qkv_ref[0] = z[:, 2 * LRU_WIDTH:].astype(BF16)
    kvlast_ref[0] = z[z.shape[0] - WINDOW:, 2 * LRU_WIDTH + ATTN_WIDTH:]


def _inproj(x, modp, w_in_bf16, tm=1024):
    b, t, d = x.shape
    return pl.pallas_call(
        _inproj_kernel,
        grid=(b, t // tm),
        in_specs=[pl.BlockSpec((1, tm, d), lambda i, j: (i, j, 0)),
                  pl.BlockSpec((1, 6, d), lambda i, j: (i, 0, 0)),
                  pl.BlockSpec((d, IN_WIDTH), lambda i, j: (0, 0))],
        out_specs=[pl.BlockSpec((1, tm, 2 * LRU_WIDTH), lambda i, j: (i, j, 0)),
                   pl.BlockSpec((1, tm, QKV_WIDTH), lambda i, j: (i, j, 0)),
                   pl.BlockSpec((1, WINDOW, 2 * KV_WIDTH), lambda i, j: (i, 0, 0))],
        out_shape=[jax.ShapeDtypeStruct((b, t, 2 * LRU_WIDTH), F32),
                   jax.ShapeDtypeStruct((b, t, QKV_WIDTH), BF16),
                   jax.ShapeDtypeStruct((b, WINDOW, 2 * KV_WIDTH), F32)],
        compiler_params=_cparams(("arbitrary", "arbitrary")),
        name="prompt_inproj",
    )(x, modp, w_in_bf16)


def _lru_gates(xc, wlo, whi, bgate, sp_neg_lam, exact):
    g_lo = _dot(xc[:, :256], wlo, exact)
    g_hi = _dot(xc[:, 256:], whi, exact)
    ga = jnp.concatenate([g_lo[:, :256], g_hi[:, :256]], axis=-1) + bgate[:, :LRU_WIDTH]
    gx = jnp.concatenate([g_lo[:, 256:], g_hi[:, 256:]], axis=-1) + bgate[:, LRU_WIDTH:]
    r = _sigmoid(ga)
    i = _sigmoid(gx)
    log_a = -LRU_C * r * sp_neg_lam
    a = jnp.exp(log_a)
    one_minus_a2 = -jnp.tanh(log_a) * (a * a + 1.0) if exact else 1.0 - a * a
    root = jnp.where(one_minus_a2 > 0.0, one_minus_a2 * lax.rsqrt(one_minus_a2), 0.0)
    bterm = root * (i * xc)
    return a, bterm


def _lru_kernel(z_ref, convw_ref, convb_ref, wlo_ref, whi_ref, bgate_ref, lam_ref, wg_ref, wu_ref, wd_ref,
                y_ref, cstate_ref, hlast_ref, wgub_ref, wdb_ref, tail_ref, carry_ref, *, tl):
    j = pl.program_id(1)

    @pl.when(j == 0)
    def _():
        tail_ref[...] = jnp.zeros_like(tail_ref)
        carry_ref[...] = jnp.zeros_like(carry_ref)

    wgub_ref[0, :, :D_EXPERT] = wg_ref[0].astype(BF16)
    wgub_ref[0, :, D_EXPERT:] = wu_ref[0].astype(BF16)
    wdb_ref[...] = wd_ref[...].astype(BF16)

    xb = z_ref[0, :, :LRU_WIDTH]
    gate = z_ref[0, :, LRU_WIDTH:]
    xc = convb_ref[...] + convw_ref[3:4, :] * xb
    rows8 = lax.broadcasted_iota(jnp.int32, (8, LRU_WIDTH), 0)
    tail = tail_ref[...]
    for back in (1, 2, 3):
        rolled = pltpu.roll(xb, back, axis=0)
        top = jnp.where(rows8 >= back, rolled[:8], pltpu.roll(tail, back, axis=0))
        shifted = jnp.concatenate([top, rolled[8:]], axis=0)
        xc = xc + convw_ref[3 - back:4 - back, :] * shifted
    tail_ref[...] = xb[tl - 8:, :]
    cstate_ref[0] = xb[tl - 8:, :]

    sp = _softplus(-lam_ref[...])
    a, bterm = _lru_gates(xc, wlo_ref[...], whi_ref[...], bgate_ref[...], sp, False)

    groups = tl // 8
    a = a.reshape(groups, 8, LRU_WIDTH)
    bterm = bterm.reshape(groups, 8, LRU_WIDTH)
    r8 = lax.broadcasted_iota(jnp.int32, (groups, 8, LRU_WIDTH), 1)
    s = 1
    while s < 8:
        a_sh = jnp.where(r8 >= s, pltpu.roll(a, s, axis=1), 1.0)
        b_sh = jnp.where(r8 >= s, pltpu.roll(bterm, s, axis=1), 0.0)
        bterm = a * b_sh + bterm
        a = a * a_sh
        s *= 2
    a_tot = jnp.broadcast_to(a[:, 7:8, :], (groups, 8, LRU_WIDTH))
    b_tot = jnp.broadcast_to(bterm[:, 7:8, :], (groups, 8, LRU_WIDTH))
    h_in = jnp.broadcast_to(carry_ref[7:8, :], (8, LRU_WIDTH))
    pieces = []
    for g in range(groups):
        pieces.append(a[g] * h_in + bterm[g])
        h_in = a_tot[g] * h_in + b_tot[g]
    h = jnp.concatenate(pieces, axis=0)
    carry_ref[...] = h_in
    hlast_ref[0] = h_in
    y_ref[0] = h * _gelu_tanh(gate)


def _lru(zin, conv_w, conv_b, wlo, whi, bgate, lam, w_gate, w_up, w_down, tl=512):
    b, t, _ = zin.shape
    steps = t // tl
    assert b * steps == N_EXPERTS
    d = D_MODEL
    kern = functools.partial(_lru_kernel, tl=tl)
    full = lambda shp: pl.BlockSpec(shp, lambda i, j: tuple(0 for _ in shp))
    per_step = lambda shp: pl.BlockSpec(shp, lambda i, j: (i * steps + j, 0, 0))
    return pl.pallas_call(
        kern,
        grid=(b, steps),
        in_specs=[pl.BlockSpec((1, tl, 2 * LRU_WIDTH), lambda i, j: (i, j, 0)),
                  full((CONV_WIDTH, LRU_WIDTH)), full((1, LRU_WIDTH)),
                  full((256, 512)), full((256, 512)), full((1, 2 * LRU_WIDTH)), full((1, LRU_WIDTH)),
                  per_step((1, d, D_EXPERT)), per_step((1, d, D_EXPERT)), per_step((1, D_EXPERT, d))],
        out_specs=[pl.BlockSpec((1, tl, LRU_WIDTH), lambda i, j: (i, j, 0)),
                   pl.BlockSpec((1, 8, LRU_WIDTH), lambda i, j: (i, 0, 0)),
                   pl.BlockSpec((1, 8, LRU_WIDTH), lambda i, j: (i, 0, 0)),
                   per_step((1, d, 2 * D_EXPERT)), per_step((1, D_EXPERT, d))],
        out_shape=[jax.ShapeDtypeStruct((b, t, LRU_WIDTH), F32),
                   jax.ShapeDtypeStruct((b, 8, LRU_WIDTH), F32),
                   jax.ShapeDtypeStruct((b, 8, LRU_WIDTH), F32),
                   jax.ShapeDtypeStruct((N_EXPERTS, d, 2 * D_EXPERT), BF16),
                   jax.ShapeDtypeStruct(w_down.shape, BF16)],
        scratch_shapes=[pltpu.VMEM((8, LRU_WIDTH), F32), pltpu.VMEM((8, LRU_WIDTH), F32)],
        compiler_params=_cparams(("arbitrary", "arbitrary")),
        name="prompt_rglru",
    )(zin, conv_w, conv_b, wlo, whi, bgate, lam, w_gate, w_up, w_down)


ATTN_BLOCKS = 16


def _attn_kernel(q_ref, k_ref, v_ref, sink_ref, o_ref, kprev_ref, vprev_ref):
    j = pl.program_id(1)

    @pl.when(j == 0)
    def _():
        kprev_ref[...] = jnp.zeros_like(kprev_ref)
        vprev_ref[...] = jnp.zeros_like(vprev_ref)

    blk = WINDOW
    lane = lax.broadcasted_iota(jnp.int32, (blk, LANES), 1)
    low = lane < HEAD_DIM
    qi = lax.broadcasted_iota(jnp.int32, (blk, 2 * blk), 0)
    sj = lax.broadcasted_iota(jnp.int32, (blk, 2 * blk), 1)
    rel = blk + qi - sj
    in_window = (rel >= 0) & (rel <= WINDOW)
    sink = sink_ref[...].reshape(N_HEADS, blk, 1)
    k_ext = jnp.concatenate([kprev_ref[...], k_ref[0]], axis=0)
    v_ext = jnp.concatenate([vprev_ref[...], v_ref[0]], axis=0)
    v_ext = jnp.concatenate([v_ext, jnp.ones_like(v_ext)], axis=-1)
    for n in range(ATTN_BLOCKS):
        q = q_ref[0, blk * n:blk * (n + 1), :]
        pieces = []
        for half in (0, 1):
            for c in range(4):
                qc = q[:, LANES * c:LANES * (c + 1)]
                pieces.append(jnp.where(low if half == 0 else ~low, qc, 0.0).astype(BF16))
        k_band = k_ext[blk * n:blk * (n + 2)]
        v_band = v_ext[blk * n:blk * (n + 2)]
        valid = in_window & ((sj >= blk) | (j > 0)) if n == 0 else in_window
        o8 = []
        for h in range(N_HEADS):
            s = _dot_nt(pieces[h], k_band, False) * ATTN_SCALE
            s = jnp.where(valid, s, -jnp.inf)
            m = jnp.maximum(jnp.max(s, axis=-1, keepdims=True), sink[h])
            ov = _dot(jnp.exp(s - m), v_band, False)
            den = ov[:, KV_WIDTH:] + jnp.exp(sink[h] - m)
            o8.append(ov[:, :KV_WIDTH] * (1.0 / den))
        cols = []
        for c in range(4):
            cols.append(jnp.where(low, o8[c], o8[c + 4]))
        o_ref[0, blk * n:blk * (n + 1), :] = jnp.concatenate(cols, axis=-1)
    kprev_ref[...] = k_ref[0, blk * (ATTN_BLOCKS - 1):, :]
    vprev_ref[...] = v_ref[0, blk * (ATTN_BLOCKS - 1):, :]


def _attn(qkv, sinks):
    b, t, _ = qkv.shape
    blk = WINDOW
    tq = blk * ATTN_BLOCKS
    sink_col = jnp.repeat(sinks.astype(F32), blk).reshape(N_HEADS * blk, 1)
    kcol = ATTN_WIDTH // KV_WIDTH
    return pl.pallas_call(
        _attn_kernel,
        grid=(b, t // tq),
        in_specs=[pl.BlockSpec((1, tq, ATTN_WIDTH), lambda i, j: (i, j, 0)),
                  pl.BlockSpec((1, tq, KV_WIDTH), lambda i, j: (i, j, kcol)),
                  pl.BlockSpec((1, tq, KV_WIDTH), lambda i, j: (i, j, kcol + 1)),
                  pl.BlockSpec((N_HEADS * blk, 1), lambda i, j: (0, 0))],
        out_specs=pl.BlockSpec((1, tq, ATTN_WIDTH), lambda i, j: (i, j, 0)),
        out_shape=jax.ShapeDtypeStruct((b, t, ATTN_WIDTH), F32),
        scratch_shapes=[pltpu.VMEM((blk, KV_WIDTH), BF16), pltpu.VMEM((blk, KV_WIDTH), BF16)],
        compiler_params=_cparams(("arbitrary", "arbitrary")),
        name="prompt_window_attention",
    )(qkv, qkv, qkv, sink_col)


def _route(h2, wr, br, exact):
    t = h2.shape[0]
    logits = _dot(h2, wr, exact) + br
    lane = lax.broadcasted_iota(jnp.int32, (t, ROUTE_LANES), 1).astype(F32)
    neg = -jnp.inf
    big = float(ROUTE_LANES)
    is_g = (lane >= N_EXPERTS) & (lane < N_EXPERTS + N_GROUPS)
    lg = jnp.where(is_g, logits, neg)
    mg = jnp.max(lg, axis=-1, keepdims=True)
    g_val = 1.0 / jnp.sum(jnp.where(is_g, jnp.exp(lg - mg), 0.0), axis=-1, keepdims=True)
    g_lane = jnp.min(jnp.where((lg == mg) & is_g, lane, big), axis=-1, keepdims=True)
    g_idx = g_lane - N_EXPERTS
    in_grp = (lane >= g_idx * EXPERTS_PER_GROUP) & (lane < (g_idx + 1.0) * EXPERTS_PER_GROUP)
    le = jnp.where(in_grp, logits, neg)
    me = jnp.max(le, axis=-1, keepdims=True)
    se = jnp.sum(jnp.where(in_grp, jnp.exp(le - me), 0.0), axis=-1, keepdims=True)
    l1 = jnp.min(jnp.where((le == me) & in_grp, lane, big), axis=-1, keepdims=True)
    rest = in_grp & (lane != l1)
    le2 = jnp.where(rest, le, neg)
    me2 = jnp.max(le2, axis=-1, keepdims=True)
    l2 = jnp.min(jnp.where((le2 == me2) & rest, lane, big), axis=-1, keepdims=True)
    v1 = 1.0 / se
    v2 = jnp.exp(me2 - me) / se
    tot = v1 + v2
    w1 = g_val * v1 / tot
    w2 = g_val * v2 / tot
    comb = jnp.where(lane == l1, w1, 0.0) + jnp.where(lane == l2, w2, 0.0)
    return (comb + jnp.where(lane == ROUTE_INFO, l1, 0.0) + jnp.where(lane == ROUTE_INFO + 1, l2, 0.0)
            + jnp.where(lane == ROUTE_INFO + 2, w1, 0.0) + jnp.where(lane == ROUTE_INFO + 3, w2, 0.0))


def _outproj_body(x, ylru, yatt, sh2, sc2, gt1, glru, gattn, wout, ln1g, ln1b, wr, br, exact):
    mixin = jnp.concatenate([_rms_norm(ylru, glru), _rms_norm(yatt, gattn)], axis=-1)
    mix = _dot(mixin, wout, exact)
    x1 = _layer_norm(DEEPNORM_ALPHA * x + (1.0 + gt1) * mix, ln1g, ln1b)
    h2 = x1 * (1.0 + sc2) + sh2
    return x1, _route(h2, wr, br, exact)


def _outproj_prompt_kernel(x_ref, ylru_ref, yatt_ref, mod_ref, glru_ref, gattn_ref, wout_ref,
                           ln1g_ref, ln1b_ref, wr_ref, br_ref, x1_ref, info_ref, cnt_ref, tri_ref, carry_ref,
                           *, tm, per_seq):
    i = pl.program_id(0)

    @pl.when(i == 0)
    def _():
        r = lax.broadcasted_iota(jnp.int32, (tm, tm), 0)
        c = lax.broadcasted_iota(jnp.int32, (tm, tm), 1)
        tri_ref[...] = jnp.where(c < r, 1.0, 0.0).astype(BF16)

    @pl.when(i % per_seq == 0)
    def _():
        carry_ref[...] = jnp.zeros_like(carry_ref)

    gt1 = mod_ref[0, 2:3, :]
    sh2 = mod_ref[0, 3:4, :]
    sc2 = mod_ref[0, 4:5, :]
    combs = []
    nsplit = 2
    for h in range(nsplit):
        rows = slice(h * (tm // nsplit), (h + 1) * (tm // nsplit))
        x1_h, comb_h = _outproj_body(x_ref[rows, :], ylru_ref[rows, :], yatt_ref[rows, :], sh2, sc2, gt1,
                                     glru_ref[...], gattn_ref[...], wout_ref[...], ln1g_ref[...], ln1b_ref[...],
                                     wr_ref[...], br_ref[...], False)
        x1_ref[rows, :] = x1_h
        combs.append(comb_h)
    comb = jnp.concatenate(combs, axis=0)
    lane = lax.broadcasted_iota(jnp.int32, (tm, ROUTE_LANES), 1).astype(F32)
    l1 = jnp.sum(jnp.where(lane == ROUTE_INFO, comb, 0.0), axis=-1, keepdims=True)
    l2 = jnp.sum(jnp.where(lane == ROUTE_INFO + 1, comb, 0.0), axis=-1, keepdims=True)
    o1 = lane == l1
    o2 = lane == l2
    onehot = jnp.where(o1 | o2, 1.0, 0.0)
    before = jnp.dot(tri_ref[...], onehot.astype(BF16), preferred_element_type=F32) + carry_ref[0:1, :]
    rank1 = jnp.sum(jnp.where(o1, before, 0.0), axis=-1, keepdims=True)
    rank2 = jnp.sum(jnp.where(o2, before, 0.0), axis=-1, keepdims=True)
    total = carry_ref[0:1, :] + jnp.sum(onehot, axis=0, keepdims=True)
    carry_ref[...] = jnp.broadcast_to(total, carry_ref.shape)
    cnt_ref[0] = jnp.broadcast_to(total, (8, ROUTE_LANES))
    info = (comb + jnp.where(lane == ROUTE_INFO + 4, rank1, 0.0) + jnp.where(lane == ROUTE_INFO + 5, rank2, 0.0))
    info_ref[0] = jnp.transpose(info)[ROUTE_INFO:ROUTE_INFO + 8, :]


OUTPROJ_TILE = 1024


def _outproj_prompt(x2d, ylru2d, yatt2d, modp, glru, gattn, wout_bf16, ln1g, ln1b, wr, br, tm=OUTPROJ_TILE):
    n, d = x2d.shape
    per_seq = SEQ // tm
    full = lambda shp: pl.BlockSpec(shp, lambda i: tuple(0 for _ in shp))
    kern = functools.partial(_outproj_prompt_kernel, tm=tm, per_seq=per_seq)
    return pl.pallas_call(
        kern,
        grid=(n // tm,),
        in_specs=[pl.BlockSpec((tm, d), lambda i: (i, 0)),
                  pl.BlockSpec((tm, LRU_WIDTH), lambda i: (i, 0)),
                  pl.BlockSpec((tm, ATTN_WIDTH), lambda i: (i, 0)),
                  pl.BlockSpec((1, 6, d), lambda i: (i // per_seq, 0, 0)),
                  full((1, LRU_WIDTH)), full((1, ATTN_WIDTH)), full((d, d)),
                  full((1, d)), full((1, d)), full((d, ROUTE_LANES)), full((1, ROUTE_LANES))],
        out_specs=[pl.BlockSpec((tm, d), lambda i: (i, 0)),
                   pl.BlockSpec((1, 8, tm), lambda i: (i, 0, 0)),
                   pl.BlockSpec((1, 8, ROUTE_LANES), lambda i: (i // per_seq, 0, 0))],
        out_shape=[jax.ShapeDtypeStruct((n, d), F32),
                   jax.ShapeDtypeStruct((n // tm, 8, tm), F32),
                   jax.ShapeDtypeStruct((n // SEQ, 8, ROUTE_LANES), F32)],
        scratch_shapes=[pltpu.VMEM((tm, tm), BF16), pltpu.VMEM((8, ROUTE_LANES), F32)],
        compiler_params=_cparams(("arbitrary",)),
        name="prompt_outproj_ln_route",
    )(x2d, ylru2d, yatt2d, modp, glru, gattn, wout_bf16, ln1g, ln1b, wr, br)


DENSE_EXPERTS_PER_STEP = 4


def _moe_kernel(x1_ref, comb_ref, sh2_ref, sc2_ref, gt2_ref, wgu_ref, wd_ref, ln2g_ref, ln2b_ref,
                o_ref, h2_ref, acc_ref):
    g = pl.program_id(1)

    @pl.when(g == 0)
    def _():
        h2_ref[...] = (x1_ref[...] * (1.0 + sc2_ref[...]) + sh2_ref[...]).astype(BF16)
        acc_ref[...] = jnp.zeros_like(acc_ref)

    h2 = h2_ref[...]
    comb = comb_ref[...]
    lane = lax.broadcasted_iota(jnp.int32, comb.shape, 1)
    part = None
    for k in range(DENSE_EXPERTS_PER_STEP):
        au = jnp.dot(h2, wgu_ref[k], preferred_element_type=F32)
        c_e = jnp.sum(jnp.where(lane == g * DENSE_EXPERTS_PER_STEP + k, comb, 0.0), axis=-1, keepdims=True)
        z = _silu(au[:, :D_EXPERT]) * au[:, D_EXPERT:] * c_e
        y = jnp.dot(z.astype(BF16), wd_ref[k], preferred_element_type=F32)
        part = y if part is None else part + y
    acc_ref[...] += part

    @pl.when(g == N_EXPERTS // DENSE_EXPERTS_PER_STEP - 1)
    def _():
        o_ref[...] = _layer_norm(DEEPNORM_ALPHA * x1_ref[...] + (1.0 + gt2_ref[...]) * acc_ref[...],
                                 ln2g_ref[...], ln2b_ref[...])


def _moe_dense(x1, comb, mod, wgu_bf16, wd_bf16, ln2g, ln2b, tm):
    n, d = x1.shape
    eg = DENSE_EXPERTS_PER_STEP
    mspec = lambda k: pl.BlockSpec((tm, d), lambda i, e: (i, k))
    full = lambda shp: pl.BlockSpec(shp, lambda i, e: tuple(0 for _ in shp))
    return pl.pallas_call(
        _moe_kernel,
        grid=(n // tm, N_EXPERTS // eg),
        in_specs=[pl.BlockSpec((tm, d), lambda i, e: (i, 0)),
                  pl.BlockSpec((tm, ROUTE_LANES), lambda i, e: (i, 0)),
                  mspec(3), mspec(4), mspec(5),
                  pl.BlockSpec((eg, d, 2 * D_EXPERT), lambda i, e: (e, 0, 0)),
                  pl.BlockSpec((eg, D_EXPERT, d), lambda i, e: (e, 0, 0)),
                  full((1, d)), full((1, d))],
        out_specs=pl.BlockSpec((tm, d), lambda i, e: (i, 0)),
        out_shape=jax.ShapeDtypeStruct((n, d), F32),
        scratch_shapes=[pltpu.VMEM((tm, d), BF16), pltpu.VMEM((tm, d), F32)],
        compiler_params=_cparams(("arbitrary", "arbitrary")),
        name="moe_dense_ln",
    )(x1, comb, mod, mod, mod, wgu_bf16, wd_bf16, ln2g, ln2b)


RB_SUB = 512
RB_NSUB = SEQ // RB_SUB
RB_CHUNK = 96
RB_NCHUNK = -(-2 * SEQ // RB_CHUNK)
RB_PITCH = RB_CHUNK + 8
RB_SPITCH = RB_SUB + 8
RB_GROUP = 4
RB_WSLOTS = 4


def _rb_kernel(cnt_ref, x1_ref, mod_ref, offs_ref, wts_ref, wgu_hbm, wd_hbm, ln2g_ref, ln2b_ref,
               o_ref, buf_ref, stage_ref, wgu_buf, wd_buf, start_ref, sem):
    b = pl.program_id(0)
    s = pl.program_id(1)

    @pl.when(s == 0)
    def _starts():
        def body(e, run):
            start_ref[e] = run
            return run + cnt_ref[b, e]
        lax.fori_loop(0, N_EXPERTS, body, jnp.int32(0))
        buf_ref[(RB_NCHUNK - 1) * 8 * RB_PITCH:(RB_NCHUNK + RB_GROUP) * 8 * RB_PITCH, :] = jnp.zeros(
            ((RB_GROUP + 1) * 8 * RB_PITCH, LANES), F32)

    @pl.when(s < RB_NSUB)
    def _dispatch():
        sh2 = mod_ref[0, 3:4, :]
        sc2 = mod_ref[0, 4:5, :]
        h2 = x1_ref[...] * (1.0 + sc2) + sh2
        for j in range(8):
            stage_ref[RB_SPITCH * j:RB_SPITCH * j + RB_SUB, :] = h2[:, LANES * j:LANES * (j + 1)]

        for t in range(RB_SUB):
            slab = stage_ref[pl.ds(t, 8, stride=RB_SPITCH), :]
            for a in range(2):
                buf_ref[pl.ds(offs_ref[0, a, t], 8, stride=RB_PITCH), :] = slab

    @pl.when(s == RB_NSUB)
    def _experts():
        def copies(e, slot):
            return (pltpu.make_async_copy(wgu_hbm.at[e], wgu_buf.at[slot], sem.at[slot, 0]),
                    pltpu.make_async_copy(wd_hbm.at[e], wd_buf.at[slot], sem.at[slot, 1]))

        def run_expert(e, slot):
            lo_row = start_ref[e]
            hi_row = lo_row + cnt_ref[b, e]

            c_lo = lax.div(lo_row, RB_CHUNK)
            c_hi = lax.div(hi_row + (RB_CHUNK - 1), RB_CHUNK)
            row = lax.broadcasted_iota(jnp.int32, (RB_CHUNK, 1), 0)

            def load(c):
                base = pl.multiple_of(c * (8 * RB_PITCH), 8)
                return [buf_ref[pl.ds(base + RB_PITCH * j, RB_CHUNK), :] for j in range(8)]

            def store(c, tiles, y):
                base = pl.multiple_of(c * (8 * RB_PITCH), 8)
                mine = (row >= lo_row - c * RB_CHUNK) & (row < hi_row - c * RB_CHUNK)
                for j in range(8):
                    buf_ref[pl.ds(base + RB_PITCH * j, RB_CHUNK), :] = jnp.where(
                        mine, y[:, LANES * j:LANES * (j + 1)], tiles[j])

            def group(first, nchunks):
                cs = [first]
                for k in range(1, nchunks):
                    cs.append(jnp.where(first + k < c_hi, first + k, RB_NCHUNK + k))
                tiles = [load(c) for c in cs]
                x = jnp.concatenate([jnp.concatenate(t, axis=-1) for t in tiles], axis=0).astype(BF16)
                au = jnp.dot(x, wgu_buf[slot], preferred_element_type=F32)
                z = (_silu(au[:, :D_EXPERT]) * au[:, D_EXPERT:]).astype(BF16)
                y = jnp.dot(z, wd_buf[slot], preferred_element_type=F32)
                for k, c in enumerate(cs):
                    store(c, tiles[k], y[RB_CHUNK * k:RB_CHUNK * (k + 1)])

            span = c_hi - c_lo

            @pl.when(span < RB_GROUP)
            def _():
                group(c_lo, RB_GROUP - 1)

            @pl.when(span >= RB_GROUP)
            def _():
                def body(i, carry):
                    group(c_lo + RB_GROUP * i, RB_GROUP)
                    return carry
                lax.fori_loop(0, lax.div(span + (RB_GROUP - 1), RB_GROUP), body, 0)

        for e in range(RB_WSLOTS - 1):
            for c in copies(e, e):
                c.start()

        def ring_body(i, carry):
            for k in range(RB_WSLOTS):
                e = RB_WSLOTS * i + k
                ahead = e + RB_WSLOTS - 1

                @pl.when(ahead < N_EXPERTS)
                def _():
                    for c in copies(ahead, (k + RB_WSLOTS - 1) % RB_WSLOTS):
                        c.start()
                for c in copies(e, k):
                    c.wait()
                run_expert(e, k)
            return carry
        lax.fori_loop(0, N_EXPERTS // RB_WSLOTS, ring_body, 0)

    @pl.when(s > RB_NSUB)
    def _combine():
        for t in range(RB_SUB):
            acc = None
            for a in range(2):
                term = wts_ref[0, a, t] * buf_ref[pl.ds(offs_ref[0, a, t], 8, stride=RB_PITCH), :]
                acc = term if acc is None else acc + term
            stage_ref[pl.ds(t, 8, stride=RB_SPITCH), :] = acc
        gt2 = mod_ref[0, 5:6, :]
        f = jnp.concatenate([stage_ref[RB_SPITCH * j:RB_SPITCH * j + RB_SUB, :] for j in range(8)], axis=-1)
        o_ref[...] = _layer_norm(DEEPNORM_ALPHA * x1_ref[...] + (1.0 + gt2) * f, ln2g_ref[...], ln2b_ref[...])


def _rb_retile(a):
    tiles, two, t = a.shape
    return a.reshape(tiles, two, t // RB_SUB, RB_SUB).transpose(0, 2, 1, 3).reshape(-1, two, RB_SUB)


def _rb_offsets(cnt, e12, rank12):
    start = jnp.cumsum(cnt, axis=-1) - cnt
    start_t = jnp.repeat(start, e12.shape[0] // cnt.shape[0], axis=0)[:, None, None, :]
    hit = e12[..., None] == jnp.arange(N_EXPERTS, dtype=jnp.int32)
    p = jnp.sum(jnp.where(hit, start_t, 0), axis=-1) + rank12
    return (p // RB_CHUNK) * (8 * RB_PITCH) + p % RB_CHUNK


def _rb_moe(x1, modp, cnt, offs, wts, wgu_bf16, wd_bf16, ln2g, ln2b):
    n, d = x1.shape
    bsz = n // SEQ
    nsteps = 2 * RB_NSUB + 1

    def sub_index(s):
        return jnp.where(s < RB_NSUB, s, jnp.where(s == RB_NSUB, RB_NSUB - 1, s - RB_NSUB - 1))

    def tile_map(b, s, cnt_r):
        return (b * RB_NSUB + sub_index(s), 0)

    def tile_map3(b, s, cnt_r):
        return (b * RB_NSUB + sub_index(s), 0, 0)

    def out_map(b, s, cnt_r):
        return (b * RB_NSUB + jnp.maximum(s - RB_NSUB - 1, 0), 0)

    const = lambda shp: pl.BlockSpec(shp, lambda b, s, cnt_r: tuple(0 for _ in shp))
    anyspec = pl.BlockSpec(memory_space=pl.ANY)
    grid_spec = pltpu.PrefetchScalarGridSpec(
        num_scalar_prefetch=1,
        grid=(bsz, nsteps),
        in_specs=[pl.BlockSpec((RB_SUB, d), tile_map),
                  pl.BlockSpec((1, 6, d), lambda b, s, cnt_r: (b, 0, 0)),
                  pl.BlockSpec((1, 2, RB_SUB), tile_map3, memory_space=pltpu.SMEM),
                  pl.BlockSpec((1, 2, RB_SUB), tile_map3, memory_space=pltpu.SMEM),
                  anyspec, anyspec,
                  const((1, d)), const((1, d))],
        out_specs=pl.BlockSpec((RB_SUB, d), out_map),
        scratch_shapes=[pltpu.VMEM(((RB_NCHUNK + RB_GROUP) * 8 * RB_PITCH, LANES), F32),
                        pltpu.VMEM((8 * RB_SPITCH, LANES), F32),
                        pltpu.VMEM((RB_WSLOTS, d, 2 * D_EXPERT), BF16),
                        pltpu.VMEM((RB_WSLOTS, D_EXPERT, d), BF16),
                        pltpu.SMEM((N_EXPERTS,), jnp.int32),
                        pltpu.SemaphoreType.DMA((RB_WSLOTS, 2))])
    return pl.pallas_call(
        _rb_kernel,
        grid_spec=grid_spec,
        out_shape=jax.ShapeDtypeStruct((n, d), F32),
        compiler_params=_cparams(("arbitrary", "arbitrary")),
        name="moe_routed_ln",
    )(cnt, x1, modp, offs, wts, wgu_bf16, wd_bf16, ln2g, ln2b)


def _sample_in_kernel(x_ref, mod_ref, win_ref, ctx_ref, h0_ref, convw_ref, convb_ref,
                      wlo_ref, whi_ref, bgate_ref, lam_ref,
                      ylru_ref, q_ref, k_ref, v_ref, cstate_ref, hnew_ref):
    sh1 = mod_ref[0:DEC_BATCH, 0:D_MODEL]
    sc1 = mod_ref[0:DEC_BATCH, D_MODEL:2 * D_MODEL]
    h = x_ref[...] * (1.0 + sc1) + sh1
    z = _dot(h, win_ref[...], True)
    xb = z[:, :LRU_WIDTH]
    gate = z[:, LRU_WIDTH:2 * LRU_WIDTH]
    c0 = ctx_ref[:, 0, :]
    c1 = ctx_ref[:, 1, :]
    c2 = ctx_ref[:, 2, :]
    xc = (convb_ref[...] + convw_ref[0:1, :] * c0 + convw_ref[1:2, :] * c1
          + convw_ref[2:3, :] * c2 + convw_ref[3:4, :] * xb)
    cstate_ref[:, 0, :] = c1
    cstate_ref[:, 1, :] = c2
    cstate_ref[:, 2, :] = xb
    sp = _softplus(-lam_ref[...])
    a, bterm = _lru_gates(xc, wlo_ref[...], whi_ref[...], bgate_ref[...], sp, True)
    hn = a * h0_ref[...] + bterm
    hnew_ref[...] = hn
    ylru_ref[...] = hn * _gelu_tanh(gate)
    low = lax.broadcasted_iota(jnp.int32, (DEC_BATCH, LANES), 1) < HEAD_DIM
    for c in range(4):
        qc = z[:, 2 * LRU_WIDTH + LANES * c:2 * LRU_WIDTH + LANES * (c + 1)]
        q_ref[pl.ds(c, DEC_BATCH, stride=N_HEADS), :] = jnp.where(low, qc, 0.0)
        q_ref[pl.ds(c + 4, DEC_BATCH, stride=N_HEADS), :] = jnp.where(low, 0.0, qc)
    k_ref[...] = z[:, 2 * LRU_WIDTH + ATTN_WIDTH:2 * LRU_WIDTH + ATTN_WIDTH + KV_WIDTH]
    v_ref[...] = z[:, 2 * LRU_WIDTH + ATTN_WIDTH + KV_WIDTH:]


def _sample_in(x, mod, w_in_p, ctx, h0, conv_w, conv_b, wlo, whi, bgate, lam):
    n = DEC_BATCH
    outs = [jax.ShapeDtypeStruct((n, LRU_WIDTH), F32),
            jax.ShapeDtypeStruct((n * N_HEADS, LANES), F32),
            jax.ShapeDtypeStruct((n, KV_WIDTH), F32),
            jax.ShapeDtypeStruct((n, KV_WIDTH), F32),
            jax.ShapeDtypeStruct((n, CONV_WIDTH - 1, LRU_WIDTH), F32),
            jax.ShapeDtypeStruct((n, LRU_WIDTH), F32)]
    return pl.pallas_call(
        _sample_in_kernel,
        out_shape=outs,
        compiler_params=pltpu.CompilerParams(vmem_limit_bytes=VMEM_LIMIT),
        name="sample_inproj_rglru",
    )(x, mod, w_in_p, ctx, h0, conv_w, conv_b, wlo, whi, bgate, lam)


def _sample_attn_kernel(q_ref, kn_ref, vn_ref, ck_ref, cv_ref, sink_ref, y_ref, nk_ref, nv_ref, *, bb):
    rows = lax.broadcasted_iota(jnp.int32, (WINDOW, KV_WIDTH), 0)
    nh = N_HEADS
    q_all = q_ref[...].reshape(bb * nh, LANES)
    kcat = ck_ref[...].reshape(bb * WINDOW, KV_WIDTH)
    vcat = cv_ref[...].reshape(bb * WINDOW, KV_WIDTH)
    kn_rep = jnp.broadcast_to(kn_ref[...][:, None, :], (bb, nh, KV_WIDTH)).reshape(bb * nh, KV_WIDTH)
    vn_rep = jnp.broadcast_to(vn_ref[...][:, None, :], (bb, nh, KV_WIDTH)).reshape(bb * nh, KV_WIDTH)
    sink = jnp.concatenate([sink_ref[...]] * bb, axis=0)
    s_full = _dot_nt(q_all, kcat, True)
    s = jnp.concatenate([s_full[nh * b:nh * (b + 1), WINDOW * b:WINDOW * (b + 1)] for b in range(bb)],
                        axis=0) * ATTN_SCALE
    s_self = jnp.sum(q_all * kn_rep, axis=-1, keepdims=True) * ATTN_SCALE
    m = jnp.maximum(jnp.maximum(jnp.max(s, axis=-1, keepdims=True), s_self), sink)
    e = jnp.exp(s - m)
    e_self = jnp.exp(s_self - m)
    den = jnp.sum(e, axis=-1, keepdims=True) + e_self + jnp.exp(sink - m)
    inv = 1.0 / den
    p = e * inv
    zero = jnp.zeros((nh, WINDOW), F32)
    p_wide = jnp.concatenate(
        [jnp.concatenate([p[nh * b:nh * (b + 1)] if c == b else zero for c in range(bb)], axis=-1)
         for b in range(bb)], axis=0)
    o = _dot(p_wide, vcat, True) + (e_self * inv) * vn_rep
    y_ref[...] = o.reshape(bb, nh, LANES)
    for b in range(bb):
        nk_ref[b] = jnp.where(rows == WINDOW - 1, kn_ref[b:b + 1, :], pltpu.roll(ck_ref[b], WINDOW - 1, axis=0))
        nv_ref[b] = jnp.where(rows == WINDOW - 1, vn_ref[b:b + 1, :], pltpu.roll(cv_ref[b], WINDOW - 1, axis=0))


def _sample_attn(q3, kn, vn, cache_k, cache_v, sinks, bb=16):
    n = DEC_BATCH
    kern = functools.partial(_sample_attn_kernel, bb=bb)
    return pl.pallas_call(
        kern,
        grid=(n // bb,),
        in_specs=[pl.BlockSpec((bb, N_HEADS, LANES), lambda i: (i, 0, 0)),
                  pl.BlockSpec((bb, KV_WIDTH), lambda i: (i, 0)),
                  pl.BlockSpec((bb, KV_WIDTH), lambda i: (i, 0)),
                  pl.BlockSpec((bb, WINDOW, KV_WIDTH), lambda i: (i, 0, 0)),
                  pl.BlockSpec((bb, WINDOW, KV_WIDTH), lambda i: (i, 0, 0)),
                  pl.BlockSpec((N_HEADS, 1), lambda i: (0, 0))],
        out_specs=[pl.BlockSpec((bb, N_HEADS, LANES), lambda i: (i, 0, 0)),
                   pl.BlockSpec((bb, WINDOW, KV_WIDTH), lambda i: (i, 0, 0)),
                   pl.BlockSpec((bb, WINDOW, KV_WIDTH), lambda i: (i, 0, 0))],
        out_shape=[jax.ShapeDtypeStruct((n, N_HEADS, LANES), F32),
                   jax.ShapeDtypeStruct((n, WINDOW, KV_WIDTH), F32),
                   jax.ShapeDtypeStruct((n, WINDOW, KV_WIDTH), F32)],
        compiler_params=_cparams(("arbitrary",)),
        name="sample_cache_attention",
    )(q3, kn, vn, cache_k, cache_v, sinks.reshape(N_HEADS, 1))


def _sample_out_kernel(x_ref, ylru_ref, yatt_ref, mod_ref, glru_ref, gattn_ref, wout_ref,
                       ln1g_ref, ln1b_ref, wr_ref, br_ref, x1_ref, comb_ref):
    low = lax.broadcasted_iota(jnp.int32, (DEC_BATCH, LANES), 1) < HEAD_DIM
    yatt = jnp.concatenate(
        [jnp.where(low, yatt_ref[pl.ds(c, DEC_BATCH, stride=N_HEADS), :],
                   yatt_ref[pl.ds(c + 4, DEC_BATCH, stride=N_HEADS), :]) for c in range(4)], axis=-1)
    gt1 = mod_ref[0:DEC_BATCH, 2 * D_MODEL:3 * D_MODEL]
    sh2 = mod_ref[0:DEC_BATCH, 3 * D_MODEL:4 * D_MODEL]
    sc2 = mod_ref[0:DEC_BATCH, 4 * D_MODEL:5 * D_MODEL]
    x1, comb = _outproj_body(x_ref[...], ylru_ref[...], yatt, sh2, sc2, gt1,
                             glru_ref[...], gattn_ref[...], wout_ref[...], ln1g_ref[...], ln1b_ref[...],
                             wr_ref[...], br_ref[...], True)
    x1_ref[...] = x1
    comb_ref[...] = comb


def _sample_out(x, ylru, yatt2d, mod, glru, gattn, wout_p, ln1g, ln1b, wr, br):
    n = DEC_BATCH
    return pl.pallas_call(
        _sample_out_kernel,
        out_shape=[jax.ShapeDtypeStruct((n, D_MODEL), F32), jax.ShapeDtypeStruct((n, ROUTE_LANES), F32)],
        compiler_params=pltpu.CompilerParams(vmem_limit_bytes=VMEM_LIMIT),
        name="sample_outproj_ln_route",
    )(x, ylru, yatt2d, mod, glru, gattn, wout_p, ln1g, ln1b, wr, br)


def _block_diag_halves(w_a, w_x):
    def bd(w4):
        eye = jnp.eye(4, dtype=w4.dtype)
        return (w4[:, :, None, :] * eye[:, None, :, None]).reshape(256, 256)
    lo = jnp.concatenate([bd(w_a[:4]), bd(w_x[:4])], axis=1)
    hi = jnp.concatenate([bd(w_a[4:]), bd(w_x[4:])], axis=1)
    return lo, hi


def kernel(x_prompt, x_sample, c_prompt, c_sample, state_conv, state_h, cache_k, cache_v, w_ada, b_ada, w_in,
           conv_w, conv_b, w_rg_a, b_rg_a, w_rg_x, b_rg_x, lru_lambda, sinks, g_lru, g_attn, w_out, ln1_g, ln1_b,
           w_group, b_group, w_router, b_router, w_gate, w_up, w_down, ln2_g, ln2_b):
    d = D_MODEL
    perm = jnp.asarray(HEAD_PERM)
    w_in0 = w_in[0]
    q0 = 2 * LRU_WIDTH
    w_in_p = jnp.concatenate([w_in0[:, :q0], w_in0[:, q0:q0 + ATTN_WIDTH][:, perm], w_in0[:, q0 + ATTN_WIDTH:]],
                             axis=1)
    w_out0 = w_out[0]
    w_out_p = jnp.concatenate([w_out0[:LRU_WIDTH], w_out0[LRU_WIDTH:][perm]], axis=0)
    g_attn_p = g_attn[0][perm].reshape(1, -1)
    glru = g_lru[0].reshape(1, -1)
    wlo, whi = _block_diag_halves(w_rg_a[0], w_rg_x[0])
    bgate = jnp.concatenate([b_rg_a[0].reshape(-1), b_rg_x[0].reshape(-1)]).reshape(1, -1)
    lam = lru_lambda[0].reshape(1, -1)
    convw = conv_w[0]
    convb = conv_b[0].reshape(1, -1)
    ln1g, ln1b = ln1_g[0].reshape(1, -1), ln1_b[0].reshape(1, -1)
    ln2g, ln2b = ln2_g[0].reshape(1, -1), ln2_b[0].reshape(1, -1)
    wr = jnp.concatenate([jnp.transpose(w_router[0], (1, 0, 2)).reshape(d, N_EXPERTS), w_group[0],
                          jnp.zeros((d, ROUTE_LANES - N_EXPERTS - N_GROUPS), F32)], axis=1)
    br = jnp.concatenate([b_router[0].reshape(-1), b_group[0],
                          jnp.zeros((ROUTE_LANES - N_EXPERTS - N_GROUPS,), F32)]).reshape(1, -1)
    sink_p = sinks[0]

    c_all = jnp.concatenate([c_sample, c_prompt, jnp.zeros((8 - BATCH, d), F32)], axis=0)
    mod = _ada(c_all, w_ada[0], b_ada[0])
    modp = mod[DEC_BATCH:DEC_BATCH + BATCH].reshape(BATCH, 6, d)

    zlru, zqkv, kvlast = _inproj(x_prompt, modp, w_in_p.astype(BF16))
    ylru, cstate8, hlast8, wgu_b, wd_b = _lru(zlru, convw, convb, wlo.astype(BF16), whi.astype(BF16), bgate, lam,
                                                   w_gate[0], w_up[0], w_down[0])
    yatt = _attn(zqkv, sink_p)
    n_p = BATCH * SEQ
    x1_p, info, cntf = _outproj_prompt(x_prompt.reshape(n_p, d), ylru.reshape(n_p, LRU_WIDTH),
                                       yatt.reshape(n_p, ATTN_WIDTH), modp, glru, g_attn_p, w_out_p.astype(BF16),
                                       ln1g, ln1b, wr, br, tm=OUTPROJ_TILE)
    cnt = cntf[:, 0, :N_EXPERTS].astype(jnp.int32)
    offs = _rb_retile(_rb_offsets(cnt, info[:, 0:2].astype(jnp.int32), info[:, 4:6].astype(jnp.int32)))
    y_p = _rb_moe(x1_p, modp, cnt, offs, _rb_retile(info[:, 2:4]), wgu_b, wd_b, ln2g, ln2b)

    ylru_s, q2d, kn, vn, cstate_s, hnew_s = _sample_in(
        x_sample.reshape(DEC_BATCH, d), mod, w_in_p, state_conv[0], state_h[0],
        convw, convb, wlo, whi, bgate, lam)
    yatt3, newk, newv = _sample_attn(q2d.reshape(DEC_BATCH, N_HEADS, LANES), kn, vn,
                                     cache_k[0].reshape(DEC_BATCH, WINDOW, KV_WIDTH),
                                     cache_v[0].reshape(DEC_BATCH, WINDOW, KV_WIDTH), sink_p)
    x1_s, comb_s = _sample_out(x_sample.reshape(DEC_BATCH, d), ylru_s, yatt3.reshape(DEC_BATCH * N_HEADS, LANES),
                               mod, glru, g_attn_p, w_out_p, ln1g, ln1b, wr, br)
    y_s = _moe_dense(x1_s, comb_s, mod, wgu_b, wd_b, ln2g, ln2b, DEC_BATCH)

    return (y_p.reshape(BATCH, SEQ, d),
            y_s.reshape(DEC_BATCH, 1, d),
            cstate8[:, 5:8][None],
            hlast8[:, 7][None],
            kvlast[:, :, :KV_WIDTH].reshape(1, BATCH, WINDOW, N_KV_HEADS, HEAD_DIM),
            kvlast[:, :, KV_WIDTH:].reshape(1, BATCH, WINDOW, N_KV_HEADS, HEAD_DIM),
            cstate_s[None],
            hnew_s[None],
            newk.reshape(1, DEC_BATCH, WINDOW, N_KV_HEADS, HEAD_DIM),
            newv.reshape(1, DEC_BATCH, WINDOW, N_KV_HEADS, HEAD_DIM))
```

```python
import functools

import jax
import jax.numpy as jnp
import numpy as np
from jax import lax
from jax.experimental import pallas as pl
from jax.experimental.pallas import tpu as pltpu

F32 = jnp.float32
BF16 = jnp.bfloat16
HIGHEST = lax.Precision.HIGHEST

D_MODEL = 1024
BATCH = 4
SEQ = 4096
DEC_BATCH = 128
LRU_WIDTH = 512
LRU_BLOCKS = 8
LRU_BLOCK = 64
CONV_WIDTH = 4
LRU_C = 8.0
N_HEADS = 8
N_KV_HEADS = 2
HEAD_DIM = 64
ATTN_WIDTH = 512
KV_WIDTH = 128
WINDOW = 128
IN_WIDTH = 2 * LRU_WIDTH + ATTN_WIDTH + 2 * KV_WIDTH
N_GROUPS = 4
EXPERTS_PER_GROUP = 8
N_EXPERTS = 32
D_EXPERT = 256
DEEPNORM_ALPHA = 2.0 ** 0.25
LN_EPS = 1e-5
RMS_EPS = 1e-6
ATTN_SCALE = HEAD_DIM ** -0.5

LANES = 128
ROUTE_LANES = 128
ROUTE_INFO = 40
VMEM_LIMIT = 56 * 1024 * 1024

HEAD_PERM = np.concatenate(
    [np.concatenate([np.arange(64 * c, 64 * c + 64), np.arange(64 * (c + 4), 64 * (c + 4) + 64)])
     for c in range(4)])


def _cparams(sem):
    return pltpu.CompilerParams(dimension_semantics=sem, vmem_limit_bytes=VMEM_LIMIT)


def _dot(a, b, exact):
    if exact:
        return jnp.dot(a, b, precision=HIGHEST, preferred_element_type=F32)
    return jnp.dot(a.astype(BF16), b.astype(BF16), preferred_element_type=F32)


def _dot_nt(a, b, exact):
    dn = (((1,), (1,)), ((), ()))
    if exact:
        return lax.dot_general(a, b, dn, precision=HIGHEST, preferred_element_type=F32)
    return lax.dot_general(a.astype(BF16), b.astype(BF16), dn, preferred_element_type=F32)


def _sigmoid(x):
    return 1.0 / (1.0 + jnp.exp(-x))


def _silu(x):
    return x * _sigmoid(x)


def _gelu_tanh(x):
    return 0.5 * x * (1.0 + jnp.tanh(np.sqrt(2.0 / np.pi).astype(np.float32) * (x + 0.044715 * (x * x * x))))


def _softplus(x):
    return jnp.maximum(x, 0.0) + jnp.log1p(jnp.exp(-jnp.abs(x)))


def _layer_norm(x, g, b):
    mu = jnp.mean(x, axis=-1, keepdims=True)
    xc = x - mu
    var = jnp.mean(xc * xc, axis=-1, keepdims=True)
    return xc * lax.rsqrt(var + LN_EPS) * g + b


def _rms_norm(x, g):
    return x * lax.rsqrt(jnp.mean(x * x, axis=-1, keepdims=True) + RMS_EPS) * g


def _ada_kernel(c_ref, w_ref, b_ref, o_ref):
    o_ref[...] = _dot(_silu(c_ref[...]), w_ref[...], True) + b_ref[...]


def _ada(c_all, w_ada, b_ada):
    rows = c_all.shape[0]
    bn = 1024
    return pl.pallas_call(
        _ada_kernel,
        grid=(6 * D_MODEL // bn,),
        in_specs=[pl.BlockSpec((rows, D_MODEL), lambda j: (0, 0)),
                  pl.BlockSpec((D_MODEL, bn), lambda j: (0, j)),
                  pl.BlockSpec((1, bn), lambda j: (0, j))],
        out_specs=pl.BlockSpec((rows, bn), lambda j: (0, j)),
        out_shape=jax.ShapeDtypeStruct((rows, 6 * D_MODEL), F32),
        compiler_params=_cparams(("arbitrary",)),
        name="ada_modulation",
    )(c_all, w_ada, b_ada.reshape(1, -1))


QKV_WIDTH = ATTN_WIDTH + 2 * KV_WIDTH


def _inproj_kernel(x_ref, mod_ref, w_ref, lru_ref, qkv_ref, kvlast_ref):
    sh1 = mod_ref[0, 0:1, :]
    sc1 = mod_ref[0, 1:2, :]
    h = x_ref[0] * (1.0 + sc1) + sh1
    z = _dot(h, w_ref[...], False)
    lru_ref[0] = z[:, :2 * LRU_WIDTH]
    qkv_ref[0] = z[:, 2 * LRU_WIDTH:].astype(BF16)
    kvlast_ref[0] = z[z.shape[0] - WINDOW:, 2 * LRU_WIDTH + ATTN_WIDTH:]


def _inproj(x, modp, w_in_bf16, tm=1024):
    b, t, d = x.shape
    return pl.pallas_call(
        _inproj_kernel,
        grid=(b, t // tm),
        in_specs=[pl.BlockSpec((1, tm, d), lambda i, j: (i, j, 0)),
                  pl.BlockSpec((1, 6, d), lambda i, j: (i, 0, 0)),
                  pl.BlockSpec((d, IN_WIDTH), lambda i, j: (0, 0))],
        out_specs=[pl.BlockSpec((1, tm, 2 * LRU_WIDTH), lambda i, j: (i, j, 0)),
                   pl.BlockSpec((1, tm, QKV_WIDTH), lambda i, j: (i, j, 0)),
                   pl.BlockSpec((1, WINDOW, 2 * KV_WIDTH), lambda i, j: (i, 0, 0))],
        out_shape=[jax.ShapeDtypeStruct((b, t, 2 * LRU_WIDTH), F32),
                   jax.ShapeDtypeStruct((b, t, QKV_WIDTH), BF16),
                   jax.ShapeDtypeStruct((b, WINDOW, 2 * KV_WIDTH), F32)],
        compiler_params=_cparams(("arbitrary", "arbitrary")),
        name="prompt_inproj",
    )(x, modp, w_in_bf16)


def _lru_gates(xc, wlo, whi, bgate, sp_neg_lam, exact):
    g_lo = _dot(xc[:, :256], wlo, exact)
    g_hi = _dot(xc[:, 256:], whi, exact)
    ga = jnp.concatenate([g_lo[:, :256], g_hi[:, :256]], axis=-1) + bgate[:, :LRU_WIDTH]
    gx = jnp.concatenate([g_lo[:, 256:], g_hi[:, 256:]], axis=-1) + bgate[:, LRU_WIDTH:]
    r = _sigmoid(ga)
    i = _sigmoid(gx)
    log_a = -LRU_C * r * sp_neg_lam
    a = jnp.exp(log_a)
    one_minus_a2 = -jnp.tanh(log_a) * (a * a + 1.0) if exact else 1.0 - a * a
    root = jnp.where(one_minus_a2 > 0.0, one_minus_a2 * lax.rsqrt(one_minus_a2), 0.0)
    bterm = root * (i * xc)
    return a, bterm


def _lru_kernel(z_ref, convw_ref, convb_ref, wlo_ref, whi_ref, bgate_ref, lam_ref, wg_ref, wu_ref, wd_ref,
                y_ref, cstate_ref, hlast_ref, wgub_ref, wdb_ref, tail_ref, carry_ref, *, tl):
    j = pl.program_id(1)

    @pl.when(j == 0)
    def _():
        tail_ref[...] = jnp.zeros_like(tail_ref)
        carry_ref[...] = jnp.zeros_like(carry_ref)

    wgub_ref[0, :, :D_EXPERT] = wg_ref[0].astype(BF16)
    wgub_ref[0, :, D_EXPERT:] = wu_ref[0].astype(BF16)
    wdb_ref[...] = wd_ref[...].astype(BF16)

    xb = z_ref[0, :, :LRU_WIDTH]
    gate = z_ref[0, :, LRU_WIDTH:]
    xc = convb_ref[...] + convw_ref[3:4, :] * xb
    rows8 = lax.broadcasted_iota(jnp.int32, (8, LRU_WIDTH), 0)
    tail = tail_ref[...]
    for back in (1, 2, 3):
        rolled = pltpu.roll(xb, back, axis=0)
        top = jnp.where(rows8 >= back, rolled[:8], pltpu.roll(tail, back, axis=0))
        shifted = jnp.concatenate([top, rolled[8:]], axis=0)
        xc = xc + convw_ref[3 - back:4 - back, :] * shifted
    tail_ref[...] = xb[tl - 8:, :]
    cstate_ref[0] = xb[tl - 8:, :]

    sp = _softplus(-lam_ref[...])
    a, bterm = _lru_gates(xc, wlo_ref[...], whi_ref[...], bgate_ref[...], sp, False)

    groups = tl // 8
    a = a.reshape(groups, 8, LRU_WIDTH)
    bterm = bterm.reshape(groups, 8, LRU_WIDTH)
    r8 = lax.broadcasted_iota(jnp.int32, (groups, 8, LRU_WIDTH), 1)
    s = 1
    while s < 8:
        a_sh = jnp.where(r8 >= s, pltpu.roll(a, s, axis=1), 1.0)
        b_sh = jnp.where(r8 >= s, pltpu.roll(bterm, s, axis=1), 0.0)
        bterm = a * b_sh + bterm
        a = a * a_sh
        s *= 2
    a_tot = jnp.broadcast_to(a[:, 7:8, :], (groups, 8, LRU_WIDTH))
    b_tot = jnp.broadcast_to(bterm[:, 7:8, :], (groups, 8, LRU_WIDTH))
    h_in = jnp.broadcast_to(carry_ref[7:8, :], (8, LRU_WIDTH))
    pieces = []
    for g in range(groups):
        pieces.append(a[g] * h_in + bterm[g])
        h_in = a_tot[g] * h_in + b_tot[g]
    h = jnp.concatenate(pieces, axis=0)
    carry_ref[...] = h_in
    hlast_ref[0] = h_in
    y_ref[0] = h * _gelu_tanh(gate)


def _lru(zin, conv_w, conv_b, wlo, whi, bgate, lam, w_gate, w_up, w_down, tl=512):
    b, t, _ = zin.shape
    steps = t // tl
    assert b * steps == N_EXPERTS
    d = D_MODEL
    kern = functools.partial(_lru_kernel, tl=tl)
    full = lambda shp: pl.BlockSpec(shp, lambda i, j: tuple(0 for _ in shp))
    per_step = lambda shp: pl.BlockSpec(shp, lambda i, j: (i * steps + j, 0, 0))
    return pl.pallas_call(
        kern,
        grid=(b, steps),
        in_specs=[pl.BlockSpec((1, tl, 2 * LRU_WIDTH), lambda i, j: (i, j, 0)),
                  full((CONV_WIDTH, LRU_WIDTH)), full((1, LRU_WIDTH)),
                  full((256, 512)), full((256, 512)), full((1, 2 * LRU_WIDTH)), full((1, LRU_WIDTH)),
                  per_step((1, d, D_EXPERT)), per_step((1, d, D_EXPERT)), per_step((1, D_EXPERT, d))],
        out_specs=[pl.BlockSpec((1, tl, LRU_WIDTH), lambda i, j: (i, j, 0)),
                   pl.BlockSpec((1, 8, LRU_WIDTH), lambda i, j: (i, 0, 0)),
                   pl.BlockSpec((1, 8, LRU_WIDTH), lambda i, j: (i, 0, 0)),
                   per_step((1, d, 2 * D_EXPERT)), per_step((1, D_EXPERT, d))],
        out_shape=[jax.ShapeDtypeStruct((b, t, LRU_WIDTH), F32),
                   jax.ShapeDtypeStruct((b, 8, LRU_WIDTH), F32),
                   jax.ShapeDtypeStruct((b, 8, LRU_WIDTH), F32),
                   jax.ShapeDtypeStruct((N_EXPERTS, d, 2 * D_EXPERT), BF16),
                   jax.ShapeDtypeStruct(w_down.shape, BF16)],
        scratch_shapes=[pltpu.VMEM((8, LRU_WIDTH), F32), pltpu.VMEM((8, LRU_WIDTH), F32)],
        compiler_params=_cparams(("arbitrary", "arbitrary")),
        name="prompt_rglru",
    )(zin, conv_w, conv_b, wlo, whi, bgate, lam, w_gate, w_up, w_down)


ATTN_BLOCKS = 16


def _attn_kernel(q_ref, k_ref, v_ref, sink_ref, o_ref, kprev_ref, vprev_ref):
    j = pl.program_id(1)

    @pl.when(j == 0)
    def _():
        kprev_ref[...] = jnp.zeros_like(kprev_ref)
        vprev_ref[...] = jnp.zeros_like(vprev_ref)

    blk = WINDOW
    lane = lax.broadcasted_iota(jnp.int32, (blk, LANES), 1)
    low = lane < HEAD_DIM
    qi = lax.broadcasted_iota(jnp.int32, (blk, 2 * blk), 0)
    sj = lax.broadcasted_iota(jnp.int32, (blk, 2 * blk), 1)
    rel = blk + qi - sj
    in_window = (rel >= 0) & (rel <= WINDOW)
    sink = sink_ref[...].reshape(N_HEADS, blk, 1)
    k_ext = jnp.concatenate([kprev_ref[...], k_ref[0]], axis=0)
    v_ext = jnp.concatenate([vprev_ref[...], v_ref[0]], axis=0)
    v_ext = jnp.concatenate([v_ext, jnp.ones_like(v_ext)], axis=-1)
    for n in range(ATTN_BLOCKS):
        q = q_ref[0, blk * n:blk * (n + 1), :]
        pieces = []
        for half in (0, 1):
            for c in range(4):
                qc = q[:, LANES * c:LANES * (c + 1)]
                pieces.append(jnp.where(low if half == 0 else ~low, qc, 0.0).astype(BF16))
        k_band = k_ext[blk * n:blk * (n + 2)]
        v_band = v_ext[blk * n:blk * (n + 2)]
        valid = in_window & ((sj >= blk) | (j > 0)) if n == 0 else in_window
        o8 = []
        for h in range(N_HEADS):
            s = _dot_nt(pieces[h], k_band, False) * ATTN_SCALE
            s = jnp.where(valid, s, -jnp.inf)
            m = jnp.maximum(jnp.max(s, axis=-1, keepdims=True), sink[h])
            ov = _dot(jnp.exp(s - m), v_band, False)
            den = ov[:, KV_WIDTH:] + jnp.exp(sink[h] - m)
            o8.append(ov[:, :KV_WIDTH] * (1.0 / den))
        cols = []
        for c in range(4):
            cols.append(jnp.where(low, o8[c], o8[c + 4]))
        o_ref[0, blk * n:blk * (n + 1), :] = jnp.concatenate(cols, axis=-1)
    kprev_ref[...] = k_ref[0, blk * (ATTN_BLOCKS - 1):, :]
    vprev_ref[...] = v_ref[0, blk * (ATTN_BLOCKS - 1):, :]


def _attn(qkv, sinks):
    b, t, _ = qkv.shape
    blk = WINDOW
    tq = blk * ATTN_BLOCKS
    sink_col = jnp.repeat(sinks.astype(F32), blk).reshape(N_HEADS * blk, 1)
    kcol = ATTN_WIDTH // KV_WIDTH
    return pl.pallas_call(
        _attn_kernel,
        grid=(b, t // tq),
        in_specs=[pl.BlockSpec((1, tq, ATTN_WIDTH), lambda i, j: (i, j, 0)),
                  pl.BlockSpec((1, tq, KV_WIDTH), lambda i, j: (i, j, kcol)),
                  pl.BlockSpec((1, tq, KV_WIDTH), lambda i, j: (i, j, kcol + 1)),
                  pl.BlockSpec((N_HEADS * blk, 1), lambda i, j: (0, 0))],
        out_specs=pl.BlockSpec((1, tq, ATTN_WIDTH), lambda i, j: (i, j, 0)),
        out_shape=jax.ShapeDtypeStruct((b, t, ATTN_WIDTH), F32),
        scratch_shapes=[pltpu.VMEM((blk, KV_WIDTH), BF16), pltpu.VMEM((blk, KV_WIDTH), BF16)],
        compiler_params=_cparams(("arbitrary", "arbitrary")),
        name="prompt_window_attention",
    )(qkv, qkv, qkv, sink_col)


def _route(h2, wr, br, exact):
    t = h2.shape[0]
    logits = _dot(h2, wr, exact) + br
    lane = lax.broadcasted_iota(jnp.int32, (t, ROUTE_LANES), 1).astype(F32)
    neg = -jnp.inf
    big = float(ROUTE_LANES)
    is_g = (lane >= N_EXPERTS) & (lane < N_EXPERTS + N_GROUPS)
    lg = jnp.where(is_g, logits, neg)
    mg = jnp.max(lg, axis=-1, keepdims=True)
    g_val = 1.0 / jnp.sum(jnp.where(is_g, jnp.exp(lg - mg), 0.0), axis=-1, keepdims=True)
    g_lane = jnp.min(jnp.where((lg == mg) & is_g, lane, big), axis=-1, keepdims=True)
    g_idx = g_lane - N_EXPERTS
    in_grp = (lane >= g_idx * EXPERTS_PER_GROUP) & (lane < (g_idx + 1.0) * EXPERTS_PER_GROUP)
    le = jnp.where(in_grp, logits, neg)
    me = jnp.max(le, axis=-1, keepdims=True)
    se = jnp.sum(jnp.where(in_grp, jnp.exp(le - me), 0.0), axis=-1, keepdims=True)
    l1 = jnp.min(jnp.where((le == me) & in_grp, lane, big), axis=-1, keepdims=True)
    rest = in_grp & (lane != l1)
    le2 = jnp.where(rest, le, neg)
    me2 = jnp.max(le2, axis=-1, keepdims=True)
    l2 = jnp.min(jnp.where((le2 == me2) & rest, lane, big), axis=-1, keepdims=True)
    v1 = 1.0 / se
    v2 = jnp.exp(me2 - me) / se
    tot = v1 + v2
    w1 = g_val * v1 / tot
    w2 = g_val * v2 / tot
    comb = jnp.where(lane == l1, w1, 0.0) + jnp.where(lane == l2, w2, 0.0)
    return (comb + jnp.where(lane == ROUTE_INFO, l1, 0.0) + jnp.where(lane == ROUTE_INFO + 1, l2, 0.0)
            + jnp.where(lane == ROUTE_INFO + 2, w1, 0.0) + jnp.where(lane == ROUTE_INFO + 3, w2, 0.0))


def _outproj_body(x, ylru, yatt, sh2, sc2, gt1, glru, gattn, wout, ln1g, ln1b, wr, br, exact):
    mixin = jnp.concatenate([_rms_norm(ylru, glru), _rms_norm(yatt, gattn)], axis=-1)
    mix = _dot(mixin, wout, exact)
    x1 = _layer_norm(DEEPNORM_ALPHA * x + (1.0 + gt1) * mix, ln1g, ln1b)
    h2 = x1 * (1.0 + sc2) + sh2
    return x1, _route(h2, wr, br, exact)


def _outproj_prompt_kernel(x_ref, ylru_ref, yatt_ref, mod_ref, glru_ref, gattn_ref, wout_ref,
                           ln1g_ref, ln1b_ref, wr_ref, br_ref, x1_ref, info_ref, cnt_ref, tri_ref, carry_ref,
                           *, tm, per_seq):
    i = pl.program_id(0)

    @pl.when(i == 0)
    def _():
        r = lax.broadcasted_iota(jnp.int32, (tm, tm), 0)
        c = lax.broadcasted_iota(jnp.int32, (tm, tm), 1)
        tri_ref[...] = jnp.where(c < r, 1.0, 0.0).astype(BF16)

    @pl.when(i % per_seq == 0)
    def _():
        carry_ref[...] = jnp.zeros_like(carry_ref)

    gt1 = mod_ref[0, 2:3, :]
    sh2 = mod_ref[0, 3:4, :]
    sc2 = mod_ref[0, 4:5, :]
    combs = []
    nsplit = 2
    for h in range(nsplit):
        rows = slice(h * (tm // nsplit), (h + 1) * (tm // nsplit))
        x1_h, comb_h = _outproj_body(x_ref[rows, :], ylru_ref[rows, :], yatt_ref[rows, :], sh2, sc2, gt1,
                                     glru_ref[...], gattn_ref[...], wout_ref[...], ln1g_ref[...], ln1b_ref[...],
                                     wr_ref[...], br_ref[...], False)
        x1_ref[rows, :] = x1_h
        combs.append(comb_h)
    comb = jnp.concatenate(combs, axis=0)
    lane = lax.broadcasted_iota(jnp.int32, (tm, ROUTE_LANES), 1).astype(F32)
    l1 = jnp.sum(jnp.where(lane == ROUTE_INFO, comb, 0.0), axis=-1, keepdims=True)
    l2 = jnp.sum(jnp.where(lane == ROUTE_INFO + 1, comb, 0.0), axis=-1, keepdims=True)
    o1 = lane == l1
    o2 = lane == l2
    onehot = jnp.where(o1 | o2, 1.0, 0.0)
    before = jnp.dot(tri_ref[...], onehot.astype(BF16), preferred_element_type=F32) + carry_ref[0:1, :]
    rank1 = jnp.sum(jnp.where(o1, before, 0.0), axis=-1, keepdims=True)
    rank2 = jnp.sum(jnp.where(o2, before, 0.0), axis=-1, keepdims=True)
    total = carry_ref[0:1, :] + jnp.sum(onehot, axis=0, keepdims=True)
    carry_ref[...] = jnp.broadcast_to(total, carry_ref.shape)
    cnt_ref[0] = jnp.broadcast_to(total, (8, ROUTE_LANES))
    info = (comb + jnp.where(lane == ROUTE_INFO + 4, rank1, 0.0) + jnp.where(lane == ROUTE_INFO + 5, rank2, 0.0))
    info_ref[0] = jnp.transpose(info)[ROUTE_INFO:ROUTE_INFO + 8, :]


OUTPROJ_TILE = 1024


def _outproj_prompt(x2d, ylru2d, yatt2d, modp, glru, gattn, wout_bf16, ln1g, ln1b, wr, br, tm=OUTPROJ_TILE):
    n, d = x2d.shape
    per_seq = SEQ // tm
    full = lambda shp: pl.BlockSpec(shp, lambda i: tuple(0 for _ in shp))
    kern = functools.partial(_outproj_prompt_kernel, tm=tm, per_seq=per_seq)
    return pl.pallas_call(
        kern,
        grid=(n // tm,),
        in_specs=[pl.BlockSpec((tm, d), lambda i: (i, 0)),
                  pl.BlockSpec((tm, LRU_WIDTH), lambda i: (i, 0)),
                  pl.BlockSpec((tm, ATTN_WIDTH), lambda i: (i, 0)),
                  pl.BlockSpec((1, 6, d), lambda i: (i // per_seq, 0, 0)),
                  full((1, LRU_WIDTH)), full((1, ATTN_WIDTH)), full((d, d)),
                  full((1, d)), full((1, d)), full((d, ROUTE_LANES)), full((1, ROUTE_LANES))],
        out_specs=[pl.BlockSpec((tm, d), lambda i: (i, 0)),
                   pl.BlockSpec((1, 8, tm), lambda i: (i, 0, 0)),
                   pl.BlockSpec((1, 8, ROUTE_LANES), lambda i: (i // per_seq, 0, 0))],
        out_shape=[jax.ShapeDtypeStruct((n, d), F32),
                   jax.ShapeDtypeStruct((n // tm, 8, tm), F32),
                   jax.ShapeDtypeStruct((n // SEQ, 8, ROUTE_LANES), F32)],
        scratch_shapes=[pltpu.VMEM((tm, tm), BF16), pltpu.VMEM((8, ROUTE_LANES), F32)],
        compiler_params=_cparams(("arbitrary",)),
        name="prompt_outproj_ln_route",
    )(x2d, ylru2d, yatt2d, modp, glru, gattn, wout_bf16, ln1g, ln1b, wr, br)


DENSE_EXPERTS_PER_STEP = 4


def _moe_kernel(x1_ref, comb_ref, sh2_ref, sc2_ref, gt2_ref, wgu_ref, wd_ref, ln2g_ref, ln2b_ref,
                o_ref, h2_ref, acc_ref):
    g = pl.program_id(1)

    @pl.when(g == 0)
    def _():
        h2_ref[...] = (x1_ref[...] * (1.0 + sc2_ref[...]) + sh2_ref[...]).astype(BF16)
        acc_ref[...] = jnp.zeros_like(acc_ref)

    h2 = h2_ref[...]
    comb = comb_ref[...]
    lane = lax.broadcasted_iota(jnp.int32, comb.shape, 1)
    part = None
    for k in range(DENSE_EXPERTS_PER_STEP):
        au = jnp.dot(h2, wgu_ref[k], preferred_element_type=F32)
        c_e = jnp.sum(jnp.where(lane == g * DENSE_EXPERTS_PER_STEP + k, comb, 0.0), axis=-1, keepdims=True)
        z = _silu(au[:, :D_EXPERT]) * au[:, D_EXPERT:] * c_e
        y = jnp.dot(z.astype(BF16), wd_ref[k], preferred_element_type=F32)
        part = y if part is None else part + y
    acc_ref[...] += part

    @pl.when(g == N_EXPERTS // DENSE_EXPERTS_PER_STEP - 1)
    def _():
        o_ref[...] = _layer_norm(DEEPNORM_ALPHA * x1_ref[...] + (1.0 + gt2_ref[...]) * acc_ref[...],
                                 ln2g_ref[...], ln2b_ref[...])


def _moe_dense(x1, comb, mod, wgu_bf16, wd_bf16, ln2g, ln2b, tm):
    n, d = x1.shape
    eg = DENSE_EXPERTS_PER_STEP
    mspec = lambda k: pl.BlockSpec((tm, d), lambda i, e: (i, k))
    full = lambda shp: pl.BlockSpec(shp, lambda i, e: tuple(0 for _ in shp))
    return pl.pallas_call(
        _moe_kernel,
        grid=(n // tm, N_EXPERTS // eg),
        in_specs=[pl.BlockSpec((tm, d), lambda i, e: (i, 0)),
                  pl.BlockSpec((tm, ROUTE_LANES), lambda i, e: (i, 0)),
                  mspec(3), mspec(4), mspec(5),
                  pl.BlockSpec((eg, d, 2 * D_EXPERT), lambda i, e: (e, 0, 0)),
                  pl.BlockSpec((eg, D_EXPERT, d), lambda i, e: (e, 0, 0)),
                  full((1, d)), full((1, d))],
        out_specs=pl.BlockSpec((tm, d), lambda i, e: (i, 0)),
        out_shape=jax.ShapeDtypeStruct((n, d), F32),
        scratch_shapes=[pltpu.VMEM((tm, d), BF16), pltpu.VMEM((tm, d), F32)],
        compiler_params=_cparams(("arbitrary", "arbitrary")),
        name="moe_dense_ln",
    )(x1, comb, mod, mod, mod, wgu_bf16, wd_bf16, ln2g, ln2b)


RB_SUB = 512
RB_NSUB = SEQ // RB_SUB
RB_CHUNK = 96
RB_NCHUNK = -(-2 * SEQ // RB_CHUNK)
RB_PITCH = RB_CHUNK + 8
RB_SPITCH = RB_SUB + 8
RB_GROUP = 4
RB_WSLOTS = 4


def _rb_kernel(cnt_ref, x1_ref, mod_ref, offs_ref, wts_ref, wgu_hbm, wd_hbm, ln2g_ref, ln2b_ref,
               o_ref, buf_ref, stage_ref, wgu_buf, wd_buf, start_ref, sem):
    b = pl.program_id(0)
    s = pl.program_id(1)

    @pl.when(s == 0)
    def _starts():
        def body(e, run):
            start_ref[e] = run
            return run + cnt_ref[b, e]
        lax.fori_loop(0, N_EXPERTS, body, jnp.int32(0))
        buf_ref[(RB_NCHUNK - 1) * 8 * RB_PITCH:(RB_NCHUNK + RB_GROUP) * 8 * RB_PITCH, :] = jnp.zeros(
            ((RB_GROUP + 1) * 8 * RB_PITCH, LANES), F32)

    @pl.when(s < RB_NSUB)
    def _dispatch():
        sh2 = mod_ref[0, 3:4, :]
        sc2 = mod_ref[0, 4:5, :]
        h2 = x1_ref[...] * (1.0 + sc2) + sh2
        for j in range(8):
            stage_ref[RB_SPITCH * j:RB_SPITCH * j + RB_SUB, :] = h2[:, LANES * j:LANES * (j + 1)]

        for t in range(RB_SUB):
            slab = stage_ref[pl.ds(t, 8, stride=RB_SPITCH), :]
            for a in range(2):
                buf_ref[pl.ds(offs_ref[0, a, t], 8, stride=RB_PITCH), :] = slab

    @pl.when(s == RB_NSUB)
    def _experts():
        def copies(e, slot):
            return (pltpu.make_async_copy(wgu_hbm.at[e], wgu_buf.at[slot], sem.at[slot, 0]),
                    pltpu.make_async_copy(wd_hbm.at[e], wd_buf.at[slot], sem.at[slot, 1]))

        def run_expert(e, slot):
            lo_row = start_ref[e]
            hi_row = lo_row + cnt_ref[b, e]

            c_lo = lax.div(lo_row, RB_CHUNK)
            c_hi = lax.div(hi_row + (RB_CHUNK - 1), RB_CHUNK)
            row = lax.broadcasted_iota(jnp.int32, (RB_CHUNK, 1), 0)

            def load(c):
                base = pl.multiple_of(c * (8 * RB_PITCH), 8)
                return [buf_ref[pl.ds(base + RB_PITCH * j, RB_CHUNK), :] for j in range(8)]

            def store(c, tiles, y):
                base = pl.multiple_of(c * (8 * RB_PITCH), 8)
                mine = (row >= lo_row - c * RB_CHUNK) & (row < hi_row - c * RB_CHUNK)
                for j in range(8):
                    buf_ref[pl.ds(base + RB_PITCH * j, RB_CHUNK), :] = jnp.where(
                        mine, y[:, LANES * j:LANES * (j + 1)], tiles[j])

            def group(first, nchunks):
                cs = [first]
                for k in range(1, nchunks):
                    cs.append(jnp.where(first + k < c_hi, first + k, RB_NCHUNK + k))
                tiles = [load(c) for c in cs]
                x = jnp.concatenate([jnp.concatenate(t, axis=-1) for t in tiles], axis=0).astype(BF16)
                au = jnp.dot(x, wgu_buf[slot], preferred_element_type=F32)
                z = (_silu(au[:, :D_EXPERT]) * au[:, D_EXPERT:]).astype(BF16)
                y = jnp.dot(z, wd_buf[slot], preferred_element_type=F32)
                for k, c in enumerate(cs):
                    store(c, tiles[k], y[RB_CHUNK * k:RB_CHUNK * (k + 1)])

            span = c_hi - c_lo

            @pl.when(span < RB_GROUP)
            def _():
                group(c_lo, RB_GROUP - 1)

            @pl.when(span >= RB_GROUP)
            def _():
                def body(i, carry):
                    group(c_lo + RB_GROUP * i, RB_GROUP)
                    return carry
                lax.fori_loop(0, lax.div(span + (RB_GROUP - 1), RB_GROUP), body, 0)

        for e in range(RB_WSLOTS - 1):
            for c in copies(e, e):
                c.start()

        def ring_body(i, carry):
            for k in range(RB_WSLOTS):
                e = RB_WSLOTS * i + k
                ahead = e + RB_WSLOTS - 1

                @pl.when(ahead < N_EXPERTS)
                def _():
                    for c in copies(ahead, (k + RB_WSLOTS - 1) % RB_WSLOTS):
                        c.start()
                for c in copies(e, k):
                    c.wait()
                run_expert(e, k)
            return carry
        lax.fori_loop(0, N_EXPERTS // RB_WSLOTS, ring_body, 0)

    @pl.when(s > RB_NSUB)
    def _combine():
        for t in range(RB_SUB):
            acc = None
            for a in range(2):
                term = wts_ref[0, a, t] * buf_ref[pl.ds(offs_ref[0, a, t], 8, stride=RB_PITCH), :]
                acc = term if acc is None else acc + term
            stage_ref[pl.ds(t, 8, stride=RB_SPITCH), :] = acc
        gt2 = mod_ref[0, 5:6, :]
        f = jnp.concatenate([stage_ref[RB_SPITCH * j:RB_SPITCH * j + RB_SUB, :] for j in range(8)], axis=-1)
        o_ref[...] = _layer_norm(DEEPNORM_ALPHA * x1_ref[...] + (1.0 + gt2) * f, ln2g_ref[...], ln2b_ref[...])


def _rb_retile(a):
    tiles, two, t = a.shape
    return a.reshape(tiles, two, t // RB_SUB, RB_SUB).transpose(0, 2, 1, 3).reshape(-1, two, RB_SUB)


def _rb_offsets(cnt, e12, rank12):
    start = jnp.cumsum(cnt, axis=-1) - cnt
    start_t = jnp.repeat(start, e12.shape[0] // cnt.shape[0], axis=0)[:, None, None, :]
    hit = e12[..., None] == jnp.arange(N_EXPERTS, dtype=jnp.int32)
    p = jnp.sum(jnp.where(hit, start_t, 0), axis=-1) + rank12
    return (p // RB_CHUNK) * (8 * RB_PITCH) + p % RB_CHUNK


def _rb_moe(x1, modp, cnt, offs, wts, wgu_bf16, wd_bf16, ln2g, ln2b):
    n, d = x1.shape
    bsz = n // SEQ
    nsteps = 2 * RB_NSUB + 1

    def sub_index(s):
        return jnp.where(s < RB_NSUB, s, jnp.where(s == RB_NSUB, RB_NSUB - 1, s - RB_NSUB - 1))

    def tile_map(b, s, cnt_r):
        return (b * RB_NSUB + sub_index(s), 0)

    def tile_map3(b, s, cnt_r):
        return (b * RB_NSUB + sub_index(s), 0, 0)

    def out_map(b, s, cnt_r):
        return (b * RB_NSUB + jnp.maximum(s - RB_NSUB - 1, 0), 0)

    const = lambda shp: pl.BlockSpec(shp, lambda b, s, cnt_r: tuple(0 for _ in shp))
    anyspec = pl.BlockSpec(memory_space=pl.ANY)
    grid_spec = pltpu.PrefetchScalarGridSpec(
        num_scalar_prefetch=1,
        grid=(bsz, nsteps),
        in_specs=[pl.BlockSpec((RB_SUB, d), tile_map),
                  pl.BlockSpec((1, 6, d), lambda b, s, cnt_r: (b, 0, 0)),
                  pl.BlockSpec((1, 2, RB_SUB), tile_map3, memory_space=pltpu.SMEM),
                  pl.BlockSpec((1, 2, RB_SUB), tile_map3, memory_space=pltpu.SMEM),
                  anyspec, anyspec,
                  const((1, d)), const((1, d))],
        out_specs=pl.BlockSpec((RB_SUB, d), out_map),
        scratch_shapes=[pltpu.VMEM(((RB_NCHUNK + RB_GROUP) * 8 * RB_PITCH, LANES), F32),
                        pltpu.VMEM((8 * RB_SPITCH, LANES), F32),
                        pltpu.VMEM((RB_WSLOTS, d, 2 * D_EXPERT), BF16),
                        pltpu.VMEM((RB_WSLOTS, D_EXPERT, d), BF16),
                        pltpu.SMEM((N_EXPERTS,), jnp.int32),
                        pltpu.SemaphoreType.DMA((RB_WSLOTS, 2))])
    return pl.pallas_call(
        _rb_kernel,
        grid_spec=grid_spec,
        out_shape=jax.ShapeDtypeStruct((n, d), F32),
        compiler_params=_cparams(("arbitrary", "arbitrary")),
        name="moe_routed_ln",
    )(cnt, x1, modp, offs, wts, wgu_bf16, wd_bf16, ln2g, ln2b)


def _sample_in_kernel(x_ref, mod_ref, win_ref, ctx_ref, h0_ref, convw_ref, convb_ref,
                      wlo_ref, whi_ref, bgate_ref, lam_ref,
                      ylru_ref, q_ref, k_ref, v_ref, cstate_ref, hnew_ref):
    sh1 = mod_ref[0:DEC_BATCH, 0:D_MODEL]
    sc1 = mod_ref[0:DEC_BATCH, D_MODEL:2 * D_MODEL]
    h = x_ref[...] * (1.0 + sc1) + sh1
    z = _dot(h, win_ref[...], True)
    xb = z[:, :LRU_WIDTH]
    gate = z[:, LRU_WIDTH:2 * LRU_WIDTH]
    c0 = ctx_ref[:, 0, :]
    c1 = ctx_ref[:, 1, :]
    c2 = ctx_ref[:, 2, :]
    xc = (convb_ref[...] + convw_ref[0:1, :] * c0 + convw_ref[1:2, :] * c1
          + convw_ref[2:3, :] * c2 + convw_ref[3:4, :] * xb)
    cstate_ref[:, 0, :] = c1
    cstate_ref[:, 1, :] = c2
    cstate_ref[:, 2, :] = xb
    sp = _softplus(-lam_ref[...])
    a, bterm = _lru_gates(xc, wlo_ref[...], whi_ref[...], bgate_ref[...], sp, True)
    hn = a * h0_ref[...] + bterm
    hnew_ref[...] = hn
    ylru_ref[...] = hn * _gelu_tanh(gate)
    low = lax.broadcasted_iota(jnp.int32, (DEC_BATCH, LANES), 1) < HEAD_DIM
    for c in range(4):
        qc = z[:, 2 * LRU_WIDTH + LANES * c:2 * LRU_WIDTH + LANES * (c + 1)]
        q_ref[pl.ds(c, DEC_BATCH, stride=N_HEADS), :] = jnp.where(low, qc, 0.0)
        q_ref[pl.ds(c + 4, DEC_BATCH, stride=N_HEADS), :] = jnp.where(low, 0.0, qc)
    k_ref[...] = z[:, 2 * LRU_WIDTH + ATTN_WIDTH:2 * LRU_WIDTH + ATTN_WIDTH + KV_WIDTH]
    v_ref[...] = z[:, 2 * LRU_WIDTH + ATTN_WIDTH + KV_WIDTH:]


def _sample_in(x, mod, w_in_p, ctx, h0, conv_w, conv_b, wlo, whi, bgate, lam):
    n = DEC_BATCH
    outs = [jax.ShapeDtypeStruct((n, LRU_WIDTH), F32),
            jax.ShapeDtypeStruct((n * N_HEADS, LANES), F32),
            jax.ShapeDtypeStruct((n, KV_WIDTH), F32),
            jax.ShapeDtypeStruct((n, KV_WIDTH), F32),
            jax.ShapeDtypeStruct((n, CONV_WIDTH - 1, LRU_WIDTH), F32),
            jax.ShapeDtypeStruct((n, LRU_WIDTH), F32)]
    return pl.pallas_call(
        _sample_in_kernel,
        out_shape=outs,
        compiler_params=pltpu.CompilerParams(vmem_limit_bytes=VMEM_LIMIT),
        name="sample_inproj_rglru",
    )(x, mod, w_in_p, ctx, h0, conv_w, conv_b, wlo, whi, bgate, lam)


def _sample_attn_kernel(q_ref, kn_ref, vn_ref, ck_ref, cv_ref, sink_ref, y_ref, nk_ref, nv_ref, *, bb):
    rows = lax.broadcasted_iota(jnp.int32, (WINDOW, KV_WIDTH), 0)
    nh = N_HEADS
    q_all = q_ref[...].reshape(bb * nh, LANES)
    kcat = ck_ref[...].reshape(bb * WINDOW, KV_WIDTH)
    vcat = cv_ref[...].reshape(bb * WINDOW, KV_WIDTH)
    kn_rep = jnp.broadcast_to(kn_ref[...][:, None, :], (bb, nh, KV_WIDTH)).reshape(bb * nh, KV_WIDTH)
    vn_rep = jnp.broadcast_to(vn_ref[...][:, None, :], (bb, nh, KV_WIDTH)).reshape(bb * nh, KV_WIDTH)
    sink = jnp.concatenate([sink_ref[...]] * bb, axis=0)
    s_full = _dot_nt(q_all, kcat, True)
    s = jnp.concatenate([s_full[nh * b:nh * (b + 1), WINDOW * b:WINDOW * (b + 1)] for b in range(bb)],
                        axis=0) * ATTN_SCALE
    s_self = jnp.sum(q_all * kn_rep, axis=-1, keepdims=True) * ATTN_SCALE
    m = jnp.maximum(jnp.maximum(jnp.max(s, axis=-1, keepdims=True), s_self), sink)
    e = jnp.exp(s - m)
    e_self = jnp.exp(s_self - m)
    den = jnp.sum(e, axis=-1, keepdims=True) + e_self + jnp.exp(sink - m)
    inv = 1.0 / den
    p = e * inv
    zero = jnp.zeros((nh, WINDOW), F32)
    p_wide = jnp.concatenate(
        [jnp.concatenate([p[nh * b:nh * (b + 1)] if c == b else zero for c in range(bb)], axis=-1)
         for b in range(bb)], axis=0)
    o = _dot(p_wide, vcat, True) + (e_self * inv) * vn_rep
    y_ref[...] = o.reshape(bb, nh, LANES)
    for b in range(bb):
        nk_ref[b] = jnp.where(rows == WINDOW - 1, kn_ref[b:b + 1, :], pltpu.roll(ck_ref[b], WINDOW - 1, axis=0))
        nv_ref[b] = jnp.where(rows == WINDOW - 1, vn_ref[b:b + 1, :], pltpu.roll(cv_ref[b], WINDOW - 1, axis=0))


def _sample_attn(q3, kn, vn, cache_k, cache_v, sinks, bb=16):
    n = DEC_BATCH
    kern = functools.partial(_sample_attn_kernel, bb=bb)
    return pl.pallas_call(
        kern,
        grid=(n // bb,),
        in_specs=[pl.BlockSpec((bb, N_HEADS, LANES), lambda i: (i, 0, 0)),
                  pl.BlockSpec((bb, KV_WIDTH), lambda i: (i, 0)),
                  pl.BlockSpec((bb, KV_WIDTH), lambda i: (i, 0)),
                  pl.BlockSpec((bb, WINDOW, KV_WIDTH), lambda i: (i, 0, 0)),
                  pl.BlockSpec((bb, WINDOW, KV_WIDTH), lambda i: (i, 0, 0)),
                  pl.BlockSpec((N_HEADS, 1), lambda i: (0, 0))],
        out_specs=[pl.BlockSpec((bb, N_HEADS, LANES), lambda i: (i, 0, 0)),
                   pl.BlockSpec((bb, WINDOW, KV_WIDTH), lambda i: (i, 0, 0)),
                   pl.BlockSpec((bb, WINDOW, KV_WIDTH), lambda i: (i, 0, 0))],
        out_shape=[jax.ShapeDtypeStruct((n, N_HEADS, LANES), F32),
                   jax.ShapeDtypeStruct((n, WINDOW, KV_WIDTH), F32),
                   jax.ShapeDtypeStruct((n, WINDOW, KV_WIDTH), F32)],
        compiler_params=_cparams(("arbitrary",)),
        name="sample_cache_attention",
    )(q3, kn, vn, cache_k, cache_v, sinks.reshape(N_HEADS, 1))


def _sample_out_kernel(x_ref, ylru_ref, yatt_ref, mod_ref, glru_ref, gattn_ref, wout_ref,
                       ln1g_ref, ln1b_ref, wr_ref, br_ref, x1_ref, comb_ref):
    low = lax.broadcasted_iota(jnp.int32, (DEC_BATCH, LANES), 1) < HEAD_DIM
    yatt = jnp.concatenate(
        [jnp.where(low, yatt_ref[pl.ds(c, DEC_BATCH, stride=N_HEADS), :],
                   yatt_ref[pl.ds(c + 4, DEC_BATCH, stride=N_HEADS), :]) for c in range(4)], axis=-1)
    gt1 = mod_ref[0:DEC_BATCH, 2 * D_MODEL:3 * D_MODEL]
    sh2 = mod_ref[0:DEC_BATCH, 3 * D_MODEL:4 * D_MODEL]
    sc2 = mod_ref[0:DEC_BATCH, 4 * D_MODEL:5 * D_MODEL]
    x1, comb = _outproj_body(x_ref[...], ylru_ref[...], yatt, sh2, sc2, gt1,
                             glru_ref[...], gattn_ref[...], wout_ref[...], ln1g_ref[...], ln1b_ref[...],
                             wr_ref[...], br_ref[...], True)
    x1_ref[...] = x1
    comb_ref[...] = comb


def _sample_out(x, ylru, yatt2d, mod, glru, gattn, wout_p, ln1g, ln1b, wr, br):
    n = DEC_BATCH
    return pl.pallas_call(
        _sample_out_kernel,
        out_shape=[jax.ShapeDtypeStruct((n, D_MODEL), F32), jax.ShapeDtypeStruct((n, ROUTE_LANES), F32)],
        compiler_params=pltpu.CompilerParams(vmem_limit_bytes=VMEM_LIMIT),
        name="sample_outproj_ln_route",
    )(x, ylru, yatt2d, mod, glru, gattn, wout_p, ln1g, ln1b, wr, br)


def _block_diag_halves(w_a, w_x):
    def bd(w4):
        eye = jnp.eye(4, dtype=w4.dtype)
        return (w4[:, :, None, :] * eye[:, None, :, None]).reshape(256, 256)
    lo = jnp.concatenate([bd(w_a[:4]), bd(w_x[:4])], axis=1)
    hi = jnp.concatenate([bd(w_a[4:]), bd(w_x[4:])], axis=1)
    return lo, hi


PREP_ROWS = 2 * LANES


def _prep_kernel(win_ref, wout_ref, winp_ref, winb_ref, woutp_ref, woutb_ref):
    i = pl.program_id(0)
    q0 = 2 * LRU_WIDTH
    z = win_ref[...]
    low = lax.broadcasted_iota(jnp.int32, (PREP_ROWS, LANES), 1) < HEAD_DIM
    swap = lambda t: pltpu.roll(t, HEAD_DIM, axis=1)
    a0, a1, b0, b1 = (z[:, q0 + LANES * c:q0 + LANES * (c + 1)] for c in range(4))
    tiles = [jnp.where(low, a0, swap(b0)), jnp.where(low, swap(a0), b0),
             jnp.where(low, a1, swap(b1)), jnp.where(low, swap(a1), b1)]
    zp = jnp.concatenate([z[:, :q0]] + tiles + [z[:, q0 + ATTN_WIDTH:]], axis=1)
    winp_ref[...] = zp
    winb_ref[...] = zp.astype(BF16)
    for k in range(D_MODEL // PREP_ROWS):
        @pl.when(i == k)
        def _(k=k):
            r0 = k * PREP_ROWS
            if r0 < LRU_WIDTH:
                blk = wout_ref[r0:r0 + PREP_ROWS, :]
            else:
                parts = []
                for c in range((r0 - LRU_WIDTH) // LANES, (r0 - LRU_WIDTH) // LANES + PREP_ROWS // LANES):
                    for head in (c, c + 4):
                        h0 = LRU_WIDTH + HEAD_DIM * head
                        parts.append(wout_ref[h0:h0 + HEAD_DIM, :])
                blk = jnp.concatenate(parts, axis=0)
            woutp_ref[...] = blk
            woutb_ref[...] = blk.astype(BF16)


def _prep_weights(w_in0, w_out0):
    d = D_MODEL
    assert LRU_WIDTH % PREP_ROWS == 0 and N_HEADS == 8 and 2 * HEAD_DIM == LANES
    return pl.pallas_call(
        _prep_kernel,
        grid=(d // PREP_ROWS,),
        in_specs=[pl.BlockSpec((PREP_ROWS, IN_WIDTH), lambda i: (i, 0)),
                  pl.BlockSpec((d, d), lambda i: (0, 0))],
        out_specs=[pl.BlockSpec((PREP_ROWS, IN_WIDTH), lambda i: (i, 0)),
                   pl.BlockSpec((PREP_ROWS, IN_WIDTH), lambda i: (i, 0)),
                   pl.BlockSpec((PREP_ROWS, d), lambda i: (i, 0)),
                   pl.BlockSpec((PREP_ROWS, d), lambda i: (i, 0))],
        out_shape=[jax.ShapeDtypeStruct((d, IN_WIDTH), F32), jax.ShapeDtypeStruct((d, IN_WIDTH), BF16),
                   jax.ShapeDtypeStruct((d, d), F32), jax.ShapeDtypeStruct((d, d), BF16)],
        compiler_params=_cparams(("arbitrary",)),
        name="weight_layout",
    )(w_in0, w_out0)


def kernel(x_prompt, x_sample, c_prompt, c_sample, state_conv, state_h, cache_k, cache_v, w_ada, b_ada, w_in,
           conv_w, conv_b, w_rg_a, b_rg_a, w_rg_x, b_rg_x, lru_lambda, sinks, g_lru, g_attn, w_out, ln1_g, ln1_b,
           w_group, b_group, w_router, b_router, w_gate, w_up, w_down, ln2_g, ln2_b):
    d = D_MODEL
    w_in_p, w_in_b, w_out_p, w_out_b = _prep_weights(w_in[0], w_out[0])
    g_attn_p = jnp.transpose(g_attn[0].reshape(2, N_HEADS // 2, HEAD_DIM), (1, 0, 2)).reshape(1, -1)
    glru = g_lru[0].reshape(1, -1)
    wlo, whi = _block_diag_halves(w_rg_a[0], w_rg_x[0])
    bgate = jnp.concatenate([b_rg_a[0].reshape(-1), b_rg_x[0].reshape(-1)]).reshape(1, -1)
    lam = lru_lambda[0].reshape(1, -1)
    convw = conv_w[0]
    convb = conv_b[0].reshape(1, -1)
    ln1g, ln1b = ln1_g[0].reshape(1, -1), ln1_b[0].reshape(1, -1)
    ln2g, ln2b = ln2_g[0].reshape(1, -1), ln2_b[0].reshape(1, -1)
    wr = jnp.concatenate([jnp.transpose(w_router[0], (1, 0, 2)).reshape(d, N_EXPERTS), w_group[0],
                          jnp.zeros((d, ROUTE_LANES - N_EXPERTS - N_GROUPS), F32)], axis=1)
    br = jnp.concatenate([b_router[0].reshape(-1), b_group[0],
                          jnp.zeros((ROUTE_LANES - N_EXPERTS - N_GROUPS,), F32)]).reshape(1, -1)
    sink_p = sinks[0]

    c_all = jnp.concatenate([c_sample, c_prompt, jnp.zeros((8 - BATCH, d), F32)], axis=0)
    mod = _ada(c_all, w_ada[0], b_ada[0])
    modp = mod[DEC_BATCH:DEC_BATCH + BATCH].reshape(BATCH, 6, d)

    zlru, zqkv, kvlast = _inproj(x_prompt, modp, w_in_b)
    ylru, cstate8, hlast8, wgu_b, wd_b = _lru(zlru, convw, convb, wlo.astype(BF16), whi.astype(BF16), bgate, lam,
                                                   w_gate[0], w_up[0], w_down[0])
    yatt = _attn(zqkv, sink_p)
    n_p = BATCH * SEQ
    x1_p, info, cntf = _outproj_prompt(x_prompt.reshape(n_p, d), ylru.reshape(n_p, LRU_WIDTH),
                                       yatt.reshape(n_p, ATTN_WIDTH), modp, glru, g_attn_p, w_out_b,
                                       ln1g, ln1b, wr, br, tm=OUTPROJ_TILE)
    cnt = cntf[:, 0, :N_EXPERTS].astype(jnp.int32)
    offs = _rb_retile(_rb_offsets(cnt, info[:, 0:2].astype(jnp.int32), info[:, 4:6].astype(jnp.int32)))
    y_p = _rb_moe(x1_p, modp, cnt, offs, _rb_retile(info[:, 2:4]), wgu_b, wd_b, ln2g, ln2b)

    ylru_s, q2d, kn, vn, cstate_s, hnew_s = _sample_in(
        x_sample.reshape(DEC_BATCH, d), mod, w_in_p, state_conv[0], state_h[0],
        convw, convb, wlo, whi, bgate, lam)
    yatt3, newk, newv = _sample_attn(q2d.reshape(DEC_BATCH, N_HEADS, LANES), kn, vn,
                                     cache_k[0].reshape(DEC_BATCH, WINDOW, KV_WIDTH),
                                     cache_v[0].reshape(DEC_BATCH, WINDOW, KV_WIDTH), sink_p)
    x1_s, comb_s = _sample_out(x_sample.reshape(DEC_BATCH, d), ylru_s, yatt3.reshape(DEC_BATCH * N_HEADS, LANES),
                               mod, glru, g_attn_p, w_out_p, ln1g, ln1b, wr, br)
    y_s = _moe_dense(x1_s, comb_s, mod, wgu_b, wd_b, ln2g, ln2b, DEC_BATCH)

    return (y_p.reshape(BATCH, SEQ, d),
            y_s.reshape(DEC_BATCH, 1, d),
            cstate8[:, 5:8][None],
            hlast8[:, 7][None],
            kvlast[:, :, :KV_WIDTH].reshape(1, BATCH, WINDOW, N_KV_HEADS, HEAD_DIM),
            kvlast[:, :, KV_WIDTH:].reshape(1, BATCH, WINDOW, N_KV_HEADS, HEAD_DIM),
            cstate_s[None],
            hnew_s[None],
            newk.reshape(1, DEC_BATCH, WINDOW, N_KV_HEADS, HEAD_DIM),
            newv.reshape(1, DEC_BATCH, WINDOW, N_KV_HEADS, HEAD_DIM))
```

```python
import functools

import jax
import jax.numpy as jnp
import numpy as np
from jax import lax
from jax.experimental import pallas as pl
from jax.experimental.pallas import tpu as pltpu

F32 = jnp.float32
BF16 = jnp.bfloat16
HIGHEST = lax.Precision.HIGHEST

D_MODEL = 1024
BATCH = 4
SEQ = 4096
DEC_BATCH = 128
LRU_WIDTH = 512
LRU_BLOCKS = 8
LRU_BLOCK = 64
CONV_WIDTH = 4
LRU_C = 8.0
N_HEADS = 8
N_KV_HEADS = 2
HEAD_DIM = 64
ATTN_WIDTH = 512
KV_WIDTH = 128
WINDOW = 128
IN_WIDTH = 2 * LRU_WIDTH + ATTN_WIDTH + 2 * KV_WIDTH
N_GROUPS = 4
EXPERTS_PER_GROUP = 8
N_EXPERTS = 32
D_EXPERT = 256
DEEPNORM_ALPHA = 2.0 ** 0.25
LN_EPS = 1e-5
RMS_EPS = 1e-6
ATTN_SCALE = HEAD_DIM ** -0.5

LANES = 128
ROUTE_LANES = 128
ROUTE_INFO = 40
VMEM_LIMIT = 56 * 1024 * 1024

HEAD_PERM = np.concatenate(
    [np.concatenate([np.arange(64 * c, 64 * c + 64), np.arange(64 * (c + 4), 64 * (c + 4) + 64)])
     for c in range(4)])


def _cparams(sem):
    return pltpu.CompilerParams(dimension_semantics=sem, vmem_limit_bytes=VMEM_LIMIT)


def _dot(a, b, exact):
    if exact:
        return jnp.dot(a, b, precision=HIGHEST, preferred_element_type=F32)
    return jnp.dot(a.astype(BF16), b.astype(BF16), preferred_element_type=F32)


def _dot_nt(a, b, exact):
    dn = (((1,), (1,)), ((), ()))
    if exact:
        return lax.dot_general(a, b, dn, precision=HIGHEST, preferred_element_type=F32)
    return lax.dot_general(a.astype(BF16), b.astype(BF16), dn, preferred_element_type=F32)


def _sigmoid(x):
    return 1.0 / (1.0 + jnp.exp(-x))


def _silu(x):
    return x * _sigmoid(x)


def _gelu_tanh(x):
    return 0.5 * x * (1.0 + jnp.tanh(np.sqrt(2.0 / np.pi).astype(np.float32) * (x + 0.044715 * (x * x * x))))


def _softplus(x):
    return jnp.maximum(x, 0.0) + jnp.log1p(jnp.exp(-jnp.abs(x)))


def _layer_norm(x, g, b):
    mu = jnp.mean(x, axis=-1, keepdims=True)
    xc = x - mu
    var = jnp.mean(xc * xc, axis=-1, keepdims=True)
    return xc * lax.rsqrt(var + LN_EPS) * g + b


def _rms_norm(x, g):
    return x * lax.rsqrt(jnp.mean(x * x, axis=-1, keepdims=True) + RMS_EPS) * g


def _ada_kernel(c_ref, w_ref, b_ref, o_ref):
    o_ref[...] = _dot(_silu(c_ref[...]), w_ref[...], True) + b_ref[...]


def _ada(c_all, w_ada, b_ada):
    rows = c_all.shape[0]
    bn = 1024
    return pl.pallas_call(
        _ada_kernel,
        grid=(6 * D_MODEL // bn,),
        in_specs=[pl.BlockSpec((rows, D_MODEL), lambda j: (0, 0)),
                  pl.BlockSpec((D_MODEL, bn), lambda j: (0, j)),
                  pl.BlockSpec((1, bn), lambda j: (0, j))],
        out_specs=pl.BlockSpec((rows, bn), lambda j: (0, j)),
        out_shape=jax.ShapeDtypeStruct((rows, 6 * D_MODEL), F32),
        compiler_params=_cparams(("arbitrary",)),
        name="ada_modulation",
    )(c_all, w_ada, b_ada.reshape(1, -1))


QKV_WIDTH = ATTN_WIDTH + 2 * KV_WIDTH


def _inproj_kernel(x_ref, mod_ref, w_ref, lru_ref, qkv_ref, kvlast_ref):
    sh1 = mod_ref[0, 0:1, :]
    sc1 = mod_ref[0, 1:2, :]
    h = x_ref[0] * (1.0 + sc1) + sh1
    z = _dot(h, w_ref[...], False)
    lru_ref[0] = z[:, :2 * LRU_WIDTH]
    qkv_ref[0] = z[:, 2 * LRU_WIDTH:].astype(BF16)
    kvlast_ref[0] = z[z.shape[0] - WINDOW:, 2 * LRU_WIDTH + ATTN_WIDTH:]


def _inproj(x, modp, w_in_bf16, tm=1024):
    b, t, d = x.shape
    return pl.pallas_call(
        _inproj_kernel,
        grid=(b, t // tm),
        in_specs=[pl.BlockSpec((1, tm, d), lambda i, j: (i, j, 0)),
                  pl.BlockSpec((1, 6, d), lambda i, j: (i, 0, 0)),
                  pl.BlockSpec((d, IN_WIDTH), lambda i, j: (0, 0))],
        out_specs=[pl.BlockSpec((1, tm, 2 * LRU_WIDTH), lambda i, j: (i, j, 0)),
                   pl.BlockSpec((1, tm, QKV_WIDTH), lambda i, j: (i, j, 0)),
                   pl.BlockSpec((1, WINDOW, 2 * KV_WIDTH), lambda i, j: (i, 0, 0))],
        out_shape=[jax.ShapeDtypeStruct((b, t, 2 * LRU_WIDTH), F32),
                   jax.ShapeDtypeStruct((b, t, QKV_WIDTH), BF16),
                   jax.ShapeDtypeStruct((b, WINDOW, 2 * KV_WIDTH), F32)],
        compiler_params=_cparams(("arbitrary", "arbitrary")),
        name="prompt_inproj",
    )(x, modp, w_in_bf16)


def _lru_gates(xc, wlo, whi, bgate, sp_neg_lam, exact):
    g_lo = _dot(xc[:, :256], wlo, exact)
    g_hi = _dot(xc[:, 256:], whi, exact)
    ga = jnp.concatenate([g_lo[:, :256], g_hi[:, :256]], axis=-1) + bgate[:, :LRU_WIDTH]
    gx = jnp.concatenate([g_lo[:, 256:], g_hi[:, 256:]], axis=-1) + bgate[:, LRU_WIDTH:]
    r = _sigmoid(ga)
    i = _sigmoid(gx)
    log_a = -LRU_C * r * sp_neg_lam
    a = jnp.exp(log_a)
    one_minus_a2 = -jnp.tanh(log_a) * (a * a + 1.0) if exact else 1.0 - a * a
    root = jnp.where(one_minus_a2 > 0.0, one_minus_a2 * lax.rsqrt(one_minus_a2), 0.0)
    bterm = root * (i * xc)
    return a, bterm


def _lru_kernel(z_ref, convw_ref, convb_ref, wlo_ref, whi_ref, bgate_ref, lam_ref, wg_ref, wu_ref, wd_ref,
                y_ref, cstate_ref, hlast_ref, wgub_ref, wdb_ref, tail_ref, carry_ref, *, tl):
    j = pl.program_id(1)

    @pl.when(j == 0)
    def _():
        tail_ref[...] = jnp.zeros_like(tail_ref)
        carry_ref[...] = jnp.zeros_like(carry_ref)

    wgub_ref[0, :, :D_EXPERT] = wg_ref[0].astype(BF16)
    wgub_ref[0, :, D_EXPERT:] = wu_ref[0].astype(BF16)
    wdb_ref[...] = wd_ref[...].astype(BF16)

    xb = z_ref[0, :, :LRU_WIDTH]
    gate = z_ref[0, :, LRU_WIDTH:]
    xc = convb_ref[...] + convw_ref[3:4, :] * xb
    rows8 = lax.broadcasted_iota(jnp.int32, (8, LRU_WIDTH), 0)
    tail = tail_ref[...]
    for back in (1, 2, 3):
        rolled = pltpu.roll(xb, back, axis=0)
        top = jnp.where(rows8 >= back, rolled[:8], pltpu.roll(tail, back, axis=0))
        shifted = jnp.concatenate([top, rolled[8:]], axis=0)
        xc = xc + convw_ref[3 - back:4 - back, :] * shifted
    tail_ref[...] = xb[tl - 8:, :]
    cstate_ref[0] = xb[tl - 8:, :]

    sp = _softplus(-lam_ref[...])
    a, bterm = _lru_gates(xc, wlo_ref[...], whi_ref[...], bgate_ref[...], sp, False)

    groups = tl // 8
    a = a.reshape(groups, 8, LRU_WIDTH)
    bterm = bterm.reshape(groups, 8, LRU_WIDTH)
    r8 = lax.broadcasted_iota(jnp.int32, (groups, 8, LRU_WIDTH), 1)
    s = 1
    while s < 8:
        a_sh = jnp.where(r8 >= s, pltpu.roll(a, s, axis=1), 1.0)
        b_sh = jnp.where(r8 >= s, pltpu.roll(bterm, s, axis=1), 0.0)
        bterm = a * b_sh + bterm
        a = a * a_sh
        s *= 2
    a_tot = jnp.broadcast_to(a[:, 7:8, :], (groups, 8, LRU_WIDTH))
    b_tot = jnp.broadcast_to(bterm[:, 7:8, :], (groups, 8, LRU_WIDTH))
    h_in = jnp.broadcast_to(carry_ref[7:8, :], (8, LRU_WIDTH))
    pieces = []
    for g in range(groups):
        pieces.append(a[g] * h_in + bterm[g])
        h_in = a_tot[g] * h_in + b_tot[g]
    h = jnp.concatenate(pieces, axis=0)
    carry_ref[...] = h_in
    hlast_ref[0] = h_in
    y_ref[0] = h * _gelu_tanh(gate)


def _lru(zin, conv_w, conv_b, wlo, whi, bgate, lam, w_gate, w_up, w_down, tl=512):
    b, t, _ = zin.shape
    steps = t // tl
    assert b * steps == N_EXPERTS
    d = D_MODEL
    kern = functools.partial(_lru_kernel, tl=tl)
    full = lambda shp: pl.BlockSpec(shp, lambda i, j: tuple(0 for _ in shp))
    per_step = lambda shp: pl.BlockSpec(shp, lambda i, j: (i * steps + j, 0, 0))
    return pl.pallas_call(
        kern,
        grid=(b, steps),
        in_specs=[pl.BlockSpec((1, tl, 2 * LRU_WIDTH), lambda i, j: (i, j, 0)),
                  full((CONV_WIDTH, LRU_WIDTH)), full((1, LRU_WIDTH)),
                  full((256, 512)), full((256, 512)), full((1, 2 * LRU_WIDTH)), full((1, LRU_WIDTH)),
                  per_step((1, d, D_EXPERT)), per_step((1, d, D_EXPERT)), per_step((1, D_EXPERT, d))],
        out_specs=[pl.BlockSpec((1, tl, LRU_WIDTH), lambda i, j: (i, j, 0)),
                   pl.BlockSpec((1, 8, LRU_WIDTH), lambda i, j: (i, 0, 0)),
                   pl.BlockSpec((1, 8, LRU_WIDTH), lambda i, j: (i, 0, 0)),
                   per_step((1, d, 2 * D_EXPERT)), per_step((1, D_EXPERT, d))],
        out_shape=[jax.ShapeDtypeStruct((b, t, LRU_WIDTH), F32),
                   jax.ShapeDtypeStruct((b, 8, LRU_WIDTH), F32),
                   jax.ShapeDtypeStruct((b, 8, LRU_WIDTH), F32),
                   jax.ShapeDtypeStruct((N_EXPERTS, d, 2 * D_EXPERT), BF16),
                   jax.ShapeDtypeStruct(w_down.shape, BF16)],
        scratch_shapes=[pltpu.VMEM((8, LRU_WIDTH), F32), pltpu.VMEM((8, LRU_WIDTH), F32)],
        compiler_params=_cparams(("arbitrary", "arbitrary")),
        name="prompt_rglru",
    )(zin, conv_w, conv_b, wlo, whi, bgate, lam, w_gate, w_up, w_down)


ATTN_BLOCKS = 16


def _attn_kernel(q_ref, k_ref, v_ref, sink_ref, o_ref, kprev_ref, vprev_ref):
    j = pl.program_id(1)

    @pl.when(j == 0)
    def _():
        kprev_ref[...] = jnp.zeros_like(kprev_ref)
        vprev_ref[...] = jnp.zeros_like(vprev_ref)

    blk = WINDOW
    lane = lax.broadcasted_iota(jnp.int32, (blk, LANES), 1)
    low = lane < HEAD_DIM
    qi = lax.broadcasted_iota(jnp.int32, (blk, 2 * blk), 0)
    sj = lax.broadcasted_iota(jnp.int32, (blk, 2 * blk), 1)
    rel = blk + qi - sj
    in_window = (rel >= 0) & (rel <= WINDOW)
    sink = sink_ref[...].reshape(N_HEADS, blk, 1)
    k_ext = jnp.concatenate([kprev_ref[...], k_ref[0]], axis=0)
    v_ext = jnp.concatenate([vprev_ref[...], v_ref[0]], axis=0)
    v_ext = jnp.concatenate([v_ext, jnp.ones_like(v_ext)], axis=-1)
    for n in range(ATTN_BLOCKS):
        q = q_ref[0, blk * n:blk * (n + 1), :]
        pieces = []
        for half in (0, 1):
            for c in range(4):
                qc = q[:, LANES * c:LANES * (c + 1)]
                pieces.append(jnp.where(low if half == 0 else ~low, qc, 0.0).astype(BF16))
        k_band = k_ext[blk * n:blk * (n + 2)]
        v_band = v_ext[blk * n:blk * (n + 2)]
        valid = in_window & ((sj >= blk) | (j > 0)) if n == 0 else in_window
        o8 = []
        for h in range(N_HEADS):
            s = _dot_nt(pieces[h], k_band, False) * ATTN_SCALE
            s = jnp.where(valid, s, -jnp.inf)
            m = jnp.maximum(jnp.max(s, axis=-1, keepdims=True), sink[h])
            ov = _dot(jnp.exp(s - m), v_band, False)
            den = ov[:, KV_WIDTH:] + jnp.exp(sink[h] - m)
            o8.append(ov[:, :KV_WIDTH] * (1.0 / den))
        cols = []
        for c in range(4):
            cols.append(jnp.where(low, o8[c], o8[c + 4]))
        o_ref[0, blk * n:blk * (n + 1), :] = jnp.concatenate(cols, axis=-1)
    kprev_ref[...] = k_ref[0, blk * (ATTN_BLOCKS - 1):, :]
    vprev_ref[...] = v_ref[0, blk * (ATTN_BLOCKS - 1):, :]


def _attn(qkv, sinks):
    b, t, _ = qkv.shape
    blk = WINDOW
    tq = blk * ATTN_BLOCKS
    sink_col = jnp.repeat(sinks.astype(F32), blk).reshape(N_HEADS * blk, 1)
    kcol = ATTN_WIDTH // KV_WIDTH
    return pl.pallas_call(
        _attn_kernel,
        grid=(b, t // tq),
        in_specs=[pl.BlockSpec((1, tq, ATTN_WIDTH), lambda i, j: (i, j, 0)),
                  pl.BlockSpec((1, tq, KV_WIDTH), lambda i, j: (i, j, kcol)),
                  pl.BlockSpec((1, tq, KV_WIDTH), lambda i, j: (i, j, kcol + 1)),
                  pl.BlockSpec((N_HEADS * blk, 1), lambda i, j: (0, 0))],
        out_specs=pl.BlockSpec((1, tq, ATTN_WIDTH), lambda i, j: (i, j, 0)),
        out_shape=jax.ShapeDtypeStruct((b, t, ATTN_WIDTH), F32),
        scratch_shapes=[pltpu.VMEM((blk, KV_WIDTH), BF16), pltpu.VMEM((blk, KV_WIDTH), BF16)],
        compiler_params=_cparams(("arbitrary", "arbitrary")),
        name="prompt_window_attention",
    )(qkv, qkv, qkv, sink_col)


def _route(h2, wr, br, exact):
    t = h2.shape[0]
    logits = _dot(h2, wr, exact) + br
    lane = lax.broadcasted_iota(jnp.int32, (t, ROUTE_LANES), 1).astype(F32)
    neg = -jnp.inf
    big = float(ROUTE_LANES)
    is_g = (lane >= N_EXPERTS) & (lane < N_EXPERTS + N_GROUPS)
    lg = jnp.where(is_g, logits, neg)
    mg = jnp.max(lg, axis=-1, keepdims=True)
    g_val = 1.0 / jnp.sum(jnp.where(is_g, jnp.exp(lg - mg), 0.0), axis=-1, keepdims=True)
    g_lane = jnp.min(jnp.where((lg == mg) & is_g, lane, big), axis=-1, keepdims=True)
    g_idx = g_lane - N_EXPERTS
    in_grp = (lane >= g_idx * EXPERTS_PER_GROUP) & (lane < (g_idx + 1.0) * EXPERTS_PER_GROUP)
    le = jnp.where(in_grp, logits, neg)
    me = jnp.max(le, axis=-1, keepdims=True)
    se = jnp.sum(jnp.where(in_grp, jnp.exp(le - me), 0.0), axis=-1, keepdims=True)
    l1 = jnp.min(jnp.where((le == me) & in_grp, lane, big), axis=-1, keepdims=True)
    rest = in_grp & (lane != l1)
    le2 = jnp.where(rest, le, neg)
    me2 = jnp.max(le2, axis=-1, keepdims=True)
    l2 = jnp.min(jnp.where((le2 == me2) & rest, lane, big), axis=-1, keepdims=True)
    v1 = 1.0 / se
    v2 = jnp.exp(me2 - me) / se
    tot = v1 + v2
    w1 = g_val * v1 / tot
    w2 = g_val * v2 / tot
    comb = jnp.where(lane == l1, w1, 0.0) + jnp.where(lane == l2, w2, 0.0)
    return (comb + jnp.where(lane == ROUTE_INFO, l1, 0.0) + jnp.where(lane == ROUTE_INFO + 1, l2, 0.0)
            + jnp.where(lane == ROUTE_INFO + 2, w1, 0.0) + jnp.where(lane == ROUTE_INFO + 3, w2, 0.0))


def _outproj_body(x, ylru, yatt, sh2, sc2, gt1, glru, gattn, wout, ln1g, ln1b, wr, br, exact):
    mixin = jnp.concatenate([_rms_norm(ylru, glru), _rms_norm(yatt, gattn)], axis=-1)
    mix = _dot(mixin, wout, exact)
    x1 = _layer_norm(DEEPNORM_ALPHA * x + (1.0 + gt1) * mix, ln1g, ln1b)
    h2 = x1 * (1.0 + sc2) + sh2
    return x1, _route(h2, wr, br, exact)


def _outproj_prompt_kernel(x_ref, ylru_ref, yatt_ref, mod_ref, glru_ref, gattn_ref, wout_ref,
                           ln1g_ref, ln1b_ref, wr_ref, br_ref, x1_ref, info_ref, cnt_ref, tri_ref, carry_ref,
                           *, tm, per_seq):
    i = pl.program_id(0)

    @pl.when(i == 0)
    def _():
        r = lax.broadcasted_iota(jnp.int32, (tm, tm), 0)
        c = lax.broadcasted_iota(jnp.int32, (tm, tm), 1)
        tri_ref[...] = jnp.where(c < r, 1.0, 0.0).astype(BF16)

    @pl.when(i % per_seq == 0)
    def _():
        carry_ref[...] = jnp.zeros_like(carry_ref)

    gt1 = mod_ref[0, 2:3, :]
    sh2 = mod_ref[0, 3:4, :]
    sc2 = mod_ref[0, 4:5, :]
    combs = []
    nsplit = 2
    for h in range(nsplit):
        rows = slice(h * (tm // nsplit), (h + 1) * (tm // nsplit))
        x1_h, comb_h = _outproj_body(x_ref[rows, :], ylru_ref[rows, :], yatt_ref[rows, :], sh2, sc2, gt1,
                                     glru_ref[...], gattn_ref[...], wout_ref[...], ln1g_ref[...], ln1b_ref[...],
                                     wr_ref[...], br_ref[...], False)
        x1_ref[rows, :] = x1_h
        combs.append(comb_h)
    comb = jnp.concatenate(combs, axis=0)
    lane = lax.broadcasted_iota(jnp.int32, (tm, ROUTE_LANES), 1).astype(F32)
    l1 = jnp.sum(jnp.where(lane == ROUTE_INFO, comb, 0.0), axis=-1, keepdims=True)
    l2 = jnp.sum(jnp.where(lane == ROUTE_INFO + 1, comb, 0.0), axis=-1, keepdims=True)
    o1 = lane == l1
    o2 = lane == l2
    onehot = jnp.where(o1 | o2, 1.0, 0.0)
    before = jnp.dot(tri_ref[...], onehot.astype(BF16), preferred_element_type=F32) + carry_ref[0:1, :]
    rank1 = jnp.sum(jnp.where(o1, before, 0.0), axis=-1, keepdims=True)
    rank2 = jnp.sum(jnp.where(o2, before, 0.0), axis=-1, keepdims=True)
    total = carry_ref[0:1, :] + jnp.sum(onehot, axis=0, keepdims=True)
    carry_ref[...] = jnp.broadcast_to(total, carry_ref.shape)
    cnt_ref[0] = jnp.broadcast_to(total, (8, ROUTE_LANES))
    info = (comb + jnp.where(lane == ROUTE_INFO + 4, rank1, 0.0) + jnp.where(lane == ROUTE_INFO + 5, rank2, 0.0))
    info_ref[0] = jnp.transpose(info)[ROUTE_INFO:ROUTE_INFO + 8, :]


OUTPROJ_TILE = 1024


def _outproj_prompt(x2d, ylru2d, yatt2d, modp, glru, gattn, wout_bf16, ln1g, ln1b, wr, br, tm=OUTPROJ_TILE):
    n, d = x2d.shape
    per_seq = SEQ // tm
    full = lambda shp: pl.BlockSpec(shp, lambda i: tuple(0 for _ in shp))
    kern = functools.partial(_outproj_prompt_kernel, tm=tm, per_seq=per_seq)
    return pl.pallas_call(
        kern,
        grid=(n // tm,),
        in_specs=[pl.BlockSpec((tm, d), lambda i: (i, 0)),
                  pl.BlockSpec((tm, LRU_WIDTH), lambda i: (i, 0)),
                  pl.BlockSpec((tm, ATTN_WIDTH), lambda i: (i, 0)),
                  pl.BlockSpec((1, 6, d), lambda i: (i // per_seq, 0, 0)),
                  full((1, LRU_WIDTH)), full((1, ATTN_WIDTH)), full((d, d)),
                  full((1, d)), full((1, d)), full((d, ROUTE_LANES)), full((1, ROUTE_LANES))],
        out_specs=[pl.BlockSpec((tm, d), lambda i: (i, 0)),
                   pl.BlockSpec((1, 8, tm), lambda i: (i, 0, 0)),
                   pl.BlockSpec((1, 8, ROUTE_LANES), lambda i: (i // per_seq, 0, 0))],
        out_shape=[jax.ShapeDtypeStruct((n, d), F32),
                   jax.ShapeDtypeStruct((n // tm, 8, tm), F32),
                   jax.ShapeDtypeStruct((n // SEQ, 8, ROUTE_LANES), F32)],
        scratch_shapes=[pltpu.VMEM((tm, tm), BF16), pltpu.VMEM((8, ROUTE_LANES), F32)],
        compiler_params=_cparams(("arbitrary",)),
        name="prompt_outproj_ln_route",
    )(x2d, ylru2d, yatt2d, modp, glru, gattn, wout_bf16, ln1g, ln1b, wr, br)


DENSE_EXPERTS_PER_STEP = 4


def _moe_kernel(x1_ref, comb_ref, sh2_ref, sc2_ref, gt2_ref, wgu_ref, wd_ref, ln2g_ref, ln2b_ref,
                o_ref, h2_ref, acc_ref):
    g = pl.program_id(1)

    @pl.when(g == 0)
    def _():
        h2_ref[...] = (x1_ref[...] * (1.0 + sc2_ref[...]) + sh2_ref[...]).astype(BF16)
        acc_ref[...] = jnp.zeros_like(acc_ref)

    h2 = h2_ref[...]
    comb = comb_ref[...]
    lane = lax.broadcasted_iota(jnp.int32, comb.shape, 1)
    part = None
    for k in range(DENSE_EXPERTS_PER_STEP):
        au = jnp.dot(h2, wgu_ref[k], preferred_element_type=F32)
        c_e = jnp.sum(jnp.where(lane == g * DENSE_EXPERTS_PER_STEP + k, comb, 0.0), axis=-1, keepdims=True)
        z = _silu(au[:, :D_EXPERT]) * au[:, D_EXPERT:] * c_e
        y = jnp.dot(z.astype(BF16), wd_ref[k], preferred_element_type=F32)
        part = y if part is None else part + y
    acc_ref[...] += part

    @pl.when(g == N_EXPERTS // DENSE_EXPERTS_PER_STEP - 1)
    def _():
        o_ref[...] = _layer_norm(DEEPNORM_ALPHA * x1_ref[...] + (1.0 + gt2_ref[...]) * acc_ref[...],
                                 ln2g_ref[...], ln2b_ref[...])


def _moe_dense(x1, comb, mod, wgu_bf16, wd_bf16, ln2g, ln2b, tm):
    n, d = x1.shape
    eg = DENSE_EXPERTS_PER_STEP
    mspec = lambda k: pl.BlockSpec((tm, d), lambda i, e: (i, k))
    full = lambda shp: pl.BlockSpec(shp, lambda i, e: tuple(0 for _ in shp))
    return pl.pallas_call(
        _moe_kernel,
        grid=(n // tm, N_EXPERTS // eg),
        in_specs=[pl.BlockSpec((tm, d), lambda i, e: (i, 0)),
                  pl.BlockSpec((tm, ROUTE_LANES), lambda i, e: (i, 0)),
                  mspec(3), mspec(4), mspec(5),
                  pl.BlockSpec((eg, d, 2 * D_EXPERT), lambda i, e: (e, 0, 0)),
                  pl.BlockSpec((eg, D_EXPERT, d), lambda i, e: (e, 0, 0)),
                  full((1, d)), full((1, d))],
        out_specs=pl.BlockSpec((tm, d), lambda i, e: (i, 0)),
        out_shape=jax.ShapeDtypeStruct((n, d), F32),
        scratch_shapes=[pltpu.VMEM((tm, d), BF16), pltpu.VMEM((tm, d), F32)],
        compiler_params=_cparams(("arbitrary", "arbitrary")),
        name="moe_dense_ln",
    )(x1, comb, mod, mod, mod, wgu_bf16, wd_bf16, ln2g, ln2b)


RB_SUB = 512
RB_NSUB = SEQ // RB_SUB
RB_CHUNK = 96
RB_NCHUNK = -(-2 * SEQ // RB_CHUNK)
RB_PITCH = RB_CHUNK + 8
RB_SPITCH = RB_SUB + 8
RB_GROUP = 4
RB_WSLOTS = 4


def _rb_kernel(cnt_ref, x1_ref, mod_ref, offs_ref, wts_ref, wgu_hbm, wd_hbm, ln2g_ref, ln2b_ref,
               o_ref, buf_ref, stage_ref, wgu_buf, wd_buf, start_ref, sem):
    b = pl.program_id(0)
    s = pl.program_id(1)

    @pl.when(s == 0)
    def _starts():
        def body(e, run):
            start_ref[e] = run
            return run + cnt_ref[b, e]
        lax.fori_loop(0, N_EXPERTS, body, jnp.int32(0))
        buf_ref[(RB_NCHUNK - 1) * 8 * RB_PITCH:(RB_NCHUNK + RB_GROUP) * 8 * RB_PITCH, :] = jnp.zeros(
            ((RB_GROUP + 1) * 8 * RB_PITCH, LANES), F32)

    @pl.when(s < RB_NSUB)
    def _dispatch():
        sh2 = mod_ref[0, 3:4, :]
        sc2 = mod_ref[0, 4:5, :]
        h2 = x1_ref[...] * (1.0 + sc2) + sh2
        for j in range(8):
            stage_ref[RB_SPITCH * j:RB_SPITCH * j + RB_SUB, :] = h2[:, LANES * j:LANES * (j + 1)]

        for t in range(RB_SUB):
            slab = stage_ref[pl.ds(t, 8, stride=RB_SPITCH), :]
            for a in range(2):
                buf_ref[pl.ds(offs_ref[0, a, t], 8, stride=RB_PITCH), :] = slab

    @pl.when(s == RB_NSUB)
    def _experts():
        def copies(e, slot):
            return (pltpu.make_async_copy(wgu_hbm.at[e], wgu_buf.at[slot], sem.at[slot, 0]),
                    pltpu.make_async_copy(wd_hbm.at[e], wd_buf.at[slot], sem.at[slot, 1]))

        def run_expert(e, slot):
            lo_row = start_ref[e]
            hi_row = lo_row + cnt_ref[b, e]

            c_lo = lax.div(lo_row, RB_CHUNK)
            c_hi = lax.div(hi_row + (RB_CHUNK - 1), RB_CHUNK)
            row = lax.broadcasted_iota(jnp.int32, (RB_CHUNK, 1), 0)

            def load(c):
                base = pl.multiple_of(c * (8 * RB_PITCH), 8)
                return [buf_ref[pl.ds(base + RB_PITCH * j, RB_CHUNK), :] for j in range(8)]

            def store(c, tiles, y):
                base = pl.multiple_of(c * (8 * RB_PITCH), 8)
                mine = (row >= lo_row - c * RB_CHUNK) & (row < hi_row - c * RB_CHUNK)
                for j in range(8):
                    buf_ref[pl.ds(base + RB_PITCH * j, RB_CHUNK), :] = jnp.where(
                        mine, y[:, LANES * j:LANES * (j + 1)], tiles[j])

            def group(first, nchunks):
                cs = [first]
                for k in range(1, nchunks):
                    cs.append(jnp.where(first + k < c_hi, first + k, RB_NCHUNK + k))
                tiles = [load(c) for c in cs]
                x = jnp.concatenate([jnp.concatenate(t, axis=-1) for t in tiles], axis=0).astype(BF16)
                au = jnp.dot(x, wgu_buf[slot], preferred_element_type=F32)
                z = (_silu(au[:, :D_EXPERT]) * au[:, D_EXPERT:]).astype(BF16)
                y = jnp.dot(z, wd_buf[slot], preferred_element_type=F32)
                for k, c in enumerate(cs):
                    store(c, tiles[k], y[RB_CHUNK * k:RB_CHUNK * (k + 1)])

            span = c_hi - c_lo

            @pl.when(span < RB_GROUP)
            def _():
                group(c_lo, RB_GROUP - 1)

            @pl.when(span >= RB_GROUP)
            def _():
                def body(i, carry):
                    group(c_lo + RB_GROUP * i, RB_GROUP)
                    return carry
                lax.fori_loop(0, lax.div(span + (RB_GROUP - 1), RB_GROUP), body, 0)

        for e in range(RB_WSLOTS - 1):
            for c in copies(e, e):
                c.start()

        def ring_body(i, carry):
            for k in range(RB_WSLOTS):
                e = RB_WSLOTS * i + k
                ahead = e + RB_WSLOTS - 1

                @pl.when(ahead < N_EXPERTS)
                def _():
                    for c in copies(ahead, (k + RB_WSLOTS - 1) % RB_WSLOTS):
                        c.start()
                for c in copies(e, k):
                    c.wait()
                run_expert(e, k)
            return carry
        lax.fori_loop(0, N_EXPERTS // RB_WSLOTS, ring_body, 0)

    @pl.when(s > RB_NSUB)
    def _combine():
        for t in range(RB_SUB):
            acc = None
            for a in range(2):
                term = wts_ref[0, a, t] * buf_ref[pl.ds(offs_ref[0, a, t], 8, stride=RB_PITCH), :]
                acc = term if acc is None else acc + term
            stage_ref[pl.ds(t, 8, stride=RB_SPITCH), :] = acc
        gt2 = mod_ref[0, 5:6, :]
        f = jnp.concatenate([stage_ref[RB_SPITCH * j:RB_SPITCH * j + RB_SUB, :] for j in range(8)], axis=-1)
        o_ref[...] = _layer_norm(DEEPNORM_ALPHA * x1_ref[...] + (1.0 + gt2) * f, ln2g_ref[...], ln2b_ref[...])


def _rb_retile(a):
    tiles, two, t = a.shape
    return a.reshape(tiles, two, t // RB_SUB, RB_SUB).transpose(0, 2, 1, 3).reshape(-1, two, RB_SUB)


def _rb_offsets(cnt, e12, rank12):
    start = jnp.cumsum(cnt, axis=-1) - cnt
    start_t = jnp.repeat(start, e12.shape[0] // cnt.shape[0], axis=0)[:, None, None, :]
    hit = e12[..., None] == jnp.arange(N_EXPERTS, dtype=jnp.int32)
    p = jnp.sum(jnp.where(hit, start_t, 0), axis=-1) + rank12
    return (p // RB_CHUNK) * (8 * RB_PITCH) + p % RB_CHUNK


def _rb_moe(x1, modp, cnt, offs, wts, wgu_bf16, wd_bf16, ln2g, ln2b):
    n, d = x1.shape
    bsz = n // SEQ
    nsteps = 2 * RB_NSUB + 1

    def sub_index(s):
        return jnp.where(s < RB_NSUB, s, jnp.where(s == RB_NSUB, RB_NSUB - 1, s - RB_NSUB - 1))

    def tile_map(b, s, cnt_r):
        return (b * RB_NSUB + sub_index(s), 0)

    def tile_map3(b, s, cnt_r):
        return (b * RB_NSUB + sub_index(s), 0, 0)

    def out_map(b, s, cnt_r):
        return (b * RB_NSUB + jnp.maximum(s - RB_NSUB - 1, 0), 0)

    const = lambda shp: pl.BlockSpec(shp, lambda b, s, cnt_r: tuple(0 for _ in shp))
    anyspec = pl.BlockSpec(memory_space=pl.ANY)
    grid_spec = pltpu.PrefetchScalarGridSpec(
        num_scalar_prefetch=1,
        grid=(bsz, nsteps),
        in_specs=[pl.BlockSpec((RB_SUB, d), tile_map),
                  pl.BlockSpec((1, 6, d), lambda b, s, cnt_r: (b, 0, 0)),
                  pl.BlockSpec((1, 2, RB_SUB), tile_map3, memory_space=pltpu.SMEM),
                  pl.BlockSpec((1, 2, RB_SUB), tile_map3, memory_space=pltpu.SMEM),
                  anyspec, anyspec,
                  const((1, d)), const((1, d))],
        out_specs=pl.BlockSpec((RB_SUB, d), out_map),
        scratch_shapes=[pltpu.VMEM(((RB_NCHUNK + RB_GROUP) * 8 * RB_PITCH, LANES), F32),
                        pltpu.VMEM((8 * RB_SPITCH, LANES), F32),
                        pltpu.VMEM((RB_WSLOTS, d, 2 * D_EXPERT), BF16),
                        pltpu.VMEM((RB_WSLOTS, D_EXPERT, d), BF16),
                        pltpu.SMEM((N_EXPERTS,), jnp.int32),
                        pltpu.SemaphoreType.DMA((RB_WSLOTS, 2))])
    return pl.pallas_call(
        _rb_kernel,
        grid_spec=grid_spec,
        out_shape=jax.ShapeDtypeStruct((n, d), F32),
        compiler_params=_cparams(("arbitrary", "arbitrary")),
        name="moe_routed_ln",
    )(cnt, x1, modp, offs, wts, wgu_bf16, wd_bf16, ln2g, ln2b)


def _sample_in_kernel(x_ref, mod_ref, win_ref, ctx_ref, h0_ref, convw_ref, convb_ref,
                      wlo_ref, whi_ref, bgate_ref, lam_ref,
                      ylru_ref, q_ref, k_ref, v_ref, cstate_ref, hnew_ref):
    sh1 = mod_ref[0:DEC_BATCH, 0:D_MODEL]
    sc1 = mod_ref[0:DEC_BATCH, D_MODEL:2 * D_MODEL]
    h = x_ref[...] * (1.0 + sc1) + sh1
    z = _dot(h, win_ref[...], True)
    xb = z[:, :LRU_WIDTH]
    gate = z[:, LRU_WIDTH:2 * LRU_WIDTH]
    c0 = ctx_ref[:, 0, :]
    c1 = ctx_ref[:, 1, :]
    c2 = ctx_ref[:, 2, :]
    xc = (convb_ref[...] + convw_ref[0:1, :] * c0 + convw_ref[1:2, :] * c1
          + convw_ref[2:3, :] * c2 + convw_ref[3:4, :] * xb)
    cstate_ref[:, 0, :] = c1
    cstate_ref[:, 1, :] = c2
    cstate_ref[:, 2, :] = xb
    sp = _softplus(-lam_ref[...])
    a, bterm = _lru_gates(xc, wlo_ref[...], whi_ref[...], bgate_ref[...], sp, True)
    hn = a * h0_ref[...] + bterm
    hnew_ref[...] = hn
    ylru_ref[...] = hn * _gelu_tanh(gate)
    low = lax.broadcasted_iota(jnp.int32, (DEC_BATCH, LANES), 1) < HEAD_DIM
    for c in range(4):
        qc = z[:, 2 * LRU_WIDTH + LANES * c:2 * LRU_WIDTH + LANES * (c + 1)]
        q_ref[pl.ds(c, DEC_BATCH, stride=N_HEADS), :] = jnp.where(low, qc, 0.0)
        q_ref[pl.ds(c + 4, DEC_BATCH, stride=N_HEADS), :] = jnp.where(low, 0.0, qc)
    k_ref[...] = z[:, 2 * LRU_WIDTH + ATTN_WIDTH:2 * LRU_WIDTH + ATTN_WIDTH + KV_WIDTH]
    v_ref[...] = z[:, 2 * LRU_WIDTH + ATTN_WIDTH + KV_WIDTH:]


def _sample_in(x, mod, w_in_p, ctx, h0, conv_w, conv_b, wlo, whi, bgate, lam):
    n = DEC_BATCH
    outs = [jax.ShapeDtypeStruct((n, LRU_WIDTH), F32),
            jax.ShapeDtypeStruct((n * N_HEADS, LANES), F32),
            jax.ShapeDtypeStruct((n, KV_WIDTH), F32),
            jax.ShapeDtypeStruct((n, KV_WIDTH), F32),
            jax.ShapeDtypeStruct((n, CONV_WIDTH - 1, LRU_WIDTH), F32),
            jax.ShapeDtypeStruct((n, LRU_WIDTH), F32)]
    return pl.pallas_call(
        _sample_in_kernel,
        out_shape=outs,
        compiler_params=pltpu.CompilerParams(vmem_limit_bytes=VMEM_LIMIT),
        name="sample_inproj_rglru",
    )(x, mod, w_in_p, ctx, h0, conv_w, conv_b, wlo, whi, bgate, lam)


def _sample_attn_kernel(q_ref, kn_ref, vn_ref, ck_ref, cv_ref, sink_ref, y_ref, nk_ref, nv_ref, *, bb):
    pos = lax.broadcasted_iota(jnp.int32, (KV_WIDTH, WINDOW), 1)
    nh = N_HEADS
    q_all = q_ref[...].reshape(bb * nh, LANES)
    kcat = jnp.concatenate([ck_ref[b] for b in range(bb)], axis=1)
    vcat = jnp.concatenate([cv_ref[b] for b in range(bb)], axis=1)
    kn_rep = jnp.broadcast_to(kn_ref[...][:, None, :], (bb, nh, KV_WIDTH)).reshape(bb * nh, KV_WIDTH)
    vn_rep = jnp.broadcast_to(vn_ref[...][:, None, :], (bb, nh, KV_WIDTH)).reshape(bb * nh, KV_WIDTH)
    sink = jnp.concatenate([sink_ref[...]] * bb, axis=0)
    s_full = _dot(q_all, kcat, True)
    s = jnp.concatenate([s_full[nh * b:nh * (b + 1), WINDOW * b:WINDOW * (b + 1)] for b in range(bb)],
                        axis=0) * ATTN_SCALE
    s_self = jnp.sum(q_all * kn_rep, axis=-1, keepdims=True) * ATTN_SCALE
    m = jnp.maximum(jnp.maximum(jnp.max(s, axis=-1, keepdims=True), s_self), sink)
    e = jnp.exp(s - m)
    e_self = jnp.exp(s_self - m)
    den = jnp.sum(e, axis=-1, keepdims=True) + e_self + jnp.exp(sink - m)
    inv = 1.0 / den
    p = e * inv
    zero = jnp.zeros((nh, WINDOW), F32)
    p_wide = jnp.concatenate(
        [jnp.concatenate([p[nh * b:nh * (b + 1)] if c == b else zero for c in range(bb)], axis=-1)
         for b in range(bb)], axis=0)
    o = _dot_nt(p_wide, vcat, True) + (e_self * inv) * vn_rep
    y_ref[...] = o.reshape(bb, nh, LANES)
    pad = jnp.zeros((LANES - bb, KV_WIDTH), F32)
    kn_t = jnp.transpose(jnp.concatenate([kn_ref[...], pad], axis=0))
    vn_t = jnp.transpose(jnp.concatenate([vn_ref[...], pad], axis=0))
    for b in range(bb):
        nk_ref[b] = jnp.where(pos == WINDOW - 1, kn_t[:, b:b + 1], pltpu.roll(ck_ref[b], WINDOW - 1, axis=1))
        nv_ref[b] = jnp.where(pos == WINDOW - 1, vn_t[:, b:b + 1], pltpu.roll(cv_ref[b], WINDOW - 1, axis=1))


def _sample_attn(q3, kn, vn, cache_k, cache_v, sinks, bb=16):
    n = DEC_BATCH
    kern = functools.partial(_sample_attn_kernel, bb=bb)
    return pl.pallas_call(
        kern,
        grid=(n // bb,),
        in_specs=[pl.BlockSpec((bb, N_HEADS, LANES), lambda i: (i, 0, 0)),
                  pl.BlockSpec((bb, KV_WIDTH), lambda i: (i, 0)),
                  pl.BlockSpec((bb, KV_WIDTH), lambda i: (i, 0)),
                  pl.BlockSpec((bb, WINDOW, KV_WIDTH), lambda i: (i, 0, 0)),
                  pl.BlockSpec((bb, WINDOW, KV_WIDTH), lambda i: (i, 0, 0)),
                  pl.BlockSpec((N_HEADS, 1), lambda i: (0, 0))],
        out_specs=[pl.BlockSpec((bb, N_HEADS, LANES), lambda i: (i, 0, 0)),
                   pl.BlockSpec((bb, WINDOW, KV_WIDTH), lambda i: (i, 0, 0)),
                   pl.BlockSpec((bb, WINDOW, KV_WIDTH), lambda i: (i, 0, 0))],
        out_shape=[jax.ShapeDtypeStruct((n, N_HEADS, LANES), F32),
                   jax.ShapeDtypeStruct((n, WINDOW, KV_WIDTH), F32),
                   jax.ShapeDtypeStruct((n, WINDOW, KV_WIDTH), F32)],
        compiler_params=_cparams(("arbitrary",)),
        name="sample_cache_attention",
    )(q3, kn, vn, cache_k, cache_v, sinks.reshape(N_HEADS, 1))


def _sample_out_kernel(x_ref, ylru_ref, yatt_ref, mod_ref, glru_ref, gattn_ref, wout_ref,
                       ln1g_ref, ln1b_ref, wr_ref, br_ref, x1_ref, comb_ref):
    low = lax.broadcasted_iota(jnp.int32, (DEC_BATCH, LANES), 1) < HEAD_DIM
    yatt = jnp.concatenate(
        [jnp.where(low, yatt_ref[pl.ds(c, DEC_BATCH, stride=N_HEADS), :],
                   yatt_ref[pl.ds(c + 4, DEC_BATCH, stride=N_HEADS), :]) for c in range(4)], axis=-1)
    gt1 = mod_ref[0:DEC_BATCH, 2 * D_MODEL:3 * D_MODEL]
    sh2 = mod_ref[0:DEC_BATCH, 3 * D_MODEL:4 * D_MODEL]
    sc2 = mod_ref[0:DEC_BATCH, 4 * D_MODEL:5 * D_MODEL]
    x1, comb = _outproj_body(x_ref[...], ylru_ref[...], yatt, sh2, sc2, gt1,
                             glru_ref[...], gattn_ref[...], wout_ref[...], ln1g_ref[...], ln1b_ref[...],
                             wr_ref[...], br_ref[...], True)
    x1_ref[...] = x1
    comb_ref[...] = comb


def _sample_out(x, ylru, yatt2d, mod, glru, gattn, wout_p, ln1g, ln1b, wr, br):
    n = DEC_BATCH
    return pl.pallas_call(
        _sample_out_kernel,
        out_shape=[jax.ShapeDtypeStruct((n, D_MODEL), F32), jax.ShapeDtypeStruct((n, ROUTE_LANES), F32)],
        compiler_params=pltpu.CompilerParams(vmem_limit_bytes=VMEM_LIMIT),
        name="sample_outproj_ln_route",
    )(x, ylru, yatt2d, mod, glru, gattn, wout_p, ln1g, ln1b, wr, br)


def _block_diag_halves(w_a, w_x):
    def bd(w4):
        eye = jnp.eye(4, dtype=w4.dtype)
        return (w4[:, :, None, :] * eye[:, None, :, None]).reshape(256, 256)
    lo = jnp.concatenate([bd(w_a[:4]), bd(w_x[:4])], axis=1)
    hi = jnp.concatenate([bd(w_a[4:]), bd(w_x[4:])], axis=1)
    return lo, hi


PREP_ROWS = 2 * LANES


def _prep_kernel(win_ref, wout_ref, winp_ref, winb_ref, woutp_ref, woutb_ref):
    i = pl.program_id(0)
    q0 = 2 * LRU_WIDTH
    z = win_ref[...]
    low = lax.broadcasted_iota(jnp.int32, (PREP_ROWS, LANES), 1) < HEAD_DIM
    swap = lambda t: pltpu.roll(t, HEAD_DIM, axis=1)
    a0, a1, b0, b1 = (z[:, q0 + LANES * c:q0 + LANES * (c + 1)] for c in range(4))
    tiles = [jnp.where(low, a0, swap(b0)), jnp.where(low, swap(a0), b0),
             jnp.where(low, a1, swap(b1)), jnp.where(low, swap(a1), b1)]
    zp = jnp.concatenate([z[:, :q0]] + tiles + [z[:, q0 + ATTN_WIDTH:]], axis=1)
    winp_ref[...] = zp
    winb_ref[...] = zp.astype(BF16)
    for k in range(D_MODEL // PREP_ROWS):
        @pl.when(i == k)
        def _(k=k):
            r0 = k * PREP_ROWS
            if r0 < LRU_WIDTH:
                blk = wout_ref[r0:r0 + PREP_ROWS, :]
            else:
                parts = []
                for c in range((r0 - LRU_WIDTH) // LANES, (r0 - LRU_WIDTH) // LANES + PREP_ROWS // LANES):
                    for head in (c, c + 4):
                        h0 = LRU_WIDTH + HEAD_DIM * head
                        parts.append(wout_ref[h0:h0 + HEAD_DIM, :])
                blk = jnp.concatenate(parts, axis=0)
            woutp_ref[...] = blk
            woutb_ref[...] = blk.astype(BF16)


def _prep_weights(w_in0, w_out0):
    d = D_MODEL
    assert LRU_WIDTH % PREP_ROWS == 0 and N_HEADS == 8 and 2 * HEAD_DIM == LANES
    return pl.pallas_call(
        _prep_kernel,
        grid=(d // PREP_ROWS,),
        in_specs=[pl.BlockSpec((PREP_ROWS, IN_WIDTH), lambda i: (i, 0)),
                  pl.BlockSpec((d, d), lambda i: (0, 0))],
        out_specs=[pl.BlockSpec((PREP_ROWS, IN_WIDTH), lambda i: (i, 0)),
                   pl.BlockSpec((PREP_ROWS, IN_WIDTH), lambda i: (i, 0)),
                   pl.BlockSpec((PREP_ROWS, d), lambda i: (i, 0)),
                   pl.BlockSpec((PREP_ROWS, d), lambda i: (i, 0))],
        out_shape=[jax.ShapeDtypeStruct((d, IN_WIDTH), F32), jax.ShapeDtypeStruct((d, IN_WIDTH), BF16),
                   jax.ShapeDtypeStruct((d, d), F32), jax.ShapeDtypeStruct((d, d), BF16)],
        compiler_params=_cparams(("arbitrary",)),
        name="weight_layout",
    )(w_in0, w_out0)


def kernel(x_prompt, x_sample, c_prompt, c_sample, state_conv, state_h, cache_k, cache_v, w_ada, b_ada, w_in,
           conv_w, conv_b, w_rg_a, b_rg_a, w_rg_x, b_rg_x, lru_lambda, sinks, g_lru, g_attn, w_out, ln1_g, ln1_b,
           w_group, b_group, w_router, b_router, w_gate, w_up, w_down, ln2_g, ln2_b):
    d = D_MODEL
    w_in_p, w_in_b, w_out_p, w_out_b = _prep_weights(w_in[0], w_out[0])
    g_attn_p = jnp.transpose(g_attn[0].reshape(2, N_HEADS // 2, HEAD_DIM), (1, 0, 2)).reshape(1, -1)
    glru = g_lru[0].reshape(1, -1)
    wlo, whi = _block_diag_halves(w_rg_a[0], w_rg_x[0])
    bgate = jnp.concatenate([b_rg_a[0].reshape(-1), b_rg_x[0].reshape(-1)]).reshape(1, -1)
    lam = lru_lambda[0].reshape(1, -1)
    convw = conv_w[0]
    convb = conv_b[0].reshape(1, -1)
    ln1g, ln1b = ln1_g[0].reshape(1, -1), ln1_b[0].reshape(1, -1)
    ln2g, ln2b = ln2_g[0].reshape(1, -1), ln2_b[0].reshape(1, -1)
    wr = jnp.concatenate([jnp.transpose(w_router[0], (1, 0, 2)).reshape(d, N_EXPERTS), w_group[0],
                          jnp.zeros((d, ROUTE_LANES - N_EXPERTS - N_GROUPS), F32)], axis=1)
    br = jnp.concatenate([b_router[0].reshape(-1), b_group[0],
                          jnp.zeros((ROUTE_LANES - N_EXPERTS - N_GROUPS,), F32)]).reshape(1, -1)
    sink_p = sinks[0]

    c_all = jnp.concatenate([c_sample, c_prompt, jnp.zeros((8 - BATCH, d), F32)], axis=0)
    mod = _ada(c_all, w_ada[0], b_ada[0])
    modp = mod[DEC_BATCH:DEC_BATCH + BATCH].reshape(BATCH, 6, d)

    zlru, zqkv, kvlast = _inproj(x_prompt, modp, w_in_b)
    ylru, cstate8, hlast8, wgu_b, wd_b = _lru(zlru, convw, convb, wlo.astype(BF16), whi.astype(BF16), bgate, lam,
                                                   w_gate[0], w_up[0], w_down[0])
    yatt = _attn(zqkv, sink_p)
    n_p = BATCH * SEQ
    x1_p, info, cntf = _outproj_prompt(x_prompt.reshape(n_p, d), ylru.reshape(n_p, LRU_WIDTH),
                                       yatt.reshape(n_p, ATTN_WIDTH), modp, glru, g_attn_p, w_out_b,
                                       ln1g, ln1b, wr, br, tm=OUTPROJ_TILE)
    cnt = cntf[:, 0, :N_EXPERTS].astype(jnp.int32)
    offs = _rb_retile(_rb_offsets(cnt, info[:, 0:2].astype(jnp.int32), info[:, 4:6].astype(jnp.int32)))
    y_p = _rb_moe(x1_p, modp, cnt, offs, _rb_retile(info[:, 2:4]), wgu_b, wd_b, ln2g, ln2b)

    ylru_s, q2d, kn, vn, cstate_s, hnew_s = _sample_in(
        x_sample.reshape(DEC_BATCH, d), mod, w_in_p, state_conv[0], state_h[0],
        convw, convb, wlo, whi, bgate, lam)
    to_minor_pos = lambda c: jnp.transpose(c[0], (0, 2, 3, 1)).reshape(DEC_BATCH, KV_WIDTH, WINDOW)
    from_minor_pos = lambda c: jnp.transpose(c.reshape(DEC_BATCH, N_KV_HEADS, HEAD_DIM, WINDOW), (0, 3, 1, 2))[None]
    yatt3, newk, newv = _sample_attn(q2d.reshape(DEC_BATCH, N_HEADS, LANES), kn, vn,
                                     to_minor_pos(cache_k), to_minor_pos(cache_v), sink_p)
    x1_s, comb_s = _sample_out(x_sample.reshape(DEC_BATCH, d), ylru_s, yatt3.reshape(DEC_BATCH * N_HEADS, LANES),
                               mod, glru, g_attn_p, w_out_p, ln1g, ln1b, wr, br)
    y_s = _moe_dense(x1_s, comb_s, mod, wgu_b, wd_b, ln2g, ln2b, DEC_BATCH)

    return (y_p.reshape(BATCH, SEQ, d),
            y_s.reshape(DEC_BATCH, 1, d),
            cstate8[:, 5:8][None],
            hlast8[:, 7][None],
            kvlast[:, :, :KV_WIDTH].reshape(1, BATCH, WINDOW, N_KV_HEADS, HEAD_DIM),
            kvlast[:, :, KV_WIDTH:].reshape(1, BATCH, WINDOW, N_KV_HEADS, HEAD_DIM),
            cstate_s[None],
            hnew_s[None],
            from_minor_pos(newk),
            from_minor_pos(newv))
```

```python
import functools

import jax
import jax.numpy as jnp
import numpy as np
from jax import lax
from jax.experimental import pallas as pl
from jax.experimental.pallas import tpu as pltpu

F32 = jnp.float32
BF16 = jnp.bfloat16
HIGHEST = lax.Precision.HIGHEST

D_MODEL = 1024
BATCH = 4
SEQ = 4096
DEC_BATCH = 128
LRU_WIDTH = 512
LRU_BLOCKS = 8
LRU_BLOCK = 64
CONV_WIDTH = 4
LRU_C = 8.0
N_HEADS = 8
N_KV_HEADS = 2
HEAD_DIM = 64
ATTN_WIDTH = 512
KV_WIDTH = 128
WINDOW = 128
IN_WIDTH = 2 * LRU_WIDTH + ATTN_WIDTH + 2 * KV_WIDTH
N_GROUPS = 4
EXPERTS_PER_GROUP = 8
N_EXPERTS = 32
D_EXPERT = 256
DEEPNORM_ALPHA = 2.0 ** 0.25
LN_EPS = 1e-5
RMS_EPS = 1e-6
ATTN_SCALE = HEAD_DIM ** -0.5

LANES = 128
ROUTE_LANES = 128
ROUTE_INFO = 40
VMEM_LIMIT = 56 * 1024 * 1024

HEAD_PERM = np.concatenate(
    [np.concatenate([np.arange(64 * c, 64 * c + 64), np.arange(64 * (c + 4), 64 * (c + 4) + 64)])
     for c in range(4)])


def _cparams(sem):
    return pltpu.CompilerParams(dimension_semantics=sem, vmem_limit_bytes=VMEM_LIMIT)


def _dot(a, b, exact):
    if exact:
        return jnp.dot(a, b, precision=HIGHEST, preferred_element_type=F32)
    return jnp.dot(a.astype(BF16), b.astype(BF16), preferred_element_type=F32)


def _dot_nt(a, b, exact):
    dn = (((1,), (1,)), ((), ()))
    if exact:
        return lax.dot_general(a, b, dn, precision=HIGHEST, preferred_element_type=F32)
    return lax.dot_general(a.astype(BF16), b.astype(BF16), dn, preferred_element_type=F32)


def _sigmoid(x):
    return 1.0 / (1.0 + jnp.exp(-x))


def _silu(x):
    return x * _sigmoid(x)


def _gelu_tanh(x):
    return 0.5 * x * (1.0 + jnp.tanh(np.sqrt(2.0 / np.pi).astype(np.float32) * (x + 0.044715 * (x * x * x))))


def _softplus(x):
    return jnp.maximum(x, 0.0) + jnp.log1p(jnp.exp(-jnp.abs(x)))


def _layer_norm(x, g, b):
    mu = jnp.mean(x, axis=-1, keepdims=True)
    xc = x - mu
    var = jnp.mean(xc * xc, axis=-1, keepdims=True)
    return xc * lax.rsqrt(var + LN_EPS) * g + b


def _rms_norm(x, g):
    return x * lax.rsqrt(jnp.mean(x * x, axis=-1, keepdims=True) + RMS_EPS) * g


def _ada_kernel(c_ref, w_ref, b_ref, o_ref):
    o_ref[...] = _dot(_silu(c_ref[...]), w_ref[...], True) + b_ref[...]


def _ada(c_all, w_ada, b_ada):
    rows = c_all.shape[0]
    bn = 1024
    return pl.pallas_call(
        _ada_kernel,
        grid=(6 * D_MODEL // bn,),
        in_specs=[pl.BlockSpec((rows, D_MODEL), lambda j: (0, 0)),
                  pl.BlockSpec((D_MODEL, bn), lambda j: (0, j)),
                  pl.BlockSpec((1, bn), lambda j: (0, j))],
        out_specs=pl.BlockSpec((rows, bn), lambda j: (0, j)),
        out_shape=jax.ShapeDtypeStruct((rows, 6 * D_MODEL), F32),
        compiler_params=_cparams(("arbitrary",)),
        name="ada_modulation",
    )(c_all, w_ada, b_ada.reshape(1, -1))


QKV_WIDTH = ATTN_WIDTH + 2 * KV_WIDTH


def _inproj_kernel(x_ref, mod_ref, w_ref, lru_ref, qkv_ref, kvlast_ref):
    sh1 = mod_ref[0, 0:1, :]
    sc1 = mod_ref[0, 1:2, :]
    h = x_ref[0] * (1.0 + sc1) + sh1
    z = _dot(h, w_ref[...], False)
    lru_ref[0] = z[:, :2 * LRU_WIDTH]
    qkv_ref[0] = z[:, 2 * LRU_WIDTH:].astype(BF16)
    kvlast_ref[0] = jnp.transpose(z[z.shape[0] - WINDOW:, 2 * LRU_WIDTH + ATTN_WIDTH:])


def _inproj(x, modp, w_in_bf16, tm=1024):
    b, t, d = x.shape
    return pl.pallas_call(
        _inproj_kernel,
        grid=(b, t // tm),
        in_specs=[pl.BlockSpec((1, tm, d), lambda i, j: (i, j, 0)),
                  pl.BlockSpec((1, 6, d), lambda i, j: (i, 0, 0)),
                  pl.BlockSpec((d, IN_WIDTH), lambda i, j: (0, 0))],
        out_specs=[pl.BlockSpec((1, tm, 2 * LRU_WIDTH), lambda i, j: (i, j, 0)),
                   pl.BlockSpec((1, tm, QKV_WIDTH), lambda i, j: (i, j, 0)),
                   pl.BlockSpec((1, 2 * KV_WIDTH, WINDOW), lambda i, j: (i, 0, 0))],
        out_shape=[jax.ShapeDtypeStruct((b, t, 2 * LRU_WIDTH), F32),
                   jax.ShapeDtypeStruct((b, t, QKV_WIDTH), BF16),
                   jax.ShapeDtypeStruct((b, 2 * KV_WIDTH, WINDOW), F32)],
        compiler_params=_cparams(("arbitrary", "arbitrary")),
        name="prompt_inproj",
    )(x, modp, w_in_bf16)


def _lru_gates(xc, wlo, whi, bgate, sp_neg_lam, exact):
    g_lo = _dot(xc[:, :256], wlo, exact)
    g_hi = _dot(xc[:, 256:], whi, exact)
    ga = jnp.concatenate([g_lo[:, :256], g_hi[:, :256]], axis=-1) + bgate[:, :LRU_WIDTH]
    gx = jnp.concatenate([g_lo[:, 256:], g_hi[:, 256:]], axis=-1) + bgate[:, LRU_WIDTH:]
    r = _sigmoid(ga)
    i = _sigmoid(gx)
    log_a = -LRU_C * r * sp_neg_lam
    a = jnp.exp(log_a)
    one_minus_a2 = -jnp.tanh(log_a) * (a * a + 1.0) if exact else 1.0 - a * a
    root = jnp.where(one_minus_a2 > 0.0, one_minus_a2 * lax.rsqrt(one_minus_a2), 0.0)
    bterm = root * (i * xc)
    return a, bterm


def _lru_kernel(z_ref, convw_ref, convb_ref, wlo_ref, whi_ref, bgate_ref, lam_ref, wg_ref, wu_ref, wd_ref,
                y_ref, cstate_ref, hlast_ref, wgub_ref, wdb_ref, tail_ref, carry_ref, *, tl):
    j = pl.program_id(1)

    @pl.when(j == 0)
    def _():
        tail_ref[...] = jnp.zeros_like(tail_ref)
        carry_ref[...] = jnp.zeros_like(carry_ref)

    wgub_ref[0, :, :D_EXPERT] = wg_ref[0].astype(BF16)
    wgub_ref[0, :, D_EXPERT:] = wu_ref[0].astype(BF16)
    wdb_ref[...] = wd_ref[...].astype(BF16)

    xb = z_ref[0, :, :LRU_WIDTH]
    gate = z_ref[0, :, LRU_WIDTH:]
    xc = convb_ref[...] + convw_ref[3:4, :] * xb
    rows8 = lax.broadcasted_iota(jnp.int32, (8, LRU_WIDTH), 0)
    tail = tail_ref[...]
    for back in (1, 2, 3):
        rolled = pltpu.roll(xb, back, axis=0)
        top = jnp.where(rows8 >= back, rolled[:8], pltpu.roll(tail, back, axis=0))
        shifted = jnp.concatenate([top, rolled[8:]], axis=0)
        xc = xc + convw_ref[3 - back:4 - back, :] * shifted
    tail_ref[...] = xb[tl - 8:, :]
    cstate_ref[0] = xb[tl - 8:, :]

    sp = _softplus(-lam_ref[...])
    a, bterm = _lru_gates(xc, wlo_ref[...], whi_ref[...], bgate_ref[...], sp, False)

    groups = tl // 8
    a = a.reshape(groups, 8, LRU_WIDTH)
    bterm = bterm.reshape(groups, 8, LRU_WIDTH)
    r8 = lax.broadcasted_iota(jnp.int32, (groups, 8, LRU_WIDTH), 1)
    s = 1
    while s < 8:
        a_sh = jnp.where(r8 >= s, pltpu.roll(a, s, axis=1), 1.0)
        b_sh = jnp.where(r8 >= s, pltpu.roll(bterm, s, axis=1), 0.0)
        bterm = a * b_sh + bterm
        a = a * a_sh
        s *= 2
    a_tot = jnp.broadcast_to(a[:, 7:8, :], (groups, 8, LRU_WIDTH))
    b_tot = jnp.broadcast_to(bterm[:, 7:8, :], (groups, 8, LRU_WIDTH))
    h_in = jnp.broadcast_to(carry_ref[7:8, :], (8, LRU_WIDTH))
    pieces = []
    for g in range(groups):
        pieces.append(a[g] * h_in + bterm[g])
        h_in = a_tot[g] * h_in + b_tot[g]
    h = jnp.concatenate(pieces, axis=0)
    carry_ref[...] = h_in
    hlast_ref[0] = h_in
    y_ref[0] = h * _gelu_tanh(gate)


def _lru(zin, conv_w, conv_b, wlo, whi, bgate, lam, w_gate, w_up, w_down, tl=512):
    b, t, _ = zin.shape
    steps = t // tl
    assert b * steps == N_EXPERTS
    d = D_MODEL
    kern = functools.partial(_lru_kernel, tl=tl)
    full = lambda shp: pl.BlockSpec(shp, lambda i, j: tuple(0 for _ in shp))
    per_step = lambda shp: pl.BlockSpec(shp, lambda i, j: (i * steps + j, 0, 0))
    return pl.pallas_call(
        kern,
        grid=(b, steps),
        in_specs=[pl.BlockSpec((1, tl, 2 * LRU_WIDTH), lambda i, j: (i, j, 0)),
                  full((CONV_WIDTH, LRU_WIDTH)), full((1, LRU_WIDTH)),
                  full((256, 512)), full((256, 512)), full((1, 2 * LRU_WIDTH)), full((1, LRU_WIDTH)),
                  per_step((1, d, D_EXPERT)), per_step((1, d, D_EXPERT)), per_step((1, D_EXPERT, d))],
        out_specs=[pl.BlockSpec((1, tl, LRU_WIDTH), lambda i, j: (i, j, 0)),
                   pl.BlockSpec((1, 8, LRU_WIDTH), lambda i, j: (i, 0, 0)),
                   pl.BlockSpec((1, 8, LRU_WIDTH), lambda i, j: (i, 0, 0)),
                   per_step((1, d, 2 * D_EXPERT)), per_step((1, D_EXPERT, d))],
        out_shape=[jax.ShapeDtypeStruct((b, t, LRU_WIDTH), F32),
                   jax.ShapeDtypeStruct((b, 8, LRU_WIDTH), F32),
                   jax.ShapeDtypeStruct((b, 8, LRU_WIDTH), F32),
                   jax.ShapeDtypeStruct((N_EXPERTS, d, 2 * D_EXPERT), BF16),
                   jax.ShapeDtypeStruct(w_down.shape, BF16)],
        scratch_shapes=[pltpu.VMEM((8, LRU_WIDTH), F32), pltpu.VMEM((8, LRU_WIDTH), F32)],
        compiler_params=_cparams(("arbitrary", "arbitrary")),
        name="prompt_rglru",
    )(zin, conv_w, conv_b, wlo, whi, bgate, lam, w_gate, w_up, w_down)


ATTN_BLOCKS = 16


def _attn_kernel(q_ref, k_ref, v_ref, sink_ref, o_ref, kprev_ref, vprev_ref):
    j = pl.program_id(1)

    @pl.when(j == 0)
    def _():
        kprev_ref[...] = jnp.zeros_like(kprev_ref)
        vprev_ref[...] = jnp.zeros_like(vprev_ref)

    blk = WINDOW
    lane = lax.broadcasted_iota(jnp.int32, (blk, LANES), 1)
    low = lane < HEAD_DIM
    qi = lax.broadcasted_iota(jnp.int32, (blk, 2 * blk), 0)
    sj = lax.broadcasted_iota(jnp.int32, (blk, 2 * blk), 1)
    rel = blk + qi - sj
    in_window = (rel >= 0) & (rel <= WINDOW)
    sink = sink_ref[...].reshape(N_HEADS, blk, 1)
    k_ext = jnp.concatenate([kprev_ref[...], k_ref[0]], axis=0)
    v_ext = jnp.concatenate([vprev_ref[...], v_ref[0]], axis=0)
    v_ext = jnp.concatenate([v_ext, jnp.ones_like(v_ext)], axis=-1)
    for n in range(ATTN_BLOCKS):
        q = q_ref[0, blk * n:blk * (n + 1), :]
        pieces = []
        for half in (0, 1):
            for c in range(4):
                qc = q[:, LANES * c:LANES * (c + 1)]
                pieces.append(jnp.where(low if half == 0 else ~low, qc, 0.0).astype(BF16))
        k_band = k_ext[blk * n:blk * (n + 2)]
        v_band = v_ext[blk * n:blk * (n + 2)]
        valid = in_window & ((sj >= blk) | (j > 0)) if n == 0 else in_window
        o8 = []
        for h in range(N_HEADS):
            s = _dot_nt(pieces[h], k_band, False) * ATTN_SCALE
            s = jnp.where(valid, s, -jnp.inf)
            m = jnp.maximum(jnp.max(s, axis=-1, keepdims=True), sink[h])
            ov = _dot(jnp.exp(s - m), v_band, False)
            den = ov[:, KV_WIDTH:] + jnp.exp(sink[h] - m)
            o8.append(ov[:, :KV_WIDTH] * (1.0 / den))
        cols = []
        for c in range(4):
            cols.append(jnp.where(low, o8[c], o8[c + 4]))
        o_ref[0, blk * n:blk * (n + 1), :] = jnp.concatenate(cols, axis=-1)
    kprev_ref[...] = k_ref[0, blk * (ATTN_BLOCKS - 1):, :]
    vprev_ref[...] = v_ref[0, blk * (ATTN_BLOCKS - 1):, :]


def _attn(qkv, sinks):
    b, t, _ = qkv.shape
    blk = WINDOW
    tq = blk * ATTN_BLOCKS
    sink_col = jnp.repeat(sinks.astype(F32), blk).reshape(N_HEADS * blk, 1)
    kcol = ATTN_WIDTH // KV_WIDTH
    return pl.pallas_call(
        _attn_kernel,
        grid=(b, t // tq),
        in_specs=[pl.BlockSpec((1, tq, ATTN_WIDTH), lambda i, j: (i, j, 0)),
                  pl.BlockSpec((1, tq, KV_WIDTH), lambda i, j: (i, j, kcol)),
                  pl.BlockSpec((1, tq, KV_WIDTH), lambda i, j: (i, j, kcol + 1)),
                  pl.BlockSpec((N_HEADS * blk, 1), lambda i, j: (0, 0))],
        out_specs=pl.BlockSpec((1, tq, ATTN_WIDTH), lambda i, j: (i, j, 0)),
        out_shape=jax.ShapeDtypeStruct((b, t, ATTN_WIDTH), F32),
        scratch_shapes=[pltpu.VMEM((blk, KV_WIDTH), BF16), pltpu.VMEM((blk, KV_WIDTH), BF16)],
        compiler_params=_cparams(("arbitrary", "arbitrary")),
        name="prompt_window_attention",
    )(qkv, qkv, qkv, sink_col)


def _route(h2, wr, br, exact):
    t = h2.shape[0]
    logits = _dot(h2, wr, exact) + br
    lane = lax.broadcasted_iota(jnp.int32, (t, ROUTE_LANES), 1).astype(F32)
    neg = -jnp.inf
    big = float(ROUTE_LANES)
    is_g = (lane >= N_EXPERTS) & (lane < N_EXPERTS + N_GROUPS)
    lg = jnp.where(is_g, logits, neg)
    mg = jnp.max(lg, axis=-1, keepdims=True)
    g_val = 1.0 / jnp.sum(jnp.where(is_g, jnp.exp(lg - mg), 0.0), axis=-1, keepdims=True)
    g_lane = jnp.min(jnp.where((lg == mg) & is_g, lane, big), axis=-1, keepdims=True)
    g_idx = g_lane - N_EXPERTS
    in_grp = (lane >= g_idx * EXPERTS_PER_GROUP) & (lane < (g_idx + 1.0) * EXPERTS_PER_GROUP)
    le = jnp.where(in_grp, logits, neg)
    me = jnp.max(le, axis=-1, keepdims=True)
    se = jnp.sum(jnp.where(in_grp, jnp.exp(le - me), 0.0), axis=-1, keepdims=True)
    l1 = jnp.min(jnp.where((le == me) & in_grp, lane, big), axis=-1, keepdims=True)
    rest = in_grp & (lane != l1)
    le2 = jnp.where(rest, le, neg)
    me2 = jnp.max(le2, axis=-1, keepdims=True)
    l2 = jnp.min(jnp.where((le2 == me2) & rest, lane, big), axis=-1, keepdims=True)
    v1 = 1.0 / se
    v2 = jnp.exp(me2 - me) / se
    tot = v1 + v2
    w1 = g_val * v1 / tot
    w2 = g_val * v2 / tot
    comb = jnp.where(lane == l1, w1, 0.0) + jnp.where(lane == l2, w2, 0.0)
    return (comb + jnp.where(lane == ROUTE_INFO, l1, 0.0) + jnp.where(lane == ROUTE_INFO + 1, l2, 0.0)
            + jnp.where(lane == ROUTE_INFO + 2, w1, 0.0) + jnp.where(lane == ROUTE_INFO + 3, w2, 0.0))


def _outproj_body(x, ylru, yatt, sh2, sc2, gt1, glru, gattn, wout, ln1g, ln1b, wr, br, exact):
    mixin = jnp.concatenate([_rms_norm(ylru, glru), _rms_norm(yatt, gattn)], axis=-1)
    mix = _dot(mixin, wout, exact)
    x1 = _layer_norm(DEEPNORM_ALPHA * x + (1.0 + gt1) * mix, ln1g, ln1b)
    h2 = x1 * (1.0 + sc2) + sh2
    return x1, _route(h2, wr, br, exact)


def _outproj_prompt_kernel(x_ref, ylru_ref, yatt_ref, mod_ref, glru_ref, gattn_ref, wout_ref,
                           ln1g_ref, ln1b_ref, wr_ref, br_ref, x1_ref, info_ref, cnt_ref, tri_ref, carry_ref,
                           *, tm, per_seq):
    i = pl.program_id(0)

    @pl.when(i == 0)
    def _():
        r = lax.broadcasted_iota(jnp.int32, (tm, tm), 0)
        c = lax.broadcasted_iota(jnp.int32, (tm, tm), 1)
        tri_ref[...] = jnp.where(c < r, 1.0, 0.0).astype(BF16)

    @pl.when(i % per_seq == 0)
    def _():
        carry_ref[...] = jnp.zeros_like(carry_ref)

    gt1 = mod_ref[0, 2:3, :]
    sh2 = mod_ref[0, 3:4, :]
    sc2 = mod_ref[0, 4:5, :]
    combs = []
    nsplit = 2
    for h in range(nsplit):
        rows = slice(h * (tm // nsplit), (h + 1) * (tm // nsplit))
        x1_h, comb_h = _outproj_body(x_ref[rows, :], ylru_ref[rows, :], yatt_ref[rows, :], sh2, sc2, gt1,
                                     glru_ref[...], gattn_ref[...], wout_ref[...], ln1g_ref[...], ln1b_ref[...],
                                     wr_ref[...], br_ref[...], False)
        x1_ref[rows, :] = x1_h
        combs.append(comb_h)
    comb = jnp.concatenate(combs, axis=0)
    lane = lax.broadcasted_iota(jnp.int32, (tm, ROUTE_LANES), 1).astype(F32)
    l1 = jnp.sum(jnp.where(lane == ROUTE_INFO, comb, 0.0), axis=-1, keepdims=True)
    l2 = jnp.sum(jnp.where(lane == ROUTE_INFO + 1, comb, 0.0), axis=-1, keepdims=True)
    o1 = lane == l1
    o2 = lane == l2
    onehot = jnp.where(o1 | o2, 1.0, 0.0)
    before = jnp.dot(tri_ref[...], onehot.astype(BF16), preferred_element_type=F32) + carry_ref[0:1, :]
    rank1 = jnp.sum(jnp.where(o1, before, 0.0), axis=-1, keepdims=True)
    rank2 = jnp.sum(jnp.where(o2, before, 0.0), axis=-1, keepdims=True)
    total = carry_ref[0:1, :] + jnp.sum(onehot, axis=0, keepdims=True)
    carry_ref[...] = jnp.broadcast_to(total, carry_ref.shape)
    cnt_ref[0] = jnp.broadcast_to(total, (8, ROUTE_LANES))
    info = (comb + jnp.where(lane == ROUTE_INFO + 4, rank1, 0.0) + jnp.where(lane == ROUTE_INFO + 5, rank2, 0.0))
    info_ref[0] = jnp.transpose(info)[ROUTE_INFO:ROUTE_INFO + 8, :]


OUTPROJ_TILE = 1024


def _outproj_prompt(x2d, ylru2d, yatt2d, modp, glru, gattn, wout_bf16, ln1g, ln1b, wr, br, tm=OUTPROJ_TILE):
    n, d = x2d.shape
    per_seq = SEQ // tm
    full = lambda shp: pl.BlockSpec(shp, lambda i: tuple(0 for _ in shp))
    kern = functools.partial(_outproj_prompt_kernel, tm=tm, per_seq=per_seq)
    return pl.pallas_call(
        kern,
        grid=(n // tm,),
        in_specs=[pl.BlockSpec((tm, d), lambda i: (i, 0)),
                  pl.BlockSpec((tm, LRU_WIDTH), lambda i: (i, 0)),
                  pl.BlockSpec((tm, ATTN_WIDTH), lambda i: (i, 0)),
                  pl.BlockSpec((1, 6, d), lambda i: (i // per_seq, 0, 0)),
                  full((1, LRU_WIDTH)), full((1, ATTN_WIDTH)), full((d, d)),
                  full((1, d)), full((1, d)), full((d, ROUTE_LANES)), full((1, ROUTE_LANES))],
        out_specs=[pl.BlockSpec((tm, d), lambda i: (i, 0)),
                   pl.BlockSpec((1, 8, tm), lambda i: (i, 0, 0)),
                   pl.BlockSpec((1, 8, ROUTE_LANES), lambda i: (i // per_seq, 0, 0))],
        out_shape=[jax.ShapeDtypeStruct((n, d), F32),
                   jax.ShapeDtypeStruct((n // tm, 8, tm), F32),
                   jax.ShapeDtypeStruct((n // SEQ, 8, ROUTE_LANES), F32)],
        scratch_shapes=[pltpu.VMEM((tm, tm), BF16), pltpu.VMEM((8, ROUTE_LANES), F32)],
        compiler_params=_cparams(("arbitrary",)),
        name="prompt_outproj_ln_route",
    )(x2d, ylru2d, yatt2d, modp, glru, gattn, wout_bf16, ln1g, ln1b, wr, br)


DENSE_EXPERTS_PER_STEP = 4


def _moe_kernel(x1_ref, comb_ref, sh2_ref, sc2_ref, gt2_ref, wgu_ref, wd_ref, ln2g_ref, ln2b_ref,
                o_ref, h2_ref, acc_ref):
    g = pl.program_id(1)

    @pl.when(g == 0)
    def _():
        h2_ref[...] = (x1_ref[...] * (1.0 + sc2_ref[...]) + sh2_ref[...]).astype(BF16)
        acc_ref[...] = jnp.zeros_like(acc_ref)

    h2 = h2_ref[...]
    comb = comb_ref[...]
    lane = lax.broadcasted_iota(jnp.int32, comb.shape, 1)
    part = None
    for k in range(DENSE_EXPERTS_PER_STEP):
        au = jnp.dot(h2, wgu_ref[k], preferred_element_type=F32)
        c_e = jnp.sum(jnp.where(lane == g * DENSE_EXPERTS_PER_STEP + k, comb, 0.0), axis=-1, keepdims=True)
        z = _silu(au[:, :D_EXPERT]) * au[:, D_EXPERT:] * c_e
        y = jnp.dot(z.astype(BF16), wd_ref[k], preferred_element_type=F32)
        part = y if part is None else part + y
    acc_ref[...] += part

    @pl.when(g == N_EXPERTS // DENSE_EXPERTS_PER_STEP - 1)
    def _():
        o_ref[...] = _layer_norm(DEEPNORM_ALPHA * x1_ref[...] + (1.0 + gt2_ref[...]) * acc_ref[...],
                                 ln2g_ref[...], ln2b_ref[...])


def _moe_dense(x1, comb, mod, wgu_bf16, wd_bf16, ln2g, ln2b, tm):
    n, d = x1.shape
    eg = DENSE_EXPERTS_PER_STEP
    mspec = lambda k: pl.BlockSpec((tm, d), lambda i, e: (i, k))
    full = lambda shp: pl.BlockSpec(shp, lambda i, e: tuple(0 for _ in shp))
    return pl.pallas_call(
        _moe_kernel,
        grid=(n // tm, N_EXPERTS // eg),
        in_specs=[pl.BlockSpec((tm, d), lambda i, e: (i, 0)),
                  pl.BlockSpec((tm, ROUTE_LANES), lambda i, e: (i, 0)),
                  mspec(3), mspec(4), mspec(5),
                  pl.BlockSpec((eg, d, 2 * D_EXPERT), lambda i, e: (e, 0, 0)),
                  pl.BlockSpec((eg, D_EXPERT, d), lambda i, e: (e, 0, 0)),
                  full((1, d)), full((1, d))],
        out_specs=pl.BlockSpec((tm, d), lambda i, e: (i, 0)),
        out_shape=jax.ShapeDtypeStruct((n, d), F32),
        scratch_shapes=[pltpu.VMEM((tm, d), BF16), pltpu.VMEM((tm, d), F32)],
        compiler_params=_cparams(("arbitrary", "arbitrary")),
        name="moe_dense_ln",
    )(x1, comb, mod, mod, mod, wgu_bf16, wd_bf16, ln2g, ln2b)


RB_SUB = 512
RB_NSUB = SEQ // RB_SUB
RB_CHUNK = 96
RB_NCHUNK = -(-2 * SEQ // RB_CHUNK)
RB_PITCH = RB_CHUNK + 8
RB_SPITCH = RB_SUB + 8
RB_GROUP = 4
RB_WSLOTS = 4


def _rb_kernel(cnt_ref, x1_ref, mod_ref, offs_ref, wts_ref, wgu_hbm, wd_hbm, ln2g_ref, ln2b_ref,
               o_ref, buf_ref, stage_ref, wgu_buf, wd_buf, start_ref, sem):
    b = pl.program_id(0)
    s = pl.program_id(1)

    @pl.when(s == 0)
    def _starts():
        def body(e, run):
            start_ref[e] = run
            return run + cnt_ref[b, e]
        lax.fori_loop(0, N_EXPERTS, body, jnp.int32(0))
        buf_ref[(RB_NCHUNK - 1) * 8 * RB_PITCH:(RB_NCHUNK + RB_GROUP) * 8 * RB_PITCH, :] = jnp.zeros(
            ((RB_GROUP + 1) * 8 * RB_PITCH, LANES), F32)

    @pl.when(s < RB_NSUB)
    def _dispatch():
        sh2 = mod_ref[0, 3:4, :]
        sc2 = mod_ref[0, 4:5, :]
        h2 = x1_ref[...] * (1.0 + sc2) + sh2
        for j in range(8):
            stage_ref[RB_SPITCH * j:RB_SPITCH * j + RB_SUB, :] = h2[:, LANES * j:LANES * (j + 1)]

        for t in range(RB_SUB):
            slab = stage_ref[pl.ds(t, 8, stride=RB_SPITCH), :]
            for a in range(2):
                buf_ref[pl.ds(offs_ref[0, a, t], 8, stride=RB_PITCH), :] = slab

    @pl.when(s == RB_NSUB)
    def _experts():
        def copies(e, slot):
            return (pltpu.make_async_copy(wgu_hbm.at[e], wgu_buf.at[slot], sem.at[slot, 0]),
                    pltpu.make_async_copy(wd_hbm.at[e], wd_buf.at[slot], sem.at[slot, 1]))

        def run_expert(e, slot):
            lo_row = start_ref[e]
            hi_row = lo_row + cnt_ref[b, e]

            c_lo = lax.div(lo_row, RB_CHUNK)
            c_hi = lax.div(hi_row + (RB_CHUNK - 1), RB_CHUNK)
            row = lax.broadcasted_iota(jnp.int32, (RB_CHUNK, 1), 0)

            def load(c):
                base = pl.multiple_of(c * (8 * RB_PITCH), 8)
                return [buf_ref[pl.ds(base + RB_PITCH * j, RB_CHUNK), :] for j in range(8)]

            def store(c, tiles, y):
                base = pl.multiple_of(c * (8 * RB_PITCH), 8)
                mine = (row >= lo_row - c * RB_CHUNK) & (row < hi_row - c * RB_CHUNK)
                for j in range(8):
                    buf_ref[pl.ds(base + RB_PITCH * j, RB_CHUNK), :] = jnp.where(
                        mine, y[:, LANES * j:LANES * (j + 1)], tiles[j])

            def group(first, nchunks):
                cs = [first]
                for k in range(1, nchunks):
                    cs.append(jnp.where(first + k < c_hi, first + k, RB_NCHUNK + k))
                tiles = [load(c) for c in cs]
                x = jnp.concatenate([jnp.concatenate(t, axis=-1) for t in tiles], axis=0).astype(BF16)
                au = jnp.dot(x, wgu_buf[slot], preferred_element_type=F32)
                z = (_silu(au[:, :D_EXPERT]) * au[:, D_EXPERT:]).astype(BF16)
                y = jnp.dot(z, wd_buf[slot], preferred_element_type=F32)
                for k, c in enumerate(cs):
                    store(c, tiles[k], y[RB_CHUNK * k:RB_CHUNK * (k + 1)])

            span = c_hi - c_lo

            @pl.when(span < RB_GROUP)
            def _():
                group(c_lo, RB_GROUP - 1)

            @pl.when(span >= RB_GROUP)
            def _():
                def body(i, carry):
                    group(c_lo + RB_GROUP * i, RB_GROUP)
                    return carry
                lax.fori_loop(0, lax.div(span + (RB_GROUP - 1), RB_GROUP), body, 0)

        for e in range(RB_WSLOTS - 1):
            for c in copies(e, e):
                c.start()

        def ring_body(i, carry):
            for k in range(RB_WSLOTS):
                e = RB_WSLOTS * i + k
                ahead = e + RB_WSLOTS - 1

                @pl.when(ahead < N_EXPERTS)
                def _():
                    for c in copies(ahead, (k + RB_WSLOTS - 1) % RB_WSLOTS):
                        c.start()
                for c in copies(e, k):
                    c.wait()
                run_expert(e, k)
            return carry
        lax.fori_loop(0, N_EXPERTS // RB_WSLOTS, ring_body, 0)

    @pl.when(s > RB_NSUB)
    def _combine():
        for t in range(RB_SUB):
            acc = None
            for a in range(2):
                term = wts_ref[0, a, t] * buf_ref[pl.ds(offs_ref[0, a, t], 8, stride=RB_PITCH), :]
                acc = term if acc is None else acc + term
            stage_ref[pl.ds(t, 8, stride=RB_SPITCH), :] = acc
        gt2 = mod_ref[0, 5:6, :]
        f = jnp.concatenate([stage_ref[RB_SPITCH * j:RB_SPITCH * j + RB_SUB, :] for j in range(8)], axis=-1)
        o_ref[...] = _layer_norm(DEEPNORM_ALPHA * x1_ref[...] + (1.0 + gt2) * f, ln2g_ref[...], ln2b_ref[...])


def _rb_retile(a):
    tiles, two, t = a.shape
    return a.reshape(tiles, two, t // RB_SUB, RB_SUB).transpose(0, 2, 1, 3).reshape(-1, two, RB_SUB)


def _rb_offsets(cnt, e12, rank12):
    start = jnp.cumsum(cnt, axis=-1) - cnt
    start_t = jnp.repeat(start, e12.shape[0] // cnt.shape[0], axis=0)[:, None, None, :]
    hit = e12[..., None] == jnp.arange(N_EXPERTS, dtype=jnp.int32)
    p = jnp.sum(jnp.where(hit, start_t, 0), axis=-1) + rank12
    return (p // RB_CHUNK) * (8 * RB_PITCH) + p % RB_CHUNK


def _rb_moe(x1, modp, cnt, offs, wts, wgu_bf16, wd_bf16, ln2g, ln2b):
    n, d = x1.shape
    bsz = n // SEQ
    nsteps = 2 * RB_NSUB + 1

    def sub_index(s):
        return jnp.where(s < RB_NSUB, s, jnp.where(s == RB_NSUB, RB_NSUB - 1, s - RB_NSUB - 1))

    def tile_map(b, s, cnt_r):
        return (b * RB_NSUB + sub_index(s), 0)

    def tile_map3(b, s, cnt_r):
        return (b * RB_NSUB + sub_index(s), 0, 0)

    def out_map(b, s, cnt_r):
        return (b * RB_NSUB + jnp.maximum(s - RB_NSUB - 1, 0), 0)

    const = lambda shp: pl.BlockSpec(shp, lambda b, s, cnt_r: tuple(0 for _ in shp))
    anyspec = pl.BlockSpec(memory_space=pl.ANY)
    grid_spec = pltpu.PrefetchScalarGridSpec(
        num_scalar_prefetch=1,
        grid=(bsz, nsteps),
        in_specs=[pl.BlockSpec((RB_SUB, d), tile_map),
                  pl.BlockSpec((1, 6, d), lambda b, s, cnt_r: (b, 0, 0)),
                  pl.BlockSpec((1, 2, RB_SUB), tile_map3, memory_space=pltpu.SMEM),
                  pl.BlockSpec((1, 2, RB_SUB), tile_map3, memory_space=pltpu.SMEM),
                  anyspec, anyspec,
                  const((1, d)), const((1, d))],
        out_specs=pl.BlockSpec((RB_SUB, d), out_map),
        scratch_shapes=[pltpu.VMEM(((RB_NCHUNK + RB_GROUP) * 8 * RB_PITCH, LANES), F32),
                        pltpu.VMEM((8 * RB_SPITCH, LANES), F32),
                        pltpu.VMEM((RB_WSLOTS, d, 2 * D_EXPERT), BF16),
                        pltpu.VMEM((RB_WSLOTS, D_EXPERT, d), BF16),
                        pltpu.SMEM((N_EXPERTS,), jnp.int32),
                        pltpu.SemaphoreType.DMA((RB_WSLOTS, 2))])
    return pl.pallas_call(
        _rb_kernel,
        grid_spec=grid_spec,
        out_shape=jax.ShapeDtypeStruct((n, d), F32),
        compiler_params=_cparams(("arbitrary", "arbitrary")),
        name="moe_routed_ln",
    )(cnt, x1, modp, offs, wts, wgu_bf16, wd_bf16, ln2g, ln2b)


def _sample_in_kernel(x_ref, mod_ref, win_ref, ctx_ref, h0_ref, convw_ref, convb_ref,
                      wlo_ref, whi_ref, bgate_ref, lam_ref,
                      ylru_ref, q_ref, k_ref, v_ref, cstate_ref, hnew_ref):
    sh1 = mod_ref[0:DEC_BATCH, 0:D_MODEL]
    sc1 = mod_ref[0:DEC_BATCH, D_MODEL:2 * D_MODEL]
    h = x_ref[...] * (1.0 + sc1) + sh1
    z = _dot(h, win_ref[...], True)
    xb = z[:, :LRU_WIDTH]
    gate = z[:, LRU_WIDTH:2 * LRU_WIDTH]
    c0 = ctx_ref[:, 0, :]
    c1 = ctx_ref[:, 1, :]
    c2 = ctx_ref[:, 2, :]
    xc = (convb_ref[...] + convw_ref[0:1, :] * c0 + convw_ref[1:2, :] * c1
          + convw_ref[2:3, :] * c2 + convw_ref[3:4, :] * xb)
    cstate_ref[:, 0, :] = c1
    cstate_ref[:, 1, :] = c2
    cstate_ref[:, 2, :] = xb
    sp = _softplus(-lam_ref[...])
    a, bterm = _lru_gates(xc, wlo_ref[...], whi_ref[...], bgate_ref[...], sp, True)
    hn = a * h0_ref[...] + bterm
    hnew_ref[...] = hn
    ylru_ref[...] = hn * _gelu_tanh(gate)
    low = lax.broadcasted_iota(jnp.int32, (DEC_BATCH, LANES), 1) < HEAD_DIM
    for c in range(4):
        qc = z[:, 2 * LRU_WIDTH + LANES * c:2 * LRU_WIDTH + LANES * (c + 1)]
        q_ref[pl.ds(c, DEC_BATCH, stride=N_HEADS), :] = jnp.where(low, qc, 0.0)
        q_ref[pl.ds(c + 4, DEC_BATCH, stride=N_HEADS), :] = jnp.where(low, 0.0, qc)
    k_ref[...] = z[:, 2 * LRU_WIDTH + ATTN_WIDTH:2 * LRU_WIDTH + ATTN_WIDTH + KV_WIDTH]
    v_ref[...] = z[:, 2 * LRU_WIDTH + ATTN_WIDTH + KV_WIDTH:]


def _sample_in(x, mod, w_in_p, ctx, h0, conv_w, conv_b, wlo, whi, bgate, lam):
    n = DEC_BATCH
    outs = [jax.ShapeDtypeStruct((n, LRU_WIDTH), F32),
            jax.ShapeDtypeStruct((n * N_HEADS, LANES), F32),
            jax.ShapeDtypeStruct((n, KV_WIDTH), F32),
            jax.ShapeDtypeStruct((n, KV_WIDTH), F32),
            jax.ShapeDtypeStruct((n, CONV_WIDTH - 1, LRU_WIDTH), F32),
            jax.ShapeDtypeStruct((n, LRU_WIDTH), F32)]
    return pl.pallas_call(
        _sample_in_kernel,
        out_shape=outs,
        compiler_params=pltpu.CompilerParams(vmem_limit_bytes=VMEM_LIMIT),
        name="sample_inproj_rglru",
    )(x, mod, w_in_p, ctx, h0, conv_w, conv_b, wlo, whi, bgate, lam)


def _sample_attn_kernel(q_ref, kn_ref, vn_ref, ck_ref, cv_ref, sink_ref, y_ref, nk_ref, nv_ref, *, bb):
    pos = lax.broadcasted_iota(jnp.int32, (KV_WIDTH, WINDOW), 1)
    nh = N_HEADS
    q_all = q_ref[...].reshape(bb * nh, LANES)
    kcat = jnp.concatenate([ck_ref[b] for b in range(bb)], axis=1)
    vcat = jnp.concatenate([cv_ref[b] for b in range(bb)], axis=1)
    kn_rep = jnp.broadcast_to(kn_ref[...][:, None, :], (bb, nh, KV_WIDTH)).reshape(bb * nh, KV_WIDTH)
    vn_rep = jnp.broadcast_to(vn_ref[...][:, None, :], (bb, nh, KV_WIDTH)).reshape(bb * nh, KV_WIDTH)
    sink = jnp.concatenate([sink_ref[...]] * bb, axis=0)
    s_full = _dot(q_all, kcat, True)
    s = jnp.concatenate([s_full[nh * b:nh * (b + 1), WINDOW * b:WINDOW * (b + 1)] for b in range(bb)],
                        axis=0) * ATTN_SCALE
    s_self = jnp.sum(q_all * kn_rep, axis=-1, keepdims=True) * ATTN_SCALE
    m = jnp.maximum(jnp.maximum(jnp.max(s, axis=-1, keepdims=True), s_self), sink)
    e = jnp.exp(s - m)
    e_self = jnp.exp(s_self - m)
    den = jnp.sum(e, axis=-1, keepdims=True) + e_self + jnp.exp(sink - m)
    inv = 1.0 / den
    p = e * inv
    zero = jnp.zeros((nh, WINDOW), F32)
    p_wide = jnp.concatenate(
        [jnp.concatenate([p[nh * b:nh * (b + 1)] if c == b else zero for c in range(bb)], axis=-1)
         for b in range(bb)], axis=0)
    o = _dot_nt(p_wide, vcat, True) + (e_self * inv) * vn_rep
    y_ref[...] = o.reshape(bb, nh, LANES)
    pad = jnp.zeros((LANES - bb, KV_WIDTH), F32)
    kn_t = jnp.transpose(jnp.concatenate([kn_ref[...], pad], axis=0))
    vn_t = jnp.transpose(jnp.concatenate([vn_ref[...], pad], axis=0))
    for b in range(bb):
        nk_ref[b] = jnp.where(pos == WINDOW - 1, kn_t[:, b:b + 1], pltpu.roll(ck_ref[b], WINDOW - 1, axis=1))
        nv_ref[b] = jnp.where(pos == WINDOW - 1, vn_t[:, b:b + 1], pltpu.roll(cv_ref[b], WINDOW - 1, axis=1))


def _sample_attn(q3, kn, vn, cache_k, cache_v, sinks, bb=16):
    n = DEC_BATCH
    kern = functools.partial(_sample_attn_kernel, bb=bb)
    return pl.pallas_call(
        kern,
        grid=(n // bb,),
        in_specs=[pl.BlockSpec((bb, N_HEADS, LANES), lambda i: (i, 0, 0)),
                  pl.BlockSpec((bb, KV_WIDTH), lambda i: (i, 0)),
                  pl.BlockSpec((bb, KV_WIDTH), lambda i: (i, 0)),
                  pl.BlockSpec((bb, WINDOW, KV_WIDTH), lambda i: (i, 0, 0)),
                  pl.BlockSpec((bb, WINDOW, KV_WIDTH), lambda i: (i, 0, 0)),
                  pl.BlockSpec((N_HEADS, 1), lambda i: (0, 0))],
        out_specs=[pl.BlockSpec((bb, N_HEADS, LANES), lambda i: (i, 0, 0)),
                   pl.BlockSpec((bb, WINDOW, KV_WIDTH), lambda i: (i, 0, 0)),
                   pl.BlockSpec((bb, WINDOW, KV_WIDTH), lambda i: (i, 0, 0))],
        out_shape=[jax.ShapeDtypeStruct((n, N_HEADS, LANES), F32),
                   jax.ShapeDtypeStruct((n, WINDOW, KV_WIDTH), F32),
                   jax.ShapeDtypeStruct((n, WINDOW, KV_WIDTH), F32)],
        compiler_params=_cparams(("arbitrary",)),
        name="sample_cache_attention",
    )(q3, kn, vn, cache_k, cache_v, sinks.reshape(N_HEADS, 1))


def _sample_out_kernel(x_ref, ylru_ref, yatt_ref, mod_ref, glru_ref, gattn_ref, wout_ref,
                       ln1g_ref, ln1b_ref, wr_ref, br_ref, x1_ref, comb_ref):
    low = lax.broadcasted_iota(jnp.int32, (DEC_BATCH, LANES), 1) < HEAD_DIM
    yatt = jnp.concatenate(
        [jnp.where(low, yatt_ref[pl.ds(c, DEC_BATCH, stride=N_HEADS), :],
                   yatt_ref[pl.ds(c + 4, DEC_BATCH, stride=N_HEADS), :]) for c in range(4)], axis=-1)
    gt1 = mod_ref[0:DEC_BATCH, 2 * D_MODEL:3 * D_MODEL]
    sh2 = mod_ref[0:DEC_BATCH, 3 * D_MODEL:4 * D_MODEL]
    sc2 = mod_ref[0:DEC_BATCH, 4 * D_MODEL:5 * D_MODEL]
    x1, comb = _outproj_body(x_ref[...], ylru_ref[...], yatt, sh2, sc2, gt1,
                             glru_ref[...], gattn_ref[...], wout_ref[...], ln1g_ref[...], ln1b_ref[...],
                             wr_ref[...], br_ref[...], True)
    x1_ref[...] = x1
    comb_ref[...] = comb


def _sample_out(x, ylru, yatt2d, mod, glru, gattn, wout_p, ln1g, ln1b, wr, br):
    n = DEC_BATCH
    return pl.pallas_call(
        _sample_out_kernel,
        out_shape=[jax.ShapeDtypeStruct((n, D_MODEL), F32), jax.ShapeDtypeStruct((n, ROUTE_LANES), F32)],
        compiler_params=pltpu.CompilerParams(vmem_limit_bytes=VMEM_LIMIT),
        name="sample_outproj_ln_route",
    )(x, ylru, yatt2d, mod, glru, gattn, wout_p, ln1g, ln1b, wr, br)


def _block_diag_halves(w_a, w_x):
    def bd(w4):
        eye = jnp.eye(4, dtype=w4.dtype)
        return (w4[:, :, None, :] * eye[:, None, :, None]).reshape(256, 256)
    lo = jnp.concatenate([bd(w_a[:4]), bd(w_x[:4])], axis=1)
    hi = jnp.concatenate([bd(w_a[4:]), bd(w_x[4:])], axis=1)
    return lo, hi


PREP_ROWS = 2 * LANES


def _prep_kernel(win_ref, wout_ref, winp_ref, winb_ref, woutp_ref, woutb_ref):
    i = pl.program_id(0)
    q0 = 2 * LRU_WIDTH
    z = win_ref[...]
    low = lax.broadcasted_iota(jnp.int32, (PREP_ROWS, LANES), 1) < HEAD_DIM
    swap = lambda t: pltpu.roll(t, HEAD_DIM, axis=1)
    a0, a1, b0, b1 = (z[:, q0 + LANES * c:q0 + LANES * (c + 1)] for c in range(4))
    tiles = [jnp.where(low, a0, swap(b0)), jnp.where(low, swap(a0), b0),
             jnp.where(low, a1, swap(b1)), jnp.where(low, swap(a1), b1)]
    zp = jnp.concatenate([z[:, :q0]] + tiles + [z[:, q0 + ATTN_WIDTH:]], axis=1)
    winp_ref[...] = zp
    winb_ref[...] = zp.astype(BF16)
    for k in range(D_MODEL // PREP_ROWS):
        @pl.when(i == k)
        def _(k=k):
            r0 = k * PREP_ROWS
            if r0 < LRU_WIDTH:
                blk = wout_ref[r0:r0 + PREP_ROWS, :]
            else:
                parts = []
                for c in range((r0 - LRU_WIDTH) // LANES, (r0 - LRU_WIDTH) // LANES + PREP_ROWS // LANES):
                    for head in (c, c + 4):
                        h0 = LRU_WIDTH + HEAD_DIM * head
                        parts.append(wout_ref[h0:h0 + HEAD_DIM, :])
                blk = jnp.concatenate(parts, axis=0)
            woutp_ref[...] = blk
            woutb_ref[...] = blk.astype(BF16)


def _prep_weights(w_in0, w_out0):
    d = D_MODEL
    assert LRU_WIDTH % PREP_ROWS == 0 and N_HEADS == 8 and 2 * HEAD_DIM == LANES
    return pl.pallas_call(
        _prep_kernel,
        grid=(d // PREP_ROWS,),
        in_specs=[pl.BlockSpec((PREP_ROWS, IN_WIDTH), lambda i: (i, 0)),
                  pl.BlockSpec((d, d), lambda i: (0, 0))],
        out_specs=[pl.BlockSpec((PREP_ROWS, IN_WIDTH), lambda i: (i, 0)),
                   pl.BlockSpec((PREP_ROWS, IN_WIDTH), lambda i: (i, 0)),
                   pl.BlockSpec((PREP_ROWS, d), lambda i: (i, 0)),
                   pl.BlockSpec((PREP_ROWS, d), lambda i: (i, 0))],
        out_shape=[jax.ShapeDtypeStruct((d, IN_WIDTH), F32), jax.ShapeDtypeStruct((d, IN_WIDTH), BF16),
                   jax.ShapeDtypeStruct((d, d), F32), jax.ShapeDtypeStruct((d, d), BF16)],
        compiler_params=_cparams(("arbitrary",)),
        name="weight_layout",
    )(w_in0, w_out0)


def kernel(x_prompt, x_sample, c_prompt, c_sample, state_conv, state_h, cache_k, cache_v, w_ada, b_ada, w_in,
           conv_w, conv_b, w_rg_a, b_rg_a, w_rg_x, b_rg_x, lru_lambda, sinks, g_lru, g_attn, w_out, ln1_g, ln1_b,
           w_group, b_group, w_router, b_router, w_gate, w_up, w_down, ln2_g, ln2_b):
    d = D_MODEL
    w_in_p, w_in_b, w_out_p, w_out_b = _prep_weights(w_in[0], w_out[0])
    g_attn_p = jnp.transpose(g_attn[0].reshape(2, N_HEADS // 2, HEAD_DIM), (1, 0, 2)).reshape(1, -1)
    glru = g_lru[0].reshape(1, -1)
    wlo, whi = _block_diag_halves(w_rg_a[0], w_rg_x[0])
    bgate = jnp.concatenate([b_rg_a[0].reshape(-1), b_rg_x[0].reshape(-1)]).reshape(1, -1)
    lam = lru_lambda[0].reshape(1, -1)
    convw = conv_w[0]
    convb = conv_b[0].reshape(1, -1)
    ln1g, ln1b = ln1_g[0].reshape(1, -1), ln1_b[0].reshape(1, -1)
    ln2g, ln2b = ln2_g[0].reshape(1, -1), ln2_b[0].reshape(1, -1)
    wr = jnp.concatenate([jnp.transpose(w_router[0], (1, 0, 2)).reshape(d, N_EXPERTS), w_group[0],
                          jnp.zeros((d, ROUTE_LANES - N_EXPERTS - N_GROUPS), F32)], axis=1)
    br = jnp.concatenate([b_router[0].reshape(-1), b_group[0],
                          jnp.zeros((ROUTE_LANES - N_EXPERTS - N_GROUPS,), F32)]).reshape(1, -1)
    sink_p = sinks[0]

    c_all = jnp.concatenate([c_sample, c_prompt, jnp.zeros((8 - BATCH, d), F32)], axis=0)
    mod = _ada(c_all, w_ada[0], b_ada[0])
    modp = mod[DEC_BATCH:DEC_BATCH + BATCH].reshape(BATCH, 6, d)

    zlru, zqkv, kvlast = _inproj(x_prompt, modp, w_in_b)
    ylru, cstate8, hlast8, wgu_b, wd_b = _lru(zlru, convw, convb, wlo.astype(BF16), whi.astype(BF16), bgate, lam,
                                                   w_gate[0], w_up[0], w_down[0])
    yatt = _attn(zqkv, sink_p)
    n_p = BATCH * SEQ
    x1_p, info, cntf = _outproj_prompt(x_prompt.reshape(n_p, d), ylru.reshape(n_p, LRU_WIDTH),
                                       yatt.reshape(n_p, ATTN_WIDTH), modp, glru, g_attn_p, w_out_b,
                                       ln1g, ln1b, wr, br, tm=OUTPROJ_TILE)
    cnt = cntf[:, 0, :N_EXPERTS].astype(jnp.int32)
    offs = _rb_retile(_rb_offsets(cnt, info[:, 0:2].astype(jnp.int32), info[:, 4:6].astype(jnp.int32)))
    y_p = _rb_moe(x1_p, modp, cnt, offs, _rb_retile(info[:, 2:4]), wgu_b, wd_b, ln2g, ln2b)

    ylru_s, q2d, kn, vn, cstate_s, hnew_s = _sample_in(
        x_sample.reshape(DEC_BATCH, d), mod, w_in_p, state_conv[0], state_h[0],
        convw, convb, wlo, whi, bgate, lam)
    to_minor_pos = lambda c: jnp.transpose(c[0], (0, 2, 3, 1)).reshape(DEC_BATCH, KV_WIDTH, WINDOW)
    from_minor_pos = lambda c: jnp.transpose(c.reshape(DEC_BATCH, N_KV_HEADS, HEAD_DIM, WINDOW), (0, 3, 1, 2))[None]
    yatt3, newk, newv = _sample_attn(q2d.reshape(DEC_BATCH, N_HEADS, LANES), kn, vn,
                                     to_minor_pos(cache_k), to_minor_pos(cache_v), sink_p)
    x1_s, comb_s = _sample_out(x_sample.reshape(DEC_BATCH, d), ylru_s, yatt3.reshape(DEC_BATCH * N_HEADS, LANES),
                               mod, glru, g_attn_p, w_out_p, ln1g, ln1b, wr, br)
    y_s = _moe_dense(x1_s, comb_s, mod, wgu_b, wd_b, ln2g, ln2b, DEC_BATCH)

    return (y_p.reshape(BATCH, SEQ, d),
            y_s.reshape(DEC_BATCH, 1, d),
            cstate8[:, 5:8][None],
            hlast8[:, 7][None],
            jnp.transpose(kvlast[:, :KV_WIDTH].reshape(BATCH, N_KV_HEADS, HEAD_DIM, WINDOW), (0, 3, 1, 2))[None],
            jnp.transpose(kvlast[:, KV_WIDTH:].reshape(BATCH, N_KV_HEADS, HEAD_DIM, WINDOW), (0, 3, 1, 2))[None],
            cstate_s[None],
            hnew_s[None],
            from_minor_pos(newk),
            from_minor_pos(newv))
```
